```python
import math
import jax, jax.numpy as jnp
from jax import lax
import numpy as np

D_MODEL = 2048
BATCH = 8
SEQ = 4096
DEPTH = 1

CHUNK = 128
A_GROUP_DIM = 128
A_GROUPS = (D_MODEL // 2) // A_GROUP_DIM
A_WIDTH = A_GROUPS * A_GROUP_DIM
HEAD_DIM = 64
B_HEADS = (D_MODEL // 2) // HEAD_DIM
B_KV_HEADS = 2
Q_PER_KV = B_HEADS // B_KV_HEADS
B_WIDTH = B_HEADS * HEAD_DIM
KV_WIDTH = B_KV_HEADS * HEAD_DIM
WINDOW = 128
BLOCK = 128
N_BUCKETS = 32
MAX_DISTANCE = 128
MIX_WIDTH = A_WIDTH + B_WIDTH
PROJ_WIDTH = 2 * A_WIDTH + B_WIDTH + 2 * KV_WIDTH
SPLITS = [A_WIDTH, 2 * A_WIDTH, 2 * A_WIDTH + B_WIDTH, 2 * A_WIDTH + B_WIDTH + KV_WIDTH]
D_FF = 4 * D_MODEL
EPS = 1e-5
NEG = -1e30

kernel_name = "hymba_gmlp_swa_sink_t5_sqrelu"


def rms_norm(x, g):
    xf = x.astype(jnp.float32)
    y = xf * lax.rsqrt(jnp.mean(xf * xf, axis=-1, keepdims=True) + EPS)
    return (y * g.astype(jnp.float32)).astype(x.dtype)


def layer_norm(x, g, b):
    xf = x.astype(jnp.float32)
    mu = jnp.mean(xf, axis=-1, keepdims=True)
    xc = xf - mu
    var = jnp.mean(xc * xc, axis=-1, keepdims=True)
    y = xc * lax.rsqrt(var + EPS) * g.astype(jnp.float32) + b.astype(jnp.float32)
    return y.astype(x.dtype)


def t5_relative_bias(table):
    i = jnp.arange(BLOCK)[:, None]
    j = jnp.arange(2 * BLOCK)[None, :]
    rel = jnp.maximum(i + BLOCK - j, 0)
    n_exact = N_BUCKETS // 2
    relf = jnp.maximum(rel, n_exact).astype(jnp.float32)
    large = n_exact + (jnp.log(relf / n_exact) / math.log(MAX_DISTANCE / n_exact)
                       * (N_BUCKETS - n_exact)).astype(jnp.int32)
    large = jnp.minimum(large, N_BUCKETS - 1)
    bucket = jnp.where(rel < n_exact, rel, large)
    return jnp.transpose(table.astype(jnp.float32)[bucket], (2, 0, 1))


def spatial_gating(u, v, ln_g, ln_b, w_s, b_s):
    Bsz, S = u.shape[:2]
    u = jax.nn.gelu(u)
    v = layer_norm(jax.nn.gelu(v), ln_g, ln_b)
    v = v.reshape(Bsz, S // CHUNK, CHUNK, A_GROUPS, A_GROUP_DIM)
    causal = jnp.tril(jnp.ones((CHUNK, CHUNK), dtype=bool))
    w = jnp.where(causal[None], w_s, jnp.zeros_like(w_s))
    mixed = jnp.einsum('gts,bcsgd->bctgd', w, v) + jnp.transpose(b_s)[None, None, :, :, None]
    return u * mixed.reshape(Bsz, S, A_GROUPS, A_GROUP_DIM)


def sliding_window_attention(q, k, v, sinks, rel_bias):
    Bsz, S = q.shape[:2]
    nb = S // BLOCK
    qb = q.reshape(Bsz, nb, BLOCK, B_KV_HEADS, Q_PER_KV, HEAD_DIM)

    def band(t):
        tp = jnp.pad(t, ((0, 0), (BLOCK, 0), (0, 0), (0, 0)))
        tp = tp.reshape(Bsz, nb + 1, BLOCK, B_KV_HEADS, HEAD_DIM)
        return jnp.concatenate([tp[:, :-1], tp[:, 1:]], axis=2)

    kb, vb = band(k), band(v)
    s = jnp.einsum('bnikgd,bnjkd->bnkgij', qb, kb).astype(jnp.float32) * (HEAD_DIM ** -0.5)
    s = s + rel_bias.reshape(B_KV_HEADS, Q_PER_KV, BLOCK, 2 * BLOCK)
    i = jnp.arange(BLOCK)[:, None]
    j = jnp.arange(2 * BLOCK)[None, :]
    rel = i + BLOCK - j
    in_window = (rel >= 0) & (rel < WINDOW)
    key_exists = (jnp.arange(nb)[:, None] > 0) | (jnp.arange(2 * BLOCK)[None, :] >= BLOCK)
    mask = in_window[None] & key_exists[:, None, :]
    s = jnp.where(mask[None, :, None, None], s, NEG)
    sink = sinks.astype(jnp.float32).reshape(B_KV_HEADS, Q_PER_KV)[None, None, :, :, None, None]
    m = jnp.maximum(jnp.max(s, axis=-1, keepdims=True), sink)
    p = jnp.exp(s - m)
    denom = jnp.sum(p, axis=-1, keepdims=True) + jnp.exp(sink - m)
    o = jnp.einsum('bnkgij,bnjkd->bnikgd', (p / denom).astype(v.dtype), vb)
    return o.reshape(Bsz, S, B_WIDTH)


def _fwd_setup_inputs(seed: int = 0) -> dict:
    key = jax.random.key(seed)
    ks = jax.random.split(key, 17)
    f32 = jnp.float32
    nrm = lambda k, shape, scale: jax.random.normal(k, shape, f32) * scale
    return {
        'x': nrm(ks[0], (BATCH, SEQ, D_MODEL), 1.0),
        'rel_bias_table': nrm(ks[1], (N_BUCKETS, B_HEADS), 0.5),
        'mix_norm_g': 1.0 + nrm(ks[2], (DEPTH, D_MODEL), 0.02),
        'w_in': nrm(ks[3], (DEPTH, D_MODEL, PROJ_WIDTH), D_MODEL ** -0.5),
        'gate_norm_g': 1.0 + nrm(ks[4], (DEPTH, A_GROUPS, A_GROUP_DIM), 0.02),
        'gate_norm_b': nrm(ks[5], (DEPTH, A_GROUPS, A_GROUP_DIM), 0.02),
        'w_spatial': nrm(ks[6], (DEPTH, A_GROUPS, CHUNK, CHUNK), CHUNK ** -0.5),
        'b_spatial': 1.0 + nrm(ks[7], (DEPTH, A_GROUPS, CHUNK), 0.1),
        'attn_sinks': nrm(ks[8], (DEPTH, B_HEADS), 0.5),
        'out_norm_a_g': 1.0 + nrm(ks[9], (DEPTH, A_WIDTH), 0.02),
        'out_norm_b_g': 1.0 + nrm(ks[10], (DEPTH, B_WIDTH), 0.02),
        'w_out': nrm(ks[11], (DEPTH, MIX_WIDTH, D_MODEL), MIX_WIDTH ** -0.5),
        'ffn_norm_g': 1.0 + nrm(ks[12], (DEPTH, D_MODEL), 0.02),
        'w_up': nrm(ks[13], (DEPTH, D_MODEL, D_FF), D_MODEL ** -0.5),
        'w_down': nrm(ks[14], (DEPTH, D_FF, D_MODEL), D_FF ** -0.5),
        'final_norm_g': 1.0 + nrm(ks[15], (D_MODEL,), 0.02),
    }


def _fwd_reference(x, rel_bias_table, mix_norm_g, w_in, gate_norm_g, gate_norm_b, w_spatial,
              b_spatial, attn_sinks, out_norm_a_g, out_norm_b_g, w_out, ffn_norm_g,
              w_up, w_down, final_norm_g):
    Bsz, S, _ = x.shape
    rel_bias = t5_relative_bias(rel_bias_table)
    h = x
    for layer in range(DEPTH):
        n = rms_norm(h, mix_norm_g[layer])
        proj = n @ w_in[layer]
        u, v, q, k, va = jnp.split(proj, SPLITS, axis=-1)
        a_out = spatial_gating(
            u.reshape(Bsz, S, A_GROUPS, A_GROUP_DIM),
            v.reshape(Bsz, S, A_GROUPS, A_GROUP_DIM),
            gate_norm_g[layer], gate_norm_b[layer], w_spatial[layer], b_spatial[layer],
        ).reshape(Bsz, S, A_WIDTH)
        b_out = sliding_window_attention(
            q.reshape(Bsz, S, B_HEADS, HEAD_DIM),
            k.reshape(Bsz, S, B_KV_HEADS, HEAD_DIM),
            va.reshape(Bsz, S, B_KV_HEADS, HEAD_DIM),
            attn_sinks[layer], rel_bias,
        )
        mixed = jnp.concatenate(
            [rms_norm(a_out, out_norm_a_g[layer]), rms_norm(b_out, out_norm_b_g[layer])], axis=-1)
        h = h + mixed @ w_out[layer]
        z = jax.nn.relu(rms_norm(h, ffn_norm_g[layer]) @ w_up[layer])
        h = h + (z * z) @ w_down[layer]
    return rms_norm(h, final_norm_g)


import jax as _jax
import jax.numpy as _jnp

TWIN_FORMAT = 'train_step'
FWD_PARAMS = ['x', 'rel_bias_table', 'mix_norm_g', 'w_in', 'gate_norm_g', 'gate_norm_b', 'w_spatial', 'b_spatial', 'attn_sinks', 'out_norm_a_g', 'out_norm_b_g', 'w_out', 'ffn_norm_g', 'w_up', 'w_down', 'final_norm_g']
TWIN_WEIGHTS = ['rel_bias_table', 'mix_norm_g', 'w_in', 'gate_norm_g', 'gate_norm_b', 'w_spatial', 'b_spatial', 'attn_sinks', 'out_norm_a_g', 'out_norm_b_g', 'w_out', 'ffn_norm_g', 'w_up', 'w_down', 'final_norm_g']
TWIN_DIFF_INPUT = 'x'
TWIN_INPUTS = ['x', 'rel_bias_table', 'mix_norm_g', 'w_in', 'gate_norm_g', 'gate_norm_b', 'w_spatial', 'b_spatial', 'attn_sinks', 'out_norm_a_g', 'out_norm_b_g', 'w_out', 'ffn_norm_g', 'w_up', 'w_down', 'final_norm_g', 'loss_target', 'm_rel_bias_table', 'm_mix_norm_g', 'm_w_in', 'm_gate_norm_g', 'm_gate_norm_b', 'm_w_spatial', 'm_b_spatial', 'm_attn_sinks', 'm_out_norm_a_g', 'm_out_norm_b_g', 'm_w_out', 'm_ffn_norm_g', 'm_w_up', 'm_w_down', 'm_final_norm_g', 'v_rel_bias_table', 'v_mix_norm_g', 'v_w_in', 'v_gate_norm_g', 'v_gate_norm_b', 'v_w_spatial', 'v_b_spatial', 'v_attn_sinks', 'v_out_norm_a_g', 'v_out_norm_b_g', 'v_w_out', 'v_ffn_norm_g', 'v_w_up', 'v_w_down', 'v_final_norm_g']
TWIN_OUTPUTS = ['loss', 'grad_x', 'grad_rel_bias_table', 'grad_mix_norm_g', 'grad_w_in', 'grad_gate_norm_g', 'grad_gate_norm_b', 'grad_w_spatial', 'grad_b_spatial', 'grad_attn_sinks', 'grad_out_norm_a_g', 'grad_out_norm_b_g', 'grad_w_out', 'grad_ffn_norm_g', 'grad_w_up', 'grad_w_down', 'grad_final_norm_g', 'delta_rel_bias_table', 'delta_mix_norm_g', 'delta_w_in', 'delta_gate_norm_g', 'delta_gate_norm_b', 'delta_w_spatial', 'delta_b_spatial', 'delta_attn_sinks', 'delta_out_norm_a_g', 'delta_out_norm_b_g', 'delta_w_out', 'delta_ffn_norm_g', 'delta_w_up', 'delta_w_down', 'delta_final_norm_g', 'new_m_rel_bias_table', 'new_m_mix_norm_g', 'new_m_w_in', 'new_m_gate_norm_g', 'new_m_gate_norm_b', 'new_m_w_spatial', 'new_m_b_spatial', 'new_m_attn_sinks', 'new_m_out_norm_a_g', 'new_m_out_norm_b_g', 'new_m_w_out', 'new_m_ffn_norm_g', 'new_m_w_up', 'new_m_w_down', 'new_m_final_norm_g', 'new_v_rel_bias_table', 'new_v_mix_norm_g', 'new_v_w_in', 'new_v_gate_norm_g', 'new_v_gate_norm_b', 'new_v_w_spatial', 'new_v_b_spatial', 'new_v_attn_sinks', 'new_v_out_norm_a_g', 'new_v_out_norm_b_g', 'new_v_w_out', 'new_v_ffn_norm_g', 'new_v_w_up', 'new_v_w_down', 'new_v_final_norm_g']
TWIN_LEAF_KINDS = {'loss': 'loss', 'grad_x': 'grad_x', 'grad_rel_bias_table': 'grad_w', 'grad_mix_norm_g': 'grad_w', 'grad_w_in': 'grad_w', 'grad_gate_norm_g': 'grad_w', 'grad_gate_norm_b': 'grad_w', 'grad_w_spatial': 'grad_w', 'grad_b_spatial': 'grad_w', 'grad_attn_sinks': 'grad_w', 'grad_out_norm_a_g': 'grad_w', 'grad_out_norm_b_g': 'grad_w', 'grad_w_out': 'grad_w', 'grad_ffn_norm_g': 'grad_w', 'grad_w_up': 'grad_w', 'grad_w_down': 'grad_w', 'grad_final_norm_g': 'grad_w', 'delta_rel_bias_table': 'delta_w', 'delta_mix_norm_g': 'delta_w', 'delta_w_in': 'delta_w', 'delta_gate_norm_g': 'delta_w', 'delta_gate_norm_b': 'delta_w', 'delta_w_spatial': 'delta_w', 'delta_b_spatial': 'delta_w', 'delta_attn_sinks': 'delta_w', 'delta_out_norm_a_g': 'delta_w', 'delta_out_norm_b_g': 'delta_w', 'delta_w_out': 'delta_w', 'delta_ffn_norm_g': 'delta_w', 'delta_w_up': 'delta_w', 'delta_w_down': 'delta_w', 'delta_final_norm_g': 'delta_w', 'new_m_rel_bias_table': 'new_m', 'new_m_mix_norm_g': 'new_m', 'new_m_w_in': 'new_m', 'new_m_gate_norm_g': 'new_m', 'new_m_gate_norm_b': 'new_m', 'new_m_w_spatial': 'new_m', 'new_m_b_spatial': 'new_m', 'new_m_attn_sinks': 'new_m', 'new_m_out_norm_a_g': 'new_m', 'new_m_out_norm_b_g': 'new_m', 'new_m_w_out': 'new_m', 'new_m_ffn_norm_g': 'new_m', 'new_m_w_up': 'new_m', 'new_m_w_down': 'new_m', 'new_m_final_norm_g': 'new_m', 'new_v_rel_bias_table': 'new_v', 'new_v_mix_norm_g': 'new_v', 'new_v_w_in': 'new_v', 'new_v_gate_norm_g': 'new_v', 'new_v_gate_norm_b': 'new_v', 'new_v_w_spatial': 'new_v', 'new_v_b_spatial': 'new_v', 'new_v_attn_sinks': 'new_v', 'new_v_out_norm_a_g': 'new_v', 'new_v_out_norm_b_g': 'new_v', 'new_v_w_out': 'new_v', 'new_v_ffn_norm_g': 'new_v', 'new_v_w_up': 'new_v', 'new_v_w_down': 'new_v', 'new_v_final_norm_g': 'new_v'}


def _forward(args):
    return _fwd_reference(*[args[k] for k in FWD_PARAMS])


def _output_shape():
    def fwd():
        inp = _fwd_setup_inputs(0)
        return _fwd_reference(*[inp[k] for k in FWD_PARAMS])
    out = _jax.eval_shape(fwd)
    return out.shape, out.dtype

N_MICROBATCH = 1
ADAM_LR = 0.001
ADAM_B1 = 0.9
ADAM_B2 = 0.999
ADAM_EPS = 1e-08
ADAM_WD = 0.01
ADAM_STEP = 10
PER_EXAMPLE_BATCH_AXIS = {'x': 0, 'loss_target': 0}
SHARED_INPUTS = []
_WEIGHT_DTYPES = {'rel_bias_table': _jnp.float32, 'mix_norm_g': _jnp.float32, 'w_in': _jnp.float32, 'gate_norm_g': _jnp.float32, 'gate_norm_b': _jnp.float32, 'w_spatial': _jnp.float32, 'b_spatial': _jnp.float32, 'attn_sinks': _jnp.float32, 'out_norm_a_g': _jnp.float32, 'out_norm_b_g': _jnp.float32, 'w_out': _jnp.float32, 'ffn_norm_g': _jnp.float32, 'w_up': _jnp.float32, 'w_down': _jnp.float32, 'final_norm_g': _jnp.float32}
MOMENT_SCALE = {'rel_bias_table': 8.696847e-02, 'mix_norm_g': 1.021674e-01, 'w_in': 7.756319e-02, 'gate_norm_g': 3.757251e-02, 'gate_norm_b': 3.966961e-02, 'w_spatial': 3.755358e-02, 'b_spatial': 5.542218e-02, 'attn_sinks': 1.604978e-02, 'out_norm_a_g': 7.229819e-02, 'out_norm_b_g': 6.703237e-02, 'w_out': 6.995028e-02, 'ffn_norm_g': 6.891637e-02, 'w_up': 3.400064e-02, 'w_down': 6.873756e-02, 'final_norm_g': 1.611873e+01}


def _to_microbatches(a, axis):
    t = _jnp.moveaxis(a, axis, 0)
    t = t.reshape((N_MICROBATCH, t.shape[0] // N_MICROBATCH) + t.shape[1:])
    return _jnp.moveaxis(t, 1, axis + 1)


def setup_inputs(seed: int = 0) -> dict:
    inp = _fwd_setup_inputs(seed)
    key = _jax.random.fold_in(_jax.random.key(seed), 7919)
    shape, _ = _output_shape()
    out = dict(inp)
    out["loss_target"] = _jax.random.normal(_jax.random.fold_in(key, 0), shape, _jnp.float32)
    for i, name in enumerate(TWIN_WEIGHTS):
        w = inp[name].astype(_jnp.float32)
        if MOMENT_SCALE is None:
            s = _jnp.sqrt(_jnp.mean(_jnp.square(w)) + 1e-30)
        else:
            s = MOMENT_SCALE[name]
        km, kv = _jax.random.split(_jax.random.fold_in(key, i + 1))
        out[name] = w
        out["m_" + name] = s * _jax.random.normal(km, w.shape, _jnp.float32)
        out["v_" + name] = (s * s) * _jax.random.uniform(kv, w.shape, _jnp.float32, 0.5, 1.5)
    if N_MICROBATCH > 1:
        for name, axis in PER_EXAMPLE_BATCH_AXIS.items():
            out[name] = _to_microbatches(out[name], axis)
    return {'x': out['x'], 'rel_bias_table': out['rel_bias_table'], 'mix_norm_g': out['mix_norm_g'], 'w_in': out['w_in'], 'gate_norm_g': out['gate_norm_g'], 'gate_norm_b': out['gate_norm_b'], 'w_spatial': out['w_spatial'], 'b_spatial': out['b_spatial'], 'attn_sinks': out['attn_sinks'], 'out_norm_a_g': out['out_norm_a_g'], 'out_norm_b_g': out['out_norm_b_g'], 'w_out': out['w_out'], 'ffn_norm_g': out['ffn_norm_g'], 'w_up': out['w_up'], 'w_down': out['w_down'], 'final_norm_g': out['final_norm_g'], 'loss_target': out['loss_target'], 'm_rel_bias_table': out['m_rel_bias_table'], 'm_mix_norm_g': out['m_mix_norm_g'], 'm_w_in': out['m_w_in'], 'm_gate_norm_g': out['m_gate_norm_g'], 'm_gate_norm_b': out['m_gate_norm_b'], 'm_w_spatial': out['m_w_spatial'], 'm_b_spatial': out['m_b_spatial'], 'm_attn_sinks': out['m_attn_sinks'], 'm_out_norm_a_g': out['m_out_norm_a_g'], 'm_out_norm_b_g': out['m_out_norm_b_g'], 'm_w_out': out['m_w_out'], 'm_ffn_norm_g': out['m_ffn_norm_g'], 'm_w_up': out['m_w_up'], 'm_w_down': out['m_w_down'], 'm_final_norm_g': out['m_final_norm_g'], 'v_rel_bias_table': out['v_rel_bias_table'], 'v_mix_norm_g': out['v_mix_norm_g'], 'v_w_in': out['v_w_in'], 'v_gate_norm_g': out['v_gate_norm_g'], 'v_gate_norm_b': out['v_gate_norm_b'], 'v_w_spatial': out['v_w_spatial'], 'v_b_spatial': out['v_b_spatial'], 'v_attn_sinks': out['v_attn_sinks'], 'v_out_norm_a_g': out['v_out_norm_a_g'], 'v_out_norm_b_g': out['v_out_norm_b_g'], 'v_w_out': out['v_w_out'], 'v_ffn_norm_g': out['v_ffn_norm_g'], 'v_w_up': out['v_w_up'], 'v_w_down': out['v_w_down'], 'v_final_norm_g': out['v_final_norm_g']}


def _loss(weights, diff, rest, loss_target):
    with _jax.named_scope("forward"):
        args = {**rest, TWIN_DIFF_INPUT: diff, **{k: w.astype(_WEIGHT_DTYPES[k]) for k, w in weights.items()}}
        y = _forward(args)
    with _jax.named_scope("loss_head"):
        err = _jnp.square(y.astype(_jnp.float32) - loss_target)
        return 0.5 * _jnp.sum(_jnp.mean(err, axis=-1)) if err.ndim else 0.5 * err


def _adamw(w, g, m, v):
    m = ADAM_B1 * m + (1.0 - ADAM_B1) * g
    v = ADAM_B2 * v + (1.0 - ADAM_B2) * _jnp.square(g)
    m_hat = m / (1.0 - ADAM_B1 ** ADAM_STEP)
    v_hat = v / (1.0 - ADAM_B2 ** ADAM_STEP)
    delta = -ADAM_LR * (m_hat / (_jnp.sqrt(v_hat) + ADAM_EPS) + ADAM_WD * w)
    return delta, m, v


def reference(x, rel_bias_table, mix_norm_g, w_in, gate_norm_g, gate_norm_b, w_spatial, b_spatial, attn_sinks, out_norm_a_g, out_norm_b_g, w_out, ffn_norm_g, w_up, w_down, final_norm_g, loss_target, m_rel_bias_table, m_mix_norm_g, m_w_in, m_gate_norm_g, m_gate_norm_b, m_w_spatial, m_b_spatial, m_attn_sinks, m_out_norm_a_g, m_out_norm_b_g, m_w_out, m_ffn_norm_g, m_w_up, m_w_down, m_final_norm_g, v_rel_bias_table, v_mix_norm_g, v_w_in, v_gate_norm_g, v_gate_norm_b, v_w_spatial, v_b_spatial, v_attn_sinks, v_out_norm_a_g, v_out_norm_b_g, v_w_out, v_ffn_norm_g, v_w_up, v_w_down, v_final_norm_g):
    given = dict(x=x, rel_bias_table=rel_bias_table, mix_norm_g=mix_norm_g, w_in=w_in, gate_norm_g=gate_norm_g, gate_norm_b=gate_norm_b, w_spatial=w_spatial, b_spatial=b_spatial, attn_sinks=attn_sinks, out_norm_a_g=out_norm_a_g, out_norm_b_g=out_norm_b_g, w_out=w_out, ffn_norm_g=ffn_norm_g, w_up=w_up, w_down=w_down, final_norm_g=final_norm_g, loss_target=loss_target, m_rel_bias_table=m_rel_bias_table, m_mix_norm_g=m_mix_norm_g, m_w_in=m_w_in, m_gate_norm_g=m_gate_norm_g, m_gate_norm_b=m_gate_norm_b, m_w_spatial=m_w_spatial, m_b_spatial=m_b_spatial, m_attn_sinks=m_attn_sinks, m_out_norm_a_g=m_out_norm_a_g, m_out_norm_b_g=m_out_norm_b_g, m_w_out=m_w_out, m_ffn_norm_g=m_ffn_norm_g, m_w_up=m_w_up, m_w_down=m_w_down, m_final_norm_g=m_final_norm_g, v_rel_bias_table=v_rel_bias_table, v_mix_norm_g=v_mix_norm_g, v_w_in=v_w_in, v_gate_norm_g=v_gate_norm_g, v_gate_norm_b=v_gate_norm_b, v_w_spatial=v_w_spatial, v_b_spatial=v_b_spatial, v_attn_sinks=v_attn_sinks, v_out_norm_a_g=v_out_norm_a_g, v_out_norm_b_g=v_out_norm_b_g, v_w_out=v_w_out, v_ffn_norm_g=v_ffn_norm_g, v_w_up=v_w_up, v_w_down=v_w_down, v_final_norm_g=v_final_norm_g)
    weights = {n: given[n] for n in TWIN_WEIGHTS}
    shared = {n: given[n] for n in SHARED_INPUTS}
    per_example = {n: given[n] for n in ['x']}
    grad_fn = _jax.value_and_grad(_loss, argnums=(0, 1))

    def one_microbatch(ex, loss_target):
        ex = dict(ex)
        diff = ex.pop(TWIN_DIFF_INPUT)
        return grad_fn(weights, diff, {**shared, **ex}, loss_target)

    if N_MICROBATCH == 1:
        loss, (grad_w, grad_x) = one_microbatch(per_example, given["loss_target"])
    else:
        def body(carry, xs):
            loss_sum, grad_sum = carry
            l_k, (gw_k, gx_k) = one_microbatch(xs[0], xs[1])
            with _jax.named_scope("update"):
                return (loss_sum + l_k, _jax.tree.map(_jnp.add, grad_sum, gw_k)), gx_k

        init = (_jnp.zeros((), _jnp.float32), _jax.tree.map(_jnp.zeros_like, weights))
        (loss, grad_w), grad_x = _jax.lax.scan(body, init, (per_example, given["loss_target"]))
    with _jax.named_scope("update"):
        delta_w, new_m, new_v = {}, {}, {}
        for n in TWIN_WEIGHTS:
            delta_w[n], new_m[n], new_v[n] = _adamw(weights[n], grad_w[n], given["m_" + n], given["v_" + n])
    return (loss, grad_x, *[grad_w[n] for n in TWIN_WEIGHTS], *[delta_w[n] for n in TWIN_WEIGHTS],
            *[new_m[n] for n in TWIN_WEIGHTS], *[new_v[n] for n in TWIN_WEIGHTS])
```

```python
import functools
import math

import numpy as np
import jax
import jax.numpy as jnp
from jax import lax
from jax.experimental import pallas as pl
from jax.experimental.pallas import tpu as pltpu

F32 = jnp.float32
BF16 = jnp.bfloat16
SDS = jax.ShapeDtypeStruct
MESH = pl.DeviceIdType.MESH

N_DEV = 8
EPS = 1e-5
NEG = -1e30
CHUNK = 128
GROUP_DIM = 128
HEAD_DIM = 64
KV_HEADS = 2
N_BUCKETS = 32
MAX_DISTANCE = 128
ADAM_LR, ADAM_B1, ADAM_B2, ADAM_EPS, ADAM_WD, ADAM_STEP = 0.001, 0.9, 0.999, 1e-08, 0.01, 10
GELU_C0 = math.sqrt(2.0 / math.pi)
GELU_C1 = 0.044715

V7X_VMEM_BYTES = 64 * 1024 * 1024
VMEM_LIMIT = V7X_VMEM_BYTES - 8 * 1024 * 1024
LANE = 128

NN = ((1,), (0,))
NT = ((1,), (1,))
TN = ((0,), (0,))


def _dot(a, b, dims):
    return lax.dot_general(a, b, (dims, ((), ())), preferred_element_type=F32)


def _tile(n, pref, unit=LANE):
    best = None
    for t in range(unit, min(n, pref) + 1, unit):
        if n % t == 0:
            best = t
    return n if best is None else best


def _params(n_grid):
    return pltpu.CompilerParams(dimension_semantics=("arbitrary",) * n_grid, vmem_limit_bytes=VMEM_LIMIT)


def _gelu(x):
    return 0.5 * x * (1.0 + jnp.tanh(GELU_C0 * (x + GELU_C1 * x * x * x)))


def _gelu_and_grad(x):
    x2 = x * x
    t = jnp.tanh(GELU_C0 * x * (1.0 + GELU_C1 * x2))
    val = 0.5 * x * (1.0 + t)
    grad = 0.5 * (1.0 + t) + 0.5 * x * (1.0 - t * t) * (GELU_C0 * (1.0 + 3.0 * GELU_C1 * x2))
    return val, grad


def _rms_stats(x):
    return lax.rsqrt(jnp.mean(x * x, axis=-1, keepdims=True) + EPS)


def _rms_bwd(dy, x, r, g):
    w = dy * g
    return r * w - x * (r * r * r) * jnp.mean(w * x, axis=-1, keepdims=True)


def _t5_bucket():
    i = np.arange(CHUNK)[:, None]
    j = np.arange(2 * CHUNK)[None, :]
    rel = np.maximum(i + CHUNK - j, 0)
    n_exact = N_BUCKETS // 2
    relf = np.maximum(rel, n_exact).astype(np.float32)
    large = n_exact + (np.log(relf / np.float32(n_exact)) / np.float32(math.log(MAX_DISTANCE / n_exact))
                       * np.float32(N_BUCKETS - n_exact)).astype(np.int32)
    large = np.minimum(large, N_BUCKETS - 1)
    bucket = np.where(rel < n_exact, rel, large)
    in_window = (i + CHUNK - j >= 0) & (i + CHUNK - j < CHUNK)
    return bucket.astype(np.int32), in_window


def _split3(x):
    hi = x.astype(BF16)
    r1 = x - hi.astype(F32)
    mid = r1.astype(BF16)
    lo = (r1 - mid.astype(F32)).astype(BF16)
    return hi, mid, lo


HBM_SPEC = pl.BlockSpec(memory_space=pltpu.HBM)


def _mesh_pos():
    return lax.axis_index("x"), lax.axis_index("y"), lax.axis_index("c")


def _dev_index(px, py, pc):
    return 4 * px + 2 * py + pc


def _all_gather(shards, name):
    n = len(shards)

    def body(*refs):
        ins, outs = refs[:n], refs[n:2 * n]
        send_sems, recv_sems, local_sems = refs[2 * n:]
        x, y, c = _mesh_pos()
        me, sibling = (x, y, c), (x, y, 1 - c)
        chips = [(1 - x, y), (x, 1 - y), (1 - x, 1 - y)]

        def copy(a, k, block, to, src=None):
            dst = outs[a].at[_dev_index(*block)]
            return pltpu.make_async_remote_copy(
                src_ref=dst if src is None else src, dst_ref=dst,
                send_sem=send_sems.at[a * 7 + k], recv_sem=recv_sems.at[a * 7 + k],
                device_id=to, device_id_type=MESH)

        mine = [pltpu.make_async_copy(ins[a], outs[a].at[_dev_index(*me)], local_sems.at[a]) for a in range(n)]
        first = []
        for a in range(n):
            for j, chip in enumerate(chips):
                first.append(copy(a, 1 + j, me, (*chip, c), src=ins[a]))
            first.append(copy(a, 0, me, sibling, src=ins[a]))
        for cp in first:
            cp.start()
        for cp in mine:
            cp.start()
        passed = []
        for a in range(n):
            for j, chip in enumerate(chips):
                copy(a, 1 + j, (*chip, c), me).wait_recv()
                fwd = copy(a, 4 + j, (*chip, c), sibling)
                fwd.start()
                passed.append(fwd)
        for a in range(n):
            copy(a, 0, sibling, me).wait_recv()
            for j, chip in enumerate(chips):
                copy(a, 4 + j, (*chip, 1 - c), me).wait_recv()
        for cp in first + passed:
            cp.wait_send()
        for cp in mine:
            cp.wait()

    return pl.pallas_call(
        body, name=name,
        out_shape=[SDS((N_DEV,) + s.shape, s.dtype) for s in shards],
        in_specs=[HBM_SPEC] * n, out_specs=[HBM_SPEC] * n,
        scratch_shapes=[pltpu.SemaphoreType.DMA((7 * n,)), pltpu.SemaphoreType.DMA((7 * n,)),
                        pltpu.SemaphoreType.DMA((n,))],
    )(*shards)


def _exchange_with_sibling(parts, name):
    n = len(parts)

    def body(*refs):
        ins, outs = refs[:n], refs[n:2 * n]
        send_sems, recv_sems = refs[2 * n:]
        x, y, c = _mesh_pos()
        sibling = (x, y, 1 - c)
        copies = []
        for a in range(n):
            for j in range(4):
                copies.append(pltpu.make_async_remote_copy(
                    src_ref=ins[a].at[2 * j + (1 - c)], dst_ref=outs[a].at[j],
                    send_sem=send_sems.at[a * 4 + j], recv_sem=recv_sems.at[a * 4 + j],
                    device_id=sibling, device_id_type=MESH))
        for cp in copies:
            cp.start()
        for cp in copies:
            cp.wait()

    return pl.pallas_call(
        body, name=name,
        out_shape=[SDS((4,) + p.shape[1:], p.dtype) for p in parts],
        in_specs=[HBM_SPEC] * n, out_specs=[HBM_SPEC] * n,
        scratch_shapes=[pltpu.SemaphoreType.DMA((4 * n,)), pltpu.SemaphoreType.DMA((4 * n,))],
    )(*parts)


def _exchange_across_chips(csums, name):
    n = len(csums)

    def body(*refs):
        ins = refs[:n]
        owns, recvs = refs[n:2 * n], refs[2 * n:3 * n]
        send_sems, recv_sems, local_sems = refs[3 * n:]
        x, y, c = _mesh_pos()
        chips = [(1 - x, y), (x, 1 - y), (1 - x, 1 - y)]
        copies = []
        for a in range(n):
            for r, (px, py) in enumerate(chips):
                copies.append(pltpu.make_async_remote_copy(
                    src_ref=ins[a].at[2 * px + py], dst_ref=recvs[a].at[r],
                    send_sem=send_sems.at[a * 3 + r], recv_sem=recv_sems.at[a * 3 + r],
                    device_id=(px, py, c), device_id_type=MESH))
        local = [pltpu.make_async_copy(ins[a].at[2 * x + y], owns[a], local_sems.at[a]) for a in range(n)]
        for cp in copies + local:
            cp.start()
        for cp in copies + local:
            cp.wait()

    outs = pl.pallas_call(
        body, name=name,
        out_shape=[SDS(p.shape[1:], p.dtype) for p in csums] + [SDS((3,) + p.shape[1:], p.dtype) for p in csums],
        in_specs=[HBM_SPEC] * n, out_specs=[HBM_SPEC] * (2 * n),
        scratch_shapes=[pltpu.SemaphoreType.DMA((3 * n,)), pltpu.SemaphoreType.DMA((3 * n,)),
                        pltpu.SemaphoreType.DMA((n,))],
    )(*csums)
    return outs[:n], outs[n:]


def _chip_sum(part, recv, name):
    _, R, C = part.shape
    tr = _tile(R, 512, 16)
    c_idx = lax.axis_index("c").astype(jnp.int32).reshape((1,))

    def body(c_ref, p_ref, r_ref, o_ref):
        o_ref[...] = (p_ref[...].astype(F32) + r_ref[...].astype(F32)).astype(o_ref.dtype)

    grid_spec = pltpu.PrefetchScalarGridSpec(
        num_scalar_prefetch=1, grid=(4, R // tr),
        in_specs=[pl.BlockSpec((None, tr, C), lambda j, i, c_ref: (2 * j + c_ref[0], i, 0)),
                  pl.BlockSpec((None, tr, C), lambda j, i, c_ref: (j, i, 0))],
        out_specs=pl.BlockSpec((None, tr, C), lambda j, i, c_ref: (j, i, 0)))
    return pl.pallas_call(body, name=name, grid_spec=grid_spec, out_shape=SDS((4, R, C), part.dtype),
                          compiler_params=_params(2))(c_idx, part, recv)


def _bias_fwd(table_t, onehot_t):
    H = table_t.shape[0]
    n = onehot_t.shape[1]

    def body(t_ref, oh_ref, o_ref):
        hi, mid, lo = _split3(t_ref[...])
        oh = oh_ref[...]
        o_ref[...] = _dot(hi, oh, NN) + _dot(mid, oh, NN) + _dot(lo, oh, NN)

    return pl.pallas_call(body, name="bias_fwd", out_shape=SDS((H, n), F32),
                          compiler_params=_params(0))(table_t, onehot_t)


def _inproj_fwd(x, g, w):
    T, D = x.shape
    P = w.shape[1]
    tm, tn = _tile(T, 512), _tile(P, 1792)

    def body(x_ref, g_ref, w_ref, proj_ref, n_ref, nbuf):
        @pl.when(pl.program_id(1) == 0)
        def _():
            xv = x_ref[...]
            n = (xv * _rms_stats(xv) * g_ref[...]).astype(BF16)
            nbuf[...] = n
            n_ref[...] = n

        proj_ref[...] = _dot(nbuf[...], w_ref[...], NN)

    return pl.pallas_call(
        body, name="inproj_fwd", grid=(T // tm, P // tn),
        in_specs=[pl.BlockSpec((tm, D), lambda i, j: (i, 0)), pl.BlockSpec((1, D), lambda i, j: (0, 0)),
                  pl.BlockSpec((D, tn), lambda i, j: (0, j))],
        out_specs=[pl.BlockSpec((tm, tn), lambda i, j: (i, j)), pl.BlockSpec((tm, D), lambda i, j: (i, 0))],
        out_shape=[SDS((T, P), F32), SDS((T, D), BF16)],
        scratch_shapes=[pltpu.VMEM((tm, D), BF16)], compiler_params=_params(2))(x, g, w)


def _layer_norm_group(vg, lg, lb):
    mu = jnp.mean(vg, axis=-1, keepdims=True)
    xc = vg - mu
    rstd = lax.rsqrt(jnp.mean(xc * xc, axis=-1, keepdims=True) + EPS)
    vhat = xc * rstd
    return vhat, rstd, vhat * lg + lb


def _gmlp_fwd(proj, lg, lb, w_s, bs_t, A):
    T = proj.shape[0]
    G = A // GROUP_DIM
    tm = _tile(T, 512)
    nc = tm // CHUNK

    def body(u_ref, v_ref, lg_ref, lb_ref, w_ref, bst_ref, a_ref):
        row = lax.broadcasted_iota(jnp.int32, (CHUNK, CHUNK), 0)
        col = lax.broadcasted_iota(jnp.int32, (CHUNK, CHUNK), 1)
        causal = row >= col
        for g in range(G):
            sl = slice(g * GROUP_DIM, (g + 1) * GROUP_DIM)
            _, _, vn = _layer_norm_group(_gelu(v_ref[:, sl]), lg_ref[:, sl], lb_ref[:, sl])
            vnb = vn.astype(BF16)
            wm = jnp.where(causal, w_ref[g], 0.0).astype(BF16)
            ug = _gelu(u_ref[:, sl])
            bcol = bst_ref[:, g:g + 1]
            for c in range(nc):
                rs = slice(c * CHUNK, (c + 1) * CHUNK)
                a_ref[rs, sl] = ug[rs] * (_dot(wm, vnb[rs], NN) + bcol)

    return pl.pallas_call(
        body, name="gmlp_fwd", grid=(T // tm,),
        in_specs=[pl.BlockSpec((tm, A), lambda i: (i, 0)), pl.BlockSpec((tm, A), lambda i: (i, 1)),
                  pl.BlockSpec((1, A), lambda i: (0, 0)), pl.BlockSpec((1, A), lambda i: (0, 0)),
                  pl.BlockSpec((G, CHUNK, CHUNK), lambda i: (0, 0, 0)), pl.BlockSpec((CHUNK, G), lambda i: (0, 0))],
        out_specs=pl.BlockSpec((tm, A), lambda i: (i, 0)),
        out_shape=SDS((T, A), F32), compiler_params=_params(1))(proj, proj, lg, lb, w_s, bs_t)


def _attn_masks(first_tile):
    ii = lax.broadcasted_iota(jnp.int32, (CHUNK, 2 * CHUNK), 0)
    jj = lax.broadcasted_iota(jnp.int32, (CHUNK, 2 * CHUNK), 1)
    in_window = (jj > ii) & (jj <= ii + CHUNK)
    first_mask = in_window & jnp.logical_or(jnp.logical_not(first_tile), jj >= CHUNK)
    return in_window, first_mask


def _attn_probs(qh, kb, bias_h, mask, sink):
    s = _dot(qh, kb, NT) * (HEAD_DIM ** -0.5) + bias_h
    s = jnp.where(mask, s, NEG)
    m = jnp.maximum(jnp.max(s, axis=-1, keepdims=True), sink)
    p = jnp.exp(s - m)
    e_sink = jnp.exp(sink - m)
    inv = 1.0 / (jnp.sum(p, axis=-1, keepdims=True) + e_sink)
    return p * inv, e_sink * inv


def _attn_specs(tq, A, B, reverse_tiles=None):
    nb = tq // CHUNK
    kcol = (2 * A + B) // LANE
    if reverse_tiles is None:
        tile = lambda i: i
    else:
        tile = lambda i: reverse_tiles - 1 - i
    prev = lambda i: jnp.maximum(tile(i) * nb - 1, 0)
    return [pl.BlockSpec((tq, B), lambda i: (tile(i), 2 * A // B)),
            pl.BlockSpec((tq, LANE), lambda i: (tile(i), kcol)),
            pl.BlockSpec((tq, LANE), lambda i: (tile(i), kcol + 1)),
            pl.BlockSpec((CHUNK, LANE), lambda i: (prev(i), kcol)),
            pl.BlockSpec((CHUNK, LANE), lambda i: (prev(i), kcol + 1))]


def _attn_fwd(proj, bias, sinks, A, B):
    T = proj.shape[0]
    H = B // HEAD_DIM
    qpk = H // KV_HEADS
    tq = _tile(T, 512)
    nb = tq // CHUNK

    def body(sink_ref, q_ref, k_ref, v_ref, kp_ref, vp_ref, bias_ref, o_ref):
        in_window, first_mask = _attn_masks(pl.program_id(0) == 0)
        for b in range(nb):
            rows = slice(b * CHUNK, (b + 1) * CHUNK)
            if b == 0:
                kprev, vprev, mask = kp_ref[...], vp_ref[...], first_mask
            else:
                prows = slice((b - 1) * CHUNK, b * CHUNK)
                kprev, vprev, mask = k_ref[prows, :], v_ref[prows, :], in_window
            kband = jnp.concatenate([kprev, k_ref[rows, :]], axis=0).astype(BF16)
            vband = jnp.concatenate([vprev, v_ref[rows, :]], axis=0).astype(BF16)
            for h in range(H):
                ks = slice((h // qpk) * HEAD_DIM, (h // qpk + 1) * HEAD_DIM)
                hs = slice(h * HEAD_DIM, (h + 1) * HEAD_DIM)
                pn, _ = _attn_probs(q_ref[rows, hs].astype(BF16), kband[:, ks], bias_ref[h], mask, sink_ref[h])
                o_ref[rows, hs] = _dot(pn.astype(BF16), vband[:, ks], NN)

    return pl.pallas_call(
        body, name="attn_fwd", grid=(T // tq,),
        in_specs=[pl.BlockSpec(memory_space=pltpu.SMEM)] + _attn_specs(tq, A, B)
        + [pl.BlockSpec((H, CHUNK, 2 * CHUNK), lambda i: (0, 0, 0))],
        out_specs=pl.BlockSpec((tq, B), lambda i: (i, 0)),
        out_shape=SDS((T, B), F32), compiler_params=_params(1))(sinks, proj, proj, proj, proj, proj, bias)


def _outproj_fwd(a, b, ga, gb, x, w):
    T, A = a.shape
    B = b.shape[1]
    D = x.shape[1]
    tm, tn = _tile(T, 512), _tile(D, 1024)

    def body(a_ref, b_ref, ga_ref, gb_ref, x_ref, w_ref, h_ref, mix_ref, mbuf):
        @pl.when(pl.program_id(1) == 0)
        def _():
            av, bv = a_ref[...], b_ref[...]
            mbuf[:, :A] = (av * _rms_stats(av) * ga_ref[...]).astype(BF16)
            mbuf[:, A:] = (bv * _rms_stats(bv) * gb_ref[...]).astype(BF16)
            mix_ref[...] = mbuf[...]

        h_ref[...] = x_ref[...] + _dot(mbuf[...], w_ref[...], NN)

    return pl.pallas_call(
        body, name="outproj_fwd", grid=(T // tm, D // tn),
        in_specs=[pl.BlockSpec((tm, A), lambda i, j: (i, 0)), pl.BlockSpec((tm, B), lambda i, j: (i, 0)),
                  pl.BlockSpec((1, A), lambda i, j: (0, 0)), pl.BlockSpec((1, B), lambda i, j: (0, 0)),
                  pl.BlockSpec((tm, tn), lambda i, j: (i, j)), pl.BlockSpec((A + B, tn), lambda i, j: (0, j))],
        out_specs=[pl.BlockSpec((tm, tn), lambda i, j: (i, j)), pl.BlockSpec((tm, A + B), lambda i, j: (i, 0))],
        out_shape=[SDS((T, D), F32), SDS((T, A + B), BF16)],
        scratch_shapes=[pltpu.VMEM((tm, A + B), BF16)], compiler_params=_params(2))(a, b, ga, gb, x, w)


def _ffn_fwd(h1, g, w_up, w_down):
    T, D = h1.shape
    Fb = w_up.shape[2]
    F = N_DEV * Fb
    tm, tf = _tile(T, 512), _tile(Fb, 1024)
    per = Fb // tf
    nj = F // tf

    def body(h_ref, g_ref, wu_ref, wd_ref, h2_ref, z_ref, n_ref, nbuf, acc):
        j = pl.program_id(1)

        @pl.when(j == 0)
        def _():
            hv = h_ref[...]
            n = (hv * _rms_stats(hv) * g_ref[...]).astype(BF16)
            nbuf[...] = n
            n_ref[...] = n
            acc[...] = hv

        z = jnp.maximum(_dot(nbuf[...], wu_ref[...], NN), 0.0)
        z_ref[...] = z.astype(BF16)
        acc[...] += _dot((z * z).astype(BF16), wd_ref[...], NN)

        @pl.when(j == nj - 1)
        def _():
            h2_ref[...] = acc[...]

    return pl.pallas_call(
        body, name="ffn_fwd", grid=(T // tm, nj),
        in_specs=[pl.BlockSpec((tm, D), lambda i, j: (i, 0)), pl.BlockSpec((1, D), lambda i, j: (0, 0)),
                  pl.BlockSpec((None, D, tf), lambda i, j: (j // per, 0, j % per)),
                  pl.BlockSpec((None, tf, D), lambda i, j: (j // per, j % per, 0))],
        out_specs=[pl.BlockSpec((tm, D), lambda i, j: (i, 0)), pl.BlockSpec((tm, tf), lambda i, j: (i, j)),
                   pl.BlockSpec((tm, D), lambda i, j: (i, 0))],
        out_shape=[SDS((T, D), F32), SDS((T, F), BF16), SDS((T, D), BF16)],
        scratch_shapes=[pltpu.VMEM((tm, D), BF16), pltpu.VMEM((tm, D), F32)],
        compiler_params=_params(2))(h1, g, w_up, w_down)


def _final_loss(h2, g, target):
    T, D = h2.shape
    tm = _tile(T, 512)

    def body(h_ref, g_ref, t_ref, loss_ref, dg_ref, dh_ref, dhb_ref):
        @pl.when(pl.program_id(0) == 0)
        def _():
            loss_ref[...] = jnp.zeros_like(loss_ref)
            dg_ref[...] = jnp.zeros_like(dg_ref)

        hv, gv = h_ref[...], g_ref[...]
        r = _rms_stats(hv)
        hn = hv * r
        e = hn * gv - t_ref[...]
        loss_ref[...] += (0.5 / D) * jnp.sum(jnp.sum(e * e, axis=-1, keepdims=True), axis=0, keepdims=True)
        dy = e * (1.0 / D)
        dg_ref[...] += jnp.sum(dy * hn, axis=0, keepdims=True)
        dh = _rms_bwd(dy, hv, r, gv)
        dh_ref[...] = dh
        dhb_ref[...] = dh.astype(BF16)

    return pl.pallas_call(
        body, name="final_loss", grid=(T // tm,),
        in_specs=[pl.BlockSpec((tm, D), lambda i: (i, 0)), pl.BlockSpec((1, D), lambda i: (0, 0)),
                  pl.BlockSpec((tm, D), lambda i: (i, 0))],
        out_specs=[pl.BlockSpec((1, 1), lambda i: (0, 0)), pl.BlockSpec((1, D), lambda i: (0, 0)),
                   pl.BlockSpec((tm, D), lambda i: (i, 0)), pl.BlockSpec((tm, D), lambda i: (i, 0))],
        out_shape=[SDS((1, 1), F32), SDS((1, D), F32), SDS((T, D), F32), SDS((T, D), BF16)],
        compiler_params=_params(1))(h2, g, target)


def _ffn_bwd(dh2, dh2b, z, h1, g, w_up, w_down):
    T, D = h1.shape
    Fb = w_up.shape[2]
    F = N_DEV * Fb
    tm, tf = _tile(T, 512), _tile(Fb, 512)
    per = Fb // tf
    nj = F // tf

    def body(dh_ref, dhb_ref, z_ref, h_ref, g_ref, wu_ref, wd_ref, dzp_ref, dh1_ref, dh1b_ref, dg_ref, acc):
        i, j = pl.program_id(0), pl.program_id(1)

        @pl.when(j == 0)
        def _():
            acc[...] = jnp.zeros_like(acc)

        @pl.when((i == 0) & (j == 0))
        def _():
            dg_ref[...] = jnp.zeros_like(dg_ref)

        dzz = _dot(dhb_ref[...], wd_ref[...], NT)
        dzp = (dzz * (2.0 * z_ref[...].astype(F32))).astype(BF16)
        dzp_ref[...] = dzp
        acc[...] += _dot(dzp, wu_ref[...], NT)

        @pl.when(j == nj - 1)
        def _():
            hv, gv, dn = h_ref[...], g_ref[...], acc[...]
            r = _rms_stats(hv)
            dg_ref[...] += jnp.sum(dn * (hv * r), axis=0, keepdims=True)
            dh1 = dh_ref[...] + _rms_bwd(dn, hv, r, gv)
            dh1_ref[...] = dh1
            dh1b_ref[...] = dh1.astype(BF16)

    return pl.pallas_call(
        body, name="ffn_bwd", grid=(T // tm, nj),
        in_specs=[pl.BlockSpec((tm, D), lambda i, j: (i, 0)), pl.BlockSpec((tm, D), lambda i, j: (i, 0)),
                  pl.BlockSpec((tm, tf), lambda i, j: (i, j)), pl.BlockSpec((tm, D), lambda i, j: (i, 0)),
                  pl.BlockSpec((1, D), lambda i, j: (0, 0)),
                  pl.BlockSpec((None, D, tf), lambda i, j: (j // per, 0, j % per)),
                  pl.BlockSpec((None, tf, D), lambda i, j: (j // per, j % per, 0))],
        out_specs=[pl.BlockSpec((tm, tf), lambda i, j: (i, j)), pl.BlockSpec((tm, D), lambda i, j: (i, 0)),
                   pl.BlockSpec((tm, D), lambda i, j: (i, 0)), pl.BlockSpec((1, D), lambda i, j: (0, 0))],
        out_shape=[SDS((T, F), BF16), SDS((T, D), F32), SDS((T, D), BF16), SDS((1, D), F32)],
        scratch_shapes=[pltpu.VMEM((tm, D), F32)], compiler_params=_params(2))(dh2, dh2b, z, h1, g, w_up, w_down)


def _matmul_tn(a, b, name, square_a=False, col_blocks=None):
    T, K = a.shape
    N = b.shape[1]
    tt, tk = _tile(T, 1024), _tile(K, 1024)
    tn = _tile(N if col_blocks is None else N // col_blocks, 1792)
    nt = T // tt

    def body(a_ref, b_ref, o_ref, acc):
        t = pl.program_id(2)

        @pl.when(t == 0)
        def _():
            acc[...] = jnp.zeros_like(acc)

        av = a_ref[...]
        if square_a:
            af = av.astype(F32)
            av = (af * af).astype(BF16)
        acc[...] += _dot(av, b_ref[...], TN)

        @pl.when(t == nt - 1)
        def _():
            o_ref[...] = acc[...].astype(o_ref.dtype)

    if col_blocks is None:
        out_shape = SDS((K, N), BF16)
        out_spec = pl.BlockSpec((tk, tn), lambda i, j, t: (i, j))
    else:
        per = (N // col_blocks) // tn
        out_shape = SDS((col_blocks, K, N // col_blocks), BF16)
        out_spec = pl.BlockSpec((None, tk, tn), lambda i, j, t: (j // per, i, j % per))
    return pl.pallas_call(
        body, name=name, grid=(K // tk, N // tn, nt),
        in_specs=[pl.BlockSpec((tt, tk), lambda i, j, t: (t, i)), pl.BlockSpec((tt, tn), lambda i, j, t: (t, j))],
        out_specs=out_spec, out_shape=out_shape,
        scratch_shapes=[pltpu.VMEM((tk, tn), F32)], compiler_params=_params(3))(a, b)


def _outproj_bwd(dh1b, w, a, b, ga, gb):
    T, D = dh1b.shape
    A, B = a.shape[1], b.shape[1]
    tm = _tile(T, 512)

    def body(dh_ref, w_ref, a_ref, b_ref, ga_ref, gb_ref, da_ref, db_ref, dga_ref, dgb_ref):
        @pl.when(pl.program_id(0) == 0)
        def _():
            dga_ref[...] = jnp.zeros_like(dga_ref)
            dgb_ref[...] = jnp.zeros_like(dgb_ref)

        dmix = _dot(dh_ref[...], w_ref[...], NT)
        for src_ref, g_ref, dx_ref, dg_ref, dn in ((a_ref, ga_ref, da_ref, dga_ref, dmix[:, :A]),
                                                   (b_ref, gb_ref, db_ref, dgb_ref, dmix[:, A:])):
            xv = src_ref[...]
            r = _rms_stats(xv)
            dg_ref[...] += jnp.sum(dn * (xv * r), axis=0, keepdims=True)
            dx_ref[...] = _rms_bwd(dn, xv, r, g_ref[...])

    return pl.pallas_call(
        body, name="outproj_bwd", grid=(T // tm,),
        in_specs=[pl.BlockSpec((tm, D), lambda i: (i, 0)), pl.BlockSpec((A + B, D), lambda i: (0, 0)),
                  pl.BlockSpec((tm, A), lambda i: (i, 0)), pl.BlockSpec((tm, B), lambda i: (i, 0)),
                  pl.BlockSpec((1, A), lambda i: (0, 0)), pl.BlockSpec((1, B), lambda i: (0, 0))],
        out_specs=[pl.BlockSpec((tm, A), lambda i: (i, 0)), pl.BlockSpec((tm, B), lambda i: (i, 0)),
                   pl.BlockSpec((1, A), lambda i: (0, 0)), pl.BlockSpec((1, B), lambda i: (0, 0))],
        out_shape=[SDS((T, A), F32), SDS((T, B), F32), SDS((1, A), F32), SDS((1, B), F32)],
        compiler_params=_params(1))(dh1b, w, a, b, ga, gb)


def _gmlp_bwd(proj, da, lg, lb, w_s, w_st, bs_t, A):
    T = proj.shape[0]
    G = A // GROUP_DIM
    tm = _tile(T, 512)
    nc = tm // CHUNK

    def body(u_ref, v_ref, da_ref, lg_ref, lb_ref, w_ref, wt_ref, bst_ref, duv_ref, dlg_ref, dlb_ref, dw_ref, dbs_ref):
        @pl.when(pl.program_id(0) == 0)
        def _():
            dlg_ref[...] = jnp.zeros_like(dlg_ref)
            dlb_ref[...] = jnp.zeros_like(dlb_ref)
            dw_ref[...] = jnp.zeros_like(dw_ref)
            dbs_ref[...] = jnp.zeros_like(dbs_ref)

        row = lax.broadcasted_iota(jnp.int32, (CHUNK, CHUNK), 0)
        col = lax.broadcasted_iota(jnp.int32, (CHUNK, CHUNK), 1)
        lower = row >= col
        upper = row <= col
        for g in range(G):
            sl = slice(g * GROUP_DIM, (g + 1) * GROUP_DIM)
            lgv = lg_ref[:, sl]
            vg, vg_grad = _gelu_and_grad(v_ref[:, sl])
            vhat, rstd, vn = _layer_norm_group(vg, lgv, lb_ref[:, sl])
            vnb = vn.astype(BF16)
            ug, ug_grad = _gelu_and_grad(u_ref[:, sl])
            dav = da_ref[:, sl]
            wm = jnp.where(lower, w_ref[g], 0.0).astype(BF16)
            wmt = jnp.where(upper, wt_ref[g], 0.0).astype(BF16)
            bcol = bst_ref[:, g:g + 1]
            dw_acc = jnp.zeros((CHUNK, CHUNK), F32)
            dbs_acc = jnp.zeros((CHUNK, 1), F32)
            dvn_parts = []
            dug_parts = []
            for c in range(nc):
                rs = slice(c * CHUNK, (c + 1) * CHUNK)
                mixed = _dot(wm, vnb[rs], NN) + bcol
                dug_parts.append(dav[rs] * mixed)
                dmix = dav[rs] * ug[rs]
                dbs_acc = dbs_acc + jnp.sum(dmix, axis=-1, keepdims=True)
                dmixb = dmix.astype(BF16)
                dw_acc = dw_acc + _dot(dmixb, vnb[rs], NT)
                dvn_parts.append(_dot(wmt, dmixb, NN))
            dvn = jnp.concatenate(dvn_parts, axis=0)
            dug = jnp.concatenate(dug_parts, axis=0)
            dw_ref[g] += jnp.where(lower, dw_acc, 0.0)
            dbs_ref[:, g:g + 1] += dbs_acc
            dlg_ref[:, sl] += jnp.sum(dvn * vhat, axis=0, keepdims=True)
            dlb_ref[:, sl] += jnp.sum(dvn, axis=0, keepdims=True)
            dvhat = dvn * lgv
            dvg = rstd * (dvhat - jnp.mean(dvhat, axis=-1, keepdims=True)
                          - vhat * jnp.mean(dvhat * vhat, axis=-1, keepdims=True))
            duv_ref[:, sl] = (dug * ug_grad).astype(BF16)
            duv_ref[:, A + g * GROUP_DIM:A + (g + 1) * GROUP_DIM] = (dvg * vg_grad).astype(BF16)

    return pl.pallas_call(
        body, name="gmlp_bwd", grid=(T // tm,),
        in_specs=[pl.BlockSpec((tm, A), lambda i: (i, 0)), pl.BlockSpec((tm, A), lambda i: (i, 1)),
                  pl.BlockSpec((tm, A), lambda i: (i, 0)),
                  pl.BlockSpec((1, A), lambda i: (0, 0)), pl.BlockSpec((1, A), lambda i: (0, 0)),
                  pl.BlockSpec((G, CHUNK, CHUNK), lambda i: (0, 0, 0)),
                  pl.BlockSpec((G, CHUNK, CHUNK), lambda i: (0, 0, 0)), pl.BlockSpec((CHUNK, G), lambda i: (0, 0))],
        out_specs=[pl.BlockSpec((tm, 2 * A), lambda i: (i, 0)),
                   pl.BlockSpec((1, A), lambda i: (0, 0)), pl.BlockSpec((1, A), lambda i: (0, 0)),
                   pl.BlockSpec((G, CHUNK, CHUNK), lambda i: (0, 0, 0)), pl.BlockSpec((CHUNK, G), lambda i: (0, 0))],
        out_shape=[SDS((T, 2 * A), BF16), SDS((1, A), F32), SDS((1, A), F32),
                   SDS((G, CHUNK, CHUNK), F32), SDS((CHUNK, G), F32)],
        compiler_params=_params(1))(proj, proj, da, lg, lb, w_s, w_st, bs_t)


def _attn_bwd(proj, o, do, duv, bias, sinks, A, B):
    T, P = proj.shape
    H = B // HEAD_DIM
    qpk = H // KV_HEADS
    tq = _tile(T, 512)
    nb = tq // CHUNK
    n_tiles = T // tq
    scale = HEAD_DIM ** -0.5
    rev = lambda i: n_tiles - 1 - i

    def body(sink_ref, q_ref, k_ref, v_ref, kp_ref, vp_ref, o_ref, do_ref, duv_ref, bias_ref,
             dproj_ref, dbias_ref, dsink_ref, carry, dkv, sacc):
        step = pl.program_id(0)

        @pl.when(step == 0)
        def _():
            carry[...] = jnp.zeros_like(carry)
            sacc[...] = jnp.zeros_like(sacc)
            dbias_ref[...] = jnp.zeros_like(dbias_ref)

        in_window, first_mask = _attn_masks(step == n_tiles - 1)
        dproj_ref[:, :2 * A] = duv_ref[...]
        dkv[...] = jnp.zeros_like(dkv)
        for b in range(nb):
            rows = slice(b * CHUNK, (b + 1) * CHUNK)
            band = slice(b * CHUNK, (b + 2) * CHUNK)
            if b == 0:
                kprev, vprev, mask = kp_ref[...], vp_ref[...], first_mask
            else:
                prows = slice((b - 1) * CHUNK, b * CHUNK)
                kprev, vprev, mask = k_ref[prows, :], v_ref[prows, :], in_window
            kband = jnp.concatenate([kprev, k_ref[rows, :]], axis=0).astype(BF16)
            vband = jnp.concatenate([vprev, v_ref[rows, :]], axis=0).astype(BF16)
            for kv in range(KV_HEADS):
                ks = slice(kv * HEAD_DIM, (kv + 1) * HEAD_DIM)
                kb, vb = kband[:, ks], vband[:, ks]
                dk_acc = jnp.zeros((2 * CHUNK, HEAD_DIM), F32)
                dv_acc = jnp.zeros((2 * CHUNK, HEAD_DIM), F32)
                for h in range(kv * qpk, (kv + 1) * qpk):
                    hs = slice(h * HEAD_DIM, (h + 1) * HEAD_DIM)
                    qh = q_ref[rows, hs].astype(BF16)
                    pn, p_sink = _attn_probs(qh, kb, bias_ref[h], mask, sink_ref[h])
                    doh = do_ref[rows, hs]
                    delta = jnp.sum(doh * o_ref[rows, hs], axis=-1, keepdims=True)
                    dohb = doh.astype(BF16)
                    ds = pn * (_dot(dohb, vb, NT) - delta)
                    dbias_ref[h] += ds
                    sacc[:, h:h + 1] += -(p_sink * delta)
                    dsb = ds.astype(BF16)
                    dproj_ref[rows, 2 * A + h * HEAD_DIM:2 * A + (h + 1) * HEAD_DIM] = (
                        _dot(dsb, kb, NN) * scale).astype(BF16)
                    dk_acc = dk_acc + _dot(dsb, qh, TN)
                    dv_acc = dv_acc + _dot(pn.astype(BF16), dohb, TN)
                dkv[band, ks] += dk_acc * scale
                dkv[band, LANE + kv * HEAD_DIM:LANE + (kv + 1) * HEAD_DIM] += dv_acc
        last = slice(tq, tq + CHUNK)
        dkv[last, :] += carry[...]
        dproj_ref[:, 2 * A + B:] = dkv[CHUNK:, :].astype(BF16)
        carry[...] = dkv[:CHUNK, :]

        @pl.when(step == n_tiles - 1)
        def _():
            dsink_ref[...] = jnp.sum(sacc[...], axis=0, keepdims=True)

    specs = _attn_specs(tq, A, B, reverse_tiles=n_tiles)
    return pl.pallas_call(
        body, name="attn_bwd", grid=(n_tiles,),
        in_specs=[pl.BlockSpec(memory_space=pltpu.SMEM)] + specs
        + [pl.BlockSpec((tq, B), lambda i: (rev(i), 0)), pl.BlockSpec((tq, B), lambda i: (rev(i), 0)),
           pl.BlockSpec((tq, 2 * A), lambda i: (rev(i), 0)),
           pl.BlockSpec((H, CHUNK, 2 * CHUNK), lambda i: (0, 0, 0))],
        out_specs=[pl.BlockSpec((tq, P), lambda i: (rev(i), 0)),
                   pl.BlockSpec((H, CHUNK, 2 * CHUNK), lambda i: (0, 0, 0)), pl.BlockSpec((1, H), lambda i: (0, 0))],
        out_shape=[SDS((T, P), BF16), SDS((H, CHUNK, 2 * CHUNK), F32), SDS((1, H), F32)],
        scratch_shapes=[pltpu.VMEM((CHUNK, 2 * LANE), F32), pltpu.VMEM((tq + CHUNK, 2 * LANE), F32),
                        pltpu.VMEM((CHUNK, H), F32)],
        compiler_params=_params(1))(sinks, proj, proj, proj, proj, proj, o, do, duv, bias)


def _bias_bwd(dbias, onehot):
    H = dbias.shape[0]
    nbk = onehot.shape[1]

    def body(d_ref, oh_ref, o_ref):
        hi, mid, lo = _split3(d_ref[...])
        oh = oh_ref[...]
        o_ref[...] = _dot(hi, oh, NN) + _dot(mid, oh, NN) + _dot(lo, oh, NN)

    return pl.pallas_call(body, name="bias_bwd", out_shape=SDS((H, nbk), F32),
                          compiler_params=_params(0))(dbias, onehot)


def _inproj_bwd(dproj, w, x, dh1, g):
    T, P = dproj.shape
    D = x.shape[1]
    tm, tn = _tile(T, 512), _tile(P, 1792)
    nj = P // tn

    def body(dp_ref, w_ref, x_ref, dh_ref, g_ref, dx_ref, dg_ref, acc):
        i, j = pl.program_id(0), pl.program_id(1)

        @pl.when(j == 0)
        def _():
            acc[...] = jnp.zeros_like(acc)

        @pl.when((i == 0) & (j == 0))
        def _():
            dg_ref[...] = jnp.zeros_like(dg_ref)

        acc[...] += _dot(dp_ref[...], w_ref[...], NT)

        @pl.when(j == nj - 1)
        def _():
            xv, dn = x_ref[...], acc[...]
            r = _rms_stats(xv)
            dg_ref[...] += jnp.sum(dn * (xv * r), axis=0, keepdims=True)
            dx_ref[...] = dh_ref[...] + _rms_bwd(dn, xv, r, g_ref[...])

    return pl.pallas_call(
        body, name="inproj_bwd", grid=(T // tm, nj),
        in_specs=[pl.BlockSpec((tm, tn), lambda i, j: (i, j)), pl.BlockSpec((D, tn), lambda i, j: (0, j)),
                  pl.BlockSpec((tm, D), lambda i, j: (i, 0)), pl.BlockSpec((tm, D), lambda i, j: (i, 0)),
                  pl.BlockSpec((1, D), lambda i, j: (0, 0))],
        out_specs=[pl.BlockSpec((tm, D), lambda i, j: (i, 0)), pl.BlockSpec((1, D), lambda i, j: (0, 0))],
        out_shape=[SDS((T, D), F32), SDS((1, D), F32)],
        scratch_shapes=[pltpu.VMEM((tm, D), F32)], compiler_params=_params(2))(dproj, w, x, dh1, g)


def _adamw(w, g, m, v):
    m = ADAM_B1 * m + (1.0 - ADAM_B1) * g
    v = ADAM_B2 * v + (1.0 - ADAM_B2) * (g * g)
    m_hat = m / (1.0 - ADAM_B1 ** ADAM_STEP)
    v_hat = v / (1.0 - ADAM_B2 ** ADAM_STEP)
    delta = -ADAM_LR * (m_hat / (jnp.sqrt(v_hat) + ADAM_EPS) + ADAM_WD * w)
    return delta, m, v


def _adam_sharded(own, recv, w, m, v, name):
    R, C = w.shape
    tr = _tile(R, 256, 16)

    def body(own_ref, recv_ref, w_ref, m_ref, v_ref, g_ref, d_ref, nm_ref, nv_ref):
        g = own_ref[...].astype(F32)
        for r in range(3):
            g = g + recv_ref[r].astype(F32)
        delta, nm, nv = _adamw(w_ref[...], g, m_ref[...], v_ref[...])
        g_ref[...] = g
        d_ref[...] = delta
        nm_ref[...] = nm
        nv_ref[...] = nv

    blk = pl.BlockSpec((tr, C), lambda i: (i, 0))
    return pl.pallas_call(
        body, name=name, grid=(R // tr,),
        in_specs=[blk, pl.BlockSpec((3, tr, C), lambda i: (0, i, 0)), blk, blk, blk],
        out_specs=[blk] * 4, out_shape=[SDS((R, C), F32)] * 4, compiler_params=_params(1))(own, recv, w, m, v)


def _adam_small(gathered, w, m, v):
    R = w.shape[0]

    def body(p_ref, w_ref, m_ref, v_ref, g_ref, d_ref, nm_ref, nv_ref):
        g = p_ref[0]
        for d in range(1, N_DEV):
            g = g + p_ref[d]
        delta, nm, nv = _adamw(w_ref[...], g, m_ref[...], v_ref[...])
        g_ref[...] = g
        d_ref[...] = delta
        nm_ref[...] = nm
        nv_ref[...] = nv

    return pl.pallas_call(body, name="adam_small", out_shape=[SDS((R, LANE), F32)] * 4,
                          compiler_params=_params(0))(gathered, w, m, v)


def _pack(arrays):
    tile = 8 * LANE
    pieces = []
    for a in arrays:
        flat = a.reshape(-1).astype(F32)
        pieces.append(jnp.pad(flat, (0, (-flat.size) % tile)))
    return jnp.concatenate(pieces).reshape(-1, LANE)


def _unpack(packed, shapes):
    tile = 8 * LANE
    flat = packed.reshape(-1)
    out, off = [], 0
    for s in shapes:
        size = int(np.prod(s))
        out.append(flat[off:off + size].reshape(s))
        off += size + (-size) % tile
    return out


def kernel(x, rel_bias_table, mix_norm_g, w_in, gate_norm_g, gate_norm_b, w_spatial, b_spatial, attn_sinks, out_norm_a_g, out_norm_b_g, w_out, ffn_norm_g, w_up, w_down, final_norm_g, loss_target, m_rel_bias_table, m_mix_norm_g, m_w_in, m_gate_norm_g, m_gate_norm_b, m_w_spatial, m_b_spatial, m_attn_sinks, m_out_norm_a_g, m_out_norm_b_g, m_w_out, m_ffn_norm_g, m_w_up, m_w_down, m_final_norm_g, v_rel_bias_table, v_mix_norm_g, v_w_in, v_gate_norm_g, v_gate_norm_b, v_w_spatial, v_b_spatial, v_attn_sinks, v_out_norm_a_g, v_out_norm_b_g, v_w_out, v_ffn_norm_g, v_w_up, v_w_down, v_final_norm_g):
    T, D = x.shape[1], x.shape[2]
    A = D // 2
    B = D // 2
    G = A // GROUP_DIM
    H = B // HEAD_DIM
    P = 2 * A + B + 2 * KV_HEADS * HEAD_DIM
    Pb = w_in.shape[2]
    xs = x.reshape(T, D)
    target = loss_target.reshape(T, D)

    shards = [w_in[0].astype(BF16), w_out[0].astype(BF16), w_up[0].astype(BF16), w_down[0].astype(BF16)]
    win_g, wout_g, wup_g, wdown_g = _all_gather(shards, "gather_weights")
    win_full = jnp.transpose(win_g, (1, 0, 2)).reshape(D, P)
    wout_full = wout_g.reshape(A + B, D)

    g1, g2, g3 = mix_norm_g.reshape(1, D), ffn_norm_g.reshape(1, D), final_norm_g.reshape(1, D)
    lg, lb = gate_norm_g.reshape(1, A), gate_norm_b.reshape(1, A)
    ws = w_spatial[0]
    ws_t = jnp.swapaxes(ws, 1, 2)
    bs_t = jnp.transpose(b_spatial[0])
    ga, gb = out_norm_a_g.reshape(1, A), out_norm_b_g.reshape(1, B)
    sinks = attn_sinks.reshape(H)
    bucket, in_window = _t5_bucket()
    onehot_np = ((bucket.reshape(-1, 1) == np.arange(N_BUCKETS)[None, :]) & in_window.reshape(-1, 1))
    onehot = jnp.asarray(onehot_np.astype(np.float32)).astype(BF16)

    bias = _bias_fwd(jnp.transpose(rel_bias_table), jnp.transpose(onehot)).reshape(H, CHUNK, 2 * CHUNK)
    proj, n1 = _inproj_fwd(xs, g1, win_full)
    a_out = _gmlp_fwd(proj, lg, lb, ws, bs_t, A)
    b_out = _attn_fwd(proj, bias, sinks, A, B)
    h1, mixed = _outproj_fwd(a_out, b_out, ga, gb, xs, wout_full)
    h2, z, n2 = _ffn_fwd(h1, g2, wup_g, wdown_g)
    loss_part, dg3, dh2, dh2b = _final_loss(h2, g3, target)

    dzp, dh1, dh1b, dg2 = _ffn_bwd(dh2, dh2b, z, h1, g2, wup_g, wdown_g)
    dwdown = _matmul_tn(z, dh2b, "grad_w_down", square_a=True).reshape(wdown_g.shape)
    dwup = _matmul_tn(n2, dzp, "grad_w_up", col_blocks=N_DEV)
    da, db, dga, dgb = _outproj_bwd(dh1b, wout_full, a_out, b_out, ga, gb)
    dwout = _matmul_tn(mixed, dh1b, "grad_w_out").reshape(wout_g.shape)
    duv, dlg, dlb, dws, dbs_t = _gmlp_bwd(proj, da, lg, lb, ws, ws_t, bs_t, A)
    dproj, dbias, dsinks = _attn_bwd(proj, b_out, db, duv, bias, sinks, A, B)
    dtable_t = _bias_bwd(dbias.reshape(H, -1), onehot)
    dwin = _matmul_tn(n1, dproj, "grad_w_in")
    dwin = jnp.transpose(dwin.reshape(D, N_DEV, Pb), (1, 0, 2))
    grad_x, dg1 = _inproj_bwd(dproj, win_full, xs, dh1, g1)

    parts = [dwin, dwout, dwup, dwdown]
    from_sibling = _exchange_with_sibling(parts, "reduce_sibling")
    csums = [_chip_sum(p, r, "chip_sum_%d" % k) for k, (p, r) in enumerate(zip(parts, from_sibling))]
    owns, recvs = _exchange_across_chips(csums, "reduce_chips")
    big = []
    for k, (w, m, v) in enumerate(((w_in, m_w_in, v_w_in), (w_out, m_w_out, v_w_out),
                                   (w_up, m_w_up, v_w_up), (w_down, m_w_down, v_w_down))):
        outs = _adam_sharded(owns[k], recvs[k], w[0], m[0], v[0], "adam_%d" % k)
        big.append([o.reshape(w.shape) for o in outs])

    small_w = [rel_bias_table, mix_norm_g, gate_norm_g, gate_norm_b, w_spatial, b_spatial, attn_sinks,
               out_norm_a_g, out_norm_b_g, ffn_norm_g, final_norm_g]
    small_m = [m_rel_bias_table, m_mix_norm_g, m_gate_norm_g, m_gate_norm_b, m_w_spatial, m_b_spatial, m_attn_sinks,
               m_out_norm_a_g, m_out_norm_b_g, m_ffn_norm_g, m_final_norm_g]
    small_v = [v_rel_bias_table, v_mix_norm_g, v_gate_norm_g, v_gate_norm_b, v_w_spatial, v_b_spatial, v_attn_sinks,
               v_out_norm_a_g, v_out_norm_b_g, v_ffn_norm_g, v_final_norm_g]
    small_g = [jnp.transpose(dtable_t), dg1, dlg, dlb, dws, jnp.transpose(dbs_t), dsinks, dga, dgb, dg2, dg3]
    shapes = [w.shape for w in small_w]
    (gathered,) = _all_gather([_pack(small_g)], "gather_small_grads")
    sg, sd, sm, sv = [_unpack(o, shapes) for o in _adam_small(gathered, _pack(small_w), _pack(small_m), _pack(small_v))]

    loss = lax.psum(loss_part[0, 0], ("x", "y", "c"))

    order = ["s0", "s1", "b0", "s2", "s3", "s4", "s5", "s6", "s7", "s8", "b1", "s9", "b2", "b3", "s10"]

    def group(idx):
        small = (sg, sd, sm, sv)[idx]
        return [small[int(t[1:])] if t[0] == "s" else big[int(t[1:])][idx] for t in order]

    return (loss, grad_x.reshape(x.shape), *group(0), *group(1), *group(2), *group(3))
```

```python
import functools
import math

import numpy as np
import jax
import jax.numpy as jnp
from jax import lax
from jax.experimental import pallas as pl
from jax.experimental.pallas import tpu as pltpu

F32 = jnp.float32
BF16 = jnp.bfloat16
SDS = jax.ShapeDtypeStruct
MESH = pl.DeviceIdType.MESH

N_DEV = 8
EPS = 1e-5
NEG = -1e30
CHUNK = 128
GROUP_DIM = 128
HEAD_DIM = 64
KV_HEADS = 2
N_BUCKETS = 32
MAX_DISTANCE = 128
ADAM_LR, ADAM_B1, ADAM_B2, ADAM_EPS, ADAM_WD, ADAM_STEP = 0.001, 0.9, 0.999, 1e-08, 0.01, 10
GELU_C0 = math.sqrt(2.0 / math.pi)
GELU_C1 = 0.044715

V7X_VMEM_BYTES = 64 * 1024 * 1024
VMEM_LIMIT = V7X_VMEM_BYTES - 8 * 1024 * 1024
LANE = 128

NN = ((1,), (0,))
NT = ((1,), (1,))
TN = ((0,), (0,))


def _dot(a, b, dims):
    return lax.dot_general(a, b, (dims, ((), ())), preferred_element_type=F32)


def _tile(n, pref, unit=LANE):
    best = None
    for t in range(unit, min(n, pref) + 1, unit):
        if n % t == 0:
            best = t
    return n if best is None else best


def _params(n_grid):
    return pltpu.CompilerParams(dimension_semantics=("arbitrary",) * n_grid, vmem_limit_bytes=VMEM_LIMIT)


def _gelu(x):
    return 0.5 * x * (1.0 + jnp.tanh(GELU_C0 * (x + GELU_C1 * x * x * x)))


def _gelu_and_grad(x):
    x2 = x * x
    t = jnp.tanh(GELU_C0 * x * (1.0 + GELU_C1 * x2))
    val = 0.5 * x * (1.0 + t)
    grad = 0.5 * (1.0 + t) + 0.5 * x * (1.0 - t * t) * (GELU_C0 * (1.0 + 3.0 * GELU_C1 * x2))
    return val, grad


def _rms_stats(x):
    return lax.rsqrt(jnp.mean(x * x, axis=-1, keepdims=True) + EPS)


def _rms_bwd(dy, x, r, g):
    w = dy * g
    return r * w - x * (r * r * r) * jnp.mean(w * x, axis=-1, keepdims=True)


def _t5_bucket():
    i = np.arange(CHUNK)[:, None]
    j = np.arange(2 * CHUNK)[None, :]
    rel = np.maximum(i + CHUNK - j, 0)
    n_exact = N_BUCKETS // 2
    relf = np.maximum(rel, n_exact).astype(np.float32)
    large = n_exact + (np.log(relf / np.float32(n_exact)) / np.float32(math.log(MAX_DISTANCE / n_exact))
                       * np.float32(N_BUCKETS - n_exact)).astype(np.int32)
    large = np.minimum(large, N_BUCKETS - 1)
    bucket = np.where(rel < n_exact, rel, large)
    in_window = (i + CHUNK - j >= 0) & (i + CHUNK - j < CHUNK)
    return bucket.astype(np.int32), in_window


def _split3(x):
    hi = x.astype(BF16)
    r1 = x - hi.astype(F32)
    mid = r1.astype(BF16)
    lo = (r1 - mid.astype(F32)).astype(BF16)
    return hi, mid, lo


HBM_SPEC = pl.BlockSpec(memory_space=pltpu.HBM)


def _mesh_pos():
    return lax.axis_index("x"), lax.axis_index("y"), lax.axis_index("c")


def _dev_index(px, py, pc):
    return 4 * px + 2 * py + pc


def _all_gather(shards, name):
    n = len(shards)

    def body(*refs):
        ins, outs = refs[:n], refs[n:2 * n]
        send_sems, recv_sems, local_sems = refs[2 * n:]
        x, y, c = _mesh_pos()
        me, sibling = (x, y, c), (x, y, 1 - c)
        chips = [(1 - x, y), (x, 1 - y), (1 - x, 1 - y)]

        def copy(a, k, block, to, src=None):
            dst = outs[a].at[_dev_index(*block)]
            return pltpu.make_async_remote_copy(
                src_ref=dst if src is None else src, dst_ref=dst,
                send_sem=send_sems.at[a * 7 + k], recv_sem=recv_sems.at[a * 7 + k],
                device_id=to, device_id_type=MESH)

        mine = [pltpu.make_async_copy(ins[a], outs[a].at[_dev_index(*me)], local_sems.at[a]) for a in range(n)]
        first = []
        for a in range(n):
            for j, chip in enumerate(chips):
                first.append(copy(a, 1 + j, me, (*chip, c), src=ins[a]))
            first.append(copy(a, 0, me, sibling, src=ins[a]))
        for cp in first:
            cp.start()
        for cp in mine:
            cp.start()
        passed = []
        for a in range(n):
            for j, chip in enumerate(chips):
                copy(a, 1 + j, (*chip, c), me).wait_recv()
                fwd = copy(a, 4 + j, (*chip, c), sibling)
                fwd.start()
                passed.append(fwd)
        for a in range(n):
            copy(a, 0, sibling, me).wait_recv()
            for j, chip in enumerate(chips):
                copy(a, 4 + j, (*chip, 1 - c), me).wait_recv()
        for cp in first + passed:
            cp.wait_send()
        for cp in mine:
            cp.wait()

    return pl.pallas_call(
        body, name=name,
        out_shape=[SDS((N_DEV,) + s.shape, s.dtype) for s in shards],
        in_specs=[HBM_SPEC] * n, out_specs=[HBM_SPEC] * n,
        scratch_shapes=[pltpu.SemaphoreType.DMA((7 * n,)), pltpu.SemaphoreType.DMA((7 * n,)),
                        pltpu.SemaphoreType.DMA((n,))],
    )(*shards)


SEM_SPEC = pl.BlockSpec(memory_space=pltpu.SEMAPHORE)
ANY_SPEC = pl.BlockSpec(memory_space=pl.ANY)
TOKEN_SPEC = pl.BlockSpec(memory_space=pltpu.VMEM)
TOKEN = SDS((8, LANE), F32)
SIDE_EFFECT = pltpu.SideEffectType.DATAFLOW_SIDE_EFFECTING


def _hbm(x):
    return pltpu.with_memory_space_constraint(x, pltpu.HBM)


def _after(x, token):
    return lax.optimization_barrier((x, token))[0]


def _split_start(bufs, copies_of, n_sems, name):
    n = len(bufs)

    def body(*refs):
        ins = refs[:n]
        send_sems, recv_sems = refs[n], refs[n + 1]
        token = refs[2 * n + 2]
        for src, dst, k, target in copies_of(ins):
            pltpu.make_async_remote_copy(src_ref=src, dst_ref=dst, send_sem=send_sems.at[k], recv_sem=recv_sems.at[k],
                                         device_id=target, device_id_type=MESH).start()
        token[...] = jnp.zeros_like(token)

    outs = pl.pallas_call(
        body, name=name,
        out_shape=[pltpu.SemaphoreType.DMA((n_sems,)), pltpu.SemaphoreType.DMA((n_sems,))]
        + [pltpu.HBM(b.shape, b.dtype) for b in bufs] + [TOKEN],
        in_specs=[HBM_SPEC] * n, out_specs=[SEM_SPEC, SEM_SPEC] + [HBM_SPEC] * n + [TOKEN_SPEC],
        input_output_aliases={a: 2 + a for a in range(n)},
        compiler_params=pltpu.CompilerParams(has_side_effects=SIDE_EFFECT),
    )(*[_hbm(b) for b in bufs])
    return outs[0], outs[1], list(outs[2:2 + n]), outs[2 + n]


def _split_wait(bufs, sem_sets, waits_of, after, name):
    n, ns = len(bufs), len(sem_sets)
    flat_sems = [s for pair in sem_sets for s in pair]

    def body(*refs):
        ins = refs[:n]
        sems = refs[n:n + 2 * ns]
        x, y, c = _mesh_pos()
        for kind, src, dst, send_sem, recv_sem in waits_of(ins, [(sems[2 * i], sems[2 * i + 1]) for i in range(ns)]):
            cp = pltpu.make_async_remote_copy(src_ref=src, dst_ref=dst, send_sem=send_sem, recv_sem=recv_sem,
                                              device_id=(x, y, c), device_id_type=MESH)
            if kind == "send":
                cp.wait_send()
            else:
                cp.wait_recv()

    outs = pl.pallas_call(
        body, name=name,
        out_shape=[pltpu.HBM(b.shape, b.dtype) for b in bufs],
        in_specs=[HBM_SPEC] * n + [SEM_SPEC] * (2 * ns) + [ANY_SPEC], out_specs=[HBM_SPEC] * n,
        input_output_aliases={a: a for a in range(n)},
        compiler_params=pltpu.CompilerParams(has_side_effects=SIDE_EFFECT),
    )(*bufs, *flat_sems, after)
    return list(outs)


def _gather_begin(shards, name):
    me = _dev_index(*_mesh_pos())
    lands = [lax.dynamic_update_index_in_dim(lax.empty((N_DEV,) + s.shape, s.dtype), s, me, 0) for s in shards]

    def copies_of(ins):
        x, y, c = _mesh_pos()
        targets = [(x, y, 1 - c), (1 - x, y, c), (x, 1 - y, c), (1 - x, 1 - y, c)]
        out = []
        for a, land in enumerate(ins):
            blk = land.at[_dev_index(x, y, c)]
            for k in (1, 2, 3, 0):
                out.append((blk, blk, 4 * a + k, targets[k]))
        return out

    send_sems, recv_sems, lands, token = _split_start(lands, copies_of, 4 * len(shards), name)
    return dict(lands=lands, sems=(send_sems, recv_sems), token=token, fwd={})


def _gather_pass_on(state, which, after, name):
    def arrivals(ins, sems):
        x, y, c = _mesh_pos()
        chips = [(1 - x, y), (x, 1 - y), (1 - x, 1 - y)]
        out = []
        for i, a in enumerate(which):
            for j, (px, py) in enumerate(chips):
                blk = ins[i].at[_dev_index(px, py, c)]
                out.append(("recv", blk, blk, sems[0][0].at[4 * a + 1 + j], sems[0][1].at[4 * a + 1 + j]))
        return out

    bufs = _split_wait([state["lands"][a] for a in which], [state["sems"]], arrivals, after, name + "_arrived")

    def copies_of(ins):
        x, y, c = _mesh_pos()
        chips = [(1 - x, y), (x, 1 - y), (1 - x, 1 - y)]
        out = []
        for i in range(len(which)):
            for j, (px, py) in enumerate(chips):
                blk = ins[i].at[_dev_index(px, py, c)]
                out.append((blk, blk, 3 * i + j, (x, y, 1 - c)))
        return out

    send_sems, recv_sems, bufs, token = _split_start(bufs, copies_of, 3 * len(which), name)
    for i, a in enumerate(which):
        state["lands"][a] = bufs[i]
    state["fwd"][tuple(which)] = (send_sems, recv_sems)
    return token


def _gather_end(state, which, after, name):
    def waits(ins, sems):
        x, y, c = _mesh_pos()
        chips = [(1 - x, y), (x, 1 - y), (1 - x, 1 - y)]
        (s_send, s_recv), (f_send, f_recv) = sems
        out = []
        for i, a in enumerate(which):
            mine = ins[i].at[_dev_index(x, y, c)]
            sib = ins[i].at[_dev_index(x, y, 1 - c)]
            out.append(("recv", sib, sib, s_send.at[4 * a], s_recv.at[4 * a]))
            for j, (px, py) in enumerate(chips):
                theirs = ins[i].at[_dev_index(px, py, 1 - c)]
                out.append(("recv", theirs, theirs, f_send.at[3 * i + j], f_recv.at[3 * i + j]))
            for k in range(4):
                out.append(("send", mine, mine, s_send.at[4 * a + k], s_recv.at[4 * a + k]))
            for j, (px, py) in enumerate(chips):
                passed = ins[i].at[_dev_index(px, py, c)]
                out.append(("send", passed, passed, f_send.at[3 * i + j], f_recv.at[3 * i + j]))
        return out

    bufs = _split_wait([state["lands"][a] for a in which], [state["sems"], state["fwd"][tuple(which)]], waits, after, name)
    for i, a in enumerate(which):
        state["lands"][a] = bufs[i]
    return bufs


def _sibling_exchange_begin(part, name):
    land = lax.empty((4,) + part.shape[1:], part.dtype)

    def copies_of(ins):
        x, y, c = _mesh_pos()
        return [(ins[0].at[2 * j + (1 - c)], ins[1].at[j], j, (x, y, 1 - c)) for j in range(4)]

    send_sems, recv_sems, bufs, token = _split_start([part, land], copies_of, 4, name)
    return dict(bufs=bufs, sems=(send_sems, recv_sems), token=token)


def _sibling_exchange_end(state, after, name):
    def waits(ins, sems):
        _, _, c = _mesh_pos()
        out = []
        for j in range(4):
            for kind in ("send", "recv"):
                out.append((kind, ins[0].at[2 * j + (1 - c)], ins[1].at[j], sems[0][0].at[j], sems[0][1].at[j]))
        return out

    return _split_wait(state["bufs"], [state["sems"]], waits, after, name)


def _chip_exchange_begin(csum, name):
    land = lax.empty((3,) + csum.shape[1:], csum.dtype)

    def copies_of(ins):
        x, y, c = _mesh_pos()
        chips = [(1 - x, y), (x, 1 - y), (1 - x, 1 - y)]
        return [(ins[0].at[2 * px + py], ins[1].at[r], r, (px, py, c)) for r, (px, py) in enumerate(chips)]

    send_sems, recv_sems, bufs, token = _split_start([csum, land], copies_of, 3, name)
    return dict(bufs=bufs, sems=(send_sems, recv_sems), token=token)


def _chip_exchange_end(state, after, name):
    def waits(ins, sems):
        x, y, _ = _mesh_pos()
        chips = [(1 - x, y), (x, 1 - y), (1 - x, 1 - y)]
        out = []
        for r, (px, py) in enumerate(chips):
            for kind in ("send", "recv"):
                out.append((kind, ins[0].at[2 * px + py], ins[1].at[r], sems[0][0].at[r], sems[0][1].at[r]))
        return out

    return _split_wait(state["bufs"], [state["sems"]], waits, after, name)


def _chip_sum(part, recv, name):
    _, R, C = part.shape
    tr = _tile(R, 512, 16)
    c_idx = lax.axis_index("c").astype(jnp.int32).reshape((1,))

    def body(c_ref, p_ref, r_ref, o_ref):
        o_ref[...] = (p_ref[...].astype(F32) + r_ref[...].astype(F32)).astype(o_ref.dtype)

    grid_spec = pltpu.PrefetchScalarGridSpec(
        num_scalar_prefetch=1, grid=(4, R // tr),
        in_specs=[pl.BlockSpec((None, tr, C), lambda j, i, c_ref: (2 * j + c_ref[0], i, 0)),
                  pl.BlockSpec((None, tr, C), lambda j, i, c_ref: (j, i, 0))],
        out_specs=pl.BlockSpec((None, tr, C), lambda j, i, c_ref: (j, i, 0)))
    return pl.pallas_call(body, name=name, grid_spec=grid_spec, out_shape=SDS((4, R, C), part.dtype),
                          compiler_params=_params(2))(c_idx, part, recv)


def _bias_fwd(table_t, onehot_t):
    H = table_t.shape[0]
    n = onehot_t.shape[1]

    def body(t_ref, oh_ref, o_ref):
        hi, mid, lo = _split3(t_ref[...])
        oh = oh_ref[...]
        o_ref[...] = _dot(hi, oh, NN) + _dot(mid, oh, NN) + _dot(lo, oh, NN)

    return pl.pallas_call(body, name="bias_fwd", out_shape=SDS((H, n), F32),
                          compiler_params=_params(0))(table_t, onehot_t)


def _inproj_fwd(x, g, w):
    T, D = x.shape
    P = w.shape[1]
    tm, tn = _tile(T, 512), _tile(P, 1792)

    def body(x_ref, g_ref, w_ref, proj_ref, n_ref, nbuf):
        @pl.when(pl.program_id(1) == 0)
        def _():
            xv = x_ref[...]
            n = (xv * _rms_stats(xv) * g_ref[...]).astype(BF16)
            nbuf[...] = n
            n_ref[...] = n

        proj_ref[...] = _dot(nbuf[...], w_ref[...], NN)

    return pl.pallas_call(
        body, name="inproj_fwd", grid=(T // tm, P // tn),
        in_specs=[pl.BlockSpec((tm, D), lambda i, j: (i, 0)), pl.BlockSpec((1, D), lambda i, j: (0, 0)),
                  pl.BlockSpec((D, tn), lambda i, j: (0, j))],
        out_specs=[pl.BlockSpec((tm, tn), lambda i, j: (i, j)), pl.BlockSpec((tm, D), lambda i, j: (i, 0))],
        out_shape=[SDS((T, P), F32), SDS((T, D), BF16)],
        scratch_shapes=[pltpu.VMEM((tm, D), BF16)], compiler_params=_params(2))(x, g, w)


def _layer_norm_group(vg, lg, lb):
    mu = jnp.mean(vg, axis=-1, keepdims=True)
    xc = vg - mu
    rstd = lax.rsqrt(jnp.mean(xc * xc, axis=-1, keepdims=True) + EPS)
    vhat = xc * rstd
    return vhat, rstd, vhat * lg + lb


def _gmlp_fwd(proj, lg, lb, w_s, bs_t, A):
    T = proj.shape[0]
    G = A // GROUP_DIM
    tm = _tile(T, 512)
    nc = tm // CHUNK

    def body(u_ref, v_ref, lg_ref, lb_ref, w_ref, bst_ref, a_ref):
        row = lax.broadcasted_iota(jnp.int32, (CHUNK, CHUNK), 0)
        col = lax.broadcasted_iota(jnp.int32, (CHUNK, CHUNK), 1)
        causal = row >= col
        for g in range(G):
            sl = slice(g * GROUP_DIM, (g + 1) * GROUP_DIM)
            _, _, vn = _layer_norm_group(_gelu(v_ref[:, sl]), lg_ref[:, sl], lb_ref[:, sl])
            vnb = vn.astype(BF16)
            wm = jnp.where(causal, w_ref[g], 0.0).astype(BF16)
            ug = _gelu(u_ref[:, sl])
            bcol = bst_ref[:, g:g + 1]
            for c in range(nc):
                rs = slice(c * CHUNK, (c + 1) * CHUNK)
                a_ref[rs, sl] = ug[rs] * (_dot(wm, vnb[rs], NN) + bcol)

    return pl.pallas_call(
        body, name="gmlp_fwd", grid=(T // tm,),
        in_specs=[pl.BlockSpec((tm, A), lambda i: (i, 0)), pl.BlockSpec((tm, A), lambda i: (i, 1)),
                  pl.BlockSpec((1, A), lambda i: (0, 0)), pl.BlockSpec((1, A), lambda i: (0, 0)),
                  pl.BlockSpec((G, CHUNK, CHUNK), lambda i: (0, 0, 0)), pl.BlockSpec((CHUNK, G), lambda i: (0, 0))],
        out_specs=pl.BlockSpec((tm, A), lambda i: (i, 0)),
        out_shape=SDS((T, A), F32), compiler_params=_params(1))(proj, proj, lg, lb, w_s, bs_t)


def _attn_masks(first_tile):
    ii = lax.broadcasted_iota(jnp.int32, (CHUNK, 2 * CHUNK), 0)
    jj = lax.broadcasted_iota(jnp.int32, (CHUNK, 2 * CHUNK), 1)
    in_window = (jj > ii) & (jj <= ii + CHUNK)
    first_mask = in_window & jnp.logical_or(jnp.logical_not(first_tile), jj >= CHUNK)
    return in_window, first_mask


def _attn_probs(qh, kb, bias_h, mask, sink):
    s = _dot(qh, kb, NT) * (HEAD_DIM ** -0.5) + bias_h
    s = jnp.where(mask, s, NEG)
    m = jnp.maximum(jnp.max(s, axis=-1, keepdims=True), sink)
    p = jnp.exp(s - m)
    e_sink = jnp.exp(sink - m)
    inv = 1.0 / (jnp.sum(p, axis=-1, keepdims=True) + e_sink)
    return p * inv, e_sink * inv


def _attn_specs(tq, A, B, reverse_tiles=None):
    nb = tq // CHUNK
    kcol = (2 * A + B) // LANE
    if reverse_tiles is None:
        tile = lambda i: i
    else:
        tile = lambda i: reverse_tiles - 1 - i
    prev = lambda i: jnp.maximum(tile(i) * nb - 1, 0)
    return [pl.BlockSpec((tq, B), lambda i: (tile(i), 2 * A // B)),
            pl.BlockSpec((tq, LANE), lambda i: (tile(i), kcol)),
            pl.BlockSpec((tq, LANE), lambda i: (tile(i), kcol + 1)),
            pl.BlockSpec((CHUNK, LANE), lambda i: (prev(i), kcol)),
            pl.BlockSpec((CHUNK, LANE), lambda i: (prev(i), kcol + 1))]


def _attn_fwd(proj, bias, sinks, A, B):
    T = proj.shape[0]
    H = B // HEAD_DIM
    qpk = H // KV_HEADS
    tq = _tile(T, 512)
    nb = tq // CHUNK

    def body(sink_ref, q_ref, k_ref, v_ref, kp_ref, vp_ref, bias_ref, o_ref):
        in_window, first_mask = _attn_masks(pl.program_id(0) == 0)
        for b in range(nb):
            rows = slice(b * CHUNK, (b + 1) * CHUNK)
            if b == 0:
                kprev, vprev, mask = kp_ref[...], vp_ref[...], first_mask
            else:
                prows = slice((b - 1) * CHUNK, b * CHUNK)
                kprev, vprev, mask = k_ref[prows, :], v_ref[prows, :], in_window
            kband = jnp.concatenate([kprev, k_ref[rows, :]], axis=0).astype(BF16)
            vband = jnp.concatenate([vprev, v_ref[rows, :]], axis=0).astype(BF16)
            for h in range(H):
                ks = slice((h // qpk) * HEAD_DIM, (h // qpk + 1) * HEAD_DIM)
                hs = slice(h * HEAD_DIM, (h + 1) * HEAD_DIM)
                pn, _ = _attn_probs(q_ref[rows, hs].astype(BF16), kband[:, ks], bias_ref[h], mask, sink_ref[h])
                o_ref[rows, hs] = _dot(pn.astype(BF16), vband[:, ks], NN)

    return pl.pallas_call(
        body, name="attn_fwd", grid=(T // tq,),
        in_specs=[pl.BlockSpec(memory_space=pltpu.SMEM)] + _attn_specs(tq, A, B)
        + [pl.BlockSpec((H, CHUNK, 2 * CHUNK), lambda i: (0, 0, 0))],
        out_specs=pl.BlockSpec((tq, B), lambda i: (i, 0)),
        out_shape=SDS((T, B), F32), compiler_params=_params(1))(sinks, proj, proj, proj, proj, proj, bias)


def _outproj_fwd(a, b, ga, gb, x, w):
    T, A = a.shape
    B = b.shape[1]
    D = x.shape[1]
    tm, tn = _tile(T, 512), _tile(D, 1024)

    def body(a_ref, b_ref, ga_ref, gb_ref, x_ref, w_ref, h_ref, mix_ref, mbuf):
        @pl.when(pl.program_id(1) == 0)
        def _():
            av, bv = a_ref[...], b_ref[...]
            mbuf[:, :A] = (av * _rms_stats(av) * ga_ref[...]).astype(BF16)
            mbuf[:, A:] = (bv * _rms_stats(bv) * gb_ref[...]).astype(BF16)
            mix_ref[...] = mbuf[...]

        h_ref[...] = x_ref[...] + _dot(mbuf[...], w_ref[...], NN)

    return pl.pallas_call(
        body, name="outproj_fwd", grid=(T // tm, D // tn),
        in_specs=[pl.BlockSpec((tm, A), lambda i, j: (i, 0)), pl.BlockSpec((tm, B), lambda i, j: (i, 0)),
                  pl.BlockSpec((1, A), lambda i, j: (0, 0)), pl.BlockSpec((1, B), lambda i, j: (0, 0)),
                  pl.BlockSpec((tm, tn), lambda i, j: (i, j)), pl.BlockSpec((A + B, tn), lambda i, j: (0, j))],
        out_specs=[pl.BlockSpec((tm, tn), lambda i, j: (i, j)), pl.BlockSpec((tm, A + B), lambda i, j: (i, 0))],
        out_shape=[SDS((T, D), F32), SDS((T, A + B), BF16)],
        scratch_shapes=[pltpu.VMEM((tm, A + B), BF16)], compiler_params=_params(2))(a, b, ga, gb, x, w)


def _ffn_up(h1, g, w_up):
    T, D = h1.shape
    Fb = w_up.shape[2]
    F = N_DEV * Fb
    tm, tf = _tile(T, 1024), _tile(Fb, 1024)
    per = Fb // tf

    def body(h_ref, g_ref, wu_ref, z_ref, n_ref, nbuf):
        @pl.when(pl.program_id(1) == 0)
        def _():
            hv = h_ref[...]
            n = (hv * _rms_stats(hv) * g_ref[...]).astype(BF16)
            nbuf[...] = n
            n_ref[...] = n

        z_ref[...] = jnp.maximum(_dot(nbuf[...], wu_ref[...], NN), 0.0).astype(BF16)

    return pl.pallas_call(
        body, name="ffn_up", grid=(T // tm, F // tf),
        in_specs=[pl.BlockSpec((tm, D), lambda i, j: (i, 0)), pl.BlockSpec((1, D), lambda i, j: (0, 0)),
                  pl.BlockSpec((None, D, tf), lambda i, j: (j // per, 0, j % per))],
        out_specs=[pl.BlockSpec((tm, tf), lambda i, j: (i, j)), pl.BlockSpec((tm, D), lambda i, j: (i, 0))],
        out_shape=[SDS((T, F), BF16), SDS((T, D), BF16)],
        scratch_shapes=[pltpu.VMEM((tm, D), BF16)], compiler_params=_params(2))(h1, g, w_up)


def _ffn_down(h1, z, w_down):
    T, D = h1.shape
    F = w_down.shape[0]
    tm, tk = _tile(T, 1024), _tile(F, 1024)
    nk = F // tk

    def body(h_ref, z_ref, wd_ref, h2_ref):
        k = pl.program_id(1)

        @pl.when(k == 0)
        def _():
            h2_ref[...] = h_ref[...]

        zf = z_ref[...].astype(F32)
        h2_ref[...] += _dot((zf * zf).astype(BF16), wd_ref[...], NN)

    return pl.pallas_call(
        body, name="ffn_down", grid=(T // tm, nk),
        in_specs=[pl.BlockSpec((tm, D), lambda i, k: (i, 0)), pl.BlockSpec((tm, tk), lambda i, k: (i, k)),
                  pl.BlockSpec((tk, D), lambda i, k: (k, 0))],
        out_specs=pl.BlockSpec((tm, D), lambda i, k: (i, 0)),
        out_shape=SDS((T, D), F32), compiler_params=_params(2))(h1, z, w_down)


def _final_loss(h2, g, target):
    T, D = h2.shape
    tm = _tile(T, 512)

    def body(h_ref, g_ref, t_ref, loss_ref, dg_ref, dh_ref, dhb_ref):
        @pl.when(pl.program_id(0) == 0)
        def _():
            loss_ref[...] = jnp.zeros_like(loss_ref)
            dg_ref[...] = jnp.zeros_like(dg_ref)

        hv, gv = h_ref[...], g_ref[...]
        r = _rms_stats(hv)
        hn = hv * r
        e = hn * gv - t_ref[...]
        loss_ref[...] += (0.5 / D) * jnp.sum(jnp.sum(e * e, axis=-1, keepdims=True), axis=0, keepdims=True)
        dy = e * (1.0 / D)
        dg_ref[...] += jnp.sum(dy * hn, axis=0, keepdims=True)
        dh = _rms_bwd(dy, hv, r, gv)
        dh_ref[...] = dh
        dhb_ref[...] = dh.astype(BF16)

    return pl.pallas_call(
        body, name="final_loss", grid=(T // tm,),
        in_specs=[pl.BlockSpec((tm, D), lambda i: (i, 0)), pl.BlockSpec((1, D), lambda i: (0, 0)),
                  pl.BlockSpec((tm, D), lambda i: (i, 0))],
        out_specs=[pl.BlockSpec((1, 1), lambda i: (0, 0)), pl.BlockSpec((1, D), lambda i: (0, 0)),
                   pl.BlockSpec((tm, D), lambda i: (i, 0)), pl.BlockSpec((tm, D), lambda i: (i, 0))],
        out_shape=[SDS((1, 1), F32), SDS((1, D), F32), SDS((T, D), F32), SDS((T, D), BF16)],
        compiler_params=_params(1))(h2, g, target)


def _ffn_bwd(dh2, dh2b, z, h1, g, w_up, w_down):
    T, D = h1.shape
    Fb = w_up.shape[2]
    F = N_DEV * Fb
    tm, tf = _tile(T, 512), _tile(Fb, 512)
    per = Fb // tf
    nj = F // tf

    def body(dh_ref, dhb_ref, z_ref, h_ref, g_ref, wu_ref, wd_ref, dzp_ref, dh1_ref, dh1b_ref, dg_ref, acc):
        i, j = pl.program_id(0), pl.program_id(1)

        @pl.when(j == 0)
        def _():
            acc[...] = jnp.zeros_like(acc)

        @pl.when((i == 0) & (j == 0))
        def _():
            dg_ref[...] = jnp.zeros_like(dg_ref)

        dzz = _dot(dhb_ref[...], wd_ref[...], NT)
        dzp = (dzz * (2.0 * z_ref[...].astype(F32))).astype(BF16)
        dzp_ref[...] = dzp
        acc[...] += _dot(dzp, wu_ref[...], NT)

        @pl.when(j == nj - 1)
        def _():
            hv, gv, dn = h_ref[...], g_ref[...], acc[...]
            r = _rms_stats(hv)
            dg_ref[...] += jnp.sum(dn * (hv * r), axis=0, keepdims=True)
            dh1 = dh_ref[...] + _rms_bwd(dn, hv, r, gv)
            dh1_ref[...] = dh1
            dh1b_ref[...] = dh1.astype(BF16)

    return pl.pallas_call(
        body, name="ffn_bwd", grid=(T // tm, nj),
        in_specs=[pl.BlockSpec((tm, D), lambda i, j: (i, 0)), pl.BlockSpec((tm, D), lambda i, j: (i, 0)),
                  pl.BlockSpec((tm, tf), lambda i, j: (i, j)), pl.BlockSpec((tm, D), lambda i, j: (i, 0)),
                  pl.BlockSpec((1, D), lambda i, j: (0, 0)),
                  pl.BlockSpec((None, D, tf), lambda i, j: (j // per, 0, j % per)),
                  pl.BlockSpec((None, tf, D), lambda i, j: (j // per, j % per, 0))],
        out_specs=[pl.BlockSpec((tm, tf), lambda i, j: (i, j)), pl.BlockSpec((tm, D), lambda i, j: (i, 0)),
                   pl.BlockSpec((tm, D), lambda i, j: (i, 0)), pl.BlockSpec((1, D), lambda i, j: (0, 0))],
        out_shape=[SDS((T, F), BF16), SDS((T, D), F32), SDS((T, D), BF16), SDS((1, D), F32)],
        scratch_shapes=[pltpu.VMEM((tm, D), F32)], compiler_params=_params(2))(dh2, dh2b, z, h1, g, w_up, w_down)


def _matmul_tn(a, b, name, square_a=False, col_blocks=None):
    T, K = a.shape
    N = b.shape[1]
    tt, tk = _tile(T, 1024), _tile(K, 1024)
    tn = _tile(N if col_blocks is None else N // col_blocks, 1792)
    nt = T // tt

    def body(a_ref, b_ref, o_ref, acc):
        t = pl.program_id(2)

        @pl.when(t == 0)
        def _():
            acc[...] = jnp.zeros_like(acc)

        av = a_ref[...]
        if square_a:
            af = av.astype(F32)
            av = (af * af).astype(BF16)
        acc[...] += _dot(av, b_ref[...], TN)

        @pl.when(t == nt - 1)
        def _():
            o_ref[...] = acc[...].astype(o_ref.dtype)

    if col_blocks is None:
        out_shape = SDS((K, N), BF16)
        out_spec = pl.BlockSpec((tk, tn), lambda i, j, t: (i, j))
    else:
        per = (N // col_blocks) // tn
        out_shape = SDS((col_blocks, K, N // col_blocks), BF16)
        out_spec = pl.BlockSpec((None, tk, tn), lambda i, j, t: (j // per, i, j % per))
    return pl.pallas_call(
        body, name=name, grid=(K // tk, N // tn, nt),
        in_specs=[pl.BlockSpec((tt, tk), lambda i, j, t: (t, i)), pl.BlockSpec((tt, tn), lambda i, j, t: (t, j))],
        out_specs=out_spec, out_shape=out_shape,
        scratch_shapes=[pltpu.VMEM((tk, tn), F32)], compiler_params=_params(3))(a, b)


def _outproj_bwd(dh1b, w, a, b, ga, gb):
    T, D = dh1b.shape
    A, B = a.shape[1], b.shape[1]
    tm = _tile(T, 512)

    def body(dh_ref, w_ref, a_ref, b_ref, ga_ref, gb_ref, da_ref, db_ref, dga_ref, dgb_ref):
        @pl.when(pl.program_id(0) == 0)
        def _():
            dga_ref[...] = jnp.zeros_like(dga_ref)
            dgb_ref[...] = jnp.zeros_like(dgb_ref)

        dmix = _dot(dh_ref[...], w_ref[...], NT)
        for src_ref, g_ref, dx_ref, dg_ref, dn in ((a_ref, ga_ref, da_ref, dga_ref, dmix[:, :A]),
                                                   (b_ref, gb_ref, db_ref, dgb_ref, dmix[:, A:])):
            xv = src_ref[...]
            r = _rms_stats(xv)
            dg_ref[...] += jnp.sum(dn * (xv * r), axis=0, keepdims=True)
            dx_ref[...] = _rms_bwd(dn, xv, r, g_ref[...])

    return pl.pallas_call(
        body, name="outproj_bwd", grid=(T // tm,),
        in_specs=[pl.BlockSpec((tm, D), lambda i: (i, 0)), pl.BlockSpec((A + B, D), lambda i: (0, 0)),
                  pl.BlockSpec((tm, A), lambda i: (i, 0)), pl.BlockSpec((tm, B), lambda i: (i, 0)),
                  pl.BlockSpec((1, A), lambda i: (0, 0)), pl.BlockSpec((1, B), lambda i: (0, 0))],
        out_specs=[pl.BlockSpec((tm, A), lambda i: (i, 0)), pl.BlockSpec((tm, B), lambda i: (i, 0)),
                   pl.BlockSpec((1, A), lambda i: (0, 0)), pl.BlockSpec((1, B), lambda i: (0, 0))],
        out_shape=[SDS((T, A), F32), SDS((T, B), F32), SDS((1, A), F32), SDS((1, B), F32)],
        compiler_params=_params(1))(dh1b, w, a, b, ga, gb)


def _gmlp_bwd(proj, da, lg, lb, w_s, w_st, bs_t, A):
    T = proj.shape[0]
    G = A // GROUP_DIM
    tm = _tile(T, 512)
    nc = tm // CHUNK

    def body(u_ref, v_ref, da_ref, lg_ref, lb_ref, w_ref, wt_ref, bst_ref, duv_ref, dlg_ref, dlb_ref, dw_ref, dbs_ref):
        @pl.when(pl.program_id(0) == 0)
        def _():
            dlg_ref[...] = jnp.zeros_like(dlg_ref)
            dlb_ref[...] = jnp.zeros_like(dlb_ref)
            dw_ref[...] = jnp.zeros_like(dw_ref)
            dbs_ref[...] = jnp.zeros_like(dbs_ref)

        row = lax.broadcasted_iota(jnp.int32, (CHUNK, CHUNK), 0)
        col = lax.broadcasted_iota(jnp.int32, (CHUNK, CHUNK), 1)
        lower = row >= col
        upper = row <= col
        for g in range(G):
            sl = slice(g * GROUP_DIM, (g + 1) * GROUP_DIM)
            lgv = lg_ref[:, sl]
            vg, vg_grad = _gelu_and_grad(v_ref[:, sl])
            vhat, rstd, vn = _layer_norm_group(vg, lgv, lb_ref[:, sl])
            vnb = vn.astype(BF16)
            ug, ug_grad = _gelu_and_grad(u_ref[:, sl])
            dav = da_ref[:, sl]
            wm = jnp.where(lower, w_ref[g], 0.0).astype(BF16)
            wmt = jnp.where(upper, wt_ref[g], 0.0).astype(BF16)
            bcol = bst_ref[:, g:g + 1]
            dw_acc = jnp.zeros((CHUNK, CHUNK), F32)
            dbs_acc = jnp.zeros((CHUNK, 1), F32)
            dvn_parts = []
            dug_parts = []
            for c in range(nc):
                rs = slice(c * CHUNK, (c + 1) * CHUNK)
                mixed = _dot(wm, vnb[rs], NN) + bcol
                dug_parts.append(dav[rs] * mixed)
                dmix = dav[rs] * ug[rs]
                dbs_acc = dbs_acc + jnp.sum(dmix, axis=-1, keepdims=True)
                dmixb = dmix.astype(BF16)
                dw_acc = dw_acc + _dot(dmixb, vnb[rs], NT)
                dvn_parts.append(_dot(wmt, dmixb, NN))
            dvn = jnp.concatenate(dvn_parts, axis=0)
            dug = jnp.concatenate(dug_parts, axis=0)
            dw_ref[g] += jnp.where(lower, dw_acc, 0.0)
            dbs_ref[:, g:g + 1] += dbs_acc
            dlg_ref[:, sl] += jnp.sum(dvn * vhat, axis=0, keepdims=True)
            dlb_ref[:, sl] += jnp.sum(dvn, axis=0, keepdims=True)
            dvhat = dvn * lgv
            dvg = rstd * (dvhat - jnp.mean(dvhat, axis=-1, keepdims=True)
                          - vhat * jnp.mean(dvhat * vhat, axis=-1, keepdims=True))
            duv_ref[:, sl] = (dug * ug_grad).astype(BF16)
            duv_ref[:, A + g * GROUP_DIM:A + (g + 1) * GROUP_DIM] = (dvg * vg_grad).astype(BF16)

    return pl.pallas_call(
        body, name="gmlp_bwd", grid=(T // tm,),
        in_specs=[pl.BlockSpec((tm, A), lambda i: (i, 0)), pl.BlockSpec((tm, A), lambda i: (i, 1)),
                  pl.BlockSpec((tm, A), lambda i: (i, 0)),
                  pl.BlockSpec((1, A), lambda i: (0, 0)), pl.BlockSpec((1, A), lambda i: (0, 0)),
                  pl.BlockSpec((G, CHUNK, CHUNK), lambda i: (0, 0, 0)),
                  pl.BlockSpec((G, CHUNK, CHUNK), lambda i: (0, 0, 0)), pl.BlockSpec((CHUNK, G), lambda i: (0, 0))],
        out_specs=[pl.BlockSpec((tm, 2 * A), lambda i: (i, 0)),
                   pl.BlockSpec((1, A), lambda i: (0, 0)), pl.BlockSpec((1, A), lambda i: (0, 0)),
                   pl.BlockSpec((G, CHUNK, CHUNK), lambda i: (0, 0, 0)), pl.BlockSpec((CHUNK, G), lambda i: (0, 0))],
        out_shape=[SDS((T, 2 * A), BF16), SDS((1, A), F32), SDS((1, A), F32),
                   SDS((G, CHUNK, CHUNK), F32), SDS((CHUNK, G), F32)],
        compiler_params=_params(1))(proj, proj, da, lg, lb, w_s, w_st, bs_t)


def _attn_bwd(proj, o, do, duv, bias, sinks, A, B):
    T, P = proj.shape
    H = B // HEAD_DIM
    qpk = H // KV_HEADS
    tq = _tile(T, 512)
    nb = tq // CHUNK
    n_tiles = T // tq
    scale = HEAD_DIM ** -0.5
    rev = lambda i: n_tiles - 1 - i

    def body(sink_ref, q_ref, k_ref, v_ref, kp_ref, vp_ref, o_ref, do_ref, duv_ref, bias_ref,
             dproj_ref, dbias_ref, dsink_ref, carry, dkv, sacc):
        step = pl.program_id(0)

        @pl.when(step == 0)
        def _():
            carry[...] = jnp.zeros_like(carry)
            sacc[...] = jnp.zeros_like(sacc)
            dbias_ref[...] = jnp.zeros_like(dbias_ref)

        in_window, first_mask = _attn_masks(step == n_tiles - 1)
        dproj_ref[:, :2 * A] = duv_ref[...]
        dkv[...] = jnp.zeros_like(dkv)
        for b in range(nb):
            rows = slice(b * CHUNK, (b + 1) * CHUNK)
            band = slice(b * CHUNK, (b + 2) * CHUNK)
            if b == 0:
                kprev, vprev, mask = kp_ref[...], vp_ref[...], first_mask
            else:
                prows = slice((b - 1) * CHUNK, b * CHUNK)
                kprev, vprev, mask = k_ref[prows, :], v_ref[prows, :], in_window
            kband = jnp.concatenate([kprev, k_ref[rows, :]], axis=0).astype(BF16)
            vband = jnp.concatenate([vprev, v_ref[rows, :]], axis=0).astype(BF16)
            for kv in range(KV_HEADS):
                ks = slice(kv * HEAD_DIM, (kv + 1) * HEAD_DIM)
                kb, vb = kband[:, ks], vband[:, ks]
                dk_acc = jnp.zeros((2 * CHUNK, HEAD_DIM), F32)
                dv_acc = jnp.zeros((2 * CHUNK, HEAD_DIM), F32)
                for h in range(kv * qpk, (kv + 1) * qpk):
                    hs = slice(h * HEAD_DIM, (h + 1) * HEAD_DIM)
                    qh = q_ref[rows, hs].astype(BF16)
                    pn, p_sink = _attn_probs(qh, kb, bias_ref[h], mask, sink_ref[h])
                    doh = do_ref[rows, hs]
                    delta = jnp.sum(doh * o_ref[rows, hs], axis=-1, keepdims=True)
                    dohb = doh.astype(BF16)
                    ds = pn * (_dot(dohb, vb, NT) - delta)
                    dbias_ref[h] += ds
                    sacc[:, h:h + 1] += -(p_sink * delta)
                    dsb = ds.astype(BF16)
                    dproj_ref[rows, 2 * A + h * HEAD_DIM:2 * A + (h + 1) * HEAD_DIM] = (
                        _dot(dsb, kb, NN) * scale).astype(BF16)
                    dk_acc = dk_acc + _dot(dsb, qh, TN)
                    dv_acc = dv_acc + _dot(pn.astype(BF16), dohb, TN)
                dkv[band, ks] += dk_acc * scale
                dkv[band, LANE + kv * HEAD_DIM:LANE + (kv + 1) * HEAD_DIM] += dv_acc
        last = slice(tq, tq + CHUNK)
        dkv[last, :] += carry[...]
        dproj_ref[:, 2 * A + B:] = dkv[CHUNK:, :].astype(BF16)
        carry[...] = dkv[:CHUNK, :]

        @pl.when(step == n_tiles - 1)
        def _():
            dsink_ref[...] = jnp.sum(sacc[...], axis=0, keepdims=True)

    specs = _attn_specs(tq, A, B, reverse_tiles=n_tiles)
    return pl.pallas_call(
        body, name="attn_bwd", grid=(n_tiles,),
        in_specs=[pl.BlockSpec(memory_space=pltpu.SMEM)] + specs
        + [pl.BlockSpec((tq, B), lambda i: (rev(i), 0)), pl.BlockSpec((tq, B), lambda i: (rev(i), 0)),
           pl.BlockSpec((tq, 2 * A), lambda i: (rev(i), 0)),
           pl.BlockSpec((H, CHUNK, 2 * CHUNK), lambda i: (0, 0, 0))],
        out_specs=[pl.BlockSpec((tq, P), lambda i: (rev(i), 0)),
                   pl.BlockSpec((H, CHUNK, 2 * CHUNK), lambda i: (0, 0, 0)), pl.BlockSpec((1, H), lambda i: (0, 0))],
        out_shape=[SDS((T, P), BF16), SDS((H, CHUNK, 2 * CHUNK), F32), SDS((1, H), F32)],
        scratch_shapes=[pltpu.VMEM((CHUNK, 2 * LANE), F32), pltpu.VMEM((tq + CHUNK, 2 * LANE), F32),
                        pltpu.VMEM((CHUNK, H), F32)],
        compiler_params=_params(1))(sinks, proj, proj, proj, proj, proj, o, do, duv, bias)


def _bias_bwd(dbias, onehot):
    H = dbias.shape[0]
    nbk = onehot.shape[1]

    def body(d_ref, oh_ref, o_ref):
        hi, mid, lo = _split3(d_ref[...])
        oh = oh_ref[...]
        o_ref[...] = _dot(hi, oh, NN) + _dot(mid, oh, NN) + _dot(lo, oh, NN)

    return pl.pallas_call(body, name="bias_bwd", out_shape=SDS((H, nbk), F32),
                          compiler_params=_params(0))(dbias, onehot)


def _inproj_bwd(dproj, w, x, dh1, g):
    T, P = dproj.shape
    D = x.shape[1]
    tm, tn = _tile(T, 512), _tile(P, 1792)
    nj = P // tn

    def body(dp_ref, w_ref, x_ref, dh_ref, g_ref, dx_ref, dg_ref, acc):
        i, j = pl.program_id(0), pl.program_id(1)

        @pl.when(j == 0)
        def _():
            acc[...] = jnp.zeros_like(acc)

        @pl.when((i == 0) & (j == 0))
        def _():
            dg_ref[...] = jnp.zeros_like(dg_ref)

        acc[...] += _dot(dp_ref[...], w_ref[...], NT)

        @pl.when(j == nj - 1)
        def _():
            xv, dn = x_ref[...], acc[...]
            r = _rms_stats(xv)
            dg_ref[...] += jnp.sum(dn * (xv * r), axis=0, keepdims=True)
            dx_ref[...] = dh_ref[...] + _rms_bwd(dn, xv, r, g_ref[...])

    return pl.pallas_call(
        body, name="inproj_bwd", grid=(T // tm, nj),
        in_specs=[pl.BlockSpec((tm, tn), lambda i, j: (i, j)), pl.BlockSpec((D, tn), lambda i, j: (0, j)),
                  pl.BlockSpec((tm, D), lambda i, j: (i, 0)), pl.BlockSpec((tm, D), lambda i, j: (i, 0)),
                  pl.BlockSpec((1, D), lambda i, j: (0, 0))],
        out_specs=[pl.BlockSpec((tm, D), lambda i, j: (i, 0)), pl.BlockSpec((1, D), lambda i, j: (0, 0))],
        out_shape=[SDS((T, D), F32), SDS((1, D), F32)],
        scratch_shapes=[pltpu.VMEM((tm, D), F32)], compiler_params=_params(2))(dproj, w, x, dh1, g)


def _adamw(w, g, m, v):
    m = ADAM_B1 * m + (1.0 - ADAM_B1) * g
    v = ADAM_B2 * v + (1.0 - ADAM_B2) * (g * g)
    m_hat = m / (1.0 - ADAM_B1 ** ADAM_STEP)
    v_hat = v / (1.0 - ADAM_B2 ** ADAM_STEP)
    delta = -ADAM_LR * (m_hat / (jnp.sqrt(v_hat) + ADAM_EPS) + ADAM_WD * w)
    return delta, m, v


def _adam_sharded(csum, recv, w, m, v, name):
    R, C = w.shape
    tr = _tile(R, 256, 16)
    own = (2 * lax.axis_index("x") + lax.axis_index("y")).astype(jnp.int32).reshape((1,))

    def body(own_idx, own_ref, recv_ref, w_ref, m_ref, v_ref, g_ref, d_ref, nm_ref, nv_ref):
        g = own_ref[...].astype(F32)
        for r in range(3):
            g = g + recv_ref[r].astype(F32)
        delta, nm, nv = _adamw(w_ref[...], g, m_ref[...], v_ref[...])
        g_ref[...] = g
        d_ref[...] = delta
        nm_ref[...] = nm
        nv_ref[...] = nv

    blk = pl.BlockSpec((tr, C), lambda i, own_idx: (i, 0))
    grid_spec = pltpu.PrefetchScalarGridSpec(
        num_scalar_prefetch=1, grid=(R // tr,),
        in_specs=[pl.BlockSpec((None, tr, C), lambda i, own_idx: (own_idx[0], i, 0)),
                  pl.BlockSpec((3, tr, C), lambda i, own_idx: (0, i, 0)), blk, blk, blk],
        out_specs=[blk] * 4)
    return pl.pallas_call(body, name=name, grid_spec=grid_spec, out_shape=[SDS((R, C), F32)] * 4,
                          compiler_params=_params(1))(own, csum, recv, w, m, v)


def _adam_small(gathered, w, m, v):
    R = w.shape[0]

    def body(p_ref, w_ref, m_ref, v_ref, g_ref, d_ref, nm_ref, nv_ref):
        g = p_ref[0]
        for d in range(1, N_DEV):
            g = g + p_ref[d]
        delta, nm, nv = _adamw(w_ref[...], g, m_ref[...], v_ref[...])
        g_ref[...] = g
        d_ref[...] = delta
        nm_ref[...] = nm
        nv_ref[...] = nv

    return pl.pallas_call(body, name="adam_small", out_shape=[SDS((R, LANE), F32)] * 4,
                          compiler_params=_params(0))(gathered, w, m, v)


def _pack(arrays):
    tile = 8 * LANE
    pieces = []
    for a in arrays:
        flat = a.reshape(-1).astype(F32)
        pieces.append(jnp.pad(flat, (0, (-flat.size) % tile)))
    return jnp.concatenate(pieces).reshape(-1, LANE)


def _unpack(packed, shapes):
    tile = 8 * LANE
    flat = packed.reshape(-1)
    out, off = [], 0
    for s in shapes:
        size = int(np.prod(s))
        out.append(flat[off:off + size].reshape(s))
        off += size + (-size) % tile
    return out


def kernel(x, rel_bias_table, mix_norm_g, w_in, gate_norm_g, gate_norm_b, w_spatial, b_spatial, attn_sinks, out_norm_a_g, out_norm_b_g, w_out, ffn_norm_g, w_up, w_down, final_norm_g, loss_target, m_rel_bias_table, m_mix_norm_g, m_w_in, m_gate_norm_g, m_gate_norm_b, m_w_spatial, m_b_spatial, m_attn_sinks, m_out_norm_a_g, m_out_norm_b_g, m_w_out, m_ffn_norm_g, m_w_up, m_w_down, m_final_norm_g, v_rel_bias_table, v_mix_norm_g, v_w_in, v_gate_norm_g, v_gate_norm_b, v_w_spatial, v_b_spatial, v_attn_sinks, v_out_norm_a_g, v_out_norm_b_g, v_w_out, v_ffn_norm_g, v_w_up, v_w_down, v_final_norm_g):
    T, D = x.shape[1], x.shape[2]
    A = D // 2
    B = D // 2
    G = A // GROUP_DIM
    H = B // HEAD_DIM
    P = 2 * A + B + 2 * KV_HEADS * HEAD_DIM
    Pb = w_in.shape[2]
    xs = x.reshape(T, D)
    target = loss_target.reshape(T, D)

    shards = [w_in[0].astype(BF16), w_out[0].astype(BF16), w_up[0].astype(BF16), w_down[0].astype(BF16)]
    gather = _gather_begin(shards, "gather_start")
    tok = _gather_pass_on(gather, [0], gather["token"], "gather_in_pass")
    (win_g,) = _gather_end(gather, [0], tok, "gather_in_end")
    win_full = jnp.transpose(win_g, (1, 0, 2)).reshape(D, P)

    g1, g2, g3 = mix_norm_g.reshape(1, D), ffn_norm_g.reshape(1, D), final_norm_g.reshape(1, D)
    lg, lb = gate_norm_g.reshape(1, A), gate_norm_b.reshape(1, A)
    ws = w_spatial[0]
    ws_t = jnp.swapaxes(ws, 1, 2)
    bs_t = jnp.transpose(b_spatial[0])
    ga, gb = out_norm_a_g.reshape(1, A), out_norm_b_g.reshape(1, B)
    sinks = attn_sinks.reshape(H)
    bucket, in_window = _t5_bucket()
    onehot_np = ((bucket.reshape(-1, 1) == np.arange(N_BUCKETS)[None, :]) & in_window.reshape(-1, 1))
    onehot = jnp.asarray(onehot_np.astype(np.float32)).astype(BF16)

    bias = _bias_fwd(jnp.transpose(rel_bias_table), jnp.transpose(onehot)).reshape(H, CHUNK, 2 * CHUNK)
    proj, n1 = _inproj_fwd(xs, g1, win_full)
    tok = _gather_pass_on(gather, [1], n1, "gather_out_pass")
    a_out = _gmlp_fwd(_after(proj, tok), lg, lb, ws, bs_t, A)
    b_out = _attn_fwd(proj, bias, sinks, A, B)
    tok = _gather_pass_on(gather, [2], b_out, "gather_up_pass")
    (wout_g,) = _gather_end(gather, [1], tok, "gather_out_end")
    wout_full = wout_g.reshape(A + B, D)
    h1, mixed = _outproj_fwd(a_out, b_out, ga, gb, xs, wout_full)
    tok = _gather_pass_on(gather, [3], h1, "gather_down_pass")
    (wup_g,) = _gather_end(gather, [2], tok, "gather_up_end")
    z, n2 = _ffn_up(h1, g2, wup_g)
    (wdown_g,) = _gather_end(gather, [3], z, "gather_down_end")
    h2 = _ffn_down(h1, z, wdown_g.reshape(-1, D))
    loss_part, dg3, dh2, dh2b = _final_loss(h2, g3, target)

    def reduce_to_chip(state, after, name):
        part, received = _sibling_exchange_end(state, after, name + "_sib_end")
        return _chip_exchange_begin(_chip_sum(part, received, name + "_chip_sum"), name + "_chip")

    dzp, dh1, dh1b, dg2 = _ffn_bwd(dh2, dh2b, z, h1, g2, wup_g, wdown_g)
    dwdown = _matmul_tn(z, dh2b, "grad_w_down", square_a=True).reshape(wdown_g.shape)
    sib_down = _sibling_exchange_begin(dwdown, "rs_down_sib")
    dwup = _matmul_tn(_after(n2, sib_down["token"]), dzp, "grad_w_up", col_blocks=N_DEV)
    chip_down = reduce_to_chip(sib_down, dwup, "rs_down")
    sib_up = _sibling_exchange_begin(_after(dwup, chip_down["token"]), "rs_up_sib")
    da, db, dga, dgb = _outproj_bwd(_after(dh1b, sib_up["token"]), wout_full, a_out, b_out, ga, gb)
    chip_up = reduce_to_chip(sib_up, da, "rs_up")
    dwout = _matmul_tn(_after(mixed, chip_up["token"]), dh1b, "grad_w_out").reshape(wout_g.shape)
    sib_out = _sibling_exchange_begin(dwout, "rs_out_sib")
    duv, dlg, dlb, dws, dbs_t = _gmlp_bwd(_after(proj, sib_out["token"]), da, lg, lb, ws, ws_t, bs_t, A)
    dproj, dbias, dsinks = _attn_bwd(proj, b_out, db, duv, bias, sinks, A, B)
    chip_out = reduce_to_chip(sib_out, dproj, "rs_out")
    dtable_t = _bias_bwd(dbias.reshape(H, -1), onehot)
    dwin = _matmul_tn(_after(n1, chip_out["token"]), dproj, "grad_w_in")
    dwin = jnp.transpose(dwin.reshape(D, N_DEV, Pb), (1, 0, 2))
    sib_in = _sibling_exchange_begin(dwin, "rs_in_sib")
    grad_x, dg1 = _inproj_bwd(_after(dproj, sib_in["token"]), win_full, xs, dh1, g1)
    chip_in = reduce_to_chip(sib_in, grad_x, "rs_in")

    big = [None] * 4
    after = chip_in["token"]
    for k, state, (w, m, v) in ((3, chip_down, (w_down, m_w_down, v_w_down)), (2, chip_up, (w_up, m_w_up, v_w_up)),
                                (1, chip_out, (w_out, m_w_out, v_w_out)), (0, chip_in, (w_in, m_w_in, v_w_in))):
        csum, received = _chip_exchange_end(state, after, "rs_%d_end" % k)
        outs = _adam_sharded(csum, received, w[0], m[0], v[0], "adam_%d" % k)
        after = outs[0]
        big[k] = [o.reshape(w.shape) for o in outs]

    small_w = [rel_bias_table, mix_norm_g, gate_norm_g, gate_norm_b, w_spatial, b_spatial, attn_sinks,
               out_norm_a_g, out_norm_b_g, ffn_norm_g, final_norm_g]
    small_m = [m_rel_bias_table, m_mix_norm_g, m_gate_norm_g, m_gate_norm_b, m_w_spatial, m_b_spatial, m_attn_sinks,
               m_out_norm_a_g, m_out_norm_b_g, m_ffn_norm_g, m_final_norm_g]
    small_v = [v_rel_bias_table, v_mix_norm_g, v_gate_norm_g, v_gate_norm_b, v_w_spatial, v_b_spatial, v_attn_sinks,
               v_out_norm_a_g, v_out_norm_b_g, v_ffn_norm_g, v_final_norm_g]
    small_g = [jnp.transpose(dtable_t), dg1, dlg, dlb, dws, jnp.transpose(dbs_t), dsinks, dga, dgb, dg2, dg3]
    shapes = [w.shape for w in small_w]
    (gathered,) = _all_gather([_pack(small_g)], "gather_small_grads")
    sg, sd, sm, sv = [_unpack(o, shapes) for o in _adam_small(gathered, _pack(small_w), _pack(small_m), _pack(small_v))]

    loss = lax.psum(loss_part[0, 0], ("x", "y", "c"))

    order = ["s0", "s1", "b0", "s2", "s3", "s4", "s5", "s6", "s7", "s8", "b1", "s9", "b2", "b3", "s10"]

    def group(idx):
        small = (sg, sd, sm, sv)[idx]
        return [small[int(t[1:])] if t[0] == "s" else big[int(t[1:])][idx] for t in order]

    return (loss, grad_x.reshape(x.shape), *group(0), *group(1), *group(2), *group(3))
```

```python
import functools
import math

import numpy as np
import jax
import jax.numpy as jnp
from jax import lax
from jax.experimental import pallas as pl
from jax.experimental.pallas import tpu as pltpu

F32 = jnp.float32
BF16 = jnp.bfloat16
SDS = jax.ShapeDtypeStruct
MESH = pl.DeviceIdType.MESH

N_DEV = 8
EPS = 1e-5
NEG = -1e30
CHUNK = 128
GROUP_DIM = 128
HEAD_DIM = 64
KV_HEADS = 2
N_BUCKETS = 32
MAX_DISTANCE = 128
ADAM_LR, ADAM_B1, ADAM_B2, ADAM_EPS, ADAM_WD, ADAM_STEP = 0.001, 0.9, 0.999, 1e-08, 0.01, 10
GELU_C0 = math.sqrt(2.0 / math.pi)
GELU_C1 = 0.044715

V7X_VMEM_BYTES = 64 * 1024 * 1024
VMEM_LIMIT = V7X_VMEM_BYTES - 8 * 1024 * 1024
LANE = 128

NN = ((1,), (0,))
NT = ((1,), (1,))
TN = ((0,), (0,))


def _dot(a, b, dims):
    return lax.dot_general(a, b, (dims, ((), ())), preferred_element_type=F32)


def _tile(n, pref, unit=LANE):
    best = None
    for t in range(unit, min(n, pref) + 1, unit):
        if n % t == 0:
            best = t
    return n if best is None else best


def _params(n_grid):
    return pltpu.CompilerParams(dimension_semantics=("arbitrary",) * n_grid, vmem_limit_bytes=VMEM_LIMIT)


def _gelu(x):
    return 0.5 * x * (1.0 + jnp.tanh(GELU_C0 * (x + GELU_C1 * x * x * x)))


def _gelu_and_grad(x):
    x2 = x * x
    t = jnp.tanh(GELU_C0 * x * (1.0 + GELU_C1 * x2))
    val = 0.5 * x * (1.0 + t)
    grad = 0.5 * (1.0 + t) + 0.5 * x * (1.0 - t * t) * (GELU_C0 * (1.0 + 3.0 * GELU_C1 * x2))
    return val, grad


def _rms_stats(x):
    return lax.rsqrt(jnp.mean(x * x, axis=-1, keepdims=True) + EPS)


def _rms_bwd(dy, x, r, g):
    w = dy * g
    return r * w - x * (r * r * r) * jnp.mean(w * x, axis=-1, keepdims=True)


def _t5_bucket():
    i = np.arange(CHUNK)[:, None]
    j = np.arange(2 * CHUNK)[None, :]
    rel = np.maximum(i + CHUNK - j, 0)
    n_exact = N_BUCKETS // 2
    relf = np.maximum(rel, n_exact).astype(np.float32)
    large = n_exact + (np.log(relf / np.float32(n_exact)) / np.float32(math.log(MAX_DISTANCE / n_exact))
                       * np.float32(N_BUCKETS - n_exact)).astype(np.int32)
    large = np.minimum(large, N_BUCKETS - 1)
    bucket = np.where(rel < n_exact, rel, large)
    in_window = (i + CHUNK - j >= 0) & (i + CHUNK - j < CHUNK)
    return bucket.astype(np.int32), in_window


def _split3(x):
    hi = x.astype(BF16)
    r1 = x - hi.astype(F32)
    mid = r1.astype(BF16)
    lo = (r1 - mid.astype(F32)).astype(BF16)
    return hi, mid, lo


HBM_SPEC = pl.BlockSpec(memory_space=pltpu.HBM)


def _mesh_pos():
    return lax.axis_index("x"), lax.axis_index("y"), lax.axis_index("c")


def _dev_index(px, py, pc):
    return 4 * px + 2 * py + pc


def _all_gather(shards, name):
    n = len(shards)

    def body(*refs):
        ins, outs = refs[:n], refs[n:2 * n]
        send_sems, recv_sems, local_sems = refs[2 * n:]
        x, y, c = _mesh_pos()
        me, sibling = (x, y, c), (x, y, 1 - c)
        chips = [(1 - x, y), (x, 1 - y), (1 - x, 1 - y)]

        def copy(a, k, block, to, src=None):
            dst = outs[a].at[_dev_index(*block)]
            return pltpu.make_async_remote_copy(
                src_ref=dst if src is None else src, dst_ref=dst,
                send_sem=send_sems.at[a * 7 + k], recv_sem=recv_sems.at[a * 7 + k],
                device_id=to, device_id_type=MESH)

        mine = [pltpu.make_async_copy(ins[a], outs[a].at[_dev_index(*me)], local_sems.at[a]) for a in range(n)]
        first = []
        for a in range(n):
            for j, chip in enumerate(chips):
                first.append(copy(a, 1 + j, me, (*chip, c), src=ins[a]))
            first.append(copy(a, 0, me, sibling, src=ins[a]))
        for cp in first:
            cp.start()
        for cp in mine:
            cp.start()
        passed = []
        for a in range(n):
            for j, chip in enumerate(chips):
                copy(a, 1 + j, (*chip, c), me).wait_recv()
                fwd = copy(a, 4 + j, (*chip, c), sibling)
                fwd.start()
                passed.append(fwd)
        for a in range(n):
            copy(a, 0, sibling, me).wait_recv()
            for j, chip in enumerate(chips):
                copy(a, 4 + j, (*chip, 1 - c), me).wait_recv()
        for cp in first + passed:
            cp.wait_send()
        for cp in mine:
            cp.wait()

    return pl.pallas_call(
        body, name=name,
        out_shape=[SDS((N_DEV,) + s.shape, s.dtype) for s in shards],
        in_specs=[HBM_SPEC] * n, out_specs=[HBM_SPEC] * n,
        scratch_shapes=[pltpu.SemaphoreType.DMA((7 * n,)), pltpu.SemaphoreType.DMA((7 * n,)),
                        pltpu.SemaphoreType.DMA((n,))],
    )(*shards)


SEM_SPEC = pl.BlockSpec(memory_space=pltpu.SEMAPHORE)
ANY_SPEC = pl.BlockSpec(memory_space=pl.ANY)
TOKEN_SPEC = pl.BlockSpec(memory_space=pltpu.VMEM)
TOKEN = SDS((8, LANE), F32)
SIDE_EFFECT = pltpu.SideEffectType.DATAFLOW_SIDE_EFFECTING


def _hbm(x):
    return pltpu.with_memory_space_constraint(x, pltpu.HBM)


def _after(x, token):
    return lax.optimization_barrier((x, token))[0]


def _split_start(bufs, copies_of, n_sems, name):
    n = len(bufs)

    def body(*refs):
        ins = refs[:n]
        send_sems, recv_sems = refs[n], refs[n + 1]
        token = refs[2 * n + 2]
        for src, dst, k, target in copies_of(ins):
            pltpu.make_async_remote_copy(src_ref=src, dst_ref=dst, send_sem=send_sems.at[k], recv_sem=recv_sems.at[k],
                                         device_id=target, device_id_type=MESH).start()
        token[...] = jnp.zeros_like(token)

    outs = pl.pallas_call(
        body, name=name,
        out_shape=[pltpu.SemaphoreType.DMA((n_sems,)), pltpu.SemaphoreType.DMA((n_sems,))]
        + [pltpu.HBM(b.shape, b.dtype) for b in bufs] + [TOKEN],
        in_specs=[HBM_SPEC] * n, out_specs=[SEM_SPEC, SEM_SPEC] + [HBM_SPEC] * n + [TOKEN_SPEC],
        input_output_aliases={a: 2 + a for a in range(n)},
        compiler_params=pltpu.CompilerParams(has_side_effects=SIDE_EFFECT),
    )(*[_hbm(b) for b in bufs])
    return outs[0], outs[1], list(outs[2:2 + n]), outs[2 + n]


def _split_wait(bufs, sem_sets, waits_of, after, name):
    n, ns = len(bufs), len(sem_sets)
    flat_sems = [s for pair in sem_sets for s in pair]

    def body(*refs):
        ins = refs[:n]
        sems = refs[n:n + 2 * ns]
        x, y, c = _mesh_pos()
        for kind, src, dst, send_sem, recv_sem in waits_of(ins, [(sems[2 * i], sems[2 * i + 1]) for i in range(ns)]):
            cp = pltpu.make_async_remote_copy(src_ref=src, dst_ref=dst, send_sem=send_sem, recv_sem=recv_sem,
                                              device_id=(x, y, c), device_id_type=MESH)
            if kind == "send":
                cp.wait_send()
            else:
                cp.wait_recv()

    outs = pl.pallas_call(
        body, name=name,
        out_shape=[pltpu.HBM(b.shape, b.dtype) for b in bufs],
        in_specs=[HBM_SPEC] * n + [SEM_SPEC] * (2 * ns) + [ANY_SPEC], out_specs=[HBM_SPEC] * n,
        input_output_aliases={a: a for a in range(n)},
        compiler_params=pltpu.CompilerParams(has_side_effects=SIDE_EFFECT),
    )(*bufs, *flat_sems, after)
    return list(outs)


def _gather_begin(shards, name):
    me = _dev_index(*_mesh_pos())
    lands = [lax.dynamic_update_index_in_dim(lax.empty((N_DEV,) + s.shape, s.dtype), s, me, 0) for s in shards]

    def copies_of(ins):
        x, y, c = _mesh_pos()
        targets = [(x, y, 1 - c), (1 - x, y, c), (x, 1 - y, c), (1 - x, 1 - y, c)]
        out = []
        for a, land in enumerate(ins):
            blk = land.at[_dev_index(x, y, c)]
            for k in (1, 2, 3, 0):
                out.append((blk, blk, 4 * a + k, targets[k]))
        return out

    send_sems, recv_sems, lands, token = _split_start(lands, copies_of, 4 * len(shards), name)
    return dict(lands=lands, sems=(send_sems, recv_sems), token=token, fwd={})


def _gather_pass_on(state, which, after, name):
    def arrivals(ins, sems):
        x, y, c = _mesh_pos()
        chips = [(1 - x, y), (x, 1 - y), (1 - x, 1 - y)]
        out = []
        for i, a in enumerate(which):
            for j, (px, py) in enumerate(chips):
                blk = ins[i].at[_dev_index(px, py, c)]
                out.append(("recv", blk, blk, sems[0][0].at[4 * a + 1 + j], sems[0][1].at[4 * a + 1 + j]))
        return out

    bufs = _split_wait([state["lands"][a] for a in which], [state["sems"]], arrivals, after, name + "_arrived")

    def copies_of(ins):
        x, y, c = _mesh_pos()
        chips = [(1 - x, y), (x, 1 - y), (1 - x, 1 - y)]
        out = []
        for i in range(len(which)):
            for j, (px, py) in enumerate(chips):
                blk = ins[i].at[_dev_index(px, py, c)]
                out.append((blk, blk, 3 * i + j, (x, y, 1 - c)))
        return out

    send_sems, recv_sems, bufs, token = _split_start(bufs, copies_of, 3 * len(which), name)
    for i, a in enumerate(which):
        state["lands"][a] = bufs[i]
    state["fwd"][tuple(which)] = (send_sems, recv_sems)
    return token


def _gather_end(state, which, after, name):
    def waits(ins, sems):
        x, y, c = _mesh_pos()
        chips = [(1 - x, y), (x, 1 - y), (1 - x, 1 - y)]
        (s_send, s_recv), (f_send, f_recv) = sems
        out = []
        for i, a in enumerate(which):
            mine = ins[i].at[_dev_index(x, y, c)]
            sib = ins[i].at[_dev_index(x, y, 1 - c)]
            out.append(("recv", sib, sib, s_send.at[4 * a], s_recv.at[4 * a]))
            for j, (px, py) in enumerate(chips):
                theirs = ins[i].at[_dev_index(px, py, 1 - c)]
                out.append(("recv", theirs, theirs, f_send.at[3 * i + j], f_recv.at[3 * i + j]))
            for k in range(4):
                out.append(("send", mine, mine, s_send.at[4 * a + k], s_recv.at[4 * a + k]))
            for j, (px, py) in enumerate(chips):
                passed = ins[i].at[_dev_index(px, py, c)]
                out.append(("send", passed, passed, f_send.at[3 * i + j], f_recv.at[3 * i + j]))
        return out

    bufs = _split_wait([state["lands"][a] for a in which], [state["sems"], state["fwd"][tuple(which)]], waits, after, name)
    for i, a in enumerate(which):
        state["lands"][a] = bufs[i]
    return bufs


def _sibling_exchange_begin(part, name):
    land = lax.empty((4,) + part.shape[1:], part.dtype)

    def copies_of(ins):
        x, y, c = _mesh_pos()
        return [(ins[0].at[2 * j + (1 - c)], ins[1].at[j], j, (x, y, 1 - c)) for j in range(4)]

    send_sems, recv_sems, bufs, token = _split_start([part, land], copies_of, 4, name)
    return dict(bufs=bufs, sems=(send_sems, recv_sems), token=token)


def _sibling_exchange_end(state, after, name):
    def waits(ins, sems):
        _, _, c = _mesh_pos()
        out = []
        for j in range(4):
            for kind in ("send", "recv"):
                out.append((kind, ins[0].at[2 * j + (1 - c)], ins[1].at[j], sems[0][0].at[j], sems[0][1].at[j]))
        return out

    return _split_wait(state["bufs"], [state["sems"]], waits, after, name)


def _chip_exchange_begin(csum, name):
    land = lax.empty((3,) + csum.shape[1:], csum.dtype)

    def copies_of(ins):
        x, y, c = _mesh_pos()
        chips = [(1 - x, y), (x, 1 - y), (1 - x, 1 - y)]
        return [(ins[0].at[2 * px + py], ins[1].at[r], r, (px, py, c)) for r, (px, py) in enumerate(chips)]

    send_sems, recv_sems, bufs, token = _split_start([csum, land], copies_of, 3, name)
    return dict(bufs=bufs, sems=(send_sems, recv_sems), token=token)


def _chip_exchange_end(state, after, name):
    def waits(ins, sems):
        x, y, _ = _mesh_pos()
        chips = [(1 - x, y), (x, 1 - y), (1 - x, 1 - y)]
        out = []
        for r, (px, py) in enumerate(chips):
            for kind in ("send", "recv"):
                out.append((kind, ins[0].at[2 * px + py], ins[1].at[r], sems[0][0].at[r], sems[0][1].at[r]))
        return out

    return _split_wait(state["bufs"], [state["sems"]], waits, after, name)


def _chip_sum(part, recv, name):
    _, R, C = part.shape
    tr = _tile(R, 512, 16)
    c_idx = lax.axis_index("c").astype(jnp.int32).reshape((1,))

    def body(c_ref, p_ref, r_ref, o_ref):
        o_ref[...] = (p_ref[...].astype(F32) + r_ref[...].astype(F32)).astype(o_ref.dtype)

    grid_spec = pltpu.PrefetchScalarGridSpec(
        num_scalar_prefetch=1, grid=(4, R // tr),
        in_specs=[pl.BlockSpec((None, tr, C), lambda j, i, c_ref: (2 * j + c_ref[0], i, 0)),
                  pl.BlockSpec((None, tr, C), lambda j, i, c_ref: (j, i, 0))],
        out_specs=pl.BlockSpec((None, tr, C), lambda j, i, c_ref: (j, i, 0)))
    return pl.pallas_call(body, name=name, grid_spec=grid_spec, out_shape=SDS((4, R, C), part.dtype),
                          compiler_params=_params(2))(c_idx, part, recv)


def _bias_fwd(table_t, onehot_t):
    H = table_t.shape[0]
    n = onehot_t.shape[1]

    def body(t_ref, oh_ref, o_ref):
        hi, mid, lo = _split3(t_ref[...])
        oh = oh_ref[...]
        o_ref[...] = _dot(hi, oh, NN) + _dot(mid, oh, NN) + _dot(lo, oh, NN)

    return pl.pallas_call(body, name="bias_fwd", out_shape=SDS((H, n), F32),
                          compiler_params=_params(0))(table_t, onehot_t)


def _inproj_fwd(x, g, w):
    T, D = x.shape
    P = w.shape[1]
    tm, tn = _tile(T, 512), _tile(P, 1792)

    def body(x_ref, g_ref, w_ref, proj_ref, n_ref, nbuf):
        @pl.when(pl.program_id(1) == 0)
        def _():
            xv = x_ref[...]
            n = (xv * _rms_stats(xv) * g_ref[...]).astype(BF16)
            nbuf[...] = n
            n_ref[...] = n

        proj_ref[...] = _dot(nbuf[...], w_ref[...], NN)

    return pl.pallas_call(
        body, name="inproj_fwd", grid=(T // tm, P // tn),
        in_specs=[pl.BlockSpec((tm, D), lambda i, j: (i, 0)), pl.BlockSpec((1, D), lambda i, j: (0, 0)),
                  pl.BlockSpec((D, tn), lambda i, j: (0, j))],
        out_specs=[pl.BlockSpec((tm, tn), lambda i, j: (i, j)), pl.BlockSpec((tm, D), lambda i, j: (i, 0))],
        out_shape=[SDS((T, P), F32), SDS((T, D), BF16)],
        scratch_shapes=[pltpu.VMEM((tm, D), BF16)], compiler_params=_params(2))(x, g, w)


def _layer_norm_group(vg, lg, lb):
    mu = jnp.mean(vg, axis=-1, keepdims=True)
    xc = vg - mu
    rstd = lax.rsqrt(jnp.mean(xc * xc, axis=-1, keepdims=True) + EPS)
    vhat = xc * rstd
    return vhat, rstd, vhat * lg + lb


def _gmlp_fwd(proj, lg, lb, w_s, bs_t, A):
    T = proj.shape[0]
    G = A // GROUP_DIM
    tm = _tile(T, 512)
    nc = tm // CHUNK

    def body(u_ref, v_ref, lg_ref, lb_ref, w_ref, bst_ref, a_ref):
        row = lax.broadcasted_iota(jnp.int32, (CHUNK, CHUNK), 0)
        col = lax.broadcasted_iota(jnp.int32, (CHUNK, CHUNK), 1)
        causal = row >= col
        for g in range(G):
            sl = slice(g * GROUP_DIM, (g + 1) * GROUP_DIM)
            _, _, vn = _layer_norm_group(_gelu(v_ref[:, sl]), lg_ref[:, sl], lb_ref[:, sl])
            vnb = vn.astype(BF16)
            wm = jnp.where(causal, w_ref[g], 0.0).astype(BF16)
            ug = _gelu(u_ref[:, sl])
            bcol = bst_ref[:, g:g + 1]
            for c in range(nc):
                rs = slice(c * CHUNK, (c + 1) * CHUNK)
                a_ref[rs, sl] = ug[rs] * (_dot(wm, vnb[rs], NN) + bcol)

    return pl.pallas_call(
        body, name="gmlp_fwd", grid=(T // tm,),
        in_specs=[pl.BlockSpec((tm, A), lambda i: (i, 0)), pl.BlockSpec((tm, A), lambda i: (i, 1)),
                  pl.BlockSpec((1, A), lambda i: (0, 0)), pl.BlockSpec((1, A), lambda i: (0, 0)),
                  pl.BlockSpec((G, CHUNK, CHUNK), lambda i: (0, 0, 0)), pl.BlockSpec((CHUNK, G), lambda i: (0, 0))],
        out_specs=pl.BlockSpec((tm, A), lambda i: (i, 0)),
        out_shape=SDS((T, A), F32), compiler_params=_params(1))(proj, proj, lg, lb, w_s, bs_t)


def _attn_masks(first_tile):
    ii = lax.broadcasted_iota(jnp.int32, (CHUNK, 2 * CHUNK), 0)
    jj = lax.broadcasted_iota(jnp.int32, (CHUNK, 2 * CHUNK), 1)
    in_window = (jj > ii) & (jj <= ii + CHUNK)
    first_mask = in_window & jnp.logical_or(jnp.logical_not(first_tile), jj >= CHUNK)
    return in_window, first_mask


def _attn_probs(qh, kb, bias_h, mask, sink):
    s = _dot(qh, kb, NT) * (HEAD_DIM ** -0.5) + bias_h
    s = jnp.where(mask, s, NEG)
    m = jnp.maximum(jnp.max(s, axis=-1, keepdims=True), sink)
    p = jnp.exp(s - m)
    e_sink = jnp.exp(sink - m)
    inv = 1.0 / (jnp.sum(p, axis=-1, keepdims=True) + e_sink)
    return p * inv, e_sink * inv


def _attn_specs(tq, A, B, reverse_tiles=None):
    nb = tq // CHUNK
    kcol = (2 * A + B) // LANE
    if reverse_tiles is None:
        tile = lambda i: i
    else:
        tile = lambda i: reverse_tiles - 1 - i
    prev = lambda i: jnp.maximum(tile(i) * nb - 1, 0)
    return [pl.BlockSpec((tq, B), lambda i: (tile(i), 2 * A // B)),
            pl.BlockSpec((tq, LANE), lambda i: (tile(i), kcol)),
            pl.BlockSpec((tq, LANE), lambda i: (tile(i), kcol + 1)),
            pl.BlockSpec((CHUNK, LANE), lambda i: (prev(i), kcol)),
            pl.BlockSpec((CHUNK, LANE), lambda i: (prev(i), kcol + 1))]


def _attn_fwd(proj, bias, sinks, A, B):
    T = proj.shape[0]
    H = B // HEAD_DIM
    qpk = H // KV_HEADS
    tq = _tile(T, 512)
    nb = tq // CHUNK

    def body(sink_ref, q_ref, k_ref, v_ref, kp_ref, vp_ref, bias_ref, o_ref):
        in_window, first_mask = _attn_masks(pl.program_id(0) == 0)
        for b in range(nb):
            rows = slice(b * CHUNK, (b + 1) * CHUNK)
            if b == 0:
                kprev, vprev, mask = kp_ref[...], vp_ref[...], first_mask
            else:
                prows = slice((b - 1) * CHUNK, b * CHUNK)
                kprev, vprev, mask = k_ref[prows, :], v_ref[prows, :], in_window
            kband = jnp.concatenate([kprev, k_ref[rows, :]], axis=0).astype(BF16)
            vband = jnp.concatenate([vprev, v_ref[rows, :]], axis=0).astype(BF16)
            for h in range(H):
                ks = slice((h // qpk) * HEAD_DIM, (h // qpk + 1) * HEAD_DIM)
                hs = slice(h * HEAD_DIM, (h + 1) * HEAD_DIM)
                pn, _ = _attn_probs(q_ref[rows, hs].astype(BF16), kband[:, ks], bias_ref[h], mask, sink_ref[h])
                o_ref[rows, hs] = _dot(pn.astype(BF16), vband[:, ks], NN)

    return pl.pallas_call(
        body, name="attn_fwd", grid=(T // tq,),
        in_specs=[pl.BlockSpec(memory_space=pltpu.SMEM)] + _attn_specs(tq, A, B)
        + [pl.BlockSpec((H, CHUNK, 2 * CHUNK), lambda i: (0, 0, 0))],
        out_specs=pl.BlockSpec((tq, B), lambda i: (i, 0)),
        out_shape=SDS((T, B), F32), compiler_params=_params(1))(sinks, proj, proj, proj, proj, proj, bias)


def _outproj_fwd(a, b, ga, gb, x, w):
    T, A = a.shape
    B = b.shape[1]
    D = x.shape[1]
    tm, tn = _tile(T, 512), _tile(D, 1024)

    def body(a_ref, b_ref, ga_ref, gb_ref, x_ref, w_ref, h_ref, mix_ref, mbuf):
        @pl.when(pl.program_id(1) == 0)
        def _():
            av, bv = a_ref[...], b_ref[...]
            mbuf[:, :A] = (av * _rms_stats(av) * ga_ref[...]).astype(BF16)
            mbuf[:, A:] = (bv * _rms_stats(bv) * gb_ref[...]).astype(BF16)
            mix_ref[...] = mbuf[...]

        h_ref[...] = x_ref[...] + _dot(mbuf[...], w_ref[...], NN)

    return pl.pallas_call(
        body, name="outproj_fwd", grid=(T // tm, D // tn),
        in_specs=[pl.BlockSpec((tm, A), lambda i, j: (i, 0)), pl.BlockSpec((tm, B), lambda i, j: (i, 0)),
                  pl.BlockSpec((1, A), lambda i, j: (0, 0)), pl.BlockSpec((1, B), lambda i, j: (0, 0)),
                  pl.BlockSpec((tm, tn), lambda i, j: (i, j)), pl.BlockSpec((A + B, tn), lambda i, j: (0, j))],
        out_specs=[pl.BlockSpec((tm, tn), lambda i, j: (i, j)), pl.BlockSpec((tm, A + B), lambda i, j: (i, 0))],
        out_shape=[SDS((T, D), F32), SDS((T, A + B), BF16)],
        scratch_shapes=[pltpu.VMEM((tm, A + B), BF16)], compiler_params=_params(2))(a, b, ga, gb, x, w)


def _ffn_up(h1, g, w_up):
    T, D = h1.shape
    Fb = w_up.shape[2]
    F = N_DEV * Fb
    tm, tf = _tile(T, 1024), _tile(Fb, 1024)
    per = Fb // tf

    def body(h_ref, g_ref, wu_ref, z_ref, n_ref, nbuf):
        @pl.when(pl.program_id(1) == 0)
        def _():
            hv = h_ref[...]
            n = (hv * _rms_stats(hv) * g_ref[...]).astype(BF16)
            nbuf[...] = n
            n_ref[...] = n

        z_ref[...] = jnp.maximum(_dot(nbuf[...], wu_ref[...], NN), 0.0).astype(BF16)

    return pl.pallas_call(
        body, name="ffn_up", grid=(T // tm, F // tf),
        in_specs=[pl.BlockSpec((tm, D), lambda i, j: (i, 0)), pl.BlockSpec((1, D), lambda i, j: (0, 0)),
                  pl.BlockSpec((None, D, tf), lambda i, j: (j // per, 0, j % per))],
        out_specs=[pl.BlockSpec((tm, tf), lambda i, j: (i, j)), pl.BlockSpec((tm, D), lambda i, j: (i, 0))],
        out_shape=[SDS((T, F), BF16), SDS((T, D), BF16)],
        scratch_shapes=[pltpu.VMEM((tm, D), BF16)], compiler_params=_params(2))(h1, g, w_up)


def _ffn_down(h1, z, w_down):
    T, D = h1.shape
    F = w_down.shape[0]
    tm, tk = _tile(T, 1024), _tile(F, 1024)
    nk = F // tk

    def body(h_ref, z_ref, wd_ref, h2_ref):
        k = pl.program_id(1)

        @pl.when(k == 0)
        def _():
            h2_ref[...] = h_ref[...]

        zf = z_ref[...].astype(F32)
        h2_ref[...] += _dot((zf * zf).astype(BF16), wd_ref[...], NN)

    return pl.pallas_call(
        body, name="ffn_down", grid=(T // tm, nk),
        in_specs=[pl.BlockSpec((tm, D), lambda i, k: (i, 0)), pl.BlockSpec((tm, tk), lambda i, k: (i, k)),
                  pl.BlockSpec((tk, D), lambda i, k: (k, 0))],
        out_specs=pl.BlockSpec((tm, D), lambda i, k: (i, 0)),
        out_shape=SDS((T, D), F32), compiler_params=_params(2))(h1, z, w_down)


def _final_loss(h2, g, target):
    T, D = h2.shape
    tm = _tile(T, 512)

    def body(h_ref, g_ref, t_ref, loss_ref, dg_ref, dh_ref, dhb_ref):
        @pl.when(pl.program_id(0) == 0)
        def _():
            loss_ref[...] = jnp.zeros_like(loss_ref)
            dg_ref[...] = jnp.zeros_like(dg_ref)

        hv, gv = h_ref[...], g_ref[...]
        r = _rms_stats(hv)
        hn = hv * r
        e = hn * gv - t_ref[...]
        loss_ref[...] += (0.5 / D) * jnp.sum(jnp.sum(e * e, axis=-1, keepdims=True), axis=0, keepdims=True)
        dy = e * (1.0 / D)
        dg_ref[...] += jnp.sum(dy * hn, axis=0, keepdims=True)
        dh = _rms_bwd(dy, hv, r, gv)
        dh_ref[...] = dh
        dhb_ref[...] = dh.astype(BF16)

    return pl.pallas_call(
        body, name="final_loss", grid=(T // tm,),
        in_specs=[pl.BlockSpec((tm, D), lambda i: (i, 0)), pl.BlockSpec((1, D), lambda i: (0, 0)),
                  pl.BlockSpec((tm, D), lambda i: (i, 0))],
        out_specs=[pl.BlockSpec((1, 1), lambda i: (0, 0)), pl.BlockSpec((1, D), lambda i: (0, 0)),
                   pl.BlockSpec((tm, D), lambda i: (i, 0)), pl.BlockSpec((tm, D), lambda i: (i, 0))],
        out_shape=[SDS((1, 1), F32), SDS((1, D), F32), SDS((T, D), F32), SDS((T, D), BF16)],
        compiler_params=_params(1))(h2, g, target)


def _ffn_bwd(dh2, dh2b, z, h1, g, w_up, w_down):
    T, D = h1.shape
    Fb = w_up.shape[2]
    F = N_DEV * Fb
    tm, tf = _tile(T, 512), _tile(Fb, 512)
    per = Fb // tf
    nj = F // tf

    def body(dh_ref, dhb_ref, z_ref, h_ref, g_ref, wu_ref, wd_ref, dzp_ref, dh1_ref, dh1b_ref, dg_ref, acc):
        i, j = pl.program_id(0), pl.program_id(1)

        @pl.when(j == 0)
        def _():
            acc[...] = jnp.zeros_like(acc)

        @pl.when((i == 0) & (j == 0))
        def _():
            dg_ref[...] = jnp.zeros_like(dg_ref)

        dzz = _dot(dhb_ref[...], wd_ref[...], NT)
        dzp = (dzz * (2.0 * z_ref[...].astype(F32))).astype(BF16)
        dzp_ref[...] = dzp
        acc[...] += _dot(dzp, wu_ref[...], NT)

        @pl.when(j == nj - 1)
        def _():
            hv, gv, dn = h_ref[...], g_ref[...], acc[...]
            r = _rms_stats(hv)
            dg_ref[...] += jnp.sum(dn * (hv * r), axis=0, keepdims=True)
            dh1 = dh_ref[...] + _rms_bwd(dn, hv, r, gv)
            dh1_ref[...] = dh1
            dh1b_ref[...] = dh1.astype(BF16)

    return pl.pallas_call(
        body, name="ffn_bwd", grid=(T // tm, nj),
        in_specs=[pl.BlockSpec((tm, D), lambda i, j: (i, 0)), pl.BlockSpec((tm, D), lambda i, j: (i, 0)),
                  pl.BlockSpec((tm, tf), lambda i, j: (i, j)), pl.BlockSpec((tm, D), lambda i, j: (i, 0)),
                  pl.BlockSpec((1, D), lambda i, j: (0, 0)),
                  pl.BlockSpec((None, D, tf), lambda i, j: (j // per, 0, j % per)),
                  pl.BlockSpec((None, tf, D), lambda i, j: (j // per, j % per, 0))],
        out_specs=[pl.BlockSpec((tm, tf), lambda i, j: (i, j)), pl.BlockSpec((tm, D), lambda i, j: (i, 0)),
                   pl.BlockSpec((tm, D), lambda i, j: (i, 0)), pl.BlockSpec((1, D), lambda i, j: (0, 0))],
        out_shape=[SDS((T, F), BF16), SDS((T, D), F32), SDS((T, D), BF16), SDS((1, D), F32)],
        scratch_shapes=[pltpu.VMEM((tm, D), F32)], compiler_params=_params(2))(dh2, dh2b, z, h1, g, w_up, w_down)


def _matmul_tn(a, b, name, square_a=False, col_blocks=None):
    T, K = a.shape
    N = b.shape[1]
    tt, tk = _tile(T, 1024), _tile(K, 1024)
    tn = _tile(N if col_blocks is None else N // col_blocks, 1792)
    nt = T // tt

    def body(a_ref, b_ref, o_ref, acc):
        t = pl.program_id(2)

        @pl.when(t == 0)
        def _():
            acc[...] = jnp.zeros_like(acc)

        av = a_ref[...]
        if square_a:
            af = av.astype(F32)
            av = (af * af).astype(BF16)
        acc[...] += _dot(av, b_ref[...], TN)

        @pl.when(t == nt - 1)
        def _():
            o_ref[...] = acc[...].astype(o_ref.dtype)

    if col_blocks is None:
        out_shape = SDS((K, N), BF16)
        out_spec = pl.BlockSpec((tk, tn), lambda i, j, t: (i, j))
    else:
        per = (N // col_blocks) // tn
        out_shape = SDS((col_blocks, K, N // col_blocks), BF16)
        out_spec = pl.BlockSpec((None, tk, tn), lambda i, j, t: (j // per, i, j % per))
    return pl.pallas_call(
        body, name=name, grid=(K // tk, N // tn, nt),
        in_specs=[pl.BlockSpec((tt, tk), lambda i, j, t: (t, i)), pl.BlockSpec((tt, tn), lambda i, j, t: (t, j))],
        out_specs=out_spec, out_shape=out_shape,
        scratch_shapes=[pltpu.VMEM((tk, tn), F32)], compiler_params=_params(3))(a, b)


def _outproj_bwd(dh1b, w, a, b, ga, gb):
    T, D = dh1b.shape
    A, B = a.shape[1], b.shape[1]
    tm = _tile(T, 512)

    def body(dh_ref, w_ref, a_ref, b_ref, ga_ref, gb_ref, da_ref, db_ref, dga_ref, dgb_ref):
        @pl.when(pl.program_id(0) == 0)
        def _():
            dga_ref[...] = jnp.zeros_like(dga_ref)
            dgb_ref[...] = jnp.zeros_like(dgb_ref)

        dmix = _dot(dh_ref[...], w_ref[...], NT)
        for src_ref, g_ref, dx_ref, dg_ref, dn in ((a_ref, ga_ref, da_ref, dga_ref, dmix[:, :A]),
                                                   (b_ref, gb_ref, db_ref, dgb_ref, dmix[:, A:])):
            xv = src_ref[...]
            r = _rms_stats(xv)
            dg_ref[...] += jnp.sum(dn * (xv * r), axis=0, keepdims=True)
            dx_ref[...] = _rms_bwd(dn, xv, r, g_ref[...])

    return pl.pallas_call(
        body, name="outproj_bwd", grid=(T // tm,),
        in_specs=[pl.BlockSpec((tm, D), lambda i: (i, 0)), pl.BlockSpec((A + B, D), lambda i: (0, 0)),
                  pl.BlockSpec((tm, A), lambda i: (i, 0)), pl.BlockSpec((tm, B), lambda i: (i, 0)),
                  pl.BlockSpec((1, A), lambda i: (0, 0)), pl.BlockSpec((1, B), lambda i: (0, 0))],
        out_specs=[pl.BlockSpec((tm, A), lambda i: (i, 0)), pl.BlockSpec((tm, B), lambda i: (i, 0)),
                   pl.BlockSpec((1, A), lambda i: (0, 0)), pl.BlockSpec((1, B), lambda i: (0, 0))],
        out_shape=[SDS((T, A), F32), SDS((T, B), F32), SDS((1, A), F32), SDS((1, B), F32)],
        compiler_params=_params(1))(dh1b, w, a, b, ga, gb)


def _gmlp_bwd(proj, da, lg, lb, w_s, w_st, bs_t, A):
    T = proj.shape[0]
    G = A // GROUP_DIM
    tm = _tile(T, 512)
    nc = tm // CHUNK

    def body(u_ref, v_ref, da_ref, lg_ref, lb_ref, w_ref, wt_ref, bst_ref, duv_ref, dlg_ref, dlb_ref, dw_ref, dbs_ref):
        @pl.when(pl.program_id(0) == 0)
        def _():
            dlg_ref[...] = jnp.zeros_like(dlg_ref)
            dlb_ref[...] = jnp.zeros_like(dlb_ref)
            dw_ref[...] = jnp.zeros_like(dw_ref)
            dbs_ref[...] = jnp.zeros_like(dbs_ref)

        row = lax.broadcasted_iota(jnp.int32, (CHUNK, CHUNK), 0)
        col = lax.broadcasted_iota(jnp.int32, (CHUNK, CHUNK), 1)
        lower = row >= col
        upper = row <= col
        for g in range(G):
            sl = slice(g * GROUP_DIM, (g + 1) * GROUP_DIM)
            lgv = lg_ref[:, sl]
            vg, vg_grad = _gelu_and_grad(v_ref[:, sl])
            vhat, rstd, vn = _layer_norm_group(vg, lgv, lb_ref[:, sl])
            vnb = vn.astype(BF16)
            ug, ug_grad = _gelu_and_grad(u_ref[:, sl])
            dav = da_ref[:, sl]
            wm = jnp.where(lower, w_ref[g], 0.0).astype(BF16)
            wmt = jnp.where(upper, wt_ref[g], 0.0).astype(BF16)
            bcol = bst_ref[:, g:g + 1]
            dw_acc = jnp.zeros((CHUNK, CHUNK), F32)
            dbs_acc = jnp.zeros((CHUNK, 1), F32)
            dvn_parts = []
            dug_parts = []
            for c in range(nc):
                rs = slice(c * CHUNK, (c + 1) * CHUNK)
                mixed = _dot(wm, vnb[rs], NN) + bcol
                dug_parts.append(dav[rs] * mixed)
                dmix = dav[rs] * ug[rs]
                dbs_acc = dbs_acc + jnp.sum(dmix, axis=-1, keepdims=True)
                dmixb = dmix.astype(BF16)
                dw_acc = dw_acc + _dot(dmixb, vnb[rs], NT)
                dvn_parts.append(_dot(wmt, dmixb, NN))
            dvn = jnp.concatenate(dvn_parts, axis=0)
            dug = jnp.concatenate(dug_parts, axis=0)
            dw_ref[g] += jnp.where(lower, dw_acc, 0.0)
            dbs_ref[:, g:g + 1] += dbs_acc
            dlg_ref[:, sl] += jnp.sum(dvn * vhat, axis=0, keepdims=True)
            dlb_ref[:, sl] += jnp.sum(dvn, axis=0, keepdims=True)
            dvhat = dvn * lgv
            dvg = rstd * (dvhat - jnp.mean(dvhat, axis=-1, keepdims=True)
                          - vhat * jnp.mean(dvhat * vhat, axis=-1, keepdims=True))
            duv_ref[:, sl] = (dug * ug_grad).astype(BF16)
            duv_ref[:, A + g * GROUP_DIM:A + (g + 1) * GROUP_DIM] = (dvg * vg_grad).astype(BF16)

    return pl.pallas_call(
        body, name="gmlp_bwd", grid=(T // tm,),
        in_specs=[pl.BlockSpec((tm, A), lambda i: (i, 0)), pl.BlockSpec((tm, A), lambda i: (i, 1)),
                  pl.BlockSpec((tm, A), lambda i: (i, 0)),
                  pl.BlockSpec((1, A), lambda i: (0, 0)), pl.BlockSpec((1, A), lambda i: (0, 0)),
                  pl.BlockSpec((G, CHUNK, CHUNK), lambda i: (0, 0, 0)),
                  pl.BlockSpec((G, CHUNK, CHUNK), lambda i: (0, 0, 0)), pl.BlockSpec((CHUNK, G), lambda i: (0, 0))],
        out_specs=[pl.BlockSpec((tm, 2 * A), lambda i: (i, 0)),
                   pl.BlockSpec((1, A), lambda i: (0, 0)), pl.BlockSpec((1, A), lambda i: (0, 0)),
                   pl.BlockSpec((G, CHUNK, CHUNK), lambda i: (0, 0, 0)), pl.BlockSpec((CHUNK, G), lambda i: (0, 0))],
        out_shape=[SDS((T, 2 * A), BF16), SDS((1, A), F32), SDS((1, A), F32),
                   SDS((G, CHUNK, CHUNK), F32), SDS((CHUNK, G), F32)],
        compiler_params=_params(1))(proj, proj, da, lg, lb, w_s, w_st, bs_t)


def _attn_bwd(proj, o, do, duv, bias, sinks, A, B):
    T, P = proj.shape
    H = B // HEAD_DIM
    qpk = H // KV_HEADS
    tq = _tile(T, 512)
    nb = tq // CHUNK
    n_tiles = T // tq
    scale = HEAD_DIM ** -0.5
    rev = lambda i: n_tiles - 1 - i

    def body(sink_ref, q_ref, k_ref, v_ref, kp_ref, vp_ref, o_ref, do_ref, duv_ref, bias_ref,
             dproj_ref, dbias_ref, dsink_ref, carry, dkv, sacc):
        step = pl.program_id(0)

        @pl.when(step == 0)
        def _():
            carry[...] = jnp.zeros_like(carry)
            sacc[...] = jnp.zeros_like(sacc)
            dbias_ref[...] = jnp.zeros_like(dbias_ref)

        in_window, first_mask = _attn_masks(step == n_tiles - 1)
        dproj_ref[:, :2 * A] = duv_ref[...]
        dkv[...] = jnp.zeros_like(dkv)
        for b in range(nb):
            rows = slice(b * CHUNK, (b + 1) * CHUNK)
            band = slice(b * CHUNK, (b + 2) * CHUNK)
            if b == 0:
                kprev, vprev, mask = kp_ref[...], vp_ref[...], first_mask
            else:
                prows = slice((b - 1) * CHUNK, b * CHUNK)
                kprev, vprev, mask = k_ref[prows, :], v_ref[prows, :], in_window
            kband = jnp.concatenate([kprev, k_ref[rows, :]], axis=0).astype(BF16)
            vband = jnp.concatenate([vprev, v_ref[rows, :]], axis=0).astype(BF16)
            for kv in range(KV_HEADS):
                ks = slice(kv * HEAD_DIM, (kv + 1) * HEAD_DIM)
                kb, vb = kband[:, ks], vband[:, ks]
                dk_acc = jnp.zeros((2 * CHUNK, HEAD_DIM), F32)
                dv_acc = jnp.zeros((2 * CHUNK, HEAD_DIM), F32)
                for h in range(kv * qpk, (kv + 1) * qpk):
                    hs = slice(h * HEAD_DIM, (h + 1) * HEAD_DIM)
                    qh = q_ref[rows, hs].astype(BF16)
                    pn, p_sink = _attn_probs(qh, kb, bias_ref[h], mask, sink_ref[h])
                    doh = do_ref[rows, hs]
                    delta = jnp.sum(doh * o_ref[rows, hs], axis=-1, keepdims=True)
                    dohb = doh.astype(BF16)
                    ds = pn * (_dot(dohb, vb, NT) - delta)
                    dbias_ref[h] += ds
                    sacc[:, h:h + 1] += -(p_sink * delta)
                    dsb = ds.astype(BF16)
                    dproj_ref[rows, 2 * A + h * HEAD_DIM:2 * A + (h + 1) * HEAD_DIM] = (
                        _dot(dsb, kb, NN) * scale).astype(BF16)
                    dk_acc = dk_acc + _dot(dsb, qh, TN)
                    dv_acc = dv_acc + _dot(pn.astype(BF16), dohb, TN)
                dkv[band, ks] += dk_acc * scale
                dkv[band, LANE + kv * HEAD_DIM:LANE + (kv + 1) * HEAD_DIM] += dv_acc
        last = slice(tq, tq + CHUNK)
        dkv[last, :] += carry[...]
        dproj_ref[:, 2 * A + B:] = dkv[CHUNK:, :].astype(BF16)
        carry[...] = dkv[:CHUNK, :]

        @pl.when(step == n_tiles - 1)
        def _():
            dsink_ref[...] = jnp.sum(sacc[...], axis=0, keepdims=True)

    specs = _attn_specs(tq, A, B, reverse_tiles=n_tiles)
    return pl.pallas_call(
        body, name="attn_bwd", grid=(n_tiles,),
        in_specs=[pl.BlockSpec(memory_space=pltpu.SMEM)] + specs
        + [pl.BlockSpec((tq, B), lambda i: (rev(i), 0)), pl.BlockSpec((tq, B), lambda i: (rev(i), 0)),
           pl.BlockSpec((tq, 2 * A), lambda i: (rev(i), 0)),
           pl.BlockSpec((H, CHUNK, 2 * CHUNK), lambda i: (0, 0, 0))],
        out_specs=[pl.BlockSpec((tq, P), lambda i: (rev(i), 0)),
                   pl.BlockSpec((H, CHUNK, 2 * CHUNK), lambda i: (0, 0, 0)), pl.BlockSpec((1, H), lambda i: (0, 0))],
        out_shape=[SDS((T, P), BF16), SDS((H, CHUNK, 2 * CHUNK), F32), SDS((1, H), F32)],
        scratch_shapes=[pltpu.VMEM((CHUNK, 2 * LANE), F32), pltpu.VMEM((tq + CHUNK, 2 * LANE), F32),
                        pltpu.VMEM((CHUNK, H), F32)],
        compiler_params=_params(1))(sinks, proj, proj, proj, proj, proj, o, do, duv, bias)


def _bias_bwd(dbias, onehot):
    H = dbias.shape[0]
    nbk = onehot.shape[1]

    def body(d_ref, oh_ref, o_ref):
        hi, mid, lo = _split3(d_ref[...])
        oh = oh_ref[...]
        o_ref[...] = _dot(hi, oh, NN) + _dot(mid, oh, NN) + _dot(lo, oh, NN)

    return pl.pallas_call(body, name="bias_bwd", out_shape=SDS((H, nbk), F32),
                          compiler_params=_params(0))(dbias, onehot)


def _inproj_bwd(dproj, w, x, dh1, g):
    T, P = dproj.shape
    D = x.shape[1]
    tm, tn = _tile(T, 512), _tile(P, 1792)
    nj = P // tn

    def body(dp_ref, w_ref, x_ref, dh_ref, g_ref, dx_ref, dg_ref, acc):
        i, j = pl.program_id(0), pl.program_id(1)

        @pl.when(j == 0)
        def _():
            acc[...] = jnp.zeros_like(acc)

        @pl.when((i == 0) & (j == 0))
        def _():
            dg_ref[...] = jnp.zeros_like(dg_ref)

        acc[...] += _dot(dp_ref[...], w_ref[...], NT)

        @pl.when(j == nj - 1)
        def _():
            xv, dn = x_ref[...], acc[...]
            r = _rms_stats(xv)
            dg_ref[...] += jnp.sum(dn * (xv * r), axis=0, keepdims=True)
            dx_ref[...] = dh_ref[...] + _rms_bwd(dn, xv, r, g_ref[...])

    return pl.pallas_call(
        body, name="inproj_bwd", grid=(T // tm, nj),
        in_specs=[pl.BlockSpec((tm, tn), lambda i, j: (i, j)), pl.BlockSpec((D, tn), lambda i, j: (0, j)),
                  pl.BlockSpec((tm, D), lambda i, j: (i, 0)), pl.BlockSpec((tm, D), lambda i, j: (i, 0)),
                  pl.BlockSpec((1, D), lambda i, j: (0, 0))],
        out_specs=[pl.BlockSpec((tm, D), lambda i, j: (i, 0)), pl.BlockSpec((1, D), lambda i, j: (0, 0))],
        out_shape=[SDS((T, D), F32), SDS((1, D), F32)],
        scratch_shapes=[pltpu.VMEM((tm, D), F32)], compiler_params=_params(2))(dproj, w, x, dh1, g)


def _adamw(w, g, m, v):
    m = ADAM_B1 * m + (1.0 - ADAM_B1) * g
    v = ADAM_B2 * v + (1.0 - ADAM_B2) * (g * g)
    m_hat = m / (1.0 - ADAM_B1 ** ADAM_STEP)
    v_hat = v / (1.0 - ADAM_B2 ** ADAM_STEP)
    delta = -ADAM_LR * (m_hat / (jnp.sqrt(v_hat) + ADAM_EPS) + ADAM_WD * w)
    return delta, m, v


def _adam_sharded(csum, recv, w, m, v, name):
    R, C = w.shape
    tr = _tile(R, 256, 16)
    own = (2 * lax.axis_index("x") + lax.axis_index("y")).astype(jnp.int32).reshape((1,))

    def body(own_idx, own_ref, recv_ref, w_ref, m_ref, v_ref, g_ref, d_ref, nm_ref, nv_ref):
        g = own_ref[...].astype(F32)
        for r in range(3):
            g = g + recv_ref[r].astype(F32)
        delta, nm, nv = _adamw(w_ref[...], g, m_ref[...], v_ref[...])
        g_ref[...] = g
        d_ref[...] = delta
        nm_ref[...] = nm
        nv_ref[...] = nv

    blk = pl.BlockSpec((tr, C), lambda i, own_idx: (i, 0))
    grid_spec = pltpu.PrefetchScalarGridSpec(
        num_scalar_prefetch=1, grid=(R // tr,),
        in_specs=[pl.BlockSpec((None, tr, C), lambda i, own_idx: (own_idx[0], i, 0)),
                  pl.BlockSpec((3, tr, C), lambda i, own_idx: (0, i, 0)), blk, blk, blk],
        out_specs=[blk] * 4)
    return pl.pallas_call(body, name=name, grid_spec=grid_spec, out_shape=[SDS((R, C), F32)] * 4,
                          compiler_params=_params(1))(own, csum, recv, w, m, v)


def _adam_small(gathered, w, m, v):
    R = w.shape[0]

    def body(p_ref, w_ref, m_ref, v_ref, g_ref, d_ref, nm_ref, nv_ref):
        g = p_ref[0]
        for d in range(1, N_DEV):
            g = g + p_ref[d]
        delta, nm, nv = _adamw(w_ref[...], g, m_ref[...], v_ref[...])
        g_ref[...] = g
        d_ref[...] = delta
        nm_ref[...] = nm
        nv_ref[...] = nv

    return pl.pallas_call(body, name="adam_small", out_shape=[SDS((R, LANE), F32)] * 4,
                          compiler_params=_params(0))(gathered, w, m, v)


def _pack(arrays):
    tile = 8 * LANE
    pieces = []
    for a in arrays:
        flat = a.reshape(-1).astype(F32)
        pieces.append(jnp.pad(flat, (0, (-flat.size) % tile)))
    return jnp.concatenate(pieces).reshape(-1, LANE)


def _unpack(packed, shapes):
    tile = 8 * LANE
    flat = packed.reshape(-1)
    out, off = [], 0
    for s in shapes:
        size = int(np.prod(s))
        out.append(flat[off:off + size].reshape(s))
        off += size + (-size) % tile
    return out


def kernel(x, rel_bias_table, mix_norm_g, w_in, gate_norm_g, gate_norm_b, w_spatial, b_spatial, attn_sinks, out_norm_a_g, out_norm_b_g, w_out, ffn_norm_g, w_up, w_down, final_norm_g, loss_target, m_rel_bias_table, m_mix_norm_g, m_w_in, m_gate_norm_g, m_gate_norm_b, m_w_spatial, m_b_spatial, m_attn_sinks, m_out_norm_a_g, m_out_norm_b_g, m_w_out, m_ffn_norm_g, m_w_up, m_w_down, m_final_norm_g, v_rel_bias_table, v_mix_norm_g, v_w_in, v_gate_norm_g, v_gate_norm_b, v_w_spatial, v_b_spatial, v_attn_sinks, v_out_norm_a_g, v_out_norm_b_g, v_w_out, v_ffn_norm_g, v_w_up, v_w_down, v_final_norm_g):
    T, D = x.shape[1], x.shape[2]
    A = D // 2
    B = D // 2
    G = A // GROUP_DIM
    H = B // HEAD_DIM
    P = 2 * A + B + 2 * KV_HEADS * HEAD_DIM
    Pb = w_in.shape[2]
    xs = x.reshape(T, D)
    target = loss_target.reshape(T, D)

    shards = [w_in[0].astype(BF16), w_out[0].astype(BF16), w_up[0].astype(BF16), w_down[0].astype(BF16)]
    gather = _gather_begin(shards, "gather_start")
    tok = _gather_pass_on(gather, [0], gather["token"], "gather_in_pass")
    (win_g,) = _gather_end(gather, [0], tok, "gather_in_end")
    win_full = jnp.transpose(win_g, (1, 0, 2)).reshape(D, P)

    g1, g2, g3 = mix_norm_g.reshape(1, D), ffn_norm_g.reshape(1, D), final_norm_g.reshape(1, D)
    lg, lb = gate_norm_g.reshape(1, A), gate_norm_b.reshape(1, A)
    ws = w_spatial[0]
    ws_t = jnp.swapaxes(ws, 1, 2)
    bs_t = jnp.transpose(b_spatial[0])
    ga, gb = out_norm_a_g.reshape(1, A), out_norm_b_g.reshape(1, B)
    sinks = attn_sinks.reshape(H)
    bucket, in_window = _t5_bucket()
    onehot_np = ((bucket.reshape(-1, 1) == np.arange(N_BUCKETS)[None, :]) & in_window.reshape(-1, 1))
    onehot = jnp.asarray(onehot_np.astype(np.float32)).astype(BF16)

    last = [win_g]

    def tie(value):
        return _after(value, last[0])

    def did(value):
        last[0] = value
        return value

    bias = did(_bias_fwd(tie(jnp.transpose(rel_bias_table)), jnp.transpose(onehot))).reshape(H, CHUNK, 2 * CHUNK)
    proj, n1 = _inproj_fwd(tie(xs), g1, win_full)
    did(n1)
    did(_gather_pass_on(gather, [1], last[0], "gather_out_pass"))
    a_out = did(_gmlp_fwd(tie(proj), lg, lb, ws, bs_t, A))
    b_out = did(_attn_fwd(tie(proj), bias, sinks, A, B))
    did(_gather_pass_on(gather, [2], last[0], "gather_up_pass"))
    (wout_g,) = _gather_end(gather, [1], last[0], "gather_out_end")
    wout_full = did(wout_g).reshape(A + B, D)
    h1, mixed = _outproj_fwd(tie(a_out), b_out, ga, gb, xs, wout_full)
    did(h1)
    did(_gather_pass_on(gather, [3], last[0], "gather_down_pass"))
    (wup_g,) = _gather_end(gather, [2], last[0], "gather_up_end")
    z, n2 = _ffn_up(tie(h1), g2, did(wup_g))
    did(z)
    (wdown_g,) = _gather_end(gather, [3], last[0], "gather_down_end")
    h2 = did(_ffn_down(tie(h1), z, did(wdown_g).reshape(-1, D)))
    loss_part, dg3, dh2, dh2b = _final_loss(tie(h2), g3, target)
    did(dh2b)

    def reduce_begin(grad, name):
        state = _sibling_exchange_begin(tie(grad), name + "_sib")
        did(state["token"])
        return state

    def reduce_to_chip(state, name):
        part, received = _sibling_exchange_end(state, last[0], name + "_sib_end")
        state = _chip_exchange_begin(_chip_sum(part, received, name + "_chip_sum"), name + "_chip")
        did(state["token"])
        return state

    dwdown = did(_matmul_tn(tie(z), dh2b, "grad_w_down", square_a=True)).reshape(wdown_g.shape)
    sib_down = reduce_begin(dwdown, "rs_down")
    dzp, dh1, dh1b, dg2 = _ffn_bwd(tie(dh2), dh2b, z, h1, g2, wup_g, wdown_g)
    did(dh1b)
    chip_down = reduce_to_chip(sib_down, "rs_down")
    dwup = did(_matmul_tn(tie(n2), dzp, "grad_w_up", col_blocks=N_DEV))
    sib_up = reduce_begin(dwup, "rs_up")
    da, db, dga, dgb = _outproj_bwd(tie(dh1b), wout_full, a_out, b_out, ga, gb)
    did(da)
    chip_up = reduce_to_chip(sib_up, "rs_up")
    dwout = did(_matmul_tn(tie(mixed), dh1b, "grad_w_out")).reshape(wout_g.shape)
    sib_out = reduce_begin(dwout, "rs_out")
    duv, dlg, dlb, dws, dbs_t = _gmlp_bwd(tie(proj), da, lg, lb, ws, ws_t, bs_t, A)
    did(duv)
    dproj, dbias, dsinks = _attn_bwd(tie(proj), b_out, db, duv, bias, sinks, A, B)
    did(dproj)
    chip_out = reduce_to_chip(sib_out, "rs_out")
    dtable_t = did(_bias_bwd(tie(dbias.reshape(H, -1)), onehot))
    dwin = _matmul_tn(tie(n1), dproj, "grad_w_in")
    dwin = did(jnp.transpose(dwin.reshape(D, N_DEV, Pb), (1, 0, 2)))
    sib_in = reduce_begin(dwin, "rs_in")
    grad_x, dg1 = _inproj_bwd(tie(dproj), win_full, xs, dh1, g1)
    did(grad_x)
    chip_in = reduce_to_chip(sib_in, "rs_in")

    small_w = [rel_bias_table, mix_norm_g, gate_norm_g, gate_norm_b, w_spatial, b_spatial, attn_sinks,
               out_norm_a_g, out_norm_b_g, ffn_norm_g, final_norm_g]
    small_m = [m_rel_bias_table, m_mix_norm_g, m_gate_norm_g, m_gate_norm_b, m_w_spatial, m_b_spatial, m_attn_sinks,
               m_out_norm_a_g, m_out_norm_b_g, m_ffn_norm_g, m_final_norm_g]
    small_v = [v_rel_bias_table, v_mix_norm_g, v_gate_norm_g, v_gate_norm_b, v_w_spatial, v_b_spatial, v_attn_sinks,
               v_out_norm_a_g, v_out_norm_b_g, v_ffn_norm_g, v_final_norm_g]
    small_g = [jnp.transpose(dtable_t), dg1, dlg, dlb, dws, jnp.transpose(dbs_t), dsinks, dga, dgb, dg2, dg3]
    shapes = [w.shape for w in small_w]
    (gathered,) = _all_gather([tie(_pack(small_g))], "gather_small_grads")
    small_out = _adam_small(gathered, _pack(small_w), _pack(small_m), _pack(small_v))
    did(small_out[0])
    sg, sd, sm, sv = [_unpack(o, shapes) for o in small_out]

    big = [None] * 4
    for k, state, (w, m, v) in ((3, chip_down, (w_down, m_w_down, v_w_down)), (2, chip_up, (w_up, m_w_up, v_w_up)),
                                (1, chip_out, (w_out, m_w_out, v_w_out)), (0, chip_in, (w_in, m_w_in, v_w_in))):
        csum, received = _chip_exchange_end(state, last[0], "rs_%d_end" % k)
        outs = _adam_sharded(csum, received, w[0], m[0], v[0], "adam_%d" % k)
        did(outs[0])
        big[k] = [o.reshape(w.shape) for o in outs]

    loss = lax.psum(loss_part[0, 0], ("x", "y", "c"))

    order = ["s0", "s1", "b0", "s2", "s3", "s4", "s5", "s6", "s7", "s8", "b1", "s9", "b2", "b3", "s10"]

    def group(idx):
        small = (sg, sd, sm, sv)[idx]
        return [small[int(t[1:])] if t[0] == "s" else big[int(t[1:])][idx] for t in order]

    return (loss, grad_x.reshape(x.shape), *group(0), *group(1), *group(2), *group(3))
```

```python
import functools
import math

import numpy as np
import jax
import jax.numpy as jnp
from jax import lax
from jax.experimental import pallas as pl
from jax.experimental.pallas import tpu as pltpu

F32 = jnp.float32
BF16 = jnp.bfloat16
SDS = jax.ShapeDtypeStruct
MESH = pl.DeviceIdType.MESH

N_DEV = 8
EPS = 1e-5
NEG = -1e30
CHUNK = 128
GROUP_DIM = 128
HEAD_DIM = 64
KV_HEADS = 2
N_BUCKETS = 32
MAX_DISTANCE = 128
ADAM_LR, ADAM_B1, ADAM_B2, ADAM_EPS, ADAM_WD, ADAM_STEP = 0.001, 0.9, 0.999, 1e-08, 0.01, 10
GELU_C0 = math.sqrt(2.0 / math.pi)
GELU_C1 = 0.044715

V7X_VMEM_BYTES = 64 * 1024 * 1024
VMEM_LIMIT = V7X_VMEM_BYTES - 8 * 1024 * 1024
LANE = 128

NN = ((1,), (0,))
NT = ((1,), (1,))
TN = ((0,), (0,))


def _dot(a, b, dims):
    return lax.dot_general(a, b, (dims, ((), ())), preferred_element_type=F32)


def _tile(n, pref, unit=LANE):
    best = None
    for t in range(unit, min(n, pref) + 1, unit):
        if n % t == 0:
            best = t
    return n if best is None else best


def _params(n_grid):
    return pltpu.CompilerParams(dimension_semantics=("arbitrary",) * n_grid, vmem_limit_bytes=VMEM_LIMIT)


def _gelu(x):
    return 0.5 * x * (1.0 + jnp.tanh(GELU_C0 * (x + GELU_C1 * x * x * x)))


def _gelu_and_grad(x):
    x2 = x * x
    t = jnp.tanh(GELU_C0 * x * (1.0 + GELU_C1 * x2))
    val = 0.5 * x * (1.0 + t)
    grad = 0.5 * (1.0 + t) + 0.5 * x * (1.0 - t * t) * (GELU_C0 * (1.0 + 3.0 * GELU_C1 * x2))
    return val, grad


def _rms_stats(x):
    return lax.rsqrt(jnp.mean(x * x, axis=-1, keepdims=True) + EPS)


def _rms_bwd(dy, x, r, g):
    w = dy * g
    return r * w - x * (r * r * r) * jnp.mean(w * x, axis=-1, keepdims=True)


def _t5_bucket():
    i = np.arange(CHUNK)[:, None]
    j = np.arange(2 * CHUNK)[None, :]
    rel = np.maximum(i + CHUNK - j, 0)
    n_exact = N_BUCKETS // 2
    relf = np.maximum(rel, n_exact).astype(np.float32)
    large = n_exact + (np.log(relf / np.float32(n_exact)) / np.float32(math.log(MAX_DISTANCE / n_exact))
                       * np.float32(N_BUCKETS - n_exact)).astype(np.int32)
    large = np.minimum(large, N_BUCKETS - 1)
    bucket = np.where(rel < n_exact, rel, large)
    in_window = (i + CHUNK - j >= 0) & (i + CHUNK - j < CHUNK)
    return bucket.astype(np.int32), in_window


def _split3(x):
    hi = x.astype(BF16)
    r1 = x - hi.astype(F32)
    mid = r1.astype(BF16)
    lo = (r1 - mid.astype(F32)).astype(BF16)
    return hi, mid, lo


HBM_SPEC = pl.BlockSpec(memory_space=pltpu.HBM)


def _mesh_pos():
    return lax.axis_index("x"), lax.axis_index("y"), lax.axis_index("c")


def _dev_index(px, py, pc):
    return 4 * px + 2 * py + pc


def _all_gather(shards, name):
    n = len(shards)

    def body(*refs):
        ins, outs = refs[:n], refs[n:2 * n]
        send_sems, recv_sems, local_sems = refs[2 * n:]
        x, y, c = _mesh_pos()
        me, sibling = (x, y, c), (x, y, 1 - c)
        chips = [(1 - x, y), (x, 1 - y), (1 - x, 1 - y)]

        def copy(a, k, block, to, src=None):
            dst = outs[a].at[_dev_index(*block)]
            return pltpu.make_async_remote_copy(
                src_ref=dst if src is None else src, dst_ref=dst,
                send_sem=send_sems.at[a * 7 + k], recv_sem=recv_sems.at[a * 7 + k],
                device_id=to, device_id_type=MESH)

        mine = [pltpu.make_async_copy(ins[a], outs[a].at[_dev_index(*me)], local_sems.at[a]) for a in range(n)]
        first = []
        for a in range(n):
            for j, chip in enumerate(chips):
                first.append(copy(a, 1 + j, me, (*chip, c), src=ins[a]))
            first.append(copy(a, 0, me, sibling, src=ins[a]))
        for cp in first:
            cp.start()
        for cp in mine:
            cp.start()
        passed = []
        for a in range(n):
            for j, chip in enumerate(chips):
                copy(a, 1 + j, (*chip, c), me).wait_recv()
                fwd = copy(a, 4 + j, (*chip, c), sibling)
                fwd.start()
                passed.append(fwd)
        for a in range(n):
            copy(a, 0, sibling, me).wait_recv()
            for j, chip in enumerate(chips):
                copy(a, 4 + j, (*chip, 1 - c), me).wait_recv()
        for cp in first + passed:
            cp.wait_send()
        for cp in mine:
            cp.wait()

    return _CHAIN.call(
        body, name=name,
        out_shape=[SDS((N_DEV,) + s.shape, s.dtype) for s in shards],
        in_specs=[HBM_SPEC] * n, out_specs=[HBM_SPEC] * n,
        scratch_shapes=[pltpu.SemaphoreType.DMA((7 * n,)), pltpu.SemaphoreType.DMA((7 * n,)),
                        pltpu.SemaphoreType.DMA((n,))],
    )(*shards)


SEM_SPEC = pl.BlockSpec(memory_space=pltpu.SEMAPHORE)
ANY_SPEC = pl.BlockSpec(memory_space=pl.ANY)
VMEM_SPEC = pl.BlockSpec(memory_space=pltpu.VMEM)
TOKEN_SPEC = VMEM_SPEC
TOKEN = SDS((8, LANE), F32)
SIDE_EFFECT = pltpu.SideEffectType.DATAFLOW_SIDE_EFFECTING


def _hbm(x):
    return pltpu.with_memory_space_constraint(x, pltpu.HBM)


class _CallChain:
    def __init__(self):
        self.token = None

    def call(self, body, *, in_specs, out_specs, out_shape, **kwargs):
        dep, n_in = self.token, len(in_specs)
        single = not isinstance(out_shape, (list, tuple))
        out_shapes = [out_shape] if single else list(out_shape)
        out_specs = [out_specs] if single else list(out_specs)
        n_out = len(out_shapes)
        n_dep = 0 if dep is None else 1
        token_spec = pl.BlockSpec((8, LANE), lambda *_: (0, 0)) if kwargs.get("grid") else VMEM_SPEC

        def chained(*refs):
            outs_at = n_in + n_dep
            body(*refs[:n_in], *refs[outs_at:outs_at + n_out], *refs[outs_at + n_out + 1:])
            token = refs[outs_at + n_out]
            token[...] = jnp.zeros_like(token)

        inner = pl.pallas_call(chained, in_specs=list(in_specs) + [ANY_SPEC] * n_dep, out_specs=out_specs + [token_spec],
                               out_shape=out_shapes + [TOKEN], **kwargs)

        def run(*operands):
            outs = inner(*operands) if dep is None else inner(*operands, dep)
            self.token = outs[n_out]
            return outs[0] if single else list(outs[:n_out])

        return run


_CHAIN = _CallChain()


def _split_start(bufs, copies_of, n_sems, name):
    n = len(bufs)

    def body(*refs):
        ins = refs[:n]
        send_sems, recv_sems = refs[n], refs[n + 1]
        for src, dst, k, target in copies_of(ins):
            pltpu.make_async_remote_copy(src_ref=src, dst_ref=dst, send_sem=send_sems.at[k], recv_sem=recv_sems.at[k],
                                         device_id=target, device_id_type=MESH).start()

    outs = _CHAIN.call(
        body, name=name,
        out_shape=[pltpu.SemaphoreType.DMA((n_sems,)), pltpu.SemaphoreType.DMA((n_sems,))]
        + [pltpu.HBM(b.shape, b.dtype) for b in bufs],
        in_specs=[HBM_SPEC] * n, out_specs=[SEM_SPEC, SEM_SPEC] + [HBM_SPEC] * n,
        input_output_aliases={a: 2 + a for a in range(n)},
        compiler_params=pltpu.CompilerParams(has_side_effects=SIDE_EFFECT),
    )(*[_hbm(b) for b in bufs])
    return outs[0], outs[1], list(outs[2:2 + n])


def _split_wait(bufs, sem_sets, waits_of, name):
    n, ns = len(bufs), len(sem_sets)
    flat_sems = [s for pair in sem_sets for s in pair]

    def body(*refs):
        ins = refs[:n]
        sems = refs[n:n + 2 * ns]
        x, y, c = _mesh_pos()
        for kind, src, dst, send_sem, recv_sem in waits_of(ins, [(sems[2 * i], sems[2 * i + 1]) for i in range(ns)]):
            cp = pltpu.make_async_remote_copy(src_ref=src, dst_ref=dst, send_sem=send_sem, recv_sem=recv_sem,
                                              device_id=(x, y, c), device_id_type=MESH)
            if kind == "send":
                cp.wait_send()
            else:
                cp.wait_recv()

    outs = _CHAIN.call(
        body, name=name,
        out_shape=[pltpu.HBM(b.shape, b.dtype) for b in bufs],
        in_specs=[HBM_SPEC] * n + [SEM_SPEC] * (2 * ns), out_specs=[HBM_SPEC] * n,
        input_output_aliases={a: a for a in range(n)},
        compiler_params=pltpu.CompilerParams(has_side_effects=SIDE_EFFECT),
    )(*bufs, *flat_sems)
    return list(outs)


def _gather_begin(shards, name):
    me = _dev_index(*_mesh_pos())
    lands = [lax.dynamic_update_index_in_dim(lax.empty((N_DEV,) + s.shape, s.dtype), s, me, 0) for s in shards]

    def copies_of(ins):
        x, y, c = _mesh_pos()
        targets = [(x, y, 1 - c), (1 - x, y, c), (x, 1 - y, c), (1 - x, 1 - y, c)]
        out = []
        for a, land in enumerate(ins):
            blk = land.at[_dev_index(x, y, c)]
            for k in (1, 2, 3, 0):
                out.append((blk, blk, 4 * a + k, targets[k]))
        return out

    send_sems, recv_sems, lands = _split_start(lands, copies_of, 4 * len(shards), name)
    return dict(lands=lands, sems=(send_sems, recv_sems), fwd={})


def _gather_pass_on(state, which, name):
    def arrivals(ins, sems):
        x, y, c = _mesh_pos()
        chips = [(1 - x, y), (x, 1 - y), (1 - x, 1 - y)]
        out = []
        for i, a in enumerate(which):
            for j, (px, py) in enumerate(chips):
                blk = ins[i].at[_dev_index(px, py, c)]
                out.append(("recv", blk, blk, sems[0][0].at[4 * a + 1 + j], sems[0][1].at[4 * a + 1 + j]))
        return out

    bufs = _split_wait([state["lands"][a] for a in which], [state["sems"]], arrivals, name + "_arrived")

    def copies_of(ins):
        x, y, c = _mesh_pos()
        chips = [(1 - x, y), (x, 1 - y), (1 - x, 1 - y)]
        out = []
        for i in range(len(which)):
            for j, (px, py) in enumerate(chips):
                blk = ins[i].at[_dev_index(px, py, c)]
                out.append((blk, blk, 3 * i + j, (x, y, 1 - c)))
        return out

    send_sems, recv_sems, bufs = _split_start(bufs, copies_of, 3 * len(which), name)
    for i, a in enumerate(which):
        state["lands"][a] = bufs[i]
    state["fwd"][tuple(which)] = (send_sems, recv_sems)


def _gather_end(state, which, name):
    def waits(ins, sems):
        x, y, c = _mesh_pos()
        chips = [(1 - x, y), (x, 1 - y), (1 - x, 1 - y)]
        (s_send, s_recv), (f_send, f_recv) = sems
        out = []
        for i, a in enumerate(which):
            mine = ins[i].at[_dev_index(x, y, c)]
            sib = ins[i].at[_dev_index(x, y, 1 - c)]
            out.append(("recv", sib, sib, s_send.at[4 * a], s_recv.at[4 * a]))
            for j, (px, py) in enumerate(chips):
                theirs = ins[i].at[_dev_index(px, py, 1 - c)]
                out.append(("recv", theirs, theirs, f_send.at[3 * i + j], f_recv.at[3 * i + j]))
            for k in range(4):
                out.append(("send", mine, mine, s_send.at[4 * a + k], s_recv.at[4 * a + k]))
            for j, (px, py) in enumerate(chips):
                passed = ins[i].at[_dev_index(px, py, c)]
                out.append(("send", passed, passed, f_send.at[3 * i + j], f_recv.at[3 * i + j]))
        return out

    bufs = _split_wait([state["lands"][a] for a in which], [state["sems"], state["fwd"][tuple(which)]], waits, name)
    for i, a in enumerate(which):
        state["lands"][a] = bufs[i]
    return bufs


def _sibling_exchange_begin(part, name):
    land = lax.empty((4,) + part.shape[1:], part.dtype)

    def copies_of(ins):
        x, y, c = _mesh_pos()
        return [(ins[0].at[2 * j + (1 - c)], ins[1].at[j], j, (x, y, 1 - c)) for j in range(4)]

    send_sems, recv_sems, bufs = _split_start([part, land], copies_of, 4, name)
    return dict(bufs=bufs, sems=(send_sems, recv_sems))


def _sibling_exchange_end(state, name):
    def waits(ins, sems):
        _, _, c = _mesh_pos()
        out = []
        for j in range(4):
            for kind in ("send", "recv"):
                out.append((kind, ins[0].at[2 * j + (1 - c)], ins[1].at[j], sems[0][0].at[j], sems[0][1].at[j]))
        return out

    return _split_wait(state["bufs"], [state["sems"]], waits, name)


def _chip_exchange_begin(csum, name):
    land = lax.empty((3,) + csum.shape[1:], csum.dtype)

    def copies_of(ins):
        x, y, c = _mesh_pos()
        chips = [(1 - x, y), (x, 1 - y), (1 - x, 1 - y)]
        return [(ins[0].at[2 * px + py], ins[1].at[r], r, (px, py, c)) for r, (px, py) in enumerate(chips)]

    send_sems, recv_sems, bufs = _split_start([csum, land], copies_of, 3, name)
    return dict(bufs=bufs, sems=(send_sems, recv_sems))


def _chip_exchange_end(state, name):
    def waits(ins, sems):
        x, y, _ = _mesh_pos()
        chips = [(1 - x, y), (x, 1 - y), (1 - x, 1 - y)]
        out = []
        for r, (px, py) in enumerate(chips):
            for kind in ("send", "recv"):
                out.append((kind, ins[0].at[2 * px + py], ins[1].at[r], sems[0][0].at[r], sems[0][1].at[r]))
        return out

    return _split_wait(state["bufs"], [state["sems"]], waits, name)


def _chip_sum(part, recv, name):
    _, R, C = part.shape
    tr = _tile(R, 512, 16)
    c_idx = lax.axis_index("c").astype(jnp.int32).reshape((1,))

    def body(c_ref, p_ref, r_ref, o_ref):
        o_ref[...] = (p_ref[...].astype(F32) + r_ref[...].astype(F32)).astype(o_ref.dtype)

    grid_spec = pltpu.PrefetchScalarGridSpec(
        num_scalar_prefetch=1, grid=(4, R // tr),
        in_specs=[pl.BlockSpec((None, tr, C), lambda j, i, c_ref: (2 * j + c_ref[0], i, 0)),
                  pl.BlockSpec((None, tr, C), lambda j, i, c_ref: (j, i, 0))],
        out_specs=pl.BlockSpec((None, tr, C), lambda j, i, c_ref: (j, i, 0)))
    return pl.pallas_call(body, name=name, grid_spec=grid_spec, out_shape=SDS((4, R, C), part.dtype),
                          compiler_params=_params(2))(c_idx, part, recv)


def _bias_fwd(table_t, onehot_t):
    H = table_t.shape[0]
    n = onehot_t.shape[1]

    def body(t_ref, oh_ref, o_ref):
        hi, mid, lo = _split3(t_ref[...])
        oh = oh_ref[...]
        o_ref[...] = _dot(hi, oh, NN) + _dot(mid, oh, NN) + _dot(lo, oh, NN)

    return _CHAIN.call(body, name="bias_fwd", in_specs=[VMEM_SPEC] * 2, out_specs=VMEM_SPEC, out_shape=SDS((H, n), F32),
                       compiler_params=_params(0))(table_t, onehot_t)


def _inproj_fwd(x, g, w):
    T, D = x.shape
    P = w.shape[1]
    tm, tn = _tile(T, 512), _tile(P, 1792)

    def body(x_ref, g_ref, w_ref, proj_ref, n_ref, nbuf):
        @pl.when(pl.program_id(1) == 0)
        def _():
            xv = x_ref[...]
            n = (xv * _rms_stats(xv) * g_ref[...]).astype(BF16)
            nbuf[...] = n
            n_ref[...] = n

        proj_ref[...] = _dot(nbuf[...], w_ref[...], NN)

    return _CHAIN.call(
        body, name="inproj_fwd", grid=(T // tm, P // tn),
        in_specs=[pl.BlockSpec((tm, D), lambda i, j: (i, 0)), pl.BlockSpec((1, D), lambda i, j: (0, 0)),
                  pl.BlockSpec((D, tn), lambda i, j: (0, j))],
        out_specs=[pl.BlockSpec((tm, tn), lambda i, j: (i, j)), pl.BlockSpec((tm, D), lambda i, j: (i, 0))],
        out_shape=[SDS((T, P), F32), SDS((T, D), BF16)],
        scratch_shapes=[pltpu.VMEM((tm, D), BF16)], compiler_params=_params(2))(x, g, w)


def _layer_norm_group(vg, lg, lb):
    mu = jnp.mean(vg, axis=-1, keepdims=True)
    xc = vg - mu
    rstd = lax.rsqrt(jnp.mean(xc * xc, axis=-1, keepdims=True) + EPS)
    vhat = xc * rstd
    return vhat, rstd, vhat * lg + lb


def _gmlp_fwd(proj, lg, lb, w_s, bs_t, A):
    T = proj.shape[0]
    G = A // GROUP_DIM
    tm = _tile(T, 512)
    nc = tm // CHUNK

    def body(u_ref, v_ref, lg_ref, lb_ref, w_ref, bst_ref, a_ref):
        row = lax.broadcasted_iota(jnp.int32, (CHUNK, CHUNK), 0)
        col = lax.broadcasted_iota(jnp.int32, (CHUNK, CHUNK), 1)
        causal = row >= col
        for g in range(G):
            sl = slice(g * GROUP_DIM, (g + 1) * GROUP_DIM)
            _, _, vn = _layer_norm_group(_gelu(v_ref[:, sl]), lg_ref[:, sl], lb_ref[:, sl])
            vnb = vn.astype(BF16)
            wm = jnp.where(causal, w_ref[g], 0.0).astype(BF16)
            ug = _gelu(u_ref[:, sl])
            bcol = bst_ref[:, g:g + 1]
            for c in range(nc):
                rs = slice(c * CHUNK, (c + 1) * CHUNK)
                a_ref[rs, sl] = ug[rs] * (_dot(wm, vnb[rs], NN) + bcol)

    return _CHAIN.call(
        body, name="gmlp_fwd", grid=(T // tm,),
        in_specs=[pl.BlockSpec((tm, A), lambda i: (i, 0)), pl.BlockSpec((tm, A), lambda i: (i, 1)),
                  pl.BlockSpec((1, A), lambda i: (0, 0)), pl.BlockSpec((1, A), lambda i: (0, 0)),
                  pl.BlockSpec((G, CHUNK, CHUNK), lambda i: (0, 0, 0)), pl.BlockSpec((CHUNK, G), lambda i: (0, 0))],
        out_specs=pl.BlockSpec((tm, A), lambda i: (i, 0)),
        out_shape=SDS((T, A), F32), compiler_params=_params(1))(proj, proj, lg, lb, w_s, bs_t)


def _attn_masks(first_tile):
    ii = lax.broadcasted_iota(jnp.int32, (CHUNK, 2 * CHUNK), 0)
    jj = lax.broadcasted_iota(jnp.int32, (CHUNK, 2 * CHUNK), 1)
    in_window = (jj > ii) & (jj <= ii + CHUNK)
    first_mask = in_window & jnp.logical_or(jnp.logical_not(first_tile), jj >= CHUNK)
    return in_window, first_mask


def _attn_probs(qh, kb, bias_h, mask, sink):
    s = _dot(qh, kb, NT) * (HEAD_DIM ** -0.5) + bias_h
    s = jnp.where(mask, s, NEG)
    m = jnp.maximum(jnp.max(s, axis=-1, keepdims=True), sink)
    p = jnp.exp(s - m)
    e_sink = jnp.exp(sink - m)
    inv = 1.0 / (jnp.sum(p, axis=-1, keepdims=True) + e_sink)
    return p * inv, e_sink * inv


def _attn_specs(tq, A, B, reverse_tiles=None):
    nb = tq // CHUNK
    kcol = (2 * A + B) // LANE
    if reverse_tiles is None:
        tile = lambda i: i
    else:
        tile = lambda i: reverse_tiles - 1 - i
    prev = lambda i: jnp.maximum(tile(i) * nb - 1, 0)
    return [pl.BlockSpec((tq, B), lambda i: (tile(i), 2 * A // B)),
            pl.BlockSpec((tq, LANE), lambda i: (tile(i), kcol)),
            pl.BlockSpec((tq, LANE), lambda i: (tile(i), kcol + 1)),
            pl.BlockSpec((CHUNK, LANE), lambda i: (prev(i), kcol)),
            pl.BlockSpec((CHUNK, LANE), lambda i: (prev(i), kcol + 1))]


def _attn_fwd(proj, bias, sinks, A, B):
    T = proj.shape[0]
    H = B // HEAD_DIM
    qpk = H // KV_HEADS
    tq = _tile(T, 512)
    nb = tq // CHUNK

    def body(sink_ref, q_ref, k_ref, v_ref, kp_ref, vp_ref, bias_ref, o_ref):
        in_window, first_mask = _attn_masks(pl.program_id(0) == 0)
        for b in range(nb):
            rows = slice(b * CHUNK, (b + 1) * CHUNK)
            if b == 0:
                kprev, vprev, mask = kp_ref[...], vp_ref[...], first_mask
            else:
                prows = slice((b - 1) * CHUNK, b * CHUNK)
                kprev, vprev, mask = k_ref[prows, :], v_ref[prows, :], in_window
            kband = jnp.concatenate([kprev, k_ref[rows, :]], axis=0).astype(BF16)
            vband = jnp.concatenate([vprev, v_ref[rows, :]], axis=0).astype(BF16)
            for h in range(H):
                ks = slice((h // qpk) * HEAD_DIM, (h // qpk + 1) * HEAD_DIM)
                hs = slice(h * HEAD_DIM, (h + 1) * HEAD_DIM)
                pn, _ = _attn_probs(q_ref[rows, hs].astype(BF16), kband[:, ks], bias_ref[h], mask, sink_ref[h])
                o_ref[rows, hs] = _dot(pn.astype(BF16), vband[:, ks], NN)

    return _CHAIN.call(
        body, name="attn_fwd", grid=(T // tq,),
        in_specs=[pl.BlockSpec(memory_space=pltpu.SMEM)] + _attn_specs(tq, A, B)
        + [pl.BlockSpec((H, CHUNK, 2 * CHUNK), lambda i: (0, 0, 0))],
        out_specs=pl.BlockSpec((tq, B), lambda i: (i, 0)),
        out_shape=SDS((T, B), F32), compiler_params=_params(1))(sinks, proj, proj, proj, proj, proj, bias)


def _outproj_fwd(a, b, ga, gb, x, w):
    T, A = a.shape
    B = b.shape[1]
    D = x.shape[1]
    tm, tn = _tile(T, 512), _tile(D, 1024)

    def body(a_ref, b_ref, ga_ref, gb_ref, x_ref, w_ref, h_ref, mix_ref, mbuf):
        @pl.when(pl.program_id(1) == 0)
        def _():
            av, bv = a_ref[...], b_ref[...]
            mbuf[:, :A] = (av * _rms_stats(av) * ga_ref[...]).astype(BF16)
            mbuf[:, A:] = (bv * _rms_stats(bv) * gb_ref[...]).astype(BF16)
            mix_ref[...] = mbuf[...]

        h_ref[...] = x_ref[...] + _dot(mbuf[...], w_ref[...], NN)

    return _CHAIN.call(
        body, name="outproj_fwd", grid=(T // tm, D // tn),
        in_specs=[pl.BlockSpec((tm, A), lambda i, j: (i, 0)), pl.BlockSpec((tm, B), lambda i, j: (i, 0)),
                  pl.BlockSpec((1, A), lambda i, j: (0, 0)), pl.BlockSpec((1, B), lambda i, j: (0, 0)),
                  pl.BlockSpec((tm, tn), lambda i, j: (i, j)), pl.BlockSpec((A + B, tn), lambda i, j: (0, j))],
        out_specs=[pl.BlockSpec((tm, tn), lambda i, j: (i, j)), pl.BlockSpec((tm, A + B), lambda i, j: (i, 0))],
        out_shape=[SDS((T, D), F32), SDS((T, A + B), BF16)],
        scratch_shapes=[pltpu.VMEM((tm, A + B), BF16)], compiler_params=_params(2))(a, b, ga, gb, x, w)


def _ffn_up(h1, g, w_up):
    T, D = h1.shape
    Fb = w_up.shape[2]
    F = N_DEV * Fb
    tm, tf = _tile(T, 1024), _tile(Fb, 1024)
    per = Fb // tf

    def body(h_ref, g_ref, wu_ref, z_ref, n_ref, nbuf):
        @pl.when(pl.program_id(1) == 0)
        def _():
            hv = h_ref[...]
            n = (hv * _rms_stats(hv) * g_ref[...]).astype(BF16)
            nbuf[...] = n
            n_ref[...] = n

        z_ref[...] = jnp.maximum(_dot(nbuf[...], wu_ref[...], NN), 0.0).astype(BF16)

    return _CHAIN.call(
        body, name="ffn_up", grid=(T // tm, F // tf),
        in_specs=[pl.BlockSpec((tm, D), lambda i, j: (i, 0)), pl.BlockSpec((1, D), lambda i, j: (0, 0)),
                  pl.BlockSpec((None, D, tf), lambda i, j: (j // per, 0, j % per))],
        out_specs=[pl.BlockSpec((tm, tf), lambda i, j: (i, j)), pl.BlockSpec((tm, D), lambda i, j: (i, 0))],
        out_shape=[SDS((T, F), BF16), SDS((T, D), BF16)],
        scratch_shapes=[pltpu.VMEM((tm, D), BF16)], compiler_params=_params(2))(h1, g, w_up)


def _ffn_down(h1, z, w_down):
    T, D = h1.shape
    F = w_down.shape[0]
    tm, tk = _tile(T, 1024), _tile(F, 1024)
    nk = F // tk

    def body(h_ref, z_ref, wd_ref, h2_ref):
        k = pl.program_id(1)

        @pl.when(k == 0)
        def _():
            h2_ref[...] = h_ref[...]

        zf = z_ref[...].astype(F32)
        h2_ref[...] += _dot((zf * zf).astype(BF16), wd_ref[...], NN)

    return _CHAIN.call(
        body, name="ffn_down", grid=(T // tm, nk),
        in_specs=[pl.BlockSpec((tm, D), lambda i, k: (i, 0)), pl.BlockSpec((tm, tk), lambda i, k: (i, k)),
                  pl.BlockSpec((tk, D), lambda i, k: (k, 0))],
        out_specs=pl.BlockSpec((tm, D), lambda i, k: (i, 0)),
        out_shape=SDS((T, D), F32), compiler_params=_params(2))(h1, z, w_down)


def _final_loss(h2, g, target):
    T, D = h2.shape
    tm = _tile(T, 512)

    def body(h_ref, g_ref, t_ref, loss_ref, dg_ref, dh_ref, dhb_ref):
        @pl.when(pl.program_id(0) == 0)
        def _():
            loss_ref[...] = jnp.zeros_like(loss_ref)
            dg_ref[...] = jnp.zeros_like(dg_ref)

        hv, gv = h_ref[...], g_ref[...]
        r = _rms_stats(hv)
        hn = hv * r
        e = hn * gv - t_ref[...]
        loss_ref[...] += (0.5 / D) * jnp.sum(jnp.sum(e * e, axis=-1, keepdims=True), axis=0, keepdims=True)
        dy = e * (1.0 / D)
        dg_ref[...] += jnp.sum(dy * hn, axis=0, keepdims=True)
        dh = _rms_bwd(dy, hv, r, gv)
        dh_ref[...] = dh
        dhb_ref[...] = dh.astype(BF16)

    return _CHAIN.call(
        body, name="final_loss", grid=(T // tm,),
        in_specs=[pl.BlockSpec((tm, D), lambda i: (i, 0)), pl.BlockSpec((1, D), lambda i: (0, 0)),
                  pl.BlockSpec((tm, D), lambda i: (i, 0))],
        out_specs=[pl.BlockSpec((1, 1), lambda i: (0, 0)), pl.BlockSpec((1, D), lambda i: (0, 0)),
                   pl.BlockSpec((tm, D), lambda i: (i, 0)), pl.BlockSpec((tm, D), lambda i: (i, 0))],
        out_shape=[SDS((1, 1), F32), SDS((1, D), F32), SDS((T, D), F32), SDS((T, D), BF16)],
        compiler_params=_params(1))(h2, g, target)


def _ffn_bwd(dh2, dh2b, z, h1, g, w_up, w_down):
    T, D = h1.shape
    Fb = w_up.shape[2]
    F = N_DEV * Fb
    tm, tf = _tile(T, 512), _tile(Fb, 512)
    per = Fb // tf
    nj = F // tf

    def body(dh_ref, dhb_ref, z_ref, h_ref, g_ref, wu_ref, wd_ref, dzp_ref, dh1_ref, dh1b_ref, dg_ref, acc):
        i, j = pl.program_id(0), pl.program_id(1)

        @pl.when(j == 0)
        def _():
            acc[...] = jnp.zeros_like(acc)

        @pl.when((i == 0) & (j == 0))
        def _():
            dg_ref[...] = jnp.zeros_like(dg_ref)

        dzz = _dot(dhb_ref[...], wd_ref[...], NT)
        dzp = (dzz * (2.0 * z_ref[...].astype(F32))).astype(BF16)
        dzp_ref[...] = dzp
        acc[...] += _dot(dzp, wu_ref[...], NT)

        @pl.when(j == nj - 1)
        def _():
            hv, gv, dn = h_ref[...], g_ref[...], acc[...]
            r = _rms_stats(hv)
            dg_ref[...] += jnp.sum(dn * (hv * r), axis=0, keepdims=True)
            dh1 = dh_ref[...] + _rms_bwd(dn, hv, r, gv)
            dh1_ref[...] = dh1
            dh1b_ref[...] = dh1.astype(BF16)

    return _CHAIN.call(
        body, name="ffn_bwd", grid=(T // tm, nj),
        in_specs=[pl.BlockSpec((tm, D), lambda i, j: (i, 0)), pl.BlockSpec((tm, D), lambda i, j: (i, 0)),
                  pl.BlockSpec((tm, tf), lambda i, j: (i, j)), pl.BlockSpec((tm, D), lambda i, j: (i, 0)),
                  pl.BlockSpec((1, D), lambda i, j: (0, 0)),
                  pl.BlockSpec((None, D, tf), lambda i, j: (j // per, 0, j % per)),
                  pl.BlockSpec((None, tf, D), lambda i, j: (j // per, j % per, 0))],
        out_specs=[pl.BlockSpec((tm, tf), lambda i, j: (i, j)), pl.BlockSpec((tm, D), lambda i, j: (i, 0)),
                   pl.BlockSpec((tm, D), lambda i, j: (i, 0)), pl.BlockSpec((1, D), lambda i, j: (0, 0))],
        out_shape=[SDS((T, F), BF16), SDS((T, D), F32), SDS((T, D), BF16), SDS((1, D), F32)],
        scratch_shapes=[pltpu.VMEM((tm, D), F32)], compiler_params=_params(2))(dh2, dh2b, z, h1, g, w_up, w_down)


def _matmul_tn(a, b, name, square_a=False, col_blocks=None):
    T, K = a.shape
    N = b.shape[1]
    tt, tk = _tile(T, 1024), _tile(K, 1024)
    tn = _tile(N if col_blocks is None else N // col_blocks, 1792)
    nt = T // tt

    def body(a_ref, b_ref, o_ref, acc):
        t = pl.program_id(2)

        @pl.when(t == 0)
        def _():
            acc[...] = jnp.zeros_like(acc)

        av = a_ref[...]
        if square_a:
            af = av.astype(F32)
            av = (af * af).astype(BF16)
        acc[...] += _dot(av, b_ref[...], TN)

        @pl.when(t == nt - 1)
        def _():
            o_ref[...] = acc[...].astype(o_ref.dtype)

    if col_blocks is None:
        out_shape = SDS((K, N), BF16)
        out_spec = pl.BlockSpec((tk, tn), lambda i, j, t: (i, j))
    else:
        per = (N // col_blocks) // tn
        out_shape = SDS((col_blocks, K, N // col_blocks), BF16)
        out_spec = pl.BlockSpec((None, tk, tn), lambda i, j, t: (j // per, i, j % per))
    return _CHAIN.call(
        body, name=name, grid=(K // tk, N // tn, nt),
        in_specs=[pl.BlockSpec((tt, tk), lambda i, j, t: (t, i)), pl.BlockSpec((tt, tn), lambda i, j, t: (t, j))],
        out_specs=out_spec, out_shape=out_shape,
        scratch_shapes=[pltpu.VMEM((tk, tn), F32)], compiler_params=_params(3))(a, b)


def _outproj_bwd(dh1b, w, a, b, ga, gb):
    T, D = dh1b.shape
    A, B = a.shape[1], b.shape[1]
    tm = _tile(T, 512)

    def body(dh_ref, w_ref, a_ref, b_ref, ga_ref, gb_ref, da_ref, db_ref, dga_ref, dgb_ref):
        @pl.when(pl.program_id(0) == 0)
        def _():
            dga_ref[...] = jnp.zeros_like(dga_ref)
            dgb_ref[...] = jnp.zeros_like(dgb_ref)

        dmix = _dot(dh_ref[...], w_ref[...], NT)
        for src_ref, g_ref, dx_ref, dg_ref, dn in ((a_ref, ga_ref, da_ref, dga_ref, dmix[:, :A]),
                                                   (b_ref, gb_ref, db_ref, dgb_ref, dmix[:, A:])):
            xv = src_ref[...]
            r = _rms_stats(xv)
            dg_ref[...] += jnp.sum(dn * (xv * r), axis=0, keepdims=True)
            dx_ref[...] = _rms_bwd(dn, xv, r, g_ref[...])

    return _CHAIN.call(
        body, name="outproj_bwd", grid=(T // tm,),
        in_specs=[pl.BlockSpec((tm, D), lambda i: (i, 0)), pl.BlockSpec((A + B, D), lambda i: (0, 0)),
                  pl.BlockSpec((tm, A), lambda i: (i, 0)), pl.BlockSpec((tm, B), lambda i: (i, 0)),
                  pl.BlockSpec((1, A), lambda i: (0, 0)), pl.BlockSpec((1, B), lambda i: (0, 0))],
        out_specs=[pl.BlockSpec((tm, A), lambda i: (i, 0)), pl.BlockSpec((tm, B), lambda i: (i, 0)),
                   pl.BlockSpec((1, A), lambda i: (0, 0)), pl.BlockSpec((1, B), lambda i: (0, 0))],
        out_shape=[SDS((T, A), F32), SDS((T, B), F32), SDS((1, A), F32), SDS((1, B), F32)],
        compiler_params=_params(1))(dh1b, w, a, b, ga, gb)


def _gmlp_bwd(proj, da, lg, lb, w_s, w_st, bs_t, A):
    T = proj.shape[0]
    G = A // GROUP_DIM
    tm = _tile(T, 512)
    nc = tm // CHUNK

    def body(u_ref, v_ref, da_ref, lg_ref, lb_ref, w_ref, wt_ref, bst_ref, duv_ref, dlg_ref, dlb_ref, dw_ref, dbs_ref):
        @pl.when(pl.program_id(0) == 0)
        def _():
            dlg_ref[...] = jnp.zeros_like(dlg_ref)
            dlb_ref[...] = jnp.zeros_like(dlb_ref)
            dw_ref[...] = jnp.zeros_like(dw_ref)
            dbs_ref[...] = jnp.zeros_like(dbs_ref)

        row = lax.broadcasted_iota(jnp.int32, (CHUNK, CHUNK), 0)
        col = lax.broadcasted_iota(jnp.int32, (CHUNK, CHUNK), 1)
        lower = row >= col
        upper = row <= col
        for g in range(G):
            sl = slice(g * GROUP_DIM, (g + 1) * GROUP_DIM)
            lgv = lg_ref[:, sl]
            vg, vg_grad = _gelu_and_grad(v_ref[:, sl])
            vhat, rstd, vn = _layer_norm_group(vg, lgv, lb_ref[:, sl])
            vnb = vn.astype(BF16)
            ug, ug_grad = _gelu_and_grad(u_ref[:, sl])
            dav = da_ref[:, sl]
            wm = jnp.where(lower, w_ref[g], 0.0).astype(BF16)
            wmt = jnp.where(upper, wt_ref[g], 0.0).astype(BF16)
            bcol = bst_ref[:, g:g + 1]
            dw_acc = jnp.zeros((CHUNK, CHUNK), F32)
            dbs_acc = jnp.zeros((CHUNK, 1), F32)
            dvn_parts = []
            dug_parts = []
            for c in range(nc):
                rs = slice(c * CHUNK, (c + 1) * CHUNK)
                mixed = _dot(wm, vnb[rs], NN) + bcol
                dug_parts.append(dav[rs] * mixed)
                dmix = dav[rs] * ug[rs]
                dbs_acc = dbs_acc + jnp.sum(dmix, axis=-1, keepdims=True)
                dmixb = dmix.astype(BF16)
                dw_acc = dw_acc + _dot(dmixb, vnb[rs], NT)
                dvn_parts.append(_dot(wmt, dmixb, NN))
            dvn = jnp.concatenate(dvn_parts, axis=0)
            dug = jnp.concatenate(dug_parts, axis=0)
            dw_ref[g] += jnp.where(lower, dw_acc, 0.0)
            dbs_ref[:, g:g + 1] += dbs_acc
            dlg_ref[:, sl] += jnp.sum(dvn * vhat, axis=0, keepdims=True)
            dlb_ref[:, sl] += jnp.sum(dvn, axis=0, keepdims=True)
            dvhat = dvn * lgv
            dvg = rstd * (dvhat - jnp.mean(dvhat, axis=-1, keepdims=True)
                          - vhat * jnp.mean(dvhat * vhat, axis=-1, keepdims=True))
            duv_ref[:, sl] = (dug * ug_grad).astype(BF16)
            duv_ref[:, A + g * GROUP_DIM:A + (g + 1) * GROUP_DIM] = (dvg * vg_grad).astype(BF16)

    return _CHAIN.call(
        body, name="gmlp_bwd", grid=(T // tm,),
        in_specs=[pl.BlockSpec((tm, A), lambda i: (i, 0)), pl.BlockSpec((tm, A), lambda i: (i, 1)),
                  pl.BlockSpec((tm, A), lambda i: (i, 0)),
                  pl.BlockSpec((1, A), lambda i: (0, 0)), pl.BlockSpec((1, A), lambda i: (0, 0)),
                  pl.BlockSpec((G, CHUNK, CHUNK), lambda i: (0, 0, 0)),
                  pl.BlockSpec((G, CHUNK, CHUNK), lambda i: (0, 0, 0)), pl.BlockSpec((CHUNK, G), lambda i: (0, 0))],
        out_specs=[pl.BlockSpec((tm, 2 * A), lambda i: (i, 0)),
                   pl.BlockSpec((1, A), lambda i: (0, 0)), pl.BlockSpec((1, A), lambda i: (0, 0)),
                   pl.BlockSpec((G, CHUNK, CHUNK), lambda i: (0, 0, 0)), pl.BlockSpec((CHUNK, G), lambda i: (0, 0))],
        out_shape=[SDS((T, 2 * A), BF16), SDS((1, A), F32), SDS((1, A), F32),
                   SDS((G, CHUNK, CHUNK), F32), SDS((CHUNK, G), F32)],
        compiler_params=_params(1))(proj, proj, da, lg, lb, w_s, w_st, bs_t)


def _attn_bwd(proj, o, do, duv, bias, sinks, A, B):
    T, P = proj.shape
    H = B // HEAD_DIM
    qpk = H // KV_HEADS
    tq = _tile(T, 512)
    nb = tq // CHUNK
    n_tiles = T // tq
    scale = HEAD_DIM ** -0.5
    rev = lambda i: n_tiles - 1 - i

    def body(sink_ref, q_ref, k_ref, v_ref, kp_ref, vp_ref, o_ref, do_ref, duv_ref, bias_ref,
             dproj_ref, dbias_ref, dsink_ref, carry, dkv, sacc):
        step = pl.program_id(0)

        @pl.when(step == 0)
        def _():
            carry[...] = jnp.zeros_like(carry)
            sacc[...] = jnp.zeros_like(sacc)
            dbias_ref[...] = jnp.zeros_like(dbias_ref)

        in_window, first_mask = _attn_masks(step == n_tiles - 1)
        dproj_ref[:, :2 * A] = duv_ref[...]
        dkv[...] = jnp.zeros_like(dkv)
        for b in range(nb):
            rows = slice(b * CHUNK, (b + 1) * CHUNK)
            band = slice(b * CHUNK, (b + 2) * CHUNK)
            if b == 0:
                kprev, vprev, mask = kp_ref[...], vp_ref[...], first_mask
            else:
                prows = slice((b - 1) * CHUNK, b * CHUNK)
                kprev, vprev, mask = k_ref[prows, :], v_ref[prows, :], in_window
            kband = jnp.concatenate([kprev, k_ref[rows, :]], axis=0).astype(BF16)
            vband = jnp.concatenate([vprev, v_ref[rows, :]], axis=0).astype(BF16)
            for kv in range(KV_HEADS):
                ks = slice(kv * HEAD_DIM, (kv + 1) * HEAD_DIM)
                kb, vb = kband[:, ks], vband[:, ks]
                dk_acc = jnp.zeros((2 * CHUNK, HEAD_DIM), F32)
                dv_acc = jnp.zeros((2 * CHUNK, HEAD_DIM), F32)
                for h in range(kv * qpk, (kv + 1) * qpk):
                    hs = slice(h * HEAD_DIM, (h + 1) * HEAD_DIM)
                    qh = q_ref[rows, hs].astype(BF16)
                    pn, p_sink = _attn_probs(qh, kb, bias_ref[h], mask, sink_ref[h])
                    doh = do_ref[rows, hs]
                    delta = jnp.sum(doh * o_ref[rows, hs], axis=-1, keepdims=True)
                    dohb = doh.astype(BF16)
                    ds = pn * (_dot(dohb, vb, NT) - delta)
                    dbias_ref[h] += ds
                    sacc[:, h:h + 1] += -(p_sink * delta)
                    dsb = ds.astype(BF16)
                    dproj_ref[rows, 2 * A + h * HEAD_DIM:2 * A + (h + 1) * HEAD_DIM] = (
                        _dot(dsb, kb, NN) * scale).astype(BF16)
                    dk_acc = dk_acc + _dot(dsb, qh, TN)
                    dv_acc = dv_acc + _dot(pn.astype(BF16), dohb, TN)
                dkv[band, ks] += dk_acc * scale
                dkv[band, LANE + kv * HEAD_DIM:LANE + (kv + 1) * HEAD_DIM] += dv_acc
        last = slice(tq, tq + CHUNK)
        dkv[last, :] += carry[...]
        dproj_ref[:, 2 * A + B:] = dkv[CHUNK:, :].astype(BF16)
        carry[...] = dkv[:CHUNK, :]

        @pl.when(step == n_tiles - 1)
        def _():
            dsink_ref[...] = jnp.sum(sacc[...], axis=0, keepdims=True)

    specs = _attn_specs(tq, A, B, reverse_tiles=n_tiles)
    return _CHAIN.call(
        body, name="attn_bwd", grid=(n_tiles,),
        in_specs=[pl.BlockSpec(memory_space=pltpu.SMEM)] + specs
        + [pl.BlockSpec((tq, B), lambda i: (rev(i), 0)), pl.BlockSpec((tq, B), lambda i: (rev(i), 0)),
           pl.BlockSpec((tq, 2 * A), lambda i: (rev(i), 0)),
           pl.BlockSpec((H, CHUNK, 2 * CHUNK), lambda i: (0, 0, 0))],
        out_specs=[pl.BlockSpec((tq, P), lambda i: (rev(i), 0)),
                   pl.BlockSpec((H, CHUNK, 2 * CHUNK), lambda i: (0, 0, 0)), pl.BlockSpec((1, H), lambda i: (0, 0))],
        out_shape=[SDS((T, P), BF16), SDS((H, CHUNK, 2 * CHUNK), F32), SDS((1, H), F32)],
        scratch_shapes=[pltpu.VMEM((CHUNK, 2 * LANE), F32), pltpu.VMEM((tq + CHUNK, 2 * LANE), F32),
                        pltpu.VMEM((CHUNK, H), F32)],
        compiler_params=_params(1))(sinks, proj, proj, proj, proj, proj, o, do, duv, bias)


def _bias_bwd(dbias, onehot):
    H = dbias.shape[0]
    nbk = onehot.shape[1]

    def body(d_ref, oh_ref, o_ref):
        hi, mid, lo = _split3(d_ref[...])
        oh = oh_ref[...]
        o_ref[...] = _dot(hi, oh, NN) + _dot(mid, oh, NN) + _dot(lo, oh, NN)

    return _CHAIN.call(body, name="bias_bwd", in_specs=[VMEM_SPEC] * 2, out_specs=VMEM_SPEC, out_shape=SDS((H, nbk), F32),
                       compiler_params=_params(0))(dbias, onehot)


def _inproj_bwd(dproj, w, x, dh1, g):
    T, P = dproj.shape
    D = x.shape[1]
    tm, tn = _tile(T, 512), _tile(P, 1792)
    nj = P // tn

    def body(dp_ref, w_ref, x_ref, dh_ref, g_ref, dx_ref, dg_ref, acc):
        i, j = pl.program_id(0), pl.program_id(1)

        @pl.when(j == 0)
        def _():
            acc[...] = jnp.zeros_like(acc)

        @pl.when((i == 0) & (j == 0))
        def _():
            dg_ref[...] = jnp.zeros_like(dg_ref)

        acc[...] += _dot(dp_ref[...], w_ref[...], NT)

        @pl.when(j == nj - 1)
        def _():
            xv, dn = x_ref[...], acc[...]
            r = _rms_stats(xv)
            dg_ref[...] += jnp.sum(dn * (xv * r), axis=0, keepdims=True)
            dx_ref[...] = dh_ref[...] + _rms_bwd(dn, xv, r, g_ref[...])

    return _CHAIN.call(
        body, name="inproj_bwd", grid=(T // tm, nj),
        in_specs=[pl.BlockSpec((tm, tn), lambda i, j: (i, j)), pl.BlockSpec((D, tn), lambda i, j: (0, j)),
                  pl.BlockSpec((tm, D), lambda i, j: (i, 0)), pl.BlockSpec((tm, D), lambda i, j: (i, 0)),
                  pl.BlockSpec((1, D), lambda i, j: (0, 0))],
        out_specs=[pl.BlockSpec((tm, D), lambda i, j: (i, 0)), pl.BlockSpec((1, D), lambda i, j: (0, 0))],
        out_shape=[SDS((T, D), F32), SDS((1, D), F32)],
        scratch_shapes=[pltpu.VMEM((tm, D), F32)], compiler_params=_params(2))(dproj, w, x, dh1, g)


def _adamw(w, g, m, v):
    m = ADAM_B1 * m + (1.0 - ADAM_B1) * g
    v = ADAM_B2 * v + (1.0 - ADAM_B2) * (g * g)
    m_hat = m / (1.0 - ADAM_B1 ** ADAM_STEP)
    v_hat = v / (1.0 - ADAM_B2 ** ADAM_STEP)
    delta = -ADAM_LR * (m_hat / (jnp.sqrt(v_hat) + ADAM_EPS) + ADAM_WD * w)
    return delta, m, v


def _adam_sharded(csum, recv, w, m, v, name):
    R, C = w.shape
    tr = _tile(R, 256, 16)
    own = (2 * lax.axis_index("x") + lax.axis_index("y")).astype(jnp.int32).reshape((1,))

    def body(own_idx, own_ref, recv_ref, w_ref, m_ref, v_ref, g_ref, d_ref, nm_ref, nv_ref):
        g = own_ref[...].astype(F32)
        for r in range(3):
            g = g + recv_ref[r].astype(F32)
        delta, nm, nv = _adamw(w_ref[...], g, m_ref[...], v_ref[...])
        g_ref[...] = g
        d_ref[...] = delta
        nm_ref[...] = nm
        nv_ref[...] = nv

    blk = pl.BlockSpec((tr, C), lambda i, own_idx: (i, 0))
    grid_spec = pltpu.PrefetchScalarGridSpec(
        num_scalar_prefetch=1, grid=(R // tr,),
        in_specs=[pl.BlockSpec((None, tr, C), lambda i, own_idx: (own_idx[0], i, 0)),
                  pl.BlockSpec((3, tr, C), lambda i, own_idx: (0, i, 0)), blk, blk, blk],
        out_specs=[blk] * 4)
    return pl.pallas_call(body, name=name, grid_spec=grid_spec, out_shape=[SDS((R, C), F32)] * 4,
                          compiler_params=_params(1))(own, csum, recv, w, m, v)


def _adam_small(gathered, w, m, v):
    R = w.shape[0]

    def body(p_ref, w_ref, m_ref, v_ref, g_ref, d_ref, nm_ref, nv_ref):
        g = p_ref[0]
        for d in range(1, N_DEV):
            g = g + p_ref[d]
        delta, nm, nv = _adamw(w_ref[...], g, m_ref[...], v_ref[...])
        g_ref[...] = g
        d_ref[...] = delta
        nm_ref[...] = nm
        nv_ref[...] = nv

    return _CHAIN.call(body, name="adam_small", in_specs=[VMEM_SPEC] * 4, out_specs=[VMEM_SPEC] * 4,
                       out_shape=[SDS((R, LANE), F32)] * 4,
                       compiler_params=_params(0))(gathered, w, m, v)


def _pack(arrays):
    tile = 8 * LANE
    pieces = []
    for a in arrays:
        flat = a.reshape(-1).astype(F32)
        pieces.append(jnp.pad(flat, (0, (-flat.size) % tile)))
    return jnp.concatenate(pieces).reshape(-1, LANE)


def _unpack(packed, shapes):
    tile = 8 * LANE
    flat = packed.reshape(-1)
    out, off = [], 0
    for s in shapes:
        size = int(np.prod(s))
        out.append(flat[off:off + size].reshape(s))
        off += size + (-size) % tile
    return out


def kernel(x, rel_bias_table, mix_norm_g, w_in, gate_norm_g, gate_norm_b, w_spatial, b_spatial, attn_sinks, out_norm_a_g, out_norm_b_g, w_out, ffn_norm_g, w_up, w_down, final_norm_g, loss_target, m_rel_bias_table, m_mix_norm_g, m_w_in, m_gate_norm_g, m_gate_norm_b, m_w_spatial, m_b_spatial, m_attn_sinks, m_out_norm_a_g, m_out_norm_b_g, m_w_out, m_ffn_norm_g, m_w_up, m_w_down, m_final_norm_g, v_rel_bias_table, v_mix_norm_g, v_w_in, v_gate_norm_g, v_gate_norm_b, v_w_spatial, v_b_spatial, v_attn_sinks, v_out_norm_a_g, v_out_norm_b_g, v_w_out, v_ffn_norm_g, v_w_up, v_w_down, v_final_norm_g):
    T, D = x.shape[1], x.shape[2]
    A = D // 2
    B = D // 2
    G = A // GROUP_DIM
    H = B // HEAD_DIM
    P = 2 * A + B + 2 * KV_HEADS * HEAD_DIM
    Pb = w_in.shape[2]
    xs = x.reshape(T, D)
    target = loss_target.reshape(T, D)

    shards = [w_in[0].astype(BF16), w_out[0].astype(BF16), w_up[0].astype(BF16), w_down[0].astype(BF16)]
    _CHAIN.token = None
    gather = _gather_begin(shards, "gather_start")
    _gather_pass_on(gather, [0], "gather_in_pass")
    (win_g,) = _gather_end(gather, [0], "gather_in_end")
    win_full = jnp.transpose(win_g, (1, 0, 2)).reshape(D, P)

    g1, g2, g3 = mix_norm_g.reshape(1, D), ffn_norm_g.reshape(1, D), final_norm_g.reshape(1, D)
    lg, lb = gate_norm_g.reshape(1, A), gate_norm_b.reshape(1, A)
    ws = w_spatial[0]
    ws_t = jnp.swapaxes(ws, 1, 2)
    bs_t = jnp.transpose(b_spatial[0])
    ga, gb = out_norm_a_g.reshape(1, A), out_norm_b_g.reshape(1, B)
    sinks = attn_sinks.reshape(H)
    bucket, in_window = _t5_bucket()
    onehot_np = ((bucket.reshape(-1, 1) == np.arange(N_BUCKETS)[None, :]) & in_window.reshape(-1, 1))
    onehot = jnp.asarray(onehot_np.astype(np.float32)).astype(BF16)

    bias = _bias_fwd(jnp.transpose(rel_bias_table), jnp.transpose(onehot)).reshape(H, CHUNK, 2 * CHUNK)
    proj, n1 = _inproj_fwd(xs, g1, win_full)
    _gather_pass_on(gather, [1], "gather_out_pass")
    a_out = _gmlp_fwd(proj, lg, lb, ws, bs_t, A)
    b_out = _attn_fwd(proj, bias, sinks, A, B)
    _gather_pass_on(gather, [2], "gather_up_pass")
    (wout_g,) = _gather_end(gather, [1], "gather_out_end")
    wout_full = wout_g.reshape(A + B, D)
    h1, mixed = _outproj_fwd(a_out, b_out, ga, gb, xs, wout_full)
    _gather_pass_on(gather, [3], "gather_down_pass")
    (wup_g,) = _gather_end(gather, [2], "gather_up_end")
    z, n2 = _ffn_up(h1, g2, wup_g)
    (wdown_g,) = _gather_end(gather, [3], "gather_down_end")
    h2 = _ffn_down(h1, z, wdown_g.reshape(-1, D))
    loss_part, dg3, dh2, dh2b = _final_loss(h2, g3, target)

    def reduce_to_chip(state, name):
        part, received = _sibling_exchange_end(state, name + "_sib_end")
        return _chip_exchange_begin(_chip_sum(part, received, name + "_chip_sum"), name + "_chip")

    dwdown = _matmul_tn(z, dh2b, "grad_w_down", square_a=True).reshape(wdown_g.shape)
    sib_down = _sibling_exchange_begin(dwdown, "rs_down_sib")
    dzp, dh1, dh1b, dg2 = _ffn_bwd(dh2, dh2b, z, h1, g2, wup_g, wdown_g)
    chip_down = reduce_to_chip(sib_down, "rs_down")
    dwup = _matmul_tn(n2, dzp, "grad_w_up", col_blocks=N_DEV)
    sib_up = _sibling_exchange_begin(dwup, "rs_up_sib")
    da, db, dga, dgb = _outproj_bwd(dh1b, wout_full, a_out, b_out, ga, gb)
    chip_up = reduce_to_chip(sib_up, "rs_up")
    dwout = _matmul_tn(mixed, dh1b, "grad_w_out").reshape(wout_g.shape)
    sib_out = _sibling_exchange_begin(dwout, "rs_out_sib")
    duv, dlg, dlb, dws, dbs_t = _gmlp_bwd(proj, da, lg, lb, ws, ws_t, bs_t, A)
    dproj, dbias, dsinks = _attn_bwd(proj, b_out, db, duv, bias, sinks, A, B)
    chip_out = reduce_to_chip(sib_out, "rs_out")
    dtable_t = _bias_bwd(dbias.reshape(H, -1), onehot)
    dwin = _matmul_tn(n1, dproj, "grad_w_in")
    dwin = jnp.transpose(dwin.reshape(D, N_DEV, Pb), (1, 0, 2))
    sib_in = _sibling_exchange_begin(dwin, "rs_in_sib")
    grad_x, dg1 = _inproj_bwd(dproj, win_full, xs, dh1, g1)
    chip_in = reduce_to_chip(sib_in, "rs_in")

    small_w = [rel_bias_table, mix_norm_g, gate_norm_g, gate_norm_b, w_spatial, b_spatial, attn_sinks,
               out_norm_a_g, out_norm_b_g, ffn_norm_g, final_norm_g]
    small_m = [m_rel_bias_table, m_mix_norm_g, m_gate_norm_g, m_gate_norm_b, m_w_spatial, m_b_spatial, m_attn_sinks,
               m_out_norm_a_g, m_out_norm_b_g, m_ffn_norm_g, m_final_norm_g]
    small_v = [v_rel_bias_table, v_mix_norm_g, v_gate_norm_g, v_gate_norm_b, v_w_spatial, v_b_spatial, v_attn_sinks,
               v_out_norm_a_g, v_out_norm_b_g, v_ffn_norm_g, v_final_norm_g]
    small_g = [jnp.transpose(dtable_t), dg1, dlg, dlb, dws, jnp.transpose(dbs_t), dsinks, dga, dgb, dg2, dg3]
    shapes = [w.shape for w in small_w]
    (gathered,) = _all_gather([_pack(small_g)], "gather_small_grads")
    sg, sd, sm, sv = [_unpack(o, shapes) for o in _adam_small(gathered, _pack(small_w), _pack(small_m), _pack(small_v))]

    big = [None] * 4
    for k, state, (w, m, v) in ((3, chip_down, (w_down, m_w_down, v_w_down)), (2, chip_up, (w_up, m_w_up, v_w_up)),
                                (1, chip_out, (w_out, m_w_out, v_w_out)), (0, chip_in, (w_in, m_w_in, v_w_in))):
        csum, received = _chip_exchange_end(state, "rs_%d_end" % k)
        outs = _adam_sharded(csum, received, w[0], m[0], v[0], "adam_%d" % k)
        big[k] = [o.reshape(w.shape) for o in outs]

    loss = lax.psum(loss_part[0, 0], ("x", "y", "c"))

    order = ["s0", "s1", "b0", "s2", "s3", "s4", "s5", "s6", "s7", "s8", "b1", "s9", "b2", "b3", "s10"]

    def group(idx):
        small = (sg, sd, sm, sv)[idx]
        return [small[int(t[1:])] if t[0] == "s" else big[int(t[1:])][idx] for t in order]

    return (loss, grad_x.reshape(x.shape), *group(0), *group(1), *group(2), *group(3))
```

```python
import functools
import math

import numpy as np
import jax
import jax.numpy as jnp
from jax import lax
from jax.experimental import pallas as pl
from jax.experimental.pallas import tpu as pltpu

F32 = jnp.float32
BF16 = jnp.bfloat16
SDS = jax.ShapeDtypeStruct
MESH = pl.DeviceIdType.MESH

N_DEV = 8
EPS = 1e-5
NEG = -1e30
CHUNK = 128
GROUP_DIM = 128
HEAD_DIM = 64
KV_HEADS = 2
N_BUCKETS = 32
MAX_DISTANCE = 128
ADAM_LR, ADAM_B1, ADAM_B2, ADAM_EPS, ADAM_WD, ADAM_STEP = 0.001, 0.9, 0.999, 1e-08, 0.01, 10
GELU_C0 = math.sqrt(2.0 / math.pi)
GELU_C1 = 0.044715

V7X_VMEM_BYTES = 64 * 1024 * 1024
VMEM_LIMIT = V7X_VMEM_BYTES - 8 * 1024 * 1024
LANE = 128

NN = ((1,), (0,))
NT = ((1,), (1,))
TN = ((0,), (0,))


def _dot(a, b, dims):
    return lax.dot_general(a, b, (dims, ((), ())), preferred_element_type=F32)


def _tile(n, pref, unit=LANE):
    best = None
    for t in range(unit, min(n, pref) + 1, unit):
        if n % t == 0:
            best = t
    return n if best is None else best


def _params(n_grid):
    return pltpu.CompilerParams(dimension_semantics=("arbitrary",) * n_grid, vmem_limit_bytes=VMEM_LIMIT)


def _resident(shape):
    return pl.BlockSpec(shape, lambda i: (0, 0), pipeline_mode=pl.Buffered(1))


def _gelu(x):
    return 0.5 * x * (1.0 + jnp.tanh(GELU_C0 * (x + GELU_C1 * x * x * x)))


def _gelu_and_grad(x):
    x2 = x * x
    t = jnp.tanh(GELU_C0 * x * (1.0 + GELU_C1 * x2))
    val = 0.5 * x * (1.0 + t)
    grad = 0.5 * (1.0 + t) + 0.5 * x * (1.0 - t * t) * (GELU_C0 * (1.0 + 3.0 * GELU_C1 * x2))
    return val, grad


def _rms_stats(x):
    return lax.rsqrt(jnp.mean(x * x, axis=-1, keepdims=True) + EPS)


def _rms_bwd(dy, x, r, g):
    w = dy * g
    return r * w - x * (r * r * r) * jnp.mean(w * x, axis=-1, keepdims=True)


def _t5_bucket():
    i = np.arange(CHUNK)[:, None]
    j = np.arange(2 * CHUNK)[None, :]
    rel = np.maximum(i + CHUNK - j, 0)
    n_exact = N_BUCKETS // 2
    relf = np.maximum(rel, n_exact).astype(np.float32)
    large = n_exact + (np.log(relf / np.float32(n_exact)) / np.float32(math.log(MAX_DISTANCE / n_exact))
                       * np.float32(N_BUCKETS - n_exact)).astype(np.int32)
    large = np.minimum(large, N_BUCKETS - 1)
    bucket = np.where(rel < n_exact, rel, large)
    in_window = (i + CHUNK - j >= 0) & (i + CHUNK - j < CHUNK)
    return bucket.astype(np.int32), in_window


def _split3(x):
    hi = x.astype(BF16)
    r1 = x - hi.astype(F32)
    mid = r1.astype(BF16)
    lo = (r1 - mid.astype(F32)).astype(BF16)
    return hi, mid, lo


HBM_SPEC = pl.BlockSpec(memory_space=pltpu.HBM)


def _mesh_pos():
    return lax.axis_index("x"), lax.axis_index("y"), lax.axis_index("c")


def _dev_index(px, py, pc):
    return 4 * px + 2 * py + pc


def _all_gather(shards, name):
    n = len(shards)

    def body(*refs):
        ins, outs = refs[:n], refs[n:2 * n]
        send_sems, recv_sems, local_sems = refs[2 * n:]
        x, y, c = _mesh_pos()
        me, sibling = (x, y, c), (x, y, 1 - c)
        chips = [(1 - x, y), (x, 1 - y), (1 - x, 1 - y)]

        def copy(a, k, block, to, src=None):
            dst = outs[a].at[_dev_index(*block)]
            return pltpu.make_async_remote_copy(
                src_ref=dst if src is None else src, dst_ref=dst,
                send_sem=send_sems.at[a * 7 + k], recv_sem=recv_sems.at[a * 7 + k],
                device_id=to, device_id_type=MESH)

        mine = [pltpu.make_async_copy(ins[a], outs[a].at[_dev_index(*me)], local_sems.at[a]) for a in range(n)]
        first = []
        for a in range(n):
            for j, chip in enumerate(chips):
                first.append(copy(a, 1 + j, me, (*chip, c), src=ins[a]))
            first.append(copy(a, 0, me, sibling, src=ins[a]))
        for cp in first:
            cp.start()
        for cp in mine:
            cp.start()
        passed = []
        for a in range(n):
            for j, chip in enumerate(chips):
                copy(a, 1 + j, (*chip, c), me).wait_recv()
                fwd = copy(a, 4 + j, (*chip, c), sibling)
                fwd.start()
                passed.append(fwd)
        for a in range(n):
            copy(a, 0, sibling, me).wait_recv()
            for j, chip in enumerate(chips):
                copy(a, 4 + j, (*chip, 1 - c), me).wait_recv()
        for cp in first + passed:
            cp.wait_send()
        for cp in mine:
            cp.wait()

    return _CHAIN.call(
        body, name=name,
        out_shape=[SDS((N_DEV,) + s.shape, s.dtype) for s in shards],
        in_specs=[HBM_SPEC] * n, out_specs=[HBM_SPEC] * n,
        scratch_shapes=[pltpu.SemaphoreType.DMA((7 * n,)), pltpu.SemaphoreType.DMA((7 * n,)),
                        pltpu.SemaphoreType.DMA((n,))],
    )(*shards)


SEM_SPEC = pl.BlockSpec(memory_space=pltpu.SEMAPHORE)
ANY_SPEC = pl.BlockSpec(memory_space=pl.ANY)
VMEM_SPEC = pl.BlockSpec(memory_space=pltpu.VMEM)
TOKEN_SPEC = VMEM_SPEC
TOKEN = SDS((8, LANE), F32)
SIDE_EFFECT = pltpu.SideEffectType.DATAFLOW_SIDE_EFFECTING


def _hbm(x):
    return pltpu.with_memory_space_constraint(x, pltpu.HBM)


class _CallChain:
    def __init__(self):
        self.token = None

    def call(self, body, *, in_specs, out_specs, out_shape, **kwargs):
        dep, n_in = self.token, len(in_specs)
        single = not isinstance(out_shape, (list, tuple))
        out_shapes = [out_shape] if single else list(out_shape)
        out_specs = [out_specs] if single else list(out_specs)
        n_out = len(out_shapes)
        n_dep = 0 if dep is None else 1
        token_spec = pl.BlockSpec((8, LANE), lambda *_: (0, 0)) if kwargs.get("grid") else VMEM_SPEC

        def chained(*refs):
            outs_at = n_in + n_dep
            body(*refs[:n_in], *refs[outs_at:outs_at + n_out], *refs[outs_at + n_out + 1:])
            token = refs[outs_at + n_out]
            token[...] = jnp.zeros_like(token)

        inner = pl.pallas_call(chained, in_specs=list(in_specs) + [ANY_SPEC] * n_dep, out_specs=out_specs + [token_spec],
                               out_shape=out_shapes + [TOKEN], **kwargs)

        def run(*operands):
            outs = inner(*operands) if dep is None else inner(*operands, dep)
            self.token = outs[n_out]
            return outs[0] if single else list(outs[:n_out])

        return run


_CHAIN = _CallChain()


def _split_start(bufs, copies_of, n_sems, name):
    n = len(bufs)

    def body(*refs):
        ins = refs[:n]
        send_sems, recv_sems = refs[n], refs[n + 1]
        for src, dst, k, target in copies_of(ins):
            pltpu.make_async_remote_copy(src_ref=src, dst_ref=dst, send_sem=send_sems.at[k], recv_sem=recv_sems.at[k],
                                         device_id=target, device_id_type=MESH).start()

    outs = _CHAIN.call(
        body, name=name,
        out_shape=[pltpu.SemaphoreType.DMA((n_sems,)), pltpu.SemaphoreType.DMA((n_sems,))]
        + [pltpu.HBM(b.shape, b.dtype) for b in bufs],
        in_specs=[HBM_SPEC] * n, out_specs=[SEM_SPEC, SEM_SPEC] + [HBM_SPEC] * n,
        input_output_aliases={a: 2 + a for a in range(n)},
        compiler_params=pltpu.CompilerParams(has_side_effects=SIDE_EFFECT),
    )(*[_hbm(b) for b in bufs])
    return outs[0], outs[1], list(outs[2:2 + n])


def _split_wait(bufs, sem_sets, waits_of, name):
    n, ns = len(bufs), len(sem_sets)
    flat_sems = [s for pair in sem_sets for s in pair]

    def body(*refs):
        ins = refs[:n]
        sems = refs[n:n + 2 * ns]
        x, y, c = _mesh_pos()
        for kind, src, dst, send_sem, recv_sem in waits_of(ins, [(sems[2 * i], sems[2 * i + 1]) for i in range(ns)]):
            cp = pltpu.make_async_remote_copy(src_ref=src, dst_ref=dst, send_sem=send_sem, recv_sem=recv_sem,
                                              device_id=(x, y, c), device_id_type=MESH)
            if kind == "send":
                cp.wait_send()
            else:
                cp.wait_recv()

    outs = _CHAIN.call(
        body, name=name,
        out_shape=[pltpu.HBM(b.shape, b.dtype) for b in bufs],
        in_specs=[HBM_SPEC] * n + [SEM_SPEC] * (2 * ns), out_specs=[HBM_SPEC] * n,
        input_output_aliases={a: a for a in range(n)},
        compiler_params=pltpu.CompilerParams(has_side_effects=SIDE_EFFECT),
    )(*bufs, *flat_sems)
    return list(outs)


def _gather_begin(shards, name):
    me = _dev_index(*_mesh_pos())
    lands = [lax.dynamic_update_index_in_dim(lax.empty((N_DEV,) + s.shape, s.dtype), s, me, 0) for s in shards]

    def copies_of(ins):
        x, y, c = _mesh_pos()
        targets = [(x, y, 1 - c), (1 - x, y, c), (x, 1 - y, c), (1 - x, 1 - y, c)]
        out = []
        for a, land in enumerate(ins):
            blk = land.at[_dev_index(x, y, c)]
            for k in (1, 2, 3, 0):
                out.append((blk, blk, 4 * a + k, targets[k]))
        return out

    send_sems, recv_sems, lands = _split_start(lands, copies_of, 4 * len(shards), name)
    return dict(lands=lands, sems=(send_sems, recv_sems), fwd={})


def _gather_pass_on(state, which, name):
    def arrivals(ins, sems):
        x, y, c = _mesh_pos()
        chips = [(1 - x, y), (x, 1 - y), (1 - x, 1 - y)]
        out = []
        for i, a in enumerate(which):
            for j, (px, py) in enumerate(chips):
                blk = ins[i].at[_dev_index(px, py, c)]
                out.append(("recv", blk, blk, sems[0][0].at[4 * a + 1 + j], sems[0][1].at[4 * a + 1 + j]))
        return out

    bufs = _split_wait([state["lands"][a] for a in which], [state["sems"]], arrivals, name + "_arrived")

    def copies_of(ins):
        x, y, c = _mesh_pos()
        chips = [(1 - x, y), (x, 1 - y), (1 - x, 1 - y)]
        out = []
        for i in range(len(which)):
            for j, (px, py) in enumerate(chips):
                blk = ins[i].at[_dev_index(px, py, c)]
                out.append((blk, blk, 3 * i + j, (x, y, 1 - c)))
        return out

    send_sems, recv_sems, bufs = _split_start(bufs, copies_of, 3 * len(which), name)
    for i, a in enumerate(which):
        state["lands"][a] = bufs[i]
    state["fwd"][tuple(which)] = (send_sems, recv_sems)


def _gather_end(state, which, name):
    def waits(ins, sems):
        x, y, c = _mesh_pos()
        chips = [(1 - x, y), (x, 1 - y), (1 - x, 1 - y)]
        (s_send, s_recv), (f_send, f_recv) = sems
        out = []
        for i, a in enumerate(which):
            mine = ins[i].at[_dev_index(x, y, c)]
            sib = ins[i].at[_dev_index(x, y, 1 - c)]
            out.append(("recv", sib, sib, s_send.at[4 * a], s_recv.at[4 * a]))
            for j, (px, py) in enumerate(chips):
                theirs = ins[i].at[_dev_index(px, py, 1 - c)]
                out.append(("recv", theirs, theirs, f_send.at[3 * i + j], f_recv.at[3 * i + j]))
            for k in range(4):
                out.append(("send", mine, mine, s_send.at[4 * a + k], s_recv.at[4 * a + k]))
            for j, (px, py) in enumerate(chips):
                passed = ins[i].at[_dev_index(px, py, c)]
                out.append(("send", passed, passed, f_send.at[3 * i + j], f_recv.at[3 * i + j]))
        return out

    bufs = _split_wait([state["lands"][a] for a in which], [state["sems"], state["fwd"][tuple(which)]], waits, name)
    for i, a in enumerate(which):
        state["lands"][a] = bufs[i]
    return bufs


def _sibling_exchange_begin(part, name):
    land = lax.empty((4,) + part.shape[1:], part.dtype)

    def copies_of(ins):
        x, y, c = _mesh_pos()
        return [(ins[0].at[2 * j + (1 - c)], ins[1].at[j], j, (x, y, 1 - c)) for j in range(4)]

    send_sems, recv_sems, bufs = _split_start([part, land], copies_of, 4, name)
    return dict(bufs=bufs, sems=(send_sems, recv_sems))


def _sibling_exchange_end(state, name):
    def waits(ins, sems):
        _, _, c = _mesh_pos()
        out = []
        for j in range(4):
            for kind in ("send", "recv"):
                out.append((kind, ins[0].at[2 * j + (1 - c)], ins[1].at[j], sems[0][0].at[j], sems[0][1].at[j]))
        return out

    return _split_wait(state["bufs"], [state["sems"]], waits, name)


def _chip_exchange_begin(csum, name):
    land = lax.empty((3,) + csum.shape[1:], csum.dtype)

    def copies_of(ins):
        x, y, c = _mesh_pos()
        chips = [(1 - x, y), (x, 1 - y), (1 - x, 1 - y)]
        return [(ins[0].at[2 * px + py], ins[1].at[r], r, (px, py, c)) for r, (px, py) in enumerate(chips)]

    send_sems, recv_sems, bufs = _split_start([csum, land], copies_of, 3, name)
    return dict(bufs=bufs, sems=(send_sems, recv_sems))


def _chip_exchange_end(state, name):
    def waits(ins, sems):
        x, y, _ = _mesh_pos()
        chips = [(1 - x, y), (x, 1 - y), (1 - x, 1 - y)]
        out = []
        for r, (px, py) in enumerate(chips):
            for kind in ("send", "recv"):
                out.append((kind, ins[0].at[2 * px + py], ins[1].at[r], sems[0][0].at[r], sems[0][1].at[r]))
        return out

    return _split_wait(state["bufs"], [state["sems"]], waits, name)


def _chip_sum(part, recv, name):
    _, R, C = part.shape
    tr = _tile(R, 512, 16)
    c_idx = lax.axis_index("c").astype(jnp.int32).reshape((1,))

    def body(c_ref, p_ref, r_ref, o_ref):
        o_ref[...] = (p_ref[...].astype(F32) + r_ref[...].astype(F32)).astype(o_ref.dtype)

    grid_spec = pltpu.PrefetchScalarGridSpec(
        num_scalar_prefetch=1, grid=(4, R // tr),
        in_specs=[pl.BlockSpec((None, tr, C), lambda j, i, c_ref: (2 * j + c_ref[0], i, 0)),
                  pl.BlockSpec((None, tr, C), lambda j, i, c_ref: (j, i, 0))],
        out_specs=pl.BlockSpec((None, tr, C), lambda j, i, c_ref: (j, i, 0)))
    return pl.pallas_call(body, name=name, grid_spec=grid_spec, out_shape=SDS((4, R, C), part.dtype),
                          compiler_params=_params(2))(c_idx, part, recv)


def _bias_fwd(table_t, onehot_t):
    H = table_t.shape[0]
    n = onehot_t.shape[1]

    def body(t_ref, oh_ref, o_ref):
        hi, mid, lo = _split3(t_ref[...])
        oh = oh_ref[...]
        o_ref[...] = _dot(hi, oh, NN) + _dot(mid, oh, NN) + _dot(lo, oh, NN)

    return _CHAIN.call(body, name="bias_fwd", in_specs=[VMEM_SPEC] * 2, out_specs=VMEM_SPEC, out_shape=SDS((H, n), F32),
                       compiler_params=_params(0))(table_t, onehot_t)


def _inproj_fwd(x, g, w):
    T, D = x.shape
    P = w.shape[1]
    tm = _tile(T, 512)

    def body(x_ref, g_ref, w_ref, proj_ref, n_ref):
        xv = x_ref[...]
        n = (xv * _rms_stats(xv) * g_ref[...]).astype(BF16)
        n_ref[...] = n
        proj_ref[...] = _dot(n, w_ref[...], NN)

    return _CHAIN.call(
        body, name="inproj_fwd", grid=(T // tm,),
        in_specs=[pl.BlockSpec((tm, D), lambda i: (i, 0)), pl.BlockSpec((1, D), lambda i: (0, 0)), _resident((D, P))],
        out_specs=[pl.BlockSpec((tm, P), lambda i: (i, 0)), pl.BlockSpec((tm, D), lambda i: (i, 0))],
        out_shape=[SDS((T, P), F32), SDS((T, D), BF16)], compiler_params=_params(1))(x, g, w)


def _layer_norm_group(vg, lg, lb):
    mu = jnp.mean(vg, axis=-1, keepdims=True)
    xc = vg - mu
    rstd = lax.rsqrt(jnp.mean(xc * xc, axis=-1, keepdims=True) + EPS)
    vhat = xc * rstd
    return vhat, rstd, vhat * lg + lb


def _gmlp_fwd(proj, lg, lb, w_s, bs_t, A):
    T = proj.shape[0]
    G = A // GROUP_DIM
    tm = _tile(T, 512)
    nc = tm // CHUNK

    def body(u_ref, v_ref, lg_ref, lb_ref, w_ref, bst_ref, a_ref):
        row = lax.broadcasted_iota(jnp.int32, (CHUNK, CHUNK), 0)
        col = lax.broadcasted_iota(jnp.int32, (CHUNK, CHUNK), 1)
        causal = row >= col
        for g in range(G):
            sl = slice(g * GROUP_DIM, (g + 1) * GROUP_DIM)
            _, _, vn = _layer_norm_group(_gelu(v_ref[:, sl]), lg_ref[:, sl], lb_ref[:, sl])
            vnb = vn.astype(BF16)
            wm = jnp.where(causal, w_ref[g], 0.0).astype(BF16)
            ug = _gelu(u_ref[:, sl])
            bcol = bst_ref[:, g:g + 1]
            for c in range(nc):
                rs = slice(c * CHUNK, (c + 1) * CHUNK)
                a_ref[rs, sl] = ug[rs] * (_dot(wm, vnb[rs], NN) + bcol)

    return _CHAIN.call(
        body, name="gmlp_fwd", grid=(T // tm,),
        in_specs=[pl.BlockSpec((tm, A), lambda i: (i, 0)), pl.BlockSpec((tm, A), lambda i: (i, 1)),
                  pl.BlockSpec((1, A), lambda i: (0, 0)), pl.BlockSpec((1, A), lambda i: (0, 0)),
                  pl.BlockSpec((G, CHUNK, CHUNK), lambda i: (0, 0, 0)), pl.BlockSpec((CHUNK, G), lambda i: (0, 0))],
        out_specs=pl.BlockSpec((tm, A), lambda i: (i, 0)),
        out_shape=SDS((T, A), F32), compiler_params=_params(1))(proj, proj, lg, lb, w_s, bs_t)


def _attn_masks(first_tile):
    ii = lax.broadcasted_iota(jnp.int32, (CHUNK, 2 * CHUNK), 0)
    jj = lax.broadcasted_iota(jnp.int32, (CHUNK, 2 * CHUNK), 1)
    in_window = (jj > ii) & (jj <= ii + CHUNK)
    first_mask = in_window & jnp.logical_or(jnp.logical_not(first_tile), jj >= CHUNK)
    return in_window, first_mask


def _attn_probs(qh, kb, bias_h, mask, sink):
    s = _dot(qh, kb, NT) * (HEAD_DIM ** -0.5) + bias_h
    s = jnp.where(mask, s, NEG)
    m = jnp.maximum(jnp.max(s, axis=-1, keepdims=True), sink)
    p = jnp.exp(s - m)
    e_sink = jnp.exp(sink - m)
    inv = 1.0 / (jnp.sum(p, axis=-1, keepdims=True) + e_sink)
    return p * inv, e_sink * inv


def _attn_specs(tq, A, B, reverse_tiles=None):
    nb = tq // CHUNK
    kcol = (2 * A + B) // LANE
    if reverse_tiles is None:
        tile = lambda i: i
    else:
        tile = lambda i: reverse_tiles - 1 - i
    prev = lambda i: jnp.maximum(tile(i) * nb - 1, 0)
    return [pl.BlockSpec((tq, B), lambda i: (tile(i), 2 * A // B)),
            pl.BlockSpec((tq, LANE), lambda i: (tile(i), kcol)),
            pl.BlockSpec((tq, LANE), lambda i: (tile(i), kcol + 1)),
            pl.BlockSpec((CHUNK, LANE), lambda i: (prev(i), kcol)),
            pl.BlockSpec((CHUNK, LANE), lambda i: (prev(i), kcol + 1))]


def _attn_fwd(proj, bias, sinks, A, B):
    T = proj.shape[0]
    H = B // HEAD_DIM
    qpk = H // KV_HEADS
    tq = _tile(T, 512)
    nb = tq // CHUNK

    def body(sink_ref, q_ref, k_ref, v_ref, kp_ref, vp_ref, bias_ref, o_ref):
        in_window, first_mask = _attn_masks(pl.program_id(0) == 0)
        for b in range(nb):
            rows = slice(b * CHUNK, (b + 1) * CHUNK)
            if b == 0:
                kprev, vprev, mask = kp_ref[...], vp_ref[...], first_mask
            else:
                prows = slice((b - 1) * CHUNK, b * CHUNK)
                kprev, vprev, mask = k_ref[prows, :], v_ref[prows, :], in_window
            kband = jnp.concatenate([kprev, k_ref[rows, :]], axis=0).astype(BF16)
            vband = jnp.concatenate([vprev, v_ref[rows, :]], axis=0).astype(BF16)
            for h in range(H):
                ks = slice((h // qpk) * HEAD_DIM, (h // qpk + 1) * HEAD_DIM)
                hs = slice(h * HEAD_DIM, (h + 1) * HEAD_DIM)
                pn, _ = _attn_probs(q_ref[rows, hs].astype(BF16), kband[:, ks], bias_ref[h], mask, sink_ref[h])
                o_ref[rows, hs] = _dot(pn.astype(BF16), vband[:, ks], NN)

    return _CHAIN.call(
        body, name="attn_fwd", grid=(T // tq,),
        in_specs=[pl.BlockSpec(memory_space=pltpu.SMEM)] + _attn_specs(tq, A, B)
        + [pl.BlockSpec((H, CHUNK, 2 * CHUNK), lambda i: (0, 0, 0))],
        out_specs=pl.BlockSpec((tq, B), lambda i: (i, 0)),
        out_shape=SDS((T, B), F32), compiler_params=_params(1))(sinks, proj, proj, proj, proj, proj, bias)


def _outproj_fwd(a, b, ga, gb, x, w):
    T, A = a.shape
    B = b.shape[1]
    D = x.shape[1]
    tm = _tile(T, 512)

    def body(a_ref, b_ref, ga_ref, gb_ref, x_ref, w_ref, h_ref, mix_ref):
        av, bv = a_ref[...], b_ref[...]
        mix_ref[:, :A] = (av * _rms_stats(av) * ga_ref[...]).astype(BF16)
        mix_ref[:, A:] = (bv * _rms_stats(bv) * gb_ref[...]).astype(BF16)
        h_ref[...] = x_ref[...] + _dot(mix_ref[...], w_ref[...], NN)

    return _CHAIN.call(
        body, name="outproj_fwd", grid=(T // tm,),
        in_specs=[pl.BlockSpec((tm, A), lambda i: (i, 0)), pl.BlockSpec((tm, B), lambda i: (i, 0)),
                  pl.BlockSpec((1, A), lambda i: (0, 0)), pl.BlockSpec((1, B), lambda i: (0, 0)),
                  pl.BlockSpec((tm, D), lambda i: (i, 0)), _resident((A + B, D))],
        out_specs=[pl.BlockSpec((tm, D), lambda i: (i, 0)), pl.BlockSpec((tm, A + B), lambda i: (i, 0))],
        out_shape=[SDS((T, D), F32), SDS((T, A + B), BF16)], compiler_params=_params(1))(a, b, ga, gb, x, w)


def _ffn_up(h1, g, w_up):
    T, D = h1.shape
    Fb = w_up.shape[2]
    F = N_DEV * Fb
    tm, tf = _tile(T, 1024), _tile(Fb, 1024)
    per = Fb // tf

    def body(h_ref, g_ref, wu_ref, z_ref, n_ref, nbuf):
        @pl.when(pl.program_id(1) == 0)
        def _():
            hv = h_ref[...]
            n = (hv * _rms_stats(hv) * g_ref[...]).astype(BF16)
            nbuf[...] = n
            n_ref[...] = n

        z_ref[...] = jnp.maximum(_dot(nbuf[...], wu_ref[...], NN), 0.0).astype(BF16)

    return _CHAIN.call(
        body, name="ffn_up", grid=(T // tm, F // tf),
        in_specs=[pl.BlockSpec((tm, D), lambda i, j: (i, 0)), pl.BlockSpec((1, D), lambda i, j: (0, 0)),
                  pl.BlockSpec((None, D, tf), lambda i, j: (j // per, 0, j % per))],
        out_specs=[pl.BlockSpec((tm, tf), lambda i, j: (i, j)), pl.BlockSpec((tm, D), lambda i, j: (i, 0))],
        out_shape=[SDS((T, F), BF16), SDS((T, D), BF16)],
        scratch_shapes=[pltpu.VMEM((tm, D), BF16)], compiler_params=_params(2))(h1, g, w_up)


def _ffn_down(h1, z, w_down):
    T, D = h1.shape
    F = w_down.shape[0]
    tm, tk = _tile(T, 1024), _tile(F, 1024)
    nk = F // tk

    def body(h_ref, z_ref, wd_ref, h2_ref):
        k = pl.program_id(1)

        @pl.when(k == 0)
        def _():
            h2_ref[...] = h_ref[...]

        zf = z_ref[...].astype(F32)
        h2_ref[...] += _dot((zf * zf).astype(BF16), wd_ref[...], NN)

    return _CHAIN.call(
        body, name="ffn_down", grid=(T // tm, nk),
        in_specs=[pl.BlockSpec((tm, D), lambda i, k: (i, 0)), pl.BlockSpec((tm, tk), lambda i, k: (i, k)),
                  pl.BlockSpec((tk, D), lambda i, k: (k, 0))],
        out_specs=pl.BlockSpec((tm, D), lambda i, k: (i, 0)),
        out_shape=SDS((T, D), F32), compiler_params=_params(2))(h1, z, w_down)


def _final_loss(h2, g, target):
    T, D = h2.shape
    tm = _tile(T, 512)

    def body(h_ref, g_ref, t_ref, loss_ref, dg_ref, dh_ref, dhb_ref):
        @pl.when(pl.program_id(0) == 0)
        def _():
            loss_ref[...] = jnp.zeros_like(loss_ref)
            dg_ref[...] = jnp.zeros_like(dg_ref)

        hv, gv = h_ref[...], g_ref[...]
        r = _rms_stats(hv)
        hn = hv * r
        e = hn * gv - t_ref[...]
        loss_ref[...] += (0.5 / D) * jnp.sum(jnp.sum(e * e, axis=-1, keepdims=True), axis=0, keepdims=True)
        dy = e * (1.0 / D)
        dg_ref[...] += jnp.sum(dy * hn, axis=0, keepdims=True)
        dh = _rms_bwd(dy, hv, r, gv)
        dh_ref[...] = dh
        dhb_ref[...] = dh.astype(BF16)

    return _CHAIN.call(
        body, name="final_loss", grid=(T // tm,),
        in_specs=[pl.BlockSpec((tm, D), lambda i: (i, 0)), pl.BlockSpec((1, D), lambda i: (0, 0)),
                  pl.BlockSpec((tm, D), lambda i: (i, 0))],
        out_specs=[pl.BlockSpec((1, 1), lambda i: (0, 0)), pl.BlockSpec((1, D), lambda i: (0, 0)),
                   pl.BlockSpec((tm, D), lambda i: (i, 0)), pl.BlockSpec((tm, D), lambda i: (i, 0))],
        out_shape=[SDS((1, 1), F32), SDS((1, D), F32), SDS((T, D), F32), SDS((T, D), BF16)],
        compiler_params=_params(1))(h2, g, target)


def _ffn_down_bwd(dh2b, z, w_down):
    T, D = dh2b.shape
    F = w_down.shape[0]
    tm, tf = _tile(T, 1024), _tile(F, 1024)

    def body(dh_ref, z_ref, wd_ref, dzp_ref):
        dzz = _dot(dh_ref[...], wd_ref[...], NT)
        dzp_ref[...] = (dzz * (2.0 * z_ref[...].astype(F32))).astype(BF16)

    return _CHAIN.call(
        body, name="ffn_down_bwd", grid=(T // tm, F // tf),
        in_specs=[pl.BlockSpec((tm, D), lambda i, j: (i, 0)), pl.BlockSpec((tm, tf), lambda i, j: (i, j)),
                  pl.BlockSpec((tf, D), lambda i, j: (j, 0))],
        out_specs=pl.BlockSpec((tm, tf), lambda i, j: (i, j)),
        out_shape=SDS((T, F), BF16), compiler_params=_params(2))(dh2b, z, w_down)


def _ffn_up_bwd(dzp, dh2, h1, g, w_up):
    T, D = h1.shape
    Fb = w_up.shape[2]
    F = N_DEV * Fb
    tm, tk = _tile(T, 512), _tile(Fb, 1024)
    per = Fb // tk
    nk = F // tk

    def body(dzp_ref, dh_ref, h_ref, g_ref, wu_ref, dh1_ref, dh1b_ref, dg_ref, acc):
        i, k = pl.program_id(0), pl.program_id(1)

        @pl.when((i == 0) & (k == 0))
        def _():
            dg_ref[...] = jnp.zeros_like(dg_ref)

        part = _dot(dzp_ref[...], wu_ref[...], NT)

        @pl.when(k == 0)
        def _():
            acc[...] = part

        @pl.when(k > 0)
        def _():
            acc[...] += part

        @pl.when(k == nk - 1)
        def _():
            hv, gv, dn = h_ref[...], g_ref[...], acc[...]
            r = _rms_stats(hv)
            dg_ref[...] += jnp.sum(dn * (hv * r), axis=0, keepdims=True)
            dh1 = dh_ref[...] + _rms_bwd(dn, hv, r, gv)
            dh1_ref[...] = dh1
            dh1b_ref[...] = dh1.astype(BF16)

    return _CHAIN.call(
        body, name="ffn_up_bwd", grid=(T // tm, nk),
        in_specs=[pl.BlockSpec((tm, tk), lambda i, k: (i, k)), pl.BlockSpec((tm, D), lambda i, k: (i, 0)),
                  pl.BlockSpec((tm, D), lambda i, k: (i, 0)), pl.BlockSpec((1, D), lambda i, k: (0, 0)),
                  pl.BlockSpec((None, D, tk), lambda i, k: (k // per, 0, k % per))],
        out_specs=[pl.BlockSpec((tm, D), lambda i, k: (i, 0)), pl.BlockSpec((tm, D), lambda i, k: (i, 0)),
                   pl.BlockSpec((1, D), lambda i, k: (0, 0))],
        out_shape=[SDS((T, D), F32), SDS((T, D), BF16), SDS((1, D), F32)],
        scratch_shapes=[pltpu.VMEM((tm, D), F32)], compiler_params=_params(2))(dzp, dh2, h1, g, w_up)


def _matmul_tn(a, b, name, square_a=False, col_blocks=None):
    T, K = a.shape
    N = b.shape[1]
    tt, tk = _tile(T, 1024), _tile(K, 1024)
    tn = _tile(N if col_blocks is None else N // col_blocks, 1792)
    nt = T // tt

    def body(a_ref, b_ref, o_ref, acc):
        t = pl.program_id(2)

        @pl.when(t == 0)
        def _():
            acc[...] = jnp.zeros_like(acc)

        av = a_ref[...]
        if square_a:
            af = av.astype(F32)
            av = (af * af).astype(BF16)
        acc[...] += _dot(av, b_ref[...], TN)

        @pl.when(t == nt - 1)
        def _():
            o_ref[...] = acc[...].astype(o_ref.dtype)

    if col_blocks is None:
        out_shape = SDS((K, N), BF16)
        out_spec = pl.BlockSpec((tk, tn), lambda i, j, t: (i, j))
    else:
        per = (N // col_blocks) // tn
        out_shape = SDS((col_blocks, K, N // col_blocks), BF16)
        out_spec = pl.BlockSpec((None, tk, tn), lambda i, j, t: (j // per, i, j % per))
    return _CHAIN.call(
        body, name=name, grid=(K // tk, N // tn, nt),
        in_specs=[pl.BlockSpec((tt, tk), lambda i, j, t: (t, i)), pl.BlockSpec((tt, tn), lambda i, j, t: (t, j))],
        out_specs=out_spec, out_shape=out_shape,
        scratch_shapes=[pltpu.VMEM((tk, tn), F32)], compiler_params=_params(3))(a, b)


def _outproj_bwd(dh1b, w, a, b, ga, gb):
    T, D = dh1b.shape
    A, B = a.shape[1], b.shape[1]
    tm = _tile(T, 512)

    def body(dh_ref, w_ref, a_ref, b_ref, ga_ref, gb_ref, da_ref, db_ref, dga_ref, dgb_ref):
        @pl.when(pl.program_id(0) == 0)
        def _():
            dga_ref[...] = jnp.zeros_like(dga_ref)
            dgb_ref[...] = jnp.zeros_like(dgb_ref)

        dmix = _dot(dh_ref[...], w_ref[...], NT)
        for src_ref, g_ref, dx_ref, dg_ref, dn in ((a_ref, ga_ref, da_ref, dga_ref, dmix[:, :A]),
                                                   (b_ref, gb_ref, db_ref, dgb_ref, dmix[:, A:])):
            xv = src_ref[...]
            r = _rms_stats(xv)
            dg_ref[...] += jnp.sum(dn * (xv * r), axis=0, keepdims=True)
            dx_ref[...] = _rms_bwd(dn, xv, r, g_ref[...])

    return _CHAIN.call(
        body, name="outproj_bwd", grid=(T // tm,),
        in_specs=[pl.BlockSpec((tm, D), lambda i: (i, 0)), _resident((A + B, D)),
                  pl.BlockSpec((tm, A), lambda i: (i, 0)), pl.BlockSpec((tm, B), lambda i: (i, 0)),
                  pl.BlockSpec((1, A), lambda i: (0, 0)), pl.BlockSpec((1, B), lambda i: (0, 0))],
        out_specs=[pl.BlockSpec((tm, A), lambda i: (i, 0)), pl.BlockSpec((tm, B), lambda i: (i, 0)),
                   pl.BlockSpec((1, A), lambda i: (0, 0)), pl.BlockSpec((1, B), lambda i: (0, 0))],
        out_shape=[SDS((T, A), F32), SDS((T, B), F32), SDS((1, A), F32), SDS((1, B), F32)],
        compiler_params=_params(1))(dh1b, w, a, b, ga, gb)


def _gmlp_bwd(proj, da, lg, lb, w_s, w_st, bs_t, A):
    T = proj.shape[0]
    G = A // GROUP_DIM
    tm = _tile(T, 512)
    nc = tm // CHUNK

    def body(u_ref, v_ref, da_ref, lg_ref, lb_ref, w_ref, wt_ref, bst_ref, duv_ref, dlg_ref, dlb_ref, dw_ref, dbs_ref):
        @pl.when(pl.program_id(0) == 0)
        def _():
            dlg_ref[...] = jnp.zeros_like(dlg_ref)
            dlb_ref[...] = jnp.zeros_like(dlb_ref)
            dw_ref[...] = jnp.zeros_like(dw_ref)
            dbs_ref[...] = jnp.zeros_like(dbs_ref)

        row = lax.broadcasted_iota(jnp.int32, (CHUNK, CHUNK), 0)
        col = lax.broadcasted_iota(jnp.int32, (CHUNK, CHUNK), 1)
        lower = row >= col
        upper = row <= col
        for g in range(G):
            sl = slice(g * GROUP_DIM, (g + 1) * GROUP_DIM)
            lgv = lg_ref[:, sl]
            vg, vg_grad = _gelu_and_grad(v_ref[:, sl])
            vhat, rstd, vn = _layer_norm_group(vg, lgv, lb_ref[:, sl])
            vnb = vn.astype(BF16)
            ug, ug_grad = _gelu_and_grad(u_ref[:, sl])
            dav = da_ref[:, sl]
            wm = jnp.where(lower, w_ref[g], 0.0).astype(BF16)
            wmt = jnp.where(upper, wt_ref[g], 0.0).astype(BF16)
            bcol = bst_ref[:, g:g + 1]
            dw_acc = jnp.zeros((CHUNK, CHUNK), F32)
            dbs_acc = jnp.zeros((CHUNK, 1), F32)
            dvn_parts = []
            dug_parts = []
            for c in range(nc):
                rs = slice(c * CHUNK, (c + 1) * CHUNK)
                mixed = _dot(wm, vnb[rs], NN) + bcol
                dug_parts.append(dav[rs] * mixed)
                dmix = dav[rs] * ug[rs]
                dbs_acc = dbs_acc + jnp.sum(dmix, axis=-1, keepdims=True)
                dmixb = dmix.astype(BF16)
                dw_acc = dw_acc + _dot(dmixb, vnb[rs], NT)
                dvn_parts.append(_dot(wmt, dmixb, NN))
            dvn = jnp.concatenate(dvn_parts, axis=0)
            dug = jnp.concatenate(dug_parts, axis=0)
            dw_ref[g] += jnp.where(lower, dw_acc, 0.0)
            dbs_ref[:, g:g + 1] += dbs_acc
            dlg_ref[:, sl] += jnp.sum(dvn * vhat, axis=0, keepdims=True)
            dlb_ref[:, sl] += jnp.sum(dvn, axis=0, keepdims=True)
            dvhat = dvn * lgv
            dvg = rstd * (dvhat - jnp.mean(dvhat, axis=-1, keepdims=True)
                          - vhat * jnp.mean(dvhat * vhat, axis=-1, keepdims=True))
            duv_ref[:, sl] = (dug * ug_grad).astype(BF16)
            duv_ref[:, A + g * GROUP_DIM:A + (g + 1) * GROUP_DIM] = (dvg * vg_grad).astype(BF16)

    return _CHAIN.call(
        body, name="gmlp_bwd", grid=(T // tm,),
        in_specs=[pl.BlockSpec((tm, A), lambda i: (i, 0)), pl.BlockSpec((tm, A), lambda i: (i, 1)),
                  pl.BlockSpec((tm, A), lambda i: (i, 0)),
                  pl.BlockSpec((1, A), lambda i: (0, 0)), pl.BlockSpec((1, A), lambda i: (0, 0)),
                  pl.BlockSpec((G, CHUNK, CHUNK), lambda i: (0, 0, 0)),
                  pl.BlockSpec((G, CHUNK, CHUNK), lambda i: (0, 0, 0)), pl.BlockSpec((CHUNK, G), lambda i: (0, 0))],
        out_specs=[pl.BlockSpec((tm, 2 * A), lambda i: (i, 0)),
                   pl.BlockSpec((1, A), lambda i: (0, 0)), pl.BlockSpec((1, A), lambda i: (0, 0)),
                   pl.BlockSpec((G, CHUNK, CHUNK), lambda i: (0, 0, 0)), pl.BlockSpec((CHUNK, G), lambda i: (0, 0))],
        out_shape=[SDS((T, 2 * A), BF16), SDS((1, A), F32), SDS((1, A), F32),
                   SDS((G, CHUNK, CHUNK), F32), SDS((CHUNK, G), F32)],
        compiler_params=_params(1))(proj, proj, da, lg, lb, w_s, w_st, bs_t)


def _attn_bwd(proj, o, do, duv, bias, sinks, A, B):
    T, P = proj.shape
    H = B // HEAD_DIM
    qpk = H // KV_HEADS
    tq = _tile(T, 512)
    nb = tq // CHUNK
    n_tiles = T // tq
    scale = HEAD_DIM ** -0.5
    rev = lambda i: n_tiles - 1 - i

    def body(sink_ref, q_ref, k_ref, v_ref, kp_ref, vp_ref, o_ref, do_ref, duv_ref, bias_ref,
             dproj_ref, dbias_ref, dsink_ref, carry, dkv, sacc):
        step = pl.program_id(0)

        @pl.when(step == 0)
        def _():
            carry[...] = jnp.zeros_like(carry)
            sacc[...] = jnp.zeros_like(sacc)
            dbias_ref[...] = jnp.zeros_like(dbias_ref)

        in_window, first_mask = _attn_masks(step == n_tiles - 1)
        dproj_ref[:, :2 * A] = duv_ref[...]
        dkv[...] = jnp.zeros_like(dkv)
        for b in range(nb):
            rows = slice(b * CHUNK, (b + 1) * CHUNK)
            band = slice(b * CHUNK, (b + 2) * CHUNK)
            if b == 0:
                kprev, vprev, mask = kp_ref[...], vp_ref[...], first_mask
            else:
                prows = slice((b - 1) * CHUNK, b * CHUNK)
                kprev, vprev, mask = k_ref[prows, :], v_ref[prows, :], in_window
            kband = jnp.concatenate([kprev, k_ref[rows, :]], axis=0).astype(BF16)
            vband = jnp.concatenate([vprev, v_ref[rows, :]], axis=0).astype(BF16)
            for kv in range(KV_HEADS):
                ks = slice(kv * HEAD_DIM, (kv + 1) * HEAD_DIM)
                kb, vb = kband[:, ks], vband[:, ks]
                dk_acc = jnp.zeros((2 * CHUNK, HEAD_DIM), F32)
                dv_acc = jnp.zeros((2 * CHUNK, HEAD_DIM), F32)
                for h in range(kv * qpk, (kv + 1) * qpk):
                    hs = slice(h * HEAD_DIM, (h + 1) * HEAD_DIM)
                    qh = q_ref[rows, hs].astype(BF16)
                    pn, p_sink = _attn_probs(qh, kb, bias_ref[h], mask, sink_ref[h])
                    doh = do_ref[rows, hs]
                    delta = jnp.sum(doh * o_ref[rows, hs], axis=-1, keepdims=True)
                    dohb = doh.astype(BF16)
                    ds = pn * (_dot(dohb, vb, NT) - delta)
                    dbias_ref[h] += ds
                    sacc[:, h:h + 1] += -(p_sink * delta)
                    dsb = ds.astype(BF16)
                    dproj_ref[rows, 2 * A + h * HEAD_DIM:2 * A + (h + 1) * HEAD_DIM] = (
                        _dot(dsb, kb, NN) * scale).astype(BF16)
                    dk_acc = dk_acc + _dot(dsb, qh, TN)
                    dv_acc = dv_acc + _dot(pn.astype(BF16), dohb, TN)
                dkv[band, ks] += dk_acc * scale
                dkv[band, LANE + kv * HEAD_DIM:LANE + (kv + 1) * HEAD_DIM] += dv_acc
        last = slice(tq, tq + CHUNK)
        dkv[last, :] += carry[...]
        dproj_ref[:, 2 * A + B:] = dkv[CHUNK:, :].astype(BF16)
        carry[...] = dkv[:CHUNK, :]

        @pl.when(step == n_tiles - 1)
        def _():
            dsink_ref[...] = jnp.sum(sacc[...], axis=0, keepdims=True)

    specs = _attn_specs(tq, A, B, reverse_tiles=n_tiles)
    return _CHAIN.call(
        body, name="attn_bwd", grid=(n_tiles,),
        in_specs=[pl.BlockSpec(memory_space=pltpu.SMEM)] + specs
        + [pl.BlockSpec((tq, B), lambda i: (rev(i), 0)), pl.BlockSpec((tq, B), lambda i: (rev(i), 0)),
           pl.BlockSpec((tq, 2 * A), lambda i: (rev(i), 0)),
           pl.BlockSpec((H, CHUNK, 2 * CHUNK), lambda i: (0, 0, 0))],
        out_specs=[pl.BlockSpec((tq, P), lambda i: (rev(i), 0)),
                   pl.BlockSpec((H, CHUNK, 2 * CHUNK), lambda i: (0, 0, 0)), pl.BlockSpec((1, H), lambda i: (0, 0))],
        out_shape=[SDS((T, P), BF16), SDS((H, CHUNK, 2 * CHUNK), F32), SDS((1, H), F32)],
        scratch_shapes=[pltpu.VMEM((CHUNK, 2 * LANE), F32), pltpu.VMEM((tq + CHUNK, 2 * LANE), F32),
                        pltpu.VMEM((CHUNK, H), F32)],
        compiler_params=_params(1))(sinks, proj, proj, proj, proj, proj, o, do, duv, bias)


def _bias_bwd(dbias, onehot):
    H = dbias.shape[0]
    nbk = onehot.shape[1]

    def body(d_ref, oh_ref, o_ref):
        hi, mid, lo = _split3(d_ref[...])
        oh = oh_ref[...]
        o_ref[...] = _dot(hi, oh, NN) + _dot(mid, oh, NN) + _dot(lo, oh, NN)

    return _CHAIN.call(body, name="bias_bwd", in_specs=[VMEM_SPEC] * 2, out_specs=VMEM_SPEC, out_shape=SDS((H, nbk), F32),
                       compiler_params=_params(0))(dbias, onehot)


def _inproj_bwd(dproj, w, x, dh1, g):
    T, P = dproj.shape
    D = x.shape[1]
    tm = _tile(T, 512)

    def body(dp_ref, w_ref, x_ref, dh_ref, g_ref, dx_ref, dg_ref):
        @pl.when(pl.program_id(0) == 0)
        def _():
            dg_ref[...] = jnp.zeros_like(dg_ref)

        dn = _dot(dp_ref[...], w_ref[...], NT)
        xv = x_ref[...]
        r = _rms_stats(xv)
        dg_ref[...] += jnp.sum(dn * (xv * r), axis=0, keepdims=True)
        dx_ref[...] = dh_ref[...] + _rms_bwd(dn, xv, r, g_ref[...])

    return _CHAIN.call(
        body, name="inproj_bwd", grid=(T // tm,),
        in_specs=[pl.BlockSpec((tm, P), lambda i: (i, 0)), _resident((D, P)),
                  pl.BlockSpec((tm, D), lambda i: (i, 0)), pl.BlockSpec((tm, D), lambda i: (i, 0)),
                  pl.BlockSpec((1, D), lambda i: (0, 0))],
        out_specs=[pl.BlockSpec((tm, D), lambda i: (i, 0)), pl.BlockSpec((1, D), lambda i: (0, 0))],
        out_shape=[SDS((T, D), F32), SDS((1, D), F32)], compiler_params=_params(1))(dproj, w, x, dh1, g)


def _adamw(w, g, m, v):
    m = ADAM_B1 * m + (1.0 - ADAM_B1) * g
    v = ADAM_B2 * v + (1.0 - ADAM_B2) * (g * g)
    m_hat = m / (1.0 - ADAM_B1 ** ADAM_STEP)
    v_hat = v / (1.0 - ADAM_B2 ** ADAM_STEP)
    delta = -ADAM_LR * (m_hat / (jnp.sqrt(v_hat) + ADAM_EPS) + ADAM_WD * w)
    return delta, m, v


def _adam_sharded(csum, recv, w, m, v, name):
    R, C = w.shape
    tr = _tile(R, 256, 16)
    own = (2 * lax.axis_index("x") + lax.axis_index("y")).astype(jnp.int32).reshape((1,))

    def body(own_idx, own_ref, recv_ref, w_ref, m_ref, v_ref, g_ref, d_ref, nm_ref, nv_ref):
        g = own_ref[...].astype(F32)
        for r in range(3):
            g = g + recv_ref[r].astype(F32)
        delta, nm, nv = _adamw(w_ref[...], g, m_ref[...], v_ref[...])
        g_ref[...] = g
        d_ref[...] = delta
        nm_ref[...] = nm
        nv_ref[...] = nv

    blk = pl.BlockSpec((tr, C), lambda i, own_idx: (i, 0))
    grid_spec = pltpu.PrefetchScalarGridSpec(
        num_scalar_prefetch=1, grid=(R // tr,),
        in_specs=[pl.BlockSpec((None, tr, C), lambda i, own_idx: (own_idx[0], i, 0)),
                  pl.BlockSpec((3, tr, C), lambda i, own_idx: (0, i, 0)), blk, blk, blk],
        out_specs=[blk] * 4)
    return pl.pallas_call(body, name=name, grid_spec=grid_spec, out_shape=[SDS((R, C), F32)] * 4,
                          compiler_params=_params(1))(own, csum, recv, w, m, v)


def _adam_small(gathered, w, m, v):
    R = w.shape[0]

    def body(p_ref, w_ref, m_ref, v_ref, g_ref, d_ref, nm_ref, nv_ref):
        g = p_ref[0]
        for d in range(1, N_DEV):
            g = g + p_ref[d]
        delta, nm, nv = _adamw(w_ref[...], g, m_ref[...], v_ref[...])
        g_ref[...] = g
        d_ref[...] = delta
        nm_ref[...] = nm
        nv_ref[...] = nv

    return _CHAIN.call(body, name="adam_small", in_specs=[VMEM_SPEC] * 4, out_specs=[VMEM_SPEC] * 4,
                       out_shape=[SDS((R, LANE), F32)] * 4,
                       compiler_params=_params(0))(gathered, w, m, v)


def _pack(arrays):
    tile = 8 * LANE
    pieces = []
    for a in arrays:
        flat = a.reshape(-1).astype(F32)
        pieces.append(jnp.pad(flat, (0, (-flat.size) % tile)))
    return jnp.concatenate(pieces).reshape(-1, LANE)


def _unpack(packed, shapes):
    tile = 8 * LANE
    flat = packed.reshape(-1)
    out, off = [], 0
    for s in shapes:
        size = int(np.prod(s))
        out.append(flat[off:off + size].reshape(s))
        off += size + (-size) % tile
    return out


def kernel(x, rel_bias_table, mix_norm_g, w_in, gate_norm_g, gate_norm_b, w_spatial, b_spatial, attn_sinks, out_norm_a_g, out_norm_b_g, w_out, ffn_norm_g, w_up, w_down, final_norm_g, loss_target, m_rel_bias_table, m_mix_norm_g, m_w_in, m_gate_norm_g, m_gate_norm_b, m_w_spatial, m_b_spatial, m_attn_sinks, m_out_norm_a_g, m_out_norm_b_g, m_w_out, m_ffn_norm_g, m_w_up, m_w_down, m_final_norm_g, v_rel_bias_table, v_mix_norm_g, v_w_in, v_gate_norm_g, v_gate_norm_b, v_w_spatial, v_b_spatial, v_attn_sinks, v_out_norm_a_g, v_out_norm_b_g, v_w_out, v_ffn_norm_g, v_w_up, v_w_down, v_final_norm_g):
    T, D = x.shape[1], x.shape[2]
    A = D // 2
    B = D // 2
    G = A // GROUP_DIM
    H = B // HEAD_DIM
    P = 2 * A + B + 2 * KV_HEADS * HEAD_DIM
    Pb = w_in.shape[2]
    xs = x.reshape(T, D)
    target = loss_target.reshape(T, D)

    shards = [w_in[0].astype(BF16), w_out[0].astype(BF16), w_up[0].astype(BF16), w_down[0].astype(BF16)]
    _CHAIN.token = None
    gather = _gather_begin(shards, "gather_start")
    _gather_pass_on(gather, [0], "gather_in_pass")
    (win_g,) = _gather_end(gather, [0], "gather_in_end")
    win_full = jnp.transpose(win_g, (1, 0, 2)).reshape(D, P)

    g1, g2, g3 = mix_norm_g.reshape(1, D), ffn_norm_g.reshape(1, D), final_norm_g.reshape(1, D)
    lg, lb = gate_norm_g.reshape(1, A), gate_norm_b.reshape(1, A)
    ws = w_spatial[0]
    ws_t = jnp.swapaxes(ws, 1, 2)
    bs_t = jnp.transpose(b_spatial[0])
    ga, gb = out_norm_a_g.reshape(1, A), out_norm_b_g.reshape(1, B)
    sinks = attn_sinks.reshape(H)
    bucket, in_window = _t5_bucket()
    onehot_np = ((bucket.reshape(-1, 1) == np.arange(N_BUCKETS)[None, :]) & in_window.reshape(-1, 1))
    onehot = jnp.asarray(onehot_np.astype(np.float32)).astype(BF16)

    bias = _bias_fwd(jnp.transpose(rel_bias_table), jnp.transpose(onehot)).reshape(H, CHUNK, 2 * CHUNK)
    proj, n1 = _inproj_fwd(xs, g1, win_full)
    _gather_pass_on(gather, [1], "gather_out_pass")
    a_out = _gmlp_fwd(proj, lg, lb, ws, bs_t, A)
    b_out = _attn_fwd(proj, bias, sinks, A, B)
    _gather_pass_on(gather, [2], "gather_up_pass")
    (wout_g,) = _gather_end(gather, [1], "gather_out_end")
    wout_full = wout_g.reshape(A + B, D)
    h1, mixed = _outproj_fwd(a_out, b_out, ga, gb, xs, wout_full)
    _gather_pass_on(gather, [3], "gather_down_pass")
    (wup_g,) = _gather_end(gather, [2], "gather_up_end")
    z, n2 = _ffn_up(h1, g2, wup_g)
    (wdown_g,) = _gather_end(gather, [3], "gather_down_end")
    h2 = _ffn_down(h1, z, wdown_g.reshape(-1, D))
    loss_part, dg3, dh2, dh2b = _final_loss(h2, g3, target)

    def reduce_to_chip(state, name):
        part, received = _sibling_exchange_end(state, name + "_sib_end")
        return _chip_exchange_begin(_chip_sum(part, received, name + "_chip_sum"), name + "_chip")

    dwdown = _matmul_tn(z, dh2b, "grad_w_down", square_a=True).reshape(wdown_g.shape)
    sib_down = _sibling_exchange_begin(dwdown, "rs_down_sib")
    dzp = _ffn_down_bwd(dh2b, z, wdown_g.reshape(-1, D))
    chip_down = reduce_to_chip(sib_down, "rs_down")
    dwup = _matmul_tn(n2, dzp, "grad_w_up", col_blocks=N_DEV)
    sib_up = _sibling_exchange_begin(dwup, "rs_up_sib")
    dh1, dh1b, dg2 = _ffn_up_bwd(dzp, dh2, h1, g2, wup_g)
    chip_up = reduce_to_chip(sib_up, "rs_up")
    da, db, dga, dgb = _outproj_bwd(dh1b, wout_full, a_out, b_out, ga, gb)
    dwout = _matmul_tn(mixed, dh1b, "grad_w_out").reshape(wout_g.shape)
    sib_out = _sibling_exchange_begin(dwout, "rs_out_sib")
    duv, dlg, dlb, dws, dbs_t = _gmlp_bwd(proj, da, lg, lb, ws, ws_t, bs_t, A)
    dproj, dbias, dsinks = _attn_bwd(proj, b_out, db, duv, bias, sinks, A, B)
    chip_out = reduce_to_chip(sib_out, "rs_out")
    dtable_t = _bias_bwd(dbias.reshape(H, -1), onehot)
    dwin = _matmul_tn(n1, dproj, "grad_w_in")
    dwin = jnp.transpose(dwin.reshape(D, N_DEV, Pb), (1, 0, 2))
    sib_in = _sibling_exchange_begin(dwin, "rs_in_sib")
    grad_x, dg1 = _inproj_bwd(dproj, win_full, xs, dh1, g1)

    small_w = [rel_bias_table, mix_norm_g, gate_norm_g, gate_norm_b, w_spatial, b_spatial, attn_sinks,
               out_norm_a_g, out_norm_b_g, ffn_norm_g, final_norm_g]
    small_m = [m_rel_bias_table, m_mix_norm_g, m_gate_norm_g, m_gate_norm_b, m_w_spatial, m_b_spatial, m_attn_sinks,
               m_out_norm_a_g, m_out_norm_b_g, m_ffn_norm_g, m_final_norm_g]
    small_v = [v_rel_bias_table, v_mix_norm_g, v_gate_norm_g, v_gate_norm_b, v_w_spatial, v_b_spatial, v_attn_sinks,
               v_out_norm_a_g, v_out_norm_b_g, v_ffn_norm_g, v_final_norm_g]
    small_g = [jnp.transpose(dtable_t), dg1, dlg, dlb, dws, jnp.transpose(dbs_t), dsinks, dga, dgb, dg2, dg3]
    shapes = [w.shape for w in small_w]
    big = [None] * 4

    def adam_of(k, state, w, m, v):
        csum, received = _chip_exchange_end(state, "rs_%d_end" % k)
        big[k] = [o.reshape(w.shape) for o in _adam_sharded(csum, received, w[0], m[0], v[0], "adam_%d" % k)]

    small_gather = _gather_begin([_pack(small_g)], "small_gather_start")
    chip_in = reduce_to_chip(sib_in, "rs_in")
    _gather_pass_on(small_gather, [0], "small_gather_pass")
    adam_of(3, chip_down, w_down, m_w_down, v_w_down)
    (gathered,) = _gather_end(small_gather, [0], "small_gather_end")
    sg, sd, sm, sv = [_unpack(o, shapes) for o in _adam_small(gathered, _pack(small_w), _pack(small_m), _pack(small_v))]
    adam_of(2, chip_up, w_up, m_w_up, v_w_up)
    adam_of(1, chip_out, w_out, m_w_out, v_w_out)
    adam_of(0, chip_in, w_in, m_w_in, v_w_in)

    loss = lax.psum(loss_part[0, 0], ("x", "y", "c"))

    order = ["s0", "s1", "b0", "s2", "s3", "s4", "s5", "s6", "s7", "s8", "b1", "s9", "b2", "b3", "s10"]

    def group(idx):
        small = (sg, sd, sm, sv)[idx]
        return [small[int(t[1:])] if t[0] == "s" else big[int(t[1:])][idx] for t in order]

    return (loss, grad_x.reshape(x.shape), *group(0), *group(1), *group(2), *group(3))
```

```python
import functools
import math

import numpy as np
import jax
import jax.numpy as jnp
from jax import lax
from jax.experimental import pallas as pl
from jax.experimental.pallas import tpu as pltpu

F32 = jnp.float32
BF16 = jnp.bfloat16
SDS = jax.ShapeDtypeStruct
MESH = pl.DeviceIdType.MESH

N_DEV = 8
EPS = 1e-5
NEG = -1e30
CHUNK = 128
GROUP_DIM = 128
HEAD_DIM = 64
KV_HEADS = 2
N_BUCKETS = 32
MAX_DISTANCE = 128
ADAM_LR, ADAM_B1, ADAM_B2, ADAM_EPS, ADAM_WD, ADAM_STEP = 0.001, 0.9, 0.999, 1e-08, 0.01, 10
GELU_C0 = math.sqrt(2.0 / math.pi)
GELU_C1 = 0.044715

V7X_VMEM_BYTES = 64 * 1024 * 1024
VMEM_LIMIT = V7X_VMEM_BYTES - 8 * 1024 * 1024
LANE = 128

NN = ((1,), (0,))
NT = ((1,), (1,))
TN = ((0,), (0,))


def _dot(a, b, dims):
    return lax.dot_general(a, b, (dims, ((), ())), preferred_element_type=F32)


def _tile(n, pref, unit=LANE):
    best = None
    for t in range(unit, min(n, pref) + 1, unit):
        if n % t == 0:
            best = t
    return n if best is None else best


def _params(n_grid):
    return pltpu.CompilerParams(dimension_semantics=("arbitrary",) * n_grid, vmem_limit_bytes=VMEM_LIMIT)


def _resident(shape):
    return pl.BlockSpec(shape, lambda i: (0, 0), pipeline_mode=pl.Buffered(1))


def _gelu(x):
    return 0.5 * x * (1.0 + jnp.tanh(GELU_C0 * (x + GELU_C1 * x * x * x)))


def _gelu_and_grad(x):
    x2 = x * x
    t = jnp.tanh(GELU_C0 * x * (1.0 + GELU_C1 * x2))
    val = 0.5 * x * (1.0 + t)
    grad = 0.5 * (1.0 + t) + 0.5 * x * (1.0 - t * t) * (GELU_C0 * (1.0 + 3.0 * GELU_C1 * x2))
    return val, grad


def _rms_stats(x):
    return lax.rsqrt(jnp.mean(x * x, axis=-1, keepdims=True) + EPS)


def _rms_bwd(dy, x, r, g):
    w = dy * g
    return r * w - x * (r * r * r) * jnp.mean(w * x, axis=-1, keepdims=True)


def _t5_bucket():
    i = np.arange(CHUNK)[:, None]
    j = np.arange(2 * CHUNK)[None, :]
    rel = np.maximum(i + CHUNK - j, 0)
    n_exact = N_BUCKETS // 2
    relf = np.maximum(rel, n_exact).astype(np.float32)
    large = n_exact + (np.log(relf / np.float32(n_exact)) / np.float32(math.log(MAX_DISTANCE / n_exact))
                       * np.float32(N_BUCKETS - n_exact)).astype(np.int32)
    large = np.minimum(large, N_BUCKETS - 1)
    bucket = np.where(rel < n_exact, rel, large)
    in_window = (i + CHUNK - j >= 0) & (i + CHUNK - j < CHUNK)
    return bucket.astype(np.int32), in_window


def _split3(x):
    hi = x.astype(BF16)
    r1 = x - hi.astype(F32)
    mid = r1.astype(BF16)
    lo = (r1 - mid.astype(F32)).astype(BF16)
    return hi, mid, lo


HBM_SPEC = pl.BlockSpec(memory_space=pltpu.HBM)


def _mesh_pos():
    return lax.axis_index("x"), lax.axis_index("y"), lax.axis_index("c")


def _dev_index(px, py, pc):
    return 4 * px + 2 * py + pc


def _all_gather(shards, name):
    n = len(shards)

    def body(*refs):
        ins, outs = refs[:n], refs[n:2 * n]
        send_sems, recv_sems, local_sems = refs[2 * n:]
        x, y, c = _mesh_pos()
        me, sibling = (x, y, c), (x, y, 1 - c)
        chips = [(1 - x, y), (x, 1 - y), (1 - x, 1 - y)]

        def copy(a, k, block, to, src=None):
            dst = outs[a].at[_dev_index(*block)]
            return pltpu.make_async_remote_copy(
                src_ref=dst if src is None else src, dst_ref=dst,
                send_sem=send_sems.at[a * 7 + k], recv_sem=recv_sems.at[a * 7 + k],
                device_id=to, device_id_type=MESH)

        mine = [pltpu.make_async_copy(ins[a], outs[a].at[_dev_index(*me)], local_sems.at[a]) for a in range(n)]
        first = []
        for a in range(n):
            for j, chip in enumerate(chips):
                first.append(copy(a, 1 + j, me, (*chip, c), src=ins[a]))
            first.append(copy(a, 0, me, sibling, src=ins[a]))
        for cp in first:
            cp.start()
        for cp in mine:
            cp.start()
        passed = []
        for a in range(n):
            for j, chip in enumerate(chips):
                copy(a, 1 + j, (*chip, c), me).wait_recv()
                fwd = copy(a, 4 + j, (*chip, c), sibling)
                fwd.start()
                passed.append(fwd)
        for a in range(n):
            copy(a, 0, sibling, me).wait_recv()
            for j, chip in enumerate(chips):
                copy(a, 4 + j, (*chip, 1 - c), me).wait_recv()
        for cp in first + passed:
            cp.wait_send()
        for cp in mine:
            cp.wait()

    return _CHAIN.call(
        body, name=name,
        out_shape=[SDS((N_DEV,) + s.shape, s.dtype) for s in shards],
        in_specs=[HBM_SPEC] * n, out_specs=[HBM_SPEC] * n,
        scratch_shapes=[pltpu.SemaphoreType.DMA((7 * n,)), pltpu.SemaphoreType.DMA((7 * n,)),
                        pltpu.SemaphoreType.DMA((n,))],
    )(*shards)


SEM_SPEC = pl.BlockSpec(memory_space=pltpu.SEMAPHORE)
ANY_SPEC = pl.BlockSpec(memory_space=pl.ANY)
VMEM_SPEC = pl.BlockSpec(memory_space=pltpu.VMEM)
TOKEN_SPEC = VMEM_SPEC
TOKEN = SDS((8, LANE), F32)
SIDE_EFFECT = pltpu.SideEffectType.DATAFLOW_SIDE_EFFECTING


def _hbm(x):
    return pltpu.with_memory_space_constraint(x, pltpu.HBM)


class _CallChain:
    def __init__(self):
        self.token = None

    def call(self, body, *, in_specs, out_specs, out_shape, **kwargs):
        dep, n_in = self.token, len(in_specs)
        single = not isinstance(out_shape, (list, tuple))
        out_shapes = [out_shape] if single else list(out_shape)
        out_specs = [out_specs] if single else list(out_specs)
        n_out = len(out_shapes)
        n_dep = 0 if dep is None else 1
        token_spec = pl.BlockSpec((8, LANE), lambda *_: (0, 0)) if kwargs.get("grid") else VMEM_SPEC

        def chained(*refs):
            outs_at = n_in + n_dep
            body(*refs[:n_in], *refs[outs_at:outs_at + n_out], *refs[outs_at + n_out + 1:])
            token = refs[outs_at + n_out]
            token[...] = jnp.zeros_like(token)

        inner = pl.pallas_call(chained, in_specs=list(in_specs) + [ANY_SPEC] * n_dep, out_specs=out_specs + [token_spec],
                               out_shape=out_shapes + [TOKEN], **kwargs)

        def run(*operands):
            outs = inner(*operands) if dep is None else inner(*operands, dep)
            self.token = outs[n_out]
            return outs[0] if single else list(outs[:n_out])

        return run


_CHAIN = _CallChain()


def _split_start(bufs, copies_of, n_sems, name):
    n = len(bufs)

    def body(*refs):
        ins = refs[:n]
        send_sems, recv_sems = refs[n], refs[n + 1]
        for src, dst, k, target in copies_of(ins):
            pltpu.make_async_remote_copy(src_ref=src, dst_ref=dst, send_sem=send_sems.at[k], recv_sem=recv_sems.at[k],
                                         device_id=target, device_id_type=MESH).start()

    outs = _CHAIN.call(
        body, name=name,
        out_shape=[pltpu.SemaphoreType.DMA((n_sems,)), pltpu.SemaphoreType.DMA((n_sems,))]
        + [pltpu.HBM(b.shape, b.dtype) for b in bufs],
        in_specs=[HBM_SPEC] * n, out_specs=[SEM_SPEC, SEM_SPEC] + [HBM_SPEC] * n,
        input_output_aliases={a: 2 + a for a in range(n)},
        compiler_params=pltpu.CompilerParams(has_side_effects=SIDE_EFFECT),
    )(*[_hbm(b) for b in bufs])
    return outs[0], outs[1], list(outs[2:2 + n])


def _split_wait(bufs, sem_sets, waits_of, name):
    n, ns = len(bufs), len(sem_sets)
    flat_sems = [s for pair in sem_sets for s in pair]

    def body(*refs):
        ins = refs[:n]
        sems = refs[n:n + 2 * ns]
        x, y, c = _mesh_pos()
        for kind, src, dst, send_sem, recv_sem in waits_of(ins, [(sems[2 * i], sems[2 * i + 1]) for i in range(ns)]):
            cp = pltpu.make_async_remote_copy(src_ref=src, dst_ref=dst, send_sem=send_sem, recv_sem=recv_sem,
                                              device_id=(x, y, c), device_id_type=MESH)
            if kind == "send":
                cp.wait_send()
            else:
                cp.wait_recv()

    outs = _CHAIN.call(
        body, name=name,
        out_shape=[pltpu.HBM(b.shape, b.dtype) for b in bufs],
        in_specs=[HBM_SPEC] * n + [SEM_SPEC] * (2 * ns), out_specs=[HBM_SPEC] * n,
        input_output_aliases={a: a for a in range(n)},
        compiler_params=pltpu.CompilerParams(has_side_effects=SIDE_EFFECT),
    )(*bufs, *flat_sems)
    return list(outs)


def _gather_begin(shards, name):
    me = _dev_index(*_mesh_pos())
    lands = [lax.dynamic_update_index_in_dim(lax.empty((N_DEV,) + s.shape, s.dtype), s, me, 0) for s in shards]

    def copies_of(ins):
        x, y, c = _mesh_pos()
        targets = [(x, y, 1 - c), (1 - x, y, c), (x, 1 - y, c), (1 - x, 1 - y, c)]
        out = []
        for a, land in enumerate(ins):
            blk = land.at[_dev_index(x, y, c)]
            for k in (1, 2, 3, 0):
                out.append((blk, blk, 4 * a + k, targets[k]))
        return out

    send_sems, recv_sems, lands = _split_start(lands, copies_of, 4 * len(shards), name)
    return dict(lands=lands, sems=(send_sems, recv_sems), fwd={})


def _gather_pass_on(state, which, name):
    def arrivals(ins, sems):
        x, y, c = _mesh_pos()
        chips = [(1 - x, y), (x, 1 - y), (1 - x, 1 - y)]
        out = []
        for i, a in enumerate(which):
            for j, (px, py) in enumerate(chips):
                blk = ins[i].at[_dev_index(px, py, c)]
                out.append(("recv", blk, blk, sems[0][0].at[4 * a + 1 + j], sems[0][1].at[4 * a + 1 + j]))
        return out

    bufs = _split_wait([state["lands"][a] for a in which], [state["sems"]], arrivals, name + "_arrived")

    def copies_of(ins):
        x, y, c = _mesh_pos()
        chips = [(1 - x, y), (x, 1 - y), (1 - x, 1 - y)]
        out = []
        for i in range(len(which)):
            for j, (px, py) in enumerate(chips):
                blk = ins[i].at[_dev_index(px, py, c)]
                out.append((blk, blk, 3 * i + j, (x, y, 1 - c)))
        return out

    send_sems, recv_sems, bufs = _split_start(bufs, copies_of, 3 * len(which), name)
    for i, a in enumerate(which):
        state["lands"][a] = bufs[i]
    state["fwd"][tuple(which)] = (send_sems, recv_sems)


def _gather_end(state, which, name):
    def waits(ins, sems):
        x, y, c = _mesh_pos()
        chips = [(1 - x, y), (x, 1 - y), (1 - x, 1 - y)]
        (s_send, s_recv), (f_send, f_recv) = sems
        out = []
        for i, a in enumerate(which):
            mine = ins[i].at[_dev_index(x, y, c)]
            sib = ins[i].at[_dev_index(x, y, 1 - c)]
            out.append(("recv", sib, sib, s_send.at[4 * a], s_recv.at[4 * a]))
            for j, (px, py) in enumerate(chips):
                theirs = ins[i].at[_dev_index(px, py, 1 - c)]
                out.append(("recv", theirs, theirs, f_send.at[3 * i + j], f_recv.at[3 * i + j]))
            for k in range(4):
                out.append(("send", mine, mine, s_send.at[4 * a + k], s_recv.at[4 * a + k]))
            for j, (px, py) in enumerate(chips):
                passed = ins[i].at[_dev_index(px, py, c)]
                out.append(("send", passed, passed, f_send.at[3 * i + j], f_recv.at[3 * i + j]))
        return out

    bufs = _split_wait([state["lands"][a] for a in which], [state["sems"], state["fwd"][tuple(which)]], waits, name)
    for i, a in enumerate(which):
        state["lands"][a] = bufs[i]
    return bufs


def _sibling_exchange_begin(part, name):
    land = lax.empty((4,) + part.shape[1:], part.dtype)

    def copies_of(ins):
        x, y, c = _mesh_pos()
        return [(ins[0].at[2 * j + (1 - c)], ins[1].at[j], j, (x, y, 1 - c)) for j in range(4)]

    send_sems, recv_sems, bufs = _split_start([part, land], copies_of, 4, name)
    return dict(bufs=bufs, sems=(send_sems, recv_sems))


def _sibling_exchange_end(state, name):
    def waits(ins, sems):
        _, _, c = _mesh_pos()
        out = []
        for j in range(4):
            for kind in ("send", "recv"):
                out.append((kind, ins[0].at[2 * j + (1 - c)], ins[1].at[j], sems[0][0].at[j], sems[0][1].at[j]))
        return out

    return _split_wait(state["bufs"], [state["sems"]], waits, name)


CHIP_FLIPS = (2, 1, 3)


def _chip_exchange_begin(csum, name):
    land = lax.empty((3,) + csum.shape[1:], csum.dtype)

    def copies_of(ins):
        x, y, c = _mesh_pos()
        chips = [(1 - x, y), (x, 1 - y), (1 - x, 1 - y)]
        return [(ins[0].at[CHIP_FLIPS[r]], ins[1].at[r], r, (px, py, c)) for r, (px, py) in enumerate(chips)]

    send_sems, recv_sems, bufs = _split_start([csum, land], copies_of, 3, name)
    return dict(bufs=bufs, sems=(send_sems, recv_sems))


def _chip_exchange_end(state, name):
    def waits(ins, sems):
        out = []
        for r in range(3):
            for kind in ("send", "recv"):
                out.append((kind, ins[0].at[CHIP_FLIPS[r]], ins[1].at[r], sems[0][0].at[r], sems[0][1].at[r]))
        return out

    return _split_wait(state["bufs"], [state["sems"]], waits, name)


def _chip_sum(part, recv, name):
    _, R, C = part.shape
    tr = _tile(R, 512, 16)
    place = jnp.stack([lax.axis_index("c"), 2 * lax.axis_index("x") + lax.axis_index("y")]).astype(jnp.int32)

    def body(place_ref, p_ref, r_ref, o_ref):
        o_ref[...] = (p_ref[...].astype(F32) + r_ref[...].astype(F32)).astype(o_ref.dtype)

    def chip(p, place_ref):
        return jnp.bitwise_xor(p, place_ref[1])

    grid_spec = pltpu.PrefetchScalarGridSpec(
        num_scalar_prefetch=1, grid=(4, R // tr),
        in_specs=[pl.BlockSpec((None, tr, C), lambda p, i, place_ref: (2 * chip(p, place_ref) + place_ref[0], i, 0)),
                  pl.BlockSpec((None, tr, C), lambda p, i, place_ref: (chip(p, place_ref), i, 0))],
        out_specs=pl.BlockSpec((None, tr, C), lambda p, i, place_ref: (p, i, 0)))
    return pl.pallas_call(body, name=name, grid_spec=grid_spec, out_shape=SDS((4, R, C), part.dtype),
                          compiler_params=_params(2))(place, part, recv)


def _bias_fwd(table_t, onehot_t, onehot_kq_t):
    H = table_t.shape[0]
    n = onehot_t.shape[1]

    def body(t_ref, oh_ref, oh_kq_ref, o_ref, o_kq_ref):
        hi, mid, lo = _split3(t_ref[...])
        for src, dst in ((oh_ref, o_ref), (oh_kq_ref, o_kq_ref)):
            oh = src[...]
            dst[...] = _dot(hi, oh, NN) + _dot(mid, oh, NN) + _dot(lo, oh, NN)

    return _CHAIN.call(body, name="bias_fwd", in_specs=[VMEM_SPEC] * 3, out_specs=[VMEM_SPEC] * 2,
                       out_shape=[SDS((H, n), F32)] * 2, compiler_params=_params(0))(table_t, onehot_t, onehot_kq_t)


def _inproj_fwd(x, g, w):
    T, D = x.shape
    P = w.shape[1]
    tm = _tile(T, 512)

    def body(x_ref, g_ref, w_ref, proj_ref, n_ref):
        xv = x_ref[...]
        n = (xv * _rms_stats(xv) * g_ref[...]).astype(BF16)
        n_ref[...] = n
        proj_ref[...] = _dot(n, w_ref[...], NN)

    return _CHAIN.call(
        body, name="inproj_fwd", grid=(T // tm,),
        in_specs=[pl.BlockSpec((tm, D), lambda i: (i, 0)), pl.BlockSpec((1, D), lambda i: (0, 0)), _resident((D, P))],
        out_specs=[pl.BlockSpec((tm, P), lambda i: (i, 0)), pl.BlockSpec((tm, D), lambda i: (i, 0))],
        out_shape=[SDS((T, P), F32), SDS((T, D), BF16)], compiler_params=_params(1))(x, g, w)


def _layer_norm_group(vg, lg, lb):
    mu = jnp.mean(vg, axis=-1, keepdims=True)
    xc = vg - mu
    rstd = lax.rsqrt(jnp.mean(xc * xc, axis=-1, keepdims=True) + EPS)
    vhat = xc * rstd
    return vhat, rstd, vhat * lg + lb


def _gmlp_fwd(proj, lg, lb, w_s, bs_t, A):
    T = proj.shape[0]
    G = A // GROUP_DIM
    tm = _tile(T, 512)
    nc = tm // CHUNK

    def body(u_ref, v_ref, lg_ref, lb_ref, w_ref, bst_ref, a_ref):
        row = lax.broadcasted_iota(jnp.int32, (CHUNK, CHUNK), 0)
        col = lax.broadcasted_iota(jnp.int32, (CHUNK, CHUNK), 1)
        causal = row >= col
        for g in range(G):
            sl = slice(g * GROUP_DIM, (g + 1) * GROUP_DIM)
            _, _, vn = _layer_norm_group(_gelu(v_ref[:, sl]), lg_ref[:, sl], lb_ref[:, sl])
            vnb = vn.astype(BF16)
            wm = jnp.where(causal, w_ref[g], 0.0).astype(BF16)
            ug = _gelu(u_ref[:, sl])
            bcol = bst_ref[:, g:g + 1]
            for c in range(nc):
                rs = slice(c * CHUNK, (c + 1) * CHUNK)
                a_ref[rs, sl] = ug[rs] * (_dot(wm, vnb[rs], NN) + bcol)

    return _CHAIN.call(
        body, name="gmlp_fwd", grid=(T // tm,),
        in_specs=[pl.BlockSpec((tm, A), lambda i: (i, 0)), pl.BlockSpec((tm, A), lambda i: (i, 1)),
                  pl.BlockSpec((1, A), lambda i: (0, 0)), pl.BlockSpec((1, A), lambda i: (0, 0)),
                  pl.BlockSpec((G, CHUNK, CHUNK), lambda i: (0, 0, 0)), pl.BlockSpec((CHUNK, G), lambda i: (0, 0))],
        out_specs=pl.BlockSpec((tm, A), lambda i: (i, 0)),
        out_shape=SDS((T, A), F32), compiler_params=_params(1))(proj, proj, lg, lb, w_s, bs_t)


def _attn_masks(first_tile):
    ii = lax.broadcasted_iota(jnp.int32, (CHUNK, 2 * CHUNK), 0)
    jj = lax.broadcasted_iota(jnp.int32, (CHUNK, 2 * CHUNK), 1)
    in_window = (jj > ii) & (jj <= ii + CHUNK)
    first_mask = in_window & jnp.logical_or(jnp.logical_not(first_tile), jj >= CHUNK)
    return in_window, first_mask


def _softmax_with_sink(s, sink, axis):
    m = jnp.maximum(jnp.max(s, axis=axis, keepdims=True), sink)
    p = jnp.exp(s - m)
    e_sink = jnp.exp(sink - m)
    inv = 1.0 / (jnp.sum(p, axis=axis, keepdims=True) + e_sink)
    return p * inv, e_sink * inv


def _pad_heads(band, group):
    lane = lax.broadcasted_iota(jnp.int32, band.shape, 1)
    if group == 0:
        low = jnp.where(lane < HEAD_DIM, band, 0.0)
        high = pltpu.roll(low, HEAD_DIM, 1)
    else:
        high = jnp.where(lane >= HEAD_DIM, band, 0.0)
        low = pltpu.roll(high, HEAD_DIM, 1)
    return low.astype(BF16), high.astype(BF16)


def _attn_specs(tq, A, B, reverse_tiles=None):
    nb = tq // CHUNK
    kcol = (2 * A + B) // LANE
    if reverse_tiles is None:
        tile = lambda i: i
    else:
        tile = lambda i: reverse_tiles - 1 - i
    prev = lambda i: jnp.maximum(tile(i) * nb - 1, 0)
    return [pl.BlockSpec((tq, B), lambda i: (tile(i), 2 * A // B)),
            pl.BlockSpec((tq, LANE), lambda i: (tile(i), kcol)),
            pl.BlockSpec((tq, LANE), lambda i: (tile(i), kcol + 1)),
            pl.BlockSpec((CHUNK, LANE), lambda i: (prev(i), kcol)),
            pl.BlockSpec((CHUNK, LANE), lambda i: (prev(i), kcol + 1))]


def _attn_fwd(proj, bias, sinks, A, B):
    T = proj.shape[0]
    H = B // HEAD_DIM
    qpk = H // KV_HEADS
    tq = _tile(T, 512)
    nb = tq // CHUNK

    scale = HEAD_DIM ** -0.5

    def body(sink_ref, q_ref, k_ref, v_ref, kp_ref, vp_ref, bias_ref, o_ref):
        in_window, first_mask = _attn_masks(pl.program_id(0) == 0)
        for b in range(nb):
            rows = slice(b * CHUNK, (b + 1) * CHUNK)
            if b == 0:
                kprev, vprev, mask = kp_ref[...], vp_ref[...], first_mask
            else:
                prows = slice((b - 1) * CHUNK, b * CHUNK)
                kprev, vprev, mask = k_ref[prows, :], v_ref[prows, :], in_window
            kband = jnp.concatenate([kprev, k_ref[rows, :]], axis=0)
            vband = jnp.concatenate([vprev, v_ref[rows, :]], axis=0)
            outs = []
            for g in range(KV_HEADS):
                k_low, k_high = _pad_heads(kband, g)
                v_both = jnp.concatenate(_pad_heads(vband, g), axis=0)
                for pair in range(qpk // 2):
                    h = g * qpk + 2 * pair
                    qs = (q_ref[rows, h * HEAD_DIM:(h + 2) * HEAD_DIM] * scale).astype(BF16)
                    probs = []
                    for head, kz in ((h, k_low), (h + 1, k_high)):
                        s = jnp.where(mask, _dot(qs, kz, NT) + bias_ref[head], NEG)
                        probs.append(_softmax_with_sink(s, sink_ref[head], -1)[0])
                    outs.append(_dot(jnp.concatenate(probs, axis=1).astype(BF16), v_both, NN))
            o_ref[rows, :] = jnp.concatenate(outs, axis=1)

    return _CHAIN.call(
        body, name="attn_fwd", grid=(T // tq,),
        in_specs=[pl.BlockSpec(memory_space=pltpu.SMEM)] + _attn_specs(tq, A, B)
        + [pl.BlockSpec((H, CHUNK, 2 * CHUNK), lambda i: (0, 0, 0))],
        out_specs=pl.BlockSpec((tq, B), lambda i: (i, 0)),
        out_shape=SDS((T, B), F32), compiler_params=_params(1))(sinks, proj, proj, proj, proj, proj, bias)


def _outproj_fwd(a, b, ga, gb, x, w):
    T, A = a.shape
    B = b.shape[1]
    D = x.shape[1]
    tm = _tile(T, 512)

    def body(a_ref, b_ref, ga_ref, gb_ref, x_ref, w_ref, h_ref, mix_ref):
        av, bv = a_ref[...], b_ref[...]
        mix_ref[:, :A] = (av * _rms_stats(av) * ga_ref[...]).astype(BF16)
        mix_ref[:, A:] = (bv * _rms_stats(bv) * gb_ref[...]).astype(BF16)
        h_ref[...] = x_ref[...] + _dot(mix_ref[...], w_ref[...], NN)

    return _CHAIN.call(
        body, name="outproj_fwd", grid=(T // tm,),
        in_specs=[pl.BlockSpec((tm, A), lambda i: (i, 0)), pl.BlockSpec((tm, B), lambda i: (i, 0)),
                  pl.BlockSpec((1, A), lambda i: (0, 0)), pl.BlockSpec((1, B), lambda i: (0, 0)),
                  pl.BlockSpec((tm, D), lambda i: (i, 0)), _resident((A + B, D))],
        out_specs=[pl.BlockSpec((tm, D), lambda i: (i, 0)), pl.BlockSpec((tm, A + B), lambda i: (i, 0))],
        out_shape=[SDS((T, D), F32), SDS((T, A + B), BF16)], compiler_params=_params(1))(a, b, ga, gb, x, w)


def _ffn_up(h1, g, w_up):
    T, D = h1.shape
    Fb = w_up.shape[2]
    F = N_DEV * Fb
    tm, tf = _tile(T, 1024), _tile(Fb, 1024)
    per = Fb // tf

    def body(h_ref, g_ref, wu_ref, z_ref, n_ref, nbuf):
        @pl.when(pl.program_id(1) == 0)
        def _():
            hv = h_ref[...]
            n = (hv * _rms_stats(hv) * g_ref[...]).astype(BF16)
            nbuf[...] = n
            n_ref[...] = n

        z_ref[...] = jnp.maximum(_dot(nbuf[...], wu_ref[...], NN), 0.0).astype(BF16)

    return _CHAIN.call(
        body, name="ffn_up", grid=(T // tm, F // tf),
        in_specs=[pl.BlockSpec((tm, D), lambda i, j: (i, 0)), pl.BlockSpec((1, D), lambda i, j: (0, 0)),
                  pl.BlockSpec((None, D, tf), lambda i, j: (j // per, 0, j % per))],
        out_specs=[pl.BlockSpec((tm, tf), lambda i, j: (i, j)), pl.BlockSpec((tm, D), lambda i, j: (i, 0))],
        out_shape=[SDS((T, F), BF16), SDS((T, D), BF16)],
        scratch_shapes=[pltpu.VMEM((tm, D), BF16)], compiler_params=_params(2))(h1, g, w_up)


def _ffn_down(h1, z, w_down):
    T, D = h1.shape
    F = w_down.shape[0]
    tm, tk = _tile(T, 1024), _tile(F, 1024)
    nk = F // tk

    def body(h_ref, z_ref, wd_ref, h2_ref):
        k = pl.program_id(1)

        @pl.when(k == 0)
        def _():
            h2_ref[...] = h_ref[...]

        zf = z_ref[...].astype(F32)
        h2_ref[...] += _dot((zf * zf).astype(BF16), wd_ref[...], NN)

    return _CHAIN.call(
        body, name="ffn_down", grid=(T // tm, nk),
        in_specs=[pl.BlockSpec((tm, D), lambda i, k: (i, 0)), pl.BlockSpec((tm, tk), lambda i, k: (i, k)),
                  pl.BlockSpec((tk, D), lambda i, k: (k, 0))],
        out_specs=pl.BlockSpec((tm, D), lambda i, k: (i, 0)),
        out_shape=SDS((T, D), F32), compiler_params=_params(2))(h1, z, w_down)


def _final_loss(h2, g, target):
    T, D = h2.shape
    tm = _tile(T, 512)

    def body(h_ref, g_ref, t_ref, loss_ref, dg_ref, dh_ref, dhb_ref):
        @pl.when(pl.program_id(0) == 0)
        def _():
            loss_ref[...] = jnp.zeros_like(loss_ref)
            dg_ref[...] = jnp.zeros_like(dg_ref)

        hv, gv = h_ref[...], g_ref[...]
        r = _rms_stats(hv)
        hn = hv * r
        e = hn * gv - t_ref[...]
        loss_ref[...] += (0.5 / D) * jnp.sum(jnp.sum(e * e, axis=-1, keepdims=True), axis=0, keepdims=True)
        dy = e * (1.0 / D)
        dg_ref[...] += jnp.sum(dy * hn, axis=0, keepdims=True)
        dh = _rms_bwd(dy, hv, r, gv)
        dh_ref[...] = dh
        dhb_ref[...] = dh.astype(BF16)

    return _CHAIN.call(
        body, name="final_loss", grid=(T // tm,),
        in_specs=[pl.BlockSpec((tm, D), lambda i: (i, 0)), pl.BlockSpec((1, D), lambda i: (0, 0)),
                  pl.BlockSpec((tm, D), lambda i: (i, 0))],
        out_specs=[pl.BlockSpec((1, 1), lambda i: (0, 0)), pl.BlockSpec((1, D), lambda i: (0, 0)),
                   pl.BlockSpec((tm, D), lambda i: (i, 0)), pl.BlockSpec((tm, D), lambda i: (i, 0))],
        out_shape=[SDS((1, 1), F32), SDS((1, D), F32), SDS((T, D), F32), SDS((T, D), BF16)],
        compiler_params=_params(1))(h2, g, target)


def _ffn_down_bwd(dh2b, z, w_down):
    T, D = dh2b.shape
    F = w_down.shape[0]
    tm, tf = _tile(T, 1024), _tile(F, 1024)

    def body(dh_ref, z_ref, wd_ref, dzp_ref):
        dzz = _dot(dh_ref[...], wd_ref[...], NT)
        dzp_ref[...] = (dzz * (2.0 * z_ref[...].astype(F32))).astype(BF16)

    return _CHAIN.call(
        body, name="ffn_down_bwd", grid=(T // tm, F // tf),
        in_specs=[pl.BlockSpec((tm, D), lambda i, j: (i, 0)), pl.BlockSpec((tm, tf), lambda i, j: (i, j)),
                  pl.BlockSpec((tf, D), lambda i, j: (j, 0))],
        out_specs=pl.BlockSpec((tm, tf), lambda i, j: (i, j)),
        out_shape=SDS((T, F), BF16), compiler_params=_params(2))(dh2b, z, w_down)


def _ffn_up_bwd(dzp, dh2, h1, g, w_up):
    T, D = h1.shape
    Fb = w_up.shape[2]
    F = N_DEV * Fb
    tm, tk = _tile(T, 512), _tile(Fb, 1024)
    per = Fb // tk
    nk = F // tk

    def body(dzp_ref, dh_ref, h_ref, g_ref, wu_ref, dh1_ref, dh1b_ref, dg_ref, acc):
        i, k = pl.program_id(0), pl.program_id(1)

        @pl.when((i == 0) & (k == 0))
        def _():
            dg_ref[...] = jnp.zeros_like(dg_ref)

        part = _dot(dzp_ref[...], wu_ref[...], NT)

        @pl.when(k == 0)
        def _():
            acc[...] = part

        @pl.when(k > 0)
        def _():
            acc[...] += part

        @pl.when(k == nk - 1)
        def _():
            hv, gv, dn = h_ref[...], g_ref[...], acc[...]
            r = _rms_stats(hv)
            dg_ref[...] += jnp.sum(dn * (hv * r), axis=0, keepdims=True)
            dh1 = dh_ref[...] + _rms_bwd(dn, hv, r, gv)
            dh1_ref[...] = dh1
            dh1b_ref[...] = dh1.astype(BF16)

    return _CHAIN.call(
        body, name="ffn_up_bwd", grid=(T // tm, nk),
        in_specs=[pl.BlockSpec((tm, tk), lambda i, k: (i, k)), pl.BlockSpec((tm, D), lambda i, k: (i, 0)),
                  pl.BlockSpec((tm, D), lambda i, k: (i, 0)), pl.BlockSpec((1, D), lambda i, k: (0, 0)),
                  pl.BlockSpec((None, D, tk), lambda i, k: (k // per, 0, k % per))],
        out_specs=[pl.BlockSpec((tm, D), lambda i, k: (i, 0)), pl.BlockSpec((tm, D), lambda i, k: (i, 0)),
                   pl.BlockSpec((1, D), lambda i, k: (0, 0))],
        out_shape=[SDS((T, D), F32), SDS((T, D), BF16), SDS((1, D), F32)],
        scratch_shapes=[pltpu.VMEM((tm, D), F32)], compiler_params=_params(2))(dzp, dh2, h1, g, w_up)


def _matmul_tn(a, b, name, square_a=False, col_blocks=None):
    T, K = a.shape
    N = b.shape[1]
    tt, tk = _tile(T, 1024), _tile(K, 1024)
    tn = _tile(N if col_blocks is None else N // col_blocks, 1792)
    nt = T // tt

    def body(a_ref, b_ref, o_ref, acc):
        t = pl.program_id(2)

        @pl.when(t == 0)
        def _():
            acc[...] = jnp.zeros_like(acc)

        av = a_ref[...]
        if square_a:
            af = av.astype(F32)
            av = (af * af).astype(BF16)
        acc[...] += _dot(av, b_ref[...], TN)

        @pl.when(t == nt - 1)
        def _():
            o_ref[...] = acc[...].astype(o_ref.dtype)

    if col_blocks is None:
        out_shape = SDS((K, N), BF16)
        out_spec = pl.BlockSpec((tk, tn), lambda i, j, t: (i, j))
    else:
        per = (N // col_blocks) // tn
        out_shape = SDS((col_blocks, K, N // col_blocks), BF16)
        out_spec = pl.BlockSpec((None, tk, tn), lambda i, j, t: (j // per, i, j % per))
    return _CHAIN.call(
        body, name=name, grid=(K // tk, N // tn, nt),
        in_specs=[pl.BlockSpec((tt, tk), lambda i, j, t: (t, i)), pl.BlockSpec((tt, tn), lambda i, j, t: (t, j))],
        out_specs=out_spec, out_shape=out_shape,
        scratch_shapes=[pltpu.VMEM((tk, tn), F32)], compiler_params=_params(3))(a, b)


def _outproj_bwd(dh1b, w, a, b, ga, gb):
    T, D = dh1b.shape
    A, B = a.shape[1], b.shape[1]
    tm = _tile(T, 512)

    def body(dh_ref, w_ref, a_ref, b_ref, ga_ref, gb_ref, da_ref, db_ref, dga_ref, dgb_ref):
        @pl.when(pl.program_id(0) == 0)
        def _():
            dga_ref[...] = jnp.zeros_like(dga_ref)
            dgb_ref[...] = jnp.zeros_like(dgb_ref)

        dmix = _dot(dh_ref[...], w_ref[...], NT)
        for src_ref, g_ref, dx_ref, dg_ref, dn in ((a_ref, ga_ref, da_ref, dga_ref, dmix[:, :A]),
                                                   (b_ref, gb_ref, db_ref, dgb_ref, dmix[:, A:])):
            xv = src_ref[...]
            r = _rms_stats(xv)
            dg_ref[...] += jnp.sum(dn * (xv * r), axis=0, keepdims=True)
            dx_ref[...] = _rms_bwd(dn, xv, r, g_ref[...])

    return _CHAIN.call(
        body, name="outproj_bwd", grid=(T // tm,),
        in_specs=[pl.BlockSpec((tm, D), lambda i: (i, 0)), _resident((A + B, D)),
                  pl.BlockSpec((tm, A), lambda i: (i, 0)), pl.BlockSpec((tm, B), lambda i: (i, 0)),
                  pl.BlockSpec((1, A), lambda i: (0, 0)), pl.BlockSpec((1, B), lambda i: (0, 0))],
        out_specs=[pl.BlockSpec((tm, A), lambda i: (i, 0)), pl.BlockSpec((tm, B), lambda i: (i, 0)),
                   pl.BlockSpec((1, A), lambda i: (0, 0)), pl.BlockSpec((1, B), lambda i: (0, 0))],
        out_shape=[SDS((T, A), F32), SDS((T, B), F32), SDS((1, A), F32), SDS((1, B), F32)],
        compiler_params=_params(1))(dh1b, w, a, b, ga, gb)


def _gmlp_bwd(proj, da, lg, lb, w_s, w_st, bs_t, A):
    T = proj.shape[0]
    G = A // GROUP_DIM
    tm = _tile(T, 512)
    nc = tm // CHUNK

    def body(u_ref, v_ref, da_ref, lg_ref, lb_ref, w_ref, wt_ref, bst_ref, duv_ref, dlg_ref, dlb_ref, dw_ref, dbs_ref):
        @pl.when(pl.program_id(0) == 0)
        def _():
            dlg_ref[...] = jnp.zeros_like(dlg_ref)
            dlb_ref[...] = jnp.zeros_like(dlb_ref)
            dw_ref[...] = jnp.zeros_like(dw_ref)
            dbs_ref[...] = jnp.zeros_like(dbs_ref)

        row = lax.broadcasted_iota(jnp.int32, (CHUNK, CHUNK), 0)
        col = lax.broadcasted_iota(jnp.int32, (CHUNK, CHUNK), 1)
        lower = row >= col
        upper = row <= col
        for g in range(G):
            sl = slice(g * GROUP_DIM, (g + 1) * GROUP_DIM)
            lgv = lg_ref[:, sl]
            vg, vg_grad = _gelu_and_grad(v_ref[:, sl])
            vhat, rstd, vn = _layer_norm_group(vg, lgv, lb_ref[:, sl])
            vnb = vn.astype(BF16)
            ug, ug_grad = _gelu_and_grad(u_ref[:, sl])
            dav = da_ref[:, sl]
            wm = jnp.where(lower, w_ref[g], 0.0).astype(BF16)
            wmt = jnp.where(upper, wt_ref[g], 0.0).astype(BF16)
            bcol = bst_ref[:, g:g + 1]
            dw_acc = jnp.zeros((CHUNK, CHUNK), F32)
            dbs_acc = jnp.zeros((CHUNK, 1), F32)
            dvn_parts = []
            dug_parts = []
            for c in range(nc):
                rs = slice(c * CHUNK, (c + 1) * CHUNK)
                mixed = _dot(wm, vnb[rs], NN) + bcol
                dug_parts.append(dav[rs] * mixed)
                dmix = dav[rs] * ug[rs]
                dbs_acc = dbs_acc + jnp.sum(dmix, axis=-1, keepdims=True)
                dmixb = dmix.astype(BF16)
                dw_acc = dw_acc + _dot(dmixb, vnb[rs], NT)
                dvn_parts.append(_dot(wmt, dmixb, NN))
            dvn = jnp.concatenate(dvn_parts, axis=0)
            dug = jnp.concatenate(dug_parts, axis=0)
            dw_ref[g] += jnp.where(lower, dw_acc, 0.0)
            dbs_ref[:, g:g + 1] += dbs_acc
            dlg_ref[:, sl] += jnp.sum(dvn * vhat, axis=0, keepdims=True)
            dlb_ref[:, sl] += jnp.sum(dvn, axis=0, keepdims=True)
            dvhat = dvn * lgv
            dvg = rstd * (dvhat - jnp.mean(dvhat, axis=-1, keepdims=True)
                          - vhat * jnp.mean(dvhat * vhat, axis=-1, keepdims=True))
            duv_ref[:, sl] = (dug * ug_grad).astype(BF16)
            duv_ref[:, A + g * GROUP_DIM:A + (g + 1) * GROUP_DIM] = (dvg * vg_grad).astype(BF16)

    return _CHAIN.call(
        body, name="gmlp_bwd", grid=(T // tm,),
        in_specs=[pl.BlockSpec((tm, A), lambda i: (i, 0)), pl.BlockSpec((tm, A), lambda i: (i, 1)),
                  pl.BlockSpec((tm, A), lambda i: (i, 0)),
                  pl.BlockSpec((1, A), lambda i: (0, 0)), pl.BlockSpec((1, A), lambda i: (0, 0)),
                  pl.BlockSpec((G, CHUNK, CHUNK), lambda i: (0, 0, 0)),
                  pl.BlockSpec((G, CHUNK, CHUNK), lambda i: (0, 0, 0)), pl.BlockSpec((CHUNK, G), lambda i: (0, 0))],
        out_specs=[pl.BlockSpec((tm, 2 * A), lambda i: (i, 0)),
                   pl.BlockSpec((1, A), lambda i: (0, 0)), pl.BlockSpec((1, A), lambda i: (0, 0)),
                   pl.BlockSpec((G, CHUNK, CHUNK), lambda i: (0, 0, 0)), pl.BlockSpec((CHUNK, G), lambda i: (0, 0))],
        out_shape=[SDS((T, 2 * A), BF16), SDS((1, A), F32), SDS((1, A), F32),
                   SDS((G, CHUNK, CHUNK), F32), SDS((CHUNK, G), F32)],
        compiler_params=_params(1))(proj, proj, da, lg, lb, w_s, w_st, bs_t)


def _attn_bwd(proj, do, duv, bias_t, sinks, A, B):
    T, P = proj.shape
    H = B // HEAD_DIM
    qpk = H // KV_HEADS
    tq = _tile(T, 512)
    nb = tq // CHUNK
    n_tiles = T // tq
    scale = HEAD_DIM ** -0.5
    rev = lambda i: n_tiles - 1 - i

    def body(sink_ref, q_ref, k_ref, v_ref, kp_ref, vp_ref, do_ref, duv_ref, bias_ref,
             dproj_ref, dbias_ref, dsink_ref, carry, dkv, sacc):
        step = pl.program_id(0)

        @pl.when(step == 0)
        def _():
            carry[...] = jnp.zeros_like(carry)
            sacc[...] = jnp.zeros_like(sacc)
            dbias_ref[...] = jnp.zeros_like(dbias_ref)

        jj = lax.broadcasted_iota(jnp.int32, (2 * CHUNK, CHUNK), 0)
        ii = lax.broadcasted_iota(jnp.int32, (2 * CHUNK, CHUNK), 1)
        in_window = (jj > ii) & (jj <= ii + CHUNK)
        first_mask = in_window & jnp.logical_or(step != n_tiles - 1, jj >= CHUNK)
        low_query = lax.broadcasted_iota(jnp.int32, (CHUNK, LANE), 1) < HEAD_DIM
        low_key = lax.broadcasted_iota(jnp.int32, (2 * CHUNK, LANE), 1) < HEAD_DIM

        def split_pair(pair_bf16):
            zero = jnp.zeros_like(pair_bf16)
            return jnp.concatenate([jnp.where(low_query, pair_bf16, zero), jnp.where(low_query, zero, pair_bf16)], axis=0)

        dproj_ref[:, :2 * A] = duv_ref[...]
        dkv[...] = jnp.zeros_like(dkv)
        for b in range(nb):
            rows = slice(b * CHUNK, (b + 1) * CHUNK)
            band = slice(b * CHUNK, (b + 2) * CHUNK)
            if b == 0:
                kprev, vprev, mask = kp_ref[...], vp_ref[...], first_mask
            else:
                prows = slice((b - 1) * CHUNK, b * CHUNK)
                kprev, vprev, mask = k_ref[prows, :], v_ref[prows, :], in_window
            kband = jnp.concatenate([kprev, k_ref[rows, :]], axis=0)
            vband = jnp.concatenate([vprev, v_ref[rows, :]], axis=0)
            dq_parts, dk_groups, dv_groups = [], [], []
            for g in range(KV_HEADS):
                k_low, k_high = _pad_heads(kband, g)
                v_low, v_high = _pad_heads(vband, g)
                k_both = jnp.concatenate([k_low, k_high], axis=0)
                dk_acc = jnp.zeros((2 * CHUNK, LANE), F32)
                dv_acc = jnp.zeros((2 * CHUNK, LANE), F32)
                for pair in range(qpk // 2):
                    h = g * qpk + 2 * pair
                    cols = slice(h * HEAD_DIM, (h + 2) * HEAD_DIM)
                    qs = (q_ref[rows, cols] * scale).astype(BF16)
                    dob = do_ref[rows, cols].astype(BF16)
                    probs, dscores = [], []
                    for head, kz, vz in ((h, k_low, v_low), (h + 1, k_high, v_high)):
                        st = jnp.where(mask, _dot(kz, qs, NT) + bias_ref[head], NEG)
                        pt, p_sink = _softmax_with_sink(st, sink_ref[head], 0)
                        dpt = _dot(vz, dob, NT)
                        delta = jnp.sum(pt * dpt, axis=0, keepdims=True)
                        dst = pt * (dpt - delta)
                        dbias_ref[head] += dst
                        sacc[head:head + 1, :] += -(p_sink * delta)
                        probs.append(pt)
                        dscores.append(dst)
                    dk_acc = dk_acc + _dot(jnp.concatenate(dscores, axis=1).astype(BF16), split_pair(qs), NN)
                    dv_acc = dv_acc + _dot(jnp.concatenate(probs, axis=1).astype(BF16), split_pair(dob), NN)
                    dq_parts.append(_dot(jnp.concatenate(dscores, axis=0).astype(BF16), k_both, TN) * scale)
                dk_groups.append(dk_acc + pltpu.roll(dk_acc, HEAD_DIM, 1))
                dv_groups.append(dv_acc + pltpu.roll(dv_acc, HEAD_DIM, 1))
            dkv[band, :LANE] += jnp.where(low_key, dk_groups[0], dk_groups[1])
            dkv[band, LANE:] += jnp.where(low_key, dv_groups[0], dv_groups[1])
            dproj_ref[rows, 2 * A:2 * A + B] = jnp.concatenate(dq_parts, axis=1).astype(BF16)
        last = slice(tq, tq + CHUNK)
        dkv[last, :] += carry[...]
        dproj_ref[:, 2 * A + B:] = dkv[CHUNK:, :].astype(BF16)
        carry[...] = dkv[:CHUNK, :]

        @pl.when(step == n_tiles - 1)
        def _():
            dsink_ref[...] = jnp.sum(sacc[...], axis=1, keepdims=True)

    specs = _attn_specs(tq, A, B, reverse_tiles=n_tiles)
    return _CHAIN.call(
        body, name="attn_bwd", grid=(n_tiles,),
        in_specs=[pl.BlockSpec(memory_space=pltpu.SMEM)] + specs
        + [pl.BlockSpec((tq, B), lambda i: (rev(i), 0)), pl.BlockSpec((tq, 2 * A), lambda i: (rev(i), 0)),
           pl.BlockSpec((H, 2 * CHUNK, CHUNK), lambda i: (0, 0, 0))],
        out_specs=[pl.BlockSpec((tq, P), lambda i: (rev(i), 0)),
                   pl.BlockSpec((H, 2 * CHUNK, CHUNK), lambda i: (0, 0, 0)), pl.BlockSpec((H, 1), lambda i: (0, 0))],
        out_shape=[SDS((T, P), BF16), SDS((H, 2 * CHUNK, CHUNK), F32), SDS((H, 1), F32)],
        scratch_shapes=[pltpu.VMEM((CHUNK, 2 * LANE), F32), pltpu.VMEM((tq + CHUNK, 2 * LANE), F32),
                        pltpu.VMEM((H, LANE), F32)],
        compiler_params=_params(1))(sinks, proj, proj, proj, proj, proj, do, duv, bias_t)


def _bias_bwd(dbias, onehot):
    H = dbias.shape[0]
    nbk = onehot.shape[1]

    def body(d_ref, oh_ref, o_ref):
        hi, mid, lo = _split3(d_ref[...])
        oh = oh_ref[...]
        o_ref[...] = _dot(hi, oh, NN) + _dot(mid, oh, NN) + _dot(lo, oh, NN)

    return _CHAIN.call(body, name="bias_bwd", in_specs=[VMEM_SPEC] * 2, out_specs=VMEM_SPEC, out_shape=SDS((H, nbk), F32),
                       compiler_params=_params(0))(dbias, onehot)


def _inproj_bwd(dproj, w, x, dh1, g):
    T, P = dproj.shape
    D = x.shape[1]
    tm = _tile(T, 512)

    def body(dp_ref, w_ref, x_ref, dh_ref, g_ref, dx_ref, dg_ref):
        @pl.when(pl.program_id(0) == 0)
        def _():
            dg_ref[...] = jnp.zeros_like(dg_ref)

        dn = _dot(dp_ref[...], w_ref[...], NT)
        xv = x_ref[...]
        r = _rms_stats(xv)
        dg_ref[...] += jnp.sum(dn * (xv * r), axis=0, keepdims=True)
        dx_ref[...] = dh_ref[...] + _rms_bwd(dn, xv, r, g_ref[...])

    return _CHAIN.call(
        body, name="inproj_bwd", grid=(T // tm,),
        in_specs=[pl.BlockSpec((tm, P), lambda i: (i, 0)), _resident((D, P)),
                  pl.BlockSpec((tm, D), lambda i: (i, 0)), pl.BlockSpec((tm, D), lambda i: (i, 0)),
                  pl.BlockSpec((1, D), lambda i: (0, 0))],
        out_specs=[pl.BlockSpec((tm, D), lambda i: (i, 0)), pl.BlockSpec((1, D), lambda i: (0, 0))],
        out_shape=[SDS((T, D), F32), SDS((1, D), F32)], compiler_params=_params(1))(dproj, w, x, dh1, g)


def _adamw(w, g, m, v):
    m = ADAM_B1 * m + (1.0 - ADAM_B1) * g
    v = ADAM_B2 * v + (1.0 - ADAM_B2) * (g * g)
    m_hat = m / (1.0 - ADAM_B1 ** ADAM_STEP)
    v_hat = v / (1.0 - ADAM_B2 ** ADAM_STEP)
    delta = -ADAM_LR * (m_hat / (jnp.sqrt(v_hat) + ADAM_EPS) + ADAM_WD * w)
    return delta, m, v


def _adam_sharded(csum, recv, w, m, v, name):
    R, C = w.shape
    tr = _tile(R, 256, 16)

    def body(own_ref, recv_ref, w_ref, m_ref, v_ref, g_ref, d_ref, nm_ref, nv_ref):
        g = own_ref[...].astype(F32)
        for r in range(3):
            g = g + recv_ref[r].astype(F32)
        delta, nm, nv = _adamw(w_ref[...], g, m_ref[...], v_ref[...])
        g_ref[...] = g
        d_ref[...] = delta
        nm_ref[...] = nm
        nv_ref[...] = nv

    blk = pl.BlockSpec((tr, C), lambda i: (i, 0))
    return _CHAIN.call(
        body, name=name, grid=(R // tr,),
        in_specs=[pl.BlockSpec((None, tr, C), lambda i: (0, i, 0)), pl.BlockSpec((3, tr, C), lambda i: (0, i, 0)),
                  blk, blk, blk],
        out_specs=[blk] * 4, out_shape=[SDS((R, C), F32)] * 4, compiler_params=_params(1))(csum, recv, w, m, v)


def _adam_small(gathered, w, m, v):
    R = w.shape[0]

    def body(p_ref, w_ref, m_ref, v_ref, g_ref, d_ref, nm_ref, nv_ref):
        g = p_ref[0]
        for d in range(1, N_DEV):
            g = g + p_ref[d]
        delta, nm, nv = _adamw(w_ref[...], g, m_ref[...], v_ref[...])
        g_ref[...] = g
        d_ref[...] = delta
        nm_ref[...] = nm
        nv_ref[...] = nv

    return _CHAIN.call(body, name="adam_small", in_specs=[VMEM_SPEC] * 4, out_specs=[VMEM_SPEC] * 4,
                       out_shape=[SDS((R, LANE), F32)] * 4,
                       compiler_params=_params(0))(gathered, w, m, v)


def _pack(arrays):
    tile = 8 * LANE
    pieces = []
    for a in arrays:
        flat = a.reshape(-1).astype(F32)
        pieces.append(jnp.pad(flat, (0, (-flat.size) % tile)))
    return jnp.concatenate(pieces).reshape(-1, LANE)


def _unpack(packed, shapes):
    tile = 8 * LANE
    flat = packed.reshape(-1)
    out, off = [], 0
    for s in shapes:
        size = int(np.prod(s))
        out.append(flat[off:off + size].reshape(s))
        off += size + (-size) % tile
    return out


def kernel(x, rel_bias_table, mix_norm_g, w_in, gate_norm_g, gate_norm_b, w_spatial, b_spatial, attn_sinks, out_norm_a_g, out_norm_b_g, w_out, ffn_norm_g, w_up, w_down, final_norm_g, loss_target, m_rel_bias_table, m_mix_norm_g, m_w_in, m_gate_norm_g, m_gate_norm_b, m_w_spatial, m_b_spatial, m_attn_sinks, m_out_norm_a_g, m_out_norm_b_g, m_w_out, m_ffn_norm_g, m_w_up, m_w_down, m_final_norm_g, v_rel_bias_table, v_mix_norm_g, v_w_in, v_gate_norm_g, v_gate_norm_b, v_w_spatial, v_b_spatial, v_attn_sinks, v_out_norm_a_g, v_out_norm_b_g, v_w_out, v_ffn_norm_g, v_w_up, v_w_down, v_final_norm_g):
    T, D = x.shape[1], x.shape[2]
    A = D // 2
    B = D // 2
    G = A // GROUP_DIM
    H = B // HEAD_DIM
    P = 2 * A + B + 2 * KV_HEADS * HEAD_DIM
    Pb = w_in.shape[2]
    xs = x.reshape(T, D)
    target = loss_target.reshape(T, D)

    shards = [w_in[0].astype(BF16), w_out[0].astype(BF16), w_up[0].astype(BF16), w_down[0].astype(BF16)]
    _CHAIN.token = None
    gather = _gather_begin(shards, "gather_start")
    _gather_pass_on(gather, [0], "gather_in_pass")
    (win_g,) = _gather_end(gather, [0], "gather_in_end")
    win_full = jnp.transpose(win_g, (1, 0, 2)).reshape(D, P)

    g1, g2, g3 = mix_norm_g.reshape(1, D), ffn_norm_g.reshape(1, D), final_norm_g.reshape(1, D)
    lg, lb = gate_norm_g.reshape(1, A), gate_norm_b.reshape(1, A)
    ws = w_spatial[0]
    ws_t = jnp.swapaxes(ws, 1, 2)
    bs_t = jnp.transpose(b_spatial[0])
    ga, gb = out_norm_a_g.reshape(1, A), out_norm_b_g.reshape(1, B)
    sinks = attn_sinks.reshape(H)
    bucket, in_window = _t5_bucket()
    onehot_np = ((bucket[:, :, None] == np.arange(N_BUCKETS)) & in_window[:, :, None]).astype(np.float32)
    onehot = jnp.asarray(onehot_np.reshape(-1, N_BUCKETS)).astype(BF16)
    onehot_kq = jnp.asarray(onehot_np.transpose(1, 0, 2).reshape(-1, N_BUCKETS)).astype(BF16)

    bias, bias_t = _bias_fwd(jnp.transpose(rel_bias_table), jnp.transpose(onehot), jnp.transpose(onehot_kq))
    bias, bias_t = bias.reshape(H, CHUNK, 2 * CHUNK), bias_t.reshape(H, 2 * CHUNK, CHUNK)
    proj, n1 = _inproj_fwd(xs, g1, win_full)
    _gather_pass_on(gather, [1], "gather_out_pass")
    a_out = _gmlp_fwd(proj, lg, lb, ws, bs_t, A)
    b_out = _attn_fwd(proj, bias, sinks, A, B)
    _gather_pass_on(gather, [2], "gather_up_pass")
    (wout_g,) = _gather_end(gather, [1], "gather_out_end")
    wout_full = wout_g.reshape(A + B, D)
    h1, mixed = _outproj_fwd(a_out, b_out, ga, gb, xs, wout_full)
    _gather_pass_on(gather, [3], "gather_down_pass")
    (wup_g,) = _gather_end(gather, [2], "gather_up_end")
    z, n2 = _ffn_up(h1, g2, wup_g)
    (wdown_g,) = _gather_end(gather, [3], "gather_down_end")
    h2 = _ffn_down(h1, z, wdown_g.reshape(-1, D))
    loss_part, dg3, dh2, dh2b = _final_loss(h2, g3, target)

    def reduce_to_chip(state, name):
        part, received = _sibling_exchange_end(state, name + "_sib_end")
        return _chip_exchange_begin(_chip_sum(part, received, name + "_chip_sum"), name + "_chip")

    dwdown = _matmul_tn(z, dh2b, "grad_w_down", square_a=True).reshape(wdown_g.shape)
    sib_down = _sibling_exchange_begin(dwdown, "rs_down_sib")
    dzp = _ffn_down_bwd(dh2b, z, wdown_g.reshape(-1, D))
    chip_down = reduce_to_chip(sib_down, "rs_down")
    dwup = _matmul_tn(n2, dzp, "grad_w_up", col_blocks=N_DEV)
    sib_up = _sibling_exchange_begin(dwup, "rs_up_sib")
    dh1, dh1b, dg2 = _ffn_up_bwd(dzp, dh2, h1, g2, wup_g)
    chip_up = reduce_to_chip(sib_up, "rs_up")
    da, db, dga, dgb = _outproj_bwd(dh1b, wout_full, a_out, b_out, ga, gb)
    dwout = _matmul_tn(mixed, dh1b, "grad_w_out").reshape(wout_g.shape)
    sib_out = _sibling_exchange_begin(dwout, "rs_out_sib")
    duv, dlg, dlb, dws, dbs_t = _gmlp_bwd(proj, da, lg, lb, ws, ws_t, bs_t, A)
    dproj, dbias_t, dsinks = _attn_bwd(proj, db, duv, bias_t, sinks, A, B)
    chip_out = reduce_to_chip(sib_out, "rs_out")
    dtable_t = _bias_bwd(dbias_t.reshape(H, -1), onehot_kq)
    dwin = _matmul_tn(n1, dproj, "grad_w_in")
    dwin = jnp.transpose(dwin.reshape(D, N_DEV, Pb), (1, 0, 2))
    sib_in = _sibling_exchange_begin(dwin, "rs_in_sib")
    grad_x, dg1 = _inproj_bwd(dproj, win_full, xs, dh1, g1)

    small_w = [rel_bias_table, mix_norm_g, gate_norm_g, gate_norm_b, w_spatial, b_spatial, attn_sinks,
               out_norm_a_g, out_norm_b_g, ffn_norm_g, final_norm_g]
    small_m = [m_rel_bias_table, m_mix_norm_g, m_gate_norm_g, m_gate_norm_b, m_w_spatial, m_b_spatial, m_attn_sinks,
               m_out_norm_a_g, m_out_norm_b_g, m_ffn_norm_g, m_final_norm_g]
    small_v = [v_rel_bias_table, v_mix_norm_g, v_gate_norm_g, v_gate_norm_b, v_w_spatial, v_b_spatial, v_attn_sinks,
               v_out_norm_a_g, v_out_norm_b_g, v_ffn_norm_g, v_final_norm_g]
    small_g = [jnp.transpose(dtable_t), dg1, dlg, dlb, dws, jnp.transpose(dbs_t), dsinks, dga, dgb, dg2, dg3]
    shapes = [w.shape for w in small_w]
    big = [None] * 4

    def adam_of(k, state, w, m, v):
        csum, received = _chip_exchange_end(state, "rs_%d_end" % k)
        big[k] = [o.reshape(w.shape) for o in _adam_sharded(csum, received, w[0], m[0], v[0], "adam_%d" % k)]

    small_gather = _gather_begin([_pack(small_g)], "small_gather_start")
    chip_in = reduce_to_chip(sib_in, "rs_in")
    _gather_pass_on(small_gather, [0], "small_gather_pass")
    adam_of(3, chip_down, w_down, m_w_down, v_w_down)
    (gathered,) = _gather_end(small_gather, [0], "small_gather_end")
    sg, sd, sm, sv = [_unpack(o, shapes) for o in _adam_small(gathered, _pack(small_w), _pack(small_m), _pack(small_v))]
    adam_of(2, chip_up, w_up, m_w_up, v_w_up)
    adam_of(1, chip_out, w_out, m_w_out, v_w_out)
    adam_of(0, chip_in, w_in, m_w_in, v_w_in)

    loss = lax.psum(loss_part[0, 0], ("x", "y", "c"))

    order = ["s0", "s1", "b0", "s2", "s3", "s4", "s5", "s6", "s7", "s8", "b1", "s9", "b2", "b3", "s10"]

    def group(idx):
        small = (sg, sd, sm, sv)[idx]
        return [small[int(t[1:])] if t[0] == "s" else big[int(t[1:])][idx] for t in order]

    return (loss, grad_x.reshape(x.shape), *group(0), *group(1), *group(2), *group(3))
```

```python
import functools
import math

import numpy as np
import jax
import jax.numpy as jnp
from jax import lax
from jax.experimental import pallas as pl
from jax.experimental.pallas import tpu as pltpu

F32 = jnp.float32
BF16 = jnp.bfloat16
SDS = jax.ShapeDtypeStruct
MESH = pl.DeviceIdType.MESH

N_DEV = 8
EPS = 1e-5
NEG = -1e30
CHUNK = 128
GROUP_DIM = 128
HEAD_DIM = 64
KV_HEADS = 2
N_BUCKETS = 32
MAX_DISTANCE = 128
ADAM_LR, ADAM_B1, ADAM_B2, ADAM_EPS, ADAM_WD, ADAM_STEP = 0.001, 0.9, 0.999, 1e-08, 0.01, 10
GELU_C0 = math.sqrt(2.0 / math.pi)
GELU_C1 = 0.044715

V7X_VMEM_BYTES = 64 * 1024 * 1024
VMEM_LIMIT = V7X_VMEM_BYTES - 8 * 1024 * 1024
LANE = 128

NN = ((1,), (0,))
NT = ((1,), (1,))
TN = ((0,), (0,))


def _dot(a, b, dims):
    return lax.dot_general(a, b, (dims, ((), ())), preferred_element_type=F32)


def _tile(n, pref, unit=LANE):
    best = None
    for t in range(unit, min(n, pref) + 1, unit):
        if n % t == 0:
            best = t
    return n if best is None else best


def _params(n_grid):
    return pltpu.CompilerParams(dimension_semantics=("arbitrary",) * n_grid, vmem_limit_bytes=VMEM_LIMIT)


def _resident(shape):
    return pl.BlockSpec(shape, lambda i: (0, 0), pipeline_mode=pl.Buffered(1))


def _gelu(x):
    return 0.5 * x * (1.0 + jnp.tanh(GELU_C0 * (x + GELU_C1 * x * x * x)))


def _gelu_and_grad(x):
    x2 = x * x
    t = jnp.tanh(GELU_C0 * x * (1.0 + GELU_C1 * x2))
    val = 0.5 * x * (1.0 + t)
    grad = 0.5 * (1.0 + t) + 0.5 * x * (1.0 - t * t) * (GELU_C0 * (1.0 + 3.0 * GELU_C1 * x2))
    return val, grad


def _rms_stats(x):
    return lax.rsqrt(jnp.mean(x * x, axis=-1, keepdims=True) + EPS)


def _rms_bwd(dy, x, r, g):
    w = dy * g
    return r * w - x * (r * r * r) * jnp.mean(w * x, axis=-1, keepdims=True)


ROW_CHUNK = 16
ROW_UNROLL = 4


def _for_row_chunks(n_rows, step, init):
    per_trip = ROW_CHUNK * ROW_UNROLL

    def trip(t, carry):
        for u in range(ROW_UNROLL):
            carry = step(pl.ds(pl.multiple_of(t * per_trip + u * ROW_CHUNK, ROW_CHUNK), ROW_CHUNK), carry)
        return carry

    return lax.fori_loop(0, n_rows // per_trip, trip, init)


def _fold_rows(v):
    return jnp.sum(v.reshape(ROW_CHUNK // 8, 8, v.shape[1]), axis=0)


def _rms_bwd_by_rows(n_rows, dn_ref, x_ref, g_ref, res_ref, dx_ref, dxb_ref):
    gv = g_ref[...]

    def chunk(rows, dg):
        xv, dn = x_ref[rows, :], dn_ref[rows, :]
        r = _rms_stats(xv)
        dx = _rms_bwd(dn, xv, r, gv)
        if res_ref is not None:
            dx = dx + res_ref[rows, :]
        dx_ref[rows, :] = dx
        if dxb_ref is not None:
            dxb_ref[rows, :] = dx.astype(BF16)
        return dg + _fold_rows(dn * (xv * r))

    return jnp.sum(_for_row_chunks(n_rows, chunk, jnp.zeros((8, x_ref.shape[1]), F32)), axis=0, keepdims=True)


def _t5_bucket():
    i = np.arange(CHUNK)[:, None]
    j = np.arange(2 * CHUNK)[None, :]
    rel = np.maximum(i + CHUNK - j, 0)
    n_exact = N_BUCKETS // 2
    relf = np.maximum(rel, n_exact).astype(np.float32)
    large = n_exact + (np.log(relf / np.float32(n_exact)) / np.float32(math.log(MAX_DISTANCE / n_exact))
                       * np.float32(N_BUCKETS - n_exact)).astype(np.int32)
    large = np.minimum(large, N_BUCKETS - 1)
    bucket = np.where(rel < n_exact, rel, large)
    in_window = (i + CHUNK - j >= 0) & (i + CHUNK - j < CHUNK)
    return bucket.astype(np.int32), in_window


def _split3(x):
    hi = x.astype(BF16)
    r1 = x - hi.astype(F32)
    mid = r1.astype(BF16)
    lo = (r1 - mid.astype(F32)).astype(BF16)
    return hi, mid, lo


HBM_SPEC = pl.BlockSpec(memory_space=pltpu.HBM)


def _mesh_pos():
    return lax.axis_index("x"), lax.axis_index("y"), lax.axis_index("c")


def _dev_index(px, py, pc):
    return 4 * px + 2 * py + pc


def _all_gather(shards, name):
    n = len(shards)

    def body(*refs):
        ins, outs = refs[:n], refs[n:2 * n]
        send_sems, recv_sems, local_sems = refs[2 * n:]
        x, y, c = _mesh_pos()
        me, sibling = (x, y, c), (x, y, 1 - c)
        chips = [(1 - x, y), (x, 1 - y), (1 - x, 1 - y)]

        def copy(a, k, block, to, src=None):
            dst = outs[a].at[_dev_index(*block)]
            return pltpu.make_async_remote_copy(
                src_ref=dst if src is None else src, dst_ref=dst,
                send_sem=send_sems.at[a * 7 + k], recv_sem=recv_sems.at[a * 7 + k],
                device_id=to, device_id_type=MESH)

        mine = [pltpu.make_async_copy(ins[a], outs[a].at[_dev_index(*me)], local_sems.at[a]) for a in range(n)]
        first = []
        for a in range(n):
            for j, chip in enumerate(chips):
                first.append(copy(a, 1 + j, me, (*chip, c), src=ins[a]))
            first.append(copy(a, 0, me, sibling, src=ins[a]))
        for cp in first:
            cp.start()
        for cp in mine:
            cp.start()
        passed = []
        for a in range(n):
            for j, chip in enumerate(chips):
                copy(a, 1 + j, (*chip, c), me).wait_recv()
                fwd = copy(a, 4 + j, (*chip, c), sibling)
                fwd.start()
                passed.append(fwd)
        for a in range(n):
            copy(a, 0, sibling, me).wait_recv()
            for j, chip in enumerate(chips):
                copy(a, 4 + j, (*chip, 1 - c), me).wait_recv()
        for cp in first + passed:
            cp.wait_send()
        for cp in mine:
            cp.wait()

    return _CHAIN.call(
        body, name=name,
        out_shape=[SDS((N_DEV,) + s.shape, s.dtype) for s in shards],
        in_specs=[HBM_SPEC] * n, out_specs=[HBM_SPEC] * n,
        scratch_shapes=[pltpu.SemaphoreType.DMA((7 * n,)), pltpu.SemaphoreType.DMA((7 * n,)),
                        pltpu.SemaphoreType.DMA((n,))],
    )(*shards)


SEM_SPEC = pl.BlockSpec(memory_space=pltpu.SEMAPHORE)
ANY_SPEC = pl.BlockSpec(memory_space=pl.ANY)
VMEM_SPEC = pl.BlockSpec(memory_space=pltpu.VMEM)
TOKEN_SPEC = VMEM_SPEC
TOKEN = SDS((8, LANE), F32)
SIDE_EFFECT = pltpu.SideEffectType.DATAFLOW_SIDE_EFFECTING


def _hbm(x):
    return pltpu.with_memory_space_constraint(x, pltpu.HBM)


class _CallChain:
    def __init__(self):
        self.token = None

    def call(self, body, *, in_specs, out_specs, out_shape, **kwargs):
        dep, n_in = self.token, len(in_specs)
        single = not isinstance(out_shape, (list, tuple))
        out_shapes = [out_shape] if single else list(out_shape)
        out_specs = [out_specs] if single else list(out_specs)
        n_out = len(out_shapes)
        n_dep = 0 if dep is None else 1
        token_spec = pl.BlockSpec((8, LANE), lambda *_: (0, 0)) if kwargs.get("grid") else VMEM_SPEC

        def chained(*refs):
            outs_at = n_in + n_dep
            body(*refs[:n_in], *refs[outs_at:outs_at + n_out], *refs[outs_at + n_out + 1:])
            token = refs[outs_at + n_out]
            token[...] = jnp.zeros_like(token)

        inner = pl.pallas_call(chained, in_specs=list(in_specs) + [ANY_SPEC] * n_dep, out_specs=out_specs + [token_spec],
                               out_shape=out_shapes + [TOKEN], **kwargs)

        def run(*operands):
            outs = inner(*operands) if dep is None else inner(*operands, dep)
            self.token = outs[n_out]
            return outs[0] if single else list(outs[:n_out])

        return run


_CHAIN = _CallChain()


def _split_start(bufs, copies_of, n_sems, name):
    n = len(bufs)

    def body(*refs):
        ins = refs[:n]
        send_sems, recv_sems = refs[n], refs[n + 1]
        for src, dst, k, target in copies_of(ins):
            pltpu.make_async_remote_copy(src_ref=src, dst_ref=dst, send_sem=send_sems.at[k], recv_sem=recv_sems.at[k],
                                         device_id=target, device_id_type=MESH).start()

    outs = _CHAIN.call(
        body, name=name,
        out_shape=[pltpu.SemaphoreType.DMA((n_sems,)), pltpu.SemaphoreType.DMA((n_sems,))]
        + [pltpu.HBM(b.shape, b.dtype) for b in bufs],
        in_specs=[HBM_SPEC] * n, out_specs=[SEM_SPEC, SEM_SPEC] + [HBM_SPEC] * n,
        input_output_aliases={a: 2 + a for a in range(n)},
        compiler_params=pltpu.CompilerParams(has_side_effects=SIDE_EFFECT),
    )(*[_hbm(b) for b in bufs])
    return outs[0], outs[1], list(outs[2:2 + n])


def _split_wait(bufs, sem_sets, waits_of, name):
    n, ns = len(bufs), len(sem_sets)
    flat_sems = [s for pair in sem_sets for s in pair]

    def body(*refs):
        ins = refs[:n]
        sems = refs[n:n + 2 * ns]
        x, y, c = _mesh_pos()
        for kind, src, dst, send_sem, recv_sem in waits_of(ins, [(sems[2 * i], sems[2 * i + 1]) for i in range(ns)]):
            cp = pltpu.make_async_remote_copy(src_ref=src, dst_ref=dst, send_sem=send_sem, recv_sem=recv_sem,
                                              device_id=(x, y, c), device_id_type=MESH)
            if kind == "send":
                cp.wait_send()
            else:
                cp.wait_recv()

    outs = _CHAIN.call(
        body, name=name,
        out_shape=[pltpu.HBM(b.shape, b.dtype) for b in bufs],
        in_specs=[HBM_SPEC] * n + [SEM_SPEC] * (2 * ns), out_specs=[HBM_SPEC] * n,
        input_output_aliases={a: a for a in range(n)},
        compiler_params=pltpu.CompilerParams(has_side_effects=SIDE_EFFECT),
    )(*bufs, *flat_sems)
    return list(outs)


def _gather_begin(shards, name):
    me = _dev_index(*_mesh_pos())
    lands = [lax.dynamic_update_index_in_dim(lax.empty((N_DEV,) + s.shape, s.dtype), s, me, 0) for s in shards]

    def copies_of(ins):
        x, y, c = _mesh_pos()
        targets = [(x, y, 1 - c), (1 - x, y, c), (x, 1 - y, c), (1 - x, 1 - y, c)]
        out = []
        for a, land in enumerate(ins):
            blk = land.at[_dev_index(x, y, c)]
            for k in (1, 2, 3, 0):
                out.append((blk, blk, 4 * a + k, targets[k]))
        return out

    send_sems, recv_sems, lands = _split_start(lands, copies_of, 4 * len(shards), name)
    return dict(lands=lands, sems=(send_sems, recv_sems), fwd={})


def _gather_pass_on(state, which, name):
    def arrivals(ins, sems):
        x, y, c = _mesh_pos()
        chips = [(1 - x, y), (x, 1 - y), (1 - x, 1 - y)]
        out = []
        for i, a in enumerate(which):
            for j, (px, py) in enumerate(chips):
                blk = ins[i].at[_dev_index(px, py, c)]
                out.append(("recv", blk, blk, sems[0][0].at[4 * a + 1 + j], sems[0][1].at[4 * a + 1 + j]))
        return out

    bufs = _split_wait([state["lands"][a] for a in which], [state["sems"]], arrivals, name + "_arrived")

    def copies_of(ins):
        x, y, c = _mesh_pos()
        chips = [(1 - x, y), (x, 1 - y), (1 - x, 1 - y)]
        out = []
        for i in range(len(which)):
            for j, (px, py) in enumerate(chips):
                blk = ins[i].at[_dev_index(px, py, c)]
                out.append((blk, blk, 3 * i + j, (x, y, 1 - c)))
        return out

    send_sems, recv_sems, bufs = _split_start(bufs, copies_of, 3 * len(which), name)
    for i, a in enumerate(which):
        state["lands"][a] = bufs[i]
    state["fwd"][tuple(which)] = (send_sems, recv_sems)


def _gather_end(state, which, name):
    def waits(ins, sems):
        x, y, c = _mesh_pos()
        chips = [(1 - x, y), (x, 1 - y), (1 - x, 1 - y)]
        (s_send, s_recv), (f_send, f_recv) = sems
        out = []
        for i, a in enumerate(which):
            mine = ins[i].at[_dev_index(x, y, c)]
            sib = ins[i].at[_dev_index(x, y, 1 - c)]
            out.append(("recv", sib, sib, s_send.at[4 * a], s_recv.at[4 * a]))
            for j, (px, py) in enumerate(chips):
                theirs = ins[i].at[_dev_index(px, py, 1 - c)]
                out.append(("recv", theirs, theirs, f_send.at[3 * i + j], f_recv.at[3 * i + j]))
            for k in range(4):
                out.append(("send", mine, mine, s_send.at[4 * a + k], s_recv.at[4 * a + k]))
            for j, (px, py) in enumerate(chips):
                passed = ins[i].at[_dev_index(px, py, c)]
                out.append(("send", passed, passed, f_send.at[3 * i + j], f_recv.at[3 * i + j]))
        return out

    bufs = _split_wait([state["lands"][a] for a in which], [state["sems"], state["fwd"][tuple(which)]], waits, name)
    for i, a in enumerate(which):
        state["lands"][a] = bufs[i]
    return bufs


def _sibling_exchange_begin(part, name):
    land = lax.empty((4,) + part.shape[1:], part.dtype)

    def copies_of(ins):
        x, y, c = _mesh_pos()
        return [(ins[0].at[2 * j + (1 - c)], ins[1].at[j], j, (x, y, 1 - c)) for j in range(4)]

    send_sems, recv_sems, bufs = _split_start([part, land], copies_of, 4, name)
    return dict(bufs=bufs, sems=(send_sems, recv_sems))


def _sibling_exchange_end(state, name):
    def waits(ins, sems):
        _, _, c = _mesh_pos()
        out = []
        for j in range(4):
            for kind in ("send", "recv"):
                out.append((kind, ins[0].at[2 * j + (1 - c)], ins[1].at[j], sems[0][0].at[j], sems[0][1].at[j]))
        return out

    return _split_wait(state["bufs"], [state["sems"]], waits, name)


CHIP_FLIPS = (2, 1, 3)


def _chip_exchange_begin(csum, name):
    land = lax.empty((3,) + csum.shape[1:], csum.dtype)

    def copies_of(ins):
        x, y, c = _mesh_pos()
        chips = [(1 - x, y), (x, 1 - y), (1 - x, 1 - y)]
        return [(ins[0].at[CHIP_FLIPS[r]], ins[1].at[r], r, (px, py, c)) for r, (px, py) in enumerate(chips)]

    send_sems, recv_sems, bufs = _split_start([csum, land], copies_of, 3, name)
    return dict(bufs=bufs, sems=(send_sems, recv_sems))


def _chip_exchange_end(state, name):
    def waits(ins, sems):
        out = []
        for r in range(3):
            for kind in ("send", "recv"):
                out.append((kind, ins[0].at[CHIP_FLIPS[r]], ins[1].at[r], sems[0][0].at[r], sems[0][1].at[r]))
        return out

    return _split_wait(state["bufs"], [state["sems"]], waits, name)


def _chip_sum(part, recv, name):
    _, R, C = part.shape
    tr = _tile(R, 512, 16)
    place = jnp.stack([lax.axis_index("c"), 2 * lax.axis_index("x") + lax.axis_index("y")]).astype(jnp.int32)

    def body(place_ref, p_ref, r_ref, o_ref):
        o_ref[...] = (p_ref[...].astype(F32) + r_ref[...].astype(F32)).astype(o_ref.dtype)

    def chip(p, place_ref):
        return jnp.bitwise_xor(p, place_ref[1])

    grid_spec = pltpu.PrefetchScalarGridSpec(
        num_scalar_prefetch=1, grid=(4, R // tr),
        in_specs=[pl.BlockSpec((None, tr, C), lambda p, i, place_ref: (2 * chip(p, place_ref) + place_ref[0], i, 0)),
                  pl.BlockSpec((None, tr, C), lambda p, i, place_ref: (chip(p, place_ref), i, 0))],
        out_specs=pl.BlockSpec((None, tr, C), lambda p, i, place_ref: (p, i, 0)))
    return pl.pallas_call(body, name=name, grid_spec=grid_spec, out_shape=SDS((4, R, C), part.dtype),
                          compiler_params=_params(2))(place, part, recv)


def _bias_fwd(table_t, onehot_t, onehot_kq_t):
    H = table_t.shape[0]
    n = onehot_t.shape[1]

    def body(t_ref, oh_ref, oh_kq_ref, o_ref, o_kq_ref):
        hi, mid, lo = _split3(t_ref[...])
        for src, dst in ((oh_ref, o_ref), (oh_kq_ref, o_kq_ref)):
            oh = src[...]
            dst[...] = _dot(hi, oh, NN) + _dot(mid, oh, NN) + _dot(lo, oh, NN)

    return _CHAIN.call(body, name="bias_fwd", in_specs=[VMEM_SPEC] * 3, out_specs=[VMEM_SPEC] * 2,
                       out_shape=[SDS((H, n), F32)] * 2, compiler_params=_params(0))(table_t, onehot_t, onehot_kq_t)


def _inproj_fwd(x, g, w):
    T, D = x.shape
    P = w.shape[1]
    tm = _tile(T, 512)

    def body(x_ref, g_ref, w_ref, proj_ref, n_ref):
        xv = x_ref[...]
        n = (xv * _rms_stats(xv) * g_ref[...]).astype(BF16)
        n_ref[...] = n
        proj_ref[...] = _dot(n, w_ref[...], NN)

    return _CHAIN.call(
        body, name="inproj_fwd", grid=(T // tm,),
        in_specs=[pl.BlockSpec((tm, D), lambda i: (i, 0)), pl.BlockSpec((1, D), lambda i: (0, 0)), _resident((D, P))],
        out_specs=[pl.BlockSpec((tm, P), lambda i: (i, 0)), pl.BlockSpec((tm, D), lambda i: (i, 0))],
        out_shape=[SDS((T, P), F32), SDS((T, D), BF16)], compiler_params=_params(1))(x, g, w)


def _layer_norm_group(vg, lg, lb):
    mu = jnp.mean(vg, axis=-1, keepdims=True)
    xc = vg - mu
    rstd = lax.rsqrt(jnp.mean(xc * xc, axis=-1, keepdims=True) + EPS)
    vhat = xc * rstd
    return vhat, rstd, vhat * lg + lb


def _gmlp_fwd(proj, lg, lb, w_s, bs_t, A):
    T = proj.shape[0]
    G = A // GROUP_DIM
    tm = _tile(T, 512)
    nc = tm // CHUNK

    def body(u_ref, v_ref, lg_ref, lb_ref, w_ref, bst_ref, a_ref):
        row = lax.broadcasted_iota(jnp.int32, (CHUNK, CHUNK), 0)
        col = lax.broadcasted_iota(jnp.int32, (CHUNK, CHUNK), 1)
        causal = row >= col
        for g in range(G):
            sl = slice(g * GROUP_DIM, (g + 1) * GROUP_DIM)
            _, _, vn = _layer_norm_group(_gelu(v_ref[:, sl]), lg_ref[:, sl], lb_ref[:, sl])
            vnb = vn.astype(BF16)
            wm = jnp.where(causal, w_ref[g], 0.0).astype(BF16)
            ug = _gelu(u_ref[:, sl])
            bcol = bst_ref[:, g:g + 1]
            for c in range(nc):
                rs = slice(c * CHUNK, (c + 1) * CHUNK)
                a_ref[rs, sl] = ug[rs] * (_dot(wm, vnb[rs], NN) + bcol)

    return _CHAIN.call(
        body, name="gmlp_fwd", grid=(T // tm,),
        in_specs=[pl.BlockSpec((tm, A), lambda i: (i, 0)), pl.BlockSpec((tm, A), lambda i: (i, 1)),
                  pl.BlockSpec((1, A), lambda i: (0, 0)), pl.BlockSpec((1, A), lambda i: (0, 0)),
                  pl.BlockSpec((G, CHUNK, CHUNK), lambda i: (0, 0, 0)), pl.BlockSpec((CHUNK, G), lambda i: (0, 0))],
        out_specs=pl.BlockSpec((tm, A), lambda i: (i, 0)),
        out_shape=SDS((T, A), F32), compiler_params=_params(1))(proj, proj, lg, lb, w_s, bs_t)


def _attn_masks(first_tile):
    ii = lax.broadcasted_iota(jnp.int32, (CHUNK, 2 * CHUNK), 0)
    jj = lax.broadcasted_iota(jnp.int32, (CHUNK, 2 * CHUNK), 1)
    in_window = (jj > ii) & (jj <= ii + CHUNK)
    first_mask = in_window & jnp.logical_or(jnp.logical_not(first_tile), jj >= CHUNK)
    return in_window, first_mask


def _softmax_with_sink(s, sink, axis):
    m = jnp.maximum(jnp.max(s, axis=axis, keepdims=True), sink)
    p = jnp.exp(s - m)
    e_sink = jnp.exp(sink - m)
    inv = 1.0 / (jnp.sum(p, axis=axis, keepdims=True) + e_sink)
    return p * inv, e_sink * inv


def _pad_heads(band, group):
    lane = lax.broadcasted_iota(jnp.int32, band.shape, 1)
    if group == 0:
        low = jnp.where(lane < HEAD_DIM, band, 0.0)
        high = pltpu.roll(low, HEAD_DIM, 1)
    else:
        high = jnp.where(lane >= HEAD_DIM, band, 0.0)
        low = pltpu.roll(high, HEAD_DIM, 1)
    return low.astype(BF16), high.astype(BF16)


def _attn_specs(tq, A, B, reverse_tiles=None):
    nb = tq // CHUNK
    kcol = (2 * A + B) // LANE
    if reverse_tiles is None:
        tile = lambda i: i
    else:
        tile = lambda i: reverse_tiles - 1 - i
    prev = lambda i: jnp.maximum(tile(i) * nb - 1, 0)
    return [pl.BlockSpec((tq, B), lambda i: (tile(i), 2 * A // B)),
            pl.BlockSpec((tq, LANE), lambda i: (tile(i), kcol)),
            pl.BlockSpec((tq, LANE), lambda i: (tile(i), kcol + 1)),
            pl.BlockSpec((CHUNK, LANE), lambda i: (prev(i), kcol)),
            pl.BlockSpec((CHUNK, LANE), lambda i: (prev(i), kcol + 1))]


def _attn_fwd(proj, bias, sinks, A, B):
    T = proj.shape[0]
    H = B // HEAD_DIM
    qpk = H // KV_HEADS
    tq = _tile(T, 512)
    nb = tq // CHUNK

    scale = HEAD_DIM ** -0.5

    def body(sink_ref, q_ref, k_ref, v_ref, kp_ref, vp_ref, bias_ref, o_ref):
        in_window, first_mask = _attn_masks(pl.program_id(0) == 0)
        for b in range(nb):
            rows = slice(b * CHUNK, (b + 1) * CHUNK)
            if b == 0:
                kprev, vprev, mask = kp_ref[...], vp_ref[...], first_mask
            else:
                prows = slice((b - 1) * CHUNK, b * CHUNK)
                kprev, vprev, mask = k_ref[prows, :], v_ref[prows, :], in_window
            kband = jnp.concatenate([kprev, k_ref[rows, :]], axis=0)
            vband = jnp.concatenate([vprev, v_ref[rows, :]], axis=0)
            outs = []
            for g in range(KV_HEADS):
                k_low, k_high = _pad_heads(kband, g)
                v_both = jnp.concatenate(_pad_heads(vband, g), axis=0)
                for pair in range(qpk // 2):
                    h = g * qpk + 2 * pair
                    qs = (q_ref[rows, h * HEAD_DIM:(h + 2) * HEAD_DIM] * scale).astype(BF16)
                    probs = []
                    for head, kz in ((h, k_low), (h + 1, k_high)):
                        s = jnp.where(mask, _dot(qs, kz, NT) + bias_ref[head], NEG)
                        probs.append(_softmax_with_sink(s, sink_ref[head], -1)[0])
                    outs.append(_dot(jnp.concatenate(probs, axis=1).astype(BF16), v_both, NN))
            o_ref[rows, :] = jnp.concatenate(outs, axis=1)

    return _CHAIN.call(
        body, name="attn_fwd", grid=(T // tq,),
        in_specs=[pl.BlockSpec(memory_space=pltpu.SMEM)] + _attn_specs(tq, A, B)
        + [pl.BlockSpec((H, CHUNK, 2 * CHUNK), lambda i: (0, 0, 0))],
        out_specs=pl.BlockSpec((tq, B), lambda i: (i, 0)),
        out_shape=SDS((T, B), F32), compiler_params=_params(1))(sinks, proj, proj, proj, proj, proj, bias)


def _outproj_fwd(a, b, ga, gb, x, w):
    T, A = a.shape
    B = b.shape[1]
    D = x.shape[1]
    tm = _tile(T, 512)

    def body(a_ref, b_ref, ga_ref, gb_ref, x_ref, w_ref, h_ref, mix_ref):
        av, bv = a_ref[...], b_ref[...]
        mix_ref[:, :A] = (av * _rms_stats(av) * ga_ref[...]).astype(BF16)
        mix_ref[:, A:] = (bv * _rms_stats(bv) * gb_ref[...]).astype(BF16)
        h_ref[...] = x_ref[...] + _dot(mix_ref[...], w_ref[...], NN)

    return _CHAIN.call(
        body, name="outproj_fwd", grid=(T // tm,),
        in_specs=[pl.BlockSpec((tm, A), lambda i: (i, 0)), pl.BlockSpec((tm, B), lambda i: (i, 0)),
                  pl.BlockSpec((1, A), lambda i: (0, 0)), pl.BlockSpec((1, B), lambda i: (0, 0)),
                  pl.BlockSpec((tm, D), lambda i: (i, 0)), _resident((A + B, D))],
        out_specs=[pl.BlockSpec((tm, D), lambda i: (i, 0)), pl.BlockSpec((tm, A + B), lambda i: (i, 0))],
        out_shape=[SDS((T, D), F32), SDS((T, A + B), BF16)], compiler_params=_params(1))(a, b, ga, gb, x, w)


def _ffn_up(h1, g, w_up):
    T, D = h1.shape
    Fb = w_up.shape[2]
    F = N_DEV * Fb
    tm, tf = _tile(T, 1024), _tile(Fb, 1024)
    per = Fb // tf

    def body(h_ref, g_ref, wu_ref, z_ref, n_ref, nbuf):
        @pl.when(pl.program_id(1) == 0)
        def _():
            hv = h_ref[...]
            n = (hv * _rms_stats(hv) * g_ref[...]).astype(BF16)
            nbuf[...] = n
            n_ref[...] = n

        z_ref[...] = jnp.maximum(_dot(nbuf[...], wu_ref[...], NN), 0.0).astype(BF16)

    return _CHAIN.call(
        body, name="ffn_up", grid=(T // tm, F // tf),
        in_specs=[pl.BlockSpec((tm, D), lambda i, j: (i, 0)), pl.BlockSpec((1, D), lambda i, j: (0, 0)),
                  pl.BlockSpec((None, D, tf), lambda i, j: (j // per, 0, j % per))],
        out_specs=[pl.BlockSpec((tm, tf), lambda i, j: (i, j)), pl.BlockSpec((tm, D), lambda i, j: (i, 0))],
        out_shape=[SDS((T, F), BF16), SDS((T, D), BF16)],
        scratch_shapes=[pltpu.VMEM((tm, D), BF16)], compiler_params=_params(2))(h1, g, w_up)


def _ffn_down(h1, z, w_down):
    T, D = h1.shape
    F = w_down.shape[0]
    tm, tn, tk = _tile(T, 1024), _tile(D, 1024), _tile(F, 4096)

    def body(h_ref, z_ref, wd_ref, h2_ref):
        k = pl.program_id(2)

        @pl.when(k == 0)
        def _():
            h2_ref[...] = h_ref[...]

        zf = z_ref[...].astype(F32)
        h2_ref[...] += _dot((zf * zf).astype(BF16), wd_ref[...], NN)

    return _CHAIN.call(
        body, name="ffn_down", grid=(T // tm, D // tn, F // tk),
        in_specs=[pl.BlockSpec((tm, tn), lambda i, j, k: (i, j)), pl.BlockSpec((tm, tk), lambda i, j, k: (i, k)),
                  pl.BlockSpec((tk, tn), lambda i, j, k: (k, j))],
        out_specs=pl.BlockSpec((tm, tn), lambda i, j, k: (i, j)),
        out_shape=SDS((T, D), F32), compiler_params=_params(3))(h1, z, w_down)


def _final_loss(h2, g, target):
    T, D = h2.shape
    tm = _tile(T, 512)

    def body(h_ref, g_ref, t_ref, loss_ref, dg_ref, dh_ref, dhb_ref):
        @pl.when(pl.program_id(0) == 0)
        def _():
            loss_ref[...] = jnp.zeros_like(loss_ref)
            dg_ref[...] = jnp.zeros_like(dg_ref)

        gv = g_ref[...]

        def chunk(rows, carry):
            loss, dg = carry
            hv = h_ref[rows, :]
            r = _rms_stats(hv)
            hn = hv * r
            e = hn * gv - t_ref[rows, :]
            dy = e * (1.0 / D)
            dh = _rms_bwd(dy, hv, r, gv)
            dh_ref[rows, :] = dh
            dhb_ref[rows, :] = dh.astype(BF16)
            return loss + _fold_rows(e * e), dg + _fold_rows(dy * hn)

        zero = jnp.zeros((8, D), F32)
        sq, dg = _for_row_chunks(tm, chunk, (zero, zero))
        loss_ref[...] += (0.5 / D) * jnp.sum(jnp.sum(sq, axis=0, keepdims=True), axis=-1, keepdims=True)
        dg_ref[...] += jnp.sum(dg, axis=0, keepdims=True)

    return _CHAIN.call(
        body, name="final_loss", grid=(T // tm,),
        in_specs=[pl.BlockSpec((tm, D), lambda i: (i, 0)), pl.BlockSpec((1, D), lambda i: (0, 0)),
                  pl.BlockSpec((tm, D), lambda i: (i, 0))],
        out_specs=[pl.BlockSpec((1, 1), lambda i: (0, 0)), pl.BlockSpec((1, D), lambda i: (0, 0)),
                   pl.BlockSpec((tm, D), lambda i: (i, 0)), pl.BlockSpec((tm, D), lambda i: (i, 0))],
        out_shape=[SDS((1, 1), F32), SDS((1, D), F32), SDS((T, D), F32), SDS((T, D), BF16)],
        compiler_params=_params(1))(h2, g, target)


def _ffn_down_bwd(dh2b, z, w_down):
    T, D = dh2b.shape
    F = w_down.shape[0]
    tm, tf = _tile(T, 1024), _tile(F, 1024)

    def body(dh_ref, z_ref, wd_ref, dzp_ref):
        dzz = _dot(dh_ref[...], wd_ref[...], NT)
        dzp_ref[...] = (dzz * (2.0 * z_ref[...].astype(F32))).astype(BF16)

    return _CHAIN.call(
        body, name="ffn_down_bwd", grid=(T // tm, F // tf),
        in_specs=[pl.BlockSpec((tm, D), lambda i, j: (i, 0)), pl.BlockSpec((tm, tf), lambda i, j: (i, j)),
                  pl.BlockSpec((tf, D), lambda i, j: (j, 0))],
        out_specs=pl.BlockSpec((tm, tf), lambda i, j: (i, j)),
        out_shape=SDS((T, F), BF16), compiler_params=_params(2))(dh2b, z, w_down)


def _ffn_up_bwd(dzp, dh2, h1, g, w_up):
    T, D = h1.shape
    Fb = w_up.shape[2]
    F = N_DEV * Fb
    tm, tk = _tile(T, 512), _tile(Fb, 1024)
    per = Fb // tk
    nk = F // tk

    def body(dzp_ref, dh_ref, h_ref, g_ref, wu_ref, dh1_ref, dh1b_ref, dg_ref, acc):
        i, k = pl.program_id(0), pl.program_id(1)

        @pl.when((i == 0) & (k == 0))
        def _():
            dg_ref[...] = jnp.zeros_like(dg_ref)

        part = _dot(dzp_ref[...], wu_ref[...], NT)

        @pl.when(k == 0)
        def _():
            acc[...] = part

        @pl.when(k > 0)
        def _():
            acc[...] += part

        @pl.when(k == nk - 1)
        def _():
            dg_ref[...] += _rms_bwd_by_rows(tm, acc, h_ref, g_ref, dh_ref, dh1_ref, dh1b_ref)

    return _CHAIN.call(
        body, name="ffn_up_bwd", grid=(T // tm, nk),
        in_specs=[pl.BlockSpec((tm, tk), lambda i, k: (i, k)), pl.BlockSpec((tm, D), lambda i, k: (i, 0)),
                  pl.BlockSpec((tm, D), lambda i, k: (i, 0)), pl.BlockSpec((1, D), lambda i, k: (0, 0)),
                  pl.BlockSpec((None, D, tk), lambda i, k: (k // per, 0, k % per))],
        out_specs=[pl.BlockSpec((tm, D), lambda i, k: (i, 0)), pl.BlockSpec((tm, D), lambda i, k: (i, 0)),
                   pl.BlockSpec((1, D), lambda i, k: (0, 0))],
        out_shape=[SDS((T, D), F32), SDS((T, D), BF16), SDS((1, D), F32)],
        scratch_shapes=[pltpu.VMEM((tm, D), F32)], compiler_params=_params(2))(dzp, dh2, h1, g, w_up)


def _matmul_tn(a, b, name, square_a=False, col_blocks=None):
    T, K = a.shape
    N = b.shape[1]
    tn = _tile(N if col_blocks is None else N // col_blocks, 1792)
    tk = _tile(K, 1024 if tn <= 1024 else 512)

    def body(a_ref, b_ref, o_ref):
        av = a_ref[...]
        if square_a:
            af = av.astype(F32)
            av = (af * af).astype(BF16)
        o_ref[...] = _dot(av, b_ref[...], TN).astype(o_ref.dtype)

    if col_blocks is None:
        out_shape = SDS((K, N), BF16)
        out_spec = pl.BlockSpec((tk, tn), lambda i, j: (i, j))
    else:
        per = (N // col_blocks) // tn
        out_shape = SDS((col_blocks, K, N // col_blocks), BF16)
        out_spec = pl.BlockSpec((None, tk, tn), lambda i, j: (j // per, i, j % per))
    return _CHAIN.call(
        body, name=name, grid=(K // tk, N // tn),
        in_specs=[pl.BlockSpec((T, tk), lambda i, j: (0, i)), pl.BlockSpec((T, tn), lambda i, j: (0, j))],
        out_specs=out_spec, out_shape=out_shape, compiler_params=_params(2))(a, b)


def _outproj_bwd(dh1b, w, a, b, ga, gb):
    T, D = dh1b.shape
    A, B = a.shape[1], b.shape[1]
    tm = _tile(T, 512)

    def body(dh_ref, w_ref, a_ref, b_ref, ga_ref, gb_ref, da_ref, db_ref, dga_ref, dgb_ref, dmix_a, dmix_b):
        @pl.when(pl.program_id(0) == 0)
        def _():
            dga_ref[...] = jnp.zeros_like(dga_ref)
            dgb_ref[...] = jnp.zeros_like(dgb_ref)

        dmix = _dot(dh_ref[...], w_ref[...], NT)
        dmix_a[...] = dmix[:, :A]
        dmix_b[...] = dmix[:, A:]
        dga_ref[...] += _rms_bwd_by_rows(tm, dmix_a, a_ref, ga_ref, None, da_ref, None)
        dgb_ref[...] += _rms_bwd_by_rows(tm, dmix_b, b_ref, gb_ref, None, db_ref, None)

    return _CHAIN.call(
        body, name="outproj_bwd", grid=(T // tm,),
        in_specs=[pl.BlockSpec((tm, D), lambda i: (i, 0)), _resident((A + B, D)),
                  pl.BlockSpec((tm, A), lambda i: (i, 0)), pl.BlockSpec((tm, B), lambda i: (i, 0)),
                  pl.BlockSpec((1, A), lambda i: (0, 0)), pl.BlockSpec((1, B), lambda i: (0, 0))],
        out_specs=[pl.BlockSpec((tm, A), lambda i: (i, 0)), pl.BlockSpec((tm, B), lambda i: (i, 0)),
                   pl.BlockSpec((1, A), lambda i: (0, 0)), pl.BlockSpec((1, B), lambda i: (0, 0))],
        out_shape=[SDS((T, A), F32), SDS((T, B), F32), SDS((1, A), F32), SDS((1, B), F32)],
        scratch_shapes=[pltpu.VMEM((tm, A), F32), pltpu.VMEM((tm, B), F32)],
        compiler_params=_params(1))(dh1b, w, a, b, ga, gb)


def _gmlp_bwd(proj, da, lg, lb, w_s, w_st, bs_t, A):
    T = proj.shape[0]
    G = A // GROUP_DIM
    tm = _tile(T, 512)
    nc = tm // CHUNK

    def body(u_ref, v_ref, da_ref, lg_ref, lb_ref, w_ref, wt_ref, bst_ref, duv_ref, dlg_ref, dlb_ref, dw_ref, dbs_ref):
        @pl.when(pl.program_id(0) == 0)
        def _():
            dlg_ref[...] = jnp.zeros_like(dlg_ref)
            dlb_ref[...] = jnp.zeros_like(dlb_ref)
            dw_ref[...] = jnp.zeros_like(dw_ref)
            dbs_ref[...] = jnp.zeros_like(dbs_ref)

        row = lax.broadcasted_iota(jnp.int32, (CHUNK, CHUNK), 0)
        col = lax.broadcasted_iota(jnp.int32, (CHUNK, CHUNK), 1)
        lower = row >= col
        upper = row <= col
        for g in range(G):
            sl = slice(g * GROUP_DIM, (g + 1) * GROUP_DIM)
            lgv = lg_ref[:, sl]
            vg, vg_grad = _gelu_and_grad(v_ref[:, sl])
            vhat, rstd, vn = _layer_norm_group(vg, lgv, lb_ref[:, sl])
            vnb = vn.astype(BF16)
            ug, ug_grad = _gelu_and_grad(u_ref[:, sl])
            dav = da_ref[:, sl]
            wm = jnp.where(lower, w_ref[g], 0.0).astype(BF16)
            wmt = jnp.where(upper, wt_ref[g], 0.0).astype(BF16)
            bcol = bst_ref[:, g:g + 1]
            dw_acc = jnp.zeros((CHUNK, CHUNK), F32)
            dbs_acc = jnp.zeros((CHUNK, 1), F32)
            dvn_parts = []
            dug_parts = []
            for c in range(nc):
                rs = slice(c * CHUNK, (c + 1) * CHUNK)
                mixed = _dot(wm, vnb[rs], NN) + bcol
                dug_parts.append(dav[rs] * mixed)
                dmix = dav[rs] * ug[rs]
                dbs_acc = dbs_acc + jnp.sum(dmix, axis=-1, keepdims=True)
                dmixb = dmix.astype(BF16)
                dw_acc = dw_acc + _dot(dmixb, vnb[rs], NT)
                dvn_parts.append(_dot(wmt, dmixb, NN))
            dvn = jnp.concatenate(dvn_parts, axis=0)
            dug = jnp.concatenate(dug_parts, axis=0)
            dw_ref[g] += jnp.where(lower, dw_acc, 0.0)
            dbs_ref[:, g:g + 1] += dbs_acc
            dlg_ref[:, sl] += jnp.sum(dvn * vhat, axis=0, keepdims=True)
            dlb_ref[:, sl] += jnp.sum(dvn, axis=0, keepdims=True)
            dvhat = dvn * lgv
            dvg = rstd * (dvhat - jnp.mean(dvhat, axis=-1, keepdims=True)
                          - vhat * jnp.mean(dvhat * vhat, axis=-1, keepdims=True))
            duv_ref[:, sl] = (dug * ug_grad).astype(BF16)
            duv_ref[:, A + g * GROUP_DIM:A + (g + 1) * GROUP_DIM] = (dvg * vg_grad).astype(BF16)

    return _CHAIN.call(
        body, name="gmlp_bwd", grid=(T // tm,),
        in_specs=[pl.BlockSpec((tm, A), lambda i: (i, 0)), pl.BlockSpec((tm, A), lambda i: (i, 1)),
                  pl.BlockSpec((tm, A), lambda i: (i, 0)),
                  pl.BlockSpec((1, A), lambda i: (0, 0)), pl.BlockSpec((1, A), lambda i: (0, 0)),
                  pl.BlockSpec((G, CHUNK, CHUNK), lambda i: (0, 0, 0)),
                  pl.BlockSpec((G, CHUNK, CHUNK), lambda i: (0, 0, 0)), pl.BlockSpec((CHUNK, G), lambda i: (0, 0))],
        out_specs=[pl.BlockSpec((tm, 2 * A), lambda i: (i, 0)),
                   pl.BlockSpec((1, A), lambda i: (0, 0)), pl.BlockSpec((1, A), lambda i: (0, 0)),
                   pl.BlockSpec((G, CHUNK, CHUNK), lambda i: (0, 0, 0)), pl.BlockSpec((CHUNK, G), lambda i: (0, 0))],
        out_shape=[SDS((T, 2 * A), BF16), SDS((1, A), F32), SDS((1, A), F32),
                   SDS((G, CHUNK, CHUNK), F32), SDS((CHUNK, G), F32)],
        compiler_params=_params(1))(proj, proj, da, lg, lb, w_s, w_st, bs_t)


def _attn_bwd(proj, do, duv, bias_t, sinks, A, B):
    T, P = proj.shape
    H = B // HEAD_DIM
    qpk = H // KV_HEADS
    tq = _tile(T, 512)
    nb = tq // CHUNK
    n_tiles = T // tq
    scale = HEAD_DIM ** -0.5
    rev = lambda i: n_tiles - 1 - i

    def body(sink_ref, q_ref, k_ref, v_ref, kp_ref, vp_ref, do_ref, duv_ref, bias_ref,
             dproj_ref, dbias_ref, dsink_ref, carry, dkv, sacc):
        step = pl.program_id(0)

        @pl.when(step == 0)
        def _():
            carry[...] = jnp.zeros_like(carry)
            sacc[...] = jnp.zeros_like(sacc)
            dbias_ref[...] = jnp.zeros_like(dbias_ref)

        jj = lax.broadcasted_iota(jnp.int32, (2 * CHUNK, CHUNK), 0)
        ii = lax.broadcasted_iota(jnp.int32, (2 * CHUNK, CHUNK), 1)
        in_window = (jj > ii) & (jj <= ii + CHUNK)
        first_mask = in_window & jnp.logical_or(step != n_tiles - 1, jj >= CHUNK)
        low_query = lax.broadcasted_iota(jnp.int32, (CHUNK, LANE), 1) < HEAD_DIM
        low_key = lax.broadcasted_iota(jnp.int32, (2 * CHUNK, LANE), 1) < HEAD_DIM

        def split_pair(pair_bf16):
            zero = jnp.zeros_like(pair_bf16)
            return jnp.concatenate([jnp.where(low_query, pair_bf16, zero), jnp.where(low_query, zero, pair_bf16)], axis=0)

        dproj_ref[:, :2 * A] = duv_ref[...]
        dkv[...] = jnp.zeros_like(dkv)
        for b in range(nb):
            rows = slice(b * CHUNK, (b + 1) * CHUNK)
            band = slice(b * CHUNK, (b + 2) * CHUNK)
            if b == 0:
                kprev, vprev, mask = kp_ref[...], vp_ref[...], first_mask
            else:
                prows = slice((b - 1) * CHUNK, b * CHUNK)
                kprev, vprev, mask = k_ref[prows, :], v_ref[prows, :], in_window
            kband = jnp.concatenate([kprev, k_ref[rows, :]], axis=0)
            vband = jnp.concatenate([vprev, v_ref[rows, :]], axis=0)
            dq_parts, dk_groups, dv_groups = [], [], []
            for g in range(KV_HEADS):
                k_low, k_high = _pad_heads(kband, g)
                v_low, v_high = _pad_heads(vband, g)
                k_both = jnp.concatenate([k_low, k_high], axis=0)
                dk_acc = jnp.zeros((2 * CHUNK, LANE), F32)
                dv_acc = jnp.zeros((2 * CHUNK, LANE), F32)
                for pair in range(qpk // 2):
                    h = g * qpk + 2 * pair
                    cols = slice(h * HEAD_DIM, (h + 2) * HEAD_DIM)
                    qs = (q_ref[rows, cols] * scale).astype(BF16)
                    dob = do_ref[rows, cols].astype(BF16)
                    probs, dscores = [], []
                    for head, kz, vz in ((h, k_low, v_low), (h + 1, k_high, v_high)):
                        st = jnp.where(mask, _dot(kz, qs, NT) + bias_ref[head], NEG)
                        pt, p_sink = _softmax_with_sink(st, sink_ref[head], 0)
                        dpt = _dot(vz, dob, NT)
                        delta = jnp.sum(pt * dpt, axis=0, keepdims=True)
                        dst = pt * (dpt - delta)
                        dbias_ref[head] += dst
                        sacc[head:head + 1, :] += -(p_sink * delta)
                        probs.append(pt)
                        dscores.append(dst)
                    dk_acc = dk_acc + _dot(jnp.concatenate(dscores, axis=1).astype(BF16), split_pair(qs), NN)
                    dv_acc = dv_acc + _dot(jnp.concatenate(probs, axis=1).astype(BF16), split_pair(dob), NN)
                    dq_parts.append(_dot(jnp.concatenate(dscores, axis=0).astype(BF16), k_both, TN) * scale)
                dk_groups.append(dk_acc + pltpu.roll(dk_acc, HEAD_DIM, 1))
                dv_groups.append(dv_acc + pltpu.roll(dv_acc, HEAD_DIM, 1))
            dkv[band, :LANE] += jnp.where(low_key, dk_groups[0], dk_groups[1])
            dkv[band, LANE:] += jnp.where(low_key, dv_groups[0], dv_groups[1])
            dproj_ref[rows, 2 * A:2 * A + B] = jnp.concatenate(dq_parts, axis=1).astype(BF16)
        last = slice(tq, tq + CHUNK)
        dkv[last, :] += carry[...]
        dproj_ref[:, 2 * A + B:] = dkv[CHUNK:, :].astype(BF16)
        carry[...] = dkv[:CHUNK, :]

        @pl.when(step == n_tiles - 1)
        def _():
            dsink_ref[...] = jnp.sum(sacc[...], axis=1, keepdims=True)

    specs = _attn_specs(tq, A, B, reverse_tiles=n_tiles)
    return _CHAIN.call(
        body, name="attn_bwd", grid=(n_tiles,),
        in_specs=[pl.BlockSpec(memory_space=pltpu.SMEM)] + specs
        + [pl.BlockSpec((tq, B), lambda i: (rev(i), 0)), pl.BlockSpec((tq, 2 * A), lambda i: (rev(i), 0)),
           pl.BlockSpec((H, 2 * CHUNK, CHUNK), lambda i: (0, 0, 0))],
        out_specs=[pl.BlockSpec((tq, P), lambda i: (rev(i), 0)),
                   pl.BlockSpec((H, 2 * CHUNK, CHUNK), lambda i: (0, 0, 0)), pl.BlockSpec((H, 1), lambda i: (0, 0))],
        out_shape=[SDS((T, P), BF16), SDS((H, 2 * CHUNK, CHUNK), F32), SDS((H, 1), F32)],
        scratch_shapes=[pltpu.VMEM((CHUNK, 2 * LANE), F32), pltpu.VMEM((tq + CHUNK, 2 * LANE), F32),
                        pltpu.VMEM((H, LANE), F32)],
        compiler_params=_params(1))(sinks, proj, proj, proj, proj, proj, do, duv, bias_t)


def _bias_bwd(dbias, onehot):
    H = dbias.shape[0]
    nbk = onehot.shape[1]

    def body(d_ref, oh_ref, o_ref):
        hi, mid, lo = _split3(d_ref[...])
        oh = oh_ref[...]
        o_ref[...] = _dot(hi, oh, NN) + _dot(mid, oh, NN) + _dot(lo, oh, NN)

    return _CHAIN.call(body, name="bias_bwd", in_specs=[VMEM_SPEC] * 2, out_specs=VMEM_SPEC, out_shape=SDS((H, nbk), F32),
                       compiler_params=_params(0))(dbias, onehot)


def _inproj_bwd(dproj, w, x, dh1, g):
    T, P = dproj.shape
    D = x.shape[1]
    tm = _tile(T, 512)

    def body(dp_ref, w_ref, x_ref, dh_ref, g_ref, dx_ref, dg_ref, dn):
        @pl.when(pl.program_id(0) == 0)
        def _():
            dg_ref[...] = jnp.zeros_like(dg_ref)

        dn[...] = _dot(dp_ref[...], w_ref[...], NT)
        dg_ref[...] += _rms_bwd_by_rows(tm, dn, x_ref, g_ref, dh_ref, dx_ref, None)

    return _CHAIN.call(
        body, name="inproj_bwd", grid=(T // tm,),
        in_specs=[pl.BlockSpec((tm, P), lambda i: (i, 0)), _resident((D, P)),
                  pl.BlockSpec((tm, D), lambda i: (i, 0)), pl.BlockSpec((tm, D), lambda i: (i, 0)),
                  pl.BlockSpec((1, D), lambda i: (0, 0))],
        out_specs=[pl.BlockSpec((tm, D), lambda i: (i, 0)), pl.BlockSpec((1, D), lambda i: (0, 0))],
        out_shape=[SDS((T, D), F32), SDS((1, D), F32)],
        scratch_shapes=[pltpu.VMEM((tm, D), F32)], compiler_params=_params(1))(dproj, w, x, dh1, g)


def _adamw(w, g, m, v):
    m = ADAM_B1 * m + (1.0 - ADAM_B1) * g
    v = ADAM_B2 * v + (1.0 - ADAM_B2) * (g * g)
    m_hat = m / (1.0 - ADAM_B1 ** ADAM_STEP)
    v_hat = v / (1.0 - ADAM_B2 ** ADAM_STEP)
    delta = -ADAM_LR * (m_hat / (jnp.sqrt(v_hat) + ADAM_EPS) + ADAM_WD * w)
    return delta, m, v


def _adam_sharded(csum, recv, w, m, v, name):
    R, C = w.shape
    tr = _tile(R, 256, 16)

    def body(own_ref, recv_ref, w_ref, m_ref, v_ref, g_ref, d_ref, nm_ref, nv_ref):
        g = own_ref[...].astype(F32)
        for r in range(3):
            g = g + recv_ref[r].astype(F32)
        delta, nm, nv = _adamw(w_ref[...], g, m_ref[...], v_ref[...])
        g_ref[...] = g
        d_ref[...] = delta
        nm_ref[...] = nm
        nv_ref[...] = nv

    blk = pl.BlockSpec((tr, C), lambda i: (i, 0))
    return _CHAIN.call(
        body, name=name, grid=(R // tr,),
        in_specs=[pl.BlockSpec((None, tr, C), lambda i: (0, i, 0)), pl.BlockSpec((3, tr, C), lambda i: (0, i, 0)),
                  blk, blk, blk],
        out_specs=[blk] * 4, out_shape=[SDS((R, C), F32)] * 4, compiler_params=_params(1))(csum, recv, w, m, v)


def _adam_small(gathered, w, m, v):
    R = w.shape[0]

    def body(p_ref, w_ref, m_ref, v_ref, g_ref, d_ref, nm_ref, nv_ref):
        g = p_ref[0]
        for d in range(1, N_DEV):
            g = g + p_ref[d]
        delta, nm, nv = _adamw(w_ref[...], g, m_ref[...], v_ref[...])
        g_ref[...] = g
        d_ref[...] = delta
        nm_ref[...] = nm
        nv_ref[...] = nv

    return _CHAIN.call(body, name="adam_small", in_specs=[VMEM_SPEC] * 4, out_specs=[VMEM_SPEC] * 4,
                       out_shape=[SDS((R, LANE), F32)] * 4,
                       compiler_params=_params(0))(gathered, w, m, v)


def _pack(arrays):
    tile = 8 * LANE
    pieces = []
    for a in arrays:
        flat = a.reshape(-1).astype(F32)
        pieces.append(jnp.pad(flat, (0, (-flat.size) % tile)))
    return jnp.concatenate(pieces).reshape(-1, LANE)


def _unpack(packed, shapes):
    tile = 8 * LANE
    flat = packed.reshape(-1)
    out, off = [], 0
    for s in shapes:
        size = int(np.prod(s))
        out.append(flat[off:off + size].reshape(s))
        off += size + (-size) % tile
    return out


def kernel(x, rel_bias_table, mix_norm_g, w_in, gate_norm_g, gate_norm_b, w_spatial, b_spatial, attn_sinks, out_norm_a_g, out_norm_b_g, w_out, ffn_norm_g, w_up, w_down, final_norm_g, loss_target, m_rel_bias_table, m_mix_norm_g, m_w_in, m_gate_norm_g, m_gate_norm_b, m_w_spatial, m_b_spatial, m_attn_sinks, m_out_norm_a_g, m_out_norm_b_g, m_w_out, m_ffn_norm_g, m_w_up, m_w_down, m_final_norm_g, v_rel_bias_table, v_mix_norm_g, v_w_in, v_gate_norm_g, v_gate_norm_b, v_w_spatial, v_b_spatial, v_attn_sinks, v_out_norm_a_g, v_out_norm_b_g, v_w_out, v_ffn_norm_g, v_w_up, v_w_down, v_final_norm_g):
    T, D = x.shape[1], x.shape[2]
    A = D // 2
    B = D // 2
    G = A // GROUP_DIM
    H = B // HEAD_DIM
    P = 2 * A + B + 2 * KV_HEADS * HEAD_DIM
    Pb = w_in.shape[2]
    xs = x.reshape(T, D)
    target = loss_target.reshape(T, D)

    shards = [w_in[0].astype(BF16), w_out[0].astype(BF16), w_up[0].astype(BF16), w_down[0].astype(BF16)]
    _CHAIN.token = None
    gather = _gather_begin(shards, "gather_start")
    _gather_pass_on(gather, [0], "gather_in_pass")
    (win_g,) = _gather_end(gather, [0], "gather_in_end")
    win_full = jnp.transpose(win_g, (1, 0, 2)).reshape(D, P)

    g1, g2, g3 = mix_norm_g.reshape(1, D), ffn_norm_g.reshape(1, D), final_norm_g.reshape(1, D)
    lg, lb = gate_norm_g.reshape(1, A), gate_norm_b.reshape(1, A)
    ws = w_spatial[0]
    ws_t = jnp.swapaxes(ws, 1, 2)
    bs_t = jnp.transpose(b_spatial[0])
    ga, gb = out_norm_a_g.reshape(1, A), out_norm_b_g.reshape(1, B)
    sinks = attn_sinks.reshape(H)
    bucket, in_window = _t5_bucket()
    onehot_np = ((bucket[:, :, None] == np.arange(N_BUCKETS)) & in_window[:, :, None]).astype(np.float32)
    onehot = jnp.asarray(onehot_np.reshape(-1, N_BUCKETS)).astype(BF16)
    onehot_kq = jnp.asarray(onehot_np.transpose(1, 0, 2).reshape(-1, N_BUCKETS)).astype(BF16)

    bias, bias_t = _bias_fwd(jnp.transpose(rel_bias_table), jnp.transpose(onehot), jnp.transpose(onehot_kq))
    bias, bias_t = bias.reshape(H, CHUNK, 2 * CHUNK), bias_t.reshape(H, 2 * CHUNK, CHUNK)
    proj, n1 = _inproj_fwd(xs, g1, win_full)
    _gather_pass_on(gather, [1], "gather_out_pass")
    a_out = _gmlp_fwd(proj, lg, lb, ws, bs_t, A)
    b_out = _attn_fwd(proj, bias, sinks, A, B)
    _gather_pass_on(gather, [2], "gather_up_pass")
    (wout_g,) = _gather_end(gather, [1], "gather_out_end")
    wout_full = wout_g.reshape(A + B, D)
    h1, mixed = _outproj_fwd(a_out, b_out, ga, gb, xs, wout_full)
    _gather_pass_on(gather, [3], "gather_down_pass")
    (wup_g,) = _gather_end(gather, [2], "gather_up_end")
    z, n2 = _ffn_up(h1, g2, wup_g)
    (wdown_g,) = _gather_end(gather, [3], "gather_down_end")
    h2 = _ffn_down(h1, z, wdown_g.reshape(-1, D))
    loss_part, dg3, dh2, dh2b = _final_loss(h2, g3, target)

    def reduce_to_chip(state, name):
        part, received = _sibling_exchange_end(state, name + "_sib_end")
        return _chip_exchange_begin(_chip_sum(part, received, name + "_chip_sum"), name + "_chip")

    dwdown = _matmul_tn(z, dh2b, "grad_w_down", square_a=True).reshape(wdown_g.shape)
    sib_down = _sibling_exchange_begin(dwdown, "rs_down_sib")
    dzp = _ffn_down_bwd(dh2b, z, wdown_g.reshape(-1, D))
    chip_down = reduce_to_chip(sib_down, "rs_down")
    dwup = _matmul_tn(n2, dzp, "grad_w_up", col_blocks=N_DEV)
    sib_up = _sibling_exchange_begin(dwup, "rs_up_sib")
    dh1, dh1b, dg2 = _ffn_up_bwd(dzp, dh2, h1, g2, wup_g)
    chip_up = reduce_to_chip(sib_up, "rs_up")
    da, db, dga, dgb = _outproj_bwd(dh1b, wout_full, a_out, b_out, ga, gb)
    dwout = _matmul_tn(mixed, dh1b, "grad_w_out").reshape(wout_g.shape)
    sib_out = _sibling_exchange_begin(dwout, "rs_out_sib")
    duv, dlg, dlb, dws, dbs_t = _gmlp_bwd(proj, da, lg, lb, ws, ws_t, bs_t, A)
    dproj, dbias_t, dsinks = _attn_bwd(proj, db, duv, bias_t, sinks, A, B)
    chip_out = reduce_to_chip(sib_out, "rs_out")
    dtable_t = _bias_bwd(dbias_t.reshape(H, -1), onehot_kq)
    dwin = _matmul_tn(n1, dproj, "grad_w_in")
    dwin = jnp.transpose(dwin.reshape(D, N_DEV, Pb), (1, 0, 2))
    sib_in = _sibling_exchange_begin(dwin, "rs_in_sib")
    grad_x, dg1 = _inproj_bwd(dproj, win_full, xs, dh1, g1)

    small_w = [rel_bias_table, mix_norm_g, gate_norm_g, gate_norm_b, w_spatial, b_spatial, attn_sinks,
               out_norm_a_g, out_norm_b_g, ffn_norm_g, final_norm_g]
    small_m = [m_rel_bias_table, m_mix_norm_g, m_gate_norm_g, m_gate_norm_b, m_w_spatial, m_b_spatial, m_attn_sinks,
               m_out_norm_a_g, m_out_norm_b_g, m_ffn_norm_g, m_final_norm_g]
    small_v = [v_rel_bias_table, v_mix_norm_g, v_gate_norm_g, v_gate_norm_b, v_w_spatial, v_b_spatial, v_attn_sinks,
               v_out_norm_a_g, v_out_norm_b_g, v_ffn_norm_g, v_final_norm_g]
    small_g = [jnp.transpose(dtable_t), dg1, dlg, dlb, dws, jnp.transpose(dbs_t), dsinks, dga, dgb, dg2, dg3]
    shapes = [w.shape for w in small_w]
    big = [None] * 4

    def adam_of(k, state, w, m, v):
        csum, received = _chip_exchange_end(state, "rs_%d_end" % k)
        big[k] = [o.reshape(w.shape) for o in _adam_sharded(csum, received, w[0], m[0], v[0], "adam_%d" % k)]

    small_gather = _gather_begin([_pack(small_g)], "small_gather_start")
    chip_in = reduce_to_chip(sib_in, "rs_in")
    _gather_pass_on(small_gather, [0], "small_gather_pass")
    adam_of(3, chip_down, w_down, m_w_down, v_w_down)
    (gathered,) = _gather_end(small_gather, [0], "small_gather_end")
    sg, sd, sm, sv = [_unpack(o, shapes) for o in _adam_small(gathered, _pack(small_w), _pack(small_m), _pack(small_v))]
    adam_of(2, chip_up, w_up, m_w_up, v_w_up)
    adam_of(1, chip_out, w_out, m_w_out, v_w_out)
    adam_of(0, chip_in, w_in, m_w_in, v_w_in)

    loss = lax.psum(loss_part[0, 0], ("x", "y", "c"))

    order = ["s0", "s1", "b0", "s2", "s3", "s4", "s5", "s6", "s7", "s8", "b1", "s9", "b2", "b3", "s10"]

    def group(idx):
        small = (sg, sd, sm, sv)[idx]
        return [small[int(t[1:])] if t[0] == "s" else big[int(t[1:])][idx] for t in order]

    return (loss, grad_x.reshape(x.shape), *group(0), *group(1), *group(2), *group(3))
```

```python
import functools
import math

import numpy as np
import jax
import jax.numpy as jnp
from jax import lax
from jax.experimental import pallas as pl
from jax.experimental.pallas import tpu as pltpu

F32 = jnp.float32
BF16 = jnp.bfloat16
SDS = jax.ShapeDtypeStruct
MESH = pl.DeviceIdType.MESH

N_DEV = 8
EPS = 1e-5
NEG = -1e30
CHUNK = 128
GROUP_DIM = 128
HEAD_DIM = 64
KV_HEADS = 2
N_BUCKETS = 32
MAX_DISTANCE = 128
ADAM_LR, ADAM_B1, ADAM_B2, ADAM_EPS, ADAM_WD, ADAM_STEP = 0.001, 0.9, 0.999, 1e-08, 0.01, 10
GELU_C0 = math.sqrt(2.0 / math.pi)
GELU_C1 = 0.044715

V7X_VMEM_BYTES = 64 * 1024 * 1024
VMEM_LIMIT = V7X_VMEM_BYTES - 8 * 1024 * 1024
LANE = 128

NN = ((1,), (0,))
NT = ((1,), (1,))
TN = ((0,), (0,))


def _dot(a, b, dims):
    return lax.dot_general(a, b, (dims, ((), ())), preferred_element_type=F32)


def _tile(n, pref, unit=LANE):
    best = None
    for t in range(unit, min(n, pref) + 1, unit):
        if n % t == 0:
            best = t
    return n if best is None else best


def _params(n_grid):
    return pltpu.CompilerParams(dimension_semantics=("arbitrary",) * n_grid, vmem_limit_bytes=VMEM_LIMIT)


def _resident(shape):
    return pl.BlockSpec(shape, lambda i: (0, 0), pipeline_mode=pl.Buffered(1))


def _gelu(x):
    return 0.5 * x * (1.0 + jnp.tanh(GELU_C0 * (x + GELU_C1 * x * x * x)))


def _gelu_and_grad(x):
    x2 = x * x
    t = jnp.tanh(GELU_C0 * x * (1.0 + GELU_C1 * x2))
    val = 0.5 * x * (1.0 + t)
    grad = 0.5 * (1.0 + t) + 0.5 * x * (1.0 - t * t) * (GELU_C0 * (1.0 + 3.0 * GELU_C1 * x2))
    return val, grad


def _rms_stats(x):
    return lax.rsqrt(jnp.mean(x * x, axis=-1, keepdims=True) + EPS)


def _rms_bwd(dy, x, r, g):
    w = dy * g
    return r * w - x * (r * r * r) * jnp.mean(w * x, axis=-1, keepdims=True)


def _t5_bucket():
    i = np.arange(CHUNK)[:, None]
    j = np.arange(2 * CHUNK)[None, :]
    rel = np.maximum(i + CHUNK - j, 0)
    n_exact = N_BUCKETS // 2
    relf = np.maximum(rel, n_exact).astype(np.float32)
    large = n_exact + (np.log(relf / np.float32(n_exact)) / np.float32(math.log(MAX_DISTANCE / n_exact))
                       * np.float32(N_BUCKETS - n_exact)).astype(np.int32)
    large = np.minimum(large, N_BUCKETS - 1)
    bucket = np.where(rel < n_exact, rel, large)
    in_window = (i + CHUNK - j >= 0) & (i + CHUNK - j < CHUNK)
    return bucket.astype(np.int32), in_window


def _split3(x):
    hi = x.astype(BF16)
    r1 = x - hi.astype(F32)
    mid = r1.astype(BF16)
    lo = (r1 - mid.astype(F32)).astype(BF16)
    return hi, mid, lo


HBM_SPEC = pl.BlockSpec(memory_space=pltpu.HBM)


def _mesh_pos():
    return lax.axis_index("x"), lax.axis_index("y"), lax.axis_index("c")


def _dev_index(px, py, pc):
    return 4 * px + 2 * py + pc


def _all_gather(shards, name):
    n = len(shards)

    def body(*refs):
        ins, outs = refs[:n], refs[n:2 * n]
        send_sems, recv_sems, local_sems = refs[2 * n:]
        x, y, c = _mesh_pos()
        me, sibling = (x, y, c), (x, y, 1 - c)
        chips = [(1 - x, y), (x, 1 - y), (1 - x, 1 - y)]

        def copy(a, k, block, to, src=None):
            dst = outs[a].at[_dev_index(*block)]
            return pltpu.make_async_remote_copy(
                src_ref=dst if src is None else src, dst_ref=dst,
                send_sem=send_sems.at[a * 7 + k], recv_sem=recv_sems.at[a * 7 + k],
                device_id=to, device_id_type=MESH)

        mine = [pltpu.make_async_copy(ins[a], outs[a].at[_dev_index(*me)], local_sems.at[a]) for a in range(n)]
        first = []
        for a in range(n):
            for j, chip in enumerate(chips):
                first.append(copy(a, 1 + j, me, (*chip, c), src=ins[a]))
            first.append(copy(a, 0, me, sibling, src=ins[a]))
        for cp in first:
            cp.start()
        for cp in mine:
            cp.start()
        passed = []
        for a in range(n):
            for j, chip in enumerate(chips):
                copy(a, 1 + j, (*chip, c), me).wait_recv()
                fwd = copy(a, 4 + j, (*chip, c), sibling)
                fwd.start()
                passed.append(fwd)
        for a in range(n):
            copy(a, 0, sibling, me).wait_recv()
            for j, chip in enumerate(chips):
                copy(a, 4 + j, (*chip, 1 - c), me).wait_recv()
        for cp in first + passed:
            cp.wait_send()
        for cp in mine:
            cp.wait()

    return _CHAIN.call(
        body, name=name,
        out_shape=[SDS((N_DEV,) + s.shape, s.dtype) for s in shards],
        in_specs=[HBM_SPEC] * n, out_specs=[HBM_SPEC] * n,
        scratch_shapes=[pltpu.SemaphoreType.DMA((7 * n,)), pltpu.SemaphoreType.DMA((7 * n,)),
                        pltpu.SemaphoreType.DMA((n,))],
    )(*shards)


SEM_SPEC = pl.BlockSpec(memory_space=pltpu.SEMAPHORE)
ANY_SPEC = pl.BlockSpec(memory_space=pl.ANY)
VMEM_SPEC = pl.BlockSpec(memory_space=pltpu.VMEM)
TOKEN_SPEC = VMEM_SPEC
TOKEN = SDS((8, LANE), F32)
SIDE_EFFECT = pltpu.SideEffectType.DATAFLOW_SIDE_EFFECTING


def _hbm(x):
    return pltpu.with_memory_space_constraint(x, pltpu.HBM)


class _CallChain:
    def __init__(self):
        self.token = None

    def call(self, body, *, in_specs, out_specs, out_shape, **kwargs):
        dep, n_in = self.token, len(in_specs)
        single = not isinstance(out_shape, (list, tuple))
        out_shapes = [out_shape] if single else list(out_shape)
        out_specs = [out_specs] if single else list(out_specs)
        n_out = len(out_shapes)
        n_dep = 0 if dep is None else 1
        token_spec = pl.BlockSpec((8, LANE), lambda *_: (0, 0)) if kwargs.get("grid") else VMEM_SPEC

        def chained(*refs):
            outs_at = n_in + n_dep
            body(*refs[:n_in], *refs[outs_at:outs_at + n_out], *refs[outs_at + n_out + 1:])
            token = refs[outs_at + n_out]
            token[...] = jnp.zeros_like(token)

        inner = pl.pallas_call(chained, in_specs=list(in_specs) + [ANY_SPEC] * n_dep, out_specs=out_specs + [token_spec],
                               out_shape=out_shapes + [TOKEN], **kwargs)

        def run(*operands):
            outs = inner(*operands) if dep is None else inner(*operands, dep)
            self.token = outs[n_out]
            return outs[0] if single else list(outs[:n_out])

        return run


_CHAIN = _CallChain()


def _split_start(bufs, copies_of, n_sems, name):
    n = len(bufs)

    def body(*refs):
        ins = refs[:n]
        send_sems, recv_sems = refs[n], refs[n + 1]
        for src, dst, k, target in copies_of(ins):
            pltpu.make_async_remote_copy(src_ref=src, dst_ref=dst, send_sem=send_sems.at[k], recv_sem=recv_sems.at[k],
                                         device_id=target, device_id_type=MESH).start()

    outs = _CHAIN.call(
        body, name=name,
        out_shape=[pltpu.SemaphoreType.DMA((n_sems,)), pltpu.SemaphoreType.DMA((n_sems,))]
        + [pltpu.HBM(b.shape, b.dtype) for b in bufs],
        in_specs=[HBM_SPEC] * n, out_specs=[SEM_SPEC, SEM_SPEC] + [HBM_SPEC] * n,
        input_output_aliases={a: 2 + a for a in range(n)},
        compiler_params=pltpu.CompilerParams(has_side_effects=SIDE_EFFECT),
    )(*[_hbm(b) for b in bufs])
    return outs[0], outs[1], list(outs[2:2 + n])


def _split_wait(bufs, sem_sets, waits_of, name):
    n, ns = len(bufs), len(sem_sets)
    flat_sems = [s for pair in sem_sets for s in pair]

    def body(*refs):
        ins = refs[:n]
        sems = refs[n:n + 2 * ns]
        x, y, c = _mesh_pos()
        for kind, src, dst, send_sem, recv_sem in waits_of(ins, [(sems[2 * i], sems[2 * i + 1]) for i in range(ns)]):
            cp = pltpu.make_async_remote_copy(src_ref=src, dst_ref=dst, send_sem=send_sem, recv_sem=recv_sem,
                                              device_id=(x, y, c), device_id_type=MESH)
            if kind == "send":
                cp.wait_send()
            else:
                cp.wait_recv()

    outs = _CHAIN.call(
        body, name=name,
        out_shape=[pltpu.HBM(b.shape, b.dtype) for b in bufs],
        in_specs=[HBM_SPEC] * n + [SEM_SPEC] * (2 * ns), out_specs=[HBM_SPEC] * n,
        input_output_aliases={a: a for a in range(n)},
        compiler_params=pltpu.CompilerParams(has_side_effects=SIDE_EFFECT),
    )(*bufs, *flat_sems)
    return list(outs)


def _gather_begin(shards, name):
    me = _dev_index(*_mesh_pos())
    lands = [lax.dynamic_update_index_in_dim(lax.empty((N_DEV,) + s.shape, s.dtype), s, me, 0) for s in shards]

    def copies_of(ins):
        x, y, c = _mesh_pos()
        targets = [(x, y, 1 - c), (1 - x, y, c), (x, 1 - y, c), (1 - x, 1 - y, c)]
        out = []
        for a, land in enumerate(ins):
            blk = land.at[_dev_index(x, y, c)]
            for k in (1, 2, 3, 0):
                out.append((blk, blk, 4 * a + k, targets[k]))
        return out

    send_sems, recv_sems, lands = _split_start(lands, copies_of, 4 * len(shards), name)
    return dict(lands=lands, sems=(send_sems, recv_sems), fwd={})


def _gather_pass_on(state, which, name):
    def arrivals(ins, sems):
        x, y, c = _mesh_pos()
        chips = [(1 - x, y), (x, 1 - y), (1 - x, 1 - y)]
        out = []
        for i, a in enumerate(which):
            for j, (px, py) in enumerate(chips):
                blk = ins[i].at[_dev_index(px, py, c)]
                out.append(("recv", blk, blk, sems[0][0].at[4 * a + 1 + j], sems[0][1].at[4 * a + 1 + j]))
        return out

    bufs = _split_wait([state["lands"][a] for a in which], [state["sems"]], arrivals, name + "_arrived")

    def copies_of(ins):
        x, y, c = _mesh_pos()
        chips = [(1 - x, y), (x, 1 - y), (1 - x, 1 - y)]
        out = []
        for i in range(len(which)):
            for j, (px, py) in enumerate(chips):
                blk = ins[i].at[_dev_index(px, py, c)]
                out.append((blk, blk, 3 * i + j, (x, y, 1 - c)))
        return out

    send_sems, recv_sems, bufs = _split_start(bufs, copies_of, 3 * len(which), name)
    for i, a in enumerate(which):
        state["lands"][a] = bufs[i]
    state["fwd"][tuple(which)] = (send_sems, recv_sems)


def _gather_end(state, which, name):
    def waits(ins, sems):
        x, y, c = _mesh_pos()
        chips = [(1 - x, y), (x, 1 - y), (1 - x, 1 - y)]
        (s_send, s_recv), (f_send, f_recv) = sems
        out = []
        for i, a in enumerate(which):
            mine = ins[i].at[_dev_index(x, y, c)]
            sib = ins[i].at[_dev_index(x, y, 1 - c)]
            out.append(("recv", sib, sib, s_send.at[4 * a], s_recv.at[4 * a]))
            for j, (px, py) in enumerate(chips):
                theirs = ins[i].at[_dev_index(px, py, 1 - c)]
                out.append(("recv", theirs, theirs, f_send.at[3 * i + j], f_recv.at[3 * i + j]))
            for k in range(4):
                out.append(("send", mine, mine, s_send.at[4 * a + k], s_recv.at[4 * a + k]))
            for j, (px, py) in enumerate(chips):
                passed = ins[i].at[_dev_index(px, py, c)]
                out.append(("send", passed, passed, f_send.at[3 * i + j], f_recv.at[3 * i + j]))
        return out

    bufs = _split_wait([state["lands"][a] for a in which], [state["sems"], state["fwd"][tuple(which)]], waits, name)
    for i, a in enumerate(which):
        state["lands"][a] = bufs[i]
    return bufs


def _sibling_exchange_begin(part, name):
    land = lax.empty((4,) + part.shape[1:], part.dtype)

    def copies_of(ins):
        x, y, c = _mesh_pos()
        return [(ins[0].at[2 * j + (1 - c)], ins[1].at[j], j, (x, y, 1 - c)) for j in range(4)]

    send_sems, recv_sems, bufs = _split_start([part, land], copies_of, 4, name)
    return dict(bufs=bufs, sems=(send_sems, recv_sems))


def _sibling_exchange_end(state, name):
    def waits(ins, sems):
        _, _, c = _mesh_pos()
        out = []
        for j in range(4):
            for kind in ("send", "recv"):
                out.append((kind, ins[0].at[2 * j + (1 - c)], ins[1].at[j], sems[0][0].at[j], sems[0][1].at[j]))
        return out

    return _split_wait(state["bufs"], [state["sems"]], waits, name)


CHIP_FLIPS = (2, 1, 3)


def _chip_exchange_begin(csum, name):
    land = lax.empty((3,) + csum.shape[1:], csum.dtype)

    def copies_of(ins):
        x, y, c = _mesh_pos()
        chips = [(1 - x, y), (x, 1 - y), (1 - x, 1 - y)]
        return [(ins[0].at[CHIP_FLIPS[r]], ins[1].at[r], r, (px, py, c)) for r, (px, py) in enumerate(chips)]

    send_sems, recv_sems, bufs = _split_start([csum, land], copies_of, 3, name)
    return dict(bufs=bufs, sems=(send_sems, recv_sems))


def _chip_exchange_end(state, name):
    def waits(ins, sems):
        out = []
        for r in range(3):
            for kind in ("send", "recv"):
                out.append((kind, ins[0].at[CHIP_FLIPS[r]], ins[1].at[r], sems[0][0].at[r], sems[0][1].at[r]))
        return out

    return _split_wait(state["bufs"], [state["sems"]], waits, name)


def _chip_sum(part, recv, name):
    _, R, C = part.shape
    tr = _tile(R, 512, 16)
    place = jnp.stack([lax.axis_index("c"), 2 * lax.axis_index("x") + lax.axis_index("y")]).astype(jnp.int32)

    def body(place_ref, p_ref, r_ref, o_ref):
        o_ref[...] = (p_ref[...].astype(F32) + r_ref[...].astype(F32)).astype(o_ref.dtype)

    def chip(p, place_ref):
        return jnp.bitwise_xor(p, place_ref[1])

    grid_spec = pltpu.PrefetchScalarGridSpec(
        num_scalar_prefetch=1, grid=(4, R // tr),
        in_specs=[pl.BlockSpec((None, tr, C), lambda p, i, place_ref: (2 * chip(p, place_ref) + place_ref[0], i, 0)),
                  pl.BlockSpec((None, tr, C), lambda p, i, place_ref: (chip(p, place_ref), i, 0))],
        out_specs=pl.BlockSpec((None, tr, C), lambda p, i, place_ref: (p, i, 0)))
    return pl.pallas_call(body, name=name, grid_spec=grid_spec, out_shape=SDS((4, R, C), part.dtype),
                          compiler_params=_params(2))(place, part, recv)


def _bias_fwd(table_t, onehot_t, onehot_kq_t):
    H = table_t.shape[0]
    n = onehot_t.shape[1]

    def body(t_ref, oh_ref, oh_kq_ref, o_ref, o_kq_ref):
        hi, mid, lo = _split3(t_ref[...])
        for src, dst in ((oh_ref, o_ref), (oh_kq_ref, o_kq_ref)):
            oh = src[...]
            dst[...] = _dot(hi, oh, NN) + _dot(mid, oh, NN) + _dot(lo, oh, NN)

    return _CHAIN.call(body, name="bias_fwd", in_specs=[VMEM_SPEC] * 3, out_specs=[VMEM_SPEC] * 2,
                       out_shape=[SDS((H, n), F32)] * 2, compiler_params=_params(0))(table_t, onehot_t, onehot_kq_t)


def _inproj_fwd(x, g, w):
    T, D = x.shape
    P = w.shape[1]
    tm = _tile(T, 512)

    def body(x_ref, g_ref, w_ref, proj_ref, n_ref):
        xv = x_ref[...]
        n = (xv * _rms_stats(xv) * g_ref[...]).astype(BF16)
        n_ref[...] = n
        proj_ref[...] = _dot(n, w_ref[...], NN)

    return _CHAIN.call(
        body, name="inproj_fwd", grid=(T // tm,),
        in_specs=[pl.BlockSpec((tm, D), lambda i: (i, 0)), pl.BlockSpec((1, D), lambda i: (0, 0)), _resident((D, P))],
        out_specs=[pl.BlockSpec((tm, P), lambda i: (i, 0)), pl.BlockSpec((tm, D), lambda i: (i, 0))],
        out_shape=[SDS((T, P), F32), SDS((T, D), BF16)], compiler_params=_params(1))(x, g, w)


def _layer_norm_group(vg, lg, lb):
    mu = jnp.mean(vg, axis=-1, keepdims=True)
    xc = vg - mu
    rstd = lax.rsqrt(jnp.mean(xc * xc, axis=-1, keepdims=True) + EPS)
    vhat = xc * rstd
    return vhat, rstd, vhat * lg + lb


def _gmlp_fwd(proj, lg, lb, w_s, bs_t, A):
    T = proj.shape[0]
    G = A // GROUP_DIM
    tm = _tile(T, 512)
    nc = tm // CHUNK

    def body(u_ref, v_ref, lg_ref, lb_ref, w_ref, bst_ref, a_ref):
        row = lax.broadcasted_iota(jnp.int32, (CHUNK, CHUNK), 0)
        col = lax.broadcasted_iota(jnp.int32, (CHUNK, CHUNK), 1)
        causal = row >= col
        for g in range(G):
            sl = slice(g * GROUP_DIM, (g + 1) * GROUP_DIM)
            _, _, vn = _layer_norm_group(_gelu(v_ref[:, sl]), lg_ref[:, sl], lb_ref[:, sl])
            vnb = vn.astype(BF16)
            wm = jnp.where(causal, w_ref[g], 0.0).astype(BF16)
            ug = _gelu(u_ref[:, sl])
            bcol = bst_ref[:, g:g + 1]
            for c in range(nc):
                rs = slice(c * CHUNK, (c + 1) * CHUNK)
                a_ref[rs, sl] = ug[rs] * (_dot(wm, vnb[rs], NN) + bcol)

    return _CHAIN.call(
        body, name="gmlp_fwd", grid=(T // tm,),
        in_specs=[pl.BlockSpec((tm, A), lambda i: (i, 0)), pl.BlockSpec((tm, A), lambda i: (i, 1)),
                  pl.BlockSpec((1, A), lambda i: (0, 0)), pl.BlockSpec((1, A), lambda i: (0, 0)),
                  pl.BlockSpec((G, CHUNK, CHUNK), lambda i: (0, 0, 0)), pl.BlockSpec((CHUNK, G), lambda i: (0, 0))],
        out_specs=pl.BlockSpec((tm, A), lambda i: (i, 0)),
        out_shape=SDS((T, A), F32), compiler_params=_params(1))(proj, proj, lg, lb, w_s, bs_t)


def _attn_masks(first_tile):
    ii = lax.broadcasted_iota(jnp.int32, (CHUNK, 2 * CHUNK), 0)
    jj = lax.broadcasted_iota(jnp.int32, (CHUNK, 2 * CHUNK), 1)
    in_window = (jj > ii) & (jj <= ii + CHUNK)
    first_mask = in_window & jnp.logical_or(jnp.logical_not(first_tile), jj >= CHUNK)
    return in_window, first_mask


def _softmax_with_sink(s, sink, axis):
    m = jnp.maximum(jnp.max(s, axis=axis, keepdims=True), sink)
    p = jnp.exp(s - m)
    e_sink = jnp.exp(sink - m)
    inv = 1.0 / (jnp.sum(p, axis=axis, keepdims=True) + e_sink)
    return p * inv, e_sink * inv


def _pad_heads(band, group):
    lane = lax.broadcasted_iota(jnp.int32, band.shape, 1)
    if group == 0:
        low = jnp.where(lane < HEAD_DIM, band, 0.0)
        high = pltpu.roll(low, HEAD_DIM, 1)
    else:
        high = jnp.where(lane >= HEAD_DIM, band, 0.0)
        low = pltpu.roll(high, HEAD_DIM, 1)
    return low.astype(BF16), high.astype(BF16)


def _attn_specs(tq, A, B, reverse_tiles=None):
    nb = tq // CHUNK
    kcol = (2 * A + B) // LANE
    if reverse_tiles is None:
        tile = lambda i: i
    else:
        tile = lambda i: reverse_tiles - 1 - i
    prev = lambda i: jnp.maximum(tile(i) * nb - 1, 0)
    return [pl.BlockSpec((tq, B), lambda i: (tile(i), 2 * A // B)),
            pl.BlockSpec((tq, LANE), lambda i: (tile(i), kcol)),
            pl.BlockSpec((tq, LANE), lambda i: (tile(i), kcol + 1)),
            pl.BlockSpec((CHUNK, LANE), lambda i: (prev(i), kcol)),
            pl.BlockSpec((CHUNK, LANE), lambda i: (prev(i), kcol + 1))]


def _attn_fwd(proj, bias, sinks, A, B):
    T = proj.shape[0]
    H = B // HEAD_DIM
    qpk = H // KV_HEADS
    tq = _tile(T, 512)
    nb = tq // CHUNK

    scale = HEAD_DIM ** -0.5

    def body(sink_ref, q_ref, k_ref, v_ref, kp_ref, vp_ref, bias_ref, o_ref):
        in_window, first_mask = _attn_masks(pl.program_id(0) == 0)
        for b in range(nb):
            rows = slice(b * CHUNK, (b + 1) * CHUNK)
            if b == 0:
                kprev, vprev, mask = kp_ref[...], vp_ref[...], first_mask
            else:
                prows = slice((b - 1) * CHUNK, b * CHUNK)
                kprev, vprev, mask = k_ref[prows, :], v_ref[prows, :], in_window
            kband = jnp.concatenate([kprev, k_ref[rows, :]], axis=0)
            vband = jnp.concatenate([vprev, v_ref[rows, :]], axis=0)
            outs = []
            for g in range(KV_HEADS):
                k_low, k_high = _pad_heads(kband, g)
                v_both = jnp.concatenate(_pad_heads(vband, g), axis=0)
                for pair in range(qpk // 2):
                    h = g * qpk + 2 * pair
                    qs = (q_ref[rows, h * HEAD_DIM:(h + 2) * HEAD_DIM] * scale).astype(BF16)
                    probs = []
                    for head, kz in ((h, k_low), (h + 1, k_high)):
                        s = jnp.where(mask, _dot(qs, kz, NT) + bias_ref[head], NEG)
                        probs.append(_softmax_with_sink(s, sink_ref[head], -1)[0])
                    outs.append(_dot(jnp.concatenate(probs, axis=1).astype(BF16), v_both, NN))
            o_ref[rows, :] = jnp.concatenate(outs, axis=1)

    return _CHAIN.call(
        body, name="attn_fwd", grid=(T // tq,),
        in_specs=[pl.BlockSpec(memory_space=pltpu.SMEM)] + _attn_specs(tq, A, B)
        + [pl.BlockSpec((H, CHUNK, 2 * CHUNK), lambda i: (0, 0, 0))],
        out_specs=pl.BlockSpec((tq, B), lambda i: (i, 0)),
        out_shape=SDS((T, B), F32), compiler_params=_params(1))(sinks, proj, proj, proj, proj, proj, bias)


def _outproj_fwd(a, b, ga, gb, x, w):
    T, A = a.shape
    B = b.shape[1]
    D = x.shape[1]
    tm = _tile(T, 512)

    def body(a_ref, b_ref, ga_ref, gb_ref, x_ref, w_ref, h_ref, mix_ref):
        av, bv = a_ref[...], b_ref[...]
        mix_ref[:, :A] = (av * _rms_stats(av) * ga_ref[...]).astype(BF16)
        mix_ref[:, A:] = (bv * _rms_stats(bv) * gb_ref[...]).astype(BF16)
        h_ref[...] = x_ref[...] + _dot(mix_ref[...], w_ref[...], NN)

    return _CHAIN.call(
        body, name="outproj_fwd", grid=(T // tm,),
        in_specs=[pl.BlockSpec((tm, A), lambda i: (i, 0)), pl.BlockSpec((tm, B), lambda i: (i, 0)),
                  pl.BlockSpec((1, A), lambda i: (0, 0)), pl.BlockSpec((1, B), lambda i: (0, 0)),
                  pl.BlockSpec((tm, D), lambda i: (i, 0)), _resident((A + B, D))],
        out_specs=[pl.BlockSpec((tm, D), lambda i: (i, 0)), pl.BlockSpec((tm, A + B), lambda i: (i, 0))],
        out_shape=[SDS((T, D), F32), SDS((T, A + B), BF16)], compiler_params=_params(1))(a, b, ga, gb, x, w)


def _ffn_up(h1, g, w_up):
    T, D = h1.shape
    Fb = w_up.shape[2]
    F = N_DEV * Fb
    tm, tf = _tile(T, 1024), _tile(Fb, 1024)
    per = Fb // tf

    def body(h_ref, g_ref, wu_ref, z_ref, n_ref, nbuf):
        @pl.when(pl.program_id(1) == 0)
        def _():
            hv = h_ref[...]
            n = (hv * _rms_stats(hv) * g_ref[...]).astype(BF16)
            nbuf[...] = n
            n_ref[...] = n

        z_ref[...] = jnp.maximum(_dot(nbuf[...], wu_ref[...], NN), 0.0).astype(BF16)

    return _CHAIN.call(
        body, name="ffn_up", grid=(T // tm, F // tf),
        in_specs=[pl.BlockSpec((tm, D), lambda i, j: (i, 0)), pl.BlockSpec((1, D), lambda i, j: (0, 0)),
                  pl.BlockSpec((None, D, tf), lambda i, j: (j // per, 0, j % per))],
        out_specs=[pl.BlockSpec((tm, tf), lambda i, j: (i, j)), pl.BlockSpec((tm, D), lambda i, j: (i, 0))],
        out_shape=[SDS((T, F), BF16), SDS((T, D), BF16)],
        scratch_shapes=[pltpu.VMEM((tm, D), BF16)], compiler_params=_params(2))(h1, g, w_up)


def _ffn_down(h1, z, w_down):
    T, D = h1.shape
    F = w_down.shape[0]
    tm, tn, tk = _tile(T, 1024), _tile(D, 1024), _tile(F, 4096)

    def body(h_ref, z_ref, wd_ref, h2_ref):
        k = pl.program_id(2)

        @pl.when(k == 0)
        def _():
            h2_ref[...] = h_ref[...]

        zf = z_ref[...].astype(F32)
        h2_ref[...] += _dot((zf * zf).astype(BF16), wd_ref[...], NN)

    return _CHAIN.call(
        body, name="ffn_down", grid=(T // tm, D // tn, F // tk),
        in_specs=[pl.BlockSpec((tm, tn), lambda i, j, k: (i, j)), pl.BlockSpec((tm, tk), lambda i, j, k: (i, k)),
                  pl.BlockSpec((tk, tn), lambda i, j, k: (k, j))],
        out_specs=pl.BlockSpec((tm, tn), lambda i, j, k: (i, j)),
        out_shape=SDS((T, D), F32), compiler_params=_params(3))(h1, z, w_down)


def _final_loss(h2, g, target):
    T, D = h2.shape
    tm = _tile(T, 512)

    def body(h_ref, g_ref, t_ref, loss_ref, dg_ref, dh_ref, dhb_ref):
        @pl.when(pl.program_id(0) == 0)
        def _():
            loss_ref[...] = jnp.zeros_like(loss_ref)
            dg_ref[...] = jnp.zeros_like(dg_ref)

        hv, gv = h_ref[...], g_ref[...]
        r = _rms_stats(hv)
        hn = hv * r
        e = hn * gv - t_ref[...]
        loss_ref[...] += (0.5 / D) * jnp.sum(jnp.sum(e * e, axis=0, keepdims=True), axis=-1, keepdims=True)
        dy = e * (1.0 / D)
        dg_ref[...] += jnp.sum(dy * hn, axis=0, keepdims=True)
        dh = _rms_bwd(dy, hv, r, gv)
        dh_ref[...] = dh
        dhb_ref[...] = dh.astype(BF16)

    return _CHAIN.call(
        body, name="final_loss", grid=(T // tm,),
        in_specs=[pl.BlockSpec((tm, D), lambda i: (i, 0)), pl.BlockSpec((1, D), lambda i: (0, 0)),
                  pl.BlockSpec((tm, D), lambda i: (i, 0))],
        out_specs=[pl.BlockSpec((1, 1), lambda i: (0, 0)), pl.BlockSpec((1, D), lambda i: (0, 0)),
                   pl.BlockSpec((tm, D), lambda i: (i, 0)), pl.BlockSpec((tm, D), lambda i: (i, 0))],
        out_shape=[SDS((1, 1), F32), SDS((1, D), F32), SDS((T, D), F32), SDS((T, D), BF16)],
        compiler_params=_params(1))(h2, g, target)


def _ffn_down_bwd(dh2b, z, w_down):
    T, D = dh2b.shape
    F = w_down.shape[0]
    tm, tf = _tile(T, 1024), _tile(F, 1024)

    def body(dh_ref, z_ref, wd_ref, dzp_ref):
        dzz = _dot(dh_ref[...], wd_ref[...], NT)
        dzp_ref[...] = (dzz * (2.0 * z_ref[...].astype(F32))).astype(BF16)

    return _CHAIN.call(
        body, name="ffn_down_bwd", grid=(T // tm, F // tf),
        in_specs=[pl.BlockSpec((tm, D), lambda i, j: (i, 0)), pl.BlockSpec((tm, tf), lambda i, j: (i, j)),
                  pl.BlockSpec((tf, D), lambda i, j: (j, 0))],
        out_specs=pl.BlockSpec((tm, tf), lambda i, j: (i, j)),
        out_shape=SDS((T, F), BF16), compiler_params=_params(2))(dh2b, z, w_down)


def _ffn_up_bwd(dzp, w_up_t):
    T, F = dzp.shape
    D = w_up_t.shape[1]
    tm, tn, tk = _tile(T, 1024), _tile(D, 1024), _tile(F, 4096)

    def body(dzp_ref, w_ref, dn_ref):
        part = _dot(dzp_ref[...], w_ref[...], NN)

        @pl.when(pl.program_id(2) == 0)
        def _():
            dn_ref[...] = part

        @pl.when(pl.program_id(2) > 0)
        def _():
            dn_ref[...] += part

    return _CHAIN.call(
        body, name="ffn_up_bwd", grid=(T // tm, D // tn, F // tk),
        in_specs=[pl.BlockSpec((tm, tk), lambda i, j, k: (i, k)), pl.BlockSpec((tk, tn), lambda i, j, k: (k, j))],
        out_specs=pl.BlockSpec((tm, tn), lambda i, j, k: (i, j)),
        out_shape=SDS((T, D), F32), compiler_params=_params(3))(dzp, w_up_t)


def _ffn_norm_bwd(dn, dh2, h1, g):
    T, D = h1.shape
    tm = _tile(T, 256)

    def body(dn_ref, dh_ref, h_ref, g_ref, dh1_ref, dh1b_ref, dg_ref):
        @pl.when(pl.program_id(0) == 0)
        def _():
            dg_ref[...] = jnp.zeros_like(dg_ref)

        hv, dnv = h_ref[...], dn_ref[...]
        r = _rms_stats(hv)
        dg_ref[...] += jnp.sum(dnv * (hv * r), axis=0, keepdims=True)
        dh1 = dh_ref[...] + _rms_bwd(dnv, hv, r, g_ref[...])
        dh1_ref[...] = dh1
        dh1b_ref[...] = dh1.astype(BF16)

    row = pl.BlockSpec((tm, D), lambda i: (i, 0))
    vec = pl.BlockSpec((1, D), lambda i: (0, 0))
    return _CHAIN.call(
        body, name="ffn_norm_bwd", grid=(T // tm,), in_specs=[row, row, row, vec], out_specs=[row, row, vec],
        out_shape=[SDS((T, D), F32), SDS((T, D), BF16), SDS((1, D), F32)], compiler_params=_params(1))(dn, dh2, h1, g)


def _matmul_tn(a, b, name, square_a=False, col_blocks=None):
    T, K = a.shape
    N = b.shape[1]
    tn = _tile(N if col_blocks is None else N // col_blocks, 1792)
    tk = _tile(K, 1024 if tn <= 1024 else 512)

    def body(a_ref, b_ref, o_ref):
        av = a_ref[...]
        if square_a:
            af = av.astype(F32)
            av = (af * af).astype(BF16)
        o_ref[...] = _dot(av, b_ref[...], TN).astype(o_ref.dtype)

    if col_blocks is None:
        out_shape = SDS((K, N), BF16)
        out_spec = pl.BlockSpec((tk, tn), lambda i, j: (i, j))
    else:
        per = (N // col_blocks) // tn
        out_shape = SDS((col_blocks, K, N // col_blocks), BF16)
        out_spec = pl.BlockSpec((None, tk, tn), lambda i, j: (j // per, i, j % per))
    return _CHAIN.call(
        body, name=name, grid=(K // tk, N // tn),
        in_specs=[pl.BlockSpec((T, tk), lambda i, j: (0, i)), pl.BlockSpec((T, tn), lambda i, j: (0, j))],
        out_specs=out_spec, out_shape=out_shape, compiler_params=_params(2))(a, b)


def _outproj_bwd(dh1b, w, a, b, ga, gb):
    T, D = dh1b.shape
    A, B = a.shape[1], b.shape[1]
    tm = _tile(T, 512)

    def body(dh_ref, w_ref, a_ref, b_ref, ga_ref, gb_ref, da_ref, db_ref, dga_ref, dgb_ref):
        @pl.when(pl.program_id(0) == 0)
        def _():
            dga_ref[...] = jnp.zeros_like(dga_ref)
            dgb_ref[...] = jnp.zeros_like(dgb_ref)

        dmix = _dot(dh_ref[...], w_ref[...], NT)
        for src_ref, g_ref, dx_ref, dg_ref, dn in ((a_ref, ga_ref, da_ref, dga_ref, dmix[:, :A]),
                                                   (b_ref, gb_ref, db_ref, dgb_ref, dmix[:, A:])):
            xv = src_ref[...]
            r = _rms_stats(xv)
            dg_ref[...] += jnp.sum(dn * (xv * r), axis=0, keepdims=True)
            dx_ref[...] = _rms_bwd(dn, xv, r, g_ref[...])

    return _CHAIN.call(
        body, name="outproj_bwd", grid=(T // tm,),
        in_specs=[pl.BlockSpec((tm, D), lambda i: (i, 0)), _resident((A + B, D)),
                  pl.BlockSpec((tm, A), lambda i: (i, 0)), pl.BlockSpec((tm, B), lambda i: (i, 0)),
                  pl.BlockSpec((1, A), lambda i: (0, 0)), pl.BlockSpec((1, B), lambda i: (0, 0))],
        out_specs=[pl.BlockSpec((tm, A), lambda i: (i, 0)), pl.BlockSpec((tm, B), lambda i: (i, 0)),
                   pl.BlockSpec((1, A), lambda i: (0, 0)), pl.BlockSpec((1, B), lambda i: (0, 0))],
        out_shape=[SDS((T, A), F32), SDS((T, B), F32), SDS((1, A), F32), SDS((1, B), F32)],
        compiler_params=_params(1))(dh1b, w, a, b, ga, gb)


def _gmlp_bwd(proj, da, lg, lb, w_s, w_st, bs_t, A):
    T = proj.shape[0]
    G = A // GROUP_DIM
    tm = _tile(T, 512)
    nc = tm // CHUNK

    def body(u_ref, v_ref, da_ref, lg_ref, lb_ref, w_ref, wt_ref, bst_ref, duv_ref, dlg_ref, dlb_ref, dw_ref, dbs_ref):
        @pl.when(pl.program_id(0) == 0)
        def _():
            dlg_ref[...] = jnp.zeros_like(dlg_ref)
            dlb_ref[...] = jnp.zeros_like(dlb_ref)
            dw_ref[...] = jnp.zeros_like(dw_ref)
            dbs_ref[...] = jnp.zeros_like(dbs_ref)

        row = lax.broadcasted_iota(jnp.int32, (CHUNK, CHUNK), 0)
        col = lax.broadcasted_iota(jnp.int32, (CHUNK, CHUNK), 1)
        lower = row >= col
        upper = row <= col
        for g in range(G):
            sl = slice(g * GROUP_DIM, (g + 1) * GROUP_DIM)
            lgv = lg_ref[:, sl]
            vg, vg_grad = _gelu_and_grad(v_ref[:, sl])
            vhat, rstd, vn = _layer_norm_group(vg, lgv, lb_ref[:, sl])
            vnb = vn.astype(BF16)
            ug, ug_grad = _gelu_and_grad(u_ref[:, sl])
            dav = da_ref[:, sl]
            wm = jnp.where(lower, w_ref[g], 0.0).astype(BF16)
            wmt = jnp.where(upper, wt_ref[g], 0.0).astype(BF16)
            bcol = bst_ref[:, g:g + 1]
            dw_acc = jnp.zeros((CHUNK, CHUNK), F32)
            dbs_acc = jnp.zeros((CHUNK, 1), F32)
            dvn_parts = []
            dug_parts = []
            for c in range(nc):
                rs = slice(c * CHUNK, (c + 1) * CHUNK)
                mixed = _dot(wm, vnb[rs], NN) + bcol
                dug_parts.append(dav[rs] * mixed)
                dmix = dav[rs] * ug[rs]
                dbs_acc = dbs_acc + jnp.sum(dmix, axis=-1, keepdims=True)
                dmixb = dmix.astype(BF16)
                dw_acc = dw_acc + _dot(dmixb, vnb[rs], NT)
                dvn_parts.append(_dot(wmt, dmixb, NN))
            dvn = jnp.concatenate(dvn_parts, axis=0)
            dug = jnp.concatenate(dug_parts, axis=0)
            dw_ref[g] += jnp.where(lower, dw_acc, 0.0)
            dbs_ref[:, g:g + 1] += dbs_acc
            dlg_ref[:, sl] += jnp.sum(dvn * vhat, axis=0, keepdims=True)
            dlb_ref[:, sl] += jnp.sum(dvn, axis=0, keepdims=True)
            dvhat = dvn * lgv
            dvg = rstd * (dvhat - jnp.mean(dvhat, axis=-1, keepdims=True)
                          - vhat * jnp.mean(dvhat * vhat, axis=-1, keepdims=True))
            duv_ref[:, sl] = (dug * ug_grad).astype(BF16)
            duv_ref[:, A + g * GROUP_DIM:A + (g + 1) * GROUP_DIM] = (dvg * vg_grad).astype(BF16)

    return _CHAIN.call(
        body, name="gmlp_bwd", grid=(T // tm,),
        in_specs=[pl.BlockSpec((tm, A), lambda i: (i, 0)), pl.BlockSpec((tm, A), lambda i: (i, 1)),
                  pl.BlockSpec((tm, A), lambda i: (i, 0)),
                  pl.BlockSpec((1, A), lambda i: (0, 0)), pl.BlockSpec((1, A), lambda i: (0, 0)),
                  pl.BlockSpec((G, CHUNK, CHUNK), lambda i: (0, 0, 0)),
                  pl.BlockSpec((G, CHUNK, CHUNK), lambda i: (0, 0, 0)), pl.BlockSpec((CHUNK, G), lambda i: (0, 0))],
        out_specs=[pl.BlockSpec((tm, 2 * A), lambda i: (i, 0)),
                   pl.BlockSpec((1, A), lambda i: (0, 0)), pl.BlockSpec((1, A), lambda i: (0, 0)),
                   pl.BlockSpec((G, CHUNK, CHUNK), lambda i: (0, 0, 0)), pl.BlockSpec((CHUNK, G), lambda i: (0, 0))],
        out_shape=[SDS((T, 2 * A), BF16), SDS((1, A), F32), SDS((1, A), F32),
                   SDS((G, CHUNK, CHUNK), F32), SDS((CHUNK, G), F32)],
        compiler_params=_params(1))(proj, proj, da, lg, lb, w_s, w_st, bs_t)


def _attn_bwd(proj, do, duv, bias_t, sinks, A, B):
    T, P = proj.shape
    H = B // HEAD_DIM
    qpk = H // KV_HEADS
    tq = _tile(T, 512)
    nb = tq // CHUNK
    n_tiles = T // tq
    scale = HEAD_DIM ** -0.5
    rev = lambda i: n_tiles - 1 - i

    def body(sink_ref, q_ref, k_ref, v_ref, kp_ref, vp_ref, do_ref, duv_ref, bias_ref,
             dproj_ref, dbias_ref, dsink_ref, carry, dkv, sacc):
        step = pl.program_id(0)

        @pl.when(step == 0)
        def _():
            carry[...] = jnp.zeros_like(carry)
            sacc[...] = jnp.zeros_like(sacc)
            dbias_ref[...] = jnp.zeros_like(dbias_ref)

        jj = lax.broadcasted_iota(jnp.int32, (2 * CHUNK, CHUNK), 0)
        ii = lax.broadcasted_iota(jnp.int32, (2 * CHUNK, CHUNK), 1)
        in_window = (jj > ii) & (jj <= ii + CHUNK)
        first_mask = in_window & jnp.logical_or(step != n_tiles - 1, jj >= CHUNK)
        low_query = lax.broadcasted_iota(jnp.int32, (CHUNK, LANE), 1) < HEAD_DIM
        low_key = lax.broadcasted_iota(jnp.int32, (2 * CHUNK, LANE), 1) < HEAD_DIM

        def split_pair(pair_bf16):
            zero = jnp.zeros_like(pair_bf16)
            return jnp.concatenate([jnp.where(low_query, pair_bf16, zero), jnp.where(low_query, zero, pair_bf16)], axis=0)

        dproj_ref[:, :2 * A] = duv_ref[...]
        dkv[...] = jnp.zeros_like(dkv)
        for b in range(nb):
            rows = slice(b * CHUNK, (b + 1) * CHUNK)
            band = slice(b * CHUNK, (b + 2) * CHUNK)
            if b == 0:
                kprev, vprev, mask = kp_ref[...], vp_ref[...], first_mask
            else:
                prows = slice((b - 1) * CHUNK, b * CHUNK)
                kprev, vprev, mask = k_ref[prows, :], v_ref[prows, :], in_window
            kband = jnp.concatenate([kprev, k_ref[rows, :]], axis=0)
            vband = jnp.concatenate([vprev, v_ref[rows, :]], axis=0)
            dq_parts, dk_groups, dv_groups = [], [], []
            for g in range(KV_HEADS):
                k_low, k_high = _pad_heads(kband, g)
                v_low, v_high = _pad_heads(vband, g)
                k_both = jnp.concatenate([k_low, k_high], axis=0)
                dk_acc = jnp.zeros((2 * CHUNK, LANE), F32)
                dv_acc = jnp.zeros((2 * CHUNK, LANE), F32)
                for pair in range(qpk // 2):
                    h = g * qpk + 2 * pair
                    cols = slice(h * HEAD_DIM, (h + 2) * HEAD_DIM)
                    qs = (q_ref[rows, cols] * scale).astype(BF16)
                    dob = do_ref[rows, cols].astype(BF16)
                    probs, dscores = [], []
                    for head, kz, vz in ((h, k_low, v_low), (h + 1, k_high, v_high)):
                        st = jnp.where(mask, _dot(kz, qs, NT) + bias_ref[head], NEG)
                        pt, p_sink = _softmax_with_sink(st, sink_ref[head], 0)
                        dpt = _dot(vz, dob, NT)
                        delta = jnp.sum(pt * dpt, axis=0, keepdims=True)
                        dst = pt * (dpt - delta)
                        dbias_ref[head] += dst
                        sacc[head:head + 1, :] += -(p_sink * delta)
                        probs.append(pt)
                        dscores.append(dst)
                    dk_acc = dk_acc + _dot(jnp.concatenate(dscores, axis=1).astype(BF16), split_pair(qs), NN)
                    dv_acc = dv_acc + _dot(jnp.concatenate(probs, axis=1).astype(BF16), split_pair(dob), NN)
                    dq_parts.append(_dot(jnp.concatenate(dscores, axis=0).astype(BF16), k_both, TN) * scale)
                dk_groups.append(dk_acc + pltpu.roll(dk_acc, HEAD_DIM, 1))
                dv_groups.append(dv_acc + pltpu.roll(dv_acc, HEAD_DIM, 1))
            dkv[band, :LANE] += jnp.where(low_key, dk_groups[0], dk_groups[1])
            dkv[band, LANE:] += jnp.where(low_key, dv_groups[0], dv_groups[1])
            dproj_ref[rows, 2 * A:2 * A + B] = jnp.concatenate(dq_parts, axis=1).astype(BF16)
        last = slice(tq, tq + CHUNK)
        dkv[last, :] += carry[...]
        dproj_ref[:, 2 * A + B:] = dkv[CHUNK:, :].astype(BF16)
        carry[...] = dkv[:CHUNK, :]

        @pl.when(step == n_tiles - 1)
        def _():
            dsink_ref[...] = jnp.sum(sacc[...], axis=1, keepdims=True)

    specs = _attn_specs(tq, A, B, reverse_tiles=n_tiles)
    return _CHAIN.call(
        body, name="attn_bwd", grid=(n_tiles,),
        in_specs=[pl.BlockSpec(memory_space=pltpu.SMEM)] + specs
        + [pl.BlockSpec((tq, B), lambda i: (rev(i), 0)), pl.BlockSpec((tq, 2 * A), lambda i: (rev(i), 0)),
           pl.BlockSpec((H, 2 * CHUNK, CHUNK), lambda i: (0, 0, 0))],
        out_specs=[pl.BlockSpec((tq, P), lambda i: (rev(i), 0)),
                   pl.BlockSpec((H, 2 * CHUNK, CHUNK), lambda i: (0, 0, 0)), pl.BlockSpec((H, 1), lambda i: (0, 0))],
        out_shape=[SDS((T, P), BF16), SDS((H, 2 * CHUNK, CHUNK), F32), SDS((H, 1), F32)],
        scratch_shapes=[pltpu.VMEM((CHUNK, 2 * LANE), F32), pltpu.VMEM((tq + CHUNK, 2 * LANE), F32),
                        pltpu.VMEM((H, LANE), F32)],
        compiler_params=_params(1))(sinks, proj, proj, proj, proj, proj, do, duv, bias_t)


def _bias_bwd(dbias, onehot):
    H = dbias.shape[0]
    nbk = onehot.shape[1]

    def body(d_ref, oh_ref, o_ref):
        hi, mid, lo = _split3(d_ref[...])
        oh = oh_ref[...]
        o_ref[...] = _dot(hi, oh, NN) + _dot(mid, oh, NN) + _dot(lo, oh, NN)

    return _CHAIN.call(body, name="bias_bwd", in_specs=[VMEM_SPEC] * 2, out_specs=VMEM_SPEC, out_shape=SDS((H, nbk), F32),
                       compiler_params=_params(0))(dbias, onehot)


def _inproj_bwd(dproj, w, x, dh1, g):
    T, P = dproj.shape
    D = x.shape[1]
    tm = _tile(T, 512)

    def body(dp_ref, w_ref, x_ref, dh_ref, g_ref, dx_ref, dg_ref):
        @pl.when(pl.program_id(0) == 0)
        def _():
            dg_ref[...] = jnp.zeros_like(dg_ref)

        dn = _dot(dp_ref[...], w_ref[...], NT)
        xv = x_ref[...]
        r = _rms_stats(xv)
        dg_ref[...] += jnp.sum(dn * (xv * r), axis=0, keepdims=True)
        dx_ref[...] = dh_ref[...] + _rms_bwd(dn, xv, r, g_ref[...])

    return _CHAIN.call(
        body, name="inproj_bwd", grid=(T // tm,),
        in_specs=[pl.BlockSpec((tm, P), lambda i: (i, 0)), _resident((D, P)),
                  pl.BlockSpec((tm, D), lambda i: (i, 0)), pl.BlockSpec((tm, D), lambda i: (i, 0)),
                  pl.BlockSpec((1, D), lambda i: (0, 0))],
        out_specs=[pl.BlockSpec((tm, D), lambda i: (i, 0)), pl.BlockSpec((1, D), lambda i: (0, 0))],
        out_shape=[SDS((T, D), F32), SDS((1, D), F32)], compiler_params=_params(1))(dproj, w, x, dh1, g)


def _adamw(w, g, m, v):
    m = ADAM_B1 * m + (1.0 - ADAM_B1) * g
    v = ADAM_B2 * v + (1.0 - ADAM_B2) * (g * g)
    m_hat = m / (1.0 - ADAM_B1 ** ADAM_STEP)
    v_hat = v / (1.0 - ADAM_B2 ** ADAM_STEP)
    delta = -ADAM_LR * (m_hat / (jnp.sqrt(v_hat) + ADAM_EPS) + ADAM_WD * w)
    return delta, m, v


def _adam_sharded(csum, recv, w, m, v, name):
    R, C = w.shape
    tr = _tile(R, 256, 16)

    def body(own_ref, recv_ref, w_ref, m_ref, v_ref, g_ref, d_ref, nm_ref, nv_ref):
        g = own_ref[...].astype(F32)
        for r in range(3):
            g = g + recv_ref[r].astype(F32)
        delta, nm, nv = _adamw(w_ref[...], g, m_ref[...], v_ref[...])
        g_ref[...] = g
        d_ref[...] = delta
        nm_ref[...] = nm
        nv_ref[...] = nv

    blk = pl.BlockSpec((tr, C), lambda i: (i, 0))
    return _CHAIN.call(
        body, name=name, grid=(R // tr,),
        in_specs=[pl.BlockSpec((None, tr, C), lambda i: (0, i, 0)), pl.BlockSpec((3, tr, C), lambda i: (0, i, 0)),
                  blk, blk, blk],
        out_specs=[blk] * 4, out_shape=[SDS((R, C), F32)] * 4, compiler_params=_params(1))(csum, recv, w, m, v)


def _adam_small(gathered, w, m, v):
    R = w.shape[0]

    def body(p_ref, w_ref, m_ref, v_ref, g_ref, d_ref, nm_ref, nv_ref):
        g = p_ref[0]
        for d in range(1, N_DEV):
            g = g + p_ref[d]
        delta, nm, nv = _adamw(w_ref[...], g, m_ref[...], v_ref[...])
        g_ref[...] = g
        d_ref[...] = delta
        nm_ref[...] = nm
        nv_ref[...] = nv

    return _CHAIN.call(body, name="adam_small", in_specs=[VMEM_SPEC] * 4, out_specs=[VMEM_SPEC] * 4,
                       out_shape=[SDS((R, LANE), F32)] * 4,
                       compiler_params=_params(0))(gathered, w, m, v)


def _pack(arrays):
    tile = 8 * LANE
    pieces = []
    for a in arrays:
        flat = a.reshape(-1).astype(F32)
        pieces.append(jnp.pad(flat, (0, (-flat.size) % tile)))
    return jnp.concatenate(pieces).reshape(-1, LANE)


def _unpack(packed, shapes):
    tile = 8 * LANE
    flat = packed.reshape(-1)
    out, off = [], 0
    for s in shapes:
        size = int(np.prod(s))
        out.append(flat[off:off + size].reshape(s))
        off += size + (-size) % tile
    return out


def kernel(x, rel_bias_table, mix_norm_g, w_in, gate_norm_g, gate_norm_b, w_spatial, b_spatial, attn_sinks, out_norm_a_g, out_norm_b_g, w_out, ffn_norm_g, w_up, w_down, final_norm_g, loss_target, m_rel_bias_table, m_mix_norm_g, m_w_in, m_gate_norm_g, m_gate_norm_b, m_w_spatial, m_b_spatial, m_attn_sinks, m_out_norm_a_g, m_out_norm_b_g, m_w_out, m_ffn_norm_g, m_w_up, m_w_down, m_final_norm_g, v_rel_bias_table, v_mix_norm_g, v_w_in, v_gate_norm_g, v_gate_norm_b, v_w_spatial, v_b_spatial, v_attn_sinks, v_out_norm_a_g, v_out_norm_b_g, v_w_out, v_ffn_norm_g, v_w_up, v_w_down, v_final_norm_g):
    T, D = x.shape[1], x.shape[2]
    A = D // 2
    B = D // 2
    G = A // GROUP_DIM
    H = B // HEAD_DIM
    P = 2 * A + B + 2 * KV_HEADS * HEAD_DIM
    Pb = w_in.shape[2]
    xs = x.reshape(T, D)
    target = loss_target.reshape(T, D)

    shards = [w_in[0].astype(BF16), w_out[0].astype(BF16), w_up[0].astype(BF16), w_down[0].astype(BF16)]
    _CHAIN.token = None
    gather = _gather_begin(shards, "gather_start")
    _gather_pass_on(gather, [0], "gather_in_pass")
    (win_g,) = _gather_end(gather, [0], "gather_in_end")
    win_full = jnp.transpose(win_g, (1, 0, 2)).reshape(D, P)

    g1, g2, g3 = mix_norm_g.reshape(1, D), ffn_norm_g.reshape(1, D), final_norm_g.reshape(1, D)
    lg, lb = gate_norm_g.reshape(1, A), gate_norm_b.reshape(1, A)
    ws = w_spatial[0]
    ws_t = jnp.swapaxes(ws, 1, 2)
    bs_t = jnp.transpose(b_spatial[0])
    ga, gb = out_norm_a_g.reshape(1, A), out_norm_b_g.reshape(1, B)
    sinks = attn_sinks.reshape(H)
    bucket, in_window = _t5_bucket()
    onehot_np = ((bucket[:, :, None] == np.arange(N_BUCKETS)) & in_window[:, :, None]).astype(np.float32)
    onehot = jnp.asarray(onehot_np.reshape(-1, N_BUCKETS)).astype(BF16)
    onehot_kq = jnp.asarray(onehot_np.transpose(1, 0, 2).reshape(-1, N_BUCKETS)).astype(BF16)

    bias, bias_t = _bias_fwd(jnp.transpose(rel_bias_table), jnp.transpose(onehot), jnp.transpose(onehot_kq))
    bias, bias_t = bias.reshape(H, CHUNK, 2 * CHUNK), bias_t.reshape(H, 2 * CHUNK, CHUNK)
    proj, n1 = _inproj_fwd(xs, g1, win_full)
    _gather_pass_on(gather, [1], "gather_out_pass")
    a_out = _gmlp_fwd(proj, lg, lb, ws, bs_t, A)
    b_out = _attn_fwd(proj, bias, sinks, A, B)
    _gather_pass_on(gather, [2], "gather_up_pass")
    (wout_g,) = _gather_end(gather, [1], "gather_out_end")
    wout_full = wout_g.reshape(A + B, D)
    h1, mixed = _outproj_fwd(a_out, b_out, ga, gb, xs, wout_full)
    _gather_pass_on(gather, [3], "gather_down_pass")
    (wup_g,) = _gather_end(gather, [2], "gather_up_end")
    wup_t = jnp.transpose(wup_g, (0, 2, 1)).reshape(-1, D)
    z, n2 = _ffn_up(h1, g2, wup_g)
    (wdown_g,) = _gather_end(gather, [3], "gather_down_end")
    h2 = _ffn_down(h1, z, wdown_g.reshape(-1, D))
    loss_part, dg3, dh2, dh2b = _final_loss(h2, g3, target)

    def reduce_to_chip(state, name):
        part, received = _sibling_exchange_end(state, name + "_sib_end")
        return _chip_exchange_begin(_chip_sum(part, received, name + "_chip_sum"), name + "_chip")

    dwdown = _matmul_tn(z, dh2b, "grad_w_down", square_a=True).reshape(wdown_g.shape)
    sib_down = _sibling_exchange_begin(dwdown, "rs_down_sib")
    dzp = _ffn_down_bwd(dh2b, z, wdown_g.reshape(-1, D))
    chip_down = reduce_to_chip(sib_down, "rs_down")
    dwup = _matmul_tn(n2, dzp, "grad_w_up", col_blocks=N_DEV)
    sib_up = _sibling_exchange_begin(dwup, "rs_up_sib")
    dh1, dh1b, dg2 = _ffn_norm_bwd(_ffn_up_bwd(dzp, wup_t), dh2, h1, g2)
    chip_up = reduce_to_chip(sib_up, "rs_up")
    da, db, dga, dgb = _outproj_bwd(dh1b, wout_full, a_out, b_out, ga, gb)
    dwout = _matmul_tn(mixed, dh1b, "grad_w_out").reshape(wout_g.shape)
    sib_out = _sibling_exchange_begin(dwout, "rs_out_sib")
    duv, dlg, dlb, dws, dbs_t = _gmlp_bwd(proj, da, lg, lb, ws, ws_t, bs_t, A)
    dproj, dbias_t, dsinks = _attn_bwd(proj, db, duv, bias_t, sinks, A, B)
    chip_out = reduce_to_chip(sib_out, "rs_out")
    dtable_t = _bias_bwd(dbias_t.reshape(H, -1), onehot_kq)
    dwin = _matmul_tn(n1, dproj, "grad_w_in")
    dwin = jnp.transpose(dwin.reshape(D, N_DEV, Pb), (1, 0, 2))
    sib_in = _sibling_exchange_begin(dwin, "rs_in_sib")
    grad_x, dg1 = _inproj_bwd(dproj, win_full, xs, dh1, g1)

    small_w = [rel_bias_table, mix_norm_g, gate_norm_g, gate_norm_b, w_spatial, b_spatial, attn_sinks,
               out_norm_a_g, out_norm_b_g, ffn_norm_g, final_norm_g]
    small_m = [m_rel_bias_table, m_mix_norm_g, m_gate_norm_g, m_gate_norm_b, m_w_spatial, m_b_spatial, m_attn_sinks,
               m_out_norm_a_g, m_out_norm_b_g, m_ffn_norm_g, m_final_norm_g]
    small_v = [v_rel_bias_table, v_mix_norm_g, v_gate_norm_g, v_gate_norm_b, v_w_spatial, v_b_spatial, v_attn_sinks,
               v_out_norm_a_g, v_out_norm_b_g, v_ffn_norm_g, v_final_norm_g]
    small_g = [jnp.transpose(dtable_t), dg1, dlg, dlb, dws, jnp.transpose(dbs_t), dsinks, dga, dgb, dg2, dg3]
    shapes = [w.shape for w in small_w]
    big = [None] * 4

    def adam_of(k, state, w, m, v):
        csum, received = _chip_exchange_end(state, "rs_%d_end" % k)
        big[k] = [o.reshape(w.shape) for o in _adam_sharded(csum, received, w[0], m[0], v[0], "adam_%d" % k)]

    small_gather = _gather_begin([_pack(small_g)], "small_gather_start")
    chip_in = reduce_to_chip(sib_in, "rs_in")
    _gather_pass_on(small_gather, [0], "small_gather_pass")
    adam_of(3, chip_down, w_down, m_w_down, v_w_down)
    (gathered,) = _gather_end(small_gather, [0], "small_gather_end")
    sg, sd, sm, sv = [_unpack(o, shapes) for o in _adam_small(gathered, _pack(small_w), _pack(small_m), _pack(small_v))]
    adam_of(2, chip_up, w_up, m_w_up, v_w_up)
    adam_of(1, chip_out, w_out, m_w_out, v_w_out)
    adam_of(0, chip_in, w_in, m_w_in, v_w_in)

    loss = lax.psum(loss_part[0, 0], ("x", "y", "c"))

    order = ["s0", "s1", "b0", "s2", "s3", "s4", "s5", "s6", "s7", "s8", "b1", "s9", "b2", "b3", "s10"]

    def group(idx):
        small = (sg, sd, sm, sv)[idx]
        return [small[int(t[1:])] if t[0] == "s" else big[int(t[1:])][idx] for t in order]

    return (loss, grad_x.reshape(x.shape), *group(0), *group(1), *group(2), *group(3))
```

```python
import functools
import math

import numpy as np
import jax
import jax.numpy as jnp
from jax import lax
from jax.experimental import pallas as pl
from jax.experimental.pallas import tpu as pltpu

F32 = jnp.float32
BF16 = jnp.bfloat16
SDS = jax.ShapeDtypeStruct
MESH = pl.DeviceIdType.MESH

N_DEV = 8
EPS = 1e-5
NEG = -1e30
CHUNK = 128
GROUP_DIM = 128
HEAD_DIM = 64
KV_HEADS = 2
N_BUCKETS = 32
MAX_DISTANCE = 128
ADAM_LR, ADAM_B1, ADAM_B2, ADAM_EPS, ADAM_WD, ADAM_STEP = 0.001, 0.9, 0.999, 1e-08, 0.01, 10
GELU_C0 = math.sqrt(2.0 / math.pi)
GELU_C1 = 0.044715

V7X_VMEM_BYTES = 64 * 1024 * 1024
VMEM_LIMIT = V7X_VMEM_BYTES - 8 * 1024 * 1024
LANE = 128

NN = ((1,), (0,))
NT = ((1,), (1,))
TN = ((0,), (0,))


def _dot(a, b, dims):
    return lax.dot_general(a, b, (dims, ((), ())), preferred_element_type=F32)


def _tile(n, pref, unit=LANE):
    best = None
    for t in range(unit, min(n, pref) + 1, unit):
        if n % t == 0:
            best = t
    return n if best is None else best


def _params(n_grid):
    return pltpu.CompilerParams(dimension_semantics=("arbitrary",) * n_grid, vmem_limit_bytes=VMEM_LIMIT)


def _resident(shape):
    return pl.BlockSpec(shape, lambda i: (0, 0), pipeline_mode=pl.Buffered(1))


def _gelu(x):
    return 0.5 * x * (1.0 + jnp.tanh(GELU_C0 * (x + GELU_C1 * x * x * x)))


def _gelu_and_grad(x):
    x2 = x * x
    t = jnp.tanh(GELU_C0 * x * (1.0 + GELU_C1 * x2))
    val = 0.5 * x * (1.0 + t)
    grad = 0.5 * (1.0 + t) + 0.5 * x * (1.0 - t * t) * (GELU_C0 * (1.0 + 3.0 * GELU_C1 * x2))
    return val, grad


def _rms_stats(x):
    return lax.rsqrt(jnp.mean(x * x, axis=-1, keepdims=True) + EPS)


def _rms_bwd(dy, x, r, g):
    w = dy * g
    return r * w - x * (r * r * r) * jnp.mean(w * x, axis=-1, keepdims=True)


def _t5_bucket():
    i = np.arange(CHUNK)[:, None]
    j = np.arange(2 * CHUNK)[None, :]
    rel = np.maximum(i + CHUNK - j, 0)
    n_exact = N_BUCKETS // 2
    relf = np.maximum(rel, n_exact).astype(np.float32)
    large = n_exact + (np.log(relf / np.float32(n_exact)) / np.float32(math.log(MAX_DISTANCE / n_exact))
                       * np.float32(N_BUCKETS - n_exact)).astype(np.int32)
    large = np.minimum(large, N_BUCKETS - 1)
    bucket = np.where(rel < n_exact, rel, large)
    in_window = (i + CHUNK - j >= 0) & (i + CHUNK - j < CHUNK)
    return bucket.astype(np.int32), in_window


def _split3(x):
    hi = x.astype(BF16)
    r1 = x - hi.astype(F32)
    mid = r1.astype(BF16)
    lo = (r1 - mid.astype(F32)).astype(BF16)
    return hi, mid, lo


HBM_SPEC = pl.BlockSpec(memory_space=pltpu.HBM)


def _mesh_pos():
    return lax.axis_index("x"), lax.axis_index("y"), lax.axis_index("c")


def _dev_index(px, py, pc):
    return 4 * px + 2 * py + pc


def _all_gather(shards, name):
    n = len(shards)

    def body(*refs):
        ins, outs = refs[:n], refs[n:2 * n]
        send_sems, recv_sems, local_sems = refs[2 * n:]
        x, y, c = _mesh_pos()
        me, sibling = (x, y, c), (x, y, 1 - c)
        chips = [(1 - x, y), (x, 1 - y), (1 - x, 1 - y)]

        def copy(a, k, block, to, src=None):
            dst = outs[a].at[_dev_index(*block)]
            return pltpu.make_async_remote_copy(
                src_ref=dst if src is None else src, dst_ref=dst,
                send_sem=send_sems.at[a * 7 + k], recv_sem=recv_sems.at[a * 7 + k],
                device_id=to, device_id_type=MESH)

        mine = [pltpu.make_async_copy(ins[a], outs[a].at[_dev_index(*me)], local_sems.at[a]) for a in range(n)]
        first = []
        for a in range(n):
            for j, chip in enumerate(chips):
                first.append(copy(a, 1 + j, me, (*chip, c), src=ins[a]))
            first.append(copy(a, 0, me, sibling, src=ins[a]))
        for cp in first:
            cp.start()
        for cp in mine:
            cp.start()
        passed = []
        for a in range(n):
            for j, chip in enumerate(chips):
                copy(a, 1 + j, (*chip, c), me).wait_recv()
                fwd = copy(a, 4 + j, (*chip, c), sibling)
                fwd.start()
                passed.append(fwd)
        for a in range(n):
            copy(a, 0, sibling, me).wait_recv()
            for j, chip in enumerate(chips):
                copy(a, 4 + j, (*chip, 1 - c), me).wait_recv()
        for cp in first + passed:
            cp.wait_send()
        for cp in mine:
            cp.wait()

    return _CHAIN.call(
        body, name=name,
        out_shape=[SDS((N_DEV,) + s.shape, s.dtype) for s in shards],
        in_specs=[HBM_SPEC] * n, out_specs=[HBM_SPEC] * n,
        scratch_shapes=[pltpu.SemaphoreType.DMA((7 * n,)), pltpu.SemaphoreType.DMA((7 * n,)),
                        pltpu.SemaphoreType.DMA((n,))],
    )(*shards)


SEM_SPEC = pl.BlockSpec(memory_space=pltpu.SEMAPHORE)
ANY_SPEC = pl.BlockSpec(memory_space=pl.ANY)
VMEM_SPEC = pl.BlockSpec(memory_space=pltpu.VMEM)
TOKEN_SPEC = VMEM_SPEC
TOKEN = SDS((8, LANE), F32)
SIDE_EFFECT = pltpu.SideEffectType.DATAFLOW_SIDE_EFFECTING


def _hbm(x):
    return pltpu.with_memory_space_constraint(x, pltpu.HBM)


class _CallChain:
    def __init__(self):
        self.token = None

    def call(self, body, *, in_specs, out_specs, out_shape, **kwargs):
        dep, n_in = self.token, len(in_specs)
        single = not isinstance(out_shape, (list, tuple))
        out_shapes = [out_shape] if single else list(out_shape)
        out_specs = [out_specs] if single else list(out_specs)
        n_out = len(out_shapes)
        n_dep = 0 if dep is None else 1
        token_spec = pl.BlockSpec((8, LANE), lambda *_: (0, 0)) if kwargs.get("grid") else VMEM_SPEC

        def chained(*refs):
            outs_at = n_in + n_dep
            body(*refs[:n_in], *refs[outs_at:outs_at + n_out], *refs[outs_at + n_out + 1:])
            token = refs[outs_at + n_out]
            token[...] = jnp.zeros_like(token)

        inner = pl.pallas_call(chained, in_specs=list(in_specs) + [ANY_SPEC] * n_dep, out_specs=out_specs + [token_spec],
                               out_shape=out_shapes + [TOKEN], **kwargs)

        def run(*operands):
            outs = inner(*operands) if dep is None else inner(*operands, dep)
            self.token = outs[n_out]
            return outs[0] if single else list(outs[:n_out])

        return run


_CHAIN = _CallChain()


def _split_start(bufs, copies_of, n_sems, name):
    n = len(bufs)

    def body(*refs):
        ins = refs[:n]
        send_sems, recv_sems = refs[n], refs[n + 1]
        for src, dst, k, target in copies_of(ins):
            pltpu.make_async_remote_copy(src_ref=src, dst_ref=dst, send_sem=send_sems.at[k], recv_sem=recv_sems.at[k],
                                         device_id=target, device_id_type=MESH).start()

    outs = _CHAIN.call(
        body, name=name,
        out_shape=[pltpu.SemaphoreType.DMA((n_sems,)), pltpu.SemaphoreType.DMA((n_sems,))]
        + [pltpu.HBM(b.shape, b.dtype) for b in bufs],
        in_specs=[HBM_SPEC] * n, out_specs=[SEM_SPEC, SEM_SPEC] + [HBM_SPEC] * n,
        input_output_aliases={a: 2 + a for a in range(n)},
        compiler_params=pltpu.CompilerParams(has_side_effects=SIDE_EFFECT),
    )(*[_hbm(b) for b in bufs])
    return outs[0], outs[1], list(outs[2:2 + n])


def _split_wait(bufs, sem_sets, waits_of, name):
    n, ns = len(bufs), len(sem_sets)
    flat_sems = [s for pair in sem_sets for s in pair]

    def body(*refs):
        ins = refs[:n]
        sems = refs[n:n + 2 * ns]
        x, y, c = _mesh_pos()
        for kind, src, dst, send_sem, recv_sem in waits_of(ins, [(sems[2 * i], sems[2 * i + 1]) for i in range(ns)]):
            cp = pltpu.make_async_remote_copy(src_ref=src, dst_ref=dst, send_sem=send_sem, recv_sem=recv_sem,
                                              device_id=(x, y, c), device_id_type=MESH)
            if kind == "send":
                cp.wait_send()
            else:
                cp.wait_recv()

    outs = _CHAIN.call(
        body, name=name,
        out_shape=[pltpu.HBM(b.shape, b.dtype) for b in bufs],
        in_specs=[HBM_SPEC] * n + [SEM_SPEC] * (2 * ns), out_specs=[HBM_SPEC] * n,
        input_output_aliases={a: a for a in range(n)},
        compiler_params=pltpu.CompilerParams(has_side_effects=SIDE_EFFECT),
    )(*bufs, *flat_sems)
    return list(outs)


def _gather_begin(shards, name):
    me = _dev_index(*_mesh_pos())
    lands = [lax.dynamic_update_index_in_dim(lax.empty((N_DEV,) + s.shape, s.dtype), s, me, 0) for s in shards]

    def copies_of(ins):
        x, y, c = _mesh_pos()
        targets = [(x, y, 1 - c), (1 - x, y, c), (x, 1 - y, c), (1 - x, 1 - y, c)]
        out = []
        for a, land in enumerate(ins):
            blk = land.at[_dev_index(x, y, c)]
            for k in (1, 2, 3, 0):
                out.append((blk, blk, 4 * a + k, targets[k]))
        return out

    send_sems, recv_sems, lands = _split_start(lands, copies_of, 4 * len(shards), name)
    return dict(lands=lands, sems=(send_sems, recv_sems), fwd={})


def _gather_pass_on(state, which, name):
    def arrivals(ins, sems):
        x, y, c = _mesh_pos()
        chips = [(1 - x, y), (x, 1 - y), (1 - x, 1 - y)]
        out = []
        for i, a in enumerate(which):
            for j, (px, py) in enumerate(chips):
                blk = ins[i].at[_dev_index(px, py, c)]
                out.append(("recv", blk, blk, sems[0][0].at[4 * a + 1 + j], sems[0][1].at[4 * a + 1 + j]))
        return out

    bufs = _split_wait([state["lands"][a] for a in which], [state["sems"]], arrivals, name + "_arrived")

    def copies_of(ins):
        x, y, c = _mesh_pos()
        chips = [(1 - x, y), (x, 1 - y), (1 - x, 1 - y)]
        out = []
        for i in range(len(which)):
            for j, (px, py) in enumerate(chips):
                blk = ins[i].at[_dev_index(px, py, c)]
                out.append((blk, blk, 3 * i + j, (x, y, 1 - c)))
        return out

    send_sems, recv_sems, bufs = _split_start(bufs, copies_of, 3 * len(which), name)
    for i, a in enumerate(which):
        state["lands"][a] = bufs[i]
    state["fwd"][tuple(which)] = (send_sems, recv_sems)


def _gather_end(state, which, name):
    def waits(ins, sems):
        x, y, c = _mesh_pos()
        chips = [(1 - x, y), (x, 1 - y), (1 - x, 1 - y)]
        (s_send, s_recv), (f_send, f_recv) = sems
        out = []
        for i, a in enumerate(which):
            mine = ins[i].at[_dev_index(x, y, c)]
            sib = ins[i].at[_dev_index(x, y, 1 - c)]
            out.append(("recv", sib, sib, s_send.at[4 * a], s_recv.at[4 * a]))
            for j, (px, py) in enumerate(chips):
                theirs = ins[i].at[_dev_index(px, py, 1 - c)]
                out.append(("recv", theirs, theirs, f_send.at[3 * i + j], f_recv.at[3 * i + j]))
            for k in range(4):
                out.append(("send", mine, mine, s_send.at[4 * a + k], s_recv.at[4 * a + k]))
            for j, (px, py) in enumerate(chips):
                passed = ins[i].at[_dev_index(px, py, c)]
                out.append(("send", passed, passed, f_send.at[3 * i + j], f_recv.at[3 * i + j]))
        return out

    bufs = _split_wait([state["lands"][a] for a in which], [state["sems"], state["fwd"][tuple(which)]], waits, name)
    for i, a in enumerate(which):
        state["lands"][a] = bufs[i]
    return bufs


def _sibling_exchange_begin(part, name):
    land = lax.empty((4,) + part.shape[1:], part.dtype)

    def copies_of(ins):
        x, y, c = _mesh_pos()
        return [(ins[0].at[2 * j + (1 - c)], ins[1].at[j], j, (x, y, 1 - c)) for j in range(4)]

    send_sems, recv_sems, bufs = _split_start([part, land], copies_of, 4, name)
    return dict(bufs=bufs, sems=(send_sems, recv_sems))


def _sibling_exchange_end(state, name):
    def waits(ins, sems):
        _, _, c = _mesh_pos()
        out = []
        for j in range(4):
            for kind in ("send", "recv"):
                out.append((kind, ins[0].at[2 * j + (1 - c)], ins[1].at[j], sems[0][0].at[j], sems[0][1].at[j]))
        return out

    return _split_wait(state["bufs"], [state["sems"]], waits, name)


CHIP_FLIPS = (2, 1, 3)


def _chip_exchange_begin(csum, name):
    land = lax.empty((3,) + csum.shape[1:], csum.dtype)

    def copies_of(ins):
        x, y, c = _mesh_pos()
        chips = [(1 - x, y), (x, 1 - y), (1 - x, 1 - y)]
        return [(ins[0].at[CHIP_FLIPS[r]], ins[1].at[r], r, (px, py, c)) for r, (px, py) in enumerate(chips)]

    send_sems, recv_sems, bufs = _split_start([csum, land], copies_of, 3, name)
    return dict(bufs=bufs, sems=(send_sems, recv_sems))


def _chip_exchange_end(state, name):
    def waits(ins, sems):
        out = []
        for r in range(3):
            for kind in ("send", "recv"):
                out.append((kind, ins[0].at[CHIP_FLIPS[r]], ins[1].at[r], sems[0][0].at[r], sems[0][1].at[r]))
        return out

    return _split_wait(state["bufs"], [state["sems"]], waits, name)


def _chip_sum(part, recv, name):
    _, R, C = part.shape
    tr = _tile(R, 512, 16)
    place = jnp.stack([lax.axis_index("c"), 2 * lax.axis_index("x") + lax.axis_index("y")]).astype(jnp.int32)

    def body(place_ref, p_ref, r_ref, o_ref):
        o_ref[...] = (p_ref[...].astype(F32) + r_ref[...].astype(F32)).astype(o_ref.dtype)

    def chip(p, place_ref):
        return jnp.bitwise_xor(p, place_ref[1])

    grid_spec = pltpu.PrefetchScalarGridSpec(
        num_scalar_prefetch=1, grid=(4, R // tr),
        in_specs=[pl.BlockSpec((None, tr, C), lambda p, i, place_ref: (2 * chip(p, place_ref) + place_ref[0], i, 0)),
                  pl.BlockSpec((None, tr, C), lambda p, i, place_ref: (chip(p, place_ref), i, 0))],
        out_specs=pl.BlockSpec((None, tr, C), lambda p, i, place_ref: (p, i, 0)))
    return pl.pallas_call(body, name=name, grid_spec=grid_spec, out_shape=SDS((4, R, C), part.dtype),
                          compiler_params=_params(2))(place, part, recv)


def _bias_fwd(table_t, onehot_t, onehot_kq_t):
    H = table_t.shape[0]
    n = onehot_t.shape[1]

    def body(t_ref, oh_ref, oh_kq_ref, o_ref, o_kq_ref):
        hi, mid, lo = _split3(t_ref[...])
        for src, dst in ((oh_ref, o_ref), (oh_kq_ref, o_kq_ref)):
            oh = src[...]
            dst[...] = _dot(hi, oh, NN) + _dot(mid, oh, NN) + _dot(lo, oh, NN)

    return _CHAIN.call(body, name="bias_fwd", in_specs=[VMEM_SPEC] * 3, out_specs=[VMEM_SPEC] * 2,
                       out_shape=[SDS((H, n), F32)] * 2, compiler_params=_params(0))(table_t, onehot_t, onehot_kq_t)


def _inproj_fwd(x, g, w_t):
    T, D = x.shape
    P = w_t.shape[0]
    tm = _tile(T, 512)

    def body(x_ref, g_ref, w_ref, proj_ref, n_ref):
        xv = x_ref[...]
        n = (xv * _rms_stats(xv) * g_ref[...]).astype(BF16)
        n_ref[...] = n
        proj_ref[...] = _dot(n, w_ref[...], NT)

    return _CHAIN.call(
        body, name="inproj_fwd", grid=(T // tm,),
        in_specs=[pl.BlockSpec((tm, D), lambda i: (i, 0)), pl.BlockSpec((1, D), lambda i: (0, 0)), _resident((P, D))],
        out_specs=[pl.BlockSpec((tm, P), lambda i: (i, 0)), pl.BlockSpec((tm, D), lambda i: (i, 0))],
        out_shape=[SDS((T, P), F32), SDS((T, D), BF16)], compiler_params=_params(1))(x, g, w_t)


def _layer_norm_group(vg, lg, lb):
    mu = jnp.mean(vg, axis=-1, keepdims=True)
    xc = vg - mu
    rstd = lax.rsqrt(jnp.mean(xc * xc, axis=-1, keepdims=True) + EPS)
    vhat = xc * rstd
    return vhat, rstd, vhat * lg + lb


def _gmlp_fwd(proj, lg, lb, w_s, bs_t, A):
    T = proj.shape[0]
    G = A // GROUP_DIM
    tm = _tile(T, 512)
    nc = tm // CHUNK

    def body(u_ref, v_ref, lg_ref, lb_ref, w_ref, bst_ref, a_ref):
        row = lax.broadcasted_iota(jnp.int32, (CHUNK, CHUNK), 0)
        col = lax.broadcasted_iota(jnp.int32, (CHUNK, CHUNK), 1)
        causal = row >= col
        for g in range(G):
            sl = slice(g * GROUP_DIM, (g + 1) * GROUP_DIM)
            _, _, vn = _layer_norm_group(_gelu(v_ref[:, sl]), lg_ref[:, sl], lb_ref[:, sl])
            vnb = vn.astype(BF16)
            wm = jnp.where(causal, w_ref[g], 0.0).astype(BF16)
            ug = _gelu(u_ref[:, sl])
            bcol = bst_ref[:, g:g + 1]
            for c in range(nc):
                rs = slice(c * CHUNK, (c + 1) * CHUNK)
                a_ref[rs, sl] = ug[rs] * (_dot(wm, vnb[rs], NN) + bcol)

    return _CHAIN.call(
        body, name="gmlp_fwd", grid=(T // tm,),
        in_specs=[pl.BlockSpec((tm, A), lambda i: (i, 0)), pl.BlockSpec((tm, A), lambda i: (i, 1)),
                  pl.BlockSpec((1, A), lambda i: (0, 0)), pl.BlockSpec((1, A), lambda i: (0, 0)),
                  pl.BlockSpec((G, CHUNK, CHUNK), lambda i: (0, 0, 0)), pl.BlockSpec((CHUNK, G), lambda i: (0, 0))],
        out_specs=pl.BlockSpec((tm, A), lambda i: (i, 0)),
        out_shape=SDS((T, A), F32), compiler_params=_params(1))(proj, proj, lg, lb, w_s, bs_t)


def _attn_masks(first_tile):
    ii = lax.broadcasted_iota(jnp.int32, (CHUNK, 2 * CHUNK), 0)
    jj = lax.broadcasted_iota(jnp.int32, (CHUNK, 2 * CHUNK), 1)
    in_window = (jj > ii) & (jj <= ii + CHUNK)
    first_mask = in_window & jnp.logical_or(jnp.logical_not(first_tile), jj >= CHUNK)
    return in_window, first_mask


def _softmax_with_sink(s, sink, axis):
    m = jnp.maximum(jnp.max(s, axis=axis, keepdims=True), sink)
    p = jnp.exp(s - m)
    e_sink = jnp.exp(sink - m)
    inv = 1.0 / (jnp.sum(p, axis=axis, keepdims=True) + e_sink)
    return p * inv, e_sink * inv


def _pad_heads(band, group):
    lane = lax.broadcasted_iota(jnp.int32, band.shape, 1)
    if group == 0:
        low = jnp.where(lane < HEAD_DIM, band, 0.0)
        high = pltpu.roll(low, HEAD_DIM, 1)
    else:
        high = jnp.where(lane >= HEAD_DIM, band, 0.0)
        low = pltpu.roll(high, HEAD_DIM, 1)
    return low.astype(BF16), high.astype(BF16)


def _attn_specs(tq, A, B, reverse_tiles=None):
    nb = tq // CHUNK
    kcol = (2 * A + B) // LANE
    if reverse_tiles is None:
        tile = lambda i: i
    else:
        tile = lambda i: reverse_tiles - 1 - i
    prev = lambda i: jnp.maximum(tile(i) * nb - 1, 0)
    return [pl.BlockSpec((tq, B), lambda i: (tile(i), 2 * A // B)),
            pl.BlockSpec((tq, LANE), lambda i: (tile(i), kcol)),
            pl.BlockSpec((tq, LANE), lambda i: (tile(i), kcol + 1)),
            pl.BlockSpec((CHUNK, LANE), lambda i: (prev(i), kcol)),
            pl.BlockSpec((CHUNK, LANE), lambda i: (prev(i), kcol + 1))]


def _attn_fwd(proj, bias, sinks, A, B):
    T = proj.shape[0]
    H = B // HEAD_DIM
    qpk = H // KV_HEADS
    tq = _tile(T, 512)
    nb = tq // CHUNK

    scale = HEAD_DIM ** -0.5

    def body(sink_ref, q_ref, k_ref, v_ref, kp_ref, vp_ref, bias_ref, o_ref):
        in_window, first_mask = _attn_masks(pl.program_id(0) == 0)
        for b in range(nb):
            rows = slice(b * CHUNK, (b + 1) * CHUNK)
            if b == 0:
                kprev, vprev, mask = kp_ref[...], vp_ref[...], first_mask
            else:
                prows = slice((b - 1) * CHUNK, b * CHUNK)
                kprev, vprev, mask = k_ref[prows, :], v_ref[prows, :], in_window
            kband = jnp.concatenate([kprev, k_ref[rows, :]], axis=0)
            vband = jnp.concatenate([vprev, v_ref[rows, :]], axis=0)
            outs = []
            for g in range(KV_HEADS):
                k_low, k_high = _pad_heads(kband, g)
                v_both = jnp.concatenate(_pad_heads(vband, g), axis=0)
                for pair in range(qpk // 2):
                    h = g * qpk + 2 * pair
                    qs = (q_ref[rows, h * HEAD_DIM:(h + 2) * HEAD_DIM] * scale).astype(BF16)
                    probs = []
                    for head, kz in ((h, k_low), (h + 1, k_high)):
                        s = jnp.where(mask, _dot(qs, kz, NT) + bias_ref[head], NEG)
                        probs.append(_softmax_with_sink(s, sink_ref[head], -1)[0])
                    outs.append(_dot(jnp.concatenate(probs, axis=1).astype(BF16), v_both, NN))
            o_ref[rows, :] = jnp.concatenate(outs, axis=1)

    return _CHAIN.call(
        body, name="attn_fwd", grid=(T // tq,),
        in_specs=[pl.BlockSpec(memory_space=pltpu.SMEM)] + _attn_specs(tq, A, B)
        + [pl.BlockSpec((H, CHUNK, 2 * CHUNK), lambda i: (0, 0, 0))],
        out_specs=pl.BlockSpec((tq, B), lambda i: (i, 0)),
        out_shape=SDS((T, B), F32), compiler_params=_params(1))(sinks, proj, proj, proj, proj, proj, bias)


def _outproj_fwd(a, b, ga, gb, x, w):
    T, A = a.shape
    B = b.shape[1]
    D = x.shape[1]
    tm = _tile(T, 512)

    def body(a_ref, b_ref, ga_ref, gb_ref, x_ref, w_ref, h_ref, mix_ref):
        av, bv = a_ref[...], b_ref[...]
        mix_ref[:, :A] = (av * _rms_stats(av) * ga_ref[...]).astype(BF16)
        mix_ref[:, A:] = (bv * _rms_stats(bv) * gb_ref[...]).astype(BF16)
        h_ref[...] = x_ref[...] + _dot(mix_ref[...], w_ref[...], NN)

    return _CHAIN.call(
        body, name="outproj_fwd", grid=(T // tm,),
        in_specs=[pl.BlockSpec((tm, A), lambda i: (i, 0)), pl.BlockSpec((tm, B), lambda i: (i, 0)),
                  pl.BlockSpec((1, A), lambda i: (0, 0)), pl.BlockSpec((1, B), lambda i: (0, 0)),
                  pl.BlockSpec((tm, D), lambda i: (i, 0)), _resident((A + B, D))],
        out_specs=[pl.BlockSpec((tm, D), lambda i: (i, 0)), pl.BlockSpec((tm, A + B), lambda i: (i, 0))],
        out_shape=[SDS((T, D), F32), SDS((T, A + B), BF16)], compiler_params=_params(1))(a, b, ga, gb, x, w)


def _ffn_up(h1, g, w_up):
    T, D = h1.shape
    Fb = w_up.shape[2]
    F = N_DEV * Fb
    tm, tf = _tile(T, 1024), _tile(Fb, 1024)
    per = Fb // tf

    def body(h_ref, g_ref, wu_ref, z_ref, n_ref, nbuf):
        @pl.when(pl.program_id(1) == 0)
        def _():
            hv = h_ref[...]
            n = (hv * _rms_stats(hv) * g_ref[...]).astype(BF16)
            nbuf[...] = n
            n_ref[...] = n

        z_ref[...] = jnp.maximum(_dot(nbuf[...], wu_ref[...], NN), 0.0).astype(BF16)

    return _CHAIN.call(
        body, name="ffn_up", grid=(T // tm, F // tf),
        in_specs=[pl.BlockSpec((tm, D), lambda i, j: (i, 0)), pl.BlockSpec((1, D), lambda i, j: (0, 0)),
                  pl.BlockSpec((None, D, tf), lambda i, j: (j // per, 0, j % per))],
        out_specs=[pl.BlockSpec((tm, tf), lambda i, j: (i, j)), pl.BlockSpec((tm, D), lambda i, j: (i, 0))],
        out_shape=[SDS((T, F), BF16), SDS((T, D), BF16)],
        scratch_shapes=[pltpu.VMEM((tm, D), BF16)], compiler_params=_params(2))(h1, g, w_up)


def _ffn_down(h1, z, w_down):
    T, D = h1.shape
    F = w_down.shape[0]
    tm, tn, tk = _tile(T, 1024), _tile(D, 1024), _tile(F, 4096)

    def body(h_ref, z_ref, wd_ref, h2_ref):
        k = pl.program_id(2)

        @pl.when(k == 0)
        def _():
            h2_ref[...] = h_ref[...]

        zf = z_ref[...].astype(F32)
        h2_ref[...] += _dot((zf * zf).astype(BF16), wd_ref[...], NN)

    return _CHAIN.call(
        body, name="ffn_down", grid=(T // tm, D // tn, F // tk),
        in_specs=[pl.BlockSpec((tm, tn), lambda i, j, k: (i, j)), pl.BlockSpec((tm, tk), lambda i, j, k: (i, k)),
                  pl.BlockSpec((tk, tn), lambda i, j, k: (k, j))],
        out_specs=pl.BlockSpec((tm, tn), lambda i, j, k: (i, j)),
        out_shape=SDS((T, D), F32), compiler_params=_params(3))(h1, z, w_down)


def _final_loss(h2, g, target):
    T, D = h2.shape
    tm = _tile(T, 512)

    def body(h_ref, g_ref, t_ref, loss_ref, dg_ref, dh_ref, dhb_ref):
        @pl.when(pl.program_id(0) == 0)
        def _():
            loss_ref[...] = jnp.zeros_like(loss_ref)
            dg_ref[...] = jnp.zeros_like(dg_ref)

        hv, gv = h_ref[...], g_ref[...]
        r = _rms_stats(hv)
        hn = hv * r
        e = hn * gv - t_ref[...]
        loss_ref[...] += (0.5 / D) * jnp.sum(jnp.sum(e * e, axis=0, keepdims=True), axis=-1, keepdims=True)
        dy = e * (1.0 / D)
        dg_ref[...] += jnp.sum(dy * hn, axis=0, keepdims=True)
        dh = _rms_bwd(dy, hv, r, gv)
        dh_ref[...] = dh
        dhb_ref[...] = dh.astype(BF16)

    return _CHAIN.call(
        body, name="final_loss", grid=(T // tm,),
        in_specs=[pl.BlockSpec((tm, D), lambda i: (i, 0)), pl.BlockSpec((1, D), lambda i: (0, 0)),
                  pl.BlockSpec((tm, D), lambda i: (i, 0))],
        out_specs=[pl.BlockSpec((1, 1), lambda i: (0, 0)), pl.BlockSpec((1, D), lambda i: (0, 0)),
                   pl.BlockSpec((tm, D), lambda i: (i, 0)), pl.BlockSpec((tm, D), lambda i: (i, 0))],
        out_shape=[SDS((1, 1), F32), SDS((1, D), F32), SDS((T, D), F32), SDS((T, D), BF16)],
        compiler_params=_params(1))(h2, g, target)


def _ffn_down_bwd(dh2b, z, w_down):
    T, D = dh2b.shape
    F = w_down.shape[0]
    tm, tf = _tile(T, 1024), _tile(F, 1024)

    def body(dh_ref, z_ref, wd_ref, dzp_ref):
        dzz = _dot(dh_ref[...], wd_ref[...], NT)
        dzp_ref[...] = (dzz * (2.0 * z_ref[...].astype(F32))).astype(BF16)

    return _CHAIN.call(
        body, name="ffn_down_bwd", grid=(T // tm, F // tf),
        in_specs=[pl.BlockSpec((tm, D), lambda i, j: (i, 0)), pl.BlockSpec((tm, tf), lambda i, j: (i, j)),
                  pl.BlockSpec((tf, D), lambda i, j: (j, 0))],
        out_specs=pl.BlockSpec((tm, tf), lambda i, j: (i, j)),
        out_shape=SDS((T, F), BF16), compiler_params=_params(2))(dh2b, z, w_down)


def _ffn_up_bwd(dzp, w_up_t):
    T, F = dzp.shape
    D = w_up_t.shape[1]
    tm, tn, tk = _tile(T, 1024), _tile(D, 1024), _tile(F, 4096)

    def body(dzp_ref, w_ref, dn_ref):
        part = _dot(dzp_ref[...], w_ref[...], NN)

        @pl.when(pl.program_id(2) == 0)
        def _():
            dn_ref[...] = part

        @pl.when(pl.program_id(2) > 0)
        def _():
            dn_ref[...] += part

    return _CHAIN.call(
        body, name="ffn_up_bwd", grid=(T // tm, D // tn, F // tk),
        in_specs=[pl.BlockSpec((tm, tk), lambda i, j, k: (i, k)), pl.BlockSpec((tk, tn), lambda i, j, k: (k, j))],
        out_specs=pl.BlockSpec((tm, tn), lambda i, j, k: (i, j)),
        out_shape=SDS((T, D), F32), compiler_params=_params(3))(dzp, w_up_t)


def _ffn_norm_bwd(dn, dh2, h1, g):
    T, D = h1.shape
    tm = _tile(T, 256)

    def body(dn_ref, dh_ref, h_ref, g_ref, dh1_ref, dh1b_ref, dg_ref):
        @pl.when(pl.program_id(0) == 0)
        def _():
            dg_ref[...] = jnp.zeros_like(dg_ref)

        hv, dnv = h_ref[...], dn_ref[...]
        r = _rms_stats(hv)
        dg_ref[...] += jnp.sum(dnv * (hv * r), axis=0, keepdims=True)
        dh1 = dh_ref[...] + _rms_bwd(dnv, hv, r, g_ref[...])
        dh1_ref[...] = dh1
        dh1b_ref[...] = dh1.astype(BF16)

    row = pl.BlockSpec((tm, D), lambda i: (i, 0))
    vec = pl.BlockSpec((1, D), lambda i: (0, 0))
    return _CHAIN.call(
        body, name="ffn_norm_bwd", grid=(T // tm,), in_specs=[row, row, row, vec], out_specs=[row, row, vec],
        out_shape=[SDS((T, D), F32), SDS((T, D), BF16), SDS((1, D), F32)], compiler_params=_params(1))(dn, dh2, h1, g)


def _matmul_tn(a, b, name, square_a=False, col_blocks=None):
    T, K = a.shape
    N = b.shape[1]
    tk = _tile(K, 1792)
    tn = _tile(N if col_blocks is None else N // col_blocks, 1024 if tk <= 1024 else 512)

    def body(a_ref, b_ref, o_ref):
        av = a_ref[...]
        if square_a:
            af = av.astype(F32)
            av = (af * af).astype(BF16)
        o_ref[...] = _dot(av, b_ref[...], TN).astype(o_ref.dtype)

    if col_blocks is None:
        out_shape = SDS((K, N), BF16)
        out_spec = pl.BlockSpec((tk, tn), lambda i, j: (i, j))
    else:
        per = (N // col_blocks) // tn
        out_shape = SDS((col_blocks, K, N // col_blocks), BF16)
        out_spec = pl.BlockSpec((None, tk, tn), lambda i, j: (j // per, i, j % per))
    return _CHAIN.call(
        body, name=name, grid=(K // tk, N // tn),
        in_specs=[pl.BlockSpec((T, tk), lambda i, j: (0, i)), pl.BlockSpec((T, tn), lambda i, j: (0, j))],
        out_specs=out_spec, out_shape=out_shape, compiler_params=_params(2))(a, b)


def _outproj_bwd(dh1b, w, a, b, ga, gb):
    T, D = dh1b.shape
    A, B = a.shape[1], b.shape[1]
    tm = _tile(T, 512)

    def body(dh_ref, w_ref, a_ref, b_ref, ga_ref, gb_ref, da_ref, db_ref, dga_ref, dgb_ref):
        @pl.when(pl.program_id(0) == 0)
        def _():
            dga_ref[...] = jnp.zeros_like(dga_ref)
            dgb_ref[...] = jnp.zeros_like(dgb_ref)

        dmix = _dot(dh_ref[...], w_ref[...], NT)
        for src_ref, g_ref, dx_ref, dg_ref, dn in ((a_ref, ga_ref, da_ref, dga_ref, dmix[:, :A]),
                                                   (b_ref, gb_ref, db_ref, dgb_ref, dmix[:, A:])):
            xv = src_ref[...]
            r = _rms_stats(xv)
            dg_ref[...] += jnp.sum(dn * (xv * r), axis=0, keepdims=True)
            dx_ref[...] = _rms_bwd(dn, xv, r, g_ref[...])

    return _CHAIN.call(
        body, name="outproj_bwd", grid=(T // tm,),
        in_specs=[pl.BlockSpec((tm, D), lambda i: (i, 0)), _resident((A + B, D)),
                  pl.BlockSpec((tm, A), lambda i: (i, 0)), pl.BlockSpec((tm, B), lambda i: (i, 0)),
                  pl.BlockSpec((1, A), lambda i: (0, 0)), pl.BlockSpec((1, B), lambda i: (0, 0))],
        out_specs=[pl.BlockSpec((tm, A), lambda i: (i, 0)), pl.BlockSpec((tm, B), lambda i: (i, 0)),
                   pl.BlockSpec((1, A), lambda i: (0, 0)), pl.BlockSpec((1, B), lambda i: (0, 0))],
        out_shape=[SDS((T, A), F32), SDS((T, B), F32), SDS((1, A), F32), SDS((1, B), F32)],
        compiler_params=_params(1))(dh1b, w, a, b, ga, gb)


def _gmlp_bwd(proj, da, lg, lb, w_s, w_st, bs_t, A):
    T = proj.shape[0]
    G = A // GROUP_DIM
    tm = _tile(T, 512)
    nc = tm // CHUNK

    def body(u_ref, v_ref, da_ref, lg_ref, lb_ref, w_ref, wt_ref, bst_ref, duv_ref, dlg_ref, dlb_ref, dw_ref, dbs_ref):
        @pl.when(pl.program_id(0) == 0)
        def _():
            dlg_ref[...] = jnp.zeros_like(dlg_ref)
            dlb_ref[...] = jnp.zeros_like(dlb_ref)
            dw_ref[...] = jnp.zeros_like(dw_ref)
            dbs_ref[...] = jnp.zeros_like(dbs_ref)

        row = lax.broadcasted_iota(jnp.int32, (CHUNK, CHUNK), 0)
        col = lax.broadcasted_iota(jnp.int32, (CHUNK, CHUNK), 1)
        lower = row >= col
        upper = row <= col
        for g in range(G):
            sl = slice(g * GROUP_DIM, (g + 1) * GROUP_DIM)
            lgv = lg_ref[:, sl]
            vg, vg_grad = _gelu_and_grad(v_ref[:, sl])
            vhat, rstd, vn = _layer_norm_group(vg, lgv, lb_ref[:, sl])
            vnb = vn.astype(BF16)
            ug, ug_grad = _gelu_and_grad(u_ref[:, sl])
            dav = da_ref[:, sl]
            wm = jnp.where(lower, w_ref[g], 0.0).astype(BF16)
            wmt = jnp.where(upper, wt_ref[g], 0.0).astype(BF16)
            bcol = bst_ref[:, g:g + 1]
            dw_acc = jnp.zeros((CHUNK, CHUNK), F32)
            dbs_acc = jnp.zeros((CHUNK, 1), F32)
            dvn_parts = []
            dug_parts = []
            for c in range(nc):
                rs = slice(c * CHUNK, (c + 1) * CHUNK)
                mixed = _dot(wm, vnb[rs], NN) + bcol
                dug_parts.append(dav[rs] * mixed)
                dmix = dav[rs] * ug[rs]
                dbs_acc = dbs_acc + jnp.sum(dmix, axis=-1, keepdims=True)
                dmixb = dmix.astype(BF16)
                dw_acc = dw_acc + _dot(dmixb, vnb[rs], NT)
                dvn_parts.append(_dot(wmt, dmixb, NN))
            dvn = jnp.concatenate(dvn_parts, axis=0)
            dug = jnp.concatenate(dug_parts, axis=0)
            dw_ref[g] += jnp.where(lower, dw_acc, 0.0)
            dbs_ref[:, g:g + 1] += dbs_acc
            dlg_ref[:, sl] += jnp.sum(dvn * vhat, axis=0, keepdims=True)
            dlb_ref[:, sl] += jnp.sum(dvn, axis=0, keepdims=True)
            dvhat = dvn * lgv
            dvg = rstd * (dvhat - jnp.mean(dvhat, axis=-1, keepdims=True)
                          - vhat * jnp.mean(dvhat * vhat, axis=-1, keepdims=True))
            duv_ref[:, sl] = (dug * ug_grad).astype(BF16)
            duv_ref[:, A + g * GROUP_DIM:A + (g + 1) * GROUP_DIM] = (dvg * vg_grad).astype(BF16)

    return _CHAIN.call(
        body, name="gmlp_bwd", grid=(T // tm,),
        in_specs=[pl.BlockSpec((tm, A), lambda i: (i, 0)), pl.BlockSpec((tm, A), lambda i: (i, 1)),
                  pl.BlockSpec((tm, A), lambda i: (i, 0)),
                  pl.BlockSpec((1, A), lambda i: (0, 0)), pl.BlockSpec((1, A), lambda i: (0, 0)),
                  pl.BlockSpec((G, CHUNK, CHUNK), lambda i: (0, 0, 0)),
                  pl.BlockSpec((G, CHUNK, CHUNK), lambda i: (0, 0, 0)), pl.BlockSpec((CHUNK, G), lambda i: (0, 0))],
        out_specs=[pl.BlockSpec((tm, 2 * A), lambda i: (i, 0)),
                   pl.BlockSpec((1, A), lambda i: (0, 0)), pl.BlockSpec((1, A), lambda i: (0, 0)),
                   pl.BlockSpec((G, CHUNK, CHUNK), lambda i: (0, 0, 0)), pl.BlockSpec((CHUNK, G), lambda i: (0, 0))],
        out_shape=[SDS((T, 2 * A), BF16), SDS((1, A), F32), SDS((1, A), F32),
                   SDS((G, CHUNK, CHUNK), F32), SDS((CHUNK, G), F32)],
        compiler_params=_params(1))(proj, proj, da, lg, lb, w_s, w_st, bs_t)


def _attn_bwd(proj, do, duv, bias_t, sinks, A, B):
    T, P = proj.shape
    H = B // HEAD_DIM
    qpk = H // KV_HEADS
    tq = _tile(T, 512)
    nb = tq // CHUNK
    n_tiles = T // tq
    scale = HEAD_DIM ** -0.5
    rev = lambda i: n_tiles - 1 - i

    def body(sink_ref, q_ref, k_ref, v_ref, kp_ref, vp_ref, do_ref, duv_ref, bias_ref,
             dproj_ref, dbias_ref, dsink_ref, carry, dkv, sacc):
        step = pl.program_id(0)

        @pl.when(step == 0)
        def _():
            carry[...] = jnp.zeros_like(carry)
            sacc[...] = jnp.zeros_like(sacc)
            dbias_ref[...] = jnp.zeros_like(dbias_ref)

        jj = lax.broadcasted_iota(jnp.int32, (2 * CHUNK, CHUNK), 0)
        ii = lax.broadcasted_iota(jnp.int32, (2 * CHUNK, CHUNK), 1)
        in_window = (jj > ii) & (jj <= ii + CHUNK)
        first_mask = in_window & jnp.logical_or(step != n_tiles - 1, jj >= CHUNK)
        low_query = lax.broadcasted_iota(jnp.int32, (CHUNK, LANE), 1) < HEAD_DIM
        low_key = lax.broadcasted_iota(jnp.int32, (2 * CHUNK, LANE), 1) < HEAD_DIM

        def split_pair(pair_bf16):
            zero = jnp.zeros_like(pair_bf16)
            return jnp.concatenate([jnp.where(low_query, pair_bf16, zero), jnp.where(low_query, zero, pair_bf16)], axis=0)

        dproj_ref[:, :2 * A] = duv_ref[...]
        dkv[...] = jnp.zeros_like(dkv)
        for b in range(nb):
            rows = slice(b * CHUNK, (b + 1) * CHUNK)
            band = slice(b * CHUNK, (b + 2) * CHUNK)
            if b == 0:
                kprev, vprev, mask = kp_ref[...], vp_ref[...], first_mask
            else:
                prows = slice((b - 1) * CHUNK, b * CHUNK)
                kprev, vprev, mask = k_ref[prows, :], v_ref[prows, :], in_window
            kband = jnp.concatenate([kprev, k_ref[rows, :]], axis=0)
            vband = jnp.concatenate([vprev, v_ref[rows, :]], axis=0)
            dq_parts, dk_groups, dv_groups = [], [], []
            for g in range(KV_HEADS):
                k_low, k_high = _pad_heads(kband, g)
                v_low, v_high = _pad_heads(vband, g)
                k_both = jnp.concatenate([k_low, k_high], axis=0)
                dk_acc = jnp.zeros((2 * CHUNK, LANE), F32)
                dv_acc = jnp.zeros((2 * CHUNK, LANE), F32)
                for pair in range(qpk // 2):
                    h = g * qpk + 2 * pair
                    cols = slice(h * HEAD_DIM, (h + 2) * HEAD_DIM)
                    qs = (q_ref[rows, cols] * scale).astype(BF16)
                    dob = do_ref[rows, cols].astype(BF16)
                    probs, dscores = [], []
                    for head, kz, vz in ((h, k_low, v_low), (h + 1, k_high, v_high)):
                        st = jnp.where(mask, _dot(kz, qs, NT) + bias_ref[head], NEG)
                        pt, p_sink = _softmax_with_sink(st, sink_ref[head], 0)
                        dpt = _dot(vz, dob, NT)
                        delta = jnp.sum(pt * dpt, axis=0, keepdims=True)
                        dst = pt * (dpt - delta)
                        dbias_ref[head] += dst
                        sacc[head:head + 1, :] += -(p_sink * delta)
                        probs.append(pt)
                        dscores.append(dst)
                    dk_acc = dk_acc + _dot(jnp.concatenate(dscores, axis=1).astype(BF16), split_pair(qs), NN)
                    dv_acc = dv_acc + _dot(jnp.concatenate(probs, axis=1).astype(BF16), split_pair(dob), NN)
                    dq_parts.append(_dot(jnp.concatenate(dscores, axis=0).astype(BF16), k_both, TN) * scale)
                dk_groups.append(dk_acc + pltpu.roll(dk_acc, HEAD_DIM, 1))
                dv_groups.append(dv_acc + pltpu.roll(dv_acc, HEAD_DIM, 1))
            dkv[band, :LANE] += jnp.where(low_key, dk_groups[0], dk_groups[1])
            dkv[band, LANE:] += jnp.where(low_key, dv_groups[0], dv_groups[1])
            dproj_ref[rows, 2 * A:2 * A + B] = jnp.concatenate(dq_parts, axis=1).astype(BF16)
        last = slice(tq, tq + CHUNK)
        dkv[last, :] += carry[...]
        dproj_ref[:, 2 * A + B:] = dkv[CHUNK:, :].astype(BF16)
        carry[...] = dkv[:CHUNK, :]

        @pl.when(step == n_tiles - 1)
        def _():
            dsink_ref[...] = jnp.sum(sacc[...], axis=1, keepdims=True)

    specs = _attn_specs(tq, A, B, reverse_tiles=n_tiles)
    return _CHAIN.call(
        body, name="attn_bwd", grid=(n_tiles,),
        in_specs=[pl.BlockSpec(memory_space=pltpu.SMEM)] + specs
        + [pl.BlockSpec((tq, B), lambda i: (rev(i), 0)), pl.BlockSpec((tq, 2 * A), lambda i: (rev(i), 0)),
           pl.BlockSpec((H, 2 * CHUNK, CHUNK), lambda i: (0, 0, 0))],
        out_specs=[pl.BlockSpec((tq, P), lambda i: (rev(i), 0)),
                   pl.BlockSpec((H, 2 * CHUNK, CHUNK), lambda i: (0, 0, 0)), pl.BlockSpec((H, 1), lambda i: (0, 0))],
        out_shape=[SDS((T, P), BF16), SDS((H, 2 * CHUNK, CHUNK), F32), SDS((H, 1), F32)],
        scratch_shapes=[pltpu.VMEM((CHUNK, 2 * LANE), F32), pltpu.VMEM((tq + CHUNK, 2 * LANE), F32),
                        pltpu.VMEM((H, LANE), F32)],
        compiler_params=_params(1))(sinks, proj, proj, proj, proj, proj, do, duv, bias_t)


def _bias_bwd(dbias, onehot):
    H = dbias.shape[0]
    nbk = onehot.shape[1]

    def body(d_ref, oh_ref, o_ref):
        hi, mid, lo = _split3(d_ref[...])
        oh = oh_ref[...]
        o_ref[...] = _dot(hi, oh, NN) + _dot(mid, oh, NN) + _dot(lo, oh, NN)

    return _CHAIN.call(body, name="bias_bwd", in_specs=[VMEM_SPEC] * 2, out_specs=VMEM_SPEC, out_shape=SDS((H, nbk), F32),
                       compiler_params=_params(0))(dbias, onehot)


def _inproj_bwd(dproj, w_t, x, dh1, g):
    T, P = dproj.shape
    D = x.shape[1]
    tm = _tile(T, 512)

    def body(dp_ref, w_ref, x_ref, dh_ref, g_ref, dx_ref, dg_ref):
        @pl.when(pl.program_id(0) == 0)
        def _():
            dg_ref[...] = jnp.zeros_like(dg_ref)

        dn = _dot(dp_ref[...], w_ref[...], NN)
        xv = x_ref[...]
        r = _rms_stats(xv)
        dg_ref[...] += jnp.sum(dn * (xv * r), axis=0, keepdims=True)
        dx_ref[...] = dh_ref[...] + _rms_bwd(dn, xv, r, g_ref[...])

    return _CHAIN.call(
        body, name="inproj_bwd", grid=(T // tm,),
        in_specs=[pl.BlockSpec((tm, P), lambda i: (i, 0)), _resident((P, D)),
                  pl.BlockSpec((tm, D), lambda i: (i, 0)), pl.BlockSpec((tm, D), lambda i: (i, 0)),
                  pl.BlockSpec((1, D), lambda i: (0, 0))],
        out_specs=[pl.BlockSpec((tm, D), lambda i: (i, 0)), pl.BlockSpec((1, D), lambda i: (0, 0))],
        out_shape=[SDS((T, D), F32), SDS((1, D), F32)], compiler_params=_params(1))(dproj, w_t, x, dh1, g)


def _adamw(w, g, m, v):
    m = ADAM_B1 * m + (1.0 - ADAM_B1) * g
    v = ADAM_B2 * v + (1.0 - ADAM_B2) * (g * g)
    m_hat = m / (1.0 - ADAM_B1 ** ADAM_STEP)
    v_hat = v / (1.0 - ADAM_B2 ** ADAM_STEP)
    delta = -ADAM_LR * (m_hat / (jnp.sqrt(v_hat) + ADAM_EPS) + ADAM_WD * w)
    return delta, m, v


def _adam_sharded(csum, recv, w, m, v, name):
    R, C = w.shape
    tr = _tile(R, 256, 16)

    def body(own_ref, recv_ref, w_ref, m_ref, v_ref, g_ref, d_ref, nm_ref, nv_ref):
        g = own_ref[...].astype(F32)
        for r in range(3):
            g = g + recv_ref[r].astype(F32)
        delta, nm, nv = _adamw(w_ref[...], g, m_ref[...], v_ref[...])
        g_ref[...] = g
        d_ref[...] = delta
        nm_ref[...] = nm
        nv_ref[...] = nv

    blk = pl.BlockSpec((tr, C), lambda i: (i, 0))
    return _CHAIN.call(
        body, name=name, grid=(R // tr,),
        in_specs=[pl.BlockSpec((None, tr, C), lambda i: (0, i, 0)), pl.BlockSpec((3, tr, C), lambda i: (0, i, 0)),
                  blk, blk, blk],
        out_specs=[blk] * 4, out_shape=[SDS((R, C), F32)] * 4, compiler_params=_params(1))(csum, recv, w, m, v)


def _adam_small(gathered, w, m, v):
    R = w.shape[0]

    def body(p_ref, w_ref, m_ref, v_ref, g_ref, d_ref, nm_ref, nv_ref):
        g = p_ref[0]
        for d in range(1, N_DEV):
            g = g + p_ref[d]
        delta, nm, nv = _adamw(w_ref[...], g, m_ref[...], v_ref[...])
        g_ref[...] = g
        d_ref[...] = delta
        nm_ref[...] = nm
        nv_ref[...] = nv

    return _CHAIN.call(body, name="adam_small", in_specs=[VMEM_SPEC] * 4, out_specs=[VMEM_SPEC] * 4,
                       out_shape=[SDS((R, LANE), F32)] * 4,
                       compiler_params=_params(0))(gathered, w, m, v)


def _pack(arrays):
    tile = 8 * LANE
    pieces = []
    for a in arrays:
        flat = a.reshape(-1).astype(F32)
        pieces.append(jnp.pad(flat, (0, (-flat.size) % tile)))
    return jnp.concatenate(pieces).reshape(-1, LANE)


def _unpack(packed, shapes):
    tile = 8 * LANE
    flat = packed.reshape(-1)
    out, off = [], 0
    for s in shapes:
        size = int(np.prod(s))
        out.append(flat[off:off + size].reshape(s))
        off += size + (-size) % tile
    return out


def kernel(x, rel_bias_table, mix_norm_g, w_in, gate_norm_g, gate_norm_b, w_spatial, b_spatial, attn_sinks, out_norm_a_g, out_norm_b_g, w_out, ffn_norm_g, w_up, w_down, final_norm_g, loss_target, m_rel_bias_table, m_mix_norm_g, m_w_in, m_gate_norm_g, m_gate_norm_b, m_w_spatial, m_b_spatial, m_attn_sinks, m_out_norm_a_g, m_out_norm_b_g, m_w_out, m_ffn_norm_g, m_w_up, m_w_down, m_final_norm_g, v_rel_bias_table, v_mix_norm_g, v_w_in, v_gate_norm_g, v_gate_norm_b, v_w_spatial, v_b_spatial, v_attn_sinks, v_out_norm_a_g, v_out_norm_b_g, v_w_out, v_ffn_norm_g, v_w_up, v_w_down, v_final_norm_g):
    T, D = x.shape[1], x.shape[2]
    A = D // 2
    B = D // 2
    G = A // GROUP_DIM
    H = B // HEAD_DIM
    P = 2 * A + B + 2 * KV_HEADS * HEAD_DIM
    xs = x.reshape(T, D)
    target = loss_target.reshape(T, D)

    win_t, m_win_t, v_win_t = (jnp.swapaxes(a[0], 0, 1) for a in (w_in, m_w_in, v_w_in))
    shards = [win_t.astype(BF16), w_out[0].astype(BF16), w_up[0].astype(BF16), w_down[0].astype(BF16)]
    _CHAIN.token = None
    gather = _gather_begin(shards, "gather_start")
    _gather_pass_on(gather, [0], "gather_in_pass")
    (win_g,) = _gather_end(gather, [0], "gather_in_end")
    win_t_full = win_g.reshape(P, D)

    g1, g2, g3 = mix_norm_g.reshape(1, D), ffn_norm_g.reshape(1, D), final_norm_g.reshape(1, D)
    lg, lb = gate_norm_g.reshape(1, A), gate_norm_b.reshape(1, A)
    ws = w_spatial[0]
    ws_t = jnp.swapaxes(ws, 1, 2)
    bs_t = jnp.transpose(b_spatial[0])
    ga, gb = out_norm_a_g.reshape(1, A), out_norm_b_g.reshape(1, B)
    sinks = attn_sinks.reshape(H)
    bucket, in_window = _t5_bucket()
    onehot_np = ((bucket[:, :, None] == np.arange(N_BUCKETS)) & in_window[:, :, None]).astype(np.float32)
    onehot = jnp.asarray(onehot_np.reshape(-1, N_BUCKETS)).astype(BF16)
    onehot_kq = jnp.asarray(onehot_np.transpose(1, 0, 2).reshape(-1, N_BUCKETS)).astype(BF16)

    bias, bias_t = _bias_fwd(jnp.transpose(rel_bias_table), jnp.transpose(onehot), jnp.transpose(onehot_kq))
    bias, bias_t = bias.reshape(H, CHUNK, 2 * CHUNK), bias_t.reshape(H, 2 * CHUNK, CHUNK)
    proj, n1 = _inproj_fwd(xs, g1, win_t_full)
    _gather_pass_on(gather, [1], "gather_out_pass")
    a_out = _gmlp_fwd(proj, lg, lb, ws, bs_t, A)
    b_out = _attn_fwd(proj, bias, sinks, A, B)
    _gather_pass_on(gather, [2], "gather_up_pass")
    (wout_g,) = _gather_end(gather, [1], "gather_out_end")
    wout_full = wout_g.reshape(A + B, D)
    h1, mixed = _outproj_fwd(a_out, b_out, ga, gb, xs, wout_full)
    _gather_pass_on(gather, [3], "gather_down_pass")
    (wup_g,) = _gather_end(gather, [2], "gather_up_end")
    wup_t = jnp.transpose(wup_g, (0, 2, 1)).reshape(-1, D)
    z, n2 = _ffn_up(h1, g2, wup_g)
    (wdown_g,) = _gather_end(gather, [3], "gather_down_end")
    h2 = _ffn_down(h1, z, wdown_g.reshape(-1, D))
    loss_part, dg3, dh2, dh2b = _final_loss(h2, g3, target)

    def reduce_to_chip(state, name):
        part, received = _sibling_exchange_end(state, name + "_sib_end")
        return _chip_exchange_begin(_chip_sum(part, received, name + "_chip_sum"), name + "_chip")

    dwdown = _matmul_tn(z, dh2b, "grad_w_down", square_a=True).reshape(wdown_g.shape)
    sib_down = _sibling_exchange_begin(dwdown, "rs_down_sib")
    dzp = _ffn_down_bwd(dh2b, z, wdown_g.reshape(-1, D))
    chip_down = reduce_to_chip(sib_down, "rs_down")
    dwup = _matmul_tn(n2, dzp, "grad_w_up", col_blocks=N_DEV)
    sib_up = _sibling_exchange_begin(dwup, "rs_up_sib")
    dh1, dh1b, dg2 = _ffn_norm_bwd(_ffn_up_bwd(dzp, wup_t), dh2, h1, g2)
    chip_up = reduce_to_chip(sib_up, "rs_up")
    da, db, dga, dgb = _outproj_bwd(dh1b, wout_full, a_out, b_out, ga, gb)
    dwout = _matmul_tn(mixed, dh1b, "grad_w_out").reshape(wout_g.shape)
    sib_out = _sibling_exchange_begin(dwout, "rs_out_sib")
    duv, dlg, dlb, dws, dbs_t = _gmlp_bwd(proj, da, lg, lb, ws, ws_t, bs_t, A)
    dproj, dbias_t, dsinks = _attn_bwd(proj, db, duv, bias_t, sinks, A, B)
    chip_out = reduce_to_chip(sib_out, "rs_out")
    dtable_t = _bias_bwd(dbias_t.reshape(H, -1), onehot_kq)
    dwin_t = _matmul_tn(dproj, n1, "grad_w_in").reshape(win_g.shape)
    sib_in = _sibling_exchange_begin(dwin_t, "rs_in_sib")
    grad_x, dg1 = _inproj_bwd(dproj, win_t_full, xs, dh1, g1)

    small_w = [rel_bias_table, mix_norm_g, gate_norm_g, gate_norm_b, w_spatial, b_spatial, attn_sinks,
               out_norm_a_g, out_norm_b_g, ffn_norm_g, final_norm_g]
    small_m = [m_rel_bias_table, m_mix_norm_g, m_gate_norm_g, m_gate_norm_b, m_w_spatial, m_b_spatial, m_attn_sinks,
               m_out_norm_a_g, m_out_norm_b_g, m_ffn_norm_g, m_final_norm_g]
    small_v = [v_rel_bias_table, v_mix_norm_g, v_gate_norm_g, v_gate_norm_b, v_w_spatial, v_b_spatial, v_attn_sinks,
               v_out_norm_a_g, v_out_norm_b_g, v_ffn_norm_g, v_final_norm_g]
    small_g = [jnp.transpose(dtable_t), dg1, dlg, dlb, dws, jnp.transpose(dbs_t), dsinks, dga, dgb, dg2, dg3]
    shapes = [w.shape for w in small_w]
    big = [None] * 4

    def adam_of(k, state, w, m, v):
        csum, received = _chip_exchange_end(state, "rs_%d_end" % k)
        big[k] = _adam_sharded(csum, received, w, m, v, "adam_%d" % k)

    small_gather = _gather_begin([_pack(small_g)], "small_gather_start")
    chip_in = reduce_to_chip(sib_in, "rs_in")
    _gather_pass_on(small_gather, [0], "small_gather_pass")
    adam_of(3, chip_down, w_down[0], m_w_down[0], v_w_down[0])
    (gathered,) = _gather_end(small_gather, [0], "small_gather_end")
    sg, sd, sm, sv = [_unpack(o, shapes) for o in _adam_small(gathered, _pack(small_w), _pack(small_m), _pack(small_v))]
    adam_of(2, chip_up, w_up[0], m_w_up[0], v_w_up[0])
    adam_of(1, chip_out, w_out[0], m_w_out[0], v_w_out[0])
    adam_of(0, chip_in, win_t, m_win_t, v_win_t)
    big[0] = [jnp.swapaxes(o, 0, 1) for o in big[0]]
    big = [[o.reshape(w.shape) for o in outs] for outs, w in zip(big, (w_in, w_out, w_up, w_down))]

    loss = lax.psum(loss_part[0, 0], ("x", "y", "c"))

    order = ["s0", "s1", "b0", "s2", "s3", "s4", "s5", "s6", "s7", "s8", "b1", "s9", "b2", "b3", "s10"]

    def group(idx):
        small = (sg, sd, sm, sv)[idx]
        return [small[int(t[1:])] if t[0] == "s" else big[int(t[1:])][idx] for t in order]

    return (loss, grad_x.reshape(x.shape), *group(0), *group(1), *group(2), *group(3))
```

```python
import functools
import math

import numpy as np
import jax
import jax.numpy as jnp
from jax import lax
from jax.experimental import pallas as pl
from jax.experimental.pallas import tpu as pltpu

F32 = jnp.float32
BF16 = jnp.bfloat16
SDS = jax.ShapeDtypeStruct
MESH = pl.DeviceIdType.MESH

N_DEV = 8
EPS = 1e-5
NEG = -1e30
CHUNK = 128
GROUP_DIM = 128
HEAD_DIM = 64
KV_HEADS = 2
N_BUCKETS = 32
MAX_DISTANCE = 128
ADAM_LR, ADAM_B1, ADAM_B2, ADAM_EPS, ADAM_WD, ADAM_STEP = 0.001, 0.9, 0.999, 1e-08, 0.01, 10
GELU_C0 = math.sqrt(2.0 / math.pi)
GELU_C1 = 0.044715

V7X_VMEM_BYTES = 64 * 1024 * 1024
VMEM_LIMIT = V7X_VMEM_BYTES - 8 * 1024 * 1024
LANE = 128

NN = ((1,), (0,))
NT = ((1,), (1,))
TN = ((0,), (0,))


def _dot(a, b, dims):
    return lax.dot_general(a, b, (dims, ((), ())), preferred_element_type=F32)


def _tile(n, pref, unit=LANE):
    best = None
    for t in range(unit, min(n, pref) + 1, unit):
        if n % t == 0:
            best = t
    return n if best is None else best


def _params(n_grid):
    return pltpu.CompilerParams(dimension_semantics=("arbitrary",) * n_grid, vmem_limit_bytes=VMEM_LIMIT)


def _resident(shape):
    return pl.BlockSpec(shape, lambda i: (0, 0), pipeline_mode=pl.Buffered(1))


def _gelu(x):
    return 0.5 * x * (1.0 + jnp.tanh(GELU_C0 * (x + GELU_C1 * x * x * x)))


def _gelu_and_grad(x):
    x2 = x * x
    t = jnp.tanh(GELU_C0 * x * (1.0 + GELU_C1 * x2))
    val = 0.5 * x * (1.0 + t)
    grad = 0.5 * (1.0 + t) + 0.5 * x * (1.0 - t * t) * (GELU_C0 * (1.0 + 3.0 * GELU_C1 * x2))
    return val, grad


def _rms_stats(x):
    return lax.rsqrt(jnp.mean(x * x, axis=-1, keepdims=True) + EPS)


def _rms_bwd(dy, x, r, g):
    w = dy * g
    return r * w - x * (r * r * r) * jnp.mean(w * x, axis=-1, keepdims=True)


def _t5_bucket():
    i = np.arange(CHUNK)[:, None]
    j = np.arange(2 * CHUNK)[None, :]
    rel = np.maximum(i + CHUNK - j, 0)
    n_exact = N_BUCKETS // 2
    relf = np.maximum(rel, n_exact).astype(np.float32)
    large = n_exact + (np.log(relf / np.float32(n_exact)) / np.float32(math.log(MAX_DISTANCE / n_exact))
                       * np.float32(N_BUCKETS - n_exact)).astype(np.int32)
    large = np.minimum(large, N_BUCKETS - 1)
    bucket = np.where(rel < n_exact, rel, large)
    in_window = (i + CHUNK - j >= 0) & (i + CHUNK - j < CHUNK)
    return bucket.astype(np.int32), in_window


def _split3(x):
    hi = x.astype(BF16)
    r1 = x - hi.astype(F32)
    mid = r1.astype(BF16)
    lo = (r1 - mid.astype(F32)).astype(BF16)
    return hi, mid, lo


HBM_SPEC = pl.BlockSpec(memory_space=pltpu.HBM)


def _mesh_pos():
    return lax.axis_index("x"), lax.axis_index("y"), lax.axis_index("c")


def _dev_index(px, py, pc):
    return 4 * px + 2 * py + pc


def _all_gather(shards, name):
    n = len(shards)

    def body(*refs):
        ins, outs = refs[:n], refs[n:2 * n]
        send_sems, recv_sems, local_sems = refs[2 * n:]
        x, y, c = _mesh_pos()
        me, sibling = (x, y, c), (x, y, 1 - c)
        chips = [(1 - x, y), (x, 1 - y), (1 - x, 1 - y)]

        def copy(a, k, block, to, src=None):
            dst = outs[a].at[_dev_index(*block)]
            return pltpu.make_async_remote_copy(
                src_ref=dst if src is None else src, dst_ref=dst,
                send_sem=send_sems.at[a * 7 + k], recv_sem=recv_sems.at[a * 7 + k],
                device_id=to, device_id_type=MESH)

        mine = [pltpu.make_async_copy(ins[a], outs[a].at[_dev_index(*me)], local_sems.at[a]) for a in range(n)]
        first = []
        for a in range(n):
            for j, chip in enumerate(chips):
                first.append(copy(a, 1 + j, me, (*chip, c), src=ins[a]))
            first.append(copy(a, 0, me, sibling, src=ins[a]))
        for cp in first:
            cp.start()
        for cp in mine:
            cp.start()
        passed = []
        for a in range(n):
            for j, chip in enumerate(chips):
                copy(a, 1 + j, (*chip, c), me).wait_recv()
                fwd = copy(a, 4 + j, (*chip, c), sibling)
                fwd.start()
                passed.append(fwd)
        for a in range(n):
            copy(a, 0, sibling, me).wait_recv()
            for j, chip in enumerate(chips):
                copy(a, 4 + j, (*chip, 1 - c), me).wait_recv()
        for cp in first + passed:
            cp.wait_send()
        for cp in mine:
            cp.wait()

    return _CHAIN.call(
        body, name=name,
        out_shape=[SDS((N_DEV,) + s.shape, s.dtype) for s in shards],
        in_specs=[HBM_SPEC] * n, out_specs=[HBM_SPEC] * n,
        scratch_shapes=[pltpu.SemaphoreType.DMA((7 * n,)), pltpu.SemaphoreType.DMA((7 * n,)),
                        pltpu.SemaphoreType.DMA((n,))],
    )(*shards)


SEM_SPEC = pl.BlockSpec(memory_space=pltpu.SEMAPHORE)
ANY_SPEC = pl.BlockSpec(memory_space=pl.ANY)
VMEM_SPEC = pl.BlockSpec(memory_space=pltpu.VMEM)
TOKEN_SPEC = VMEM_SPEC
TOKEN = SDS((8, LANE), F32)
SIDE_EFFECT = pltpu.SideEffectType.DATAFLOW_SIDE_EFFECTING


def _hbm(x):
    return pltpu.with_memory_space_constraint(x, pltpu.HBM)


class _CallChain:
    def __init__(self):
        self.token = None

    def call(self, body, *, in_specs, out_specs, out_shape, **kwargs):
        dep, n_in = self.token, len(in_specs)
        single = not isinstance(out_shape, (list, tuple))
        out_shapes = [out_shape] if single else list(out_shape)
        out_specs = [out_specs] if single else list(out_specs)
        n_out = len(out_shapes)
        n_dep = 0 if dep is None else 1
        token_spec = pl.BlockSpec((8, LANE), lambda *_: (0, 0)) if kwargs.get("grid") else VMEM_SPEC

        def chained(*refs):
            outs_at = n_in + n_dep
            body(*refs[:n_in], *refs[outs_at:outs_at + n_out], *refs[outs_at + n_out + 1:])
            token = refs[outs_at + n_out]
            token[...] = jnp.zeros_like(token)

        inner = pl.pallas_call(chained, in_specs=list(in_specs) + [ANY_SPEC] * n_dep, out_specs=out_specs + [token_spec],
                               out_shape=out_shapes + [TOKEN], **kwargs)

        def run(*operands):
            outs = inner(*operands) if dep is None else inner(*operands, dep)
            self.token = outs[n_out]
            return outs[0] if single else list(outs[:n_out])

        return run


_CHAIN = _CallChain()


def _split_start(bufs, copies_of, n_sems, name):
    n = len(bufs)

    def body(*refs):
        ins = refs[:n]
        send_sems, recv_sems = refs[n], refs[n + 1]
        for src, dst, k, target in copies_of(ins):
            pltpu.make_async_remote_copy(src_ref=src, dst_ref=dst, send_sem=send_sems.at[k], recv_sem=recv_sems.at[k],
                                         device_id=target, device_id_type=MESH).start()

    outs = _CHAIN.call(
        body, name=name,
        out_shape=[pltpu.SemaphoreType.DMA((n_sems,)), pltpu.SemaphoreType.DMA((n_sems,))]
        + [pltpu.HBM(b.shape, b.dtype) for b in bufs],
        in_specs=[HBM_SPEC] * n, out_specs=[SEM_SPEC, SEM_SPEC] + [HBM_SPEC] * n,
        input_output_aliases={a: 2 + a for a in range(n)},
        compiler_params=pltpu.CompilerParams(has_side_effects=SIDE_EFFECT),
    )(*[_hbm(b) for b in bufs])
    return outs[0], outs[1], list(outs[2:2 + n])


def _split_wait(bufs, sem_sets, waits_of, name):
    n, ns = len(bufs), len(sem_sets)
    flat_sems = [s for pair in sem_sets for s in pair]

    def body(*refs):
        ins = refs[:n]
        sems = refs[n:n + 2 * ns]
        x, y, c = _mesh_pos()
        for kind, src, dst, send_sem, recv_sem in waits_of(ins, [(sems[2 * i], sems[2 * i + 1]) for i in range(ns)]):
            cp = pltpu.make_async_remote_copy(src_ref=src, dst_ref=dst, send_sem=send_sem, recv_sem=recv_sem,
                                              device_id=(x, y, c), device_id_type=MESH)
            if kind == "send":
                cp.wait_send()
            else:
                cp.wait_recv()

    outs = _CHAIN.call(
        body, name=name,
        out_shape=[pltpu.HBM(b.shape, b.dtype) for b in bufs],
        in_specs=[HBM_SPEC] * n + [SEM_SPEC] * (2 * ns), out_specs=[HBM_SPEC] * n,
        input_output_aliases={a: a for a in range(n)},
        compiler_params=pltpu.CompilerParams(has_side_effects=SIDE_EFFECT),
    )(*bufs, *flat_sems)
    return list(outs)


def _gather_begin(shards, name):
    me = _dev_index(*_mesh_pos())
    lands = [lax.dynamic_update_index_in_dim(lax.empty((N_DEV,) + s.shape, s.dtype), s, me, 0) for s in shards]

    def copies_of(ins):
        x, y, c = _mesh_pos()
        targets = [(x, y, 1 - c), (1 - x, y, c), (x, 1 - y, c), (1 - x, 1 - y, c)]
        out = []
        for a, land in enumerate(ins):
            blk = land.at[_dev_index(x, y, c)]
            for k in (1, 2, 3, 0):
                out.append((blk, blk, 4 * a + k, targets[k]))
        return out

    send_sems, recv_sems, lands = _split_start(lands, copies_of, 4 * len(shards), name)
    return dict(lands=lands, sems=(send_sems, recv_sems), fwd={})


def _gather_pass_on(state, which, name):
    def arrivals(ins, sems):
        x, y, c = _mesh_pos()
        chips = [(1 - x, y), (x, 1 - y), (1 - x, 1 - y)]
        out = []
        for i, a in enumerate(which):
            for j, (px, py) in enumerate(chips):
                blk = ins[i].at[_dev_index(px, py, c)]
                out.append(("recv", blk, blk, sems[0][0].at[4 * a + 1 + j], sems[0][1].at[4 * a + 1 + j]))
        return out

    bufs = _split_wait([state["lands"][a] for a in which], [state["sems"]], arrivals, name + "_arrived")

    def copies_of(ins):
        x, y, c = _mesh_pos()
        chips = [(1 - x, y), (x, 1 - y), (1 - x, 1 - y)]
        out = []
        for i in range(len(which)):
            for j, (px, py) in enumerate(chips):
                blk = ins[i].at[_dev_index(px, py, c)]
                out.append((blk, blk, 3 * i + j, (x, y, 1 - c)))
        return out

    send_sems, recv_sems, bufs = _split_start(bufs, copies_of, 3 * len(which), name)
    for i, a in enumerate(which):
        state["lands"][a] = bufs[i]
    state["fwd"][tuple(which)] = (send_sems, recv_sems)


def _gather_end(state, which, name):
    def waits(ins, sems):
        x, y, c = _mesh_pos()
        chips = [(1 - x, y), (x, 1 - y), (1 - x, 1 - y)]
        (s_send, s_recv), (f_send, f_recv) = sems
        out = []
        for i, a in enumerate(which):
            mine = ins[i].at[_dev_index(x, y, c)]
            sib = ins[i].at[_dev_index(x, y, 1 - c)]
            out.append(("recv", sib, sib, s_send.at[4 * a], s_recv.at[4 * a]))
            for j, (px, py) in enumerate(chips):
                theirs = ins[i].at[_dev_index(px, py, 1 - c)]
                out.append(("recv", theirs, theirs, f_send.at[3 * i + j], f_recv.at[3 * i + j]))
            for k in range(4):
                out.append(("send", mine, mine, s_send.at[4 * a + k], s_recv.at[4 * a + k]))
            for j, (px, py) in enumerate(chips):
                passed = ins[i].at[_dev_index(px, py, c)]
                out.append(("send", passed, passed, f_send.at[3 * i + j], f_recv.at[3 * i + j]))
        return out

    bufs = _split_wait([state["lands"][a] for a in which], [state["sems"], state["fwd"][tuple(which)]], waits, name)
    for i, a in enumerate(which):
        state["lands"][a] = bufs[i]
    return bufs


def _sibling_exchange_begin(part, name):
    land = lax.empty((4,) + part.shape[1:], part.dtype)

    def copies_of(ins):
        x, y, c = _mesh_pos()
        return [(ins[0].at[2 * j + (1 - c)], ins[1].at[j], j, (x, y, 1 - c)) for j in range(4)]

    send_sems, recv_sems, bufs = _split_start([part, land], copies_of, 4, name)
    return dict(bufs=bufs, sems=(send_sems, recv_sems))


def _sibling_exchange_end(state, name):
    def waits(ins, sems):
        _, _, c = _mesh_pos()
        out = []
        for j in range(4):
            for kind in ("send", "recv"):
                out.append((kind, ins[0].at[2 * j + (1 - c)], ins[1].at[j], sems[0][0].at[j], sems[0][1].at[j]))
        return out

    return _split_wait(state["bufs"], [state["sems"]], waits, name)


CHIP_FLIPS = (2, 1, 3)


def _chip_exchange_begin(csum, name):
    land = lax.empty((3,) + csum.shape[1:], csum.dtype)

    def copies_of(ins):
        x, y, c = _mesh_pos()
        chips = [(1 - x, y), (x, 1 - y), (1 - x, 1 - y)]
        return [(ins[0].at[CHIP_FLIPS[r]], ins[1].at[r], r, (px, py, c)) for r, (px, py) in enumerate(chips)]

    send_sems, recv_sems, bufs = _split_start([csum, land], copies_of, 3, name)
    return dict(bufs=bufs, sems=(send_sems, recv_sems))


def _chip_exchange_end(state, name):
    def waits(ins, sems):
        out = []
        for r in range(3):
            for kind in ("send", "recv"):
                out.append((kind, ins[0].at[CHIP_FLIPS[r]], ins[1].at[r], sems[0][0].at[r], sems[0][1].at[r]))
        return out

    return _split_wait(state["bufs"], [state["sems"]], waits, name)


def _chip_sum(part, recv, name):
    _, R, C = part.shape
    tr = _tile(R, 512, 16)
    place = jnp.stack([lax.axis_index("c"), 2 * lax.axis_index("x") + lax.axis_index("y")]).astype(jnp.int32)

    def body(place_ref, p_ref, r_ref, o_ref):
        o_ref[...] = (p_ref[...].astype(F32) + r_ref[...].astype(F32)).astype(o_ref.dtype)

    def chip(p, place_ref):
        return jnp.bitwise_xor(p, place_ref[1])

    grid_spec = pltpu.PrefetchScalarGridSpec(
        num_scalar_prefetch=1, grid=(4, R // tr),
        in_specs=[pl.BlockSpec((None, tr, C), lambda p, i, place_ref: (2 * chip(p, place_ref) + place_ref[0], i, 0)),
                  pl.BlockSpec((None, tr, C), lambda p, i, place_ref: (chip(p, place_ref), i, 0))],
        out_specs=pl.BlockSpec((None, tr, C), lambda p, i, place_ref: (p, i, 0)))
    return pl.pallas_call(body, name=name, grid_spec=grid_spec, out_shape=SDS((4, R, C), part.dtype),
                          compiler_params=_params(2))(place, part, recv)


def _bias_fwd(table_t, onehot_t, onehot_kq_t):
    H = table_t.shape[0]
    n = onehot_t.shape[1]

    def body(t_ref, oh_ref, oh_kq_ref, o_ref, o_kq_ref):
        hi, mid, lo = _split3(t_ref[...])
        for src, dst in ((oh_ref, o_ref), (oh_kq_ref, o_kq_ref)):
            oh = src[...]
            dst[...] = _dot(hi, oh, NN) + _dot(mid, oh, NN) + _dot(lo, oh, NN)

    return _CHAIN.call(body, name="bias_fwd", in_specs=[VMEM_SPEC] * 3, out_specs=[VMEM_SPEC] * 2,
                       out_shape=[SDS((H, n), F32)] * 2, compiler_params=_params(0))(table_t, onehot_t, onehot_kq_t)


def _mix_norm(x, g):
    T, D = x.shape
    tm = _tile(T, 512)

    def body(x_ref, g_ref, n_ref):
        xv = x_ref[...]
        n_ref[...] = (xv * _rms_stats(xv) * g_ref[...]).astype(BF16)

    row = pl.BlockSpec((tm, D), lambda i: (i, 0))
    return _CHAIN.call(body, name="mix_norm", grid=(T // tm,), in_specs=[row, pl.BlockSpec((1, D), lambda i: (0, 0))],
                       out_specs=row, out_shape=SDS((T, D), BF16), compiler_params=_params(1))(x, g)


def _inproj_fwd(n, w_t):
    T, D = n.shape
    P = w_t.shape[0]
    tm = _tile(T, 512)

    def body(n_ref, w_ref, proj_ref):
        proj_ref[...] = _dot(n_ref[...], w_ref[...], NT)

    return _CHAIN.call(
        body, name="inproj_fwd", grid=(T // tm,),
        in_specs=[pl.BlockSpec((tm, D), lambda i: (i, 0)), _resident((P, D))],
        out_specs=pl.BlockSpec((tm, P), lambda i: (i, 0)),
        out_shape=SDS((T, P), F32), compiler_params=_params(1))(n, w_t)


def _layer_norm_group(vg, lg, lb):
    mu = jnp.mean(vg, axis=-1, keepdims=True)
    xc = vg - mu
    rstd = lax.rsqrt(jnp.mean(xc * xc, axis=-1, keepdims=True) + EPS)
    vhat = xc * rstd
    return vhat, rstd, vhat * lg + lb


def _gmlp_fwd(proj, lg, lb, w_s, bs_t, A):
    T = proj.shape[0]
    G = A // GROUP_DIM
    tm = _tile(T, 512)
    nc = tm // CHUNK

    def body(u_ref, v_ref, lg_ref, lb_ref, w_ref, bst_ref, a_ref):
        row = lax.broadcasted_iota(jnp.int32, (CHUNK, CHUNK), 0)
        col = lax.broadcasted_iota(jnp.int32, (CHUNK, CHUNK), 1)
        causal = row >= col
        for g in range(G):
            sl = slice(g * GROUP_DIM, (g + 1) * GROUP_DIM)
            _, _, vn = _layer_norm_group(_gelu(v_ref[:, sl]), lg_ref[:, sl], lb_ref[:, sl])
            vnb = vn.astype(BF16)
            wm = jnp.where(causal, w_ref[g], 0.0).astype(BF16)
            ug = _gelu(u_ref[:, sl])
            bcol = bst_ref[:, g:g + 1]
            for c in range(nc):
                rs = slice(c * CHUNK, (c + 1) * CHUNK)
                a_ref[rs, sl] = ug[rs] * (_dot(wm, vnb[rs], NN) + bcol)

    return _CHAIN.call(
        body, name="gmlp_fwd", grid=(T // tm,),
        in_specs=[pl.BlockSpec((tm, A), lambda i: (i, 0)), pl.BlockSpec((tm, A), lambda i: (i, 1)),
                  pl.BlockSpec((1, A), lambda i: (0, 0)), pl.BlockSpec((1, A), lambda i: (0, 0)),
                  pl.BlockSpec((G, CHUNK, CHUNK), lambda i: (0, 0, 0)), pl.BlockSpec((CHUNK, G), lambda i: (0, 0))],
        out_specs=pl.BlockSpec((tm, A), lambda i: (i, 0)),
        out_shape=SDS((T, A), F32), compiler_params=_params(1))(proj, proj, lg, lb, w_s, bs_t)


def _attn_masks(first_tile):
    ii = lax.broadcasted_iota(jnp.int32, (CHUNK, 2 * CHUNK), 0)
    jj = lax.broadcasted_iota(jnp.int32, (CHUNK, 2 * CHUNK), 1)
    in_window = (jj > ii) & (jj <= ii + CHUNK)
    first_mask = in_window & jnp.logical_or(jnp.logical_not(first_tile), jj >= CHUNK)
    return in_window, first_mask


def _softmax_with_sink(s, sink, axis):
    m = jnp.maximum(jnp.max(s, axis=axis, keepdims=True), sink)
    p = jnp.exp(s - m)
    e_sink = jnp.exp(sink - m)
    inv = 1.0 / (jnp.sum(p, axis=axis, keepdims=True) + e_sink)
    return p * inv, e_sink * inv


def _pad_heads(band, group):
    lane = lax.broadcasted_iota(jnp.int32, band.shape, 1)
    if group == 0:
        low = jnp.where(lane < HEAD_DIM, band, 0.0)
        high = pltpu.roll(low, HEAD_DIM, 1)
    else:
        high = jnp.where(lane >= HEAD_DIM, band, 0.0)
        low = pltpu.roll(high, HEAD_DIM, 1)
    return low.astype(BF16), high.astype(BF16)


def _attn_specs(tq, A, B, reverse_tiles=None):
    nb = tq // CHUNK
    kcol = (2 * A + B) // LANE
    if reverse_tiles is None:
        tile = lambda i: i
    else:
        tile = lambda i: reverse_tiles - 1 - i
    prev = lambda i: jnp.maximum(tile(i) * nb - 1, 0)
    return [pl.BlockSpec((tq, B), lambda i: (tile(i), 2 * A // B)),
            pl.BlockSpec((tq, LANE), lambda i: (tile(i), kcol)),
            pl.BlockSpec((tq, LANE), lambda i: (tile(i), kcol + 1)),
            pl.BlockSpec((CHUNK, LANE), lambda i: (prev(i), kcol)),
            pl.BlockSpec((CHUNK, LANE), lambda i: (prev(i), kcol + 1))]


def _attn_fwd(proj, bias, sinks, A, B):
    T = proj.shape[0]
    H = B // HEAD_DIM
    qpk = H // KV_HEADS
    tq = _tile(T, 512)
    nb = tq // CHUNK

    scale = HEAD_DIM ** -0.5

    def body(sink_ref, q_ref, k_ref, v_ref, kp_ref, vp_ref, bias_ref, o_ref):
        in_window, first_mask = _attn_masks(pl.program_id(0) == 0)
        for b in range(nb):
            rows = slice(b * CHUNK, (b + 1) * CHUNK)
            if b == 0:
                kprev, vprev, mask = kp_ref[...], vp_ref[...], first_mask
            else:
                prows = slice((b - 1) * CHUNK, b * CHUNK)
                kprev, vprev, mask = k_ref[prows, :], v_ref[prows, :], in_window
            kband = jnp.concatenate([kprev, k_ref[rows, :]], axis=0)
            vband = jnp.concatenate([vprev, v_ref[rows, :]], axis=0)
            outs = []
            for g in range(KV_HEADS):
                k_low, k_high = _pad_heads(kband, g)
                v_both = jnp.concatenate(_pad_heads(vband, g), axis=0)
                for pair in range(qpk // 2):
                    h = g * qpk + 2 * pair
                    qs = (q_ref[rows, h * HEAD_DIM:(h + 2) * HEAD_DIM] * scale).astype(BF16)
                    probs = []
                    for head, kz in ((h, k_low), (h + 1, k_high)):
                        s = jnp.where(mask, _dot(qs, kz, NT) + bias_ref[head], NEG)
                        probs.append(_softmax_with_sink(s, sink_ref[head], -1)[0])
                    outs.append(_dot(jnp.concatenate(probs, axis=1).astype(BF16), v_both, NN))
            o_ref[rows, :] = jnp.concatenate(outs, axis=1)

    return _CHAIN.call(
        body, name="attn_fwd", grid=(T // tq,),
        in_specs=[pl.BlockSpec(memory_space=pltpu.SMEM)] + _attn_specs(tq, A, B)
        + [pl.BlockSpec((H, CHUNK, 2 * CHUNK), lambda i: (0, 0, 0))],
        out_specs=pl.BlockSpec((tq, B), lambda i: (i, 0)),
        out_shape=SDS((T, B), F32), compiler_params=_params(1))(sinks, proj, proj, proj, proj, proj, bias)


def _outproj_fwd(a, b, ga, gb, x, w):
    T, A = a.shape
    B = b.shape[1]
    D = x.shape[1]
    tm = _tile(T, 512)

    def body(a_ref, b_ref, ga_ref, gb_ref, x_ref, w_ref, h_ref, mix_ref):
        av, bv = a_ref[...], b_ref[...]
        mix_ref[:, :A] = (av * _rms_stats(av) * ga_ref[...]).astype(BF16)
        mix_ref[:, A:] = (bv * _rms_stats(bv) * gb_ref[...]).astype(BF16)
        h_ref[...] = x_ref[...] + _dot(mix_ref[...], w_ref[...], NN)

    return _CHAIN.call(
        body, name="outproj_fwd", grid=(T // tm,),
        in_specs=[pl.BlockSpec((tm, A), lambda i: (i, 0)), pl.BlockSpec((tm, B), lambda i: (i, 0)),
                  pl.BlockSpec((1, A), lambda i: (0, 0)), pl.BlockSpec((1, B), lambda i: (0, 0)),
                  pl.BlockSpec((tm, D), lambda i: (i, 0)), _resident((A + B, D))],
        out_specs=[pl.BlockSpec((tm, D), lambda i: (i, 0)), pl.BlockSpec((tm, A + B), lambda i: (i, 0))],
        out_shape=[SDS((T, D), F32), SDS((T, A + B), BF16)], compiler_params=_params(1))(a, b, ga, gb, x, w)


def _ffn_up(h1, g, w_up):
    T, D = h1.shape
    Fb = w_up.shape[2]
    F = N_DEV * Fb
    tm, tf = _tile(T, 1024), _tile(Fb, 1024)
    per = Fb // tf

    def body(h_ref, g_ref, wu_ref, z_ref, n_ref, nbuf):
        @pl.when(pl.program_id(1) == 0)
        def _():
            hv = h_ref[...]
            n = (hv * _rms_stats(hv) * g_ref[...]).astype(BF16)
            nbuf[...] = n
            n_ref[...] = n

        z_ref[...] = jnp.maximum(_dot(nbuf[...], wu_ref[...], NN), 0.0).astype(BF16)

    return _CHAIN.call(
        body, name="ffn_up", grid=(T // tm, F // tf),
        in_specs=[pl.BlockSpec((tm, D), lambda i, j: (i, 0)), pl.BlockSpec((1, D), lambda i, j: (0, 0)),
                  pl.BlockSpec((None, D, tf), lambda i, j: (j // per, 0, j % per))],
        out_specs=[pl.BlockSpec((tm, tf), lambda i, j: (i, j)), pl.BlockSpec((tm, D), lambda i, j: (i, 0))],
        out_shape=[SDS((T, F), BF16), SDS((T, D), BF16)],
        scratch_shapes=[pltpu.VMEM((tm, D), BF16)], compiler_params=_params(2))(h1, g, w_up)


def _ffn_down(h1, z, w_down):
    T, D = h1.shape
    F = w_down.shape[0]
    tm, tn, tk = _tile(T, 1024), _tile(D, 1024), _tile(F, 4096)

    def body(h_ref, z_ref, wd_ref, h2_ref):
        k = pl.program_id(2)

        @pl.when(k == 0)
        def _():
            h2_ref[...] = h_ref[...]

        zf = z_ref[...].astype(F32)
        h2_ref[...] += _dot((zf * zf).astype(BF16), wd_ref[...], NN)

    return _CHAIN.call(
        body, name="ffn_down", grid=(T // tm, D // tn, F // tk),
        in_specs=[pl.BlockSpec((tm, tn), lambda i, j, k: (i, j)), pl.BlockSpec((tm, tk), lambda i, j, k: (i, k)),
                  pl.BlockSpec((tk, tn), lambda i, j, k: (k, j))],
        out_specs=pl.BlockSpec((tm, tn), lambda i, j, k: (i, j)),
        out_shape=SDS((T, D), F32), compiler_params=_params(3))(h1, z, w_down)


def _final_loss(h2, g, target):
    T, D = h2.shape
    tm = _tile(T, 512)

    def body(h_ref, g_ref, t_ref, loss_ref, dg_ref, dh_ref, dhb_ref):
        @pl.when(pl.program_id(0) == 0)
        def _():
            loss_ref[...] = jnp.zeros_like(loss_ref)
            dg_ref[...] = jnp.zeros_like(dg_ref)

        hv, gv = h_ref[...], g_ref[...]
        r = _rms_stats(hv)
        hn = hv * r
        e = hn * gv - t_ref[...]
        loss_ref[...] += (0.5 / D) * jnp.sum(jnp.sum(e * e, axis=0, keepdims=True), axis=-1, keepdims=True)
        dy = e * (1.0 / D)
        dg_ref[...] += jnp.sum(dy * hn, axis=0, keepdims=True)
        dh = _rms_bwd(dy, hv, r, gv)
        dh_ref[...] = dh
        dhb_ref[...] = dh.astype(BF16)

    return _CHAIN.call(
        body, name="final_loss", grid=(T // tm,),
        in_specs=[pl.BlockSpec((tm, D), lambda i: (i, 0)), pl.BlockSpec((1, D), lambda i: (0, 0)),
                  pl.BlockSpec((tm, D), lambda i: (i, 0))],
        out_specs=[pl.BlockSpec((1, 1), lambda i: (0, 0)), pl.BlockSpec((1, D), lambda i: (0, 0)),
                   pl.BlockSpec((tm, D), lambda i: (i, 0)), pl.BlockSpec((tm, D), lambda i: (i, 0))],
        out_shape=[SDS((1, 1), F32), SDS((1, D), F32), SDS((T, D), F32), SDS((T, D), BF16)],
        compiler_params=_params(1))(h2, g, target)


def _ffn_down_bwd(dh2b, z, w_down):
    T, D = dh2b.shape
    F = w_down.shape[0]
    tm, tf = _tile(T, 1024), _tile(F, 1024)

    def body(dh_ref, z_ref, wd_ref, dzp_ref):
        dzz = _dot(dh_ref[...], wd_ref[...], NT)
        dzp_ref[...] = (dzz * (2.0 * z_ref[...].astype(F32))).astype(BF16)

    return _CHAIN.call(
        body, name="ffn_down_bwd", grid=(T // tm, F // tf),
        in_specs=[pl.BlockSpec((tm, D), lambda i, j: (i, 0)), pl.BlockSpec((tm, tf), lambda i, j: (i, j)),
                  pl.BlockSpec((tf, D), lambda i, j: (j, 0))],
        out_specs=pl.BlockSpec((tm, tf), lambda i, j: (i, j)),
        out_shape=SDS((T, F), BF16), compiler_params=_params(2))(dh2b, z, w_down)


def _ffn_up_bwd(dzp, w_up_t):
    T, F = dzp.shape
    D = w_up_t.shape[1]
    tm, tn, tk = _tile(T, 1024), _tile(D, 1024), _tile(F, 4096)

    def body(dzp_ref, w_ref, dn_ref):
        part = _dot(dzp_ref[...], w_ref[...], NN)

        @pl.when(pl.program_id(2) == 0)
        def _():
            dn_ref[...] = part

        @pl.when(pl.program_id(2) > 0)
        def _():
            dn_ref[...] += part

    return _CHAIN.call(
        body, name="ffn_up_bwd", grid=(T // tm, D // tn, F // tk),
        in_specs=[pl.BlockSpec((tm, tk), lambda i, j, k: (i, k)), pl.BlockSpec((tk, tn), lambda i, j, k: (k, j))],
        out_specs=pl.BlockSpec((tm, tn), lambda i, j, k: (i, j)),
        out_shape=SDS((T, D), F32), compiler_params=_params(3))(dzp, w_up_t)


def _ffn_norm_bwd(dn, dh2, h1, g):
    T, D = h1.shape
    tm = _tile(T, 256)

    def body(dn_ref, dh_ref, h_ref, g_ref, dh1_ref, dh1b_ref, dg_ref):
        @pl.when(pl.program_id(0) == 0)
        def _():
            dg_ref[...] = jnp.zeros_like(dg_ref)

        hv, dnv = h_ref[...], dn_ref[...]
        r = _rms_stats(hv)
        dg_ref[...] += jnp.sum(dnv * (hv * r), axis=0, keepdims=True)
        dh1 = dh_ref[...] + _rms_bwd(dnv, hv, r, g_ref[...])
        dh1_ref[...] = dh1
        dh1b_ref[...] = dh1.astype(BF16)

    row = pl.BlockSpec((tm, D), lambda i: (i, 0))
    vec = pl.BlockSpec((1, D), lambda i: (0, 0))
    return _CHAIN.call(
        body, name="ffn_norm_bwd", grid=(T // tm,), in_specs=[row, row, row, vec], out_specs=[row, row, vec],
        out_shape=[SDS((T, D), F32), SDS((T, D), BF16), SDS((1, D), F32)], compiler_params=_params(1))(dn, dh2, h1, g)


def _matmul_tn(a, b, name, square_a=False, col_blocks=None):
    T, K = a.shape
    N = b.shape[1]
    tk = _tile(K, 1792)
    tn = _tile(N if col_blocks is None else N // col_blocks, 1024 if tk <= 1024 else 512)

    def body(a_ref, b_ref, o_ref):
        av = a_ref[...]
        if square_a:
            af = av.astype(F32)
            av = (af * af).astype(BF16)
        o_ref[...] = _dot(av, b_ref[...], TN).astype(o_ref.dtype)

    if col_blocks is None:
        out_shape = SDS((K, N), BF16)
        out_spec = pl.BlockSpec((tk, tn), lambda i, j: (i, j))
    else:
        per = (N // col_blocks) // tn
        out_shape = SDS((col_blocks, K, N // col_blocks), BF16)
        out_spec = pl.BlockSpec((None, tk, tn), lambda i, j: (j // per, i, j % per))
    return _CHAIN.call(
        body, name=name, grid=(K // tk, N // tn),
        in_specs=[pl.BlockSpec((T, tk), lambda i, j: (0, i)), pl.BlockSpec((T, tn), lambda i, j: (0, j))],
        out_specs=out_spec, out_shape=out_shape, compiler_params=_params(2))(a, b)


def _outproj_bwd(dh1b, w, a, b, ga, gb):
    T, D = dh1b.shape
    A, B = a.shape[1], b.shape[1]
    tm = _tile(T, 512)

    def body(dh_ref, w_ref, a_ref, b_ref, ga_ref, gb_ref, da_ref, db_ref, dga_ref, dgb_ref):
        @pl.when(pl.program_id(0) == 0)
        def _():
            dga_ref[...] = jnp.zeros_like(dga_ref)
            dgb_ref[...] = jnp.zeros_like(dgb_ref)

        dmix = _dot(dh_ref[...], w_ref[...], NT)
        for src_ref, g_ref, dx_ref, dg_ref, dn in ((a_ref, ga_ref, da_ref, dga_ref, dmix[:, :A]),
                                                   (b_ref, gb_ref, db_ref, dgb_ref, dmix[:, A:])):
            xv = src_ref[...]
            r = _rms_stats(xv)
            dg_ref[...] += jnp.sum(dn * (xv * r), axis=0, keepdims=True)
            dx_ref[...] = _rms_bwd(dn, xv, r, g_ref[...])

    return _CHAIN.call(
        body, name="outproj_bwd", grid=(T // tm,),
        in_specs=[pl.BlockSpec((tm, D), lambda i: (i, 0)), _resident((A + B, D)),
                  pl.BlockSpec((tm, A), lambda i: (i, 0)), pl.BlockSpec((tm, B), lambda i: (i, 0)),
                  pl.BlockSpec((1, A), lambda i: (0, 0)), pl.BlockSpec((1, B), lambda i: (0, 0))],
        out_specs=[pl.BlockSpec((tm, A), lambda i: (i, 0)), pl.BlockSpec((tm, B), lambda i: (i, 0)),
                   pl.BlockSpec((1, A), lambda i: (0, 0)), pl.BlockSpec((1, B), lambda i: (0, 0))],
        out_shape=[SDS((T, A), F32), SDS((T, B), F32), SDS((1, A), F32), SDS((1, B), F32)],
        compiler_params=_params(1))(dh1b, w, a, b, ga, gb)


def _gmlp_bwd(proj, da, lg, lb, w_s, w_st, bs_t, A):
    T = proj.shape[0]
    G = A // GROUP_DIM
    tm = _tile(T, 512)
    nc = tm // CHUNK

    def body(u_ref, v_ref, da_ref, lg_ref, lb_ref, w_ref, wt_ref, bst_ref, duv_ref, dlg_ref, dlb_ref, dw_ref, dbs_ref):
        @pl.when(pl.program_id(0) == 0)
        def _():
            dlg_ref[...] = jnp.zeros_like(dlg_ref)
            dlb_ref[...] = jnp.zeros_like(dlb_ref)
            dw_ref[...] = jnp.zeros_like(dw_ref)
            dbs_ref[...] = jnp.zeros_like(dbs_ref)

        row = lax.broadcasted_iota(jnp.int32, (CHUNK, CHUNK), 0)
        col = lax.broadcasted_iota(jnp.int32, (CHUNK, CHUNK), 1)
        lower = row >= col
        upper = row <= col
        for g in range(G):
            sl = slice(g * GROUP_DIM, (g + 1) * GROUP_DIM)
            lgv = lg_ref[:, sl]
            vg, vg_grad = _gelu_and_grad(v_ref[:, sl])
            vhat, rstd, vn = _layer_norm_group(vg, lgv, lb_ref[:, sl])
            vnb = vn.astype(BF16)
            ug, ug_grad = _gelu_and_grad(u_ref[:, sl])
            dav = da_ref[:, sl]
            wm = jnp.where(lower, w_ref[g], 0.0).astype(BF16)
            wmt = jnp.where(upper, wt_ref[g], 0.0).astype(BF16)
            bcol = bst_ref[:, g:g + 1]
            dw_acc = jnp.zeros((CHUNK, CHUNK), F32)
            dbs_acc = jnp.zeros((CHUNK, 1), F32)
            dvn_parts = []
            dug_parts = []
            for c in range(nc):
                rs = slice(c * CHUNK, (c + 1) * CHUNK)
                mixed = _dot(wm, vnb[rs], NN) + bcol
                dug_parts.append(dav[rs] * mixed)
                dmix = dav[rs] * ug[rs]
                dbs_acc = dbs_acc + jnp.sum(dmix, axis=-1, keepdims=True)
                dmixb = dmix.astype(BF16)
                dw_acc = dw_acc + _dot(dmixb, vnb[rs], NT)
                dvn_parts.append(_dot(wmt, dmixb, NN))
            dvn = jnp.concatenate(dvn_parts, axis=0)
            dug = jnp.concatenate(dug_parts, axis=0)
            dw_ref[g] += jnp.where(lower, dw_acc, 0.0)
            dbs_ref[:, g:g + 1] += dbs_acc
            dlg_ref[:, sl] += jnp.sum(dvn * vhat, axis=0, keepdims=True)
            dlb_ref[:, sl] += jnp.sum(dvn, axis=0, keepdims=True)
            dvhat = dvn * lgv
            dvg = rstd * (dvhat - jnp.mean(dvhat, axis=-1, keepdims=True)
                          - vhat * jnp.mean(dvhat * vhat, axis=-1, keepdims=True))
            duv_ref[:, sl] = (dug * ug_grad).astype(BF16)
            duv_ref[:, A + g * GROUP_DIM:A + (g + 1) * GROUP_DIM] = (dvg * vg_grad).astype(BF16)

    return _CHAIN.call(
        body, name="gmlp_bwd", grid=(T // tm,),
        in_specs=[pl.BlockSpec((tm, A), lambda i: (i, 0)), pl.BlockSpec((tm, A), lambda i: (i, 1)),
                  pl.BlockSpec((tm, A), lambda i: (i, 0)),
                  pl.BlockSpec((1, A), lambda i: (0, 0)), pl.BlockSpec((1, A), lambda i: (0, 0)),
                  pl.BlockSpec((G, CHUNK, CHUNK), lambda i: (0, 0, 0)),
                  pl.BlockSpec((G, CHUNK, CHUNK), lambda i: (0, 0, 0)), pl.BlockSpec((CHUNK, G), lambda i: (0, 0))],
        out_specs=[pl.BlockSpec((tm, 2 * A), lambda i: (i, 0)),
                   pl.BlockSpec((1, A), lambda i: (0, 0)), pl.BlockSpec((1, A), lambda i: (0, 0)),
                   pl.BlockSpec((G, CHUNK, CHUNK), lambda i: (0, 0, 0)), pl.BlockSpec((CHUNK, G), lambda i: (0, 0))],
        out_shape=[SDS((T, 2 * A), BF16), SDS((1, A), F32), SDS((1, A), F32),
                   SDS((G, CHUNK, CHUNK), F32), SDS((CHUNK, G), F32)],
        compiler_params=_params(1))(proj, proj, da, lg, lb, w_s, w_st, bs_t)


def _attn_bwd(proj, do, duv, bias_t, sinks, A, B):
    T, P = proj.shape
    H = B // HEAD_DIM
    qpk = H // KV_HEADS
    tq = _tile(T, 512)
    nb = tq // CHUNK
    n_tiles = T // tq
    scale = HEAD_DIM ** -0.5
    rev = lambda i: n_tiles - 1 - i

    def body(sink_ref, q_ref, k_ref, v_ref, kp_ref, vp_ref, do_ref, duv_ref, bias_ref,
             dproj_ref, dbias_ref, dsink_ref, carry, dkv, sacc):
        step = pl.program_id(0)

        @pl.when(step == 0)
        def _():
            carry[...] = jnp.zeros_like(carry)
            sacc[...] = jnp.zeros_like(sacc)
            dbias_ref[...] = jnp.zeros_like(dbias_ref)

        jj = lax.broadcasted_iota(jnp.int32, (2 * CHUNK, CHUNK), 0)
        ii = lax.broadcasted_iota(jnp.int32, (2 * CHUNK, CHUNK), 1)
        in_window = (jj > ii) & (jj <= ii + CHUNK)
        first_mask = in_window & jnp.logical_or(step != n_tiles - 1, jj >= CHUNK)
        low_query = lax.broadcasted_iota(jnp.int32, (CHUNK, LANE), 1) < HEAD_DIM
        low_key = lax.broadcasted_iota(jnp.int32, (2 * CHUNK, LANE), 1) < HEAD_DIM

        def split_pair(pair_bf16):
            zero = jnp.zeros_like(pair_bf16)
            return jnp.concatenate([jnp.where(low_query, pair_bf16, zero), jnp.where(low_query, zero, pair_bf16)], axis=0)

        dproj_ref[:, :2 * A] = duv_ref[...]
        dkv[...] = jnp.zeros_like(dkv)
        for b in range(nb):
            rows = slice(b * CHUNK, (b + 1) * CHUNK)
            band = slice(b * CHUNK, (b + 2) * CHUNK)
            if b == 0:
                kprev, vprev, mask = kp_ref[...], vp_ref[...], first_mask
            else:
                prows = slice((b - 1) * CHUNK, b * CHUNK)
                kprev, vprev, mask = k_ref[prows, :], v_ref[prows, :], in_window
            kband = jnp.concatenate([kprev, k_ref[rows, :]], axis=0)
            vband = jnp.concatenate([vprev, v_ref[rows, :]], axis=0)
            dq_parts, dk_groups, dv_groups = [], [], []
            for g in range(KV_HEADS):
                k_low, k_high = _pad_heads(kband, g)
                v_low, v_high = _pad_heads(vband, g)
                k_both = jnp.concatenate([k_low, k_high], axis=0)
                dk_acc = jnp.zeros((2 * CHUNK, LANE), F32)
                dv_acc = jnp.zeros((2 * CHUNK, LANE), F32)
                for pair in range(qpk // 2):
                    h = g * qpk + 2 * pair
                    cols = slice(h * HEAD_DIM, (h + 2) * HEAD_DIM)
                    qs = (q_ref[rows, cols] * scale).astype(BF16)
                    dob = do_ref[rows, cols].astype(BF16)
                    probs, dscores = [], []
                    for head, kz, vz in ((h, k_low, v_low), (h + 1, k_high, v_high)):
                        st = jnp.where(mask, _dot(kz, qs, NT) + bias_ref[head], NEG)
                        pt, p_sink = _softmax_with_sink(st, sink_ref[head], 0)
                        dpt = _dot(vz, dob, NT)
                        delta = jnp.sum(pt * dpt, axis=0, keepdims=True)
                        dst = pt * (dpt - delta)
                        dbias_ref[head] += dst
                        sacc[head:head + 1, :] += -(p_sink * delta)
                        probs.append(pt)
                        dscores.append(dst)
                    dk_acc = dk_acc + _dot(jnp.concatenate(dscores, axis=1).astype(BF16), split_pair(qs), NN)
                    dv_acc = dv_acc + _dot(jnp.concatenate(probs, axis=1).astype(BF16), split_pair(dob), NN)
                    dq_parts.append(_dot(jnp.concatenate(dscores, axis=0).astype(BF16), k_both, TN) * scale)
                dk_groups.append(dk_acc + pltpu.roll(dk_acc, HEAD_DIM, 1))
                dv_groups.append(dv_acc + pltpu.roll(dv_acc, HEAD_DIM, 1))
            dkv[band, :LANE] += jnp.where(low_key, dk_groups[0], dk_groups[1])
            dkv[band, LANE:] += jnp.where(low_key, dv_groups[0], dv_groups[1])
            dproj_ref[rows, 2 * A:2 * A + B] = jnp.concatenate(dq_parts, axis=1).astype(BF16)
        last = slice(tq, tq + CHUNK)
        dkv[last, :] += carry[...]
        dproj_ref[:, 2 * A + B:] = dkv[CHUNK:, :].astype(BF16)
        carry[...] = dkv[:CHUNK, :]

        @pl.when(step == n_tiles - 1)
        def _():
            dsink_ref[...] = jnp.sum(sacc[...], axis=1, keepdims=True)

    specs = _attn_specs(tq, A, B, reverse_tiles=n_tiles)
    return _CHAIN.call(
        body, name="attn_bwd", grid=(n_tiles,),
        in_specs=[pl.BlockSpec(memory_space=pltpu.SMEM)] + specs
        + [pl.BlockSpec((tq, B), lambda i: (rev(i), 0)), pl.BlockSpec((tq, 2 * A), lambda i: (rev(i), 0)),
           pl.BlockSpec((H, 2 * CHUNK, CHUNK), lambda i: (0, 0, 0))],
        out_specs=[pl.BlockSpec((tq, P), lambda i: (rev(i), 0)),
                   pl.BlockSpec((H, 2 * CHUNK, CHUNK), lambda i: (0, 0, 0)), pl.BlockSpec((H, 1), lambda i: (0, 0))],
        out_shape=[SDS((T, P), BF16), SDS((H, 2 * CHUNK, CHUNK), F32), SDS((H, 1), F32)],
        scratch_shapes=[pltpu.VMEM((CHUNK, 2 * LANE), F32), pltpu.VMEM((tq + CHUNK, 2 * LANE), F32),
                        pltpu.VMEM((H, LANE), F32)],
        compiler_params=_params(1))(sinks, proj, proj, proj, proj, proj, do, duv, bias_t)


def _bias_bwd(dbias, onehot):
    H = dbias.shape[0]
    nbk = onehot.shape[1]

    def body(d_ref, oh_ref, o_ref):
        hi, mid, lo = _split3(d_ref[...])
        oh = oh_ref[...]
        o_ref[...] = _dot(hi, oh, NN) + _dot(mid, oh, NN) + _dot(lo, oh, NN)

    return _CHAIN.call(body, name="bias_bwd", in_specs=[VMEM_SPEC] * 2, out_specs=VMEM_SPEC, out_shape=SDS((H, nbk), F32),
                       compiler_params=_params(0))(dbias, onehot)


def _inproj_bwd(dproj, w_t, x, dh1, g):
    T, P = dproj.shape
    D = x.shape[1]
    tm = _tile(T, 512)

    def body(dp_ref, w_ref, x_ref, dh_ref, g_ref, dx_ref, dg_ref):
        @pl.when(pl.program_id(0) == 0)
        def _():
            dg_ref[...] = jnp.zeros_like(dg_ref)

        dn = _dot(dp_ref[...], w_ref[...], NN)
        xv = x_ref[...]
        r = _rms_stats(xv)
        dg_ref[...] += jnp.sum(dn * (xv * r), axis=0, keepdims=True)
        dx_ref[...] = dh_ref[...] + _rms_bwd(dn, xv, r, g_ref[...])

    return _CHAIN.call(
        body, name="inproj_bwd", grid=(T // tm,),
        in_specs=[pl.BlockSpec((tm, P), lambda i: (i, 0)), _resident((P, D)),
                  pl.BlockSpec((tm, D), lambda i: (i, 0)), pl.BlockSpec((tm, D), lambda i: (i, 0)),
                  pl.BlockSpec((1, D), lambda i: (0, 0))],
        out_specs=[pl.BlockSpec((tm, D), lambda i: (i, 0)), pl.BlockSpec((1, D), lambda i: (0, 0))],
        out_shape=[SDS((T, D), F32), SDS((1, D), F32)], compiler_params=_params(1))(dproj, w_t, x, dh1, g)


def _adamw(w, g, m, v):
    m = ADAM_B1 * m + (1.0 - ADAM_B1) * g
    v = ADAM_B2 * v + (1.0 - ADAM_B2) * (g * g)
    m_hat = m / (1.0 - ADAM_B1 ** ADAM_STEP)
    v_hat = v / (1.0 - ADAM_B2 ** ADAM_STEP)
    delta = -ADAM_LR * (m_hat / (jnp.sqrt(v_hat) + ADAM_EPS) + ADAM_WD * w)
    return delta, m, v


def _adam_sharded(csum, recv, w, m, v, name):
    R, C = w.shape
    tr = _tile(R, 256, 16)

    def body(own_ref, recv_ref, w_ref, m_ref, v_ref, g_ref, d_ref, nm_ref, nv_ref):
        g = own_ref[...].astype(F32)
        for r in range(3):
            g = g + recv_ref[r].astype(F32)
        delta, nm, nv = _adamw(w_ref[...], g, m_ref[...], v_ref[...])
        g_ref[...] = g
        d_ref[...] = delta
        nm_ref[...] = nm
        nv_ref[...] = nv

    blk = pl.BlockSpec((tr, C), lambda i: (i, 0))
    return _CHAIN.call(
        body, name=name, grid=(R // tr,),
        in_specs=[pl.BlockSpec((None, tr, C), lambda i: (0, i, 0)), pl.BlockSpec((3, tr, C), lambda i: (0, i, 0)),
                  blk, blk, blk],
        out_specs=[blk] * 4, out_shape=[SDS((R, C), F32)] * 4, compiler_params=_params(1))(csum, recv, w, m, v)


def _adam_small(gathered, w, m, v):
    R = w.shape[0]

    def body(p_ref, w_ref, m_ref, v_ref, g_ref, d_ref, nm_ref, nv_ref):
        g = p_ref[0]
        for d in range(1, N_DEV):
            g = g + p_ref[d]
        delta, nm, nv = _adamw(w_ref[...], g, m_ref[...], v_ref[...])
        g_ref[...] = g
        d_ref[...] = delta
        nm_ref[...] = nm
        nv_ref[...] = nv

    return _CHAIN.call(body, name="adam_small", in_specs=[VMEM_SPEC] * 4, out_specs=[VMEM_SPEC] * 4,
                       out_shape=[SDS((R, LANE), F32)] * 4,
                       compiler_params=_params(0))(gathered, w, m, v)


def _pack(arrays):
    tile = 8 * LANE
    pieces = []
    for a in arrays:
        flat = a.reshape(-1).astype(F32)
        pieces.append(jnp.pad(flat, (0, (-flat.size) % tile)))
    return jnp.concatenate(pieces).reshape(-1, LANE)


def _unpack(packed, shapes):
    tile = 8 * LANE
    flat = packed.reshape(-1)
    out, off = [], 0
    for s in shapes:
        size = int(np.prod(s))
        out.append(flat[off:off + size].reshape(s))
        off += size + (-size) % tile
    return out


def kernel(x, rel_bias_table, mix_norm_g, w_in, gate_norm_g, gate_norm_b, w_spatial, b_spatial, attn_sinks, out_norm_a_g, out_norm_b_g, w_out, ffn_norm_g, w_up, w_down, final_norm_g, loss_target, m_rel_bias_table, m_mix_norm_g, m_w_in, m_gate_norm_g, m_gate_norm_b, m_w_spatial, m_b_spatial, m_attn_sinks, m_out_norm_a_g, m_out_norm_b_g, m_w_out, m_ffn_norm_g, m_w_up, m_w_down, m_final_norm_g, v_rel_bias_table, v_mix_norm_g, v_w_in, v_gate_norm_g, v_gate_norm_b, v_w_spatial, v_b_spatial, v_attn_sinks, v_out_norm_a_g, v_out_norm_b_g, v_w_out, v_ffn_norm_g, v_w_up, v_w_down, v_final_norm_g):
    T, D = x.shape[1], x.shape[2]
    A = D // 2
    B = D // 2
    G = A // GROUP_DIM
    H = B // HEAD_DIM
    P = 2 * A + B + 2 * KV_HEADS * HEAD_DIM
    xs = x.reshape(T, D)
    target = loss_target.reshape(T, D)

    win_t, m_win_t, v_win_t = (jnp.swapaxes(a[0], 0, 1) for a in (w_in, m_w_in, v_w_in))
    shards = [win_t.astype(BF16), w_out[0].astype(BF16), w_up[0].astype(BF16), w_down[0].astype(BF16)]
    _CHAIN.token = None
    gather = _gather_begin(shards, "gather_start")

    g1, g2, g3 = mix_norm_g.reshape(1, D), ffn_norm_g.reshape(1, D), final_norm_g.reshape(1, D)
    lg, lb = gate_norm_g.reshape(1, A), gate_norm_b.reshape(1, A)
    ws = w_spatial[0]
    ws_t = jnp.swapaxes(ws, 1, 2)
    bs_t = jnp.transpose(b_spatial[0])
    ga, gb = out_norm_a_g.reshape(1, A), out_norm_b_g.reshape(1, B)
    sinks = attn_sinks.reshape(H)
    bucket, in_window = _t5_bucket()
    onehot_np = ((bucket[:, :, None] == np.arange(N_BUCKETS)) & in_window[:, :, None]).astype(np.float32)
    onehot = jnp.asarray(onehot_np.reshape(-1, N_BUCKETS)).astype(BF16)
    onehot_kq = jnp.asarray(onehot_np.transpose(1, 0, 2).reshape(-1, N_BUCKETS)).astype(BF16)

    bias, bias_t = _bias_fwd(jnp.transpose(rel_bias_table), jnp.transpose(onehot), jnp.transpose(onehot_kq))
    bias, bias_t = bias.reshape(H, CHUNK, 2 * CHUNK), bias_t.reshape(H, 2 * CHUNK, CHUNK)
    n1 = _mix_norm(xs, g1)
    _gather_pass_on(gather, [0], "gather_in_pass")
    (win_g,) = _gather_end(gather, [0], "gather_in_end")
    win_t_full = win_g.reshape(P, D)
    proj = _inproj_fwd(n1, win_t_full)
    _gather_pass_on(gather, [1], "gather_out_pass")
    a_out = _gmlp_fwd(proj, lg, lb, ws, bs_t, A)
    b_out = _attn_fwd(proj, bias, sinks, A, B)
    _gather_pass_on(gather, [2], "gather_up_pass")
    (wout_g,) = _gather_end(gather, [1], "gather_out_end")
    wout_full = wout_g.reshape(A + B, D)
    h1, mixed = _outproj_fwd(a_out, b_out, ga, gb, xs, wout_full)
    (wup_g,) = _gather_end(gather, [2], "gather_up_end")
    wup_t = jnp.transpose(wup_g, (0, 2, 1)).reshape(-1, D)
    z, n2 = _ffn_up(h1, g2, wup_g)
    _gather_pass_on(gather, [3], "gather_down_pass")
    (wdown_g,) = _gather_end(gather, [3], "gather_down_end")
    h2 = _ffn_down(h1, z, wdown_g.reshape(-1, D))
    loss_part, dg3, dh2, dh2b = _final_loss(h2, g3, target)

    def reduce_to_chip(state, name):
        part, received = _sibling_exchange_end(state, name + "_sib_end")
        return _chip_exchange_begin(_chip_sum(part, received, name + "_chip_sum"), name + "_chip")

    dwdown = _matmul_tn(z, dh2b, "grad_w_down", square_a=True).reshape(wdown_g.shape)
    sib_down = _sibling_exchange_begin(dwdown, "rs_down_sib")
    dzp = _ffn_down_bwd(dh2b, z, wdown_g.reshape(-1, D))
    chip_down = reduce_to_chip(sib_down, "rs_down")
    dwup = _matmul_tn(n2, dzp, "grad_w_up", col_blocks=N_DEV)
    sib_up = _sibling_exchange_begin(dwup, "rs_up_sib")
    dh1, dh1b, dg2 = _ffn_norm_bwd(_ffn_up_bwd(dzp, wup_t), dh2, h1, g2)
    chip_up = reduce_to_chip(sib_up, "rs_up")
    da, db, dga, dgb = _outproj_bwd(dh1b, wout_full, a_out, b_out, ga, gb)
    dwout = _matmul_tn(mixed, dh1b, "grad_w_out").reshape(wout_g.shape)
    sib_out = _sibling_exchange_begin(dwout, "rs_out_sib")
    duv, dlg, dlb, dws, dbs_t = _gmlp_bwd(proj, da, lg, lb, ws, ws_t, bs_t, A)
    dproj, dbias_t, dsinks = _attn_bwd(proj, db, duv, bias_t, sinks, A, B)
    chip_out = reduce_to_chip(sib_out, "rs_out")
    dtable_t = _bias_bwd(dbias_t.reshape(H, -1), onehot_kq)
    dwin_t = _matmul_tn(dproj, n1, "grad_w_in").reshape(win_g.shape)
    sib_in = _sibling_exchange_begin(dwin_t, "rs_in_sib")
    grad_x, dg1 = _inproj_bwd(dproj, win_t_full, xs, dh1, g1)

    small_w = [rel_bias_table, mix_norm_g, gate_norm_g, gate_norm_b, w_spatial, b_spatial, attn_sinks,
               out_norm_a_g, out_norm_b_g, ffn_norm_g, final_norm_g]
    small_m = [m_rel_bias_table, m_mix_norm_g, m_gate_norm_g, m_gate_norm_b, m_w_spatial, m_b_spatial, m_attn_sinks,
               m_out_norm_a_g, m_out_norm_b_g, m_ffn_norm_g, m_final_norm_g]
    small_v = [v_rel_bias_table, v_mix_norm_g, v_gate_norm_g, v_gate_norm_b, v_w_spatial, v_b_spatial, v_attn_sinks,
               v_out_norm_a_g, v_out_norm_b_g, v_ffn_norm_g, v_final_norm_g]
    small_g = [jnp.transpose(dtable_t), dg1, dlg, dlb, dws, jnp.transpose(dbs_t), dsinks, dga, dgb, dg2, dg3]
    shapes = [w.shape for w in small_w]
    big = [None] * 4

    def adam_of(k, state, w, m, v):
        csum, received = _chip_exchange_end(state, "rs_%d_end" % k)
        big[k] = _adam_sharded(csum, received, w, m, v, "adam_%d" % k)

    small_gather = _gather_begin([_pack(small_g)], "small_gather_start")
    chip_in = reduce_to_chip(sib_in, "rs_in")
    _gather_pass_on(small_gather, [0], "small_gather_pass")
    adam_of(3, chip_down, w_down[0], m_w_down[0], v_w_down[0])
    (gathered,) = _gather_end(small_gather, [0], "small_gather_end")
    sg, sd, sm, sv = [_unpack(o, shapes) for o in _adam_small(gathered, _pack(small_w), _pack(small_m), _pack(small_v))]
    adam_of(2, chip_up, w_up[0], m_w_up[0], v_w_up[0])
    adam_of(1, chip_out, w_out[0], m_w_out[0], v_w_out[0])
    adam_of(0, chip_in, win_t, m_win_t, v_win_t)
    big[0] = [jnp.swapaxes(o, 0, 1) for o in big[0]]
    big = [[o.reshape(w.shape) for o in outs] for outs, w in zip(big, (w_in, w_out, w_up, w_down))]

    loss = lax.psum(loss_part[0, 0], ("x", "y", "c"))

    order = ["s0", "s1", "b0", "s2", "s3", "s4", "s5", "s6", "s7", "s8", "b1", "s9", "b2", "b3", "s10"]

    def group(idx):
        small = (sg, sd, sm, sv)[idx]
        return [small[int(t[1:])] if t[0] == "s" else big[int(t[1:])][idx] for t in order]

    return (loss, grad_x.reshape(x.shape), *group(0), *group(1), *group(2), *group(3))
```

```python
import functools
import math

import numpy as np
import jax
import jax.numpy as jnp
from jax import lax
from jax.experimental import pallas as pl
from jax.experimental.pallas import tpu as pltpu

F32 = jnp.float32
BF16 = jnp.bfloat16
SDS = jax.ShapeDtypeStruct
MESH = pl.DeviceIdType.MESH

N_DEV = 8
EPS = 1e-5
NEG = -1e30
CHUNK = 128
GROUP_DIM = 128
HEAD_DIM = 64
KV_HEADS = 2
N_BUCKETS = 32
MAX_DISTANCE = 128
ADAM_LR, ADAM_B1, ADAM_B2, ADAM_EPS, ADAM_WD, ADAM_STEP = 0.001, 0.9, 0.999, 1e-08, 0.01, 10
GELU_C0 = math.sqrt(2.0 / math.pi)
GELU_C1 = 0.044715

V7X_VMEM_BYTES = 64 * 1024 * 1024
VMEM_LIMIT = V7X_VMEM_BYTES - 8 * 1024 * 1024
LANE = 128

NN = ((1,), (0,))
NT = ((1,), (1,))
TN = ((0,), (0,))


def _dot(a, b, dims):
    return lax.dot_general(a, b, (dims, ((), ())), preferred_element_type=F32)


def _tile(n, pref, unit=LANE):
    best = None
    for t in range(unit, min(n, pref) + 1, unit):
        if n % t == 0:
            best = t
    return n if best is None else best


def _params(n_grid):
    return pltpu.CompilerParams(dimension_semantics=("arbitrary",) * n_grid, vmem_limit_bytes=VMEM_LIMIT)


def _resident(shape):
    return pl.BlockSpec(shape, lambda i: (0, 0), pipeline_mode=pl.Buffered(1))


def _gelu(x):
    return 0.5 * x * (1.0 + jnp.tanh(GELU_C0 * (x + GELU_C1 * x * x * x)))


def _gelu_and_grad(x):
    x2 = x * x
    t = jnp.tanh(GELU_C0 * x * (1.0 + GELU_C1 * x2))
    val = 0.5 * x * (1.0 + t)
    grad = 0.5 * (1.0 + t) + 0.5 * x * (1.0 - t * t) * (GELU_C0 * (1.0 + 3.0 * GELU_C1 * x2))
    return val, grad


def _rms_stats(x):
    return lax.rsqrt(jnp.mean(x * x, axis=-1, keepdims=True) + EPS)


def _rms_bwd(dy, x, r, g):
    w = dy * g
    return r * w - x * (r * r * r) * jnp.mean(w * x, axis=-1, keepdims=True)


def _t5_bucket():
    i = np.arange(CHUNK)[:, None]
    j = np.arange(2 * CHUNK)[None, :]
    rel = np.maximum(i + CHUNK - j, 0)
    n_exact = N_BUCKETS // 2
    relf = np.maximum(rel, n_exact).astype(np.float32)
    large = n_exact + (np.log(relf / np.float32(n_exact)) / np.float32(math.log(MAX_DISTANCE / n_exact))
                       * np.float32(N_BUCKETS - n_exact)).astype(np.int32)
    large = np.minimum(large, N_BUCKETS - 1)
    bucket = np.where(rel < n_exact, rel, large)
    in_window = (i + CHUNK - j >= 0) & (i + CHUNK - j < CHUNK)
    return bucket.astype(np.int32), in_window


def _split3(x):
    hi = x.astype(BF16)
    r1 = x - hi.astype(F32)
    mid = r1.astype(BF16)
    lo = (r1 - mid.astype(F32)).astype(BF16)
    return hi, mid, lo


HBM_SPEC = pl.BlockSpec(memory_space=pltpu.HBM)


def _mesh_pos():
    return lax.axis_index("x"), lax.axis_index("y"), lax.axis_index("c")


def _dev_index(px, py, pc):
    return 4 * px + 2 * py + pc


def _all_gather(shards, name):
    n = len(shards)

    def body(*refs):
        ins, outs = refs[:n], refs[n:2 * n]
        send_sems, recv_sems, local_sems = refs[2 * n:]
        x, y, c = _mesh_pos()
        me, sibling = (x, y, c), (x, y, 1 - c)
        chips = [(1 - x, y), (x, 1 - y), (1 - x, 1 - y)]

        def copy(a, k, block, to, src=None):
            dst = outs[a].at[_dev_index(*block)]
            return pltpu.make_async_remote_copy(
                src_ref=dst if src is None else src, dst_ref=dst,
                send_sem=send_sems.at[a * 7 + k], recv_sem=recv_sems.at[a * 7 + k],
                device_id=to, device_id_type=MESH)

        mine = [pltpu.make_async_copy(ins[a], outs[a].at[_dev_index(*me)], local_sems.at[a]) for a in range(n)]
        first = []
        for a in range(n):
            for j, chip in enumerate(chips):
                first.append(copy(a, 1 + j, me, (*chip, c), src=ins[a]))
            first.append(copy(a, 0, me, sibling, src=ins[a]))
        for cp in first:
            cp.start()
        for cp in mine:
            cp.start()
        passed = []
        for a in range(n):
            for j, chip in enumerate(chips):
                copy(a, 1 + j, (*chip, c), me).wait_recv()
                fwd = copy(a, 4 + j, (*chip, c), sibling)
                fwd.start()
                passed.append(fwd)
        for a in range(n):
            copy(a, 0, sibling, me).wait_recv()
            for j, chip in enumerate(chips):
                copy(a, 4 + j, (*chip, 1 - c), me).wait_recv()
        for cp in first + passed:
            cp.wait_send()
        for cp in mine:
            cp.wait()

    return _CHAIN.call(
        body, name=name,
        out_shape=[SDS((N_DEV,) + s.shape, s.dtype) for s in shards],
        in_specs=[HBM_SPEC] * n, out_specs=[HBM_SPEC] * n,
        scratch_shapes=[pltpu.SemaphoreType.DMA((7 * n,)), pltpu.SemaphoreType.DMA((7 * n,)),
                        pltpu.SemaphoreType.DMA((n,))],
    )(*shards)


SEM_SPEC = pl.BlockSpec(memory_space=pltpu.SEMAPHORE)
ANY_SPEC = pl.BlockSpec(memory_space=pl.ANY)
VMEM_SPEC = pl.BlockSpec(memory_space=pltpu.VMEM)
TOKEN_SPEC = VMEM_SPEC
TOKEN = SDS((8, LANE), F32)
SIDE_EFFECT = pltpu.SideEffectType.DATAFLOW_SIDE_EFFECTING


def _hbm(x):
    return pltpu.with_memory_space_constraint(x, pltpu.HBM)


class _CallChain:
    def __init__(self):
        self.token = None

    def call(self, body, *, in_specs, out_specs, out_shape, **kwargs):
        dep, n_in = self.token, len(in_specs)
        single = not isinstance(out_shape, (list, tuple))
        out_shapes = [out_shape] if single else list(out_shape)
        out_specs = [out_specs] if single else list(out_specs)
        n_out = len(out_shapes)
        n_dep = 0 if dep is None else 1
        token_spec = pl.BlockSpec((8, LANE), lambda *_: (0, 0)) if kwargs.get("grid") else VMEM_SPEC

        def chained(*refs):
            outs_at = n_in + n_dep
            body(*refs[:n_in], *refs[outs_at:outs_at + n_out], *refs[outs_at + n_out + 1:])
            token = refs[outs_at + n_out]
            token[...] = jnp.zeros_like(token)

        inner = pl.pallas_call(chained, in_specs=list(in_specs) + [ANY_SPEC] * n_dep, out_specs=out_specs + [token_spec],
                               out_shape=out_shapes + [TOKEN], **kwargs)

        def run(*operands):
            outs = inner(*operands) if dep is None else inner(*operands, dep)
            self.token = outs[n_out]
            return outs[0] if single else list(outs[:n_out])

        return run


_CHAIN = _CallChain()


def _split_start(bufs, copies_of, n_sems, name):
    n = len(bufs)

    def body(*refs):
        ins = refs[:n]
        send_sems, recv_sems = refs[n], refs[n + 1]
        for src, dst, k, target in copies_of(ins):
            pltpu.make_async_remote_copy(src_ref=src, dst_ref=dst, send_sem=send_sems.at[k], recv_sem=recv_sems.at[k],
                                         device_id=target, device_id_type=MESH).start()

    outs = _CHAIN.call(
        body, name=name,
        out_shape=[pltpu.SemaphoreType.DMA((n_sems,)), pltpu.SemaphoreType.DMA((n_sems,))]
        + [pltpu.HBM(b.shape, b.dtype) for b in bufs],
        in_specs=[HBM_SPEC] * n, out_specs=[SEM_SPEC, SEM_SPEC] + [HBM_SPEC] * n,
        input_output_aliases={a: 2 + a for a in range(n)},
        compiler_params=pltpu.CompilerParams(has_side_effects=SIDE_EFFECT),
    )(*[_hbm(b) for b in bufs])
    return outs[0], outs[1], list(outs[2:2 + n])


def _split_wait(bufs, sem_sets, waits_of, name):
    n, ns = len(bufs), len(sem_sets)
    flat_sems = [s for pair in sem_sets for s in pair]

    def body(*refs):
        ins = refs[:n]
        sems = refs[n:n + 2 * ns]
        x, y, c = _mesh_pos()
        for kind, src, dst, send_sem, recv_sem in waits_of(ins, [(sems[2 * i], sems[2 * i + 1]) for i in range(ns)]):
            cp = pltpu.make_async_remote_copy(src_ref=src, dst_ref=dst, send_sem=send_sem, recv_sem=recv_sem,
                                              device_id=(x, y, c), device_id_type=MESH)
            if kind == "send":
                cp.wait_send()
            else:
                cp.wait_recv()

    outs = _CHAIN.call(
        body, name=name,
        out_shape=[pltpu.HBM(b.shape, b.dtype) for b in bufs],
        in_specs=[HBM_SPEC] * n + [SEM_SPEC] * (2 * ns), out_specs=[HBM_SPEC] * n,
        input_output_aliases={a: a for a in range(n)},
        compiler_params=pltpu.CompilerParams(has_side_effects=SIDE_EFFECT),
    )(*bufs, *flat_sems)
    return list(outs)


def _gather_begin(shards, name):
    me = _dev_index(*_mesh_pos())
    lands = [lax.dynamic_update_index_in_dim(lax.empty((N_DEV,) + s.shape, s.dtype), s, me, 0) for s in shards]

    def copies_of(ins):
        x, y, c = _mesh_pos()
        targets = [(x, y, 1 - c), (1 - x, y, c), (x, 1 - y, c), (1 - x, 1 - y, c)]
        out = []
        for a, land in enumerate(ins):
            blk = land.at[_dev_index(x, y, c)]
            for k in (1, 2, 3, 0):
                out.append((blk, blk, 4 * a + k, targets[k]))
        return out

    send_sems, recv_sems, lands = _split_start(lands, copies_of, 4 * len(shards), name)
    return dict(lands=lands, sems=(send_sems, recv_sems), fwd={})


def _gather_pass_on(state, which, name):
    def arrivals(ins, sems):
        x, y, c = _mesh_pos()
        chips = [(1 - x, y), (x, 1 - y), (1 - x, 1 - y)]
        out = []
        for i, a in enumerate(which):
            for j, (px, py) in enumerate(chips):
                blk = ins[i].at[_dev_index(px, py, c)]
                out.append(("recv", blk, blk, sems[0][0].at[4 * a + 1 + j], sems[0][1].at[4 * a + 1 + j]))
        return out

    bufs = _split_wait([state["lands"][a] for a in which], [state["sems"]], arrivals, name + "_arrived")

    def copies_of(ins):
        x, y, c = _mesh_pos()
        chips = [(1 - x, y), (x, 1 - y), (1 - x, 1 - y)]
        out = []
        for i in range(len(which)):
            for j, (px, py) in enumerate(chips):
                blk = ins[i].at[_dev_index(px, py, c)]
                out.append((blk, blk, 3 * i + j, (x, y, 1 - c)))
        return out

    send_sems, recv_sems, bufs = _split_start(bufs, copies_of, 3 * len(which), name)
    for i, a in enumerate(which):
        state["lands"][a] = bufs[i]
    state["fwd"][tuple(which)] = (send_sems, recv_sems)


def _gather_end(state, which, name):
    def waits(ins, sems):
        x, y, c = _mesh_pos()
        chips = [(1 - x, y), (x, 1 - y), (1 - x, 1 - y)]
        (s_send, s_recv), (f_send, f_recv) = sems
        out = []
        for i, a in enumerate(which):
            mine = ins[i].at[_dev_index(x, y, c)]
            sib = ins[i].at[_dev_index(x, y, 1 - c)]
            out.append(("recv", sib, sib, s_send.at[4 * a], s_recv.at[4 * a]))
            for j, (px, py) in enumerate(chips):
                theirs = ins[i].at[_dev_index(px, py, 1 - c)]
                out.append(("recv", theirs, theirs, f_send.at[3 * i + j], f_recv.at[3 * i + j]))
            for k in range(4):
                out.append(("send", mine, mine, s_send.at[4 * a + k], s_recv.at[4 * a + k]))
            for j, (px, py) in enumerate(chips):
                passed = ins[i].at[_dev_index(px, py, c)]
                out.append(("send", passed, passed, f_send.at[3 * i + j], f_recv.at[3 * i + j]))
        return out

    bufs = _split_wait([state["lands"][a] for a in which], [state["sems"], state["fwd"][tuple(which)]], waits, name)
    for i, a in enumerate(which):
        state["lands"][a] = bufs[i]
    return bufs


def _sibling_exchange_begin(part, name):
    land = lax.empty((4,) + part.shape[1:], part.dtype)

    def copies_of(ins):
        x, y, c = _mesh_pos()
        return [(ins[0].at[2 * j + (1 - c)], ins[1].at[j], j, (x, y, 1 - c)) for j in range(4)]

    send_sems, recv_sems, bufs = _split_start([part, land], copies_of, 4, name)
    return dict(bufs=bufs, sems=(send_sems, recv_sems))


def _sibling_exchange_end(state, name):
    def waits(ins, sems):
        _, _, c = _mesh_pos()
        out = []
        for j in range(4):
            for kind in ("send", "recv"):
                out.append((kind, ins[0].at[2 * j + (1 - c)], ins[1].at[j], sems[0][0].at[j], sems[0][1].at[j]))
        return out

    return _split_wait(state["bufs"], [state["sems"]], waits, name)


CHIP_FLIPS = (2, 1, 3)


def _chip_exchange_begin(csum, name):
    land = lax.empty((3,) + csum.shape[1:], csum.dtype)

    def copies_of(ins):
        x, y, c = _mesh_pos()
        chips = [(1 - x, y), (x, 1 - y), (1 - x, 1 - y)]
        return [(ins[0].at[CHIP_FLIPS[r]], ins[1].at[r], r, (px, py, c)) for r, (px, py) in enumerate(chips)]

    send_sems, recv_sems, bufs = _split_start([csum, land], copies_of, 3, name)
    return dict(bufs=bufs, sems=(send_sems, recv_sems))


def _chip_exchange_end(state, name):
    def waits(ins, sems):
        out = []
        for r in range(3):
            for kind in ("send", "recv"):
                out.append((kind, ins[0].at[CHIP_FLIPS[r]], ins[1].at[r], sems[0][0].at[r], sems[0][1].at[r]))
        return out

    return _split_wait(state["bufs"], [state["sems"]], waits, name)


def _chip_sum(part, recv, name):
    _, R, C = part.shape
    tr = _tile(R, 512, 16)
    place = jnp.stack([lax.axis_index("c"), 2 * lax.axis_index("x") + lax.axis_index("y")]).astype(jnp.int32)

    def body(place_ref, p_ref, r_ref, o_ref):
        o_ref[...] = (p_ref[...].astype(F32) + r_ref[...].astype(F32)).astype(o_ref.dtype)

    def chip(p, place_ref):
        return jnp.bitwise_xor(p, place_ref[1])

    grid_spec = pltpu.PrefetchScalarGridSpec(
        num_scalar_prefetch=1, grid=(4, R // tr),
        in_specs=[pl.BlockSpec((None, tr, C), lambda p, i, place_ref: (2 * chip(p, place_ref) + place_ref[0], i, 0)),
                  pl.BlockSpec((None, tr, C), lambda p, i, place_ref: (chip(p, place_ref), i, 0))],
        out_specs=pl.BlockSpec((None, tr, C), lambda p, i, place_ref: (p, i, 0)))
    return pl.pallas_call(body, name=name, grid_spec=grid_spec, out_shape=SDS((4, R, C), part.dtype),
                          compiler_params=_params(2))(place, part, recv)


def _bias_fwd(table_t, onehot_t, onehot_kq_t):
    H = table_t.shape[0]
    n = onehot_t.shape[1]

    def body(t_ref, oh_ref, oh_kq_ref, o_ref, o_kq_ref):
        hi, mid, lo = _split3(t_ref[...])
        for src, dst in ((oh_ref, o_ref), (oh_kq_ref, o_kq_ref)):
            oh = src[...]
            dst[...] = _dot(hi, oh, NN) + _dot(mid, oh, NN) + _dot(lo, oh, NN)

    return _CHAIN.call(body, name="bias_fwd", in_specs=[VMEM_SPEC] * 3, out_specs=[VMEM_SPEC] * 2,
                       out_shape=[SDS((H, n), F32)] * 2, compiler_params=_params(0))(table_t, onehot_t, onehot_kq_t)


def _mix_norm(x, g):
    T, D = x.shape
    tm = _tile(T, 512)

    def body(x_ref, g_ref, n_ref):
        xv = x_ref[...]
        n_ref[...] = (xv * _rms_stats(xv) * g_ref[...]).astype(BF16)

    row = pl.BlockSpec((tm, D), lambda i: (i, 0))
    return _CHAIN.call(body, name="mix_norm", grid=(T // tm,), in_specs=[row, pl.BlockSpec((1, D), lambda i: (0, 0))],
                       out_specs=row, out_shape=SDS((T, D), BF16), compiler_params=_params(1))(x, g)


def _inproj_fwd(n, w_t):
    T, D = n.shape
    P = w_t.shape[0]
    tm = _tile(T, 512)

    def body(n_ref, w_ref, proj_ref):
        proj_ref[...] = _dot(n_ref[...], w_ref[...], NT)

    return _CHAIN.call(
        body, name="inproj_fwd", grid=(T // tm,),
        in_specs=[pl.BlockSpec((tm, D), lambda i: (i, 0)), _resident((P, D))],
        out_specs=pl.BlockSpec((tm, P), lambda i: (i, 0)),
        out_shape=SDS((T, P), F32), compiler_params=_params(1))(n, w_t)


def _layer_norm_group(vg, lg, lb):
    mu = jnp.mean(vg, axis=-1, keepdims=True)
    xc = vg - mu
    rstd = lax.rsqrt(jnp.mean(xc * xc, axis=-1, keepdims=True) + EPS)
    vhat = xc * rstd
    return vhat, rstd, vhat * lg + lb


def _gmlp_fwd(proj, lg, lb, w_s, bs_t, A):
    T = proj.shape[0]
    G = A // GROUP_DIM
    tm = _tile(T, 512)
    nc = tm // CHUNK

    def body(u_ref, v_ref, lg_ref, lb_ref, w_ref, bst_ref, a_ref):
        row = lax.broadcasted_iota(jnp.int32, (CHUNK, CHUNK), 0)
        col = lax.broadcasted_iota(jnp.int32, (CHUNK, CHUNK), 1)
        causal = row >= col
        for g in range(G):
            sl = slice(g * GROUP_DIM, (g + 1) * GROUP_DIM)
            _, _, vn = _layer_norm_group(_gelu(v_ref[:, sl]), lg_ref[:, sl], lb_ref[:, sl])
            vnb = vn.astype(BF16)
            wm = jnp.where(causal, w_ref[g], 0.0).astype(BF16)
            ug = _gelu(u_ref[:, sl])
            bcol = bst_ref[:, g:g + 1]
            for c in range(nc):
                rs = slice(c * CHUNK, (c + 1) * CHUNK)
                a_ref[rs, sl] = ug[rs] * (_dot(wm, vnb[rs], NN) + bcol)

    return _CHAIN.call(
        body, name="gmlp_fwd", grid=(T // tm,),
        in_specs=[pl.BlockSpec((tm, A), lambda i: (i, 0)), pl.BlockSpec((tm, A), lambda i: (i, 1)),
                  pl.BlockSpec((1, A), lambda i: (0, 0)), pl.BlockSpec((1, A), lambda i: (0, 0)),
                  pl.BlockSpec((G, CHUNK, CHUNK), lambda i: (0, 0, 0)), pl.BlockSpec((CHUNK, G), lambda i: (0, 0))],
        out_specs=pl.BlockSpec((tm, A), lambda i: (i, 0)),
        out_shape=SDS((T, A), F32), compiler_params=_params(1))(proj, proj, lg, lb, w_s, bs_t)


def _attn_masks(first_tile):
    ii = lax.broadcasted_iota(jnp.int32, (CHUNK, 2 * CHUNK), 0)
    jj = lax.broadcasted_iota(jnp.int32, (CHUNK, 2 * CHUNK), 1)
    in_window = (jj > ii) & (jj <= ii + CHUNK)
    first_mask = in_window & jnp.logical_or(jnp.logical_not(first_tile), jj >= CHUNK)
    return in_window, first_mask


def _softmax_with_sink(s, sink, axis):
    m = jnp.maximum(jnp.max(s, axis=axis, keepdims=True), sink)
    p = jnp.exp(s - m)
    e_sink = jnp.exp(sink - m)
    inv = 1.0 / (jnp.sum(p, axis=axis, keepdims=True) + e_sink)
    return p * inv, e_sink * inv


def _pad_heads(band, group):
    lane = lax.broadcasted_iota(jnp.int32, band.shape, 1)
    if group == 0:
        low = jnp.where(lane < HEAD_DIM, band, 0.0)
        high = pltpu.roll(low, HEAD_DIM, 1)
    else:
        high = jnp.where(lane >= HEAD_DIM, band, 0.0)
        low = pltpu.roll(high, HEAD_DIM, 1)
    return low.astype(BF16), high.astype(BF16)


def _attn_specs(tq, A, B, reverse_tiles=None):
    nb = tq // CHUNK
    kcol = (2 * A + B) // LANE
    if reverse_tiles is None:
        tile = lambda i: i
    else:
        tile = lambda i: reverse_tiles - 1 - i
    prev = lambda i: jnp.maximum(tile(i) * nb - 1, 0)
    return [pl.BlockSpec((tq, B), lambda i: (tile(i), 2 * A // B)),
            pl.BlockSpec((tq, LANE), lambda i: (tile(i), kcol)),
            pl.BlockSpec((tq, LANE), lambda i: (tile(i), kcol + 1)),
            pl.BlockSpec((CHUNK, LANE), lambda i: (prev(i), kcol)),
            pl.BlockSpec((CHUNK, LANE), lambda i: (prev(i), kcol + 1))]


def _attn_fwd(proj, bias, sinks, A, B):
    T = proj.shape[0]
    H = B // HEAD_DIM
    qpk = H // KV_HEADS
    tq = _tile(T, 512)
    nb = tq // CHUNK

    scale = HEAD_DIM ** -0.5

    def body(sink_ref, q_ref, k_ref, v_ref, kp_ref, vp_ref, bias_ref, o_ref):
        in_window, first_mask = _attn_masks(pl.program_id(0) == 0)
        for b in range(nb):
            rows = slice(b * CHUNK, (b + 1) * CHUNK)
            if b == 0:
                kprev, vprev, mask = kp_ref[...], vp_ref[...], first_mask
            else:
                prows = slice((b - 1) * CHUNK, b * CHUNK)
                kprev, vprev, mask = k_ref[prows, :], v_ref[prows, :], in_window
            kband = jnp.concatenate([kprev, k_ref[rows, :]], axis=0)
            vband = jnp.concatenate([vprev, v_ref[rows, :]], axis=0)
            k_pads = [_pad_heads(kband, g) for g in range(KV_HEADS)]
            v_both = [jnp.concatenate(_pad_heads(vband, g), axis=0) for g in range(KV_HEADS)]
            scores = []
            for pair in range(H // 2):
                h = 2 * pair
                qs = (q_ref[rows, h * HEAD_DIM:(h + 2) * HEAD_DIM] * scale).astype(BF16)
                scores += [_dot(qs, kz, NT) for kz in k_pads[h // qpk]]
            probs = [_softmax_with_sink(jnp.where(mask, s + bias_ref[h], NEG), sink_ref[h], -1)[0].astype(BF16)
                     for h, s in enumerate(scores)]
            outs = [_dot(jnp.concatenate(probs[h:h + 2], axis=1), v_both[h // qpk], NN) for h in range(0, H, 2)]
            o_ref[rows, :] = jnp.concatenate(outs, axis=1)

    return _CHAIN.call(
        body, name="attn_fwd", grid=(T // tq,),
        in_specs=[pl.BlockSpec(memory_space=pltpu.SMEM)] + _attn_specs(tq, A, B)
        + [pl.BlockSpec((H, CHUNK, 2 * CHUNK), lambda i: (0, 0, 0))],
        out_specs=pl.BlockSpec((tq, B), lambda i: (i, 0)),
        out_shape=SDS((T, B), F32), compiler_params=_params(1))(sinks, proj, proj, proj, proj, proj, bias)


def _outproj_fwd(a, b, ga, gb, x, w):
    T, A = a.shape
    B = b.shape[1]
    D = x.shape[1]
    tm = _tile(T, 512)

    def body(a_ref, b_ref, ga_ref, gb_ref, x_ref, w_ref, h_ref, mix_ref):
        av, bv = a_ref[...], b_ref[...]
        mix_ref[:, :A] = (av * _rms_stats(av) * ga_ref[...]).astype(BF16)
        mix_ref[:, A:] = (bv * _rms_stats(bv) * gb_ref[...]).astype(BF16)
        h_ref[...] = x_ref[...] + _dot(mix_ref[...], w_ref[...], NN)

    return _CHAIN.call(
        body, name="outproj_fwd", grid=(T // tm,),
        in_specs=[pl.BlockSpec((tm, A), lambda i: (i, 0)), pl.BlockSpec((tm, B), lambda i: (i, 0)),
                  pl.BlockSpec((1, A), lambda i: (0, 0)), pl.BlockSpec((1, B), lambda i: (0, 0)),
                  pl.BlockSpec((tm, D), lambda i: (i, 0)), _resident((A + B, D))],
        out_specs=[pl.BlockSpec((tm, D), lambda i: (i, 0)), pl.BlockSpec((tm, A + B), lambda i: (i, 0))],
        out_shape=[SDS((T, D), F32), SDS((T, A + B), BF16)], compiler_params=_params(1))(a, b, ga, gb, x, w)


def _ffn_up(h1, g, w_up):
    T, D = h1.shape
    Fb = w_up.shape[2]
    F = N_DEV * Fb
    tm, tf = _tile(T, 1024), _tile(Fb, 1024)
    per = Fb // tf

    def body(h_ref, g_ref, wu_ref, z_ref, n_ref, nbuf):
        @pl.when(pl.program_id(1) == 0)
        def _():
            hv = h_ref[...]
            n = (hv * _rms_stats(hv) * g_ref[...]).astype(BF16)
            nbuf[...] = n
            n_ref[...] = n

        z_ref[...] = jnp.maximum(_dot(nbuf[...], wu_ref[...], NN), 0.0).astype(BF16)

    return _CHAIN.call(
        body, name="ffn_up", grid=(T // tm, F // tf),
        in_specs=[pl.BlockSpec((tm, D), lambda i, j: (i, 0)), pl.BlockSpec((1, D), lambda i, j: (0, 0)),
                  pl.BlockSpec((None, D, tf), lambda i, j: (j // per, 0, j % per))],
        out_specs=[pl.BlockSpec((tm, tf), lambda i, j: (i, j)), pl.BlockSpec((tm, D), lambda i, j: (i, 0))],
        out_shape=[SDS((T, F), BF16), SDS((T, D), BF16)],
        scratch_shapes=[pltpu.VMEM((tm, D), BF16)], compiler_params=_params(2))(h1, g, w_up)


def _ffn_down(h1, z, w_down):
    T, D = h1.shape
    F = w_down.shape[0]
    tm, tn, tk = _tile(T, 1024), _tile(D, 1024), _tile(F, 4096)

    def body(h_ref, z_ref, wd_ref, h2_ref):
        k = pl.program_id(2)

        @pl.when(k == 0)
        def _():
            h2_ref[...] = h_ref[...]

        zf = z_ref[...].astype(F32)
        h2_ref[...] += _dot((zf * zf).astype(BF16), wd_ref[...], NN)

    return _CHAIN.call(
        body, name="ffn_down", grid=(T // tm, D // tn, F // tk),
        in_specs=[pl.BlockSpec((tm, tn), lambda i, j, k: (i, j)), pl.BlockSpec((tm, tk), lambda i, j, k: (i, k)),
                  pl.BlockSpec((tk, tn), lambda i, j, k: (k, j))],
        out_specs=pl.BlockSpec((tm, tn), lambda i, j, k: (i, j)),
        out_shape=SDS((T, D), F32), compiler_params=_params(3))(h1, z, w_down)


def _final_loss(h2, g, target):
    T, D = h2.shape
    tm = _tile(T, 512)

    def body(h_ref, g_ref, t_ref, loss_ref, dg_ref, dh_ref, dhb_ref):
        @pl.when(pl.program_id(0) == 0)
        def _():
            loss_ref[...] = jnp.zeros_like(loss_ref)
            dg_ref[...] = jnp.zeros_like(dg_ref)

        hv, gv = h_ref[...], g_ref[...]
        r = _rms_stats(hv)
        hn = hv * r
        e = hn * gv - t_ref[...]
        loss_ref[...] += (0.5 / D) * jnp.sum(jnp.sum(e * e, axis=0, keepdims=True), axis=-1, keepdims=True)
        dy = e * (1.0 / D)
        dg_ref[...] += jnp.sum(dy * hn, axis=0, keepdims=True)
        dh = _rms_bwd(dy, hv, r, gv)
        dh_ref[...] = dh
        dhb_ref[...] = dh.astype(BF16)

    return _CHAIN.call(
        body, name="final_loss", grid=(T // tm,),
        in_specs=[pl.BlockSpec((tm, D), lambda i: (i, 0)), pl.BlockSpec((1, D), lambda i: (0, 0)),
                  pl.BlockSpec((tm, D), lambda i: (i, 0))],
        out_specs=[pl.BlockSpec((1, 1), lambda i: (0, 0)), pl.BlockSpec((1, D), lambda i: (0, 0)),
                   pl.BlockSpec((tm, D), lambda i: (i, 0)), pl.BlockSpec((tm, D), lambda i: (i, 0))],
        out_shape=[SDS((1, 1), F32), SDS((1, D), F32), SDS((T, D), F32), SDS((T, D), BF16)],
        compiler_params=_params(1))(h2, g, target)


def _ffn_down_bwd(dh2b, z, w_down):
    T, D = dh2b.shape
    F = w_down.shape[0]
    tm, tf = _tile(T, 1024), _tile(F, 1024)

    def body(dh_ref, z_ref, wd_ref, dzp_ref):
        dzz = _dot(dh_ref[...], wd_ref[...], NT)
        dzp_ref[...] = (dzz * (2.0 * z_ref[...].astype(F32))).astype(BF16)

    return _CHAIN.call(
        body, name="ffn_down_bwd", grid=(T // tm, F // tf),
        in_specs=[pl.BlockSpec((tm, D), lambda i, j: (i, 0)), pl.BlockSpec((tm, tf), lambda i, j: (i, j)),
                  pl.BlockSpec((tf, D), lambda i, j: (j, 0))],
        out_specs=pl.BlockSpec((tm, tf), lambda i, j: (i, j)),
        out_shape=SDS((T, F), BF16), compiler_params=_params(2))(dh2b, z, w_down)


def _ffn_up_bwd(dzp, w_up_t):
    T, F = dzp.shape
    D = w_up_t.shape[1]
    tm, tn, tk = _tile(T, 1024), _tile(D, 1024), _tile(F, 4096)

    def body(dzp_ref, w_ref, dn_ref):
        part = _dot(dzp_ref[...], w_ref[...], NN)

        @pl.when(pl.program_id(2) == 0)
        def _():
            dn_ref[...] = part

        @pl.when(pl.program_id(2) > 0)
        def _():
            dn_ref[...] += part

    return _CHAIN.call(
        body, name="ffn_up_bwd", grid=(T // tm, D // tn, F // tk),
        in_specs=[pl.BlockSpec((tm, tk), lambda i, j, k: (i, k)), pl.BlockSpec((tk, tn), lambda i, j, k: (k, j))],
        out_specs=pl.BlockSpec((tm, tn), lambda i, j, k: (i, j)),
        out_shape=SDS((T, D), F32), compiler_params=_params(3))(dzp, w_up_t)


def _ffn_norm_bwd(dn, dh2, h1, g):
    T, D = h1.shape
    tm = _tile(T, 256)

    def body(dn_ref, dh_ref, h_ref, g_ref, dh1_ref, dh1b_ref, dg_ref):
        @pl.when(pl.program_id(0) == 0)
        def _():
            dg_ref[...] = jnp.zeros_like(dg_ref)

        hv, dnv = h_ref[...], dn_ref[...]
        r = _rms_stats(hv)
        dg_ref[...] += jnp.sum(dnv * (hv * r), axis=0, keepdims=True)
        dh1 = dh_ref[...] + _rms_bwd(dnv, hv, r, g_ref[...])
        dh1_ref[...] = dh1
        dh1b_ref[...] = dh1.astype(BF16)

    row = pl.BlockSpec((tm, D), lambda i: (i, 0))
    vec = pl.BlockSpec((1, D), lambda i: (0, 0))
    return _CHAIN.call(
        body, name="ffn_norm_bwd", grid=(T // tm,), in_specs=[row, row, row, vec], out_specs=[row, row, vec],
        out_shape=[SDS((T, D), F32), SDS((T, D), BF16), SDS((1, D), F32)], compiler_params=_params(1))(dn, dh2, h1, g)


def _matmul_tn(a, b, name, square_a=False, col_blocks=None):
    T, K = a.shape
    N = b.shape[1]
    tk = _tile(K, 1792)
    tn = _tile(N if col_blocks is None else N // col_blocks, 1024 if tk <= 1024 else 512)

    def body(a_ref, b_ref, o_ref):
        av = a_ref[...]
        if square_a:
            af = av.astype(F32)
            av = (af * af).astype(BF16)
        o_ref[...] = _dot(av, b_ref[...], TN).astype(o_ref.dtype)

    if col_blocks is None:
        out_shape = SDS((K, N), BF16)
        out_spec = pl.BlockSpec((tk, tn), lambda i, j: (i, j))
    else:
        per = (N // col_blocks) // tn
        out_shape = SDS((col_blocks, K, N // col_blocks), BF16)
        out_spec = pl.BlockSpec((None, tk, tn), lambda i, j: (j // per, i, j % per))
    return _CHAIN.call(
        body, name=name, grid=(K // tk, N // tn),
        in_specs=[pl.BlockSpec((T, tk), lambda i, j: (0, i)), pl.BlockSpec((T, tn), lambda i, j: (0, j))],
        out_specs=out_spec, out_shape=out_shape, compiler_params=_params(2))(a, b)


def _outproj_bwd(dh1b, w, a, b, ga, gb):
    T, D = dh1b.shape
    A, B = a.shape[1], b.shape[1]
    tm = _tile(T, 512)

    def body(dh_ref, w_ref, a_ref, b_ref, ga_ref, gb_ref, da_ref, db_ref, dga_ref, dgb_ref):
        @pl.when(pl.program_id(0) == 0)
        def _():
            dga_ref[...] = jnp.zeros_like(dga_ref)
            dgb_ref[...] = jnp.zeros_like(dgb_ref)

        dmix = _dot(dh_ref[...], w_ref[...], NT)
        for src_ref, g_ref, dx_ref, dg_ref, dn in ((a_ref, ga_ref, da_ref, dga_ref, dmix[:, :A]),
                                                   (b_ref, gb_ref, db_ref, dgb_ref, dmix[:, A:])):
            xv = src_ref[...]
            r = _rms_stats(xv)
            dg_ref[...] += jnp.sum(dn * (xv * r), axis=0, keepdims=True)
            dx_ref[...] = _rms_bwd(dn, xv, r, g_ref[...])

    return _CHAIN.call(
        body, name="outproj_bwd", grid=(T // tm,),
        in_specs=[pl.BlockSpec((tm, D), lambda i: (i, 0)), _resident((A + B, D)),
                  pl.BlockSpec((tm, A), lambda i: (i, 0)), pl.BlockSpec((tm, B), lambda i: (i, 0)),
                  pl.BlockSpec((1, A), lambda i: (0, 0)), pl.BlockSpec((1, B), lambda i: (0, 0))],
        out_specs=[pl.BlockSpec((tm, A), lambda i: (i, 0)), pl.BlockSpec((tm, B), lambda i: (i, 0)),
                   pl.BlockSpec((1, A), lambda i: (0, 0)), pl.BlockSpec((1, B), lambda i: (0, 0))],
        out_shape=[SDS((T, A), F32), SDS((T, B), F32), SDS((1, A), F32), SDS((1, B), F32)],
        compiler_params=_params(1))(dh1b, w, a, b, ga, gb)


def _gmlp_bwd(proj, da, lg, lb, w_s, w_st, bs_t, A):
    T = proj.shape[0]
    G = A // GROUP_DIM
    tm = _tile(T, 512)
    nc = tm // CHUNK

    def body(u_ref, v_ref, da_ref, lg_ref, lb_ref, w_ref, wt_ref, bst_ref, duv_ref, dlg_ref, dlb_ref, dw_ref, dbs_ref):
        @pl.when(pl.program_id(0) == 0)
        def _():
            dlg_ref[...] = jnp.zeros_like(dlg_ref)
            dlb_ref[...] = jnp.zeros_like(dlb_ref)
            dw_ref[...] = jnp.zeros_like(dw_ref)
            dbs_ref[...] = jnp.zeros_like(dbs_ref)

        row = lax.broadcasted_iota(jnp.int32, (CHUNK, CHUNK), 0)
        col = lax.broadcasted_iota(jnp.int32, (CHUNK, CHUNK), 1)
        lower = row >= col
        upper = row <= col
        for g in range(G):
            sl = slice(g * GROUP_DIM, (g + 1) * GROUP_DIM)
            lgv = lg_ref[:, sl]
            vg, vg_grad = _gelu_and_grad(v_ref[:, sl])
            vhat, rstd, vn = _layer_norm_group(vg, lgv, lb_ref[:, sl])
            vnb = vn.astype(BF16)
            ug, ug_grad = _gelu_and_grad(u_ref[:, sl])
            dav = da_ref[:, sl]
            wm = jnp.where(lower, w_ref[g], 0.0).astype(BF16)
            wmt = jnp.where(upper, wt_ref[g], 0.0).astype(BF16)
            bcol = bst_ref[:, g:g + 1]
            dw_acc = jnp.zeros((CHUNK, CHUNK), F32)
            dbs_acc = jnp.zeros((CHUNK, 1), F32)
            dvn_parts = []
            dug_parts = []
            for c in range(nc):
                rs = slice(c * CHUNK, (c + 1) * CHUNK)
                mixed = _dot(wm, vnb[rs], NN) + bcol
                dug_parts.append(dav[rs] * mixed)
                dmix = dav[rs] * ug[rs]
                dbs_acc = dbs_acc + jnp.sum(dmix, axis=-1, keepdims=True)
                dmixb = dmix.astype(BF16)
                dw_acc = dw_acc + _dot(dmixb, vnb[rs], NT)
                dvn_parts.append(_dot(wmt, dmixb, NN))
            dvn = jnp.concatenate(dvn_parts, axis=0)
            dug = jnp.concatenate(dug_parts, axis=0)
            dw_ref[g] += jnp.where(lower, dw_acc, 0.0)
            dbs_ref[:, g:g + 1] += dbs_acc
            dlg_ref[:, sl] += jnp.sum(dvn * vhat, axis=0, keepdims=True)
            dlb_ref[:, sl] += jnp.sum(dvn, axis=0, keepdims=True)
            dvhat = dvn * lgv
            dvg = rstd * (dvhat - jnp.mean(dvhat, axis=-1, keepdims=True)
                          - vhat * jnp.mean(dvhat * vhat, axis=-1, keepdims=True))
            duv_ref[:, sl] = (dug * ug_grad).astype(BF16)
            duv_ref[:, A + g * GROUP_DIM:A + (g + 1) * GROUP_DIM] = (dvg * vg_grad).astype(BF16)

    return _CHAIN.call(
        body, name="gmlp_bwd", grid=(T // tm,),
        in_specs=[pl.BlockSpec((tm, A), lambda i: (i, 0)), pl.BlockSpec((tm, A), lambda i: (i, 1)),
                  pl.BlockSpec((tm, A), lambda i: (i, 0)),
                  pl.BlockSpec((1, A), lambda i: (0, 0)), pl.BlockSpec((1, A), lambda i: (0, 0)),
                  pl.BlockSpec((G, CHUNK, CHUNK), lambda i: (0, 0, 0)),
                  pl.BlockSpec((G, CHUNK, CHUNK), lambda i: (0, 0, 0)), pl.BlockSpec((CHUNK, G), lambda i: (0, 0))],
        out_specs=[pl.BlockSpec((tm, 2 * A), lambda i: (i, 0)),
                   pl.BlockSpec((1, A), lambda i: (0, 0)), pl.BlockSpec((1, A), lambda i: (0, 0)),
                   pl.BlockSpec((G, CHUNK, CHUNK), lambda i: (0, 0, 0)), pl.BlockSpec((CHUNK, G), lambda i: (0, 0))],
        out_shape=[SDS((T, 2 * A), BF16), SDS((1, A), F32), SDS((1, A), F32),
                   SDS((G, CHUNK, CHUNK), F32), SDS((CHUNK, G), F32)],
        compiler_params=_params(1))(proj, proj, da, lg, lb, w_s, w_st, bs_t)


def _attn_bwd(proj, do, duv, bias_t, sinks, A, B):
    T, P = proj.shape
    H = B // HEAD_DIM
    qpk = H // KV_HEADS
    tq = _tile(T, 512)
    nb = tq // CHUNK
    n_tiles = T // tq
    scale = HEAD_DIM ** -0.5
    rev = lambda i: n_tiles - 1 - i

    def body(sink_ref, q_ref, k_ref, v_ref, kp_ref, vp_ref, do_ref, duv_ref, bias_ref,
             dproj_ref, dbias_ref, dsink_ref, carry, dkv, sacc):
        step = pl.program_id(0)

        @pl.when(step == 0)
        def _():
            carry[...] = jnp.zeros_like(carry)
            sacc[...] = jnp.zeros_like(sacc)
            dbias_ref[...] = jnp.zeros_like(dbias_ref)

        jj = lax.broadcasted_iota(jnp.int32, (2 * CHUNK, CHUNK), 0)
        ii = lax.broadcasted_iota(jnp.int32, (2 * CHUNK, CHUNK), 1)
        in_window = (jj > ii) & (jj <= ii + CHUNK)
        first_mask = in_window & jnp.logical_or(step != n_tiles - 1, jj >= CHUNK)
        low_query = lax.broadcasted_iota(jnp.int32, (CHUNK, LANE), 1) < HEAD_DIM
        low_key = lax.broadcasted_iota(jnp.int32, (2 * CHUNK, LANE), 1) < HEAD_DIM

        def split_pair(pair_bf16):
            zero = jnp.zeros_like(pair_bf16)
            return jnp.concatenate([jnp.where(low_query, pair_bf16, zero), jnp.where(low_query, zero, pair_bf16)], axis=0)

        dproj_ref[:, :2 * A] = duv_ref[...]
        dkv[...] = jnp.zeros_like(dkv)
        for b in range(nb):
            rows = slice(b * CHUNK, (b + 1) * CHUNK)
            band = slice(b * CHUNK, (b + 2) * CHUNK)
            if b == 0:
                kprev, vprev, mask = kp_ref[...], vp_ref[...], first_mask
            else:
                prows = slice((b - 1) * CHUNK, b * CHUNK)
                kprev, vprev, mask = k_ref[prows, :], v_ref[prows, :], in_window
            kband = jnp.concatenate([kprev, k_ref[rows, :]], axis=0)
            vband = jnp.concatenate([vprev, v_ref[rows, :]], axis=0)
            k_pads = [_pad_heads(kband, g) for g in range(KV_HEADS)]
            v_pads = [_pad_heads(vband, g) for g in range(KV_HEADS)]
            queries, douts, scores, dprobs = [], [], [], []
            for pair in range(H // 2):
                cols = slice(2 * pair * HEAD_DIM, (2 * pair + 2) * HEAD_DIM)
                qs = (q_ref[rows, cols] * scale).astype(BF16)
                dob = do_ref[rows, cols].astype(BF16)
                queries.append(qs)
                douts.append(dob)
                scores += [_dot(kz, qs, NT) for kz in k_pads[2 * pair // qpk]]
                dprobs += [_dot(vz, dob, NT) for vz in v_pads[2 * pair // qpk]]
            probs, dscores = [], []
            for h in range(H):
                pt, p_sink = _softmax_with_sink(jnp.where(mask, scores[h] + bias_ref[h], NEG), sink_ref[h], 0)
                delta = jnp.sum(pt * dprobs[h], axis=0, keepdims=True)
                dst = pt * (dprobs[h] - delta)
                dbias_ref[h] += dst
                sacc[h:h + 1, :] += -(p_sink * delta)
                probs.append(pt.astype(BF16))
                dscores.append(dst.astype(BF16))
            dq_parts, dk_groups, dv_groups = [], [], []
            for g in range(KV_HEADS):
                k_both = jnp.concatenate(k_pads[g], axis=0)
                dk_acc = jnp.zeros((2 * CHUNK, LANE), F32)
                dv_acc = jnp.zeros((2 * CHUNK, LANE), F32)
                for pair in range(g * qpk // 2, (g + 1) * qpk // 2):
                    pair_heads = slice(2 * pair, 2 * pair + 2)
                    dk_acc = dk_acc + _dot(jnp.concatenate(dscores[pair_heads], axis=1), split_pair(queries[pair]), NN)
                    dv_acc = dv_acc + _dot(jnp.concatenate(probs[pair_heads], axis=1), split_pair(douts[pair]), NN)
                    dq_parts.append(_dot(jnp.concatenate(dscores[pair_heads], axis=0), k_both, TN) * scale)
                dk_groups.append(dk_acc + pltpu.roll(dk_acc, HEAD_DIM, 1))
                dv_groups.append(dv_acc + pltpu.roll(dv_acc, HEAD_DIM, 1))
            dkv[band, :LANE] += jnp.where(low_key, dk_groups[0], dk_groups[1])
            dkv[band, LANE:] += jnp.where(low_key, dv_groups[0], dv_groups[1])
            dproj_ref[rows, 2 * A:2 * A + B] = jnp.concatenate(dq_parts, axis=1).astype(BF16)
        last = slice(tq, tq + CHUNK)
        dkv[last, :] += carry[...]
        dproj_ref[:, 2 * A + B:] = dkv[CHUNK:, :].astype(BF16)
        carry[...] = dkv[:CHUNK, :]

        @pl.when(step == n_tiles - 1)
        def _():
            dsink_ref[...] = jnp.sum(sacc[...], axis=1, keepdims=True)

    specs = _attn_specs(tq, A, B, reverse_tiles=n_tiles)
    return _CHAIN.call(
        body, name="attn_bwd", grid=(n_tiles,),
        in_specs=[pl.BlockSpec(memory_space=pltpu.SMEM)] + specs
        + [pl.BlockSpec((tq, B), lambda i: (rev(i), 0)), pl.BlockSpec((tq, 2 * A), lambda i: (rev(i), 0)),
           pl.BlockSpec((H, 2 * CHUNK, CHUNK), lambda i: (0, 0, 0))],
        out_specs=[pl.BlockSpec((tq, P), lambda i: (rev(i), 0)),
                   pl.BlockSpec((H, 2 * CHUNK, CHUNK), lambda i: (0, 0, 0)), pl.BlockSpec((H, 1), lambda i: (0, 0))],
        out_shape=[SDS((T, P), BF16), SDS((H, 2 * CHUNK, CHUNK), F32), SDS((H, 1), F32)],
        scratch_shapes=[pltpu.VMEM((CHUNK, 2 * LANE), F32), pltpu.VMEM((tq + CHUNK, 2 * LANE), F32),
                        pltpu.VMEM((H, LANE), F32)],
        compiler_params=_params(1))(sinks, proj, proj, proj, proj, proj, do, duv, bias_t)


def _bias_bwd(dbias, onehot):
    H = dbias.shape[0]
    nbk = onehot.shape[1]

    def body(d_ref, oh_ref, o_ref):
        hi, mid, lo = _split3(d_ref[...])
        oh = oh_ref[...]
        o_ref[...] = _dot(hi, oh, NN) + _dot(mid, oh, NN) + _dot(lo, oh, NN)

    return _CHAIN.call(body, name="bias_bwd", in_specs=[VMEM_SPEC] * 2, out_specs=VMEM_SPEC, out_shape=SDS((H, nbk), F32),
                       compiler_params=_params(0))(dbias, onehot)


def _inproj_bwd(dproj, w_t, x, dh1, g):
    T, P = dproj.shape
    D = x.shape[1]
    tm = _tile(T, 512)

    def body(dp_ref, w_ref, x_ref, dh_ref, g_ref, dx_ref, dg_ref):
        @pl.when(pl.program_id(0) == 0)
        def _():
            dg_ref[...] = jnp.zeros_like(dg_ref)

        dn = _dot(dp_ref[...], w_ref[...], NN)
        xv = x_ref[...]
        r = _rms_stats(xv)
        dg_ref[...] += jnp.sum(dn * (xv * r), axis=0, keepdims=True)
        dx_ref[...] = dh_ref[...] + _rms_bwd(dn, xv, r, g_ref[...])

    return _CHAIN.call(
        body, name="inproj_bwd", grid=(T // tm,),
        in_specs=[pl.BlockSpec((tm, P), lambda i: (i, 0)), _resident((P, D)),
                  pl.BlockSpec((tm, D), lambda i: (i, 0)), pl.BlockSpec((tm, D), lambda i: (i, 0)),
                  pl.BlockSpec((1, D), lambda i: (0, 0))],
        out_specs=[pl.BlockSpec((tm, D), lambda i: (i, 0)), pl.BlockSpec((1, D), lambda i: (0, 0))],
        out_shape=[SDS((T, D), F32), SDS((1, D), F32)], compiler_params=_params(1))(dproj, w_t, x, dh1, g)


def _adamw(w, g, m, v):
    m = ADAM_B1 * m + (1.0 - ADAM_B1) * g
    v = ADAM_B2 * v + (1.0 - ADAM_B2) * (g * g)
    m_hat = m / (1.0 - ADAM_B1 ** ADAM_STEP)
    v_hat = v / (1.0 - ADAM_B2 ** ADAM_STEP)
    delta = -ADAM_LR * (m_hat / (jnp.sqrt(v_hat) + ADAM_EPS) + ADAM_WD * w)
    return delta, m, v


def _adam_sharded(csum, recv, w, m, v, name):
    R, C = w.shape
    tr = _tile(R, 256, 16)

    def body(own_ref, recv_ref, w_ref, m_ref, v_ref, g_ref, d_ref, nm_ref, nv_ref):
        g = own_ref[...].astype(F32)
        for r in range(3):
            g = g + recv_ref[r].astype(F32)
        delta, nm, nv = _adamw(w_ref[...], g, m_ref[...], v_ref[...])
        g_ref[...] = g
        d_ref[...] = delta
        nm_ref[...] = nm
        nv_ref[...] = nv

    blk = pl.BlockSpec((tr, C), lambda i: (i, 0))
    return _CHAIN.call(
        body, name=name, grid=(R // tr,),
        in_specs=[pl.BlockSpec((None, tr, C), lambda i: (0, i, 0)), pl.BlockSpec((3, tr, C), lambda i: (0, i, 0)),
                  blk, blk, blk],
        out_specs=[blk] * 4, out_shape=[SDS((R, C), F32)] * 4, compiler_params=_params(1))(csum, recv, w, m, v)


def _adam_small(gathered, w, m, v):
    R = w.shape[0]

    def body(p_ref, w_ref, m_ref, v_ref, g_ref, d_ref, nm_ref, nv_ref):
        g = p_ref[0]
        for d in range(1, N_DEV):
            g = g + p_ref[d]
        delta, nm, nv = _adamw(w_ref[...], g, m_ref[...], v_ref[...])
        g_ref[...] = g
        d_ref[...] = delta
        nm_ref[...] = nm
        nv_ref[...] = nv

    return _CHAIN.call(body, name="adam_small", in_specs=[VMEM_SPEC] * 4, out_specs=[VMEM_SPEC] * 4,
                       out_shape=[SDS((R, LANE), F32)] * 4,
                       compiler_params=_params(0))(gathered, w, m, v)


def _pack(arrays):
    tile = 8 * LANE
    pieces = []
    for a in arrays:
        flat = a.reshape(-1).astype(F32)
        pieces.append(jnp.pad(flat, (0, (-flat.size) % tile)))
    return jnp.concatenate(pieces).reshape(-1, LANE)


def _unpack(packed, shapes):
    tile = 8 * LANE
    flat = packed.reshape(-1)
    out, off = [], 0
    for s in shapes:
        size = int(np.prod(s))
        out.append(flat[off:off + size].reshape(s))
        off += size + (-size) % tile
    return out


def kernel(x, rel_bias_table, mix_norm_g, w_in, gate_norm_g, gate_norm_b, w_spatial, b_spatial, attn_sinks, out_norm_a_g, out_norm_b_g, w_out, ffn_norm_g, w_up, w_down, final_norm_g, loss_target, m_rel_bias_table, m_mix_norm_g, m_w_in, m_gate_norm_g, m_gate_norm_b, m_w_spatial, m_b_spatial, m_attn_sinks, m_out_norm_a_g, m_out_norm_b_g, m_w_out, m_ffn_norm_g, m_w_up, m_w_down, m_final_norm_g, v_rel_bias_table, v_mix_norm_g, v_w_in, v_gate_norm_g, v_gate_norm_b, v_w_spatial, v_b_spatial, v_attn_sinks, v_out_norm_a_g, v_out_norm_b_g, v_w_out, v_ffn_norm_g, v_w_up, v_w_down, v_final_norm_g):
    T, D = x.shape[1], x.shape[2]
    A = D // 2
    B = D // 2
    G = A // GROUP_DIM
    H = B // HEAD_DIM
    P = 2 * A + B + 2 * KV_HEADS * HEAD_DIM
    xs = x.reshape(T, D)
    target = loss_target.reshape(T, D)

    win_t, m_win_t, v_win_t = (jnp.swapaxes(a[0], 0, 1) for a in (w_in, m_w_in, v_w_in))
    shards = [win_t.astype(BF16), w_out[0].astype(BF16), w_up[0].astype(BF16), w_down[0].astype(BF16)]
    _CHAIN.token = None
    gather = _gather_begin(shards, "gather_start")

    g1, g2, g3 = mix_norm_g.reshape(1, D), ffn_norm_g.reshape(1, D), final_norm_g.reshape(1, D)
    lg, lb = gate_norm_g.reshape(1, A), gate_norm_b.reshape(1, A)
    ws = w_spatial[0]
    ws_t = jnp.swapaxes(ws, 1, 2)
    bs_t = jnp.transpose(b_spatial[0])
    ga, gb = out_norm_a_g.reshape(1, A), out_norm_b_g.reshape(1, B)
    sinks = attn_sinks.reshape(H)
    bucket, in_window = _t5_bucket()
    onehot_np = ((bucket[:, :, None] == np.arange(N_BUCKETS)) & in_window[:, :, None]).astype(np.float32)
    onehot = jnp.asarray(onehot_np.reshape(-1, N_BUCKETS)).astype(BF16)
    onehot_kq = jnp.asarray(onehot_np.transpose(1, 0, 2).reshape(-1, N_BUCKETS)).astype(BF16)

    bias, bias_t = _bias_fwd(jnp.transpose(rel_bias_table), jnp.transpose(onehot), jnp.transpose(onehot_kq))
    bias, bias_t = bias.reshape(H, CHUNK, 2 * CHUNK), bias_t.reshape(H, 2 * CHUNK, CHUNK)
    n1 = _mix_norm(xs, g1)
    _gather_pass_on(gather, [0], "gather_in_pass")
    (win_g,) = _gather_end(gather, [0], "gather_in_end")
    win_t_full = win_g.reshape(P, D)
    proj = _inproj_fwd(n1, win_t_full)
    _gather_pass_on(gather, [1], "gather_out_pass")
    a_out = _gmlp_fwd(proj, lg, lb, ws, bs_t, A)
    b_out = _attn_fwd(proj, bias, sinks, A, B)
    _gather_pass_on(gather, [2], "gather_up_pass")
    (wout_g,) = _gather_end(gather, [1], "gather_out_end")
    wout_full = wout_g.reshape(A + B, D)
    h1, mixed = _outproj_fwd(a_out, b_out, ga, gb, xs, wout_full)
    (wup_g,) = _gather_end(gather, [2], "gather_up_end")
    wup_t = jnp.transpose(wup_g, (0, 2, 1)).reshape(-1, D)
    z, n2 = _ffn_up(h1, g2, wup_g)
    _gather_pass_on(gather, [3], "gather_down_pass")
    (wdown_g,) = _gather_end(gather, [3], "gather_down_end")
    h2 = _ffn_down(h1, z, wdown_g.reshape(-1, D))
    loss_part, dg3, dh2, dh2b = _final_loss(h2, g3, target)

    def reduce_to_chip(state, name):
        part, received = _sibling_exchange_end(state, name + "_sib_end")
        return _chip_exchange_begin(_chip_sum(part, received, name + "_chip_sum"), name + "_chip")

    dwdown = _matmul_tn(z, dh2b, "grad_w_down", square_a=True).reshape(wdown_g.shape)
    sib_down = _sibling_exchange_begin(dwdown, "rs_down_sib")
    dzp = _ffn_down_bwd(dh2b, z, wdown_g.reshape(-1, D))
    chip_down = reduce_to_chip(sib_down, "rs_down")
    dwup = _matmul_tn(n2, dzp, "grad_w_up", col_blocks=N_DEV)
    sib_up = _sibling_exchange_begin(dwup, "rs_up_sib")
    dh1, dh1b, dg2 = _ffn_norm_bwd(_ffn_up_bwd(dzp, wup_t), dh2, h1, g2)
    chip_up = reduce_to_chip(sib_up, "rs_up")
    da, db, dga, dgb = _outproj_bwd(dh1b, wout_full, a_out, b_out, ga, gb)
    dwout = _matmul_tn(mixed, dh1b, "grad_w_out").reshape(wout_g.shape)
    sib_out = _sibling_exchange_begin(dwout, "rs_out_sib")
    duv, dlg, dlb, dws, dbs_t = _gmlp_bwd(proj, da, lg, lb, ws, ws_t, bs_t, A)
    dproj, dbias_t, dsinks = _attn_bwd(proj, db, duv, bias_t, sinks, A, B)
    chip_out = reduce_to_chip(sib_out, "rs_out")
    dtable_t = _bias_bwd(dbias_t.reshape(H, -1), onehot_kq)
    dwin_t = _matmul_tn(dproj, n1, "grad_w_in").reshape(win_g.shape)
    sib_in = _sibling_exchange_begin(dwin_t, "rs_in_sib")
    grad_x, dg1 = _inproj_bwd(dproj, win_t_full, xs, dh1, g1)

    small_w = [rel_bias_table, mix_norm_g, gate_norm_g, gate_norm_b, w_spatial, b_spatial, attn_sinks,
               out_norm_a_g, out_norm_b_g, ffn_norm_g, final_norm_g]
    small_m = [m_rel_bias_table, m_mix_norm_g, m_gate_norm_g, m_gate_norm_b, m_w_spatial, m_b_spatial, m_attn_sinks,
               m_out_norm_a_g, m_out_norm_b_g, m_ffn_norm_g, m_final_norm_g]
    small_v = [v_rel_bias_table, v_mix_norm_g, v_gate_norm_g, v_gate_norm_b, v_w_spatial, v_b_spatial, v_attn_sinks,
               v_out_norm_a_g, v_out_norm_b_g, v_ffn_norm_g, v_final_norm_g]
    small_g = [jnp.transpose(dtable_t), dg1, dlg, dlb, dws, jnp.transpose(dbs_t), dsinks, dga, dgb, dg2, dg3]
    shapes = [w.shape for w in small_w]
    big = [None] * 4

    def adam_of(k, state, w, m, v):
        csum, received = _chip_exchange_end(state, "rs_%d_end" % k)
        big[k] = _adam_sharded(csum, received, w, m, v, "adam_%d" % k)

    small_gather = _gather_begin([_pack(small_g)], "small_gather_start")
    chip_in = reduce_to_chip(sib_in, "rs_in")
    _gather_pass_on(small_gather, [0], "small_gather_pass")
    adam_of(3, chip_down, w_down[0], m_w_down[0], v_w_down[0])
    (gathered,) = _gather_end(small_gather, [0], "small_gather_end")
    sg, sd, sm, sv = [_unpack(o, shapes) for o in _adam_small(gathered, _pack(small_w), _pack(small_m), _pack(small_v))]
    adam_of(2, chip_up, w_up[0], m_w_up[0], v_w_up[0])
    adam_of(1, chip_out, w_out[0], m_w_out[0], v_w_out[0])
    adam_of(0, chip_in, win_t, m_win_t, v_win_t)
    big[0] = [jnp.swapaxes(o, 0, 1) for o in big[0]]
    big = [[o.reshape(w.shape) for o in outs] for outs, w in zip(big, (w_in, w_out, w_up, w_down))]

    loss = lax.psum(loss_part[0, 0], ("x", "y", "c"))

    order = ["s0", "s1", "b0", "s2", "s3", "s4", "s5", "s6", "s7", "s8", "b1", "s9", "b2", "b3", "s10"]

    def group(idx):
        small = (sg, sd, sm, sv)[idx]
        return [small[int(t[1:])] if t[0] == "s" else big[int(t[1:])][idx] for t in order]

    return (loss, grad_x.reshape(x.shape), *group(0), *group(1), *group(2), *group(3))
```

```python
import functools
import math

import numpy as np
import jax
import jax.numpy as jnp
from jax import lax
from jax.experimental import pallas as pl
from jax.experimental.pallas import tpu as pltpu

F32 = jnp.float32
BF16 = jnp.bfloat16
SDS = jax.ShapeDtypeStruct
MESH = pl.DeviceIdType.MESH

N_DEV = 8
EPS = 1e-5
NEG = -1e30
CHUNK = 128
GROUP_DIM = 128
HEAD_DIM = 64
KV_HEADS = 2
N_BUCKETS = 32
MAX_DISTANCE = 128
ADAM_LR, ADAM_B1, ADAM_B2, ADAM_EPS, ADAM_WD, ADAM_STEP = 0.001, 0.9, 0.999, 1e-08, 0.01, 10
GELU_C0 = math.sqrt(2.0 / math.pi)
GELU_C1 = 0.044715

V7X_VMEM_BYTES = 64 * 1024 * 1024
VMEM_LIMIT = V7X_VMEM_BYTES - 8 * 1024 * 1024
LANE = 128

NN = ((1,), (0,))
NT = ((1,), (1,))
TN = ((0,), (0,))


def _dot(a, b, dims):
    return lax.dot_general(a, b, (dims, ((), ())), preferred_element_type=F32)


def _tile(n, pref, unit=LANE):
    best = None
    for t in range(unit, min(n, pref) + 1, unit):
        if n % t == 0:
            best = t
    return n if best is None else best


def _params(n_grid):
    return pltpu.CompilerParams(dimension_semantics=("arbitrary",) * n_grid, vmem_limit_bytes=VMEM_LIMIT)


def _resident(shape):
    return pl.BlockSpec(shape, lambda i: (0, 0), pipeline_mode=pl.Buffered(1))


def _gelu(x):
    return 0.5 * x * (1.0 + jnp.tanh(GELU_C0 * (x + GELU_C1 * x * x * x)))


def _gelu_and_grad(x):
    x2 = x * x
    t = jnp.tanh(GELU_C0 * x * (1.0 + GELU_C1 * x2))
    val = 0.5 * x * (1.0 + t)
    grad = 0.5 * (1.0 + t) + 0.5 * x * (1.0 - t * t) * (GELU_C0 * (1.0 + 3.0 * GELU_C1 * x2))
    return val, grad


def _rms_stats(x):
    return lax.rsqrt(jnp.mean(x * x, axis=-1, keepdims=True) + EPS)


def _rms_bwd(dy, x, r, g):
    w = dy * g
    return r * w - x * (r * r * r) * jnp.mean(w * x, axis=-1, keepdims=True)


def _t5_bucket():
    i = np.arange(CHUNK)[:, None]
    j = np.arange(2 * CHUNK)[None, :]
    rel = np.maximum(i + CHUNK - j, 0)
    n_exact = N_BUCKETS // 2
    relf = np.maximum(rel, n_exact).astype(np.float32)
    large = n_exact + (np.log(relf / np.float32(n_exact)) / np.float32(math.log(MAX_DISTANCE / n_exact))
                       * np.float32(N_BUCKETS - n_exact)).astype(np.int32)
    large = np.minimum(large, N_BUCKETS - 1)
    bucket = np.where(rel < n_exact, rel, large)
    in_window = (i + CHUNK - j >= 0) & (i + CHUNK - j < CHUNK)
    return bucket.astype(np.int32), in_window


def _split3(x):
    hi = x.astype(BF16)
    r1 = x - hi.astype(F32)
    mid = r1.astype(BF16)
    lo = (r1 - mid.astype(F32)).astype(BF16)
    return hi, mid, lo


HBM_SPEC = pl.BlockSpec(memory_space=pltpu.HBM)


def _mesh_pos():
    return lax.axis_index("x"), lax.axis_index("y"), lax.axis_index("c")


def _dev_index(px, py, pc):
    return 4 * px + 2 * py + pc


def _all_gather(shards, name):
    n = len(shards)

    def body(*refs):
        ins, outs = refs[:n], refs[n:2 * n]
        send_sems, recv_sems, local_sems = refs[2 * n:]
        x, y, c = _mesh_pos()
        me, sibling = (x, y, c), (x, y, 1 - c)
        chips = [(1 - x, y), (x, 1 - y), (1 - x, 1 - y)]

        def copy(a, k, block, to, src=None):
            dst = outs[a].at[_dev_index(*block)]
            return pltpu.make_async_remote_copy(
                src_ref=dst if src is None else src, dst_ref=dst,
                send_sem=send_sems.at[a * 7 + k], recv_sem=recv_sems.at[a * 7 + k],
                device_id=to, device_id_type=MESH)

        mine = [pltpu.make_async_copy(ins[a], outs[a].at[_dev_index(*me)], local_sems.at[a]) for a in range(n)]
        first = []
        for a in range(n):
            for j, chip in enumerate(chips):
                first.append(copy(a, 1 + j, me, (*chip, c), src=ins[a]))
            first.append(copy(a, 0, me, sibling, src=ins[a]))
        for cp in first:
            cp.start()
        for cp in mine:
            cp.start()
        passed = []
        for a in range(n):
            for j, chip in enumerate(chips):
                copy(a, 1 + j, (*chip, c), me).wait_recv()
                fwd = copy(a, 4 + j, (*chip, c), sibling)
                fwd.start()
                passed.append(fwd)
        for a in range(n):
            copy(a, 0, sibling, me).wait_recv()
            for j, chip in enumerate(chips):
                copy(a, 4 + j, (*chip, 1 - c), me).wait_recv()
        for cp in first + passed:
            cp.wait_send()
        for cp in mine:
            cp.wait()

    return _CHAIN.call(
        body, name=name,
        out_shape=[SDS((N_DEV,) + s.shape, s.dtype) for s in shards],
        in_specs=[HBM_SPEC] * n, out_specs=[HBM_SPEC] * n,
        scratch_shapes=[pltpu.SemaphoreType.DMA((7 * n,)), pltpu.SemaphoreType.DMA((7 * n,)),
                        pltpu.SemaphoreType.DMA((n,))],
    )(*shards)


SEM_SPEC = pl.BlockSpec(memory_space=pltpu.SEMAPHORE)
ANY_SPEC = pl.BlockSpec(memory_space=pl.ANY)
VMEM_SPEC = pl.BlockSpec(memory_space=pltpu.VMEM)
TOKEN_SPEC = VMEM_SPEC
TOKEN = SDS((8, LANE), F32)
SIDE_EFFECT = pltpu.SideEffectType.DATAFLOW_SIDE_EFFECTING


def _hbm(x):
    return pltpu.with_memory_space_constraint(x, pltpu.HBM)


class _CallChain:
    def __init__(self):
        self.token = None

    def call(self, body, *, in_specs, out_specs, out_shape, **kwargs):
        dep, n_in = self.token, len(in_specs)
        single = not isinstance(out_shape, (list, tuple))
        out_shapes = [out_shape] if single else list(out_shape)
        out_specs = [out_specs] if single else list(out_specs)
        n_out = len(out_shapes)
        n_dep = 0 if dep is None else 1
        token_spec = pl.BlockSpec((8, LANE), lambda *_: (0, 0)) if kwargs.get("grid") else VMEM_SPEC

        def chained(*refs):
            outs_at = n_in + n_dep
            body(*refs[:n_in], *refs[outs_at:outs_at + n_out], *refs[outs_at + n_out + 1:])
            token = refs[outs_at + n_out]
            token[...] = jnp.zeros_like(token)

        inner = pl.pallas_call(chained, in_specs=list(in_specs) + [ANY_SPEC] * n_dep, out_specs=out_specs + [token_spec],
                               out_shape=out_shapes + [TOKEN], **kwargs)

        def run(*operands):
            outs = inner(*operands) if dep is None else inner(*operands, dep)
            self.token = outs[n_out]
            return outs[0] if single else list(outs[:n_out])

        return run


_CHAIN = _CallChain()


def _split_start(bufs, copies_of, n_sems, name):
    n = len(bufs)

    def body(*refs):
        ins = refs[:n]
        send_sems, recv_sems = refs[n], refs[n + 1]
        for src, dst, k, target in copies_of(ins):
            pltpu.make_async_remote_copy(src_ref=src, dst_ref=dst, send_sem=send_sems.at[k], recv_sem=recv_sems.at[k],
                                         device_id=target, device_id_type=MESH).start()

    outs = _CHAIN.call(
        body, name=name,
        out_shape=[pltpu.SemaphoreType.DMA((n_sems,)), pltpu.SemaphoreType.DMA((n_sems,))]
        + [pltpu.HBM(b.shape, b.dtype) for b in bufs],
        in_specs=[HBM_SPEC] * n, out_specs=[SEM_SPEC, SEM_SPEC] + [HBM_SPEC] * n,
        input_output_aliases={a: 2 + a for a in range(n)},
        compiler_params=pltpu.CompilerParams(has_side_effects=SIDE_EFFECT),
    )(*[_hbm(b) for b in bufs])
    return outs[0], outs[1], list(outs[2:2 + n])


def _split_wait(bufs, sem_sets, waits_of, name):
    n, ns = len(bufs), len(sem_sets)
    flat_sems = [s for pair in sem_sets for s in pair]

    def body(*refs):
        ins = refs[:n]
        sems = refs[n:n + 2 * ns]
        x, y, c = _mesh_pos()
        for kind, src, dst, send_sem, recv_sem in waits_of(ins, [(sems[2 * i], sems[2 * i + 1]) for i in range(ns)]):
            cp = pltpu.make_async_remote_copy(src_ref=src, dst_ref=dst, send_sem=send_sem, recv_sem=recv_sem,
                                              device_id=(x, y, c), device_id_type=MESH)
            if kind == "send":
                cp.wait_send()
            else:
                cp.wait_recv()

    outs = _CHAIN.call(
        body, name=name,
        out_shape=[pltpu.HBM(b.shape, b.dtype) for b in bufs],
        in_specs=[HBM_SPEC] * n + [SEM_SPEC] * (2 * ns), out_specs=[HBM_SPEC] * n,
        input_output_aliases={a: a for a in range(n)},
        compiler_params=pltpu.CompilerParams(has_side_effects=SIDE_EFFECT),
    )(*bufs, *flat_sems)
    return list(outs)


def _gather_begin(shards, name):
    me = _dev_index(*_mesh_pos())
    lands = [lax.dynamic_update_index_in_dim(lax.empty((N_DEV,) + s.shape, s.dtype), s, me, 0) for s in shards]

    def copies_of(ins):
        x, y, c = _mesh_pos()
        targets = [(x, y, 1 - c), (1 - x, y, c), (x, 1 - y, c), (1 - x, 1 - y, c)]
        out = []
        for a, land in enumerate(ins):
            blk = land.at[_dev_index(x, y, c)]
            for k in (1, 2, 3, 0):
                out.append((blk, blk, 4 * a + k, targets[k]))
        return out

    send_sems, recv_sems, lands = _split_start(lands, copies_of, 4 * len(shards), name)
    return dict(lands=lands, sems=(send_sems, recv_sems), fwd={})


def _gather_pass_on(state, which, name):
    def arrivals(ins, sems):
        x, y, c = _mesh_pos()
        chips = [(1 - x, y), (x, 1 - y), (1 - x, 1 - y)]
        out = []
        for i, a in enumerate(which):
            for j, (px, py) in enumerate(chips):
                blk = ins[i].at[_dev_index(px, py, c)]
                out.append(("recv", blk, blk, sems[0][0].at[4 * a + 1 + j], sems[0][1].at[4 * a + 1 + j]))
        return out

    bufs = _split_wait([state["lands"][a] for a in which], [state["sems"]], arrivals, name + "_arrived")

    def copies_of(ins):
        x, y, c = _mesh_pos()
        chips = [(1 - x, y), (x, 1 - y), (1 - x, 1 - y)]
        out = []
        for i in range(len(which)):
            for j, (px, py) in enumerate(chips):
                blk = ins[i].at[_dev_index(px, py, c)]
                out.append((blk, blk, 3 * i + j, (x, y, 1 - c)))
        return out

    send_sems, recv_sems, bufs = _split_start(bufs, copies_of, 3 * len(which), name)
    for i, a in enumerate(which):
        state["lands"][a] = bufs[i]
    state["fwd"][tuple(which)] = (send_sems, recv_sems)


def _gather_end(state, which, name):
    def waits(ins, sems):
        x, y, c = _mesh_pos()
        chips = [(1 - x, y), (x, 1 - y), (1 - x, 1 - y)]
        (s_send, s_recv), (f_send, f_recv) = sems
        out = []
        for i, a in enumerate(which):
            mine = ins[i].at[_dev_index(x, y, c)]
            sib = ins[i].at[_dev_index(x, y, 1 - c)]
            out.append(("recv", sib, sib, s_send.at[4 * a], s_recv.at[4 * a]))
            for j, (px, py) in enumerate(chips):
                theirs = ins[i].at[_dev_index(px, py, 1 - c)]
                out.append(("recv", theirs, theirs, f_send.at[3 * i + j], f_recv.at[3 * i + j]))
            for k in range(4):
                out.append(("send", mine, mine, s_send.at[4 * a + k], s_recv.at[4 * a + k]))
            for j, (px, py) in enumerate(chips):
                passed = ins[i].at[_dev_index(px, py, c)]
                out.append(("send", passed, passed, f_send.at[3 * i + j], f_recv.at[3 * i + j]))
        return out

    bufs = _split_wait([state["lands"][a] for a in which], [state["sems"], state["fwd"][tuple(which)]], waits, name)
    for i, a in enumerate(which):
        state["lands"][a] = bufs[i]
    return bufs


def _sibling_exchange_begin(part, name):
    land = lax.empty((4,) + part.shape[1:], part.dtype)

    def copies_of(ins):
        x, y, c = _mesh_pos()
        return [(ins[0].at[2 * j + (1 - c)], ins[1].at[j], j, (x, y, 1 - c)) for j in range(4)]

    send_sems, recv_sems, bufs = _split_start([part, land], copies_of, 4, name)
    return dict(bufs=bufs, sems=(send_sems, recv_sems))


def _sibling_exchange_end(state, name):
    def waits(ins, sems):
        _, _, c = _mesh_pos()
        out = []
        for j in range(4):
            for kind in ("send", "recv"):
                out.append((kind, ins[0].at[2 * j + (1 - c)], ins[1].at[j], sems[0][0].at[j], sems[0][1].at[j]))
        return out

    return _split_wait(state["bufs"], [state["sems"]], waits, name)


CHIP_FLIPS = (2, 1, 3)


def _chip_exchange_begin(csum, name):
    land = lax.empty((3,) + csum.shape[1:], csum.dtype)

    def copies_of(ins):
        x, y, c = _mesh_pos()
        chips = [(1 - x, y), (x, 1 - y), (1 - x, 1 - y)]
        return [(ins[0].at[CHIP_FLIPS[r]], ins[1].at[r], r, (px, py, c)) for r, (px, py) in enumerate(chips)]

    send_sems, recv_sems, bufs = _split_start([csum, land], copies_of, 3, name)
    return dict(bufs=bufs, sems=(send_sems, recv_sems))


def _chip_exchange_end(state, name):
    def waits(ins, sems):
        out = []
        for r in range(3):
            for kind in ("send", "recv"):
                out.append((kind, ins[0].at[CHIP_FLIPS[r]], ins[1].at[r], sems[0][0].at[r], sems[0][1].at[r]))
        return out

    return _split_wait(state["bufs"], [state["sems"]], waits, name)


def _chip_sum(part, recv, name):
    _, R, C = part.shape
    tr = _tile(R, 512, 16)
    place = jnp.stack([lax.axis_index("c"), 2 * lax.axis_index("x") + lax.axis_index("y")]).astype(jnp.int32)

    def body(place_ref, p_ref, r_ref, o_ref):
        o_ref[...] = (p_ref[...].astype(F32) + r_ref[...].astype(F32)).astype(o_ref.dtype)

    def chip(p, place_ref):
        return jnp.bitwise_xor(p, place_ref[1])

    grid_spec = pltpu.PrefetchScalarGridSpec(
        num_scalar_prefetch=1, grid=(4, R // tr),
        in_specs=[pl.BlockSpec((None, tr, C), lambda p, i, place_ref: (2 * chip(p, place_ref) + place_ref[0], i, 0)),
                  pl.BlockSpec((None, tr, C), lambda p, i, place_ref: (chip(p, place_ref), i, 0))],
        out_specs=pl.BlockSpec((None, tr, C), lambda p, i, place_ref: (p, i, 0)))
    return pl.pallas_call(body, name=name, grid_spec=grid_spec, out_shape=SDS((4, R, C), part.dtype),
                          compiler_params=_params(2))(place, part, recv)


def _bias_fwd(table_t, onehot_t, onehot_kq_t):
    H = table_t.shape[0]
    n = onehot_t.shape[1]

    def body(t_ref, oh_ref, oh_kq_ref, o_ref, o_kq_ref):
        hi, mid, lo = _split3(t_ref[...])
        for src, dst in ((oh_ref, o_ref), (oh_kq_ref, o_kq_ref)):
            oh = src[...]
            dst[...] = _dot(hi, oh, NN) + _dot(mid, oh, NN) + _dot(lo, oh, NN)

    return _CHAIN.call(body, name="bias_fwd", in_specs=[VMEM_SPEC] * 3, out_specs=[VMEM_SPEC] * 2,
                       out_shape=[SDS((H, n), F32)] * 2, compiler_params=_params(0))(table_t, onehot_t, onehot_kq_t)


def _mix_norm(x, g):
    T, D = x.shape
    tm = _tile(T, 512)

    def body(x_ref, g_ref, n_ref):
        xv = x_ref[...]
        n_ref[...] = (xv * _rms_stats(xv) * g_ref[...]).astype(BF16)

    row = pl.BlockSpec((tm, D), lambda i: (i, 0))
    return _CHAIN.call(body, name="mix_norm", grid=(T // tm,), in_specs=[row, pl.BlockSpec((1, D), lambda i: (0, 0))],
                       out_specs=row, out_shape=SDS((T, D), BF16), compiler_params=_params(1))(x, g)


def _inproj_fwd(n, w_t):
    T, D = n.shape
    P = w_t.shape[0]
    tm = _tile(T, 512)

    def body(n_ref, w_ref, proj_ref):
        proj_ref[...] = _dot(n_ref[...], w_ref[...], NT)

    return _CHAIN.call(
        body, name="inproj_fwd", grid=(T // tm,),
        in_specs=[pl.BlockSpec((tm, D), lambda i: (i, 0)), _resident((P, D))],
        out_specs=pl.BlockSpec((tm, P), lambda i: (i, 0)),
        out_shape=SDS((T, P), F32), compiler_params=_params(1))(n, w_t)


def _layer_norm_group(vg, lg, lb):
    mu = jnp.mean(vg, axis=-1, keepdims=True)
    xc = vg - mu
    rstd = lax.rsqrt(jnp.mean(xc * xc, axis=-1, keepdims=True) + EPS)
    vhat = xc * rstd
    return vhat, rstd, vhat * lg + lb


def _gmlp_fwd(proj, lg, lb, w_s, bs_t, A):
    T = proj.shape[0]
    G = A // GROUP_DIM
    tm = _tile(T, 512)
    nc = tm // CHUNK

    def body(u_ref, v_ref, lg_ref, lb_ref, w_ref, bst_ref, a_ref):
        row = lax.broadcasted_iota(jnp.int32, (CHUNK, CHUNK), 0)
        col = lax.broadcasted_iota(jnp.int32, (CHUNK, CHUNK), 1)
        causal = row >= col
        for g in range(G):
            sl = slice(g * GROUP_DIM, (g + 1) * GROUP_DIM)
            _, _, vn = _layer_norm_group(_gelu(v_ref[:, sl]), lg_ref[:, sl], lb_ref[:, sl])
            vnb = vn.astype(BF16)
            wm = jnp.where(causal, w_ref[g], 0.0).astype(BF16)
            ug = _gelu(u_ref[:, sl])
            bcol = bst_ref[:, g:g + 1]
            for c in range(nc):
                rs = slice(c * CHUNK, (c + 1) * CHUNK)
                a_ref[rs, sl] = ug[rs] * (_dot(wm, vnb[rs], NN) + bcol)

    return _CHAIN.call(
        body, name="gmlp_fwd", grid=(T // tm,),
        in_specs=[pl.BlockSpec((tm, A), lambda i: (i, 0)), pl.BlockSpec((tm, A), lambda i: (i, 1)),
                  pl.BlockSpec((1, A), lambda i: (0, 0)), pl.BlockSpec((1, A), lambda i: (0, 0)),
                  pl.BlockSpec((G, CHUNK, CHUNK), lambda i: (0, 0, 0)), pl.BlockSpec((CHUNK, G), lambda i: (0, 0))],
        out_specs=pl.BlockSpec((tm, A), lambda i: (i, 0)),
        out_shape=SDS((T, A), F32), compiler_params=_params(1))(proj, proj, lg, lb, w_s, bs_t)


def _attn_masks(first_tile):
    ii = lax.broadcasted_iota(jnp.int32, (CHUNK, 2 * CHUNK), 0)
    jj = lax.broadcasted_iota(jnp.int32, (CHUNK, 2 * CHUNK), 1)
    in_window = (jj > ii) & (jj <= ii + CHUNK)
    first_mask = in_window & jnp.logical_or(jnp.logical_not(first_tile), jj >= CHUNK)
    return in_window, first_mask


def _softmax_with_sink(s, sink, axis):
    m = jnp.maximum(jnp.max(s, axis=axis, keepdims=True), sink)
    p = jnp.exp(s - m)
    e_sink = jnp.exp(sink - m)
    inv = 1.0 / (jnp.sum(p, axis=axis, keepdims=True) + e_sink)
    return p * inv, e_sink * inv


def _pad_heads(band, group):
    lane = lax.broadcasted_iota(jnp.int32, band.shape, 1)
    if group == 0:
        low = jnp.where(lane < HEAD_DIM, band, 0.0)
        high = pltpu.roll(low, HEAD_DIM, 1)
    else:
        high = jnp.where(lane >= HEAD_DIM, band, 0.0)
        low = pltpu.roll(high, HEAD_DIM, 1)
    return low.astype(BF16), high.astype(BF16)


def _attn_specs(tq, A, B, reverse_tiles=None):
    nb = tq // CHUNK
    kcol = (2 * A + B) // LANE
    if reverse_tiles is None:
        tile = lambda i: i
    else:
        tile = lambda i: reverse_tiles - 1 - i
    prev = lambda i: jnp.maximum(tile(i) * nb - 1, 0)
    return [pl.BlockSpec((tq, B), lambda i: (tile(i), 2 * A // B)),
            pl.BlockSpec((tq, LANE), lambda i: (tile(i), kcol)),
            pl.BlockSpec((tq, LANE), lambda i: (tile(i), kcol + 1)),
            pl.BlockSpec((CHUNK, LANE), lambda i: (prev(i), kcol)),
            pl.BlockSpec((CHUNK, LANE), lambda i: (prev(i), kcol + 1))]


def _attn_fwd(proj, bias, sinks, A, B):
    T = proj.shape[0]
    H = B // HEAD_DIM
    qpk = H // KV_HEADS
    tq = _tile(T, 512)
    nb = tq // CHUNK

    scale = HEAD_DIM ** -0.5

    def body(sink_ref, q_ref, k_ref, v_ref, kp_ref, vp_ref, bias_ref, o_ref):
        in_window, first_mask = _attn_masks(pl.program_id(0) == 0)
        for b in range(nb):
            rows = slice(b * CHUNK, (b + 1) * CHUNK)
            if b == 0:
                kprev, vprev, mask = kp_ref[...], vp_ref[...], first_mask
            else:
                prows = slice((b - 1) * CHUNK, b * CHUNK)
                kprev, vprev, mask = k_ref[prows, :], v_ref[prows, :], in_window
            kband = jnp.concatenate([kprev, k_ref[rows, :]], axis=0)
            vband = jnp.concatenate([vprev, v_ref[rows, :]], axis=0)
            k_pads = [_pad_heads(kband, g) for g in range(KV_HEADS)]
            v_both = [jnp.concatenate(_pad_heads(vband, g), axis=0) for g in range(KV_HEADS)]
            scores = []
            for pair in range(H // 2):
                h = 2 * pair
                qs = (q_ref[rows, h * HEAD_DIM:(h + 2) * HEAD_DIM] * scale).astype(BF16)
                scores += [_dot(qs, kz, NT) for kz in k_pads[h // qpk]]
            probs = [_softmax_with_sink(jnp.where(mask, s + bias_ref[h], NEG), sink_ref[h], -1)[0].astype(BF16)
                     for h, s in enumerate(scores)]
            outs = [_dot(jnp.concatenate(probs[h:h + 2], axis=1), v_both[h // qpk], NN) for h in range(0, H, 2)]
            o_ref[rows, :] = jnp.concatenate(outs, axis=1)

    return _CHAIN.call(
        body, name="attn_fwd", grid=(T // tq,),
        in_specs=[pl.BlockSpec(memory_space=pltpu.SMEM)] + _attn_specs(tq, A, B)
        + [pl.BlockSpec((H, CHUNK, 2 * CHUNK), lambda i: (0, 0, 0))],
        out_specs=pl.BlockSpec((tq, B), lambda i: (i, 0)),
        out_shape=SDS((T, B), F32), compiler_params=_params(1))(sinks, proj, proj, proj, proj, proj, bias)


def _outproj_fwd(a, b, ga, gb, x, w, g_ffn):
    T, A = a.shape
    B = b.shape[1]
    D = x.shape[1]
    tm = _tile(T, 512)

    def body(a_ref, b_ref, ga_ref, gb_ref, x_ref, w_ref, gf_ref, h_ref, mix_ref, n_ref):
        av, bv = a_ref[...], b_ref[...]
        mix_ref[:, :A] = (av * _rms_stats(av) * ga_ref[...]).astype(BF16)
        mix_ref[:, A:] = (bv * _rms_stats(bv) * gb_ref[...]).astype(BF16)
        hv = x_ref[...] + _dot(mix_ref[...], w_ref[...], NN)
        h_ref[...] = hv
        n_ref[...] = (hv * _rms_stats(hv) * gf_ref[...]).astype(BF16)

    row = pl.BlockSpec((tm, D), lambda i: (i, 0))
    return _CHAIN.call(
        body, name="outproj_fwd", grid=(T // tm,),
        in_specs=[pl.BlockSpec((tm, A), lambda i: (i, 0)), pl.BlockSpec((tm, B), lambda i: (i, 0)),
                  pl.BlockSpec((1, A), lambda i: (0, 0)), pl.BlockSpec((1, B), lambda i: (0, 0)),
                  row, _resident((A + B, D)), pl.BlockSpec((1, D), lambda i: (0, 0))],
        out_specs=[row, pl.BlockSpec((tm, A + B), lambda i: (i, 0)), row],
        out_shape=[SDS((T, D), F32), SDS((T, A + B), BF16), SDS((T, D), BF16)],
        compiler_params=_params(1))(a, b, ga, gb, x, w, g_ffn)


def _ffn_up(n, w_up):
    T, D = n.shape
    Fb = w_up.shape[2]
    F = N_DEV * Fb
    tm, tf = _tile(T, 1024), _tile(Fb, 1024)
    per = Fb // tf

    def body(n_ref, wu_ref, z_ref):
        z_ref[...] = jnp.maximum(_dot(n_ref[...], wu_ref[...], NN), 0.0).astype(BF16)

    return _CHAIN.call(
        body, name="ffn_up", grid=(T // tm, F // tf),
        in_specs=[pl.BlockSpec((tm, D), lambda i, j: (i, 0)),
                  pl.BlockSpec((None, D, tf), lambda i, j: (j // per, 0, j % per))],
        out_specs=pl.BlockSpec((tm, tf), lambda i, j: (i, j)),
        out_shape=SDS((T, F), BF16), compiler_params=_params(2))(n, w_up)


def _ffn_down(h1, z, w_down):
    T, D = h1.shape
    F = w_down.shape[0]
    tm, tn, tk = _tile(T, 1024), _tile(D, 1024), _tile(F, 4096)

    def body(h_ref, z_ref, wd_ref, h2_ref):
        k = pl.program_id(2)

        @pl.when(k == 0)
        def _():
            h2_ref[...] = h_ref[...]

        zf = z_ref[...].astype(F32)
        h2_ref[...] += _dot((zf * zf).astype(BF16), wd_ref[...], NN)

    return _CHAIN.call(
        body, name="ffn_down", grid=(T // tm, D // tn, F // tk),
        in_specs=[pl.BlockSpec((tm, tn), lambda i, j, k: (i, j)), pl.BlockSpec((tm, tk), lambda i, j, k: (i, k)),
                  pl.BlockSpec((tk, tn), lambda i, j, k: (k, j))],
        out_specs=pl.BlockSpec((tm, tn), lambda i, j, k: (i, j)),
        out_shape=SDS((T, D), F32), compiler_params=_params(3))(h1, z, w_down)


def _final_loss(h2, g, target):
    T, D = h2.shape
    tm = _tile(T, 512)

    def body(h_ref, g_ref, t_ref, loss_ref, dg_ref, dh_ref, dhb_ref):
        @pl.when(pl.program_id(0) == 0)
        def _():
            loss_ref[...] = jnp.zeros_like(loss_ref)
            dg_ref[...] = jnp.zeros_like(dg_ref)

        hv, gv = h_ref[...], g_ref[...]
        r = _rms_stats(hv)
        hn = hv * r
        e = hn * gv - t_ref[...]
        loss_ref[...] += (0.5 / D) * jnp.sum(jnp.sum(e * e, axis=0, keepdims=True), axis=-1, keepdims=True)
        dy = e * (1.0 / D)
        dg_ref[...] += jnp.sum(dy * hn, axis=0, keepdims=True)
        dh = _rms_bwd(dy, hv, r, gv)
        dh_ref[...] = dh
        dhb_ref[...] = dh.astype(BF16)

    return _CHAIN.call(
        body, name="final_loss", grid=(T // tm,),
        in_specs=[pl.BlockSpec((tm, D), lambda i: (i, 0)), pl.BlockSpec((1, D), lambda i: (0, 0)),
                  pl.BlockSpec((tm, D), lambda i: (i, 0))],
        out_specs=[pl.BlockSpec((1, 1), lambda i: (0, 0)), pl.BlockSpec((1, D), lambda i: (0, 0)),
                   pl.BlockSpec((tm, D), lambda i: (i, 0)), pl.BlockSpec((tm, D), lambda i: (i, 0))],
        out_shape=[SDS((1, 1), F32), SDS((1, D), F32), SDS((T, D), F32), SDS((T, D), BF16)],
        compiler_params=_params(1))(h2, g, target)


def _ffn_down_bwd(dh2b, z, w_down):
    T, D = dh2b.shape
    F = w_down.shape[0]
    tm, tf = _tile(T, 1024), _tile(F, 1024)

    def body(dh_ref, z_ref, wd_ref, dzp_ref):
        dzz = _dot(dh_ref[...], wd_ref[...], NT)
        dzp_ref[...] = (dzz * (2.0 * z_ref[...].astype(F32))).astype(BF16)

    return _CHAIN.call(
        body, name="ffn_down_bwd", grid=(T // tm, F // tf),
        in_specs=[pl.BlockSpec((tm, D), lambda i, j: (i, 0)), pl.BlockSpec((tm, tf), lambda i, j: (i, j)),
                  pl.BlockSpec((tf, D), lambda i, j: (j, 0))],
        out_specs=pl.BlockSpec((tm, tf), lambda i, j: (i, j)),
        out_shape=SDS((T, F), BF16), compiler_params=_params(2))(dh2b, z, w_down)


def _ffn_up_bwd(dzp, w_up_t):
    T, F = dzp.shape
    D = w_up_t.shape[1]
    tm, tn, tk = _tile(T, 1024), _tile(D, 1024), _tile(F, 4096)

    def body(dzp_ref, w_ref, dn_ref):
        part = _dot(dzp_ref[...], w_ref[...], NN)

        @pl.when(pl.program_id(2) == 0)
        def _():
            dn_ref[...] = part

        @pl.when(pl.program_id(2) > 0)
        def _():
            dn_ref[...] += part

    return _CHAIN.call(
        body, name="ffn_up_bwd", grid=(T // tm, D // tn, F // tk),
        in_specs=[pl.BlockSpec((tm, tk), lambda i, j, k: (i, k)), pl.BlockSpec((tk, tn), lambda i, j, k: (k, j))],
        out_specs=pl.BlockSpec((tm, tn), lambda i, j, k: (i, j)),
        out_shape=SDS((T, D), F32), compiler_params=_params(3))(dzp, w_up_t)


def _ffn_norm_bwd(dn, dh2, h1, g):
    T, D = h1.shape
    tm = _tile(T, 256)

    def body(dn_ref, dh_ref, h_ref, g_ref, dh1_ref, dh1b_ref, dg_ref):
        @pl.when(pl.program_id(0) == 0)
        def _():
            dg_ref[...] = jnp.zeros_like(dg_ref)

        hv, dnv = h_ref[...], dn_ref[...]
        r = _rms_stats(hv)
        dg_ref[...] += jnp.sum(dnv * (hv * r), axis=0, keepdims=True)
        dh1 = dh_ref[...] + _rms_bwd(dnv, hv, r, g_ref[...])
        dh1_ref[...] = dh1
        dh1b_ref[...] = dh1.astype(BF16)

    row = pl.BlockSpec((tm, D), lambda i: (i, 0))
    vec = pl.BlockSpec((1, D), lambda i: (0, 0))
    return _CHAIN.call(
        body, name="ffn_norm_bwd", grid=(T // tm,), in_specs=[row, row, row, vec], out_specs=[row, row, vec],
        out_shape=[SDS((T, D), F32), SDS((T, D), BF16), SDS((1, D), F32)], compiler_params=_params(1))(dn, dh2, h1, g)


def _matmul_tn(a, b, name, square_a=False, col_blocks=None):
    T, K = a.shape
    N = b.shape[1]
    tk = _tile(K, 1792)
    tn = _tile(N if col_blocks is None else N // col_blocks, 1024 if tk <= 1024 else 512)

    def body(a_ref, b_ref, o_ref):
        av = a_ref[...]
        if square_a:
            af = av.astype(F32)
            av = (af * af).astype(BF16)
        o_ref[...] = _dot(av, b_ref[...], TN).astype(o_ref.dtype)

    if col_blocks is None:
        out_shape = SDS((K, N), BF16)
        out_spec = pl.BlockSpec((tk, tn), lambda i, j: (i, j))
    else:
        per = (N // col_blocks) // tn
        out_shape = SDS((col_blocks, K, N // col_blocks), BF16)
        out_spec = pl.BlockSpec((None, tk, tn), lambda i, j: (j // per, i, j % per))
    return _CHAIN.call(
        body, name=name, grid=(K // tk, N // tn),
        in_specs=[pl.BlockSpec((T, tk), lambda i, j: (0, i)), pl.BlockSpec((T, tn), lambda i, j: (0, j))],
        out_specs=out_spec, out_shape=out_shape, compiler_params=_params(2))(a, b)


def _outproj_bwd(dh1b, w, a, b, ga, gb):
    T, D = dh1b.shape
    A, B = a.shape[1], b.shape[1]
    tm = _tile(T, 512)

    def body(dh_ref, w_ref, a_ref, b_ref, ga_ref, gb_ref, da_ref, db_ref, dga_ref, dgb_ref):
        @pl.when(pl.program_id(0) == 0)
        def _():
            dga_ref[...] = jnp.zeros_like(dga_ref)
            dgb_ref[...] = jnp.zeros_like(dgb_ref)

        dmix = _dot(dh_ref[...], w_ref[...], NT)
        for src_ref, g_ref, dx_ref, dg_ref, dn in ((a_ref, ga_ref, da_ref, dga_ref, dmix[:, :A]),
                                                   (b_ref, gb_ref, db_ref, dgb_ref, dmix[:, A:])):
            xv = src_ref[...]
            r = _rms_stats(xv)
            dg_ref[...] += jnp.sum(dn * (xv * r), axis=0, keepdims=True)
            dx_ref[...] = _rms_bwd(dn, xv, r, g_ref[...])

    return _CHAIN.call(
        body, name="outproj_bwd", grid=(T // tm,),
        in_specs=[pl.BlockSpec((tm, D), lambda i: (i, 0)), _resident((A + B, D)),
                  pl.BlockSpec((tm, A), lambda i: (i, 0)), pl.BlockSpec((tm, B), lambda i: (i, 0)),
                  pl.BlockSpec((1, A), lambda i: (0, 0)), pl.BlockSpec((1, B), lambda i: (0, 0))],
        out_specs=[pl.BlockSpec((tm, A), lambda i: (i, 0)), pl.BlockSpec((tm, B), lambda i: (i, 0)),
                   pl.BlockSpec((1, A), lambda i: (0, 0)), pl.BlockSpec((1, B), lambda i: (0, 0))],
        out_shape=[SDS((T, A), F32), SDS((T, B), F32), SDS((1, A), F32), SDS((1, B), F32)],
        compiler_params=_params(1))(dh1b, w, a, b, ga, gb)


def _gmlp_bwd(proj, da, lg, lb, w_s, w_st, bs_t, A):
    T = proj.shape[0]
    G = A // GROUP_DIM
    tm = _tile(T, 512)
    nc = tm // CHUNK

    def body(u_ref, v_ref, da_ref, lg_ref, lb_ref, w_ref, wt_ref, bst_ref, duv_ref, dlg_ref, dlb_ref, dw_ref, dbs_ref):
        @pl.when(pl.program_id(0) == 0)
        def _():
            dlg_ref[...] = jnp.zeros_like(dlg_ref)
            dlb_ref[...] = jnp.zeros_like(dlb_ref)
            dw_ref[...] = jnp.zeros_like(dw_ref)
            dbs_ref[...] = jnp.zeros_like(dbs_ref)

        row = lax.broadcasted_iota(jnp.int32, (CHUNK, CHUNK), 0)
        col = lax.broadcasted_iota(jnp.int32, (CHUNK, CHUNK), 1)
        lower = row >= col
        upper = row <= col
        for g in range(G):
            sl = slice(g * GROUP_DIM, (g + 1) * GROUP_DIM)
            lgv = lg_ref[:, sl]
            vg, vg_grad = _gelu_and_grad(v_ref[:, sl])
            vhat, rstd, vn = _layer_norm_group(vg, lgv, lb_ref[:, sl])
            vnb = vn.astype(BF16)
            ug, ug_grad = _gelu_and_grad(u_ref[:, sl])
            dav = da_ref[:, sl]
            wm = jnp.where(lower, w_ref[g], 0.0).astype(BF16)
            wmt = jnp.where(upper, wt_ref[g], 0.0).astype(BF16)
            bcol = bst_ref[:, g:g + 1]
            dw_acc = jnp.zeros((CHUNK, CHUNK), F32)
            dbs_acc = jnp.zeros((CHUNK, 1), F32)
            dvn_parts = []
            dug_parts = []
            for c in range(nc):
                rs = slice(c * CHUNK, (c + 1) * CHUNK)
                mixed = _dot(wm, vnb[rs], NN) + bcol
                dug_parts.append(dav[rs] * mixed)
                dmix = dav[rs] * ug[rs]
                dbs_acc = dbs_acc + jnp.sum(dmix, axis=-1, keepdims=True)
                dmixb = dmix.astype(BF16)
                dw_acc = dw_acc + _dot(dmixb, vnb[rs], NT)
                dvn_parts.append(_dot(wmt, dmixb, NN))
            dvn = jnp.concatenate(dvn_parts, axis=0)
            dug = jnp.concatenate(dug_parts, axis=0)
            dw_ref[g] += jnp.where(lower, dw_acc, 0.0)
            dbs_ref[:, g:g + 1] += dbs_acc
            dlg_ref[:, sl] += jnp.sum(dvn * vhat, axis=0, keepdims=True)
            dlb_ref[:, sl] += jnp.sum(dvn, axis=0, keepdims=True)
            dvhat = dvn * lgv
            dvg = rstd * (dvhat - jnp.mean(dvhat, axis=-1, keepdims=True)
                          - vhat * jnp.mean(dvhat * vhat, axis=-1, keepdims=True))
            duv_ref[:, sl] = (dug * ug_grad).astype(BF16)
            duv_ref[:, A + g * GROUP_DIM:A + (g + 1) * GROUP_DIM] = (dvg * vg_grad).astype(BF16)

    return _CHAIN.call(
        body, name="gmlp_bwd", grid=(T // tm,),
        in_specs=[pl.BlockSpec((tm, A), lambda i: (i, 0)), pl.BlockSpec((tm, A), lambda i: (i, 1)),
                  pl.BlockSpec((tm, A), lambda i: (i, 0)),
                  pl.BlockSpec((1, A), lambda i: (0, 0)), pl.BlockSpec((1, A), lambda i: (0, 0)),
                  pl.BlockSpec((G, CHUNK, CHUNK), lambda i: (0, 0, 0)),
                  pl.BlockSpec((G, CHUNK, CHUNK), lambda i: (0, 0, 0)), pl.BlockSpec((CHUNK, G), lambda i: (0, 0))],
        out_specs=[pl.BlockSpec((tm, 2 * A), lambda i: (i, 0)),
                   pl.BlockSpec((1, A), lambda i: (0, 0)), pl.BlockSpec((1, A), lambda i: (0, 0)),
                   pl.BlockSpec((G, CHUNK, CHUNK), lambda i: (0, 0, 0)), pl.BlockSpec((CHUNK, G), lambda i: (0, 0))],
        out_shape=[SDS((T, 2 * A), BF16), SDS((1, A), F32), SDS((1, A), F32),
                   SDS((G, CHUNK, CHUNK), F32), SDS((CHUNK, G), F32)],
        compiler_params=_params(1))(proj, proj, da, lg, lb, w_s, w_st, bs_t)


def _attn_bwd(proj, do, duv, bias_t, sinks, A, B):
    T, P = proj.shape
    H = B // HEAD_DIM
    qpk = H // KV_HEADS
    tq = _tile(T, 512)
    nb = tq // CHUNK
    n_tiles = T // tq
    scale = HEAD_DIM ** -0.5
    rev = lambda i: n_tiles - 1 - i

    def body(sink_ref, q_ref, k_ref, v_ref, kp_ref, vp_ref, do_ref, duv_ref, bias_ref,
             dproj_ref, dbias_ref, dsink_ref, carry, dkv, sacc):
        step = pl.program_id(0)

        @pl.when(step == 0)
        def _():
            carry[...] = jnp.zeros_like(carry)
            sacc[...] = jnp.zeros_like(sacc)
            dbias_ref[...] = jnp.zeros_like(dbias_ref)

        jj = lax.broadcasted_iota(jnp.int32, (2 * CHUNK, CHUNK), 0)
        ii = lax.broadcasted_iota(jnp.int32, (2 * CHUNK, CHUNK), 1)
        in_window = (jj > ii) & (jj <= ii + CHUNK)
        first_mask = in_window & jnp.logical_or(step != n_tiles - 1, jj >= CHUNK)
        low_query = lax.broadcasted_iota(jnp.int32, (CHUNK, LANE), 1) < HEAD_DIM
        low_key = lax.broadcasted_iota(jnp.int32, (2 * CHUNK, LANE), 1) < HEAD_DIM

        def split_pair(pair_bf16):
            zero = jnp.zeros_like(pair_bf16)
            return jnp.concatenate([jnp.where(low_query, pair_bf16, zero), jnp.where(low_query, zero, pair_bf16)], axis=0)

        dproj_ref[:, :2 * A] = duv_ref[...]
        dkv[...] = jnp.zeros_like(dkv)
        for b in range(nb):
            rows = slice(b * CHUNK, (b + 1) * CHUNK)
            band = slice(b * CHUNK, (b + 2) * CHUNK)
            if b == 0:
                kprev, vprev, mask = kp_ref[...], vp_ref[...], first_mask
            else:
                prows = slice((b - 1) * CHUNK, b * CHUNK)
                kprev, vprev, mask = k_ref[prows, :], v_ref[prows, :], in_window
            kband = jnp.concatenate([kprev, k_ref[rows, :]], axis=0)
            vband = jnp.concatenate([vprev, v_ref[rows, :]], axis=0)
            k_pads = [_pad_heads(kband, g) for g in range(KV_HEADS)]
            v_pads = [_pad_heads(vband, g) for g in range(KV_HEADS)]
            queries, douts, scores, dprobs = [], [], [], []
            for pair in range(H // 2):
                cols = slice(2 * pair * HEAD_DIM, (2 * pair + 2) * HEAD_DIM)
                qs = (q_ref[rows, cols] * scale).astype(BF16)
                dob = do_ref[rows, cols].astype(BF16)
                queries.append(qs)
                douts.append(dob)
                scores += [_dot(kz, qs, NT) for kz in k_pads[2 * pair // qpk]]
                dprobs += [_dot(vz, dob, NT) for vz in v_pads[2 * pair // qpk]]
            probs, dscores = [], []
            for h in range(H):
                pt, p_sink = _softmax_with_sink(jnp.where(mask, scores[h] + bias_ref[h], NEG), sink_ref[h], 0)
                delta = jnp.sum(pt * dprobs[h], axis=0, keepdims=True)
                dst = pt * (dprobs[h] - delta)
                dbias_ref[h] += dst
                sacc[h:h + 1, :] += -(p_sink * delta)
                probs.append(pt.astype(BF16))
                dscores.append(dst.astype(BF16))
            dq_parts, dk_groups, dv_groups = [], [], []
            for g in range(KV_HEADS):
                k_both = jnp.concatenate(k_pads[g], axis=0)
                dk_acc = jnp.zeros((2 * CHUNK, LANE), F32)
                dv_acc = jnp.zeros((2 * CHUNK, LANE), F32)
                for pair in range(g * qpk // 2, (g + 1) * qpk // 2):
                    pair_heads = slice(2 * pair, 2 * pair + 2)
                    dk_acc = dk_acc + _dot(jnp.concatenate(dscores[pair_heads], axis=1), split_pair(queries[pair]), NN)
                    dv_acc = dv_acc + _dot(jnp.concatenate(probs[pair_heads], axis=1), split_pair(douts[pair]), NN)
                    dq_parts.append(_dot(jnp.concatenate(dscores[pair_heads], axis=0), k_both, TN) * scale)
                dk_groups.append(dk_acc + pltpu.roll(dk_acc, HEAD_DIM, 1))
                dv_groups.append(dv_acc + pltpu.roll(dv_acc, HEAD_DIM, 1))
            dkv[band, :LANE] += jnp.where(low_key, dk_groups[0], dk_groups[1])
            dkv[band, LANE:] += jnp.where(low_key, dv_groups[0], dv_groups[1])
            dproj_ref[rows, 2 * A:2 * A + B] = jnp.concatenate(dq_parts, axis=1).astype(BF16)
        last = slice(tq, tq + CHUNK)
        dkv[last, :] += carry[...]
        dproj_ref[:, 2 * A + B:] = dkv[CHUNK:, :].astype(BF16)
        carry[...] = dkv[:CHUNK, :]

        @pl.when(step == n_tiles - 1)
        def _():
            dsink_ref[...] = jnp.sum(sacc[...], axis=1, keepdims=True)

    specs = _attn_specs(tq, A, B, reverse_tiles=n_tiles)
    return _CHAIN.call(
        body, name="attn_bwd", grid=(n_tiles,),
        in_specs=[pl.BlockSpec(memory_space=pltpu.SMEM)] + specs
        + [pl.BlockSpec((tq, B), lambda i: (rev(i), 0)), pl.BlockSpec((tq, 2 * A), lambda i: (rev(i), 0)),
           pl.BlockSpec((H, 2 * CHUNK, CHUNK), lambda i: (0, 0, 0))],
        out_specs=[pl.BlockSpec((tq, P), lambda i: (rev(i), 0)),
                   pl.BlockSpec((H, 2 * CHUNK, CHUNK), lambda i: (0, 0, 0)), pl.BlockSpec((H, 1), lambda i: (0, 0))],
        out_shape=[SDS((T, P), BF16), SDS((H, 2 * CHUNK, CHUNK), F32), SDS((H, 1), F32)],
        scratch_shapes=[pltpu.VMEM((CHUNK, 2 * LANE), F32), pltpu.VMEM((tq + CHUNK, 2 * LANE), F32),
                        pltpu.VMEM((H, LANE), F32)],
        compiler_params=_params(1))(sinks, proj, proj, proj, proj, proj, do, duv, bias_t)


def _bias_bwd(dbias, onehot):
    H = dbias.shape[0]
    nbk = onehot.shape[1]

    def body(d_ref, oh_ref, o_ref):
        hi, mid, lo = _split3(d_ref[...])
        oh = oh_ref[...]
        o_ref[...] = _dot(hi, oh, NN) + _dot(mid, oh, NN) + _dot(lo, oh, NN)

    return _CHAIN.call(body, name="bias_bwd", in_specs=[VMEM_SPEC] * 2, out_specs=VMEM_SPEC, out_shape=SDS((H, nbk), F32),
                       compiler_params=_params(0))(dbias, onehot)


def _inproj_bwd(dproj, w_t, x, dh1, g):
    T, P = dproj.shape
    D = x.shape[1]
    tm = _tile(T, 512)

    def body(dp_ref, w_ref, x_ref, dh_ref, g_ref, dx_ref, dg_ref):
        @pl.when(pl.program_id(0) == 0)
        def _():
            dg_ref[...] = jnp.zeros_like(dg_ref)

        dn = _dot(dp_ref[...], w_ref[...], NN)
        xv = x_ref[...]
        r = _rms_stats(xv)
        dg_ref[...] += jnp.sum(dn * (xv * r), axis=0, keepdims=True)
        dx_ref[...] = dh_ref[...] + _rms_bwd(dn, xv, r, g_ref[...])

    return _CHAIN.call(
        body, name="inproj_bwd", grid=(T // tm,),
        in_specs=[pl.BlockSpec((tm, P), lambda i: (i, 0)), _resident((P, D)),
                  pl.BlockSpec((tm, D), lambda i: (i, 0)), pl.BlockSpec((tm, D), lambda i: (i, 0)),
                  pl.BlockSpec((1, D), lambda i: (0, 0))],
        out_specs=[pl.BlockSpec((tm, D), lambda i: (i, 0)), pl.BlockSpec((1, D), lambda i: (0, 0))],
        out_shape=[SDS((T, D), F32), SDS((1, D), F32)], compiler_params=_params(1))(dproj, w_t, x, dh1, g)


def _adamw(w, g, m, v):
    m = ADAM_B1 * m + (1.0 - ADAM_B1) * g
    v = ADAM_B2 * v + (1.0 - ADAM_B2) * (g * g)
    m_hat = m / (1.0 - ADAM_B1 ** ADAM_STEP)
    v_hat = v / (1.0 - ADAM_B2 ** ADAM_STEP)
    delta = -ADAM_LR * (m_hat / (jnp.sqrt(v_hat) + ADAM_EPS) + ADAM_WD * w)
    return delta, m, v


def _adam_sharded(csum, recv, w, m, v, name):
    R, C = w.shape
    tr = _tile(R, 256, 16)

    def body(own_ref, recv_ref, w_ref, m_ref, v_ref, g_ref, d_ref, nm_ref, nv_ref):
        g = own_ref[...].astype(F32)
        for r in range(3):
            g = g + recv_ref[r].astype(F32)
        delta, nm, nv = _adamw(w_ref[...], g, m_ref[...], v_ref[...])
        g_ref[...] = g
        d_ref[...] = delta
        nm_ref[...] = nm
        nv_ref[...] = nv

    blk = pl.BlockSpec((tr, C), lambda i: (i, 0))
    return _CHAIN.call(
        body, name=name, grid=(R // tr,),
        in_specs=[pl.BlockSpec((None, tr, C), lambda i: (0, i, 0)), pl.BlockSpec((3, tr, C), lambda i: (0, i, 0)),
                  blk, blk, blk],
        out_specs=[blk] * 4, out_shape=[SDS((R, C), F32)] * 4, compiler_params=_params(1))(csum, recv, w, m, v)


def _adam_small(gathered, w, m, v):
    R = w.shape[0]

    def body(p_ref, w_ref, m_ref, v_ref, g_ref, d_ref, nm_ref, nv_ref):
        g = p_ref[0]
        for d in range(1, N_DEV):
            g = g + p_ref[d]
        delta, nm, nv = _adamw(w_ref[...], g, m_ref[...], v_ref[...])
        g_ref[...] = g
        d_ref[...] = delta
        nm_ref[...] = nm
        nv_ref[...] = nv

    return _CHAIN.call(body, name="adam_small", in_specs=[VMEM_SPEC] * 4, out_specs=[VMEM_SPEC] * 4,
                       out_shape=[SDS((R, LANE), F32)] * 4,
                       compiler_params=_params(0))(gathered, w, m, v)


def _pack(arrays):
    tile = 8 * LANE
    pieces = []
    for a in arrays:
        flat = a.reshape(-1).astype(F32)
        pieces.append(jnp.pad(flat, (0, (-flat.size) % tile)))
    return jnp.concatenate(pieces).reshape(-1, LANE)


def _unpack(packed, shapes):
    tile = 8 * LANE
    flat = packed.reshape(-1)
    out, off = [], 0
    for s in shapes:
        size = int(np.prod(s))
        out.append(flat[off:off + size].reshape(s))
        off += size + (-size) % tile
    return out


def kernel(x, rel_bias_table, mix_norm_g, w_in, gate_norm_g, gate_norm_b, w_spatial, b_spatial, attn_sinks, out_norm_a_g, out_norm_b_g, w_out, ffn_norm_g, w_up, w_down, final_norm_g, loss_target, m_rel_bias_table, m_mix_norm_g, m_w_in, m_gate_norm_g, m_gate_norm_b, m_w_spatial, m_b_spatial, m_attn_sinks, m_out_norm_a_g, m_out_norm_b_g, m_w_out, m_ffn_norm_g, m_w_up, m_w_down, m_final_norm_g, v_rel_bias_table, v_mix_norm_g, v_w_in, v_gate_norm_g, v_gate_norm_b, v_w_spatial, v_b_spatial, v_attn_sinks, v_out_norm_a_g, v_out_norm_b_g, v_w_out, v_ffn_norm_g, v_w_up, v_w_down, v_final_norm_g):
    T, D = x.shape[1], x.shape[2]
    A = D // 2
    B = D // 2
    G = A // GROUP_DIM
    H = B // HEAD_DIM
    P = 2 * A + B + 2 * KV_HEADS * HEAD_DIM
    xs = x.reshape(T, D)
    target = loss_target.reshape(T, D)

    win_t, m_win_t, v_win_t = (jnp.swapaxes(a[0], 0, 1) for a in (w_in, m_w_in, v_w_in))
    shards = [win_t.astype(BF16), w_out[0].astype(BF16), w_up[0].astype(BF16), w_down[0].astype(BF16)]
    _CHAIN.token = None
    gather = _gather_begin(shards, "gather_start")

    g1, g2, g3 = mix_norm_g.reshape(1, D), ffn_norm_g.reshape(1, D), final_norm_g.reshape(1, D)
    lg, lb = gate_norm_g.reshape(1, A), gate_norm_b.reshape(1, A)
    ws = w_spatial[0]
    ws_t = jnp.swapaxes(ws, 1, 2)
    bs_t = jnp.transpose(b_spatial[0])
    ga, gb = out_norm_a_g.reshape(1, A), out_norm_b_g.reshape(1, B)
    sinks = attn_sinks.reshape(H)
    bucket, in_window = _t5_bucket()
    onehot_np = ((bucket[:, :, None] == np.arange(N_BUCKETS)) & in_window[:, :, None]).astype(np.float32)
    onehot = jnp.asarray(onehot_np.reshape(-1, N_BUCKETS)).astype(BF16)
    onehot_kq = jnp.asarray(onehot_np.transpose(1, 0, 2).reshape(-1, N_BUCKETS)).astype(BF16)

    bias, bias_t = _bias_fwd(jnp.transpose(rel_bias_table), jnp.transpose(onehot), jnp.transpose(onehot_kq))
    bias, bias_t = bias.reshape(H, CHUNK, 2 * CHUNK), bias_t.reshape(H, 2 * CHUNK, CHUNK)
    n1 = _mix_norm(xs, g1)
    _gather_pass_on(gather, [0], "gather_in_pass")
    (win_g,) = _gather_end(gather, [0], "gather_in_end")
    win_t_full = win_g.reshape(P, D)
    proj = _inproj_fwd(n1, win_t_full)
    _gather_pass_on(gather, [1], "gather_out_pass")
    a_out = _gmlp_fwd(proj, lg, lb, ws, bs_t, A)
    b_out = _attn_fwd(proj, bias, sinks, A, B)
    (wout_g,) = _gather_end(gather, [1], "gather_out_end")
    wout_full = wout_g.reshape(A + B, D)
    h1, mixed, n2 = _outproj_fwd(a_out, b_out, ga, gb, xs, wout_full, g2)
    _gather_pass_on(gather, [2], "gather_up_pass")
    (wup_g,) = _gather_end(gather, [2], "gather_up_end")
    wup_t = jnp.transpose(wup_g, (0, 2, 1)).reshape(-1, D)
    z = _ffn_up(n2, wup_g)
    _gather_pass_on(gather, [3], "gather_down_pass")
    (wdown_g,) = _gather_end(gather, [3], "gather_down_end")
    h2 = _ffn_down(h1, z, wdown_g.reshape(-1, D))
    loss_part, dg3, dh2, dh2b = _final_loss(h2, g3, target)

    def reduce_to_chip(state, name):
        part, received = _sibling_exchange_end(state, name + "_sib_end")
        return _chip_exchange_begin(_chip_sum(part, received, name + "_chip_sum"), name + "_chip")

    dwdown = _matmul_tn(z, dh2b, "grad_w_down", square_a=True).reshape(wdown_g.shape)
    sib_down = _sibling_exchange_begin(dwdown, "rs_down_sib")
    dzp = _ffn_down_bwd(dh2b, z, wdown_g.reshape(-1, D))
    chip_down = reduce_to_chip(sib_down, "rs_down")
    dwup = _matmul_tn(n2, dzp, "grad_w_up", col_blocks=N_DEV)
    sib_up = _sibling_exchange_begin(dwup, "rs_up_sib")
    dh1, dh1b, dg2 = _ffn_norm_bwd(_ffn_up_bwd(dzp, wup_t), dh2, h1, g2)
    chip_up = reduce_to_chip(sib_up, "rs_up")
    da, db, dga, dgb = _outproj_bwd(dh1b, wout_full, a_out, b_out, ga, gb)
    dwout = _matmul_tn(mixed, dh1b, "grad_w_out").reshape(wout_g.shape)
    sib_out = _sibling_exchange_begin(dwout, "rs_out_sib")
    duv, dlg, dlb, dws, dbs_t = _gmlp_bwd(proj, da, lg, lb, ws, ws_t, bs_t, A)
    dproj, dbias_t, dsinks = _attn_bwd(proj, db, duv, bias_t, sinks, A, B)
    chip_out = reduce_to_chip(sib_out, "rs_out")
    dtable_t = _bias_bwd(dbias_t.reshape(H, -1), onehot_kq)
    dwin_t = _matmul_tn(dproj, n1, "grad_w_in").reshape(win_g.shape)
    sib_in = _sibling_exchange_begin(dwin_t, "rs_in_sib")
    grad_x, dg1 = _inproj_bwd(dproj, win_t_full, xs, dh1, g1)

    small_w = [rel_bias_table, mix_norm_g, gate_norm_g, gate_norm_b, w_spatial, b_spatial, attn_sinks,
               out_norm_a_g, out_norm_b_g, ffn_norm_g, final_norm_g]
    small_m = [m_rel_bias_table, m_mix_norm_g, m_gate_norm_g, m_gate_norm_b, m_w_spatial, m_b_spatial, m_attn_sinks,
               m_out_norm_a_g, m_out_norm_b_g, m_ffn_norm_g, m_final_norm_g]
    small_v = [v_rel_bias_table, v_mix_norm_g, v_gate_norm_g, v_gate_norm_b, v_w_spatial, v_b_spatial, v_attn_sinks,
               v_out_norm_a_g, v_out_norm_b_g, v_ffn_norm_g, v_final_norm_g]
    small_g = [jnp.transpose(dtable_t), dg1, dlg, dlb, dws, jnp.transpose(dbs_t), dsinks, dga, dgb, dg2, dg3]
    nothing = jnp.zeros((1, 1), F32)
    small_w, small_m, small_v, small_g = small_w + [nothing], small_m + [nothing], small_v + [nothing], small_g + [loss_part]
    shapes = [w.shape for w in small_w]
    big = [None] * 4

    def adam_of(k, state, w, m, v):
        csum, received = _chip_exchange_end(state, "rs_%d_end" % k)
        big[k] = _adam_sharded(csum, received, w, m, v, "adam_%d" % k)

    small_gather = _gather_begin([_pack(small_g)], "small_gather_start")
    chip_in = reduce_to_chip(sib_in, "rs_in")
    _gather_pass_on(small_gather, [0], "small_gather_pass")
    adam_of(3, chip_down, w_down[0], m_w_down[0], v_w_down[0])
    (gathered,) = _gather_end(small_gather, [0], "small_gather_end")
    sg, sd, sm, sv = [_unpack(o, shapes) for o in _adam_small(gathered, _pack(small_w), _pack(small_m), _pack(small_v))]
    adam_of(2, chip_up, w_up[0], m_w_up[0], v_w_up[0])
    adam_of(1, chip_out, w_out[0], m_w_out[0], v_w_out[0])
    adam_of(0, chip_in, win_t, m_win_t, v_win_t)
    big[0] = [jnp.swapaxes(o, 0, 1) for o in big[0]]
    big = [[o.reshape(w.shape) for o in outs] for outs, w in zip(big, (w_in, w_out, w_up, w_down))]

    loss = sg[-1].reshape(())

    order = ["s0", "s1", "b0", "s2", "s3", "s4", "s5", "s6", "s7", "s8", "b1", "s9", "b2", "b3", "s10"]

    def group(idx):
        small = (sg, sd, sm, sv)[idx]
        return [small[int(t[1:])] if t[0] == "s" else big[int(t[1:])][idx] for t in order]

    return (loss, grad_x.reshape(x.shape), *group(0), *group(1), *group(2), *group(3))
```

```python
import functools
import math

import numpy as np
import jax
import jax.numpy as jnp
from jax import lax
from jax.experimental import pallas as pl
from jax.experimental.pallas import tpu as pltpu

F32 = jnp.float32
BF16 = jnp.bfloat16
SDS = jax.ShapeDtypeStruct
MESH = pl.DeviceIdType.MESH

N_DEV = 8
EPS = 1e-5
NEG = -1e30
CHUNK = 128
GROUP_DIM = 128
HEAD_DIM = 64
KV_HEADS = 2
N_BUCKETS = 32
MAX_DISTANCE = 128
ADAM_LR, ADAM_B1, ADAM_B2, ADAM_EPS, ADAM_WD, ADAM_STEP = 0.001, 0.9, 0.999, 1e-08, 0.01, 10
GELU_C0 = math.sqrt(2.0 / math.pi)
GELU_C1 = 0.044715

V7X_VMEM_BYTES = 64 * 1024 * 1024
VMEM_LIMIT = V7X_VMEM_BYTES - 8 * 1024 * 1024
LANE = 128

NN = ((1,), (0,))
NT = ((1,), (1,))
TN = ((0,), (0,))


def _dot(a, b, dims):
    return lax.dot_general(a, b, (dims, ((), ())), preferred_element_type=F32)


def _tile(n, pref, unit=LANE):
    best = None
    for t in range(unit, min(n, pref) + 1, unit):
        if n % t == 0:
            best = t
    return n if best is None else best


def _params(n_grid):
    return pltpu.CompilerParams(dimension_semantics=("arbitrary",) * n_grid, vmem_limit_bytes=VMEM_LIMIT)


def _resident(shape):
    return pl.BlockSpec(shape, lambda i: (0, 0), pipeline_mode=pl.Buffered(1))


def _gelu(x):
    return 0.5 * x * (1.0 + jnp.tanh(GELU_C0 * (x + GELU_C1 * x * x * x)))


def _gelu_and_grad(x):
    x2 = x * x
    t = jnp.tanh(GELU_C0 * x * (1.0 + GELU_C1 * x2))
    val = 0.5 * x * (1.0 + t)
    grad = 0.5 * (1.0 + t) + 0.5 * x * (1.0 - t * t) * (GELU_C0 * (1.0 + 3.0 * GELU_C1 * x2))
    return val, grad


def _rms_stats(x):
    return lax.rsqrt(jnp.mean(x * x, axis=-1, keepdims=True) + EPS)


def _rms_bwd(dy, x, r, g):
    w = dy * g
    return r * w - x * (r * r * r) * jnp.mean(w * x, axis=-1, keepdims=True)


def _t5_bucket():
    i = np.arange(CHUNK)[:, None]
    j = np.arange(2 * CHUNK)[None, :]
    rel = np.maximum(i + CHUNK - j, 0)
    n_exact = N_BUCKETS // 2
    relf = np.maximum(rel, n_exact).astype(np.float32)
    large = n_exact + (np.log(relf / np.float32(n_exact)) / np.float32(math.log(MAX_DISTANCE / n_exact))
                       * np.float32(N_BUCKETS - n_exact)).astype(np.int32)
    large = np.minimum(large, N_BUCKETS - 1)
    bucket = np.where(rel < n_exact, rel, large)
    in_window = (i + CHUNK - j >= 0) & (i + CHUNK - j < CHUNK)
    return bucket.astype(np.int32), in_window


def _split3(x):
    hi = x.astype(BF16)
    r1 = x - hi.astype(F32)
    mid = r1.astype(BF16)
    lo = (r1 - mid.astype(F32)).astype(BF16)
    return hi, mid, lo


HBM_SPEC = pl.BlockSpec(memory_space=pltpu.HBM)


def _mesh_pos():
    return lax.axis_index("x"), lax.axis_index("y"), lax.axis_index("c")


def _dev_index(px, py, pc):
    return 4 * px + 2 * py + pc


SEM_SPEC = pl.BlockSpec(memory_space=pltpu.SEMAPHORE)
ANY_SPEC = pl.BlockSpec(memory_space=pl.ANY)
VMEM_SPEC = pl.BlockSpec(memory_space=pltpu.VMEM)
TOKEN_SPEC = VMEM_SPEC
TOKEN = SDS((8, LANE), F32)
SIDE_EFFECT = pltpu.SideEffectType.DATAFLOW_SIDE_EFFECTING


def _hbm(x):
    return pltpu.with_memory_space_constraint(x, pltpu.HBM)


class _CallChain:
    def __init__(self):
        self.token = None

    def call(self, body, *, in_specs, out_specs, out_shape, **kwargs):
        dep, n_in = self.token, len(in_specs)
        single = not isinstance(out_shape, (list, tuple))
        out_shapes = [out_shape] if single else list(out_shape)
        out_specs = [out_specs] if single else list(out_specs)
        n_out = len(out_shapes)
        n_dep = 0 if dep is None else 1
        token_spec = pl.BlockSpec((8, LANE), lambda *_: (0, 0)) if kwargs.get("grid") else VMEM_SPEC

        def chained(*refs):
            outs_at = n_in + n_dep
            body(*refs[:n_in], *refs[outs_at:outs_at + n_out], *refs[outs_at + n_out + 1:])
            token = refs[outs_at + n_out]
            token[...] = jnp.zeros_like(token)

        inner = pl.pallas_call(chained, in_specs=list(in_specs) + [ANY_SPEC] * n_dep, out_specs=out_specs + [token_spec],
                               out_shape=out_shapes + [TOKEN], **kwargs)

        def run(*operands):
            outs = inner(*operands) if dep is None else inner(*operands, dep)
            self.token = outs[n_out]
            return outs[0] if single else list(outs[:n_out])

        return run


_CHAIN = _CallChain()


def _wait_all(waits, x, y, c):
    for kind, src, dst, send_sem, recv_sem in waits:
        cp = pltpu.make_async_remote_copy(src_ref=src, dst_ref=dst, send_sem=send_sem, recv_sem=recv_sem,
                                          device_id=(x, y, c), device_id_type=MESH)
        if kind == "send":
            cp.wait_send()
        else:
            cp.wait_recv()


def _split_start(bufs, copies_of, n_sems, name, sem_sets=(), waits_of=None):
    n, ns = len(bufs), len(sem_sets)
    flat_sems = [s for pair in sem_sets for s in pair]

    def body(*refs):
        ins = refs[:n]
        sems = refs[n:n + 2 * ns]
        send_sems, recv_sems = refs[n + 2 * ns], refs[n + 2 * ns + 1]
        if waits_of is not None:
            _wait_all(waits_of(ins, [(sems[2 * i], sems[2 * i + 1]) for i in range(ns)]), *_mesh_pos())
        for src, dst, k, target in copies_of(ins):
            pltpu.make_async_remote_copy(src_ref=src, dst_ref=dst, send_sem=send_sems.at[k], recv_sem=recv_sems.at[k],
                                         device_id=target, device_id_type=MESH).start()

    outs = _CHAIN.call(
        body, name=name,
        out_shape=[pltpu.SemaphoreType.DMA((n_sems,)), pltpu.SemaphoreType.DMA((n_sems,))]
        + [pltpu.HBM(b.shape, b.dtype) for b in bufs],
        in_specs=[HBM_SPEC] * n + [SEM_SPEC] * (2 * ns), out_specs=[SEM_SPEC, SEM_SPEC] + [HBM_SPEC] * n,
        input_output_aliases={a: 2 + a for a in range(n)},
        compiler_params=pltpu.CompilerParams(has_side_effects=SIDE_EFFECT),
    )(*[_hbm(b) for b in bufs], *flat_sems)
    return outs[0], outs[1], list(outs[2:2 + n])


def _split_wait(bufs, sem_sets, waits_of, name):
    n, ns = len(bufs), len(sem_sets)
    flat_sems = [s for pair in sem_sets for s in pair]

    def body(*refs):
        ins = refs[:n]
        sems = refs[n:n + 2 * ns]
        _wait_all(waits_of(ins, [(sems[2 * i], sems[2 * i + 1]) for i in range(ns)]), *_mesh_pos())

    outs = _CHAIN.call(
        body, name=name,
        out_shape=[pltpu.HBM(b.shape, b.dtype) for b in bufs],
        in_specs=[HBM_SPEC] * n + [SEM_SPEC] * (2 * ns), out_specs=[HBM_SPEC] * n,
        input_output_aliases={a: a for a in range(n)},
        compiler_params=pltpu.CompilerParams(has_side_effects=SIDE_EFFECT),
    )(*bufs, *flat_sems)
    return list(outs)


def _gather_blocks(land):
    rows = land.shape[1]
    first = (rows // 2) // 16 * 16

    def block(px, py, pc):
        return land.at[_dev_index(px, py, pc)]

    def halves(px, py, pc):
        return (land.at[_dev_index(px, py, pc), pl.ds(0, first)], land.at[_dev_index(px, py, pc), pl.ds(first, rows - first)])

    return block, halves


def _gather_begin(shards, name):
    me = _dev_index(*_mesh_pos())
    lands = [lax.dynamic_update_index_in_dim(lax.empty((N_DEV,) + s.shape, s.dtype), s, me, 0) for s in shards]

    def copies_of(ins):
        x, y, c = _mesh_pos()
        targets = [(x, y, 1 - c), (1 - x, y, c), (x, 1 - y, c)]
        out = []
        for a, land in enumerate(ins):
            mine = _gather_blocks(land)[0](x, y, c)
            out += [(mine, mine, 3 * a + k, targets[k]) for k in (1, 2, 0)]
        return out

    sems = _split_start(lands, copies_of, 3 * len(shards), name)
    return dict(lands=sems[2], begin=sems[:2], stage={})


def _gather_step(state, items, name):
    which = sorted({a for a, _ in items})
    at = {a: i for i, a in enumerate(which)}
    sem_sets = [state["begin"]] + [state["stage"][(a, 1)][0] for a, s in items if s == 2]
    offset, n_sems = {}, 0
    for a, s in items:
        offset[(a, s)] = n_sems
        n_sems += 4 if s == 1 else 1

    def waits_of(ins, sems):
        x, y, c = _mesh_pos()
        out, second = [], 1
        for a, s in items:
            block, halves = _gather_blocks(ins[at[a]])
            if s == 1:
                for k, blk in ((1, block(1 - x, y, c)), (2, block(x, 1 - y, c))):
                    out.append(("recv", blk, blk, sems[0][0].at[3 * a + k], sems[0][1].at[3 * a + k]))
            else:
                off = state["stage"][(a, 1)][1]
                for j, half in zip((2, 3), halves(1 - x, 1 - y, c)):
                    out.append(("recv", half, half, sems[second][0].at[off + j], sems[second][1].at[off + j]))
                second += 1
        return out

    def copies_of(ins):
        x, y, c = _mesh_pos()
        sibling = (x, y, 1 - c)
        out = []
        for a, s in items:
            block, halves = _gather_blocks(ins[at[a]])
            off = offset[(a, s)]
            if s == 1:
                from_x, from_y = block(1 - x, y, c), block(x, 1 - y, c)
                out += [(halves(1 - x, y, c)[0], halves(1 - x, y, c)[0], off + 2, (x, 1 - y, c)),
                        (halves(x, 1 - y, c)[1], halves(x, 1 - y, c)[1], off + 3, (1 - x, y, c)),
                        (from_x, from_x, off, sibling), (from_y, from_y, off + 1, sibling)]
            else:
                diag = block(1 - x, 1 - y, c)
                out.append((diag, diag, off, sibling))
        return out

    send_sems, recv_sems, bufs = _split_start([state["lands"][a] for a in which], copies_of, n_sems, name,
                                              sem_sets=sem_sets, waits_of=waits_of)
    for a in which:
        state["lands"][a] = bufs[at[a]]
    for a, s in items:
        state["stage"][(a, s)] = ((send_sems, recv_sems), offset[(a, s)])


def _gather_end(state, which, name):
    sem_sets = [state["begin"]]
    for a in which:
        sem_sets += [state["stage"][(a, 1)][0], state["stage"][(a, 2)][0]]

    def waits(ins, sems):
        x, y, c = _mesh_pos()
        out = []
        for i, a in enumerate(which):
            block, halves = _gather_blocks(ins[i])
            (b_send, b_recv), (s1_send, s1_recv), (s2_send, s2_recv) = sems[0], sems[1 + 2 * i], sems[2 + 2 * i]
            o1, o2 = state["stage"][(a, 1)][1], state["stage"][(a, 2)][1]
            arrivals = [(block(x, y, 1 - c), b_send, b_recv, 3 * a),
                        (block(1 - x, y, 1 - c), s1_send, s1_recv, o1), (block(x, 1 - y, 1 - c), s1_send, s1_recv, o1 + 1),
                        (block(1 - x, 1 - y, 1 - c), s2_send, s2_recv, o2)]
            mine = block(x, y, c)
            sent = [(mine, b_send, b_recv, 3 * a + k) for k in range(3)]
            sent += [(block(1 - x, y, c), s1_send, s1_recv, o1), (block(x, 1 - y, c), s1_send, s1_recv, o1 + 1),
                     (halves(1 - x, y, c)[0], s1_send, s1_recv, o1 + 2), (halves(x, 1 - y, c)[1], s1_send, s1_recv, o1 + 3),
                     (block(1 - x, 1 - y, c), s2_send, s2_recv, o2)]
            out += [("recv", ref, ref, s.at[k], r.at[k]) for ref, s, r, k in arrivals]
            out += [("send", ref, ref, s.at[k], r.at[k]) for ref, s, r, k in sent]
        return out

    bufs = _split_wait([state["lands"][a] for a in which], sem_sets, waits, name)
    for i, a in enumerate(which):
        state["lands"][a] = bufs[i]
    return bufs


def _sibling_exchange_begin(part, name):
    land = lax.empty((4,) + part.shape[1:], part.dtype)

    def copies_of(ins):
        x, y, c = _mesh_pos()
        return [(ins[0].at[2 * j + (1 - c)], ins[1].at[j], j, (x, y, 1 - c)) for j in range(4)]

    send_sems, recv_sems, bufs = _split_start([part, land], copies_of, 4, name)
    return dict(bufs=bufs, sems=(send_sems, recv_sems))


def _sibling_exchange_end(state, name):
    def waits(ins, sems):
        _, _, c = _mesh_pos()
        out = []
        for j in range(4):
            for kind in ("send", "recv"):
                out.append((kind, ins[0].at[2 * j + (1 - c)], ins[1].at[j], sems[0][0].at[j], sems[0][1].at[j]))
        return out

    return _split_wait(state["bufs"], [state["sems"]], waits, name)


CHIP_FLIPS = (2, 1, 3)


def _chip_exchange_begin(csum, name):
    land = lax.empty((3,) + csum.shape[1:], csum.dtype)

    def copies_of(ins):
        x, y, c = _mesh_pos()
        chips = [(1 - x, y), (x, 1 - y), (1 - x, 1 - y)]
        return [(ins[0].at[CHIP_FLIPS[r]], ins[1].at[r], r, (px, py, c)) for r, (px, py) in enumerate(chips)]

    send_sems, recv_sems, bufs = _split_start([csum, land], copies_of, 3, name)
    return dict(bufs=bufs, sems=(send_sems, recv_sems))


def _chip_exchange_end(state, name):
    def waits(ins, sems):
        out = []
        for r in range(3):
            for kind in ("send", "recv"):
                out.append((kind, ins[0].at[CHIP_FLIPS[r]], ins[1].at[r], sems[0][0].at[r], sems[0][1].at[r]))
        return out

    return _split_wait(state["bufs"], [state["sems"]], waits, name)


def _chip_sum(part, recv, name):
    _, R, C = part.shape
    tr = _tile(R, 512, 16)
    place = jnp.stack([lax.axis_index("c"), 2 * lax.axis_index("x") + lax.axis_index("y")]).astype(jnp.int32)

    def body(place_ref, p_ref, r_ref, o_ref):
        o_ref[...] = (p_ref[...].astype(F32) + r_ref[...].astype(F32)).astype(o_ref.dtype)

    def chip(p, place_ref):
        return jnp.bitwise_xor(p, place_ref[1])

    grid_spec = pltpu.PrefetchScalarGridSpec(
        num_scalar_prefetch=1, grid=(4, R // tr),
        in_specs=[pl.BlockSpec((None, tr, C), lambda p, i, place_ref: (2 * chip(p, place_ref) + place_ref[0], i, 0)),
                  pl.BlockSpec((None, tr, C), lambda p, i, place_ref: (chip(p, place_ref), i, 0))],
        out_specs=pl.BlockSpec((None, tr, C), lambda p, i, place_ref: (p, i, 0)))
    return pl.pallas_call(body, name=name, grid_spec=grid_spec, out_shape=SDS((4, R, C), part.dtype),
                          compiler_params=_params(2))(place, part, recv)


def _bias_fwd(table_t, onehot_t, onehot_kq_t):
    H = table_t.shape[0]
    n = onehot_t.shape[1]

    def body(t_ref, oh_ref, oh_kq_ref, o_ref, o_kq_ref):
        hi, mid, lo = _split3(t_ref[...])
        for src, dst in ((oh_ref, o_ref), (oh_kq_ref, o_kq_ref)):
            oh = src[...]
            dst[...] = _dot(hi, oh, NN) + _dot(mid, oh, NN) + _dot(lo, oh, NN)

    return _CHAIN.call(body, name="bias_fwd", in_specs=[VMEM_SPEC] * 3, out_specs=[VMEM_SPEC] * 2,
                       out_shape=[SDS((H, n), F32)] * 2, compiler_params=_params(0))(table_t, onehot_t, onehot_kq_t)


def _mix_norm(x, g):
    T, D = x.shape
    tm = _tile(T, 512)

    def body(x_ref, g_ref, n_ref):
        xv = x_ref[...]
        n_ref[...] = (xv * _rms_stats(xv) * g_ref[...]).astype(BF16)

    row = pl.BlockSpec((tm, D), lambda i: (i, 0))
    return _CHAIN.call(body, name="mix_norm", grid=(T // tm,), in_specs=[row, pl.BlockSpec((1, D), lambda i: (0, 0))],
                       out_specs=row, out_shape=SDS((T, D), BF16), compiler_params=_params(1))(x, g)


def _inproj_fwd(n, w_t):
    T, D = n.shape
    P = w_t.shape[0]
    tm = _tile(T, 512)

    def body(n_ref, w_ref, proj_ref):
        proj_ref[...] = _dot(n_ref[...], w_ref[...], NT)

    return _CHAIN.call(
        body, name="inproj_fwd", grid=(T // tm,),
        in_specs=[pl.BlockSpec((tm, D), lambda i: (i, 0)), _resident((P, D))],
        out_specs=pl.BlockSpec((tm, P), lambda i: (i, 0)),
        out_shape=SDS((T, P), F32), compiler_params=_params(1))(n, w_t)


def _layer_norm_group(vg, lg, lb):
    mu = jnp.mean(vg, axis=-1, keepdims=True)
    xc = vg - mu
    rstd = lax.rsqrt(jnp.mean(xc * xc, axis=-1, keepdims=True) + EPS)
    vhat = xc * rstd
    return vhat, rstd, vhat * lg + lb


def _gmlp_fwd(proj, lg, lb, w_s, bs_t, A):
    T = proj.shape[0]
    G = A // GROUP_DIM
    tm = _tile(T, 512)
    nc = tm // CHUNK

    def body(u_ref, v_ref, lg_ref, lb_ref, w_ref, bst_ref, a_ref):
        row = lax.broadcasted_iota(jnp.int32, (CHUNK, CHUNK), 0)
        col = lax.broadcasted_iota(jnp.int32, (CHUNK, CHUNK), 1)
        causal = row >= col
        for g in range(G):
            sl = slice(g * GROUP_DIM, (g + 1) * GROUP_DIM)
            _, _, vn = _layer_norm_group(_gelu(v_ref[:, sl]), lg_ref[:, sl], lb_ref[:, sl])
            vnb = vn.astype(BF16)
            wm = jnp.where(causal, w_ref[g], 0.0).astype(BF16)
            ug = _gelu(u_ref[:, sl])
            bcol = bst_ref[:, g:g + 1]
            for c in range(nc):
                rs = slice(c * CHUNK, (c + 1) * CHUNK)
                a_ref[rs, sl] = ug[rs] * (_dot(wm, vnb[rs], NN) + bcol)

    return _CHAIN.call(
        body, name="gmlp_fwd", grid=(T // tm,),
        in_specs=[pl.BlockSpec((tm, A), lambda i: (i, 0)), pl.BlockSpec((tm, A), lambda i: (i, 1)),
                  pl.BlockSpec((1, A), lambda i: (0, 0)), pl.BlockSpec((1, A), lambda i: (0, 0)),
                  pl.BlockSpec((G, CHUNK, CHUNK), lambda i: (0, 0, 0)), pl.BlockSpec((CHUNK, G), lambda i: (0, 0))],
        out_specs=pl.BlockSpec((tm, A), lambda i: (i, 0)),
        out_shape=SDS((T, A), F32), compiler_params=_params(1))(proj, proj, lg, lb, w_s, bs_t)


def _attn_masks(first_tile):
    ii = lax.broadcasted_iota(jnp.int32, (CHUNK, 2 * CHUNK), 0)
    jj = lax.broadcasted_iota(jnp.int32, (CHUNK, 2 * CHUNK), 1)
    in_window = (jj > ii) & (jj <= ii + CHUNK)
    first_mask = in_window & jnp.logical_or(jnp.logical_not(first_tile), jj >= CHUNK)
    return in_window, first_mask


def _softmax_with_sink(s, sink, axis):
    m = jnp.maximum(jnp.max(s, axis=axis, keepdims=True), sink)
    p = jnp.exp(s - m)
    e_sink = jnp.exp(sink - m)
    inv = 1.0 / (jnp.sum(p, axis=axis, keepdims=True) + e_sink)
    return p * inv, e_sink * inv


def _pad_heads(band, group):
    lane = lax.broadcasted_iota(jnp.int32, band.shape, 1)
    if group == 0:
        low = jnp.where(lane < HEAD_DIM, band, 0.0)
        high = pltpu.roll(low, HEAD_DIM, 1)
    else:
        high = jnp.where(lane >= HEAD_DIM, band, 0.0)
        low = pltpu.roll(high, HEAD_DIM, 1)
    return low.astype(BF16), high.astype(BF16)


def _attn_specs(tq, A, B, reverse_tiles=None):
    nb = tq // CHUNK
    kcol = (2 * A + B) // LANE
    if reverse_tiles is None:
        tile = lambda i: i
    else:
        tile = lambda i: reverse_tiles - 1 - i
    prev = lambda i: jnp.maximum(tile(i) * nb - 1, 0)
    return [pl.BlockSpec((tq, B), lambda i: (tile(i), 2 * A // B)),
            pl.BlockSpec((tq, LANE), lambda i: (tile(i), kcol)),
            pl.BlockSpec((tq, LANE), lambda i: (tile(i), kcol + 1)),
            pl.BlockSpec((CHUNK, LANE), lambda i: (prev(i), kcol)),
            pl.BlockSpec((CHUNK, LANE), lambda i: (prev(i), kcol + 1))]


def _attn_fwd(proj, bias, sinks, A, B):
    T = proj.shape[0]
    H = B // HEAD_DIM
    qpk = H // KV_HEADS
    tq = _tile(T, 512)
    nb = tq // CHUNK

    scale = HEAD_DIM ** -0.5

    def body(sink_ref, q_ref, k_ref, v_ref, kp_ref, vp_ref, bias_ref, o_ref):
        in_window, first_mask = _attn_masks(pl.program_id(0) == 0)
        for b in range(nb):
            rows = slice(b * CHUNK, (b + 1) * CHUNK)
            if b == 0:
                kprev, vprev, mask = kp_ref[...], vp_ref[...], first_mask
            else:
                prows = slice((b - 1) * CHUNK, b * CHUNK)
                kprev, vprev, mask = k_ref[prows, :], v_ref[prows, :], in_window
            kband = jnp.concatenate([kprev, k_ref[rows, :]], axis=0)
            vband = jnp.concatenate([vprev, v_ref[rows, :]], axis=0)
            k_pads = [_pad_heads(kband, g) for g in range(KV_HEADS)]
            v_both = [jnp.concatenate(_pad_heads(vband, g), axis=0) for g in range(KV_HEADS)]
            scores = []
            for pair in range(H // 2):
                h = 2 * pair
                qs = (q_ref[rows, h * HEAD_DIM:(h + 2) * HEAD_DIM] * scale).astype(BF16)
                scores += [_dot(qs, kz, NT) for kz in k_pads[h // qpk]]
            probs = [_softmax_with_sink(jnp.where(mask, s + bias_ref[h], NEG), sink_ref[h], -1)[0].astype(BF16)
                     for h, s in enumerate(scores)]
            outs = [_dot(jnp.concatenate(probs[h:h + 2], axis=1), v_both[h // qpk], NN) for h in range(0, H, 2)]
            o_ref[rows, :] = jnp.concatenate(outs, axis=1)

    return _CHAIN.call(
        body, name="attn_fwd", grid=(T // tq,),
        in_specs=[pl.BlockSpec(memory_space=pltpu.SMEM)] + _attn_specs(tq, A, B)
        + [pl.BlockSpec((H, CHUNK, 2 * CHUNK), lambda i: (0, 0, 0))],
        out_specs=pl.BlockSpec((tq, B), lambda i: (i, 0)),
        out_shape=SDS((T, B), F32), compiler_params=_params(1))(sinks, proj, proj, proj, proj, proj, bias)


def _outproj_fwd(a, b, ga, gb, x, w, g_ffn):
    T, A = a.shape
    B = b.shape[1]
    D = x.shape[1]
    tm = _tile(T, 512)

    def body(a_ref, b_ref, ga_ref, gb_ref, x_ref, w_ref, gf_ref, h_ref, mix_ref, n_ref):
        av, bv = a_ref[...], b_ref[...]
        mix_ref[:, :A] = (av * _rms_stats(av) * ga_ref[...]).astype(BF16)
        mix_ref[:, A:] = (bv * _rms_stats(bv) * gb_ref[...]).astype(BF16)
        hv = x_ref[...] + _dot(mix_ref[...], w_ref[...], NN)
        h_ref[...] = hv
        n_ref[...] = (hv * _rms_stats(hv) * gf_ref[...]).astype(BF16)

    row = pl.BlockSpec((tm, D), lambda i: (i, 0))
    return _CHAIN.call(
        body, name="outproj_fwd", grid=(T // tm,),
        in_specs=[pl.BlockSpec((tm, A), lambda i: (i, 0)), pl.BlockSpec((tm, B), lambda i: (i, 0)),
                  pl.BlockSpec((1, A), lambda i: (0, 0)), pl.BlockSpec((1, B), lambda i: (0, 0)),
                  row, _resident((A + B, D)), pl.BlockSpec((1, D), lambda i: (0, 0))],
        out_specs=[row, pl.BlockSpec((tm, A + B), lambda i: (i, 0)), row],
        out_shape=[SDS((T, D), F32), SDS((T, A + B), BF16), SDS((T, D), BF16)],
        compiler_params=_params(1))(a, b, ga, gb, x, w, g_ffn)


def _ffn_up(n, w_up):
    T, D = n.shape
    Fb = w_up.shape[2]
    F = N_DEV * Fb
    tm, tf = _tile(T, 1024), _tile(Fb, 1024)
    per = Fb // tf

    def body(n_ref, wu_ref, z_ref):
        z_ref[...] = jnp.maximum(_dot(n_ref[...], wu_ref[...], NN), 0.0).astype(BF16)

    return _CHAIN.call(
        body, name="ffn_up", grid=(T // tm, F // tf),
        in_specs=[pl.BlockSpec((tm, D), lambda i, j: (i, 0)),
                  pl.BlockSpec((None, D, tf), lambda i, j: (j // per, 0, j % per))],
        out_specs=pl.BlockSpec((tm, tf), lambda i, j: (i, j)),
        out_shape=SDS((T, F), BF16), compiler_params=_params(2))(n, w_up)


def _ffn_down(h1, z, w_down):
    T, D = h1.shape
    F = w_down.shape[0]
    tm, tn, tk = _tile(T, 1024), _tile(D, 1024), _tile(F, 4096)

    def body(h_ref, z_ref, wd_ref, h2_ref):
        k = pl.program_id(2)

        @pl.when(k == 0)
        def _():
            h2_ref[...] = h_ref[...]

        zf = z_ref[...].astype(F32)
        h2_ref[...] += _dot((zf * zf).astype(BF16), wd_ref[...], NN)

    return _CHAIN.call(
        body, name="ffn_down", grid=(T // tm, D // tn, F // tk),
        in_specs=[pl.BlockSpec((tm, tn), lambda i, j, k: (i, j)), pl.BlockSpec((tm, tk), lambda i, j, k: (i, k)),
                  pl.BlockSpec((tk, tn), lambda i, j, k: (k, j))],
        out_specs=pl.BlockSpec((tm, tn), lambda i, j, k: (i, j)),
        out_shape=SDS((T, D), F32), compiler_params=_params(3))(h1, z, w_down)


def _final_loss(h2, g, target):
    T, D = h2.shape
    tm = _tile(T, 512)

    def body(h_ref, g_ref, t_ref, loss_ref, dg_ref, dh_ref, dhb_ref):
        @pl.when(pl.program_id(0) == 0)
        def _():
            loss_ref[...] = jnp.zeros_like(loss_ref)
            dg_ref[...] = jnp.zeros_like(dg_ref)

        hv, gv = h_ref[...], g_ref[...]
        r = _rms_stats(hv)
        hn = hv * r
        e = hn * gv - t_ref[...]
        loss_ref[...] += (0.5 / D) * jnp.sum(jnp.sum(e * e, axis=0, keepdims=True), axis=-1, keepdims=True)
        dy = e * (1.0 / D)
        dg_ref[...] += jnp.sum(dy * hn, axis=0, keepdims=True)
        dh = _rms_bwd(dy, hv, r, gv)
        dh_ref[...] = dh
        dhb_ref[...] = dh.astype(BF16)

    return _CHAIN.call(
        body, name="final_loss", grid=(T // tm,),
        in_specs=[pl.BlockSpec((tm, D), lambda i: (i, 0)), pl.BlockSpec((1, D), lambda i: (0, 0)),
                  pl.BlockSpec((tm, D), lambda i: (i, 0))],
        out_specs=[pl.BlockSpec((1, 1), lambda i: (0, 0)), pl.BlockSpec((1, D), lambda i: (0, 0)),
                   pl.BlockSpec((tm, D), lambda i: (i, 0)), pl.BlockSpec((tm, D), lambda i: (i, 0))],
        out_shape=[SDS((1, 1), F32), SDS((1, D), F32), SDS((T, D), F32), SDS((T, D), BF16)],
        compiler_params=_params(1))(h2, g, target)


def _ffn_down_bwd(dh2b, z, w_down):
    T, D = dh2b.shape
    F = w_down.shape[0]
    tm, tf = _tile(T, 1024), _tile(F, 1024)

    def body(dh_ref, z_ref, wd_ref, dzp_ref):
        dzz = _dot(dh_ref[...], wd_ref[...], NT)
        dzp_ref[...] = (dzz * (2.0 * z_ref[...].astype(F32))).astype(BF16)

    return _CHAIN.call(
        body, name="ffn_down_bwd", grid=(T // tm, F // tf),
        in_specs=[pl.BlockSpec((tm, D), lambda i, j: (i, 0)), pl.BlockSpec((tm, tf), lambda i, j: (i, j)),
                  pl.BlockSpec((tf, D), lambda i, j: (j, 0))],
        out_specs=pl.BlockSpec((tm, tf), lambda i, j: (i, j)),
        out_shape=SDS((T, F), BF16), compiler_params=_params(2))(dh2b, z, w_down)


def _ffn_up_bwd(dzp, w_up_t):
    T, F = dzp.shape
    D = w_up_t.shape[1]
    tm, tn, tk = _tile(T, 1024), _tile(D, 1024), _tile(F, 4096)

    def body(dzp_ref, w_ref, dn_ref):
        part = _dot(dzp_ref[...], w_ref[...], NN)

        @pl.when(pl.program_id(2) == 0)
        def _():
            dn_ref[...] = part

        @pl.when(pl.program_id(2) > 0)
        def _():
            dn_ref[...] += part

    return _CHAIN.call(
        body, name="ffn_up_bwd", grid=(T // tm, D // tn, F // tk),
        in_specs=[pl.BlockSpec((tm, tk), lambda i, j, k: (i, k)), pl.BlockSpec((tk, tn), lambda i, j, k: (k, j))],
        out_specs=pl.BlockSpec((tm, tn), lambda i, j, k: (i, j)),
        out_shape=SDS((T, D), F32), compiler_params=_params(3))(dzp, w_up_t)


def _ffn_norm_bwd(dn, dh2, h1, g):
    T, D = h1.shape
    tm = _tile(T, 256)

    def body(dn_ref, dh_ref, h_ref, g_ref, dh1_ref, dh1b_ref, dg_ref):
        @pl.when(pl.program_id(0) == 0)
        def _():
            dg_ref[...] = jnp.zeros_like(dg_ref)

        hv, dnv = h_ref[...], dn_ref[...]
        r = _rms_stats(hv)
        dg_ref[...] += jnp.sum(dnv * (hv * r), axis=0, keepdims=True)
        dh1 = dh_ref[...] + _rms_bwd(dnv, hv, r, g_ref[...])
        dh1_ref[...] = dh1
        dh1b_ref[...] = dh1.astype(BF16)

    row = pl.BlockSpec((tm, D), lambda i: (i, 0))
    vec = pl.BlockSpec((1, D), lambda i: (0, 0))
    return _CHAIN.call(
        body, name="ffn_norm_bwd", grid=(T // tm,), in_specs=[row, row, row, vec], out_specs=[row, row, vec],
        out_shape=[SDS((T, D), F32), SDS((T, D), BF16), SDS((1, D), F32)], compiler_params=_params(1))(dn, dh2, h1, g)


def _matmul_tn(a, b, name, square_a=False, col_blocks=None):
    T, K = a.shape
    N = b.shape[1]
    tk = _tile(K, 1792)
    tn = _tile(N if col_blocks is None else N // col_blocks, 1024 if tk <= 1024 else 512)

    def body(a_ref, b_ref, o_ref):
        av = a_ref[...]
        if square_a:
            af = av.astype(F32)
            av = (af * af).astype(BF16)
        o_ref[...] = _dot(av, b_ref[...], TN).astype(o_ref.dtype)

    if col_blocks is None:
        out_shape = SDS((K, N), BF16)
        out_spec = pl.BlockSpec((tk, tn), lambda i, j: (i, j))
    else:
        per = (N // col_blocks) // tn
        out_shape = SDS((col_blocks, K, N // col_blocks), BF16)
        out_spec = pl.BlockSpec((None, tk, tn), lambda i, j: (j // per, i, j % per))
    return _CHAIN.call(
        body, name=name, grid=(K // tk, N // tn),
        in_specs=[pl.BlockSpec((T, tk), lambda i, j: (0, i)), pl.BlockSpec((T, tn), lambda i, j: (0, j))],
        out_specs=out_spec, out_shape=out_shape, compiler_params=_params(2))(a, b)


def _outproj_bwd(dh1b, w, a, b, ga, gb):
    T, D = dh1b.shape
    A, B = a.shape[1], b.shape[1]
    tm = _tile(T, 512)

    def body(dh_ref, w_ref, a_ref, b_ref, ga_ref, gb_ref, da_ref, db_ref, dga_ref, dgb_ref):
        @pl.when(pl.program_id(0) == 0)
        def _():
            dga_ref[...] = jnp.zeros_like(dga_ref)
            dgb_ref[...] = jnp.zeros_like(dgb_ref)

        dmix = _dot(dh_ref[...], w_ref[...], NT)
        for src_ref, g_ref, dx_ref, dg_ref, dn in ((a_ref, ga_ref, da_ref, dga_ref, dmix[:, :A]),
                                                   (b_ref, gb_ref, db_ref, dgb_ref, dmix[:, A:])):
            xv = src_ref[...]
            r = _rms_stats(xv)
            dg_ref[...] += jnp.sum(dn * (xv * r), axis=0, keepdims=True)
            dx_ref[...] = _rms_bwd(dn, xv, r, g_ref[...])

    return _CHAIN.call(
        body, name="outproj_bwd", grid=(T // tm,),
        in_specs=[pl.BlockSpec((tm, D), lambda i: (i, 0)), _resident((A + B, D)),
                  pl.BlockSpec((tm, A), lambda i: (i, 0)), pl.BlockSpec((tm, B), lambda i: (i, 0)),
                  pl.BlockSpec((1, A), lambda i: (0, 0)), pl.BlockSpec((1, B), lambda i: (0, 0))],
        out_specs=[pl.BlockSpec((tm, A), lambda i: (i, 0)), pl.BlockSpec((tm, B), lambda i: (i, 0)),
                   pl.BlockSpec((1, A), lambda i: (0, 0)), pl.BlockSpec((1, B), lambda i: (0, 0))],
        out_shape=[SDS((T, A), F32), SDS((T, B), F32), SDS((1, A), F32), SDS((1, B), F32)],
        compiler_params=_params(1))(dh1b, w, a, b, ga, gb)


def _gmlp_bwd(proj, da, lg, lb, w_s, w_st, bs_t, A):
    T = proj.shape[0]
    G = A // GROUP_DIM
    tm = _tile(T, 512)
    nc = tm // CHUNK

    def body(u_ref, v_ref, da_ref, lg_ref, lb_ref, w_ref, wt_ref, bst_ref, duv_ref, dlg_ref, dlb_ref, dw_ref, dbs_ref):
        @pl.when(pl.program_id(0) == 0)
        def _():
            dlg_ref[...] = jnp.zeros_like(dlg_ref)
            dlb_ref[...] = jnp.zeros_like(dlb_ref)
            dw_ref[...] = jnp.zeros_like(dw_ref)
            dbs_ref[...] = jnp.zeros_like(dbs_ref)

        row = lax.broadcasted_iota(jnp.int32, (CHUNK, CHUNK), 0)
        col = lax.broadcasted_iota(jnp.int32, (CHUNK, CHUNK), 1)
        lower = row >= col
        upper = row <= col
        for g in range(G):
            sl = slice(g * GROUP_DIM, (g + 1) * GROUP_DIM)
            lgv = lg_ref[:, sl]
            vg, vg_grad = _gelu_and_grad(v_ref[:, sl])
            vhat, rstd, vn = _layer_norm_group(vg, lgv, lb_ref[:, sl])
            vnb = vn.astype(BF16)
            ug, ug_grad = _gelu_and_grad(u_ref[:, sl])
            dav = da_ref[:, sl]
            wm = jnp.where(lower, w_ref[g], 0.0).astype(BF16)
            wmt = jnp.where(upper, wt_ref[g], 0.0).astype(BF16)
            bcol = bst_ref[:, g:g + 1]
            dw_acc = jnp.zeros((CHUNK, CHUNK), F32)
            dbs_acc = jnp.zeros((CHUNK, 1), F32)
            dvn_parts = []
            dug_parts = []
            for c in range(nc):
                rs = slice(c * CHUNK, (c + 1) * CHUNK)
                mixed = _dot(wm, vnb[rs], NN) + bcol
                dug_parts.append(dav[rs] * mixed)
                dmix = dav[rs] * ug[rs]
                dbs_acc = dbs_acc + jnp.sum(dmix, axis=-1, keepdims=True)
                dmixb = dmix.astype(BF16)
                dw_acc = dw_acc + _dot(dmixb, vnb[rs], NT)
                dvn_parts.append(_dot(wmt, dmixb, NN))
            dvn = jnp.concatenate(dvn_parts, axis=0)
            dug = jnp.concatenate(dug_parts, axis=0)
            dw_ref[g] += jnp.where(lower, dw_acc, 0.0)
            dbs_ref[:, g:g + 1] += dbs_acc
            dlg_ref[:, sl] += jnp.sum(dvn * vhat, axis=0, keepdims=True)
            dlb_ref[:, sl] += jnp.sum(dvn, axis=0, keepdims=True)
            dvhat = dvn * lgv
            dvg = rstd * (dvhat - jnp.mean(dvhat, axis=-1, keepdims=True)
                          - vhat * jnp.mean(dvhat * vhat, axis=-1, keepdims=True))
            duv_ref[:, sl] = (dug * ug_grad).astype(BF16)
            duv_ref[:, A + g * GROUP_DIM:A + (g + 1) * GROUP_DIM] = (dvg * vg_grad).astype(BF16)

    return _CHAIN.call(
        body, name="gmlp_bwd", grid=(T // tm,),
        in_specs=[pl.BlockSpec((tm, A), lambda i: (i, 0)), pl.BlockSpec((tm, A), lambda i: (i, 1)),
                  pl.BlockSpec((tm, A), lambda i: (i, 0)),
                  pl.BlockSpec((1, A), lambda i: (0, 0)), pl.BlockSpec((1, A), lambda i: (0, 0)),
                  pl.BlockSpec((G, CHUNK, CHUNK), lambda i: (0, 0, 0)),
                  pl.BlockSpec((G, CHUNK, CHUNK), lambda i: (0, 0, 0)), pl.BlockSpec((CHUNK, G), lambda i: (0, 0))],
        out_specs=[pl.BlockSpec((tm, 2 * A), lambda i: (i, 0)),
                   pl.BlockSpec((1, A), lambda i: (0, 0)), pl.BlockSpec((1, A), lambda i: (0, 0)),
                   pl.BlockSpec((G, CHUNK, CHUNK), lambda i: (0, 0, 0)), pl.BlockSpec((CHUNK, G), lambda i: (0, 0))],
        out_shape=[SDS((T, 2 * A), BF16), SDS((1, A), F32), SDS((1, A), F32),
                   SDS((G, CHUNK, CHUNK), F32), SDS((CHUNK, G), F32)],
        compiler_params=_params(1))(proj, proj, da, lg, lb, w_s, w_st, bs_t)


def _attn_bwd(proj, do, duv, bias_t, sinks, A, B):
    T, P = proj.shape
    H = B // HEAD_DIM
    qpk = H // KV_HEADS
    tq = _tile(T, 512)
    nb = tq // CHUNK
    n_tiles = T // tq
    scale = HEAD_DIM ** -0.5
    rev = lambda i: n_tiles - 1 - i

    def body(sink_ref, q_ref, k_ref, v_ref, kp_ref, vp_ref, do_ref, duv_ref, bias_ref,
             dproj_ref, dbias_ref, dsink_ref, carry, dkv, sacc):
        step = pl.program_id(0)

        @pl.when(step == 0)
        def _():
            carry[...] = jnp.zeros_like(carry)
            sacc[...] = jnp.zeros_like(sacc)
            dbias_ref[...] = jnp.zeros_like(dbias_ref)

        jj = lax.broadcasted_iota(jnp.int32, (2 * CHUNK, CHUNK), 0)
        ii = lax.broadcasted_iota(jnp.int32, (2 * CHUNK, CHUNK), 1)
        in_window = (jj > ii) & (jj <= ii + CHUNK)
        first_mask = in_window & jnp.logical_or(step != n_tiles - 1, jj >= CHUNK)
        low_query = lax.broadcasted_iota(jnp.int32, (CHUNK, LANE), 1) < HEAD_DIM
        low_key = lax.broadcasted_iota(jnp.int32, (2 * CHUNK, LANE), 1) < HEAD_DIM

        def split_pair(pair_bf16):
            zero = jnp.zeros_like(pair_bf16)
            return jnp.concatenate([jnp.where(low_query, pair_bf16, zero), jnp.where(low_query, zero, pair_bf16)], axis=0)

        dproj_ref[:, :2 * A] = duv_ref[...]
        dkv[...] = jnp.zeros_like(dkv)
        for b in range(nb):
            rows = slice(b * CHUNK, (b + 1) * CHUNK)
            band = slice(b * CHUNK, (b + 2) * CHUNK)
            if b == 0:
                kprev, vprev, mask = kp_ref[...], vp_ref[...], first_mask
            else:
                prows = slice((b - 1) * CHUNK, b * CHUNK)
                kprev, vprev, mask = k_ref[prows, :], v_ref[prows, :], in_window
            kband = jnp.concatenate([kprev, k_ref[rows, :]], axis=0)
            vband = jnp.concatenate([vprev, v_ref[rows, :]], axis=0)
            k_pads = [_pad_heads(kband, g) for g in range(KV_HEADS)]
            v_pads = [_pad_heads(vband, g) for g in range(KV_HEADS)]
            queries, douts, scores, dprobs = [], [], [], []
            for pair in range(H // 2):
                cols = slice(2 * pair * HEAD_DIM, (2 * pair + 2) * HEAD_DIM)
                qs = (q_ref[rows, cols] * scale).astype(BF16)
                dob = do_ref[rows, cols].astype(BF16)
                queries.append(qs)
                douts.append(dob)
                scores += [_dot(kz, qs, NT) for kz in k_pads[2 * pair // qpk]]
                dprobs += [_dot(vz, dob, NT) for vz in v_pads[2 * pair // qpk]]
            probs, dscores = [], []
            for h in range(H):
                pt, p_sink = _softmax_with_sink(jnp.where(mask, scores[h] + bias_ref[h], NEG), sink_ref[h], 0)
                delta = jnp.sum(pt * dprobs[h], axis=0, keepdims=True)
                dst = pt * (dprobs[h] - delta)
                dbias_ref[h] += dst
                sacc[h:h + 1, :] += -(p_sink * delta)
                probs.append(pt.astype(BF16))
                dscores.append(dst.astype(BF16))
            dq_parts, dk_groups, dv_groups = [], [], []
            for g in range(KV_HEADS):
                k_both = jnp.concatenate(k_pads[g], axis=0)
                dk_acc = jnp.zeros((2 * CHUNK, LANE), F32)
                dv_acc = jnp.zeros((2 * CHUNK, LANE), F32)
                for pair in range(g * qpk // 2, (g + 1) * qpk // 2):
                    pair_heads = slice(2 * pair, 2 * pair + 2)
                    dk_acc = dk_acc + _dot(jnp.concatenate(dscores[pair_heads], axis=1), split_pair(queries[pair]), NN)
                    dv_acc = dv_acc + _dot(jnp.concatenate(probs[pair_heads], axis=1), split_pair(douts[pair]), NN)
                    dq_parts.append(_dot(jnp.concatenate(dscores[pair_heads], axis=0), k_both, TN) * scale)
                dk_groups.append(dk_acc + pltpu.roll(dk_acc, HEAD_DIM, 1))
                dv_groups.append(dv_acc + pltpu.roll(dv_acc, HEAD_DIM, 1))
            dkv[band, :LANE] += jnp.where(low_key, dk_groups[0], dk_groups[1])
            dkv[band, LANE:] += jnp.where(low_key, dv_groups[0], dv_groups[1])
            dproj_ref[rows, 2 * A:2 * A + B] = jnp.concatenate(dq_parts, axis=1).astype(BF16)
        last = slice(tq, tq + CHUNK)
        dkv[last, :] += carry[...]
        dproj_ref[:, 2 * A + B:] = dkv[CHUNK:, :].astype(BF16)
        carry[...] = dkv[:CHUNK, :]

        @pl.when(step == n_tiles - 1)
        def _():
            dsink_ref[...] = jnp.sum(sacc[...], axis=1, keepdims=True)

    specs = _attn_specs(tq, A, B, reverse_tiles=n_tiles)
    return _CHAIN.call(
        body, name="attn_bwd", grid=(n_tiles,),
        in_specs=[pl.BlockSpec(memory_space=pltpu.SMEM)] + specs
        + [pl.BlockSpec((tq, B), lambda i: (rev(i), 0)), pl.BlockSpec((tq, 2 * A), lambda i: (rev(i), 0)),
           pl.BlockSpec((H, 2 * CHUNK, CHUNK), lambda i: (0, 0, 0))],
        out_specs=[pl.BlockSpec((tq, P), lambda i: (rev(i), 0)),
                   pl.BlockSpec((H, 2 * CHUNK, CHUNK), lambda i: (0, 0, 0)), pl.BlockSpec((H, 1), lambda i: (0, 0))],
        out_shape=[SDS((T, P), BF16), SDS((H, 2 * CHUNK, CHUNK), F32), SDS((H, 1), F32)],
        scratch_shapes=[pltpu.VMEM((CHUNK, 2 * LANE), F32), pltpu.VMEM((tq + CHUNK, 2 * LANE), F32),
                        pltpu.VMEM((H, LANE), F32)],
        compiler_params=_params(1))(sinks, proj, proj, proj, proj, proj, do, duv, bias_t)


def _bias_bwd(dbias, onehot):
    H = dbias.shape[0]
    nbk = onehot.shape[1]

    def body(d_ref, oh_ref, o_ref):
        hi, mid, lo = _split3(d_ref[...])
        oh = oh_ref[...]
        o_ref[...] = _dot(hi, oh, NN) + _dot(mid, oh, NN) + _dot(lo, oh, NN)

    return _CHAIN.call(body, name="bias_bwd", in_specs=[VMEM_SPEC] * 2, out_specs=VMEM_SPEC, out_shape=SDS((H, nbk), F32),
                       compiler_params=_params(0))(dbias, onehot)


def _inproj_bwd(dproj, w_t, x, dh1, g):
    T, P = dproj.shape
    D = x.shape[1]
    tm = _tile(T, 512)

    def body(dp_ref, w_ref, x_ref, dh_ref, g_ref, dx_ref, dg_ref):
        @pl.when(pl.program_id(0) == 0)
        def _():
            dg_ref[...] = jnp.zeros_like(dg_ref)

        dn = _dot(dp_ref[...], w_ref[...], NN)
        xv = x_ref[...]
        r = _rms_stats(xv)
        dg_ref[...] += jnp.sum(dn * (xv * r), axis=0, keepdims=True)
        dx_ref[...] = dh_ref[...] + _rms_bwd(dn, xv, r, g_ref[...])

    return _CHAIN.call(
        body, name="inproj_bwd", grid=(T // tm,),
        in_specs=[pl.BlockSpec((tm, P), lambda i: (i, 0)), _resident((P, D)),
                  pl.BlockSpec((tm, D), lambda i: (i, 0)), pl.BlockSpec((tm, D), lambda i: (i, 0)),
                  pl.BlockSpec((1, D), lambda i: (0, 0))],
        out_specs=[pl.BlockSpec((tm, D), lambda i: (i, 0)), pl.BlockSpec((1, D), lambda i: (0, 0))],
        out_shape=[SDS((T, D), F32), SDS((1, D), F32)], compiler_params=_params(1))(dproj, w_t, x, dh1, g)


def _adamw(w, g, m, v):
    m = ADAM_B1 * m + (1.0 - ADAM_B1) * g
    v = ADAM_B2 * v + (1.0 - ADAM_B2) * (g * g)
    m_hat = m / (1.0 - ADAM_B1 ** ADAM_STEP)
    v_hat = v / (1.0 - ADAM_B2 ** ADAM_STEP)
    delta = -ADAM_LR * (m_hat / (jnp.sqrt(v_hat) + ADAM_EPS) + ADAM_WD * w)
    return delta, m, v


def _adam_sharded(csum, recv, w, m, v, name):
    R, C = w.shape
    tr = _tile(R, 256, 16)

    def body(own_ref, recv_ref, w_ref, m_ref, v_ref, g_ref, d_ref, nm_ref, nv_ref):
        g = own_ref[...].astype(F32)
        for r in range(3):
            g = g + recv_ref[r].astype(F32)
        delta, nm, nv = _adamw(w_ref[...], g, m_ref[...], v_ref[...])
        g_ref[...] = g
        d_ref[...] = delta
        nm_ref[...] = nm
        nv_ref[...] = nv

    blk = pl.BlockSpec((tr, C), lambda i: (i, 0))
    return _CHAIN.call(
        body, name=name, grid=(R // tr,),
        in_specs=[pl.BlockSpec((None, tr, C), lambda i: (0, i, 0)), pl.BlockSpec((3, tr, C), lambda i: (0, i, 0)),
                  blk, blk, blk],
        out_specs=[blk] * 4, out_shape=[SDS((R, C), F32)] * 4, compiler_params=_params(1))(csum, recv, w, m, v)


def _adam_small(gathered, w, m, v):
    R = w.shape[0]

    def body(p_ref, w_ref, m_ref, v_ref, g_ref, d_ref, nm_ref, nv_ref):
        g = p_ref[0]
        for d in range(1, N_DEV):
            g = g + p_ref[d]
        delta, nm, nv = _adamw(w_ref[...], g, m_ref[...], v_ref[...])
        g_ref[...] = g
        d_ref[...] = delta
        nm_ref[...] = nm
        nv_ref[...] = nv

    return _CHAIN.call(body, name="adam_small", in_specs=[VMEM_SPEC] * 4, out_specs=[VMEM_SPEC] * 4,
                       out_shape=[SDS((R, LANE), F32)] * 4,
                       compiler_params=_params(0))(gathered, w, m, v)


def _pack(arrays):
    tile = 8 * LANE
    pieces = []
    for a in arrays:
        flat = a.reshape(-1).astype(F32)
        pieces.append(jnp.pad(flat, (0, (-flat.size) % tile)))
    return jnp.concatenate(pieces).reshape(-1, LANE)


def _unpack(packed, shapes):
    tile = 8 * LANE
    flat = packed.reshape(-1)
    out, off = [], 0
    for s in shapes:
        size = int(np.prod(s))
        out.append(flat[off:off + size].reshape(s))
        off += size + (-size) % tile
    return out


def kernel(x, rel_bias_table, mix_norm_g, w_in, gate_norm_g, gate_norm_b, w_spatial, b_spatial, attn_sinks, out_norm_a_g, out_norm_b_g, w_out, ffn_norm_g, w_up, w_down, final_norm_g, loss_target, m_rel_bias_table, m_mix_norm_g, m_w_in, m_gate_norm_g, m_gate_norm_b, m_w_spatial, m_b_spatial, m_attn_sinks, m_out_norm_a_g, m_out_norm_b_g, m_w_out, m_ffn_norm_g, m_w_up, m_w_down, m_final_norm_g, v_rel_bias_table, v_mix_norm_g, v_w_in, v_gate_norm_g, v_gate_norm_b, v_w_spatial, v_b_spatial, v_attn_sinks, v_out_norm_a_g, v_out_norm_b_g, v_w_out, v_ffn_norm_g, v_w_up, v_w_down, v_final_norm_g):
    T, D = x.shape[1], x.shape[2]
    A = D // 2
    B = D // 2
    G = A // GROUP_DIM
    H = B // HEAD_DIM
    P = 2 * A + B + 2 * KV_HEADS * HEAD_DIM
    xs = x.reshape(T, D)
    target = loss_target.reshape(T, D)

    win_t, m_win_t, v_win_t = (jnp.swapaxes(a[0], 0, 1) for a in (w_in, m_w_in, v_w_in))
    shards = [win_t.astype(BF16), w_out[0].astype(BF16), w_up[0].astype(BF16), w_down[0].astype(BF16)]
    _CHAIN.token = None
    gather = _gather_begin(shards, "gather_start")

    g1, g2, g3 = mix_norm_g.reshape(1, D), ffn_norm_g.reshape(1, D), final_norm_g.reshape(1, D)
    lg, lb = gate_norm_g.reshape(1, A), gate_norm_b.reshape(1, A)
    ws = w_spatial[0]
    ws_t = jnp.swapaxes(ws, 1, 2)
    bs_t = jnp.transpose(b_spatial[0])
    ga, gb = out_norm_a_g.reshape(1, A), out_norm_b_g.reshape(1, B)
    sinks = attn_sinks.reshape(H)
    bucket, in_window = _t5_bucket()
    onehot_np = ((bucket[:, :, None] == np.arange(N_BUCKETS)) & in_window[:, :, None]).astype(np.float32)
    onehot = jnp.asarray(onehot_np.reshape(-1, N_BUCKETS)).astype(BF16)
    onehot_kq = jnp.asarray(onehot_np.transpose(1, 0, 2).reshape(-1, N_BUCKETS)).astype(BF16)

    bias, bias_t = _bias_fwd(jnp.transpose(rel_bias_table), jnp.transpose(onehot), jnp.transpose(onehot_kq))
    bias, bias_t = bias.reshape(H, CHUNK, 2 * CHUNK), bias_t.reshape(H, 2 * CHUNK, CHUNK)
    n1 = _mix_norm(xs, g1)
    _gather_step(gather, [(0, 1)], "gather_in_1")
    _gather_step(gather, [(0, 2)], "gather_in_2")
    (win_g,) = _gather_end(gather, [0], "gather_in_end")
    win_t_full = win_g.reshape(P, D)
    proj = _inproj_fwd(n1, win_t_full)
    _gather_step(gather, [(1, 1)], "gather_out_1")
    a_out = _gmlp_fwd(proj, lg, lb, ws, bs_t, A)
    _gather_step(gather, [(1, 2), (2, 1)], "gather_out_2_up_1")
    b_out = _attn_fwd(proj, bias, sinks, A, B)
    (wout_g,) = _gather_end(gather, [1], "gather_out_end")
    _gather_step(gather, [(2, 2)], "gather_up_2")
    wout_full = wout_g.reshape(A + B, D)
    h1, mixed, n2 = _outproj_fwd(a_out, b_out, ga, gb, xs, wout_full, g2)
    (wup_g,) = _gather_end(gather, [2], "gather_up_end")
    _gather_step(gather, [(3, 1)], "gather_down_1")
    wup_t = jnp.transpose(wup_g, (0, 2, 1)).reshape(-1, D)
    z = _ffn_up(n2, wup_g)
    _gather_step(gather, [(3, 2)], "gather_down_2")
    (wdown_g,) = _gather_end(gather, [3], "gather_down_end")
    h2 = _ffn_down(h1, z, wdown_g.reshape(-1, D))
    loss_part, dg3, dh2, dh2b = _final_loss(h2, g3, target)

    def reduce_to_chip(state, name):
        part, received = _sibling_exchange_end(state, name + "_sib_end")
        return _chip_exchange_begin(_chip_sum(part, received, name + "_chip_sum"), name + "_chip")

    dwdown = _matmul_tn(z, dh2b, "grad_w_down", square_a=True).reshape(wdown_g.shape)
    sib_down = _sibling_exchange_begin(dwdown, "rs_down_sib")
    dzp = _ffn_down_bwd(dh2b, z, wdown_g.reshape(-1, D))
    chip_down = reduce_to_chip(sib_down, "rs_down")
    dwup = _matmul_tn(n2, dzp, "grad_w_up", col_blocks=N_DEV)
    sib_up = _sibling_exchange_begin(dwup, "rs_up_sib")
    dh1, dh1b, dg2 = _ffn_norm_bwd(_ffn_up_bwd(dzp, wup_t), dh2, h1, g2)
    chip_up = reduce_to_chip(sib_up, "rs_up")
    da, db, dga, dgb = _outproj_bwd(dh1b, wout_full, a_out, b_out, ga, gb)
    dwout = _matmul_tn(mixed, dh1b, "grad_w_out").reshape(wout_g.shape)
    sib_out = _sibling_exchange_begin(dwout, "rs_out_sib")
    duv, dlg, dlb, dws, dbs_t = _gmlp_bwd(proj, da, lg, lb, ws, ws_t, bs_t, A)
    dproj, dbias_t, dsinks = _attn_bwd(proj, db, duv, bias_t, sinks, A, B)
    chip_out = reduce_to_chip(sib_out, "rs_out")
    dtable_t = _bias_bwd(dbias_t.reshape(H, -1), onehot_kq)
    dwin_t = _matmul_tn(dproj, n1, "grad_w_in").reshape(win_g.shape)
    sib_in = _sibling_exchange_begin(dwin_t, "rs_in_sib")
    grad_x, dg1 = _inproj_bwd(dproj, win_t_full, xs, dh1, g1)

    small_w = [rel_bias_table, mix_norm_g, gate_norm_g, gate_norm_b, w_spatial, b_spatial, attn_sinks,
               out_norm_a_g, out_norm_b_g, ffn_norm_g, final_norm_g]
    small_m = [m_rel_bias_table, m_mix_norm_g, m_gate_norm_g, m_gate_norm_b, m_w_spatial, m_b_spatial, m_attn_sinks,
               m_out_norm_a_g, m_out_norm_b_g, m_ffn_norm_g, m_final_norm_g]
    small_v = [v_rel_bias_table, v_mix_norm_g, v_gate_norm_g, v_gate_norm_b, v_w_spatial, v_b_spatial, v_attn_sinks,
               v_out_norm_a_g, v_out_norm_b_g, v_ffn_norm_g, v_final_norm_g]
    small_g = [jnp.transpose(dtable_t), dg1, dlg, dlb, dws, jnp.transpose(dbs_t), dsinks, dga, dgb, dg2, dg3]
    nothing = jnp.zeros((1, 1), F32)
    small_w, small_m, small_v, small_g = small_w + [nothing], small_m + [nothing], small_v + [nothing], small_g + [loss_part]
    shapes = [w.shape for w in small_w]
    big = [None] * 4

    def adam_of(k, state, w, m, v):
        csum, received = _chip_exchange_end(state, "rs_%d_end" % k)
        big[k] = _adam_sharded(csum, received, w, m, v, "adam_%d" % k)

    small_gather = _gather_begin([_pack(small_g)], "small_gather_start")
    chip_in = reduce_to_chip(sib_in, "rs_in")
    _gather_step(small_gather, [(0, 1)], "small_gather_1")
    adam_of(3, chip_down, w_down[0], m_w_down[0], v_w_down[0])
    _gather_step(small_gather, [(0, 2)], "small_gather_2")
    adam_of(2, chip_up, w_up[0], m_w_up[0], v_w_up[0])
    (gathered,) = _gather_end(small_gather, [0], "small_gather_end")
    sg, sd, sm, sv = [_unpack(o, shapes) for o in _adam_small(gathered, _pack(small_w), _pack(small_m), _pack(small_v))]
    adam_of(1, chip_out, w_out[0], m_w_out[0], v_w_out[0])
    adam_of(0, chip_in, win_t, m_win_t, v_win_t)
    big[0] = [jnp.swapaxes(o, 0, 1) for o in big[0]]
    big = [[o.reshape(w.shape) for o in outs] for outs, w in zip(big, (w_in, w_out, w_up, w_down))]

    loss = sg[-1].reshape(())

    order = ["s0", "s1", "b0", "s2", "s3", "s4", "s5", "s6", "s7", "s8", "b1", "s9", "b2", "b3", "s10"]

    def group(idx):
        small = (sg, sd, sm, sv)[idx]
        return [small[int(t[1:])] if t[0] == "s" else big[int(t[1:])][idx] for t in order]

    return (loss, grad_x.reshape(x.shape), *group(0), *group(1), *group(2), *group(3))
```

```python
import functools
import math

import numpy as np
import jax
import jax.numpy as jnp
from jax import lax
from jax.experimental import pallas as pl
from jax.experimental.pallas import tpu as pltpu

F32 = jnp.float32
BF16 = jnp.bfloat16
SDS = jax.ShapeDtypeStruct
MESH = pl.DeviceIdType.MESH

N_DEV = 8
EPS = 1e-5
NEG = -1e30
CHUNK = 128
GROUP_DIM = 128
HEAD_DIM = 64
KV_HEADS = 2
N_BUCKETS = 32
MAX_DISTANCE = 128
ADAM_LR, ADAM_B1, ADAM_B2, ADAM_EPS, ADAM_WD, ADAM_STEP = 0.001, 0.9, 0.999, 1e-08, 0.01, 10
GELU_C0 = math.sqrt(2.0 / math.pi)
GELU_C1 = 0.044715

V7X_VMEM_BYTES = 64 * 1024 * 1024
VMEM_LIMIT = V7X_VMEM_BYTES - 8 * 1024 * 1024
LANE = 128

NN = ((1,), (0,))
NT = ((1,), (1,))
TN = ((0,), (0,))


def _dot(a, b, dims):
    return lax.dot_general(a, b, (dims, ((), ())), preferred_element_type=F32)


def _tile(n, pref, unit=LANE):
    best = None
    for t in range(unit, min(n, pref) + 1, unit):
        if n % t == 0:
            best = t
    return n if best is None else best


def _params(n_grid):
    return pltpu.CompilerParams(dimension_semantics=("arbitrary",) * n_grid, vmem_limit_bytes=VMEM_LIMIT)


def _resident(shape):
    return pl.BlockSpec(shape, lambda i: (0, 0), pipeline_mode=pl.Buffered(1))


def _gelu(x):
    return 0.5 * x * (1.0 + jnp.tanh(GELU_C0 * (x + GELU_C1 * x * x * x)))


def _gelu_and_grad(x):
    x2 = x * x
    t = jnp.tanh(GELU_C0 * x * (1.0 + GELU_C1 * x2))
    val = 0.5 * x * (1.0 + t)
    grad = 0.5 * (1.0 + t) + 0.5 * x * (1.0 - t * t) * (GELU_C0 * (1.0 + 3.0 * GELU_C1 * x2))
    return val, grad


def _rms_stats(x):
    return lax.rsqrt(jnp.mean(x * x, axis=-1, keepdims=True) + EPS)


def _rms_bwd(dy, x, r, g):
    w = dy * g
    return r * w - x * (r * r * r) * jnp.mean(w * x, axis=-1, keepdims=True)


def _t5_bucket():
    i = np.arange(CHUNK)[:, None]
    j = np.arange(2 * CHUNK)[None, :]
    rel = np.maximum(i + CHUNK - j, 0)
    n_exact = N_BUCKETS // 2
    relf = np.maximum(rel, n_exact).astype(np.float32)
    large = n_exact + (np.log(relf / np.float32(n_exact)) / np.float32(math.log(MAX_DISTANCE / n_exact))
                       * np.float32(N_BUCKETS - n_exact)).astype(np.int32)
    large = np.minimum(large, N_BUCKETS - 1)
    bucket = np.where(rel < n_exact, rel, large)
    in_window = (i + CHUNK - j >= 0) & (i + CHUNK - j < CHUNK)
    return bucket.astype(np.int32), in_window


def _split3(x):
    hi = x.astype(BF16)
    r1 = x - hi.astype(F32)
    mid = r1.astype(BF16)
    lo = (r1 - mid.astype(F32)).astype(BF16)
    return hi, mid, lo


HBM_SPEC = pl.BlockSpec(memory_space=pltpu.HBM)


def _mesh_pos():
    return lax.axis_index("x"), lax.axis_index("y"), lax.axis_index("c")


def _dev_index(px, py, pc):
    return 4 * px + 2 * py + pc


SEM_SPEC = pl.BlockSpec(memory_space=pltpu.SEMAPHORE)
ANY_SPEC = pl.BlockSpec(memory_space=pl.ANY)
VMEM_SPEC = pl.BlockSpec(memory_space=pltpu.VMEM)
TOKEN_SPEC = VMEM_SPEC
TOKEN = SDS((8, LANE), F32)
SIDE_EFFECT = pltpu.SideEffectType.DATAFLOW_SIDE_EFFECTING


def _hbm(x):
    return pltpu.with_memory_space_constraint(x, pltpu.HBM)


class _CallChain:
    def __init__(self):
        self.token = None

    def call(self, body, *, in_specs, out_specs, out_shape, **kwargs):
        dep, n_in = self.token, len(in_specs)
        single = not isinstance(out_shape, (list, tuple))
        out_shapes = [out_shape] if single else list(out_shape)
        out_specs = [out_specs] if single else list(out_specs)
        n_out = len(out_shapes)
        n_dep = 0 if dep is None else 1
        token_spec = pl.BlockSpec((8, LANE), lambda *_: (0, 0)) if kwargs.get("grid") else VMEM_SPEC

        def chained(*refs):
            outs_at = n_in + n_dep
            body(*refs[:n_in], *refs[outs_at:outs_at + n_out], *refs[outs_at + n_out + 1:])
            token = refs[outs_at + n_out]
            token[...] = jnp.zeros_like(token)

        inner = pl.pallas_call(chained, in_specs=list(in_specs) + [ANY_SPEC] * n_dep, out_specs=out_specs + [token_spec],
                               out_shape=out_shapes + [TOKEN], **kwargs)

        def run(*operands):
            outs = inner(*operands) if dep is None else inner(*operands, dep)
            self.token = outs[n_out]
            return outs[0] if single else list(outs[:n_out])

        return run


_CHAIN = _CallChain()


def _wait_all(waits, x, y, c):
    for kind, src, dst, send_sem, recv_sem in waits:
        cp = pltpu.make_async_remote_copy(src_ref=src, dst_ref=dst, send_sem=send_sem, recv_sem=recv_sem,
                                          device_id=(x, y, c), device_id_type=MESH)
        if kind == "send":
            cp.wait_send()
        else:
            cp.wait_recv()


def _split_start(bufs, copies_of, n_sems, name, sem_sets=(), waits_of=None):
    n, ns = len(bufs), len(sem_sets)
    flat_sems = [s for pair in sem_sets for s in pair]

    def body(*refs):
        ins = refs[:n]
        sems = refs[n:n + 2 * ns]
        send_sems, recv_sems = refs[n + 2 * ns], refs[n + 2 * ns + 1]
        if waits_of is not None:
            _wait_all(waits_of(ins, [(sems[2 * i], sems[2 * i + 1]) for i in range(ns)]), *_mesh_pos())
        for src, dst, k, target in copies_of(ins):
            pltpu.make_async_remote_copy(src_ref=src, dst_ref=dst, send_sem=send_sems.at[k], recv_sem=recv_sems.at[k],
                                         device_id=target, device_id_type=MESH).start()

    outs = _CHAIN.call(
        body, name=name,
        out_shape=[pltpu.SemaphoreType.DMA((n_sems,)), pltpu.SemaphoreType.DMA((n_sems,))]
        + [pltpu.HBM(b.shape, b.dtype) for b in bufs],
        in_specs=[HBM_SPEC] * n + [SEM_SPEC] * (2 * ns), out_specs=[SEM_SPEC, SEM_SPEC] + [HBM_SPEC] * n,
        input_output_aliases={a: 2 + a for a in range(n)},
        compiler_params=pltpu.CompilerParams(has_side_effects=SIDE_EFFECT),
    )(*[_hbm(b) for b in bufs], *flat_sems)
    return outs[0], outs[1], list(outs[2:2 + n])


def _split_wait(bufs, sem_sets, waits_of, name):
    n, ns = len(bufs), len(sem_sets)
    flat_sems = [s for pair in sem_sets for s in pair]

    def body(*refs):
        ins = refs[:n]
        sems = refs[n:n + 2 * ns]
        _wait_all(waits_of(ins, [(sems[2 * i], sems[2 * i + 1]) for i in range(ns)]), *_mesh_pos())

    outs = _CHAIN.call(
        body, name=name,
        out_shape=[pltpu.HBM(b.shape, b.dtype) for b in bufs],
        in_specs=[HBM_SPEC] * n + [SEM_SPEC] * (2 * ns), out_specs=[HBM_SPEC] * n,
        input_output_aliases={a: a for a in range(n)},
        compiler_params=pltpu.CompilerParams(has_side_effects=SIDE_EFFECT),
    )(*bufs, *flat_sems)
    return list(outs)


def _gather_blocks(land):
    rows = land.shape[1]
    first = (rows // 2) // 16 * 16

    def block(px, py, pc):
        return land.at[_dev_index(px, py, pc)]

    def halves(px, py, pc):
        return (land.at[_dev_index(px, py, pc), pl.ds(0, first)], land.at[_dev_index(px, py, pc), pl.ds(first, rows - first)])

    return block, halves


def _gather_begin(shards):
    me = _dev_index(*_mesh_pos())
    lands = [lax.dynamic_update_index_in_dim(lax.empty((N_DEV,) + s.shape, s.dtype), s, me, 0) for s in shards]
    return dict(lands=lands, stage={})


STAGE_COPIES = (3, 4, 1)


def _gather_step(state, items, name):
    which = sorted({a for a, _ in items})
    at = {a: i for i, a in enumerate(which)}
    sem_sets = [state["stage"][(a, s - 1)][0] for a, s in items if s > 0]
    offset, n_sems = {}, 0
    for a, s in items:
        offset[(a, s)] = n_sems
        n_sems += STAGE_COPIES[s]

    def waits_of(ins, sems):
        x, y, c = _mesh_pos()
        out, earlier = [], 0
        for a, s in items:
            if s == 0:
                continue
            block, halves = _gather_blocks(ins[at[a]])
            send, recv = sems[earlier]
            off = state["stage"][(a, s - 1)][1]
            earlier += 1
            if s == 1:
                arrived = [(1, block(1 - x, y, c)), (2, block(x, 1 - y, c))]
            else:
                arrived = list(zip((2, 3), halves(1 - x, 1 - y, c)))
            out += [("recv", ref, ref, send.at[off + k], recv.at[off + k]) for k, ref in arrived]
        return out

    def copies_of(ins):
        x, y, c = _mesh_pos()
        sibling = (x, y, 1 - c)
        out = []
        for a, s in items:
            block, halves = _gather_blocks(ins[at[a]])
            off = offset[(a, s)]
            if s == 0:
                mine = block(x, y, c)
                out += [(mine, mine, off + 1, (1 - x, y, c)), (mine, mine, off + 2, (x, 1 - y, c)), (mine, mine, off, sibling)]
            elif s == 1:
                from_x, from_y = block(1 - x, y, c), block(x, 1 - y, c)
                out += [(halves(1 - x, y, c)[0], halves(1 - x, y, c)[0], off + 2, (x, 1 - y, c)),
                        (halves(x, 1 - y, c)[1], halves(x, 1 - y, c)[1], off + 3, (1 - x, y, c)),
                        (from_x, from_x, off, sibling), (from_y, from_y, off + 1, sibling)]
            else:
                diag = block(1 - x, 1 - y, c)
                out.append((diag, diag, off, sibling))
        return out

    send_sems, recv_sems, bufs = _split_start([state["lands"][a] for a in which], copies_of, n_sems, name,
                                              sem_sets=sem_sets, waits_of=waits_of)
    for a in which:
        state["lands"][a] = bufs[at[a]]
    for a, s in items:
        state["stage"][(a, s)] = ((send_sems, recv_sems), offset[(a, s)])


def _gather_end(state, which, name):
    sem_sets = [state["stage"][(a, s)][0] for a in which for s in range(3)]

    def waits(ins, sems):
        x, y, c = _mesh_pos()
        out = []
        for i, a in enumerate(which):
            block, halves = _gather_blocks(ins[i])
            (b_send, b_recv), (s1_send, s1_recv), (s2_send, s2_recv) = sems[3 * i:3 * i + 3]
            o0, o1, o2 = (state["stage"][(a, s)][1] for s in range(3))
            arrivals = [(block(x, y, 1 - c), b_send, b_recv, o0),
                        (block(1 - x, y, 1 - c), s1_send, s1_recv, o1), (block(x, 1 - y, 1 - c), s1_send, s1_recv, o1 + 1),
                        (block(1 - x, 1 - y, 1 - c), s2_send, s2_recv, o2)]
            mine = block(x, y, c)
            sent = [(mine, b_send, b_recv, o0 + k) for k in range(3)]
            sent += [(block(1 - x, y, c), s1_send, s1_recv, o1), (block(x, 1 - y, c), s1_send, s1_recv, o1 + 1),
                     (halves(1 - x, y, c)[0], s1_send, s1_recv, o1 + 2), (halves(x, 1 - y, c)[1], s1_send, s1_recv, o1 + 3),
                     (block(1 - x, 1 - y, c), s2_send, s2_recv, o2)]
            out += [("recv", ref, ref, s.at[k], r.at[k]) for ref, s, r, k in arrivals]
            out += [("send", ref, ref, s.at[k], r.at[k]) for ref, s, r, k in sent]
        return out

    bufs = _split_wait([state["lands"][a] for a in which], sem_sets, waits, name)
    for i, a in enumerate(which):
        state["lands"][a] = bufs[i]
    return bufs


def _sibling_exchange_begin(part, name):
    land = lax.empty((4,) + part.shape[1:], part.dtype)

    def copies_of(ins):
        x, y, c = _mesh_pos()
        return [(ins[0].at[2 * j + (1 - c)], ins[1].at[j], j, (x, y, 1 - c)) for j in range(4)]

    send_sems, recv_sems, bufs = _split_start([part, land], copies_of, 4, name)
    return dict(bufs=bufs, sems=(send_sems, recv_sems))


def _sibling_exchange_end(state, name):
    def waits(ins, sems):
        _, _, c = _mesh_pos()
        out = []
        for j in range(4):
            for kind in ("send", "recv"):
                out.append((kind, ins[0].at[2 * j + (1 - c)], ins[1].at[j], sems[0][0].at[j], sems[0][1].at[j]))
        return out

    return _split_wait(state["bufs"], [state["sems"]], waits, name)


CHIP_FLIPS = (2, 1, 3)


def _chip_exchange_begin(csum, name):
    land = lax.empty((3,) + csum.shape[1:], csum.dtype)

    def copies_of(ins):
        x, y, c = _mesh_pos()
        chips = [(1 - x, y), (x, 1 - y), (1 - x, 1 - y)]
        return [(ins[0].at[CHIP_FLIPS[r]], ins[1].at[r], r, (px, py, c)) for r, (px, py) in enumerate(chips)]

    send_sems, recv_sems, bufs = _split_start([csum, land], copies_of, 3, name)
    return dict(bufs=bufs, sems=(send_sems, recv_sems))


def _chip_exchange_end(state, name):
    def waits(ins, sems):
        out = []
        for r in range(3):
            for kind in ("send", "recv"):
                out.append((kind, ins[0].at[CHIP_FLIPS[r]], ins[1].at[r], sems[0][0].at[r], sems[0][1].at[r]))
        return out

    return _split_wait(state["bufs"], [state["sems"]], waits, name)


def _chip_sum(part, recv, name):
    _, R, C = part.shape
    tr = _tile(R, 512, 16)
    place = jnp.stack([lax.axis_index("c"), 2 * lax.axis_index("x") + lax.axis_index("y")]).astype(jnp.int32)

    def body(place_ref, p_ref, r_ref, o_ref):
        o_ref[...] = (p_ref[...].astype(F32) + r_ref[...].astype(F32)).astype(o_ref.dtype)

    def chip(p, place_ref):
        return jnp.bitwise_xor(p, place_ref[1])

    grid_spec = pltpu.PrefetchScalarGridSpec(
        num_scalar_prefetch=1, grid=(4, R // tr),
        in_specs=[pl.BlockSpec((None, tr, C), lambda p, i, place_ref: (2 * chip(p, place_ref) + place_ref[0], i, 0)),
                  pl.BlockSpec((None, tr, C), lambda p, i, place_ref: (chip(p, place_ref), i, 0))],
        out_specs=pl.BlockSpec((None, tr, C), lambda p, i, place_ref: (p, i, 0)))
    return pl.pallas_call(body, name=name, grid_spec=grid_spec, out_shape=SDS((4, R, C), part.dtype),
                          compiler_params=_params(2))(place, part, recv)


def _bias_fwd(table_t, onehot_t, onehot_kq_t):
    H = table_t.shape[0]
    n = onehot_t.shape[1]

    def body(t_ref, oh_ref, oh_kq_ref, o_ref, o_kq_ref):
        hi, mid, lo = _split3(t_ref[...])
        for src, dst in ((oh_ref, o_ref), (oh_kq_ref, o_kq_ref)):
            oh = src[...]
            dst[...] = _dot(hi, oh, NN) + _dot(mid, oh, NN) + _dot(lo, oh, NN)

    return _CHAIN.call(body, name="bias_fwd", in_specs=[VMEM_SPEC] * 3, out_specs=[VMEM_SPEC] * 2,
                       out_shape=[SDS((H, n), F32)] * 2, compiler_params=_params(0))(table_t, onehot_t, onehot_kq_t)


def _mix_norm(x, g):
    T, D = x.shape
    tm = _tile(T, 512)

    def body(x_ref, g_ref, n_ref):
        xv = x_ref[...]
        n_ref[...] = (xv * _rms_stats(xv) * g_ref[...]).astype(BF16)

    row = pl.BlockSpec((tm, D), lambda i: (i, 0))
    return _CHAIN.call(body, name="mix_norm", grid=(T // tm,), in_specs=[row, pl.BlockSpec((1, D), lambda i: (0, 0))],
                       out_specs=row, out_shape=SDS((T, D), BF16), compiler_params=_params(1))(x, g)


def _inproj_fwd(n, w_t):
    T, D = n.shape
    P = w_t.shape[0]
    tm = _tile(T, 512)

    def body(n_ref, w_ref, proj_ref):
        proj_ref[...] = _dot(n_ref[...], w_ref[...], NT)

    return _CHAIN.call(
        body, name="inproj_fwd", grid=(T // tm,),
        in_specs=[pl.BlockSpec((tm, D), lambda i: (i, 0)), _resident((P, D))],
        out_specs=pl.BlockSpec((tm, P), lambda i: (i, 0)),
        out_shape=SDS((T, P), F32), compiler_params=_params(1))(n, w_t)


def _layer_norm_group(vg, lg, lb):
    mu = jnp.mean(vg, axis=-1, keepdims=True)
    xc = vg - mu
    rstd = lax.rsqrt(jnp.mean(xc * xc, axis=-1, keepdims=True) + EPS)
    vhat = xc * rstd
    return vhat, rstd, vhat * lg + lb


def _gmlp_fwd(proj, lg, lb, w_s, bs_t, A):
    T = proj.shape[0]
    G = A // GROUP_DIM
    tm = _tile(T, 512)
    nc = tm // CHUNK

    def body(u_ref, v_ref, lg_ref, lb_ref, w_ref, bst_ref, a_ref):
        row = lax.broadcasted_iota(jnp.int32, (CHUNK, CHUNK), 0)
        col = lax.broadcasted_iota(jnp.int32, (CHUNK, CHUNK), 1)
        causal = row >= col
        for g in range(G):
            sl = slice(g * GROUP_DIM, (g + 1) * GROUP_DIM)
            _, _, vn = _layer_norm_group(_gelu(v_ref[:, sl]), lg_ref[:, sl], lb_ref[:, sl])
            vnb = vn.astype(BF16)
            wm = jnp.where(causal, w_ref[g], 0.0).astype(BF16)
            ug = _gelu(u_ref[:, sl])
            bcol = bst_ref[:, g:g + 1]
            for c in range(nc):
                rs = slice(c * CHUNK, (c + 1) * CHUNK)
                a_ref[rs, sl] = ug[rs] * (_dot(wm, vnb[rs], NN) + bcol)

    return _CHAIN.call(
        body, name="gmlp_fwd", grid=(T // tm,),
        in_specs=[pl.BlockSpec((tm, A), lambda i: (i, 0)), pl.BlockSpec((tm, A), lambda i: (i, 1)),
                  pl.BlockSpec((1, A), lambda i: (0, 0)), pl.BlockSpec((1, A), lambda i: (0, 0)),
                  pl.BlockSpec((G, CHUNK, CHUNK), lambda i: (0, 0, 0)), pl.BlockSpec((CHUNK, G), lambda i: (0, 0))],
        out_specs=pl.BlockSpec((tm, A), lambda i: (i, 0)),
        out_shape=SDS((T, A), F32), compiler_params=_params(1))(proj, proj, lg, lb, w_s, bs_t)


def _attn_masks(first_tile):
    ii = lax.broadcasted_iota(jnp.int32, (CHUNK, 2 * CHUNK), 0)
    jj = lax.broadcasted_iota(jnp.int32, (CHUNK, 2 * CHUNK), 1)
    in_window = (jj > ii) & (jj <= ii + CHUNK)
    first_mask = in_window & jnp.logical_or(jnp.logical_not(first_tile), jj >= CHUNK)
    return in_window, first_mask


def _softmax_with_sink(s, sink, axis):
    m = jnp.maximum(jnp.max(s, axis=axis, keepdims=True), sink)
    p = jnp.exp(s - m)
    e_sink = jnp.exp(sink - m)
    inv = 1.0 / (jnp.sum(p, axis=axis, keepdims=True) + e_sink)
    return p * inv, e_sink * inv


def _pad_heads(band, group):
    lane = lax.broadcasted_iota(jnp.int32, band.shape, 1)
    if group == 0:
        low = jnp.where(lane < HEAD_DIM, band, 0.0)
        high = pltpu.roll(low, HEAD_DIM, 1)
    else:
        high = jnp.where(lane >= HEAD_DIM, band, 0.0)
        low = pltpu.roll(high, HEAD_DIM, 1)
    return low.astype(BF16), high.astype(BF16)


def _attn_specs(tq, A, B, reverse_tiles=None):
    nb = tq // CHUNK
    kcol = (2 * A + B) // LANE
    if reverse_tiles is None:
        tile = lambda i: i
    else:
        tile = lambda i: reverse_tiles - 1 - i
    prev = lambda i: jnp.maximum(tile(i) * nb - 1, 0)
    return [pl.BlockSpec((tq, B), lambda i: (tile(i), 2 * A // B)),
            pl.BlockSpec((tq, LANE), lambda i: (tile(i), kcol)),
            pl.BlockSpec((tq, LANE), lambda i: (tile(i), kcol + 1)),
            pl.BlockSpec((CHUNK, LANE), lambda i: (prev(i), kcol)),
            pl.BlockSpec((CHUNK, LANE), lambda i: (prev(i), kcol + 1))]


def _attn_fwd(proj, bias, sinks, A, B):
    T = proj.shape[0]
    H = B // HEAD_DIM
    qpk = H // KV_HEADS
    tq = _tile(T, 512)
    nb = tq // CHUNK

    scale = HEAD_DIM ** -0.5

    def body(sink_ref, q_ref, k_ref, v_ref, kp_ref, vp_ref, bias_ref, o_ref):
        in_window, first_mask = _attn_masks(pl.program_id(0) == 0)
        for b in range(nb):
            rows = slice(b * CHUNK, (b + 1) * CHUNK)
            if b == 0:
                kprev, vprev, mask = kp_ref[...], vp_ref[...], first_mask
            else:
                prows = slice((b - 1) * CHUNK, b * CHUNK)
                kprev, vprev, mask = k_ref[prows, :], v_ref[prows, :], in_window
            kband = jnp.concatenate([kprev, k_ref[rows, :]], axis=0)
            vband = jnp.concatenate([vprev, v_ref[rows, :]], axis=0)
            k_pads = [_pad_heads(kband, g) for g in range(KV_HEADS)]
            v_both = [jnp.concatenate(_pad_heads(vband, g), axis=0) for g in range(KV_HEADS)]
            scores = []
            for pair in range(H // 2):
                h = 2 * pair
                qs = (q_ref[rows, h * HEAD_DIM:(h + 2) * HEAD_DIM] * scale).astype(BF16)
                scores += [_dot(qs, kz, NT) for kz in k_pads[h // qpk]]
            probs = [_softmax_with_sink(jnp.where(mask, s + bias_ref[h], NEG), sink_ref[h], -1)[0].astype(BF16)
                     for h, s in enumerate(scores)]
            outs = [_dot(jnp.concatenate(probs[h:h + 2], axis=1), v_both[h // qpk], NN) for h in range(0, H, 2)]
            o_ref[rows, :] = jnp.concatenate(outs, axis=1)

    return _CHAIN.call(
        body, name="attn_fwd", grid=(T // tq,),
        in_specs=[pl.BlockSpec(memory_space=pltpu.SMEM)] + _attn_specs(tq, A, B)
        + [pl.BlockSpec((H, CHUNK, 2 * CHUNK), lambda i: (0, 0, 0))],
        out_specs=pl.BlockSpec((tq, B), lambda i: (i, 0)),
        out_shape=SDS((T, B), F32), compiler_params=_params(1))(sinks, proj, proj, proj, proj, proj, bias)


def _outproj_fwd(a, b, ga, gb, x, w, g_ffn):
    T, A = a.shape
    B = b.shape[1]
    D = x.shape[1]
    tm = _tile(T, 512)

    def body(a_ref, b_ref, ga_ref, gb_ref, x_ref, w_ref, gf_ref, h_ref, mix_ref, n_ref):
        av, bv = a_ref[...], b_ref[...]
        mix_ref[:, :A] = (av * _rms_stats(av) * ga_ref[...]).astype(BF16)
        mix_ref[:, A:] = (bv * _rms_stats(bv) * gb_ref[...]).astype(BF16)
        hv = x_ref[...] + _dot(mix_ref[...], w_ref[...], NN)
        h_ref[...] = hv
        n_ref[...] = (hv * _rms_stats(hv) * gf_ref[...]).astype(BF16)

    row = pl.BlockSpec((tm, D), lambda i: (i, 0))
    return _CHAIN.call(
        body, name="outproj_fwd", grid=(T // tm,),
        in_specs=[pl.BlockSpec((tm, A), lambda i: (i, 0)), pl.BlockSpec((tm, B), lambda i: (i, 0)),
                  pl.BlockSpec((1, A), lambda i: (0, 0)), pl.BlockSpec((1, B), lambda i: (0, 0)),
                  row, _resident((A + B, D)), pl.BlockSpec((1, D), lambda i: (0, 0))],
        out_specs=[row, pl.BlockSpec((tm, A + B), lambda i: (i, 0)), row],
        out_shape=[SDS((T, D), F32), SDS((T, A + B), BF16), SDS((T, D), BF16)],
        compiler_params=_params(1))(a, b, ga, gb, x, w, g_ffn)


def _ffn_up(n, w_up):
    T, D = n.shape
    Fb = w_up.shape[2]
    F = N_DEV * Fb
    tm, tf = _tile(T, 1024), _tile(Fb, 1024)
    per = Fb // tf

    def body(n_ref, wu_ref, z_ref):
        z_ref[...] = jnp.maximum(_dot(n_ref[...], wu_ref[...], NN), 0.0).astype(BF16)

    return _CHAIN.call(
        body, name="ffn_up", grid=(T // tm, F // tf),
        in_specs=[pl.BlockSpec((tm, D), lambda i, j: (i, 0)),
                  pl.BlockSpec((None, D, tf), lambda i, j: (j // per, 0, j % per))],
        out_specs=pl.BlockSpec((tm, tf), lambda i, j: (i, j)),
        out_shape=SDS((T, F), BF16), compiler_params=_params(2))(n, w_up)


def _ffn_down(h1, z, w_down):
    T, D = h1.shape
    F = w_down.shape[0]
    tm, tn, tk = _tile(T, 1024), _tile(D, 1024), _tile(F, 4096)

    def body(h_ref, z_ref, wd_ref, h2_ref):
        k = pl.program_id(2)

        @pl.when(k == 0)
        def _():
            h2_ref[...] = h_ref[...]

        zf = z_ref[...].astype(F32)
        h2_ref[...] += _dot((zf * zf).astype(BF16), wd_ref[...], NN)

    return _CHAIN.call(
        body, name="ffn_down", grid=(T // tm, D // tn, F // tk),
        in_specs=[pl.BlockSpec((tm, tn), lambda i, j, k: (i, j)), pl.BlockSpec((tm, tk), lambda i, j, k: (i, k)),
                  pl.BlockSpec((tk, tn), lambda i, j, k: (k, j))],
        out_specs=pl.BlockSpec((tm, tn), lambda i, j, k: (i, j)),
        out_shape=SDS((T, D), F32), compiler_params=_params(3))(h1, z, w_down)


def _final_loss(h2, g, target):
    T, D = h2.shape
    tm = _tile(T, 512)

    def body(h_ref, g_ref, t_ref, loss_ref, dg_ref, dh_ref, dhb_ref):
        @pl.when(pl.program_id(0) == 0)
        def _():
            loss_ref[...] = jnp.zeros_like(loss_ref)
            dg_ref[...] = jnp.zeros_like(dg_ref)

        hv, gv = h_ref[...], g_ref[...]
        r = _rms_stats(hv)
        hn = hv * r
        e = hn * gv - t_ref[...]
        loss_ref[...] += (0.5 / D) * jnp.sum(jnp.sum(e * e, axis=0, keepdims=True), axis=-1, keepdims=True)
        dy = e * (1.0 / D)
        dg_ref[...] += jnp.sum(dy * hn, axis=0, keepdims=True)
        dh = _rms_bwd(dy, hv, r, gv)
        dh_ref[...] = dh
        dhb_ref[...] = dh.astype(BF16)

    return _CHAIN.call(
        body, name="final_loss", grid=(T // tm,),
        in_specs=[pl.BlockSpec((tm, D), lambda i: (i, 0)), pl.BlockSpec((1, D), lambda i: (0, 0)),
                  pl.BlockSpec((tm, D), lambda i: (i, 0))],
        out_specs=[pl.BlockSpec((1, 1), lambda i: (0, 0)), pl.BlockSpec((1, D), lambda i: (0, 0)),
                   pl.BlockSpec((tm, D), lambda i: (i, 0)), pl.BlockSpec((tm, D), lambda i: (i, 0))],
        out_shape=[SDS((1, 1), F32), SDS((1, D), F32), SDS((T, D), F32), SDS((T, D), BF16)],
        compiler_params=_params(1))(h2, g, target)


def _ffn_down_bwd(dh2b, z, w_down):
    T, D = dh2b.shape
    F = w_down.shape[0]
    tm, tf = _tile(T, 1024), _tile(F, 1024)

    def body(dh_ref, z_ref, wd_ref, dzp_ref):
        dzz = _dot(dh_ref[...], wd_ref[...], NT)
        dzp_ref[...] = (dzz * (2.0 * z_ref[...].astype(F32))).astype(BF16)

    return _CHAIN.call(
        body, name="ffn_down_bwd", grid=(T // tm, F // tf),
        in_specs=[pl.BlockSpec((tm, D), lambda i, j: (i, 0)), pl.BlockSpec((tm, tf), lambda i, j: (i, j)),
                  pl.BlockSpec((tf, D), lambda i, j: (j, 0))],
        out_specs=pl.BlockSpec((tm, tf), lambda i, j: (i, j)),
        out_shape=SDS((T, F), BF16), compiler_params=_params(2))(dh2b, z, w_down)


def _ffn_up_bwd(dzp, w_up_t):
    T, F = dzp.shape
    D = w_up_t.shape[1]
    tm, tn, tk = _tile(T, 1024), _tile(D, 1024), _tile(F, 4096)

    def body(dzp_ref, w_ref, dn_ref):
        part = _dot(dzp_ref[...], w_ref[...], NN)

        @pl.when(pl.program_id(2) == 0)
        def _():
            dn_ref[...] = part

        @pl.when(pl.program_id(2) > 0)
        def _():
            dn_ref[...] += part

    return _CHAIN.call(
        body, name="ffn_up_bwd", grid=(T // tm, D // tn, F // tk),
        in_specs=[pl.BlockSpec((tm, tk), lambda i, j, k: (i, k)), pl.BlockSpec((tk, tn), lambda i, j, k: (k, j))],
        out_specs=pl.BlockSpec((tm, tn), lambda i, j, k: (i, j)),
        out_shape=SDS((T, D), F32), compiler_params=_params(3))(dzp, w_up_t)


def _ffn_norm_bwd(dn, dh2, h1, g):
    T, D = h1.shape
    tm = _tile(T, 256)

    def body(dn_ref, dh_ref, h_ref, g_ref, dh1_ref, dh1b_ref, dg_ref):
        @pl.when(pl.program_id(0) == 0)
        def _():
            dg_ref[...] = jnp.zeros_like(dg_ref)

        hv, dnv = h_ref[...], dn_ref[...]
        r = _rms_stats(hv)
        dg_ref[...] += jnp.sum(dnv * (hv * r), axis=0, keepdims=True)
        dh1 = dh_ref[...] + _rms_bwd(dnv, hv, r, g_ref[...])
        dh1_ref[...] = dh1
        dh1b_ref[...] = dh1.astype(BF16)

    row = pl.BlockSpec((tm, D), lambda i: (i, 0))
    vec = pl.BlockSpec((1, D), lambda i: (0, 0))
    return _CHAIN.call(
        body, name="ffn_norm_bwd", grid=(T // tm,), in_specs=[row, row, row, vec], out_specs=[row, row, vec],
        out_shape=[SDS((T, D), F32), SDS((T, D), BF16), SDS((1, D), F32)], compiler_params=_params(1))(dn, dh2, h1, g)


def _matmul_tn(a, b, name, square_a=False, col_blocks=None):
    T, K = a.shape
    N = b.shape[1]
    tk = _tile(K, 1792)
    tn = _tile(N if col_blocks is None else N // col_blocks, 1024 if tk <= 1024 else 512)

    def body(a_ref, b_ref, o_ref):
        av = a_ref[...]
        if square_a:
            af = av.astype(F32)
            av = (af * af).astype(BF16)
        o_ref[...] = _dot(av, b_ref[...], TN).astype(o_ref.dtype)

    if col_blocks is None:
        out_shape = SDS((K, N), BF16)
        out_spec = pl.BlockSpec((tk, tn), lambda i, j: (i, j))
    else:
        per = (N // col_blocks) // tn
        out_shape = SDS((col_blocks, K, N // col_blocks), BF16)
        out_spec = pl.BlockSpec((None, tk, tn), lambda i, j: (j // per, i, j % per))
    return _CHAIN.call(
        body, name=name, grid=(K // tk, N // tn),
        in_specs=[pl.BlockSpec((T, tk), lambda i, j: (0, i)), pl.BlockSpec((T, tn), lambda i, j: (0, j))],
        out_specs=out_spec, out_shape=out_shape, compiler_params=_params(2))(a, b)


def _outproj_bwd(dh1b, w, a, b, ga, gb):
    T, D = dh1b.shape
    A, B = a.shape[1], b.shape[1]
    tm = _tile(T, 512)

    def body(dh_ref, w_ref, a_ref, b_ref, ga_ref, gb_ref, da_ref, db_ref, dga_ref, dgb_ref):
        @pl.when(pl.program_id(0) == 0)
        def _():
            dga_ref[...] = jnp.zeros_like(dga_ref)
            dgb_ref[...] = jnp.zeros_like(dgb_ref)

        dmix = _dot(dh_ref[...], w_ref[...], NT)
        for src_ref, g_ref, dx_ref, dg_ref, dn in ((a_ref, ga_ref, da_ref, dga_ref, dmix[:, :A]),
                                                   (b_ref, gb_ref, db_ref, dgb_ref, dmix[:, A:])):
            xv = src_ref[...]
            r = _rms_stats(xv)
            dg_ref[...] += jnp.sum(dn * (xv * r), axis=0, keepdims=True)
            dx_ref[...] = _rms_bwd(dn, xv, r, g_ref[...])

    return _CHAIN.call(
        body, name="outproj_bwd", grid=(T // tm,),
        in_specs=[pl.BlockSpec((tm, D), lambda i: (i, 0)), _resident((A + B, D)),
                  pl.BlockSpec((tm, A), lambda i: (i, 0)), pl.BlockSpec((tm, B), lambda i: (i, 0)),
                  pl.BlockSpec((1, A), lambda i: (0, 0)), pl.BlockSpec((1, B), lambda i: (0, 0))],
        out_specs=[pl.BlockSpec((tm, A), lambda i: (i, 0)), pl.BlockSpec((tm, B), lambda i: (i, 0)),
                   pl.BlockSpec((1, A), lambda i: (0, 0)), pl.BlockSpec((1, B), lambda i: (0, 0))],
        out_shape=[SDS((T, A), F32), SDS((T, B), F32), SDS((1, A), F32), SDS((1, B), F32)],
        compiler_params=_params(1))(dh1b, w, a, b, ga, gb)


def _gmlp_bwd(proj, da, lg, lb, w_s, w_st, bs_t, A):
    T = proj.shape[0]
    G = A // GROUP_DIM
    tm = _tile(T, 512)
    nc = tm // CHUNK

    def body(u_ref, v_ref, da_ref, lg_ref, lb_ref, w_ref, wt_ref, bst_ref, duv_ref, dlg_ref, dlb_ref, dw_ref, dbs_ref):
        @pl.when(pl.program_id(0) == 0)
        def _():
            dlg_ref[...] = jnp.zeros_like(dlg_ref)
            dlb_ref[...] = jnp.zeros_like(dlb_ref)
            dw_ref[...] = jnp.zeros_like(dw_ref)
            dbs_ref[...] = jnp.zeros_like(dbs_ref)

        row = lax.broadcasted_iota(jnp.int32, (CHUNK, CHUNK), 0)
        col = lax.broadcasted_iota(jnp.int32, (CHUNK, CHUNK), 1)
        lower = row >= col
        upper = row <= col
        for g in range(G):
            sl = slice(g * GROUP_DIM, (g + 1) * GROUP_DIM)
            lgv = lg_ref[:, sl]
            vg, vg_grad = _gelu_and_grad(v_ref[:, sl])
            vhat, rstd, vn = _layer_norm_group(vg, lgv, lb_ref[:, sl])
            vnb = vn.astype(BF16)
            ug, ug_grad = _gelu_and_grad(u_ref[:, sl])
            dav = da_ref[:, sl]
            wm = jnp.where(lower, w_ref[g], 0.0).astype(BF16)
            wmt = jnp.where(upper, wt_ref[g], 0.0).astype(BF16)
            bcol = bst_ref[:, g:g + 1]
            dw_acc = jnp.zeros((CHUNK, CHUNK), F32)
            dbs_acc = jnp.zeros((CHUNK, 1), F32)
            dvn_parts = []
            dug_parts = []
            for c in range(nc):
                rs = slice(c * CHUNK, (c + 1) * CHUNK)
                mixed = _dot(wm, vnb[rs], NN) + bcol
                dug_parts.append(dav[rs] * mixed)
                dmix = dav[rs] * ug[rs]
                dbs_acc = dbs_acc + jnp.sum(dmix, axis=-1, keepdims=True)
                dmixb = dmix.astype(BF16)
                dw_acc = dw_acc + _dot(dmixb, vnb[rs], NT)
                dvn_parts.append(_dot(wmt, dmixb, NN))
            dvn = jnp.concatenate(dvn_parts, axis=0)
            dug = jnp.concatenate(dug_parts, axis=0)
            dw_ref[g] += jnp.where(lower, dw_acc, 0.0)
            dbs_ref[:, g:g + 1] += dbs_acc
            dlg_ref[:, sl] += jnp.sum(dvn * vhat, axis=0, keepdims=True)
            dlb_ref[:, sl] += jnp.sum(dvn, axis=0, keepdims=True)
            dvhat = dvn * lgv
            dvg = rstd * (dvhat - jnp.mean(dvhat, axis=-1, keepdims=True)
                          - vhat * jnp.mean(dvhat * vhat, axis=-1, keepdims=True))
            duv_ref[:, sl] = (dug * ug_grad).astype(BF16)
            duv_ref[:, A + g * GROUP_DIM:A + (g + 1) * GROUP_DIM] = (dvg * vg_grad).astype(BF16)

    return _CHAIN.call(
        body, name="gmlp_bwd", grid=(T // tm,),
        in_specs=[pl.BlockSpec((tm, A), lambda i: (i, 0)), pl.BlockSpec((tm, A), lambda i: (i, 1)),
                  pl.BlockSpec((tm, A), lambda i: (i, 0)),
                  pl.BlockSpec((1, A), lambda i: (0, 0)), pl.BlockSpec((1, A), lambda i: (0, 0)),
                  pl.BlockSpec((G, CHUNK, CHUNK), lambda i: (0, 0, 0)),
                  pl.BlockSpec((G, CHUNK, CHUNK), lambda i: (0, 0, 0)), pl.BlockSpec((CHUNK, G), lambda i: (0, 0))],
        out_specs=[pl.BlockSpec((tm, 2 * A), lambda i: (i, 0)),
                   pl.BlockSpec((1, A), lambda i: (0, 0)), pl.BlockSpec((1, A), lambda i: (0, 0)),
                   pl.BlockSpec((G, CHUNK, CHUNK), lambda i: (0, 0, 0)), pl.BlockSpec((CHUNK, G), lambda i: (0, 0))],
        out_shape=[SDS((T, 2 * A), BF16), SDS((1, A), F32), SDS((1, A), F32),
                   SDS((G, CHUNK, CHUNK), F32), SDS((CHUNK, G), F32)],
        compiler_params=_params(1))(proj, proj, da, lg, lb, w_s, w_st, bs_t)


def _attn_bwd(proj, do, duv, bias_t, sinks, A, B):
    T, P = proj.shape
    H = B // HEAD_DIM
    qpk = H // KV_HEADS
    tq = _tile(T, 512)
    nb = tq // CHUNK
    n_tiles = T // tq
    scale = HEAD_DIM ** -0.5
    rev = lambda i: n_tiles - 1 - i

    def body(sink_ref, q_ref, k_ref, v_ref, kp_ref, vp_ref, do_ref, duv_ref, bias_ref,
             dproj_ref, dbias_ref, dsink_ref, carry, dkv, sacc):
        step = pl.program_id(0)

        @pl.when(step == 0)
        def _():
            carry[...] = jnp.zeros_like(carry)
            sacc[...] = jnp.zeros_like(sacc)
            dbias_ref[...] = jnp.zeros_like(dbias_ref)

        jj = lax.broadcasted_iota(jnp.int32, (2 * CHUNK, CHUNK), 0)
        ii = lax.broadcasted_iota(jnp.int32, (2 * CHUNK, CHUNK), 1)
        in_window = (jj > ii) & (jj <= ii + CHUNK)
        first_mask = in_window & jnp.logical_or(step != n_tiles - 1, jj >= CHUNK)
        low_query = lax.broadcasted_iota(jnp.int32, (CHUNK, LANE), 1) < HEAD_DIM
        low_key = lax.broadcasted_iota(jnp.int32, (2 * CHUNK, LANE), 1) < HEAD_DIM

        def split_pair(pair_bf16):
            zero = jnp.zeros_like(pair_bf16)
            return jnp.concatenate([jnp.where(low_query, pair_bf16, zero), jnp.where(low_query, zero, pair_bf16)], axis=0)

        dproj_ref[:, :2 * A] = duv_ref[...]
        dkv[...] = jnp.zeros_like(dkv)
        for b in range(nb):
            rows = slice(b * CHUNK, (b + 1) * CHUNK)
            band = slice(b * CHUNK, (b + 2) * CHUNK)
            if b == 0:
                kprev, vprev, mask = kp_ref[...], vp_ref[...], first_mask
            else:
                prows = slice((b - 1) * CHUNK, b * CHUNK)
                kprev, vprev, mask = k_ref[prows, :], v_ref[prows, :], in_window
            kband = jnp.concatenate([kprev, k_ref[rows, :]], axis=0)
            vband = jnp.concatenate([vprev, v_ref[rows, :]], axis=0)
            k_pads = [_pad_heads(kband, g) for g in range(KV_HEADS)]
            v_pads = [_pad_heads(vband, g) for g in range(KV_HEADS)]
            queries, douts, scores, dprobs = [], [], [], []
            for pair in range(H // 2):
                cols = slice(2 * pair * HEAD_DIM, (2 * pair + 2) * HEAD_DIM)
                qs = (q_ref[rows, cols] * scale).astype(BF16)
                dob = do_ref[rows, cols].astype(BF16)
                queries.append(qs)
                douts.append(dob)
                scores += [_dot(kz, qs, NT) for kz in k_pads[2 * pair // qpk]]
                dprobs += [_dot(vz, dob, NT) for vz in v_pads[2 * pair // qpk]]
            probs, dscores = [], []
            for h in range(H):
                pt, p_sink = _softmax_with_sink(jnp.where(mask, scores[h] + bias_ref[h], NEG), sink_ref[h], 0)
                delta = jnp.sum(pt * dprobs[h], axis=0, keepdims=True)
                dst = pt * (dprobs[h] - delta)
                dbias_ref[h] += dst
                sacc[h:h + 1, :] += -(p_sink * delta)
                probs.append(pt.astype(BF16))
                dscores.append(dst.astype(BF16))
            dq_parts, dk_groups, dv_groups = [], [], []
            for g in range(KV_HEADS):
                k_both = jnp.concatenate(k_pads[g], axis=0)
                dk_acc = jnp.zeros((2 * CHUNK, LANE), F32)
                dv_acc = jnp.zeros((2 * CHUNK, LANE), F32)
                for pair in range(g * qpk // 2, (g + 1) * qpk // 2):
                    pair_heads = slice(2 * pair, 2 * pair + 2)
                    dk_acc = dk_acc + _dot(jnp.concatenate(dscores[pair_heads], axis=1), split_pair(queries[pair]), NN)
                    dv_acc = dv_acc + _dot(jnp.concatenate(probs[pair_heads], axis=1), split_pair(douts[pair]), NN)
                    dq_parts.append(_dot(jnp.concatenate(dscores[pair_heads], axis=0), k_both, TN) * scale)
                dk_groups.append(dk_acc + pltpu.roll(dk_acc, HEAD_DIM, 1))
                dv_groups.append(dv_acc + pltpu.roll(dv_acc, HEAD_DIM, 1))
            dkv[band, :LANE] += jnp.where(low_key, dk_groups[0], dk_groups[1])
            dkv[band, LANE:] += jnp.where(low_key, dv_groups[0], dv_groups[1])
            dproj_ref[rows, 2 * A:2 * A + B] = jnp.concatenate(dq_parts, axis=1).astype(BF16)
        last = slice(tq, tq + CHUNK)
        dkv[last, :] += carry[...]
        dproj_ref[:, 2 * A + B:] = dkv[CHUNK:, :].astype(BF16)
        carry[...] = dkv[:CHUNK, :]

        @pl.when(step == n_tiles - 1)
        def _():
            dsink_ref[...] = jnp.sum(sacc[...], axis=1, keepdims=True)

    specs = _attn_specs(tq, A, B, reverse_tiles=n_tiles)
    return _CHAIN.call(
        body, name="attn_bwd", grid=(n_tiles,),
        in_specs=[pl.BlockSpec(memory_space=pltpu.SMEM)] + specs
        + [pl.BlockSpec((tq, B), lambda i: (rev(i), 0)), pl.BlockSpec((tq, 2 * A), lambda i: (rev(i), 0)),
           pl.BlockSpec((H, 2 * CHUNK, CHUNK), lambda i: (0, 0, 0))],
        out_specs=[pl.BlockSpec((tq, P), lambda i: (rev(i), 0)),
                   pl.BlockSpec((H, 2 * CHUNK, CHUNK), lambda i: (0, 0, 0)), pl.BlockSpec((H, 1), lambda i: (0, 0))],
        out_shape=[SDS((T, P), BF16), SDS((H, 2 * CHUNK, CHUNK), F32), SDS((H, 1), F32)],
        scratch_shapes=[pltpu.VMEM((CHUNK, 2 * LANE), F32), pltpu.VMEM((tq + CHUNK, 2 * LANE), F32),
                        pltpu.VMEM((H, LANE), F32)],
        compiler_params=_params(1))(sinks, proj, proj, proj, proj, proj, do, duv, bias_t)


def _bias_bwd(dbias, onehot):
    H = dbias.shape[0]
    nbk = onehot.shape[1]

    def body(d_ref, oh_ref, o_ref):
        hi, mid, lo = _split3(d_ref[...])
        oh = oh_ref[...]
        o_ref[...] = _dot(hi, oh, NN) + _dot(mid, oh, NN) + _dot(lo, oh, NN)

    return _CHAIN.call(body, name="bias_bwd", in_specs=[VMEM_SPEC] * 2, out_specs=VMEM_SPEC, out_shape=SDS((H, nbk), F32),
                       compiler_params=_params(0))(dbias, onehot)


def _inproj_bwd(dproj, w_t, x, dh1, g):
    T, P = dproj.shape
    D = x.shape[1]
    tm = _tile(T, 512)

    def body(dp_ref, w_ref, x_ref, dh_ref, g_ref, dx_ref, dg_ref):
        @pl.when(pl.program_id(0) == 0)
        def _():
            dg_ref[...] = jnp.zeros_like(dg_ref)

        dn = _dot(dp_ref[...], w_ref[...], NN)
        xv = x_ref[...]
        r = _rms_stats(xv)
        dg_ref[...] += jnp.sum(dn * (xv * r), axis=0, keepdims=True)
        dx_ref[...] = dh_ref[...] + _rms_bwd(dn, xv, r, g_ref[...])

    return _CHAIN.call(
        body, name="inproj_bwd", grid=(T // tm,),
        in_specs=[pl.BlockSpec((tm, P), lambda i: (i, 0)), _resident((P, D)),
                  pl.BlockSpec((tm, D), lambda i: (i, 0)), pl.BlockSpec((tm, D), lambda i: (i, 0)),
                  pl.BlockSpec((1, D), lambda i: (0, 0))],
        out_specs=[pl.BlockSpec((tm, D), lambda i: (i, 0)), pl.BlockSpec((1, D), lambda i: (0, 0))],
        out_shape=[SDS((T, D), F32), SDS((1, D), F32)], compiler_params=_params(1))(dproj, w_t, x, dh1, g)


def _adamw(w, g, m, v):
    m = ADAM_B1 * m + (1.0 - ADAM_B1) * g
    v = ADAM_B2 * v + (1.0 - ADAM_B2) * (g * g)
    m_hat = m / (1.0 - ADAM_B1 ** ADAM_STEP)
    v_hat = v / (1.0 - ADAM_B2 ** ADAM_STEP)
    delta = -ADAM_LR * (m_hat / (jnp.sqrt(v_hat) + ADAM_EPS) + ADAM_WD * w)
    return delta, m, v


def _adam_sharded(csum, recv, w, m, v, name):
    R, C = w.shape
    tr = _tile(R, 256, 16)

    def body(own_ref, recv_ref, w_ref, m_ref, v_ref, g_ref, d_ref, nm_ref, nv_ref):
        g = own_ref[...].astype(F32)
        for r in range(3):
            g = g + recv_ref[r].astype(F32)
        delta, nm, nv = _adamw(w_ref[...], g, m_ref[...], v_ref[...])
        g_ref[...] = g
        d_ref[...] = delta
        nm_ref[...] = nm
        nv_ref[...] = nv

    blk = pl.BlockSpec((tr, C), lambda i: (i, 0))
    return _CHAIN.call(
        body, name=name, grid=(R // tr,),
        in_specs=[pl.BlockSpec((None, tr, C), lambda i: (0, i, 0)), pl.BlockSpec((3, tr, C), lambda i: (0, i, 0)),
                  blk, blk, blk],
        out_specs=[blk] * 4, out_shape=[SDS((R, C), F32)] * 4, compiler_params=_params(1))(csum, recv, w, m, v)


def _adam_small(gathered, w, m, v):
    R = w.shape[0]

    def body(p_ref, w_ref, m_ref, v_ref, g_ref, d_ref, nm_ref, nv_ref):
        g = p_ref[0]
        for d in range(1, N_DEV):
            g = g + p_ref[d]
        delta, nm, nv = _adamw(w_ref[...], g, m_ref[...], v_ref[...])
        g_ref[...] = g
        d_ref[...] = delta
        nm_ref[...] = nm
        nv_ref[...] = nv

    return _CHAIN.call(body, name="adam_small", in_specs=[VMEM_SPEC] * 4, out_specs=[VMEM_SPEC] * 4,
                       out_shape=[SDS((R, LANE), F32)] * 4,
                       compiler_params=_params(0))(gathered, w, m, v)


def _pack(arrays):
    tile = 8 * LANE
    pieces = []
    for a in arrays:
        flat = a.reshape(-1).astype(F32)
        pieces.append(jnp.pad(flat, (0, (-flat.size) % tile)))
    return jnp.concatenate(pieces).reshape(-1, LANE)


def _unpack(packed, shapes):
    tile = 8 * LANE
    flat = packed.reshape(-1)
    out, off = [], 0
    for s in shapes:
        size = int(np.prod(s))
        out.append(flat[off:off + size].reshape(s))
        off += size + (-size) % tile
    return out


def kernel(x, rel_bias_table, mix_norm_g, w_in, gate_norm_g, gate_norm_b, w_spatial, b_spatial, attn_sinks, out_norm_a_g, out_norm_b_g, w_out, ffn_norm_g, w_up, w_down, final_norm_g, loss_target, m_rel_bias_table, m_mix_norm_g, m_w_in, m_gate_norm_g, m_gate_norm_b, m_w_spatial, m_b_spatial, m_attn_sinks, m_out_norm_a_g, m_out_norm_b_g, m_w_out, m_ffn_norm_g, m_w_up, m_w_down, m_final_norm_g, v_rel_bias_table, v_mix_norm_g, v_w_in, v_gate_norm_g, v_gate_norm_b, v_w_spatial, v_b_spatial, v_attn_sinks, v_out_norm_a_g, v_out_norm_b_g, v_w_out, v_ffn_norm_g, v_w_up, v_w_down, v_final_norm_g):
    T, D = x.shape[1], x.shape[2]
    A = D // 2
    B = D // 2
    G = A // GROUP_DIM
    H = B // HEAD_DIM
    P = 2 * A + B + 2 * KV_HEADS * HEAD_DIM
    xs = x.reshape(T, D)
    target = loss_target.reshape(T, D)

    win_t, m_win_t, v_win_t = (jnp.swapaxes(a[0], 0, 1) for a in (w_in, m_w_in, v_w_in))
    shards = [win_t.astype(BF16), w_out[0].astype(BF16), w_up[0].astype(BF16), w_down[0].astype(BF16)]
    _CHAIN.token = None
    gather = _gather_begin(shards)
    _gather_step(gather, [(0, 0)], "gather_start")

    g1, g2, g3 = mix_norm_g.reshape(1, D), ffn_norm_g.reshape(1, D), final_norm_g.reshape(1, D)
    lg, lb = gate_norm_g.reshape(1, A), gate_norm_b.reshape(1, A)
    ws = w_spatial[0]
    ws_t = jnp.swapaxes(ws, 1, 2)
    bs_t = jnp.transpose(b_spatial[0])
    ga, gb = out_norm_a_g.reshape(1, A), out_norm_b_g.reshape(1, B)
    sinks = attn_sinks.reshape(H)
    bucket, in_window = _t5_bucket()
    onehot_np = ((bucket[:, :, None] == np.arange(N_BUCKETS)) & in_window[:, :, None]).astype(np.float32)
    onehot = jnp.asarray(onehot_np.reshape(-1, N_BUCKETS)).astype(BF16)
    onehot_kq = jnp.asarray(onehot_np.transpose(1, 0, 2).reshape(-1, N_BUCKETS)).astype(BF16)

    bias, bias_t = _bias_fwd(jnp.transpose(rel_bias_table), jnp.transpose(onehot), jnp.transpose(onehot_kq))
    bias, bias_t = bias.reshape(H, CHUNK, 2 * CHUNK), bias_t.reshape(H, 2 * CHUNK, CHUNK)
    n1 = _mix_norm(xs, g1)
    _gather_step(gather, [(0, 1), (1, 0), (2, 0)], "gather_in_1")
    _gather_step(gather, [(0, 2)], "gather_in_2")
    (win_g,) = _gather_end(gather, [0], "gather_in_end")
    win_t_full = win_g.reshape(P, D)
    proj = _inproj_fwd(n1, win_t_full)
    _gather_step(gather, [(1, 1)], "gather_out_1")
    a_out = _gmlp_fwd(proj, lg, lb, ws, bs_t, A)
    _gather_step(gather, [(1, 2), (2, 1), (3, 0)], "gather_out_2_up_1")
    b_out = _attn_fwd(proj, bias, sinks, A, B)
    (wout_g,) = _gather_end(gather, [1], "gather_out_end")
    _gather_step(gather, [(2, 2)], "gather_up_2")
    wout_full = wout_g.reshape(A + B, D)
    h1, mixed, n2 = _outproj_fwd(a_out, b_out, ga, gb, xs, wout_full, g2)
    (wup_g,) = _gather_end(gather, [2], "gather_up_end")
    _gather_step(gather, [(3, 1)], "gather_down_1")
    wup_t = jnp.transpose(wup_g, (0, 2, 1)).reshape(-1, D)
    z = _ffn_up(n2, wup_g)
    _gather_step(gather, [(3, 2)], "gather_down_2")
    (wdown_g,) = _gather_end(gather, [3], "gather_down_end")
    h2 = _ffn_down(h1, z, wdown_g.reshape(-1, D))
    loss_part, dg3, dh2, dh2b = _final_loss(h2, g3, target)

    def reduce_to_chip(state, name):
        part, received = _sibling_exchange_end(state, name + "_sib_end")
        return _chip_exchange_begin(_chip_sum(part, received, name + "_chip_sum"), name + "_chip")

    dwdown = _matmul_tn(z, dh2b, "grad_w_down", square_a=True).reshape(wdown_g.shape)
    sib_down = _sibling_exchange_begin(dwdown, "rs_down_sib")
    dzp = _ffn_down_bwd(dh2b, z, wdown_g.reshape(-1, D))
    chip_down = reduce_to_chip(sib_down, "rs_down")
    dwup = _matmul_tn(n2, dzp, "grad_w_up", col_blocks=N_DEV)
    sib_up = _sibling_exchange_begin(dwup, "rs_up_sib")
    dh1, dh1b, dg2 = _ffn_norm_bwd(_ffn_up_bwd(dzp, wup_t), dh2, h1, g2)
    chip_up = reduce_to_chip(sib_up, "rs_up")
    da, db, dga, dgb = _outproj_bwd(dh1b, wout_full, a_out, b_out, ga, gb)
    dwout = _matmul_tn(mixed, dh1b, "grad_w_out").reshape(wout_g.shape)
    sib_out = _sibling_exchange_begin(dwout, "rs_out_sib")
    duv, dlg, dlb, dws, dbs_t = _gmlp_bwd(proj, da, lg, lb, ws, ws_t, bs_t, A)
    dproj, dbias_t, dsinks = _attn_bwd(proj, db, duv, bias_t, sinks, A, B)
    chip_out = reduce_to_chip(sib_out, "rs_out")
    dtable_t = _bias_bwd(dbias_t.reshape(H, -1), onehot_kq)
    dwin_t = _matmul_tn(dproj, n1, "grad_w_in").reshape(win_g.shape)
    sib_in = _sibling_exchange_begin(dwin_t, "rs_in_sib")
    grad_x, dg1 = _inproj_bwd(dproj, win_t_full, xs, dh1, g1)

    small_w = [rel_bias_table, mix_norm_g, gate_norm_g, gate_norm_b, w_spatial, b_spatial, attn_sinks,
               out_norm_a_g, out_norm_b_g, ffn_norm_g, final_norm_g]
    small_m = [m_rel_bias_table, m_mix_norm_g, m_gate_norm_g, m_gate_norm_b, m_w_spatial, m_b_spatial, m_attn_sinks,
               m_out_norm_a_g, m_out_norm_b_g, m_ffn_norm_g, m_final_norm_g]
    small_v = [v_rel_bias_table, v_mix_norm_g, v_gate_norm_g, v_gate_norm_b, v_w_spatial, v_b_spatial, v_attn_sinks,
               v_out_norm_a_g, v_out_norm_b_g, v_ffn_norm_g, v_final_norm_g]
    small_g = [jnp.transpose(dtable_t), dg1, dlg, dlb, dws, jnp.transpose(dbs_t), dsinks, dga, dgb, dg2, dg3]
    nothing = jnp.zeros((1, 1), F32)
    small_w, small_m, small_v, small_g = small_w + [nothing], small_m + [nothing], small_v + [nothing], small_g + [loss_part]
    shapes = [w.shape for w in small_w]
    big = [None] * 4

    def adam_of(k, state, w, m, v):
        csum, received = _chip_exchange_end(state, "rs_%d_end" % k)
        big[k] = _adam_sharded(csum, received, w, m, v, "adam_%d" % k)

    small_gather = _gather_begin([_pack(small_g)])
    _gather_step(small_gather, [(0, 0)], "small_gather_start")
    chip_in = reduce_to_chip(sib_in, "rs_in")
    _gather_step(small_gather, [(0, 1)], "small_gather_1")
    adam_of(3, chip_down, w_down[0], m_w_down[0], v_w_down[0])
    _gather_step(small_gather, [(0, 2)], "small_gather_2")
    adam_of(2, chip_up, w_up[0], m_w_up[0], v_w_up[0])
    (gathered,) = _gather_end(small_gather, [0], "small_gather_end")
    sg, sd, sm, sv = [_unpack(o, shapes) for o in _adam_small(gathered, _pack(small_w), _pack(small_m), _pack(small_v))]
    adam_of(1, chip_out, w_out[0], m_w_out[0], v_w_out[0])
    adam_of(0, chip_in, win_t, m_win_t, v_win_t)
    big[0] = [jnp.swapaxes(o, 0, 1) for o in big[0]]
    big = [[o.reshape(w.shape) for o in outs] for outs, w in zip(big, (w_in, w_out, w_up, w_down))]

    loss = sg[-1].reshape(())

    order = ["s0", "s1", "b0", "s2", "s3", "s4", "s5", "s6", "s7", "s8", "b1", "s9", "b2", "b3", "s10"]

    def group(idx):
        small = (sg, sd, sm, sv)[idx]
        return [small[int(t[1:])] if t[0] == "s" else big[int(t[1:])][idx] for t in order]

    return (loss, grad_x.reshape(x.shape), *group(0), *group(1), *group(2), *group(3))
```

```python
import functools
import math

import numpy as np
import jax
import jax.numpy as jnp
from jax import lax
from jax.experimental import pallas as pl
from jax.experimental.pallas import tpu as pltpu

F32 = jnp.float32
BF16 = jnp.bfloat16
SDS = jax.ShapeDtypeStruct
MESH = pl.DeviceIdType.MESH

N_DEV = 8
EPS = 1e-5
NEG = -1e30
CHUNK = 128
GROUP_DIM = 128
HEAD_DIM = 64
KV_HEADS = 2
N_BUCKETS = 32
MAX_DISTANCE = 128
ADAM_LR, ADAM_B1, ADAM_B2, ADAM_EPS, ADAM_WD, ADAM_STEP = 0.001, 0.9, 0.999, 1e-08, 0.01, 10
GELU_C0 = math.sqrt(2.0 / math.pi)
GELU_C1 = 0.044715

V7X_VMEM_BYTES = 64 * 1024 * 1024
VMEM_LIMIT = V7X_VMEM_BYTES - 8 * 1024 * 1024
LANE = 128

NN = ((1,), (0,))
NT = ((1,), (1,))
TN = ((0,), (0,))


def _dot(a, b, dims):
    return lax.dot_general(a, b, (dims, ((), ())), preferred_element_type=F32)


def _tile(n, pref, unit=LANE):
    best = None
    for t in range(unit, min(n, pref) + 1, unit):
        if n % t == 0:
            best = t
    return n if best is None else best


def _params(n_grid):
    return pltpu.CompilerParams(dimension_semantics=("arbitrary",) * n_grid, vmem_limit_bytes=VMEM_LIMIT)


def _resident(shape):
    return pl.BlockSpec(shape, lambda i: (0, 0), pipeline_mode=pl.Buffered(1))


def _gelu(x):
    return 0.5 * x * (1.0 + jnp.tanh(GELU_C0 * (x + GELU_C1 * x * x * x)))


def _gelu_and_grad(x):
    x2 = x * x
    t = jnp.tanh(GELU_C0 * x * (1.0 + GELU_C1 * x2))
    val = 0.5 * x * (1.0 + t)
    grad = 0.5 * (1.0 + t) + 0.5 * x * (1.0 - t * t) * (GELU_C0 * (1.0 + 3.0 * GELU_C1 * x2))
    return val, grad


def _rms_stats(x):
    return lax.rsqrt(jnp.mean(x * x, axis=-1, keepdims=True) + EPS)


def _rms_bwd(dy, x, r, g):
    w = dy * g
    return r * w - x * (r * r * r) * jnp.mean(w * x, axis=-1, keepdims=True)


def _t5_bucket():
    i = np.arange(CHUNK)[:, None]
    j = np.arange(2 * CHUNK)[None, :]
    rel = np.maximum(i + CHUNK - j, 0)
    n_exact = N_BUCKETS // 2
    relf = np.maximum(rel, n_exact).astype(np.float32)
    large = n_exact + (np.log(relf / np.float32(n_exact)) / np.float32(math.log(MAX_DISTANCE / n_exact))
                       * np.float32(N_BUCKETS - n_exact)).astype(np.int32)
    large = np.minimum(large, N_BUCKETS - 1)
    bucket = np.where(rel < n_exact, rel, large)
    in_window = (i + CHUNK - j >= 0) & (i + CHUNK - j < CHUNK)
    return bucket.astype(np.int32), in_window


def _split3(x):
    hi = x.astype(BF16)
    r1 = x - hi.astype(F32)
    mid = r1.astype(BF16)
    lo = (r1 - mid.astype(F32)).astype(BF16)
    return hi, mid, lo


HBM_SPEC = pl.BlockSpec(memory_space=pltpu.HBM)


def _mesh_pos():
    return lax.axis_index("x"), lax.axis_index("y"), lax.axis_index("c")


def _dev_index(px, py, pc):
    return 4 * px + 2 * py + pc


SEM_SPEC = pl.BlockSpec(memory_space=pltpu.SEMAPHORE)
ANY_SPEC = pl.BlockSpec(memory_space=pl.ANY)
VMEM_SPEC = pl.BlockSpec(memory_space=pltpu.VMEM)
TOKEN_SPEC = VMEM_SPEC
TOKEN = SDS((8, LANE), F32)
SIDE_EFFECT = pltpu.SideEffectType.DATAFLOW_SIDE_EFFECTING


def _hbm(x):
    return pltpu.with_memory_space_constraint(x, pltpu.HBM)


class _CallChain:
    def __init__(self):
        self.token = None

    def call(self, body, *, in_specs, out_specs, out_shape, **kwargs):
        dep, n_in = self.token, len(in_specs)
        single = not isinstance(out_shape, (list, tuple))
        out_shapes = [out_shape] if single else list(out_shape)
        out_specs = [out_specs] if single else list(out_specs)
        n_out = len(out_shapes)
        n_dep = 0 if dep is None else 1
        token_spec = pl.BlockSpec((8, LANE), lambda *_: (0, 0)) if kwargs.get("grid") else VMEM_SPEC

        def chained(*refs):
            outs_at = n_in + n_dep
            body(*refs[:n_in], *refs[outs_at:outs_at + n_out], *refs[outs_at + n_out + 1:])
            token = refs[outs_at + n_out]
            token[...] = jnp.zeros_like(token)

        inner = pl.pallas_call(chained, in_specs=list(in_specs) + [ANY_SPEC] * n_dep, out_specs=out_specs + [token_spec],
                               out_shape=out_shapes + [TOKEN], **kwargs)

        def run(*operands):
            outs = inner(*operands) if dep is None else inner(*operands, dep)
            self.token = outs[n_out]
            return outs[0] if single else list(outs[:n_out])

        return run


_CHAIN = _CallChain()


def _wait_all(waits, x, y, c):
    for kind, src, dst, send_sem, recv_sem in waits:
        cp = pltpu.make_async_remote_copy(src_ref=src, dst_ref=dst, send_sem=send_sem, recv_sem=recv_sem,
                                          device_id=(x, y, c), device_id_type=MESH)
        if kind == "send":
            cp.wait_send()
        else:
            cp.wait_recv()


def _split_start(bufs, copies_of, n_sems, name, sem_sets=(), waits_of=None):
    n, ns = len(bufs), len(sem_sets)
    flat_sems = [s for pair in sem_sets for s in pair]

    def body(*refs):
        ins = refs[:n]
        sems = refs[n:n + 2 * ns]
        send_sems, recv_sems = refs[n + 2 * ns], refs[n + 2 * ns + 1]
        if waits_of is not None:
            _wait_all(waits_of(ins, [(sems[2 * i], sems[2 * i + 1]) for i in range(ns)]), *_mesh_pos())
        for src, dst, k, target in copies_of(ins):
            pltpu.make_async_remote_copy(src_ref=src, dst_ref=dst, send_sem=send_sems.at[k], recv_sem=recv_sems.at[k],
                                         device_id=target, device_id_type=MESH).start()

    outs = _CHAIN.call(
        body, name=name,
        out_shape=[pltpu.SemaphoreType.DMA((n_sems,)), pltpu.SemaphoreType.DMA((n_sems,))]
        + [pltpu.HBM(b.shape, b.dtype) for b in bufs],
        in_specs=[HBM_SPEC] * n + [SEM_SPEC] * (2 * ns), out_specs=[SEM_SPEC, SEM_SPEC] + [HBM_SPEC] * n,
        input_output_aliases={a: 2 + a for a in range(n)},
        compiler_params=pltpu.CompilerParams(has_side_effects=SIDE_EFFECT),
    )(*[_hbm(b) for b in bufs], *flat_sems)
    return outs[0], outs[1], list(outs[2:2 + n])


def _split_wait(bufs, sem_sets, waits_of, name):
    n, ns = len(bufs), len(sem_sets)
    flat_sems = [s for pair in sem_sets for s in pair]

    def body(*refs):
        ins = refs[:n]
        sems = refs[n:n + 2 * ns]
        _wait_all(waits_of(ins, [(sems[2 * i], sems[2 * i + 1]) for i in range(ns)]), *_mesh_pos())

    outs = _CHAIN.call(
        body, name=name,
        out_shape=[pltpu.HBM(b.shape, b.dtype) for b in bufs],
        in_specs=[HBM_SPEC] * n + [SEM_SPEC] * (2 * ns), out_specs=[HBM_SPEC] * n,
        input_output_aliases={a: a for a in range(n)},
        compiler_params=pltpu.CompilerParams(has_side_effects=SIDE_EFFECT),
    )(*bufs, *flat_sems)
    return list(outs)


def _gather_blocks(land):
    rows = land.shape[1]
    first = (rows // 2) // 16 * 16

    def block(px, py, pc):
        return land.at[_dev_index(px, py, pc)]

    def halves(px, py, pc):
        return (land.at[_dev_index(px, py, pc), pl.ds(0, first)], land.at[_dev_index(px, py, pc), pl.ds(first, rows - first)])

    return block, halves


def _gather_begin(shards):
    me = _dev_index(*_mesh_pos())
    lands = [lax.dynamic_update_index_in_dim(lax.empty((N_DEV,) + s.shape, s.dtype), s, me, 0) for s in shards]
    return dict(lands=lands, stage={})


STAGE_COPIES = (3, 4, 1)


def _gather_step(state, items, name):
    which = sorted({a for a, _ in items})
    at = {a: i for i, a in enumerate(which)}
    sem_sets = [state["stage"][(a, s - 1)][0] for a, s in items if s > 0]
    offset, n_sems = {}, 0
    for a, s in items:
        offset[(a, s)] = n_sems
        n_sems += STAGE_COPIES[s]

    def waits_of(ins, sems):
        x, y, c = _mesh_pos()
        out, earlier = [], 0
        for a, s in items:
            if s == 0:
                continue
            block, halves = _gather_blocks(ins[at[a]])
            send, recv = sems[earlier]
            off = state["stage"][(a, s - 1)][1]
            earlier += 1
            if s == 1:
                arrived = [(1, block(1 - x, y, c)), (2, block(x, 1 - y, c))]
            else:
                arrived = list(zip((2, 3), halves(1 - x, 1 - y, c)))
            out += [("recv", ref, ref, send.at[off + k], recv.at[off + k]) for k, ref in arrived]
        return out

    def copies_of(ins):
        x, y, c = _mesh_pos()
        sibling = (x, y, 1 - c)
        out = []
        for a, s in items:
            block, halves = _gather_blocks(ins[at[a]])
            off = offset[(a, s)]
            if s == 0:
                mine = block(x, y, c)
                out += [(mine, mine, off + 1, (1 - x, y, c)), (mine, mine, off + 2, (x, 1 - y, c)), (mine, mine, off, sibling)]
            elif s == 1:
                from_x, from_y = block(1 - x, y, c), block(x, 1 - y, c)
                out += [(halves(1 - x, y, c)[0], halves(1 - x, y, c)[0], off + 2, (x, 1 - y, c)),
                        (halves(x, 1 - y, c)[1], halves(x, 1 - y, c)[1], off + 3, (1 - x, y, c)),
                        (from_x, from_x, off, sibling), (from_y, from_y, off + 1, sibling)]
            else:
                diag = block(1 - x, 1 - y, c)
                out.append((diag, diag, off, sibling))
        return out

    send_sems, recv_sems, bufs = _split_start([state["lands"][a] for a in which], copies_of, n_sems, name,
                                              sem_sets=sem_sets, waits_of=waits_of)
    for a in which:
        state["lands"][a] = bufs[at[a]]
    for a, s in items:
        state["stage"][(a, s)] = ((send_sems, recv_sems), offset[(a, s)])


def _gather_end(state, which, name):
    sem_sets = [state["stage"][(a, s)][0] for a in which for s in range(3)]

    def waits(ins, sems):
        x, y, c = _mesh_pos()
        out = []
        for i, a in enumerate(which):
            block, halves = _gather_blocks(ins[i])
            (b_send, b_recv), (s1_send, s1_recv), (s2_send, s2_recv) = sems[3 * i:3 * i + 3]
            o0, o1, o2 = (state["stage"][(a, s)][1] for s in range(3))
            arrivals = [(block(x, y, 1 - c), b_send, b_recv, o0),
                        (block(1 - x, y, 1 - c), s1_send, s1_recv, o1), (block(x, 1 - y, 1 - c), s1_send, s1_recv, o1 + 1),
                        (block(1 - x, 1 - y, 1 - c), s2_send, s2_recv, o2)]
            mine = block(x, y, c)
            sent = [(mine, b_send, b_recv, o0 + k) for k in range(3)]
            sent += [(block(1 - x, y, c), s1_send, s1_recv, o1), (block(x, 1 - y, c), s1_send, s1_recv, o1 + 1),
                     (halves(1 - x, y, c)[0], s1_send, s1_recv, o1 + 2), (halves(x, 1 - y, c)[1], s1_send, s1_recv, o1 + 3),
                     (block(1 - x, 1 - y, c), s2_send, s2_recv, o2)]
            out += [("recv", ref, ref, s.at[k], r.at[k]) for ref, s, r, k in arrivals]
            out += [("send", ref, ref, s.at[k], r.at[k]) for ref, s, r, k in sent]
        return out

    bufs = _split_wait([state["lands"][a] for a in which], sem_sets, waits, name)
    for i, a in enumerate(which):
        state["lands"][a] = bufs[i]
    return bufs


def _sibling_exchange_begin(part, name):
    land = lax.empty((4,) + part.shape[1:], part.dtype)

    def copies_of(ins):
        x, y, c = _mesh_pos()
        return [(ins[0].at[2 * j + (1 - c)], ins[1].at[j], j, (x, y, 1 - c)) for j in range(4)]

    send_sems, recv_sems, bufs = _split_start([part, land], copies_of, 4, name)
    return dict(bufs=bufs, sems=(send_sems, recv_sems))


def _sibling_exchange_end(state, name):
    def waits(ins, sems):
        _, _, c = _mesh_pos()
        out = []
        for j in range(4):
            for kind in ("send", "recv"):
                out.append((kind, ins[0].at[2 * j + (1 - c)], ins[1].at[j], sems[0][0].at[j], sems[0][1].at[j]))
        return out

    return _split_wait(state["bufs"], [state["sems"]], waits, name)


CHIP_FLIPS = (2, 1, 3)


def _chip_exchange_begin(csum, name):
    land = lax.empty((3,) + csum.shape[1:], csum.dtype)

    def copies_of(ins):
        x, y, c = _mesh_pos()
        chips = [(1 - x, y), (x, 1 - y), (1 - x, 1 - y)]
        return [(ins[0].at[CHIP_FLIPS[r]], ins[1].at[r], r, (px, py, c)) for r, (px, py) in enumerate(chips)]

    send_sems, recv_sems, bufs = _split_start([csum, land], copies_of, 3, name)
    return dict(bufs=bufs, sems=(send_sems, recv_sems))


def _chip_exchange_end(state, name):
    def waits(ins, sems):
        out = []
        for r in range(3):
            for kind in ("send", "recv"):
                out.append((kind, ins[0].at[CHIP_FLIPS[r]], ins[1].at[r], sems[0][0].at[r], sems[0][1].at[r]))
        return out

    return _split_wait(state["bufs"], [state["sems"]], waits, name)


def _chip_sum(part, recv, name):
    _, R, C = part.shape
    tr = _tile(R, 512, 16)
    place = jnp.stack([lax.axis_index("c"), 2 * lax.axis_index("x") + lax.axis_index("y")]).astype(jnp.int32)

    def body(place_ref, p_ref, r_ref, o_ref):
        o_ref[...] = (p_ref[...].astype(F32) + r_ref[...].astype(F32)).astype(o_ref.dtype)

    def chip(p, place_ref):
        return jnp.bitwise_xor(p, place_ref[1])

    grid_spec = pltpu.PrefetchScalarGridSpec(
        num_scalar_prefetch=1, grid=(4, R // tr),
        in_specs=[pl.BlockSpec((None, tr, C), lambda p, i, place_ref: (2 * chip(p, place_ref) + place_ref[0], i, 0)),
                  pl.BlockSpec((None, tr, C), lambda p, i, place_ref: (chip(p, place_ref), i, 0))],
        out_specs=pl.BlockSpec((None, tr, C), lambda p, i, place_ref: (p, i, 0)))
    return pl.pallas_call(body, name=name, grid_spec=grid_spec, out_shape=SDS((4, R, C), part.dtype),
                          compiler_params=_params(2))(place, part, recv)


def _bias_fwd(table_t, onehot_t, onehot_kq_t):
    H = table_t.shape[0]
    n = onehot_t.shape[1]

    def body(t_ref, oh_ref, oh_kq_ref, o_ref, o_kq_ref):
        hi, mid, lo = _split3(t_ref[...])
        for src, dst in ((oh_ref, o_ref), (oh_kq_ref, o_kq_ref)):
            oh = src[...]
            dst[...] = _dot(hi, oh, NN) + _dot(mid, oh, NN) + _dot(lo, oh, NN)

    return _CHAIN.call(body, name="bias_fwd", in_specs=[VMEM_SPEC] * 3, out_specs=[VMEM_SPEC] * 2,
                       out_shape=[SDS((H, n), F32)] * 2, compiler_params=_params(0))(table_t, onehot_t, onehot_kq_t)


def _mix_norm(x, g):
    T, D = x.shape
    tm = _tile(T, 512)

    def body(x_ref, g_ref, n_ref):
        xv = x_ref[...]
        n_ref[...] = (xv * _rms_stats(xv) * g_ref[...]).astype(BF16)

    row = pl.BlockSpec((tm, D), lambda i: (i, 0))
    return _CHAIN.call(body, name="mix_norm", grid=(T // tm,), in_specs=[row, pl.BlockSpec((1, D), lambda i: (0, 0))],
                       out_specs=row, out_shape=SDS((T, D), BF16), compiler_params=_params(1))(x, g)


def _inproj_fwd(n, w_t):
    T, D = n.shape
    P = w_t.shape[0]
    tm = _tile(T, 512)

    def body(n_ref, w_ref, proj_ref):
        proj_ref[...] = _dot(n_ref[...], w_ref[...], NT)

    return _CHAIN.call(
        body, name="inproj_fwd", grid=(T // tm,),
        in_specs=[pl.BlockSpec((tm, D), lambda i: (i, 0)), _resident((P, D))],
        out_specs=pl.BlockSpec((tm, P), lambda i: (i, 0)),
        out_shape=SDS((T, P), F32), compiler_params=_params(1))(n, w_t)


def _layer_norm_group(vg, lg, lb):
    mu = jnp.mean(vg, axis=-1, keepdims=True)
    xc = vg - mu
    rstd = lax.rsqrt(jnp.mean(xc * xc, axis=-1, keepdims=True) + EPS)
    vhat = xc * rstd
    return vhat, rstd, vhat * lg + lb


def _gmlp_fwd(proj, lg, lb, w_s, bs_t, A):
    T = proj.shape[0]
    G = A // GROUP_DIM
    tm = _tile(T, 512)
    nc = tm // CHUNK

    def body(u_ref, v_ref, lg_ref, lb_ref, w_ref, bst_ref, a_ref):
        row = lax.broadcasted_iota(jnp.int32, (CHUNK, CHUNK), 0)
        col = lax.broadcasted_iota(jnp.int32, (CHUNK, CHUNK), 1)
        causal = row >= col
        for g in range(G):
            sl = slice(g * GROUP_DIM, (g + 1) * GROUP_DIM)
            _, _, vn = _layer_norm_group(_gelu(v_ref[:, sl]), lg_ref[:, sl], lb_ref[:, sl])
            vnb = vn.astype(BF16)
            wm = jnp.where(causal, w_ref[g], 0.0).astype(BF16)
            ug = _gelu(u_ref[:, sl])
            bcol = bst_ref[:, g:g + 1]
            for c in range(nc):
                rs = slice(c * CHUNK, (c + 1) * CHUNK)
                a_ref[rs, sl] = ug[rs] * (_dot(wm, vnb[rs], NN) + bcol)

    return _CHAIN.call(
        body, name="gmlp_fwd", grid=(T // tm,),
        in_specs=[pl.BlockSpec((tm, A), lambda i: (i, 0)), pl.BlockSpec((tm, A), lambda i: (i, 1)),
                  pl.BlockSpec((1, A), lambda i: (0, 0)), pl.BlockSpec((1, A), lambda i: (0, 0)),
                  pl.BlockSpec((G, CHUNK, CHUNK), lambda i: (0, 0, 0)), pl.BlockSpec((CHUNK, G), lambda i: (0, 0))],
        out_specs=pl.BlockSpec((tm, A), lambda i: (i, 0)),
        out_shape=SDS((T, A), F32), compiler_params=_params(1))(proj, proj, lg, lb, w_s, bs_t)


def _attn_masks(first_tile):
    ii = lax.broadcasted_iota(jnp.int32, (CHUNK, 2 * CHUNK), 0)
    jj = lax.broadcasted_iota(jnp.int32, (CHUNK, 2 * CHUNK), 1)
    in_window = (jj > ii) & (jj <= ii + CHUNK)
    first_mask = in_window & jnp.logical_or(jnp.logical_not(first_tile), jj >= CHUNK)
    return in_window, first_mask


def _softmax_with_sink(s, sink, axis):
    m = jnp.maximum(jnp.max(s, axis=axis, keepdims=True), sink)
    p = jnp.exp(s - m)
    e_sink = jnp.exp(sink - m)
    inv = 1.0 / (jnp.sum(p, axis=axis, keepdims=True) + e_sink)
    return p * inv, e_sink * inv


def _pad_heads(band, group):
    lane = lax.broadcasted_iota(jnp.int32, band.shape, 1)
    if group == 0:
        low = jnp.where(lane < HEAD_DIM, band, 0.0)
        high = pltpu.roll(low, HEAD_DIM, 1)
    else:
        high = jnp.where(lane >= HEAD_DIM, band, 0.0)
        low = pltpu.roll(high, HEAD_DIM, 1)
    return low.astype(BF16), high.astype(BF16)


def _attn_specs(tq, A, B, reverse_tiles=None):
    nb = tq // CHUNK
    kcol = (2 * A + B) // LANE
    if reverse_tiles is None:
        tile = lambda i: i
    else:
        tile = lambda i: reverse_tiles - 1 - i
    prev = lambda i: jnp.maximum(tile(i) * nb - 1, 0)
    return [pl.BlockSpec((tq, B), lambda i: (tile(i), 2 * A // B)),
            pl.BlockSpec((tq, LANE), lambda i: (tile(i), kcol)),
            pl.BlockSpec((tq, LANE), lambda i: (tile(i), kcol + 1)),
            pl.BlockSpec((CHUNK, LANE), lambda i: (prev(i), kcol)),
            pl.BlockSpec((CHUNK, LANE), lambda i: (prev(i), kcol + 1))]


def _attn_fwd(proj, bias, sinks, A, B):
    T = proj.shape[0]
    H = B // HEAD_DIM
    qpk = H // KV_HEADS
    tq = _tile(T, 512)
    nb = tq // CHUNK

    scale = HEAD_DIM ** -0.5

    def body(sink_ref, q_ref, k_ref, v_ref, kp_ref, vp_ref, bias_ref, o_ref):
        in_window, first_mask = _attn_masks(pl.program_id(0) == 0)
        for b in range(nb):
            rows = slice(b * CHUNK, (b + 1) * CHUNK)
            if b == 0:
                kprev, vprev, mask = kp_ref[...], vp_ref[...], first_mask
            else:
                prows = slice((b - 1) * CHUNK, b * CHUNK)
                kprev, vprev, mask = k_ref[prows, :], v_ref[prows, :], in_window
            kband = jnp.concatenate([kprev, k_ref[rows, :]], axis=0)
            vband = jnp.concatenate([vprev, v_ref[rows, :]], axis=0)
            k_pads = [_pad_heads(kband, g) for g in range(KV_HEADS)]
            v_both = [jnp.concatenate(_pad_heads(vband, g), axis=0) for g in range(KV_HEADS)]
            scores = []
            for pair in range(H // 2):
                h = 2 * pair
                qs = (q_ref[rows, h * HEAD_DIM:(h + 2) * HEAD_DIM] * scale).astype(BF16)
                scores += [_dot(qs, kz, NT) for kz in k_pads[h // qpk]]
            probs = [_softmax_with_sink(jnp.where(mask, s + bias_ref[h], NEG), sink_ref[h], -1)[0].astype(BF16)
                     for h, s in enumerate(scores)]
            outs = [_dot(jnp.concatenate(probs[h:h + 2], axis=1), v_both[h // qpk], NN) for h in range(0, H, 2)]
            o_ref[rows, :] = jnp.concatenate(outs, axis=1)

    return _CHAIN.call(
        body, name="attn_fwd", grid=(T // tq,),
        in_specs=[pl.BlockSpec(memory_space=pltpu.SMEM)] + _attn_specs(tq, A, B)
        + [pl.BlockSpec((H, CHUNK, 2 * CHUNK), lambda i: (0, 0, 0))],
        out_specs=pl.BlockSpec((tq, B), lambda i: (i, 0)),
        out_shape=SDS((T, B), F32), compiler_params=_params(1))(sinks, proj, proj, proj, proj, proj, bias)


def _outproj_fwd(a, b, ga, gb, x, w, g_ffn):
    T, A = a.shape
    B = b.shape[1]
    D = x.shape[1]
    tm = _tile(T, 512)

    def body(a_ref, b_ref, ga_ref, gb_ref, x_ref, w_ref, gf_ref, h_ref, mix_ref, n_ref):
        av, bv = a_ref[...], b_ref[...]
        mix_ref[:, :A] = (av * _rms_stats(av) * ga_ref[...]).astype(BF16)
        mix_ref[:, A:] = (bv * _rms_stats(bv) * gb_ref[...]).astype(BF16)
        hv = x_ref[...] + _dot(mix_ref[...], w_ref[...], NN)
        h_ref[...] = hv
        n_ref[...] = (hv * _rms_stats(hv) * gf_ref[...]).astype(BF16)

    row = pl.BlockSpec((tm, D), lambda i: (i, 0))
    return _CHAIN.call(
        body, name="outproj_fwd", grid=(T // tm,),
        in_specs=[pl.BlockSpec((tm, A), lambda i: (i, 0)), pl.BlockSpec((tm, B), lambda i: (i, 0)),
                  pl.BlockSpec((1, A), lambda i: (0, 0)), pl.BlockSpec((1, B), lambda i: (0, 0)),
                  row, _resident((A + B, D)), pl.BlockSpec((1, D), lambda i: (0, 0))],
        out_specs=[row, pl.BlockSpec((tm, A + B), lambda i: (i, 0)), row],
        out_shape=[SDS((T, D), F32), SDS((T, A + B), BF16), SDS((T, D), BF16)],
        compiler_params=_params(1))(a, b, ga, gb, x, w, g_ffn)


def _ffn_up(n, w_up):
    T, D = n.shape
    Fb = w_up.shape[2]
    F = N_DEV * Fb
    tm, tf = _tile(T, 1024), _tile(Fb, 1024)
    per = Fb // tf

    def body(n_ref, wu_ref, z_ref):
        z_ref[...] = jnp.maximum(_dot(n_ref[...], wu_ref[...], NN), 0.0).astype(BF16)

    return _CHAIN.call(
        body, name="ffn_up", grid=(T // tm, F // tf),
        in_specs=[pl.BlockSpec((tm, D), lambda i, j: (i, 0)),
                  pl.BlockSpec((None, D, tf), lambda i, j: (j // per, 0, j % per))],
        out_specs=pl.BlockSpec((tm, tf), lambda i, j: (i, j)),
        out_shape=SDS((T, F), BF16), compiler_params=_params(2))(n, w_up)


def _ffn_down(h1, z, w_down):
    T, D = h1.shape
    F = w_down.shape[0]
    tm, tn, tk = _tile(T, 1024), _tile(D, 1024), _tile(F, 4096)

    def body(h_ref, z_ref, wd_ref, h2_ref):
        k = pl.program_id(2)

        @pl.when(k == 0)
        def _():
            h2_ref[...] = h_ref[...]

        zf = z_ref[...].astype(F32)
        h2_ref[...] += _dot((zf * zf).astype(BF16), wd_ref[...], NN)

    return _CHAIN.call(
        body, name="ffn_down", grid=(T // tm, D // tn, F // tk),
        in_specs=[pl.BlockSpec((tm, tn), lambda i, j, k: (i, j)), pl.BlockSpec((tm, tk), lambda i, j, k: (i, k)),
                  pl.BlockSpec((tk, tn), lambda i, j, k: (k, j))],
        out_specs=pl.BlockSpec((tm, tn), lambda i, j, k: (i, j)),
        out_shape=SDS((T, D), F32), compiler_params=_params(3))(h1, z, w_down)


def _final_loss(h2, g, target):
    T, D = h2.shape
    tm = _tile(T, 512)

    def body(h_ref, g_ref, t_ref, loss_ref, dg_ref, dh_ref, dhb_ref):
        @pl.when(pl.program_id(0) == 0)
        def _():
            loss_ref[...] = jnp.zeros_like(loss_ref)
            dg_ref[...] = jnp.zeros_like(dg_ref)

        hv, gv = h_ref[...], g_ref[...]
        r = _rms_stats(hv)
        hn = hv * r
        e = hn * gv - t_ref[...]
        loss_ref[...] += (0.5 / D) * jnp.sum(jnp.sum(e * e, axis=0, keepdims=True), axis=-1, keepdims=True)
        dy = e * (1.0 / D)
        dg_ref[...] += jnp.sum(dy * hn, axis=0, keepdims=True)
        dh = _rms_bwd(dy, hv, r, gv)
        dh_ref[...] = dh
        dhb_ref[...] = dh.astype(BF16)

    return _CHAIN.call(
        body, name="final_loss", grid=(T // tm,),
        in_specs=[pl.BlockSpec((tm, D), lambda i: (i, 0)), pl.BlockSpec((1, D), lambda i: (0, 0)),
                  pl.BlockSpec((tm, D), lambda i: (i, 0))],
        out_specs=[pl.BlockSpec((1, 1), lambda i: (0, 0)), pl.BlockSpec((1, D), lambda i: (0, 0)),
                   pl.BlockSpec((tm, D), lambda i: (i, 0)), pl.BlockSpec((tm, D), lambda i: (i, 0))],
        out_shape=[SDS((1, 1), F32), SDS((1, D), F32), SDS((T, D), F32), SDS((T, D), BF16)],
        compiler_params=_params(1))(h2, g, target)


def _ffn_down_bwd(dh2b, z, w_down):
    T, D = dh2b.shape
    F = w_down.shape[0]
    tm, tf = _tile(T, 1024), _tile(F, 1024)

    def body(dh_ref, z_ref, wd_ref, dzp_ref):
        dzz = _dot(dh_ref[...], wd_ref[...], NT)
        dzp_ref[...] = (dzz * (2.0 * z_ref[...].astype(F32))).astype(BF16)

    return _CHAIN.call(
        body, name="ffn_down_bwd", grid=(T // tm, F // tf),
        in_specs=[pl.BlockSpec((tm, D), lambda i, j: (i, 0)), pl.BlockSpec((tm, tf), lambda i, j: (i, j)),
                  pl.BlockSpec((tf, D), lambda i, j: (j, 0))],
        out_specs=pl.BlockSpec((tm, tf), lambda i, j: (i, j)),
        out_shape=SDS((T, F), BF16), compiler_params=_params(2))(dh2b, z, w_down)


def _ffn_up_bwd(dzp, w_up_t):
    T, F = dzp.shape
    D = w_up_t.shape[1]
    tm, tn, tk = _tile(T, 1024), _tile(D, 1024), _tile(F, 4096)

    def body(dzp_ref, w_ref, dn_ref):
        part = _dot(dzp_ref[...], w_ref[...], NN)

        @pl.when(pl.program_id(2) == 0)
        def _():
            dn_ref[...] = part

        @pl.when(pl.program_id(2) > 0)
        def _():
            dn_ref[...] += part

    return _CHAIN.call(
        body, name="ffn_up_bwd", grid=(T // tm, D // tn, F // tk),
        in_specs=[pl.BlockSpec((tm, tk), lambda i, j, k: (i, k)), pl.BlockSpec((tk, tn), lambda i, j, k: (k, j))],
        out_specs=pl.BlockSpec((tm, tn), lambda i, j, k: (i, j)),
        out_shape=SDS((T, D), F32), compiler_params=_params(3))(dzp, w_up_t)


def _ffn_norm_bwd(dn, dh2, h1, g):
    T, D = h1.shape
    tm = _tile(T, 256)

    def body(dn_ref, dh_ref, h_ref, g_ref, dh1_ref, dh1b_ref, dg_ref):
        @pl.when(pl.program_id(0) == 0)
        def _():
            dg_ref[...] = jnp.zeros_like(dg_ref)

        hv, dnv = h_ref[...], dn_ref[...]
        r = _rms_stats(hv)
        dg_ref[...] += jnp.sum(dnv * (hv * r), axis=0, keepdims=True)
        dh1 = dh_ref[...] + _rms_bwd(dnv, hv, r, g_ref[...])
        dh1_ref[...] = dh1
        dh1b_ref[...] = dh1.astype(BF16)

    row = pl.BlockSpec((tm, D), lambda i: (i, 0))
    vec = pl.BlockSpec((1, D), lambda i: (0, 0))
    return _CHAIN.call(
        body, name="ffn_norm_bwd", grid=(T // tm,), in_specs=[row, row, row, vec], out_specs=[row, row, vec],
        out_shape=[SDS((T, D), F32), SDS((T, D), BF16), SDS((1, D), F32)], compiler_params=_params(1))(dn, dh2, h1, g)


def _matmul_tn(a, b, name, square_a=False, col_blocks=None):
    T, K = a.shape
    N = b.shape[1]
    tk = _tile(K, 1792)
    tn = _tile(N if col_blocks is None else N // col_blocks, 1024 if tk <= 1024 else 512)

    def body(a_ref, b_ref, o_ref):
        av = a_ref[...]
        if square_a:
            af = av.astype(F32)
            av = (af * af).astype(BF16)
        o_ref[...] = _dot(av, b_ref[...], TN).astype(o_ref.dtype)

    if col_blocks is None:
        out_shape = SDS((K, N), BF16)
        out_spec = pl.BlockSpec((tk, tn), lambda i, j: (i, j))
    else:
        per = (N // col_blocks) // tn
        out_shape = SDS((col_blocks, K, N // col_blocks), BF16)
        out_spec = pl.BlockSpec((None, tk, tn), lambda i, j: (j // per, i, j % per))
    return _CHAIN.call(
        body, name=name, grid=(K // tk, N // tn),
        in_specs=[pl.BlockSpec((T, tk), lambda i, j: (0, i)), pl.BlockSpec((T, tn), lambda i, j: (0, j))],
        out_specs=out_spec, out_shape=out_shape, compiler_params=_params(2))(a, b)


def _outproj_bwd(dh1b, w, a, b, ga, gb):
    T, D = dh1b.shape
    A, B = a.shape[1], b.shape[1]
    tm = _tile(T, 512)

    def body(dh_ref, w_ref, a_ref, b_ref, ga_ref, gb_ref, da_ref, db_ref, dga_ref, dgb_ref):
        @pl.when(pl.program_id(0) == 0)
        def _():
            dga_ref[...] = jnp.zeros_like(dga_ref)
            dgb_ref[...] = jnp.zeros_like(dgb_ref)

        dmix = _dot(dh_ref[...], w_ref[...], NT)
        for src_ref, g_ref, dx_ref, dg_ref, dn in ((a_ref, ga_ref, da_ref, dga_ref, dmix[:, :A]),
                                                   (b_ref, gb_ref, db_ref, dgb_ref, dmix[:, A:])):
            xv = src_ref[...]
            r = _rms_stats(xv)
            dg_ref[...] += jnp.sum(dn * (xv * r), axis=0, keepdims=True)
            dx_ref[...] = _rms_bwd(dn, xv, r, g_ref[...])

    return _CHAIN.call(
        body, name="outproj_bwd", grid=(T // tm,),
        in_specs=[pl.BlockSpec((tm, D), lambda i: (i, 0)), _resident((A + B, D)),
                  pl.BlockSpec((tm, A), lambda i: (i, 0)), pl.BlockSpec((tm, B), lambda i: (i, 0)),
                  pl.BlockSpec((1, A), lambda i: (0, 0)), pl.BlockSpec((1, B), lambda i: (0, 0))],
        out_specs=[pl.BlockSpec((tm, A), lambda i: (i, 0)), pl.BlockSpec((tm, B), lambda i: (i, 0)),
                   pl.BlockSpec((1, A), lambda i: (0, 0)), pl.BlockSpec((1, B), lambda i: (0, 0))],
        out_shape=[SDS((T, A), F32), SDS((T, B), F32), SDS((1, A), F32), SDS((1, B), F32)],
        compiler_params=_params(1))(dh1b, w, a, b, ga, gb)


def _gmlp_bwd(proj, da, lg, lb, w_s, w_st, bs_t, A):
    T = proj.shape[0]
    G = A // GROUP_DIM
    tm = _tile(T, 512)
    nc = tm // CHUNK

    def body(u_ref, v_ref, da_ref, lg_ref, lb_ref, w_ref, wt_ref, bst_ref, duv_ref, dlg_ref, dlb_ref, dw_ref, dbs_ref):
        @pl.when(pl.program_id(0) == 0)
        def _():
            dlg_ref[...] = jnp.zeros_like(dlg_ref)
            dlb_ref[...] = jnp.zeros_like(dlb_ref)
            dw_ref[...] = jnp.zeros_like(dw_ref)
            dbs_ref[...] = jnp.zeros_like(dbs_ref)

        row = lax.broadcasted_iota(jnp.int32, (CHUNK, CHUNK), 0)
        col = lax.broadcasted_iota(jnp.int32, (CHUNK, CHUNK), 1)
        lower = row >= col
        upper = row <= col
        for g in range(G):
            sl = slice(g * GROUP_DIM, (g + 1) * GROUP_DIM)
            lgv = lg_ref[:, sl]
            vg, vg_grad = _gelu_and_grad(v_ref[:, sl])
            vhat, rstd, vn = _layer_norm_group(vg, lgv, lb_ref[:, sl])
            vnb = vn.astype(BF16)
            ug, ug_grad = _gelu_and_grad(u_ref[:, sl])
            dav = da_ref[:, sl]
            wm = jnp.where(lower, w_ref[g], 0.0).astype(BF16)
            wmt = jnp.where(upper, wt_ref[g], 0.0).astype(BF16)
            bcol = bst_ref[:, g:g + 1]
            dw_acc = jnp.zeros((CHUNK, CHUNK), F32)
            dbs_acc = jnp.zeros((CHUNK, 1), F32)
            dvn_parts = []
            dug_parts = []
            for c in range(nc):
                rs = slice(c * CHUNK, (c + 1) * CHUNK)
                mixed = _dot(wm, vnb[rs], NN) + bcol
                dug_parts.append(dav[rs] * mixed)
                dmix = dav[rs] * ug[rs]
                dbs_acc = dbs_acc + jnp.sum(dmix, axis=-1, keepdims=True)
                dmixb = dmix.astype(BF16)
                dw_acc = dw_acc + _dot(dmixb, vnb[rs], NT)
                dvn_parts.append(_dot(wmt, dmixb, NN))
            dvn = jnp.concatenate(dvn_parts, axis=0)
            dug = jnp.concatenate(dug_parts, axis=0)
            dw_ref[g] += jnp.where(lower, dw_acc, 0.0)
            dbs_ref[:, g:g + 1] += dbs_acc
            dlg_ref[:, sl] += jnp.sum(dvn * vhat, axis=0, keepdims=True)
            dlb_ref[:, sl] += jnp.sum(dvn, axis=0, keepdims=True)
            dvhat = dvn * lgv
            dvg = rstd * (dvhat - jnp.mean(dvhat, axis=-1, keepdims=True)
                          - vhat * jnp.mean(dvhat * vhat, axis=-1, keepdims=True))
            duv_ref[:, sl] = (dug * ug_grad).astype(BF16)
            duv_ref[:, A + g * GROUP_DIM:A + (g + 1) * GROUP_DIM] = (dvg * vg_grad).astype(BF16)

    return _CHAIN.call(
        body, name="gmlp_bwd", grid=(T // tm,),
        in_specs=[pl.BlockSpec((tm, A), lambda i: (i, 0)), pl.BlockSpec((tm, A), lambda i: (i, 1)),
                  pl.BlockSpec((tm, A), lambda i: (i, 0)),
                  pl.BlockSpec((1, A), lambda i: (0, 0)), pl.BlockSpec((1, A), lambda i: (0, 0)),
                  pl.BlockSpec((G, CHUNK, CHUNK), lambda i: (0, 0, 0)),
                  pl.BlockSpec((G, CHUNK, CHUNK), lambda i: (0, 0, 0)), pl.BlockSpec((CHUNK, G), lambda i: (0, 0))],
        out_specs=[pl.BlockSpec((tm, 2 * A), lambda i: (i, 0)),
                   pl.BlockSpec((1, A), lambda i: (0, 0)), pl.BlockSpec((1, A), lambda i: (0, 0)),
                   pl.BlockSpec((G, CHUNK, CHUNK), lambda i: (0, 0, 0)), pl.BlockSpec((CHUNK, G), lambda i: (0, 0))],
        out_shape=[SDS((T, 2 * A), BF16), SDS((1, A), F32), SDS((1, A), F32),
                   SDS((G, CHUNK, CHUNK), F32), SDS((CHUNK, G), F32)],
        compiler_params=_params(1))(proj, proj, da, lg, lb, w_s, w_st, bs_t)


def _attn_bwd(proj, do, duv, bias_t, sinks, A, B):
    T, P = proj.shape
    H = B // HEAD_DIM
    qpk = H // KV_HEADS
    tq = _tile(T, 512)
    nb = tq // CHUNK
    n_tiles = T // tq
    scale = HEAD_DIM ** -0.5
    rev = lambda i: n_tiles - 1 - i

    def body(sink_ref, q_ref, k_ref, v_ref, kp_ref, vp_ref, do_ref, duv_ref, bias_ref,
             dproj_ref, dbias_ref, dsink_ref, carry, dkv, sacc):
        step = pl.program_id(0)

        @pl.when(step == 0)
        def _():
            carry[...] = jnp.zeros_like(carry)
            sacc[...] = jnp.zeros_like(sacc)
            dbias_ref[...] = jnp.zeros_like(dbias_ref)

        jj = lax.broadcasted_iota(jnp.int32, (2 * CHUNK, CHUNK), 0)
        ii = lax.broadcasted_iota(jnp.int32, (2 * CHUNK, CHUNK), 1)
        in_window = (jj > ii) & (jj <= ii + CHUNK)
        first_mask = in_window & jnp.logical_or(step != n_tiles - 1, jj >= CHUNK)
        low_query = lax.broadcasted_iota(jnp.int32, (CHUNK, LANE), 1) < HEAD_DIM
        low_key = lax.broadcasted_iota(jnp.int32, (2 * CHUNK, LANE), 1) < HEAD_DIM

        def split_pair(pair_bf16):
            zero = jnp.zeros_like(pair_bf16)
            return jnp.concatenate([jnp.where(low_query, pair_bf16, zero), jnp.where(low_query, zero, pair_bf16)], axis=0)

        dproj_ref[:, :2 * A] = duv_ref[...]
        dkv[...] = jnp.zeros_like(dkv)
        for b in range(nb):
            rows = slice(b * CHUNK, (b + 1) * CHUNK)
            band = slice(b * CHUNK, (b + 2) * CHUNK)
            if b == 0:
                kprev, vprev, mask = kp_ref[...], vp_ref[...], first_mask
            else:
                prows = slice((b - 1) * CHUNK, b * CHUNK)
                kprev, vprev, mask = k_ref[prows, :], v_ref[prows, :], in_window
            kband = jnp.concatenate([kprev, k_ref[rows, :]], axis=0)
            vband = jnp.concatenate([vprev, v_ref[rows, :]], axis=0)
            k_pads = [_pad_heads(kband, g) for g in range(KV_HEADS)]
            v_pads = [_pad_heads(vband, g) for g in range(KV_HEADS)]
            queries, douts, scores, dprobs = [], [], [], []
            for pair in range(H // 2):
                cols = slice(2 * pair * HEAD_DIM, (2 * pair + 2) * HEAD_DIM)
                qs = (q_ref[rows, cols] * scale).astype(BF16)
                dob = do_ref[rows, cols].astype(BF16)
                queries.append(qs)
                douts.append(dob)
                scores += [_dot(kz, qs, NT) for kz in k_pads[2 * pair // qpk]]
                dprobs += [_dot(vz, dob, NT) for vz in v_pads[2 * pair // qpk]]
            probs, dscores = [], []
            for h in range(H):
                pt, p_sink = _softmax_with_sink(jnp.where(mask, scores[h] + bias_ref[h], NEG), sink_ref[h], 0)
                delta = jnp.sum(pt * dprobs[h], axis=0, keepdims=True)
                dst = pt * (dprobs[h] - delta)
                dbias_ref[h] += dst
                sacc[h:h + 1, :] += -(p_sink * delta)
                probs.append(pt.astype(BF16))
                dscores.append(dst.astype(BF16))
            dq_parts, dk_groups, dv_groups = [], [], []
            for g in range(KV_HEADS):
                k_both = jnp.concatenate(k_pads[g], axis=0)
                dk_acc = jnp.zeros((2 * CHUNK, LANE), F32)
                dv_acc = jnp.zeros((2 * CHUNK, LANE), F32)
                for pair in range(g * qpk // 2, (g + 1) * qpk // 2):
                    pair_heads = slice(2 * pair, 2 * pair + 2)
                    dk_acc = dk_acc + _dot(jnp.concatenate(dscores[pair_heads], axis=1), split_pair(queries[pair]), NN)
                    dv_acc = dv_acc + _dot(jnp.concatenate(probs[pair_heads], axis=1), split_pair(douts[pair]), NN)
                    dq_parts.append(_dot(jnp.concatenate(dscores[pair_heads], axis=0), k_both, TN) * scale)
                dk_groups.append(dk_acc + pltpu.roll(dk_acc, HEAD_DIM, 1))
                dv_groups.append(dv_acc + pltpu.roll(dv_acc, HEAD_DIM, 1))
            dkv[band, :LANE] += jnp.where(low_key, dk_groups[0], dk_groups[1])
            dkv[band, LANE:] += jnp.where(low_key, dv_groups[0], dv_groups[1])
            dproj_ref[rows, 2 * A:2 * A + B] = jnp.concatenate(dq_parts, axis=1).astype(BF16)
        last = slice(tq, tq + CHUNK)
        dkv[last, :] += carry[...]
        dproj_ref[:, 2 * A + B:] = dkv[CHUNK:, :].astype(BF16)
        carry[...] = dkv[:CHUNK, :]

        @pl.when(step == n_tiles - 1)
        def _():
            dsink_ref[...] = jnp.sum(sacc[...], axis=1, keepdims=True)

    specs = _attn_specs(tq, A, B, reverse_tiles=n_tiles)
    return _CHAIN.call(
        body, name="attn_bwd", grid=(n_tiles,),
        in_specs=[pl.BlockSpec(memory_space=pltpu.SMEM)] + specs
        + [pl.BlockSpec((tq, B), lambda i: (rev(i), 0)), pl.BlockSpec((tq, 2 * A), lambda i: (rev(i), 0)),
           pl.BlockSpec((H, 2 * CHUNK, CHUNK), lambda i: (0, 0, 0))],
        out_specs=[pl.BlockSpec((tq, P), lambda i: (rev(i), 0)),
                   pl.BlockSpec((H, 2 * CHUNK, CHUNK), lambda i: (0, 0, 0)), pl.BlockSpec((H, 1), lambda i: (0, 0))],
        out_shape=[SDS((T, P), BF16), SDS((H, 2 * CHUNK, CHUNK), F32), SDS((H, 1), F32)],
        scratch_shapes=[pltpu.VMEM((CHUNK, 2 * LANE), F32), pltpu.VMEM((tq + CHUNK, 2 * LANE), F32),
                        pltpu.VMEM((H, LANE), F32)],
        compiler_params=_params(1))(sinks, proj, proj, proj, proj, proj, do, duv, bias_t)


def _bias_bwd(dbias, onehot):
    H = dbias.shape[0]
    nbk = onehot.shape[1]

    def body(d_ref, oh_ref, o_ref):
        hi, mid, lo = _split3(d_ref[...])
        oh = oh_ref[...]
        o_ref[...] = _dot(hi, oh, NN) + _dot(mid, oh, NN) + _dot(lo, oh, NN)

    return _CHAIN.call(body, name="bias_bwd", in_specs=[VMEM_SPEC] * 2, out_specs=VMEM_SPEC, out_shape=SDS((H, nbk), F32),
                       compiler_params=_params(0))(dbias, onehot)


def _inproj_bwd(dproj, w_t, x, dh1, g):
    T, P = dproj.shape
    D = x.shape[1]
    tm = _tile(T, 512)

    def body(dp_ref, w_ref, x_ref, dh_ref, g_ref, dx_ref, dg_ref):
        @pl.when(pl.program_id(0) == 0)
        def _():
            dg_ref[...] = jnp.zeros_like(dg_ref)

        dn = _dot(dp_ref[...], w_ref[...], NN)
        xv = x_ref[...]
        r = _rms_stats(xv)
        dg_ref[...] += jnp.sum(dn * (xv * r), axis=0, keepdims=True)
        dx_ref[...] = dh_ref[...] + _rms_bwd(dn, xv, r, g_ref[...])

    return _CHAIN.call(
        body, name="inproj_bwd", grid=(T // tm,),
        in_specs=[pl.BlockSpec((tm, P), lambda i: (i, 0)), _resident((P, D)),
                  pl.BlockSpec((tm, D), lambda i: (i, 0)), pl.BlockSpec((tm, D), lambda i: (i, 0)),
                  pl.BlockSpec((1, D), lambda i: (0, 0))],
        out_specs=[pl.BlockSpec((tm, D), lambda i: (i, 0)), pl.BlockSpec((1, D), lambda i: (0, 0))],
        out_shape=[SDS((T, D), F32), SDS((1, D), F32)], compiler_params=_params(1))(dproj, w_t, x, dh1, g)


def _adamw(w, g, m, v):
    m = ADAM_B1 * m + (1.0 - ADAM_B1) * g
    v = ADAM_B2 * v + (1.0 - ADAM_B2) * (g * g)
    m_hat = m / (1.0 - ADAM_B1 ** ADAM_STEP)
    v_hat = v / (1.0 - ADAM_B2 ** ADAM_STEP)
    delta = -ADAM_LR * (m_hat / (jnp.sqrt(v_hat) + ADAM_EPS) + ADAM_WD * w)
    return delta, m, v


def _adam_sharded(csum, recv, w, m, v, name):
    R, C = w.shape
    tr = _tile(R, 256, 16)

    def body(own_ref, recv_ref, w_ref, m_ref, v_ref, g_ref, d_ref, nm_ref, nv_ref):
        g = own_ref[...].astype(F32)
        for r in range(3):
            g = g + recv_ref[r].astype(F32)
        delta, nm, nv = _adamw(w_ref[...], g, m_ref[...], v_ref[...])
        g_ref[...] = g
        d_ref[...] = delta
        nm_ref[...] = nm
        nv_ref[...] = nv

    blk = pl.BlockSpec((tr, C), lambda i: (i, 0))
    return _CHAIN.call(
        body, name=name, grid=(R // tr,),
        in_specs=[pl.BlockSpec((None, tr, C), lambda i: (0, i, 0)), pl.BlockSpec((3, tr, C), lambda i: (0, i, 0)),
                  blk, blk, blk],
        out_specs=[blk] * 4, out_shape=[SDS((R, C), F32)] * 4, compiler_params=_params(1))(csum, recv, w, m, v)


def _adam_small(gathered, w, m, v):
    R = w.shape[0]

    def body(p_ref, w_ref, m_ref, v_ref, g_ref, d_ref, nm_ref, nv_ref):
        g = p_ref[0]
        for d in range(1, N_DEV):
            g = g + p_ref[d]
        delta, nm, nv = _adamw(w_ref[...], g, m_ref[...], v_ref[...])
        g_ref[...] = g
        d_ref[...] = delta
        nm_ref[...] = nm
        nv_ref[...] = nv

    return _CHAIN.call(body, name="adam_small", in_specs=[VMEM_SPEC] * 4, out_specs=[VMEM_SPEC] * 4,
                       out_shape=[SDS((R, LANE), F32)] * 4,
                       compiler_params=_params(0))(gathered, w, m, v)


def _pack(arrays):
    tile = 8 * LANE
    pieces = []
    for a in arrays:
        flat = a.reshape(-1).astype(F32)
        pieces.append(jnp.pad(flat, (0, (-flat.size) % tile)))
    return jnp.concatenate(pieces).reshape(-1, LANE)


def _unpack(packed, shapes):
    tile = 8 * LANE
    flat = packed.reshape(-1)
    out, off = [], 0
    for s in shapes:
        size = int(np.prod(s))
        out.append(flat[off:off + size].reshape(s))
        off += size + (-size) % tile
    return out


def kernel(x, rel_bias_table, mix_norm_g, w_in, gate_norm_g, gate_norm_b, w_spatial, b_spatial, attn_sinks, out_norm_a_g, out_norm_b_g, w_out, ffn_norm_g, w_up, w_down, final_norm_g, loss_target, m_rel_bias_table, m_mix_norm_g, m_w_in, m_gate_norm_g, m_gate_norm_b, m_w_spatial, m_b_spatial, m_attn_sinks, m_out_norm_a_g, m_out_norm_b_g, m_w_out, m_ffn_norm_g, m_w_up, m_w_down, m_final_norm_g, v_rel_bias_table, v_mix_norm_g, v_w_in, v_gate_norm_g, v_gate_norm_b, v_w_spatial, v_b_spatial, v_attn_sinks, v_out_norm_a_g, v_out_norm_b_g, v_w_out, v_ffn_norm_g, v_w_up, v_w_down, v_final_norm_g):
    T, D = x.shape[1], x.shape[2]
    A = D // 2
    B = D // 2
    G = A // GROUP_DIM
    H = B // HEAD_DIM
    P = 2 * A + B + 2 * KV_HEADS * HEAD_DIM
    xs = x.reshape(T, D)
    target = loss_target.reshape(T, D)

    win_t, m_win_t, v_win_t = (jnp.swapaxes(a[0], 0, 1) for a in (w_in, m_w_in, v_w_in))
    shards = [win_t.astype(BF16), w_out[0].astype(BF16), w_up[0].astype(BF16), w_down[0].astype(BF16)]
    _CHAIN.token = None
    gather = _gather_begin(shards)
    _gather_step(gather, [(0, 0)], "gather_start")

    g1, g2, g3 = mix_norm_g.reshape(1, D), ffn_norm_g.reshape(1, D), final_norm_g.reshape(1, D)
    lg, lb = gate_norm_g.reshape(1, A), gate_norm_b.reshape(1, A)
    ws = w_spatial[0]
    ws_t = jnp.swapaxes(ws, 1, 2)
    bs_t = jnp.transpose(b_spatial[0])
    ga, gb = out_norm_a_g.reshape(1, A), out_norm_b_g.reshape(1, B)
    sinks = attn_sinks.reshape(H)
    bucket, in_window = _t5_bucket()
    onehot_np = ((bucket[:, :, None] == np.arange(N_BUCKETS)) & in_window[:, :, None]).astype(np.float32)
    onehot = jnp.asarray(onehot_np.reshape(-1, N_BUCKETS)).astype(BF16)
    onehot_kq = jnp.asarray(onehot_np.transpose(1, 0, 2).reshape(-1, N_BUCKETS)).astype(BF16)

    bias, bias_t = _bias_fwd(jnp.transpose(rel_bias_table), jnp.transpose(onehot), jnp.transpose(onehot_kq))
    bias, bias_t = bias.reshape(H, CHUNK, 2 * CHUNK), bias_t.reshape(H, 2 * CHUNK, CHUNK)
    n1 = _mix_norm(xs, g1)
    _gather_step(gather, [(0, 1), (1, 0), (2, 0)], "gather_in_1")
    _gather_step(gather, [(0, 2)], "gather_in_2")
    (win_g,) = _gather_end(gather, [0], "gather_in_end")
    win_t_full = win_g.reshape(P, D)
    proj = _inproj_fwd(n1, win_t_full)
    _gather_step(gather, [(1, 1)], "gather_out_1")
    a_out = _gmlp_fwd(proj, lg, lb, ws, bs_t, A)
    _gather_step(gather, [(1, 2), (2, 1), (3, 0)], "gather_out_2_up_1")
    b_out = _attn_fwd(proj, bias, sinks, A, B)
    (wout_g,) = _gather_end(gather, [1], "gather_out_end")
    _gather_step(gather, [(2, 2)], "gather_up_2")
    wout_full = wout_g.reshape(A + B, D)
    h1, mixed, n2 = _outproj_fwd(a_out, b_out, ga, gb, xs, wout_full, g2)
    (wup_g,) = _gather_end(gather, [2], "gather_up_end")
    _gather_step(gather, [(3, 1)], "gather_down_1")
    wup_t = jnp.transpose(wup_g, (0, 2, 1)).reshape(-1, D)
    z = _ffn_up(n2, wup_g)
    _gather_step(gather, [(3, 2)], "gather_down_2")
    (wdown_g,) = _gather_end(gather, [3], "gather_down_end")
    h2 = _ffn_down(h1, z, wdown_g.reshape(-1, D))
    loss_part, dg3, dh2, dh2b = _final_loss(h2, g3, target)

    def reduce_to_chip(state, name):
        part, received = _sibling_exchange_end(state, name + "_sib_end")
        return _chip_exchange_begin(_chip_sum(part, received, name + "_chip_sum"), name + "_chip")

    dwdown = _matmul_tn(z, dh2b, "grad_w_down", square_a=True).reshape(wdown_g.shape)
    sib_down = _sibling_exchange_begin(dwdown, "rs_down_sib")
    dzp = _ffn_down_bwd(dh2b, z, wdown_g.reshape(-1, D))
    chip_down = reduce_to_chip(sib_down, "rs_down")
    dwup = _matmul_tn(n2, dzp, "grad_w_up", col_blocks=N_DEV)
    sib_up = _sibling_exchange_begin(dwup, "rs_up_sib")
    dh1, dh1b, dg2 = _ffn_norm_bwd(_ffn_up_bwd(dzp, wup_t), dh2, h1, g2)
    chip_up = reduce_to_chip(sib_up, "rs_up")
    da, db, dga, dgb = _outproj_bwd(dh1b, wout_full, a_out, b_out, ga, gb)
    dwout = _matmul_tn(mixed, dh1b, "grad_w_out").reshape(wout_g.shape)
    sib_out = _sibling_exchange_begin(dwout, "rs_out_sib")
    duv, dlg, dlb, dws, dbs_t = _gmlp_bwd(proj, da, lg, lb, ws, ws_t, bs_t, A)
    dproj, dbias_t, dsinks = _attn_bwd(proj, db, duv, bias_t, sinks, A, B)
    chip_out = reduce_to_chip(sib_out, "rs_out")
    dtable_t = _bias_bwd(dbias_t.reshape(H, -1), onehot_kq)
    dwin_t = _matmul_tn(dproj, n1, "grad_w_in").reshape(win_g.shape)
    sib_in = _sibling_exchange_begin(dwin_t, "rs_in_sib")
    grad_x, dg1 = _inproj_bwd(dproj, win_t_full, xs, dh1, g1)

    small_w = [rel_bias_table, mix_norm_g, gate_norm_g, gate_norm_b, w_spatial, b_spatial, attn_sinks,
               out_norm_a_g, out_norm_b_g, ffn_norm_g, final_norm_g]
    small_m = [m_rel_bias_table, m_mix_norm_g, m_gate_norm_g, m_gate_norm_b, m_w_spatial, m_b_spatial, m_attn_sinks,
               m_out_norm_a_g, m_out_norm_b_g, m_ffn_norm_g, m_final_norm_g]
    small_v = [v_rel_bias_table, v_mix_norm_g, v_gate_norm_g, v_gate_norm_b, v_w_spatial, v_b_spatial, v_attn_sinks,
               v_out_norm_a_g, v_out_norm_b_g, v_ffn_norm_g, v_final_norm_g]
    small_g = [jnp.transpose(dtable_t), dg1, dlg, dlb, dws, jnp.transpose(dbs_t), dsinks, dga, dgb, dg2, dg3]
    nothing = jnp.zeros((1, 1), F32)
    small_w, small_m, small_v, small_g = small_w + [nothing], small_m + [nothing], small_v + [nothing], small_g + [loss_part]
    shapes = [w.shape for w in small_w]
    big = [None] * 4

    def adam_of(k, state, w, m, v):
        csum, received = _chip_exchange_end(state, "rs_%d_end" % k)
        big[k] = _adam_sharded(csum, received, w, m, v, "adam_%d" % k)

    small_gather = _gather_begin([_pack(small_g)])
    _gather_step(small_gather, [(0, 0)], "small_gather_start")
    part, received = _sibling_exchange_end(sib_in, "rs_in_sib_end")
    csum_in = _chip_sum(part, received, "rs_in_chip_sum")
    _gather_step(small_gather, [(0, 1)], "small_gather_1")
    chip_in = _chip_exchange_begin(csum_in, "rs_in_chip")
    adam_of(3, chip_down, w_down[0], m_w_down[0], v_w_down[0])
    _gather_step(small_gather, [(0, 2)], "small_gather_2")
    adam_of(2, chip_up, w_up[0], m_w_up[0], v_w_up[0])
    (gathered,) = _gather_end(small_gather, [0], "small_gather_end")
    sg, sd, sm, sv = [_unpack(o, shapes) for o in _adam_small(gathered, _pack(small_w), _pack(small_m), _pack(small_v))]
    adam_of(1, chip_out, w_out[0], m_w_out[0], v_w_out[0])
    adam_of(0, chip_in, win_t, m_win_t, v_win_t)
    big[0] = [jnp.swapaxes(o, 0, 1) for o in big[0]]
    big = [[o.reshape(w.shape) for o in outs] for outs, w in zip(big, (w_in, w_out, w_up, w_down))]

    loss = sg[-1].reshape(())

    order = ["s0", "s1", "b0", "s2", "s3", "s4", "s5", "s6", "s7", "s8", "b1", "s9", "b2", "b3", "s10"]

    def group(idx):
        small = (sg, sd, sm, sv)[idx]
        return [small[int(t[1:])] if t[0] == "s" else big[int(t[1:])][idx] for t in order]

    return (loss, grad_x.reshape(x.shape), *group(0), *group(1), *group(2), *group(3))
```

```python
import functools
import math

import numpy as np
import jax
import jax.numpy as jnp
from jax import lax
from jax.experimental import pallas as pl
from jax.experimental.pallas import tpu as pltpu

F32 = jnp.float32
BF16 = jnp.bfloat16
SDS = jax.ShapeDtypeStruct
MESH = pl.DeviceIdType.MESH

N_DEV = 8
EPS = 1e-5
NEG = -1e30
CHUNK = 128
GROUP_DIM = 128
HEAD_DIM = 64
KV_HEADS = 2
N_BUCKETS = 32
MAX_DISTANCE = 128
ADAM_LR, ADAM_B1, ADAM_B2, ADAM_EPS, ADAM_WD, ADAM_STEP = 0.001, 0.9, 0.999, 1e-08, 0.01, 10
GELU_C0 = math.sqrt(2.0 / math.pi)
GELU_C1 = 0.044715

V7X_VMEM_BYTES = 64 * 1024 * 1024
VMEM_LIMIT = V7X_VMEM_BYTES - 8 * 1024 * 1024
LANE = 128

NN = ((1,), (0,))
NT = ((1,), (1,))
TN = ((0,), (0,))


def _dot(a, b, dims):
    return lax.dot_general(a, b, (dims, ((), ())), preferred_element_type=F32)


def _tile(n, pref, unit=LANE):
    best = None
    for t in range(unit, min(n, pref) + 1, unit):
        if n % t == 0:
            best = t
    return n if best is None else best


def _params(n_grid):
    return pltpu.CompilerParams(dimension_semantics=("arbitrary",) * n_grid, vmem_limit_bytes=VMEM_LIMIT)


def _resident(shape):
    return pl.BlockSpec(shape, lambda i: (0, 0), pipeline_mode=pl.Buffered(1))


def _gelu(x):
    return 0.5 * x * (1.0 + jnp.tanh(GELU_C0 * (x + GELU_C1 * x * x * x)))


def _gelu_and_grad(x):
    x2 = x * x
    t = jnp.tanh(GELU_C0 * x * (1.0 + GELU_C1 * x2))
    val = 0.5 * x * (1.0 + t)
    grad = 0.5 * (1.0 + t) + 0.5 * x * (1.0 - t * t) * (GELU_C0 * (1.0 + 3.0 * GELU_C1 * x2))
    return val, grad


def _rms_stats(x):
    return lax.rsqrt(jnp.mean(x * x, axis=-1, keepdims=True) + EPS)


def _rms_bwd(dy, x, r, g):
    w = dy * g
    return r * w - x * (r * r * r) * jnp.mean(w * x, axis=-1, keepdims=True)


def _t5_bucket():
    i = np.arange(CHUNK)[:, None]
    j = np.arange(2 * CHUNK)[None, :]
    rel = np.maximum(i + CHUNK - j, 0)
    n_exact = N_BUCKETS // 2
    relf = np.maximum(rel, n_exact).astype(np.float32)
    large = n_exact + (np.log(relf / np.float32(n_exact)) / np.float32(math.log(MAX_DISTANCE / n_exact))
                       * np.float32(N_BUCKETS - n_exact)).astype(np.int32)
    large = np.minimum(large, N_BUCKETS - 1)
    bucket = np.where(rel < n_exact, rel, large)
    in_window = (i + CHUNK - j >= 0) & (i + CHUNK - j < CHUNK)
    return bucket.astype(np.int32), in_window


def _split3(x):
    hi = x.astype(BF16)
    r1 = x - hi.astype(F32)
    mid = r1.astype(BF16)
    lo = (r1 - mid.astype(F32)).astype(BF16)
    return hi, mid, lo


HBM_SPEC = pl.BlockSpec(memory_space=pltpu.HBM)


def _mesh_pos():
    return lax.axis_index("x"), lax.axis_index("y"), lax.axis_index("c")


def _dev_index(px, py, pc):
    return 4 * px + 2 * py + pc


SEM_SPEC = pl.BlockSpec(memory_space=pltpu.SEMAPHORE)
ANY_SPEC = pl.BlockSpec(memory_space=pl.ANY)
VMEM_SPEC = pl.BlockSpec(memory_space=pltpu.VMEM)
TOKEN_SPEC = VMEM_SPEC
TOKEN = SDS((8, LANE), F32)
SIDE_EFFECT = pltpu.SideEffectType.DATAFLOW_SIDE_EFFECTING


def _hbm(x):
    return pltpu.with_memory_space_constraint(x, pltpu.HBM)


class _CallChain:
    def __init__(self):
        self.token = None

    def call(self, body, *, in_specs, out_specs, out_shape, **kwargs):
        dep, n_in = self.token, len(in_specs)
        single = not isinstance(out_shape, (list, tuple))
        out_shapes = [out_shape] if single else list(out_shape)
        out_specs = [out_specs] if single else list(out_specs)
        n_out = len(out_shapes)
        n_dep = 0 if dep is None else 1
        token_spec = pl.BlockSpec((8, LANE), lambda *_: (0, 0)) if kwargs.get("grid") else VMEM_SPEC

        def chained(*refs):
            outs_at = n_in + n_dep
            body(*refs[:n_in], *refs[outs_at:outs_at + n_out], *refs[outs_at + n_out + 1:])
            token = refs[outs_at + n_out]
            token[...] = jnp.zeros_like(token)

        inner = pl.pallas_call(chained, in_specs=list(in_specs) + [ANY_SPEC] * n_dep, out_specs=out_specs + [token_spec],
                               out_shape=out_shapes + [TOKEN], **kwargs)

        def run(*operands):
            outs = inner(*operands) if dep is None else inner(*operands, dep)
            self.token = outs[n_out]
            return outs[0] if single else list(outs[:n_out])

        return run


_CHAIN = _CallChain()


def _wait_all(waits, x, y, c):
    for kind, src, dst, send_sem, recv_sem in waits:
        cp = pltpu.make_async_remote_copy(src_ref=src, dst_ref=dst, send_sem=send_sem, recv_sem=recv_sem,
                                          device_id=(x, y, c), device_id_type=MESH)
        if kind == "send":
            cp.wait_send()
        else:
            cp.wait_recv()


def _split_start(bufs, copies_of, n_sems, name, sem_sets=(), waits_of=None):
    n, ns = len(bufs), len(sem_sets)
    flat_sems = [s for pair in sem_sets for s in pair]

    def body(*refs):
        ins = refs[:n]
        sems = refs[n:n + 2 * ns]
        send_sems, recv_sems = refs[n + 2 * ns], refs[n + 2 * ns + 1]
        if waits_of is not None:
            _wait_all(waits_of(ins, [(sems[2 * i], sems[2 * i + 1]) for i in range(ns)]), *_mesh_pos())
        for src, dst, k, target in copies_of(ins):
            pltpu.make_async_remote_copy(src_ref=src, dst_ref=dst, send_sem=send_sems.at[k], recv_sem=recv_sems.at[k],
                                         device_id=target, device_id_type=MESH).start()

    outs = _CHAIN.call(
        body, name=name,
        out_shape=[pltpu.SemaphoreType.DMA((n_sems,)), pltpu.SemaphoreType.DMA((n_sems,))]
        + [pltpu.HBM(b.shape, b.dtype) for b in bufs],
        in_specs=[HBM_SPEC] * n + [SEM_SPEC] * (2 * ns), out_specs=[SEM_SPEC, SEM_SPEC] + [HBM_SPEC] * n,
        input_output_aliases={a: 2 + a for a in range(n)},
        compiler_params=pltpu.CompilerParams(has_side_effects=SIDE_EFFECT),
    )(*[_hbm(b) for b in bufs], *flat_sems)
    return outs[0], outs[1], list(outs[2:2 + n])


def _split_wait(bufs, sem_sets, waits_of, name):
    n, ns = len(bufs), len(sem_sets)
    flat_sems = [s for pair in sem_sets for s in pair]

    def body(*refs):
        ins = refs[:n]
        sems = refs[n:n + 2 * ns]
        _wait_all(waits_of(ins, [(sems[2 * i], sems[2 * i + 1]) for i in range(ns)]), *_mesh_pos())

    outs = _CHAIN.call(
        body, name=name,
        out_shape=[pltpu.HBM(b.shape, b.dtype) for b in bufs],
        in_specs=[HBM_SPEC] * n + [SEM_SPEC] * (2 * ns), out_specs=[HBM_SPEC] * n,
        input_output_aliases={a: a for a in range(n)},
        compiler_params=pltpu.CompilerParams(has_side_effects=SIDE_EFFECT),
    )(*bufs, *flat_sems)
    return list(outs)


def _gather_blocks(land):
    rows = land.shape[1]
    first = (rows // 2) // 16 * 16

    def block(px, py, pc):
        return land.at[_dev_index(px, py, pc)]

    def halves(px, py, pc):
        return (land.at[_dev_index(px, py, pc), pl.ds(0, first)], land.at[_dev_index(px, py, pc), pl.ds(first, rows - first)])

    return block, halves


def _gather_begin(shards):
    me = _dev_index(*_mesh_pos())
    lands = [lax.dynamic_update_index_in_dim(lax.empty((N_DEV,) + s.shape, s.dtype), s, me, 0) for s in shards]
    return dict(lands=lands, stage={})


STAGE_COPIES = (3, 4, 1)


def _gather_step(state, items, name):
    which = sorted({a for a, _ in items})
    at = {a: i for i, a in enumerate(which)}
    sem_sets = [state["stage"][(a, s - 1)][0] for a, s in items if s > 0]
    offset, n_sems = {}, 0
    for a, s in items:
        offset[(a, s)] = n_sems
        n_sems += STAGE_COPIES[s]

    def waits_of(ins, sems):
        x, y, c = _mesh_pos()
        out, earlier = [], 0
        for a, s in items:
            if s == 0:
                continue
            block, halves = _gather_blocks(ins[at[a]])
            send, recv = sems[earlier]
            off = state["stage"][(a, s - 1)][1]
            earlier += 1
            if s == 1:
                arrived = [(1, block(1 - x, y, c)), (2, block(x, 1 - y, c))]
            else:
                arrived = list(zip((2, 3), halves(1 - x, 1 - y, c)))
            out += [("recv", ref, ref, send.at[off + k], recv.at[off + k]) for k, ref in arrived]
        return out

    def copies_of(ins):
        x, y, c = _mesh_pos()
        sibling = (x, y, 1 - c)
        out = []
        for a, s in items:
            block, halves = _gather_blocks(ins[at[a]])
            off = offset[(a, s)]
            if s == 0:
                mine = block(x, y, c)
                out += [(mine, mine, off + 1, (1 - x, y, c)), (mine, mine, off + 2, (x, 1 - y, c)), (mine, mine, off, sibling)]
            elif s == 1:
                from_x, from_y = block(1 - x, y, c), block(x, 1 - y, c)
                out += [(halves(1 - x, y, c)[0], halves(1 - x, y, c)[0], off + 2, (x, 1 - y, c)),
                        (halves(x, 1 - y, c)[1], halves(x, 1 - y, c)[1], off + 3, (1 - x, y, c)),
                        (from_x, from_x, off, sibling), (from_y, from_y, off + 1, sibling)]
            else:
                diag = block(1 - x, 1 - y, c)
                out.append((diag, diag, off, sibling))
        return out

    send_sems, recv_sems, bufs = _split_start([state["lands"][a] for a in which], copies_of, n_sems, name,
                                              sem_sets=sem_sets, waits_of=waits_of)
    for a in which:
        state["lands"][a] = bufs[at[a]]
    for a, s in items:
        state["stage"][(a, s)] = ((send_sems, recv_sems), offset[(a, s)])


def _gather_end(state, which, name):
    sem_sets = [state["stage"][(a, s)][0] for a in which for s in range(3)]

    def waits(ins, sems):
        x, y, c = _mesh_pos()
        out = []
        for i, a in enumerate(which):
            block, halves = _gather_blocks(ins[i])
            (b_send, b_recv), (s1_send, s1_recv), (s2_send, s2_recv) = sems[3 * i:3 * i + 3]
            o0, o1, o2 = (state["stage"][(a, s)][1] for s in range(3))
            arrivals = [(block(x, y, 1 - c), b_send, b_recv, o0),
                        (block(1 - x, y, 1 - c), s1_send, s1_recv, o1), (block(x, 1 - y, 1 - c), s1_send, s1_recv, o1 + 1),
                        (block(1 - x, 1 - y, 1 - c), s2_send, s2_recv, o2)]
            mine = block(x, y, c)
            sent = [(mine, b_send, b_recv, o0 + k) for k in range(3)]
            sent += [(block(1 - x, y, c), s1_send, s1_recv, o1), (block(x, 1 - y, c), s1_send, s1_recv, o1 + 1),
                     (halves(1 - x, y, c)[0], s1_send, s1_recv, o1 + 2), (halves(x, 1 - y, c)[1], s1_send, s1_recv, o1 + 3),
                     (block(1 - x, 1 - y, c), s2_send, s2_recv, o2)]
            out += [("recv", ref, ref, s.at[k], r.at[k]) for ref, s, r, k in arrivals]
            out += [("send", ref, ref, s.at[k], r.at[k]) for ref, s, r, k in sent]
        return out

    bufs = _split_wait([state["lands"][a] for a in which], sem_sets, waits, name)
    for i, a in enumerate(which):
        state["lands"][a] = bufs[i]
    return bufs


def _sibling_exchange_begin(part, name):
    land = lax.empty((4,) + part.shape[1:], part.dtype)

    def copies_of(ins):
        x, y, c = _mesh_pos()
        return [(ins[0].at[2 * j + (1 - c)], ins[1].at[j], j, (x, y, 1 - c)) for j in range(4)]

    send_sems, recv_sems, bufs = _split_start([part, land], copies_of, 4, name)
    return dict(bufs=bufs, sems=(send_sems, recv_sems))


def _sibling_exchange_end(state, name):
    def waits(ins, sems):
        _, _, c = _mesh_pos()
        out = []
        for j in range(4):
            for kind in ("send", "recv"):
                out.append((kind, ins[0].at[2 * j + (1 - c)], ins[1].at[j], sems[0][0].at[j], sems[0][1].at[j]))
        return out

    return _split_wait(state["bufs"], [state["sems"]], waits, name)


CHIP_FLIPS = (2, 1, 3)


def _chip_exchange_begin(csum, name):
    land = lax.empty((3,) + csum.shape[1:], csum.dtype)

    def copies_of(ins):
        x, y, c = _mesh_pos()
        chips = [(1 - x, y), (x, 1 - y), (1 - x, 1 - y)]
        return [(ins[0].at[CHIP_FLIPS[r]], ins[1].at[r], r, (px, py, c)) for r, (px, py) in enumerate(chips)]

    send_sems, recv_sems, bufs = _split_start([csum, land], copies_of, 3, name)
    return dict(bufs=bufs, sems=(send_sems, recv_sems))


def _chip_exchange_end(state, name):
    def waits(ins, sems):
        out = []
        for r in range(3):
            for kind in ("send", "recv"):
                out.append((kind, ins[0].at[CHIP_FLIPS[r]], ins[1].at[r], sems[0][0].at[r], sems[0][1].at[r]))
        return out

    return _split_wait(state["bufs"], [state["sems"]], waits, name)


def _chip_sum(part, recv, name):
    _, R, C = part.shape
    tr = _tile(R, 512, 16)
    place = jnp.stack([lax.axis_index("c"), 2 * lax.axis_index("x") + lax.axis_index("y")]).astype(jnp.int32)

    def body(place_ref, p_ref, r_ref, o_ref):
        o_ref[...] = (p_ref[...].astype(F32) + r_ref[...].astype(F32)).astype(o_ref.dtype)

    def chip(p, place_ref):
        return jnp.bitwise_xor(p, place_ref[1])

    grid_spec = pltpu.PrefetchScalarGridSpec(
        num_scalar_prefetch=1, grid=(4, R // tr),
        in_specs=[pl.BlockSpec((None, tr, C), lambda p, i, place_ref: (2 * chip(p, place_ref) + place_ref[0], i, 0)),
                  pl.BlockSpec((None, tr, C), lambda p, i, place_ref: (chip(p, place_ref), i, 0))],
        out_specs=pl.BlockSpec((None, tr, C), lambda p, i, place_ref: (p, i, 0)))
    return pl.pallas_call(body, name=name, grid_spec=grid_spec, out_shape=SDS((4, R, C), part.dtype),
                          compiler_params=_params(2))(place, part, recv)


def _bias_fwd(table_t, onehot_t, onehot_kq_t):
    H = table_t.shape[0]
    n = onehot_t.shape[1]

    def body(t_ref, oh_ref, oh_kq_ref, o_ref, o_kq_ref):
        hi, mid, lo = _split3(t_ref[...])
        for src, dst in ((oh_ref, o_ref), (oh_kq_ref, o_kq_ref)):
            oh = src[...]
            dst[...] = _dot(hi, oh, NN) + _dot(mid, oh, NN) + _dot(lo, oh, NN)

    return _CHAIN.call(body, name="bias_fwd", in_specs=[VMEM_SPEC] * 3, out_specs=[VMEM_SPEC] * 2,
                       out_shape=[SDS((H, n), F32)] * 2, compiler_params=_params(0))(table_t, onehot_t, onehot_kq_t)


def _mix_norm(x, g):
    T, D = x.shape
    tm = _tile(T, 512)

    def body(x_ref, g_ref, n_ref):
        xv = x_ref[...]
        n_ref[...] = (xv * _rms_stats(xv) * g_ref[...]).astype(BF16)

    row = pl.BlockSpec((tm, D), lambda i: (i, 0))
    return _CHAIN.call(body, name="mix_norm", grid=(T // tm,), in_specs=[row, pl.BlockSpec((1, D), lambda i: (0, 0))],
                       out_specs=row, out_shape=SDS((T, D), BF16), compiler_params=_params(1))(x, g)


def _inproj_fwd(n, w_t):
    T, D = n.shape
    P = w_t.shape[0]
    tm = _tile(T, 512)

    def body(n_ref, w_ref, proj_ref):
        proj_ref[...] = _dot(n_ref[...], w_ref[...], NT)

    return _CHAIN.call(
        body, name="inproj_fwd", grid=(T // tm,),
        in_specs=[pl.BlockSpec((tm, D), lambda i: (i, 0)), _resident((P, D))],
        out_specs=pl.BlockSpec((tm, P), lambda i: (i, 0)),
        out_shape=SDS((T, P), F32), compiler_params=_params(1))(n, w_t)


def _layer_norm_group(vg, lg, lb):
    mu = jnp.mean(vg, axis=-1, keepdims=True)
    xc = vg - mu
    rstd = lax.rsqrt(jnp.mean(xc * xc, axis=-1, keepdims=True) + EPS)
    vhat = xc * rstd
    return vhat, rstd, vhat * lg + lb


def _gmlp_fwd(proj, lg, lb, w_s, bs_t, A):
    T = proj.shape[0]
    G = A // GROUP_DIM
    tm = _tile(T, 512)
    nc = tm // CHUNK

    def body(u_ref, v_ref, lg_ref, lb_ref, w_ref, bst_ref, a_ref):
        row = lax.broadcasted_iota(jnp.int32, (CHUNK, CHUNK), 0)
        col = lax.broadcasted_iota(jnp.int32, (CHUNK, CHUNK), 1)
        causal = row >= col
        for g in range(G):
            sl = slice(g * GROUP_DIM, (g + 1) * GROUP_DIM)
            _, _, vn = _layer_norm_group(_gelu(v_ref[:, sl]), lg_ref[:, sl], lb_ref[:, sl])
            vnb = vn.astype(BF16)
            wm = jnp.where(causal, w_ref[g], 0.0).astype(BF16)
            ug = _gelu(u_ref[:, sl])
            bcol = bst_ref[:, g:g + 1]
            for c in range(nc):
                rs = slice(c * CHUNK, (c + 1) * CHUNK)
                a_ref[rs, sl] = ug[rs] * (_dot(wm, vnb[rs], NN) + bcol)

    return _CHAIN.call(
        body, name="gmlp_fwd", grid=(T // tm,),
        in_specs=[pl.BlockSpec((tm, A), lambda i: (i, 0)), pl.BlockSpec((tm, A), lambda i: (i, 1)),
                  pl.BlockSpec((1, A), lambda i: (0, 0)), pl.BlockSpec((1, A), lambda i: (0, 0)),
                  pl.BlockSpec((G, CHUNK, CHUNK), lambda i: (0, 0, 0)), pl.BlockSpec((CHUNK, G), lambda i: (0, 0))],
        out_specs=pl.BlockSpec((tm, A), lambda i: (i, 0)),
        out_shape=SDS((T, A), F32), compiler_params=_params(1))(proj, proj, lg, lb, w_s, bs_t)


def _attn_masks(first_tile):
    ii = lax.broadcasted_iota(jnp.int32, (CHUNK, 2 * CHUNK), 0)
    jj = lax.broadcasted_iota(jnp.int32, (CHUNK, 2 * CHUNK), 1)
    in_window = (jj > ii) & (jj <= ii + CHUNK)
    first_mask = in_window & jnp.logical_or(jnp.logical_not(first_tile), jj >= CHUNK)
    return in_window, first_mask


def _softmax_with_sink(s, sink, axis):
    m = jnp.maximum(jnp.max(s, axis=axis, keepdims=True), sink)
    p = jnp.exp(s - m)
    e_sink = jnp.exp(sink - m)
    inv = 1.0 / (jnp.sum(p, axis=axis, keepdims=True) + e_sink)
    return p * inv, e_sink * inv


def _pad_heads(band, group):
    lane = lax.broadcasted_iota(jnp.int32, band.shape, 1)
    if group == 0:
        low = jnp.where(lane < HEAD_DIM, band, 0.0)
        high = pltpu.roll(low, HEAD_DIM, 1)
    else:
        high = jnp.where(lane >= HEAD_DIM, band, 0.0)
        low = pltpu.roll(high, HEAD_DIM, 1)
    return low.astype(BF16), high.astype(BF16)


def _attn_specs(tq, A, B, reverse_tiles=None):
    nb = tq // CHUNK
    kcol = (2 * A + B) // LANE
    if reverse_tiles is None:
        tile = lambda i: i
    else:
        tile = lambda i: reverse_tiles - 1 - i
    prev = lambda i: jnp.maximum(tile(i) * nb - 1, 0)
    return [pl.BlockSpec((tq, B), lambda i: (tile(i), 2 * A // B)),
            pl.BlockSpec((tq, LANE), lambda i: (tile(i), kcol)),
            pl.BlockSpec((tq, LANE), lambda i: (tile(i), kcol + 1)),
            pl.BlockSpec((CHUNK, LANE), lambda i: (prev(i), kcol)),
            pl.BlockSpec((CHUNK, LANE), lambda i: (prev(i), kcol + 1))]


def _attn_fwd(proj, bias, sinks, A, B):
    T = proj.shape[0]
    H = B // HEAD_DIM
    qpk = H // KV_HEADS
    tq = _tile(T, 512)
    nb = tq // CHUNK

    scale = HEAD_DIM ** -0.5

    def body(sink_ref, q_ref, k_ref, v_ref, kp_ref, vp_ref, bias_ref, o_ref):
        in_window, first_mask = _attn_masks(pl.program_id(0) == 0)
        for b in range(nb):
            rows = slice(b * CHUNK, (b + 1) * CHUNK)
            if b == 0:
                kprev, vprev, mask = kp_ref[...], vp_ref[...], first_mask
            else:
                prows = slice((b - 1) * CHUNK, b * CHUNK)
                kprev, vprev, mask = k_ref[prows, :], v_ref[prows, :], in_window
            kband = jnp.concatenate([kprev, k_ref[rows, :]], axis=0)
            vband = jnp.concatenate([vprev, v_ref[rows, :]], axis=0)
            k_pads = [_pad_heads(kband, g) for g in range(KV_HEADS)]
            v_both = [jnp.concatenate(_pad_heads(vband, g), axis=0) for g in range(KV_HEADS)]
            scores = []
            for pair in range(H // 2):
                h = 2 * pair
                qs = (q_ref[rows, h * HEAD_DIM:(h + 2) * HEAD_DIM] * scale).astype(BF16)
                scores += [_dot(qs, kz, NT) for kz in k_pads[h // qpk]]
            probs = [_softmax_with_sink(jnp.where(mask, s + bias_ref[h], NEG), sink_ref[h], -1)[0].astype(BF16)
                     for h, s in enumerate(scores)]
            outs = [_dot(jnp.concatenate(probs[h:h + 2], axis=1), v_both[h // qpk], NN) for h in range(0, H, 2)]
            o_ref[rows, :] = jnp.concatenate(outs, axis=1)

    return _CHAIN.call(
        body, name="attn_fwd", grid=(T // tq,),
        in_specs=[pl.BlockSpec(memory_space=pltpu.SMEM)] + _attn_specs(tq, A, B)
        + [pl.BlockSpec((H, CHUNK, 2 * CHUNK), lambda i: (0, 0, 0))],
        out_specs=pl.BlockSpec((tq, B), lambda i: (i, 0)),
        out_shape=SDS((T, B), F32), compiler_params=_params(1))(sinks, proj, proj, proj, proj, proj, bias)


def _outproj_fwd(a, b, ga, gb, x, w, g_ffn):
    T, A = a.shape
    B = b.shape[1]
    D = x.shape[1]
    tm = _tile(T, 512)

    def body(a_ref, b_ref, ga_ref, gb_ref, x_ref, w_ref, gf_ref, h_ref, mix_ref, n_ref):
        av, bv = a_ref[...], b_ref[...]
        mix_ref[:, :A] = (av * _rms_stats(av) * ga_ref[...]).astype(BF16)
        mix_ref[:, A:] = (bv * _rms_stats(bv) * gb_ref[...]).astype(BF16)
        hv = x_ref[...] + _dot(mix_ref[...], w_ref[...], NN)
        h_ref[...] = hv
        n_ref[...] = (hv * _rms_stats(hv) * gf_ref[...]).astype(BF16)

    row = pl.BlockSpec((tm, D), lambda i: (i, 0))
    return _CHAIN.call(
        body, name="outproj_fwd", grid=(T // tm,),
        in_specs=[pl.BlockSpec((tm, A), lambda i: (i, 0)), pl.BlockSpec((tm, B), lambda i: (i, 0)),
                  pl.BlockSpec((1, A), lambda i: (0, 0)), pl.BlockSpec((1, B), lambda i: (0, 0)),
                  row, _resident((A + B, D)), pl.BlockSpec((1, D), lambda i: (0, 0))],
        out_specs=[row, pl.BlockSpec((tm, A + B), lambda i: (i, 0)), row],
        out_shape=[SDS((T, D), F32), SDS((T, A + B), BF16), SDS((T, D), BF16)],
        compiler_params=_params(1))(a, b, ga, gb, x, w, g_ffn)


def _ffn_up(n, w_up):
    T, D = n.shape
    Fb = w_up.shape[2]
    F = N_DEV * Fb
    tm, tf = _tile(T, 1024), _tile(Fb, 1024)
    per = Fb // tf

    def body(n_ref, wu_ref, z_ref):
        z_ref[...] = jnp.maximum(_dot(n_ref[...], wu_ref[...], NN), 0.0).astype(BF16)

    return _CHAIN.call(
        body, name="ffn_up", grid=(T // tm, F // tf),
        in_specs=[pl.BlockSpec((tm, D), lambda i, j: (i, 0)),
                  pl.BlockSpec((None, D, tf), lambda i, j: (j // per, 0, j % per))],
        out_specs=pl.BlockSpec((tm, tf), lambda i, j: (i, j)),
        out_shape=SDS((T, F), BF16), compiler_params=_params(2))(n, w_up)


def _ffn_down(h1, z, w_down):
    T, D = h1.shape
    F = w_down.shape[0]
    tm, tn, tk = _tile(T, 1024), _tile(D, 1024), _tile(F, 4096)

    def body(h_ref, z_ref, wd_ref, h2_ref):
        k = pl.program_id(2)

        @pl.when(k == 0)
        def _():
            h2_ref[...] = h_ref[...]

        zf = z_ref[...].astype(F32)
        h2_ref[...] += _dot((zf * zf).astype(BF16), wd_ref[...], NN)

    return _CHAIN.call(
        body, name="ffn_down", grid=(T // tm, D // tn, F // tk),
        in_specs=[pl.BlockSpec((tm, tn), lambda i, j, k: (i, j)), pl.BlockSpec((tm, tk), lambda i, j, k: (i, k)),
                  pl.BlockSpec((tk, tn), lambda i, j, k: (k, j))],
        out_specs=pl.BlockSpec((tm, tn), lambda i, j, k: (i, j)),
        out_shape=SDS((T, D), F32), compiler_params=_params(3))(h1, z, w_down)


def _final_loss(h2, g, target):
    T, D = h2.shape
    tm = _tile(T, 512)

    def body(h_ref, g_ref, t_ref, loss_ref, dg_ref, dh_ref, dhb_ref):
        @pl.when(pl.program_id(0) == 0)
        def _():
            loss_ref[...] = jnp.zeros_like(loss_ref)
            dg_ref[...] = jnp.zeros_like(dg_ref)

        hv, gv = h_ref[...], g_ref[...]
        r = _rms_stats(hv)
        hn = hv * r
        e = hn * gv - t_ref[...]
        loss_ref[...] += (0.5 / D) * jnp.sum(jnp.sum(e * e, axis=0, keepdims=True), axis=-1, keepdims=True)
        dy = e * (1.0 / D)
        dg_ref[...] += jnp.sum(dy * hn, axis=0, keepdims=True)
        dh = _rms_bwd(dy, hv, r, gv)
        dh_ref[...] = dh
        dhb_ref[...] = dh.astype(BF16)

    return _CHAIN.call(
        body, name="final_loss", grid=(T // tm,),
        in_specs=[pl.BlockSpec((tm, D), lambda i: (i, 0)), pl.BlockSpec((1, D), lambda i: (0, 0)),
                  pl.BlockSpec((tm, D), lambda i: (i, 0))],
        out_specs=[pl.BlockSpec((1, 1), lambda i: (0, 0)), pl.BlockSpec((1, D), lambda i: (0, 0)),
                   pl.BlockSpec((tm, D), lambda i: (i, 0)), pl.BlockSpec((tm, D), lambda i: (i, 0))],
        out_shape=[SDS((1, 1), F32), SDS((1, D), F32), SDS((T, D), F32), SDS((T, D), BF16)],
        compiler_params=_params(1))(h2, g, target)


def _ffn_down_bwd(dh2b, z, w_down):
    T, D = dh2b.shape
    F = w_down.shape[0]
    tm, tf = _tile(T, 1024), _tile(F, 1024)

    def body(dh_ref, z_ref, wd_ref, dzp_ref):
        dzz = _dot(dh_ref[...], wd_ref[...], NT)
        dzp_ref[...] = (dzz * (2.0 * z_ref[...].astype(F32))).astype(BF16)

    return _CHAIN.call(
        body, name="ffn_down_bwd", grid=(T // tm, F // tf),
        in_specs=[pl.BlockSpec((tm, D), lambda i, j: (i, 0)), pl.BlockSpec((tm, tf), lambda i, j: (i, j)),
                  pl.BlockSpec((tf, D), lambda i, j: (j, 0))],
        out_specs=pl.BlockSpec((tm, tf), lambda i, j: (i, j)),
        out_shape=SDS((T, F), BF16), compiler_params=_params(2))(dh2b, z, w_down)


def _ffn_up_bwd(dzp, w_up_t):
    T, F = dzp.shape
    D = w_up_t.shape[1]
    tm, tn, tk = _tile(T, 1024), _tile(D, 1024), _tile(F, 4096)

    def body(dzp_ref, w_ref, dn_ref):
        part = _dot(dzp_ref[...], w_ref[...], NN)

        @pl.when(pl.program_id(2) == 0)
        def _():
            dn_ref[...] = part

        @pl.when(pl.program_id(2) > 0)
        def _():
            dn_ref[...] += part

    return _CHAIN.call(
        body, name="ffn_up_bwd", grid=(T // tm, D // tn, F // tk),
        in_specs=[pl.BlockSpec((tm, tk), lambda i, j, k: (i, k)), pl.BlockSpec((tk, tn), lambda i, j, k: (k, j))],
        out_specs=pl.BlockSpec((tm, tn), lambda i, j, k: (i, j)),
        out_shape=SDS((T, D), F32), compiler_params=_params(3))(dzp, w_up_t)


def _ffn_norm_bwd(dn, dh2, h1, g):
    T, D = h1.shape
    tm = _tile(T, 256)

    def body(dn_ref, dh_ref, h_ref, g_ref, dh1_ref, dh1b_ref, dg_ref):
        @pl.when(pl.program_id(0) == 0)
        def _():
            dg_ref[...] = jnp.zeros_like(dg_ref)

        hv, dnv = h_ref[...], dn_ref[...]
        r = _rms_stats(hv)
        dg_ref[...] += jnp.sum(dnv * (hv * r), axis=0, keepdims=True)
        dh1 = dh_ref[...] + _rms_bwd(dnv, hv, r, g_ref[...])
        dh1_ref[...] = dh1
        dh1b_ref[...] = dh1.astype(BF16)

    row = pl.BlockSpec((tm, D), lambda i: (i, 0))
    vec = pl.BlockSpec((1, D), lambda i: (0, 0))
    return _CHAIN.call(
        body, name="ffn_norm_bwd", grid=(T // tm,), in_specs=[row, row, row, vec], out_specs=[row, row, vec],
        out_shape=[SDS((T, D), F32), SDS((T, D), BF16), SDS((1, D), F32)], compiler_params=_params(1))(dn, dh2, h1, g)


def _matmul_tn(a, b, name, square_a=False, col_blocks=None):
    T, K = a.shape
    N = b.shape[1]
    tk = _tile(K, 1792)
    tn = _tile(N if col_blocks is None else N // col_blocks, 1024 if tk <= 1024 else 512)

    def body(a_ref, b_ref, o_ref):
        av = a_ref[...]
        if square_a:
            af = av.astype(F32)
            av = (af * af).astype(BF16)
        o_ref[...] = _dot(av, b_ref[...], TN).astype(o_ref.dtype)

    if col_blocks is None:
        out_shape = SDS((K, N), BF16)
        out_spec = pl.BlockSpec((tk, tn), lambda i, j: (i, j))
    else:
        per = (N // col_blocks) // tn
        out_shape = SDS((col_blocks, K, N // col_blocks), BF16)
        out_spec = pl.BlockSpec((None, tk, tn), lambda i, j: (j // per, i, j % per))
    return _CHAIN.call(
        body, name=name, grid=(K // tk, N // tn),
        in_specs=[pl.BlockSpec((T, tk), lambda i, j: (0, i)), pl.BlockSpec((T, tn), lambda i, j: (0, j))],
        out_specs=out_spec, out_shape=out_shape, compiler_params=_params(2))(a, b)


def _outproj_bwd(dh1b, w, a, b, ga, gb):
    T, D = dh1b.shape
    A, B = a.shape[1], b.shape[1]
    tm = _tile(T, 512)

    def body(dh_ref, w_ref, a_ref, b_ref, ga_ref, gb_ref, da_ref, db_ref, dga_ref, dgb_ref):
        @pl.when(pl.program_id(0) == 0)
        def _():
            dga_ref[...] = jnp.zeros_like(dga_ref)
            dgb_ref[...] = jnp.zeros_like(dgb_ref)

        dmix = _dot(dh_ref[...], w_ref[...], NT)
        for src_ref, g_ref, dx_ref, dg_ref, dn in ((a_ref, ga_ref, da_ref, dga_ref, dmix[:, :A]),
                                                   (b_ref, gb_ref, db_ref, dgb_ref, dmix[:, A:])):
            xv = src_ref[...]
            r = _rms_stats(xv)
            dg_ref[...] += jnp.sum(dn * (xv * r), axis=0, keepdims=True)
            dx_ref[...] = _rms_bwd(dn, xv, r, g_ref[...])

    return _CHAIN.call(
        body, name="outproj_bwd", grid=(T // tm,),
        in_specs=[pl.BlockSpec((tm, D), lambda i: (i, 0)), _resident((A + B, D)),
                  pl.BlockSpec((tm, A), lambda i: (i, 0)), pl.BlockSpec((tm, B), lambda i: (i, 0)),
                  pl.BlockSpec((1, A), lambda i: (0, 0)), pl.BlockSpec((1, B), lambda i: (0, 0))],
        out_specs=[pl.BlockSpec((tm, A), lambda i: (i, 0)), pl.BlockSpec((tm, B), lambda i: (i, 0)),
                   pl.BlockSpec((1, A), lambda i: (0, 0)), pl.BlockSpec((1, B), lambda i: (0, 0))],
        out_shape=[SDS((T, A), F32), SDS((T, B), F32), SDS((1, A), F32), SDS((1, B), F32)],
        compiler_params=_params(1))(dh1b, w, a, b, ga, gb)


def _gmlp_bwd(proj, da, lg, lb, w_s, w_st, bs_t, A):
    T = proj.shape[0]
    G = A // GROUP_DIM
    tm = _tile(T, 512)
    nc = tm // CHUNK

    def body(u_ref, v_ref, da_ref, lg_ref, lb_ref, w_ref, wt_ref, bst_ref, duv_ref, dlg_ref, dlb_ref, dw_ref, dbs_ref):
        @pl.when(pl.program_id(0) == 0)
        def _():
            dlg_ref[...] = jnp.zeros_like(dlg_ref)
            dlb_ref[...] = jnp.zeros_like(dlb_ref)
            dw_ref[...] = jnp.zeros_like(dw_ref)
            dbs_ref[...] = jnp.zeros_like(dbs_ref)

        row = lax.broadcasted_iota(jnp.int32, (CHUNK, CHUNK), 0)
        col = lax.broadcasted_iota(jnp.int32, (CHUNK, CHUNK), 1)
        lower = row >= col
        upper = row <= col
        for g in range(G):
            sl = slice(g * GROUP_DIM, (g + 1) * GROUP_DIM)
            lgv = lg_ref[:, sl]
            vg, vg_grad = _gelu_and_grad(v_ref[:, sl])
            vhat, rstd, vn = _layer_norm_group(vg, lgv, lb_ref[:, sl])
            vnb = vn.astype(BF16)
            ug, ug_grad = _gelu_and_grad(u_ref[:, sl])
            dav = da_ref[:, sl]
            wm = jnp.where(lower, w_ref[g], 0.0).astype(BF16)
            wmt = jnp.where(upper, wt_ref[g], 0.0).astype(BF16)
            bcol = bst_ref[:, g:g + 1]
            dw_acc = jnp.zeros((CHUNK, CHUNK), F32)
            dbs_acc = jnp.zeros((CHUNK, 1), F32)
            dvn_parts = []
            dug_parts = []
            for c in range(nc):
                rs = slice(c * CHUNK, (c + 1) * CHUNK)
                mixed = _dot(wm, vnb[rs], NN) + bcol
                dug_parts.append(dav[rs] * mixed)
                dmix = dav[rs] * ug[rs]
                dbs_acc = dbs_acc + jnp.sum(dmix, axis=-1, keepdims=True)
                dmixb = dmix.astype(BF16)
                dw_acc = dw_acc + _dot(dmixb, vnb[rs], NT)
                dvn_parts.append(_dot(wmt, dmixb, NN))
            dvn = jnp.concatenate(dvn_parts, axis=0)
            dug = jnp.concatenate(dug_parts, axis=0)
            dw_ref[g] += jnp.where(lower, dw_acc, 0.0)
            dbs_ref[:, g:g + 1] += dbs_acc
            dlg_ref[:, sl] += jnp.sum(dvn * vhat, axis=0, keepdims=True)
            dlb_ref[:, sl] += jnp.sum(dvn, axis=0, keepdims=True)
            dvhat = dvn * lgv
            dvg = rstd * (dvhat - jnp.mean(dvhat, axis=-1, keepdims=True)
                          - vhat * jnp.mean(dvhat * vhat, axis=-1, keepdims=True))
            duv_ref[:, sl] = (dug * ug_grad).astype(BF16)
            duv_ref[:, A + g * GROUP_DIM:A + (g + 1) * GROUP_DIM] = (dvg * vg_grad).astype(BF16)

    return _CHAIN.call(
        body, name="gmlp_bwd", grid=(T // tm,),
        in_specs=[pl.BlockSpec((tm, A), lambda i: (i, 0)), pl.BlockSpec((tm, A), lambda i: (i, 1)),
                  pl.BlockSpec((tm, A), lambda i: (i, 0)),
                  pl.BlockSpec((1, A), lambda i: (0, 0)), pl.BlockSpec((1, A), lambda i: (0, 0)),
                  pl.BlockSpec((G, CHUNK, CHUNK), lambda i: (0, 0, 0)),
                  pl.BlockSpec((G, CHUNK, CHUNK), lambda i: (0, 0, 0)), pl.BlockSpec((CHUNK, G), lambda i: (0, 0))],
        out_specs=[pl.BlockSpec((tm, 2 * A), lambda i: (i, 0)),
                   pl.BlockSpec((1, A), lambda i: (0, 0)), pl.BlockSpec((1, A), lambda i: (0, 0)),
                   pl.BlockSpec((G, CHUNK, CHUNK), lambda i: (0, 0, 0)), pl.BlockSpec((CHUNK, G), lambda i: (0, 0))],
        out_shape=[SDS((T, 2 * A), BF16), SDS((1, A), F32), SDS((1, A), F32),
                   SDS((G, CHUNK, CHUNK), F32), SDS((CHUNK, G), F32)],
        compiler_params=_params(1))(proj, proj, da, lg, lb, w_s, w_st, bs_t)


def _attn_bwd(proj, do, duv, bias_t, sinks, A, B):
    T, P = proj.shape
    H = B // HEAD_DIM
    qpk = H // KV_HEADS
    tq = _tile(T, 512)
    nb = tq // CHUNK
    n_tiles = T // tq
    scale = HEAD_DIM ** -0.5
    rev = lambda i: n_tiles - 1 - i

    def body(sink_ref, q_ref, k_ref, v_ref, kp_ref, vp_ref, do_ref, duv_ref, bias_ref,
             dproj_ref, dbias_ref, dsink_ref, carry, dkv, sacc):
        step = pl.program_id(0)

        @pl.when(step == 0)
        def _():
            carry[...] = jnp.zeros_like(carry)
            sacc[...] = jnp.zeros_like(sacc)
            dbias_ref[...] = jnp.zeros_like(dbias_ref)

        jj = lax.broadcasted_iota(jnp.int32, (2 * CHUNK, CHUNK), 0)
        ii = lax.broadcasted_iota(jnp.int32, (2 * CHUNK, CHUNK), 1)
        in_window = (jj > ii) & (jj <= ii + CHUNK)
        first_mask = in_window & jnp.logical_or(step != n_tiles - 1, jj >= CHUNK)
        low_query = lax.broadcasted_iota(jnp.int32, (CHUNK, LANE), 1) < HEAD_DIM
        low_key = lax.broadcasted_iota(jnp.int32, (2 * CHUNK, LANE), 1) < HEAD_DIM

        def split_pair(pair_bf16):
            zero = jnp.zeros_like(pair_bf16)
            return jnp.concatenate([jnp.where(low_query, pair_bf16, zero), jnp.where(low_query, zero, pair_bf16)], axis=0)

        dproj_ref[:, :2 * A] = duv_ref[...]
        dkv[...] = jnp.zeros_like(dkv)
        for b in range(nb):
            rows = slice(b * CHUNK, (b + 1) * CHUNK)
            band = slice(b * CHUNK, (b + 2) * CHUNK)
            if b == 0:
                kprev, vprev, mask = kp_ref[...], vp_ref[...], first_mask
            else:
                prows = slice((b - 1) * CHUNK, b * CHUNK)
                kprev, vprev, mask = k_ref[prows, :], v_ref[prows, :], in_window
            kband = jnp.concatenate([kprev, k_ref[rows, :]], axis=0)
            vband = jnp.concatenate([vprev, v_ref[rows, :]], axis=0)
            k_pads = [_pad_heads(kband, g) for g in range(KV_HEADS)]
            v_pads = [_pad_heads(vband, g) for g in range(KV_HEADS)]
            queries, douts, scores, dprobs = [], [], [], []
            for pair in range(H // 2):
                cols = slice(2 * pair * HEAD_DIM, (2 * pair + 2) * HEAD_DIM)
                qs = (q_ref[rows, cols] * scale).astype(BF16)
                dob = do_ref[rows, cols].astype(BF16)
                queries.append(qs)
                douts.append(dob)
                scores += [_dot(kz, qs, NT) for kz in k_pads[2 * pair // qpk]]
                dprobs += [_dot(vz, dob, NT) for vz in v_pads[2 * pair // qpk]]
            probs, dscores = [], []
            for h in range(H):
                pt, p_sink = _softmax_with_sink(jnp.where(mask, scores[h] + bias_ref[h], NEG), sink_ref[h], 0)
                delta = jnp.sum(pt * dprobs[h], axis=0, keepdims=True)
                dst = pt * (dprobs[h] - delta)
                dbias_ref[h] += dst
                sacc[h:h + 1, :] += -(p_sink * delta)
                probs.append(pt.astype(BF16))
                dscores.append(dst.astype(BF16))
            dq_parts, dk_groups, dv_groups = [], [], []
            for g in range(KV_HEADS):
                k_both = jnp.concatenate(k_pads[g], axis=0)
                dk_acc = jnp.zeros((2 * CHUNK, LANE), F32)
                dv_acc = jnp.zeros((2 * CHUNK, LANE), F32)
                for pair in range(g * qpk // 2, (g + 1) * qpk // 2):
                    pair_heads = slice(2 * pair, 2 * pair + 2)
                    dk_acc = dk_acc + _dot(jnp.concatenate(dscores[pair_heads], axis=1), split_pair(queries[pair]), NN)
                    dv_acc = dv_acc + _dot(jnp.concatenate(probs[pair_heads], axis=1), split_pair(douts[pair]), NN)
                    dq_parts.append(_dot(jnp.concatenate(dscores[pair_heads], axis=0), k_both, TN) * scale)
                dk_groups.append(dk_acc + pltpu.roll(dk_acc, HEAD_DIM, 1))
                dv_groups.append(dv_acc + pltpu.roll(dv_acc, HEAD_DIM, 1))
            dkv[band, :LANE] += jnp.where(low_key, dk_groups[0], dk_groups[1])
            dkv[band, LANE:] += jnp.where(low_key, dv_groups[0], dv_groups[1])
            dproj_ref[rows, 2 * A:2 * A + B] = jnp.concatenate(dq_parts, axis=1).astype(BF16)
        last = slice(tq, tq + CHUNK)
        dkv[last, :] += carry[...]
        dproj_ref[:, 2 * A + B:] = dkv[CHUNK:, :].astype(BF16)
        carry[...] = dkv[:CHUNK, :]

        @pl.when(step == n_tiles - 1)
        def _():
            dsink_ref[...] = jnp.sum(sacc[...], axis=1, keepdims=True)

    specs = _attn_specs(tq, A, B, reverse_tiles=n_tiles)
    return _CHAIN.call(
        body, name="attn_bwd", grid=(n_tiles,),
        in_specs=[pl.BlockSpec(memory_space=pltpu.SMEM)] + specs
        + [pl.BlockSpec((tq, B), lambda i: (rev(i), 0)), pl.BlockSpec((tq, 2 * A), lambda i: (rev(i), 0)),
           pl.BlockSpec((H, 2 * CHUNK, CHUNK), lambda i: (0, 0, 0))],
        out_specs=[pl.BlockSpec((tq, P), lambda i: (rev(i), 0)),
                   pl.BlockSpec((H, 2 * CHUNK, CHUNK), lambda i: (0, 0, 0)), pl.BlockSpec((H, 1), lambda i: (0, 0))],
        out_shape=[SDS((T, P), BF16), SDS((H, 2 * CHUNK, CHUNK), F32), SDS((H, 1), F32)],
        scratch_shapes=[pltpu.VMEM((CHUNK, 2 * LANE), F32), pltpu.VMEM((tq + CHUNK, 2 * LANE), F32),
                        pltpu.VMEM((H, LANE), F32)],
        compiler_params=_params(1))(sinks, proj, proj, proj, proj, proj, do, duv, bias_t)


def _bias_bwd(dbias, onehot):
    H = dbias.shape[0]
    nbk = onehot.shape[1]

    def body(d_ref, oh_ref, o_ref):
        hi, mid, lo = _split3(d_ref[...])
        oh = oh_ref[...]
        o_ref[...] = _dot(hi, oh, NN) + _dot(mid, oh, NN) + _dot(lo, oh, NN)

    return _CHAIN.call(body, name="bias_bwd", in_specs=[VMEM_SPEC] * 2, out_specs=VMEM_SPEC, out_shape=SDS((H, nbk), F32),
                       compiler_params=_params(0))(dbias, onehot)


def _inproj_bwd(dproj, w_t, x, dh1, g):
    T, P = dproj.shape
    D = x.shape[1]
    tm = _tile(T, 512)

    def body(dp_ref, w_ref, x_ref, dh_ref, g_ref, dx_ref, dg_ref):
        @pl.when(pl.program_id(0) == 0)
        def _():
            dg_ref[...] = jnp.zeros_like(dg_ref)

        dn = _dot(dp_ref[...], w_ref[...], NN)
        xv = x_ref[...]
        r = _rms_stats(xv)
        dg_ref[...] += jnp.sum(dn * (xv * r), axis=0, keepdims=True)
        dx_ref[...] = dh_ref[...] + _rms_bwd(dn, xv, r, g_ref[...])

    return _CHAIN.call(
        body, name="inproj_bwd", grid=(T // tm,),
        in_specs=[pl.BlockSpec((tm, P), lambda i: (i, 0)), _resident((P, D)),
                  pl.BlockSpec((tm, D), lambda i: (i, 0)), pl.BlockSpec((tm, D), lambda i: (i, 0)),
                  pl.BlockSpec((1, D), lambda i: (0, 0))],
        out_specs=[pl.BlockSpec((tm, D), lambda i: (i, 0)), pl.BlockSpec((1, D), lambda i: (0, 0))],
        out_shape=[SDS((T, D), F32), SDS((1, D), F32)], compiler_params=_params(1))(dproj, w_t, x, dh1, g)


def _adamw(w, g, m, v):
    m = ADAM_B1 * m + (1.0 - ADAM_B1) * g
    v = ADAM_B2 * v + (1.0 - ADAM_B2) * (g * g)
    m_hat = m / (1.0 - ADAM_B1 ** ADAM_STEP)
    v_hat = v / (1.0 - ADAM_B2 ** ADAM_STEP)
    delta = -ADAM_LR * (m_hat / (jnp.sqrt(v_hat) + ADAM_EPS) + ADAM_WD * w)
    return delta, m, v


def _adam_sharded(csum, recv, w, m, v, name):
    R, C = w.shape
    tr = _tile(R, 256, 16)

    def body(own_ref, recv_ref, w_ref, m_ref, v_ref, g_ref, d_ref, nm_ref, nv_ref):
        g = own_ref[...].astype(F32)
        for r in range(3):
            g = g + recv_ref[r].astype(F32)
        delta, nm, nv = _adamw(w_ref[...], g, m_ref[...], v_ref[...])
        g_ref[...] = g
        d_ref[...] = delta
        nm_ref[...] = nm
        nv_ref[...] = nv

    blk = pl.BlockSpec((tr, C), lambda i: (i, 0))
    return _CHAIN.call(
        body, name=name, grid=(R // tr,),
        in_specs=[pl.BlockSpec((None, tr, C), lambda i: (0, i, 0)), pl.BlockSpec((3, tr, C), lambda i: (0, i, 0)),
                  blk, blk, blk],
        out_specs=[blk] * 4, out_shape=[SDS((R, C), F32)] * 4, compiler_params=_params(1))(csum, recv, w, m, v)


def _rows2d(shape):
    return (int(np.prod(shape[:-1])) if len(shape) > 1 else 1, shape[-1])


def _small_layout(shapes):
    totals, places = {}, []
    for s in shapes:
        r, w = _rows2d(s)
        off = totals.get(w, 0)
        places.append((w, off, r))
        totals[w] = off + -(-r // 8) * 8
    return {w: -(-t // 32) * 32 for w, t in totals.items()}, places


def _pack_small(arrays, totals, places):
    bufs = []
    for w, total in totals.items():
        buf = jnp.zeros((total, w), F32)
        for a, (pw, off, r) in zip(arrays, places):
            if pw == w:
                buf = lax.dynamic_update_slice(buf, a.reshape(r, w).astype(F32), (off, 0))
        bufs.append(buf)
    return bufs


def _adam_small(gathered, totals, places, ws, ms, vs):
    widths = list(totals)
    n, nw = len(places), len(widths)

    def body(*refs):
        gath, params, outs = refs[:nw], refs[nw:nw + 3 * n], refs[nw + 3 * n:]
        for p, (w, off, r) in enumerate(places):
            g_ref = gath[widths.index(w)]
            g = g_ref[0, off:off + r, :]
            for d in range(1, N_DEV):
                g = g + g_ref[d, off:off + r, :]
            delta, nm, nv = _adamw(params[p][...], g, params[n + p][...], params[2 * n + p][...])
            for k, val in enumerate((g, delta, nm, nv)):
                outs[4 * p + k][...] = val

    shapes2d = [SDS((r, w), F32) for w, _, r in places for _ in range(4)]
    outs = _CHAIN.call(body, name="adam_small", in_specs=[VMEM_SPEC] * (nw + 3 * n), out_specs=[VMEM_SPEC] * (4 * n),
                       out_shape=shapes2d, compiler_params=_params(0))(*gathered, *ws, *ms, *vs)
    return [outs[4 * p:4 * p + 4] for p in range(n)]


def kernel(x, rel_bias_table, mix_norm_g, w_in, gate_norm_g, gate_norm_b, w_spatial, b_spatial, attn_sinks, out_norm_a_g, out_norm_b_g, w_out, ffn_norm_g, w_up, w_down, final_norm_g, loss_target, m_rel_bias_table, m_mix_norm_g, m_w_in, m_gate_norm_g, m_gate_norm_b, m_w_spatial, m_b_spatial, m_attn_sinks, m_out_norm_a_g, m_out_norm_b_g, m_w_out, m_ffn_norm_g, m_w_up, m_w_down, m_final_norm_g, v_rel_bias_table, v_mix_norm_g, v_w_in, v_gate_norm_g, v_gate_norm_b, v_w_spatial, v_b_spatial, v_attn_sinks, v_out_norm_a_g, v_out_norm_b_g, v_w_out, v_ffn_norm_g, v_w_up, v_w_down, v_final_norm_g):
    T, D = x.shape[1], x.shape[2]
    A = D // 2
    B = D // 2
    G = A // GROUP_DIM
    H = B // HEAD_DIM
    P = 2 * A + B + 2 * KV_HEADS * HEAD_DIM
    xs = x.reshape(T, D)
    target = loss_target.reshape(T, D)

    win_t, m_win_t, v_win_t = (jnp.swapaxes(a[0], 0, 1) for a in (w_in, m_w_in, v_w_in))
    shards = [win_t.astype(BF16), w_out[0].astype(BF16), w_up[0].astype(BF16), w_down[0].astype(BF16)]
    _CHAIN.token = None
    gather = _gather_begin(shards)
    _gather_step(gather, [(0, 0)], "gather_start")

    g1, g2, g3 = mix_norm_g.reshape(1, D), ffn_norm_g.reshape(1, D), final_norm_g.reshape(1, D)
    lg, lb = gate_norm_g.reshape(1, A), gate_norm_b.reshape(1, A)
    ws = w_spatial[0]
    ws_t = jnp.swapaxes(ws, 1, 2)
    bs_t = jnp.transpose(b_spatial[0])
    ga, gb = out_norm_a_g.reshape(1, A), out_norm_b_g.reshape(1, B)
    sinks = attn_sinks.reshape(H)
    bucket, in_window = _t5_bucket()
    onehot_np = ((bucket[:, :, None] == np.arange(N_BUCKETS)) & in_window[:, :, None]).astype(np.float32)
    onehot = jnp.asarray(onehot_np.reshape(-1, N_BUCKETS)).astype(BF16)
    onehot_kq = jnp.asarray(onehot_np.transpose(1, 0, 2).reshape(-1, N_BUCKETS)).astype(BF16)

    bias, bias_t = _bias_fwd(jnp.transpose(rel_bias_table), jnp.transpose(onehot), jnp.transpose(onehot_kq))
    bias, bias_t = bias.reshape(H, CHUNK, 2 * CHUNK), bias_t.reshape(H, 2 * CHUNK, CHUNK)
    n1 = _mix_norm(xs, g1)
    _gather_step(gather, [(0, 1), (1, 0), (2, 0)], "gather_in_1")
    _gather_step(gather, [(0, 2)], "gather_in_2")
    (win_g,) = _gather_end(gather, [0], "gather_in_end")
    win_t_full = win_g.reshape(P, D)
    proj = _inproj_fwd(n1, win_t_full)
    _gather_step(gather, [(1, 1)], "gather_out_1")
    a_out = _gmlp_fwd(proj, lg, lb, ws, bs_t, A)
    _gather_step(gather, [(1, 2), (2, 1), (3, 0)], "gather_out_2_up_1")
    b_out = _attn_fwd(proj, bias, sinks, A, B)
    (wout_g,) = _gather_end(gather, [1], "gather_out_end")
    _gather_step(gather, [(2, 2)], "gather_up_2")
    wout_full = wout_g.reshape(A + B, D)
    h1, mixed, n2 = _outproj_fwd(a_out, b_out, ga, gb, xs, wout_full, g2)
    (wup_g,) = _gather_end(gather, [2], "gather_up_end")
    _gather_step(gather, [(3, 1)], "gather_down_1")
    wup_t = jnp.transpose(wup_g, (0, 2, 1)).reshape(-1, D)
    z = _ffn_up(n2, wup_g)
    _gather_step(gather, [(3, 2)], "gather_down_2")
    (wdown_g,) = _gather_end(gather, [3], "gather_down_end")
    h2 = _ffn_down(h1, z, wdown_g.reshape(-1, D))
    loss_part, dg3, dh2, dh2b = _final_loss(h2, g3, target)

    def reduce_to_chip(state, name):
        part, received = _sibling_exchange_end(state, name + "_sib_end")
        return _chip_exchange_begin(_chip_sum(part, received, name + "_chip_sum"), name + "_chip")

    dwdown = _matmul_tn(z, dh2b, "grad_w_down", square_a=True).reshape(wdown_g.shape)
    sib_down = _sibling_exchange_begin(dwdown, "rs_down_sib")
    dzp = _ffn_down_bwd(dh2b, z, wdown_g.reshape(-1, D))
    chip_down = reduce_to_chip(sib_down, "rs_down")
    dwup = _matmul_tn(n2, dzp, "grad_w_up", col_blocks=N_DEV)
    sib_up = _sibling_exchange_begin(dwup, "rs_up_sib")
    dh1, dh1b, dg2 = _ffn_norm_bwd(_ffn_up_bwd(dzp, wup_t), dh2, h1, g2)
    chip_up = reduce_to_chip(sib_up, "rs_up")
    da, db, dga, dgb = _outproj_bwd(dh1b, wout_full, a_out, b_out, ga, gb)
    dwout = _matmul_tn(mixed, dh1b, "grad_w_out").reshape(wout_g.shape)
    sib_out = _sibling_exchange_begin(dwout, "rs_out_sib")
    duv, dlg, dlb, dws, dbs_t = _gmlp_bwd(proj, da, lg, lb, ws, ws_t, bs_t, A)
    dproj, dbias_t, dsinks = _attn_bwd(proj, db, duv, bias_t, sinks, A, B)
    chip_out = reduce_to_chip(sib_out, "rs_out")
    dtable_t = _bias_bwd(dbias_t.reshape(H, -1), onehot_kq)
    dwin_t = _matmul_tn(dproj, n1, "grad_w_in").reshape(win_g.shape)
    sib_in = _sibling_exchange_begin(dwin_t, "rs_in_sib")
    grad_x, dg1 = _inproj_bwd(dproj, win_t_full, xs, dh1, g1)

    small_w = [rel_bias_table, mix_norm_g, gate_norm_g, gate_norm_b, w_spatial, b_spatial, attn_sinks,
               out_norm_a_g, out_norm_b_g, ffn_norm_g, final_norm_g]
    small_m = [m_rel_bias_table, m_mix_norm_g, m_gate_norm_g, m_gate_norm_b, m_w_spatial, m_b_spatial, m_attn_sinks,
               m_out_norm_a_g, m_out_norm_b_g, m_ffn_norm_g, m_final_norm_g]
    small_v = [v_rel_bias_table, v_mix_norm_g, v_gate_norm_g, v_gate_norm_b, v_w_spatial, v_b_spatial, v_attn_sinks,
               v_out_norm_a_g, v_out_norm_b_g, v_ffn_norm_g, v_final_norm_g]
    small_g = [jnp.transpose(dtable_t), dg1, dlg, dlb, dws, jnp.transpose(dbs_t), dsinks, dga, dgb, dg2, dg3]
    nothing = jnp.zeros((1, H), F32)
    small_w, small_m, small_v = small_w + [nothing], small_m + [nothing], small_v + [nothing]
    small_g = small_g + [jnp.broadcast_to(loss_part, (1, H))]
    shapes = [w.shape for w in small_w]
    totals, places = _small_layout(shapes)
    as_rows = lambda arrays: [a.reshape(_rows2d(a.shape)) for a in arrays]
    big = [None] * 4

    def adam_of(k, state, w, m, v):
        csum, received = _chip_exchange_end(state, "rs_%d_end" % k)
        big[k] = _adam_sharded(csum, received, w, m, v, "adam_%d" % k)

    small_gather = _gather_begin(_pack_small(small_g, totals, places))
    every = range(len(totals))
    _gather_step(small_gather, [(a, 0) for a in every], "small_gather_start")
    part, received = _sibling_exchange_end(sib_in, "rs_in_sib_end")
    csum_in = _chip_sum(part, received, "rs_in_chip_sum")
    _gather_step(small_gather, [(a, 1) for a in every], "small_gather_1")
    chip_in = _chip_exchange_begin(csum_in, "rs_in_chip")
    adam_of(3, chip_down, w_down[0], m_w_down[0], v_w_down[0])
    _gather_step(small_gather, [(a, 2) for a in every], "small_gather_2")
    adam_of(2, chip_up, w_up[0], m_w_up[0], v_w_up[0])
    gathered = _gather_end(small_gather, list(every), "small_gather_end")
    small_out = _adam_small(gathered, totals, places, as_rows(small_w), as_rows(small_m), as_rows(small_v))
    sg, sd, sm, sv = [[outs[k].reshape(s) for outs, s in zip(small_out, shapes)] for k in range(4)]
    adam_of(1, chip_out, w_out[0], m_w_out[0], v_w_out[0])
    adam_of(0, chip_in, win_t, m_win_t, v_win_t)
    big[0] = [jnp.swapaxes(o, 0, 1) for o in big[0]]
    big = [[o.reshape(w.shape) for o in outs] for outs, w in zip(big, (w_in, w_out, w_up, w_down))]

    loss = sg[-1][0, 0]

    order = ["s0", "s1", "b0", "s2", "s3", "s4", "s5", "s6", "s7", "s8", "b1", "s9", "b2", "b3", "s10"]

    def group(idx):
        small = (sg, sd, sm, sv)[idx]
        return [small[int(t[1:])] if t[0] == "s" else big[int(t[1:])][idx] for t in order]

    return (loss, grad_x.reshape(x.shape), *group(0), *group(1), *group(2), *group(3))
```

```python
import functools
import math

import numpy as np
import jax
import jax.numpy as jnp
from jax import lax
from jax.experimental import pallas as pl
from jax.experimental.pallas import tpu as pltpu

F32 = jnp.float32
BF16 = jnp.bfloat16
SDS = jax.ShapeDtypeStruct
MESH = pl.DeviceIdType.MESH

N_DEV = 8
EPS = 1e-5
NEG = -1e30
CHUNK = 128
GROUP_DIM = 128
HEAD_DIM = 64
KV_HEADS = 2
N_BUCKETS = 32
MAX_DISTANCE = 128
ADAM_LR, ADAM_B1, ADAM_B2, ADAM_EPS, ADAM_WD, ADAM_STEP = 0.001, 0.9, 0.999, 1e-08, 0.01, 10
GELU_C0 = math.sqrt(2.0 / math.pi)
GELU_C1 = 0.044715

V7X_VMEM_BYTES = 64 * 1024 * 1024
VMEM_LIMIT = V7X_VMEM_BYTES - 8 * 1024 * 1024
LANE = 128

NN = ((1,), (0,))
NT = ((1,), (1,))
TN = ((0,), (0,))


def _dot(a, b, dims):
    return lax.dot_general(a, b, (dims, ((), ())), preferred_element_type=F32)


def _tile(n, pref, unit=LANE):
    best = None
    for t in range(unit, min(n, pref) + 1, unit):
        if n % t == 0:
            best = t
    return n if best is None else best


def _params(n_grid):
    return pltpu.CompilerParams(dimension_semantics=("arbitrary",) * n_grid, vmem_limit_bytes=VMEM_LIMIT)


def _resident(shape):
    return pl.BlockSpec(shape, lambda i: (0, 0), pipeline_mode=pl.Buffered(1))


def _gelu(x):
    return 0.5 * x * (1.0 + jnp.tanh(GELU_C0 * (x + GELU_C1 * x * x * x)))


def _gelu_and_grad(x):
    x2 = x * x
    t = jnp.tanh(GELU_C0 * x * (1.0 + GELU_C1 * x2))
    val = 0.5 * x * (1.0 + t)
    grad = 0.5 * (1.0 + t) + 0.5 * x * (1.0 - t * t) * (GELU_C0 * (1.0 + 3.0 * GELU_C1 * x2))
    return val, grad


def _rms_stats(x):
    return lax.rsqrt(jnp.mean(x * x, axis=-1, keepdims=True) + EPS)


def _rms_bwd(dy, x, r, g):
    w = dy * g
    return r * w - x * (r * r * r) * jnp.mean(w * x, axis=-1, keepdims=True)


def _t5_bucket():
    i = np.arange(CHUNK)[:, None]
    j = np.arange(2 * CHUNK)[None, :]
    rel = np.maximum(i + CHUNK - j, 0)
    n_exact = N_BUCKETS // 2
    relf = np.maximum(rel, n_exact).astype(np.float32)
    large = n_exact + (np.log(relf / np.float32(n_exact)) / np.float32(math.log(MAX_DISTANCE / n_exact))
                       * np.float32(N_BUCKETS - n_exact)).astype(np.int32)
    large = np.minimum(large, N_BUCKETS - 1)
    bucket = np.where(rel < n_exact, rel, large)
    in_window = (i + CHUNK - j >= 0) & (i + CHUNK - j < CHUNK)
    return bucket.astype(np.int32), in_window


def _split3(x):
    hi = x.astype(BF16)
    r1 = x - hi.astype(F32)
    mid = r1.astype(BF16)
    lo = (r1 - mid.astype(F32)).astype(BF16)
    return hi, mid, lo


HBM_SPEC = pl.BlockSpec(memory_space=pltpu.HBM)


def _mesh_pos():
    return lax.axis_index("x"), lax.axis_index("y"), lax.axis_index("c")


def _dev_index(px, py, pc):
    return 4 * px + 2 * py + pc


SEM_SPEC = pl.BlockSpec(memory_space=pltpu.SEMAPHORE)
ANY_SPEC = pl.BlockSpec(memory_space=pl.ANY)
VMEM_SPEC = pl.BlockSpec(memory_space=pltpu.VMEM)
TOKEN_SPEC = VMEM_SPEC
TOKEN = SDS((8, LANE), F32)
SIDE_EFFECT = pltpu.SideEffectType.DATAFLOW_SIDE_EFFECTING


def _hbm(x):
    return pltpu.with_memory_space_constraint(x, pltpu.HBM)


class _CallChain:
    def __init__(self):
        self.token = None

    def call(self, body, *, in_specs, out_specs, out_shape, **kwargs):
        dep, n_in = self.token, len(in_specs)
        single = not isinstance(out_shape, (list, tuple))
        out_shapes = [out_shape] if single else list(out_shape)
        out_specs = [out_specs] if single else list(out_specs)
        n_out = len(out_shapes)
        n_dep = 0 if dep is None else 1
        token_spec = pl.BlockSpec((8, LANE), lambda *_: (0, 0)) if kwargs.get("grid") else VMEM_SPEC

        def chained(*refs):
            outs_at = n_in + n_dep
            body(*refs[:n_in], *refs[outs_at:outs_at + n_out], *refs[outs_at + n_out + 1:])
            token = refs[outs_at + n_out]
            token[...] = jnp.zeros_like(token)

        inner = pl.pallas_call(chained, in_specs=list(in_specs) + [ANY_SPEC] * n_dep, out_specs=out_specs + [token_spec],
                               out_shape=out_shapes + [TOKEN], **kwargs)

        def run(*operands):
            outs = inner(*operands) if dep is None else inner(*operands, dep)
            self.token = outs[n_out]
            return outs[0] if single else list(outs[:n_out])

        return run


_CHAIN = _CallChain()


def _wait_all(waits, x, y, c):
    for kind, src, dst, send_sem, recv_sem in waits:
        cp = pltpu.make_async_remote_copy(src_ref=src, dst_ref=dst, send_sem=send_sem, recv_sem=recv_sem,
                                          device_id=(x, y, c), device_id_type=MESH)
        if kind == "send":
            cp.wait_send()
        else:
            cp.wait_recv()


def _split_start(bufs, copies_of, n_sems, name, sem_sets=(), waits_of=None):
    n, ns = len(bufs), len(sem_sets)
    flat_sems = [s for pair in sem_sets for s in pair]

    def body(*refs):
        ins = refs[:n]
        sems = refs[n:n + 2 * ns]
        send_sems, recv_sems = refs[n + 2 * ns], refs[n + 2 * ns + 1]
        if waits_of is not None:
            _wait_all(waits_of(ins, [(sems[2 * i], sems[2 * i + 1]) for i in range(ns)]), *_mesh_pos())
        for src, dst, k, target in copies_of(ins):
            pltpu.make_async_remote_copy(src_ref=src, dst_ref=dst, send_sem=send_sems.at[k], recv_sem=recv_sems.at[k],
                                         device_id=target, device_id_type=MESH).start()

    outs = _CHAIN.call(
        body, name=name,
        out_shape=[pltpu.SemaphoreType.DMA((n_sems,)), pltpu.SemaphoreType.DMA((n_sems,))]
        + [pltpu.HBM(b.shape, b.dtype) for b in bufs],
        in_specs=[HBM_SPEC] * n + [SEM_SPEC] * (2 * ns), out_specs=[SEM_SPEC, SEM_SPEC] + [HBM_SPEC] * n,
        input_output_aliases={a: 2 + a for a in range(n)},
        compiler_params=pltpu.CompilerParams(has_side_effects=SIDE_EFFECT),
    )(*[_hbm(b) for b in bufs], *flat_sems)
    return outs[0], outs[1], list(outs[2:2 + n])


def _split_wait(bufs, sem_sets, waits_of, name):
    n, ns = len(bufs), len(sem_sets)
    flat_sems = [s for pair in sem_sets for s in pair]

    def body(*refs):
        ins = refs[:n]
        sems = refs[n:n + 2 * ns]
        _wait_all(waits_of(ins, [(sems[2 * i], sems[2 * i + 1]) for i in range(ns)]), *_mesh_pos())

    outs = _CHAIN.call(
        body, name=name,
        out_shape=[pltpu.HBM(b.shape, b.dtype) for b in bufs],
        in_specs=[HBM_SPEC] * n + [SEM_SPEC] * (2 * ns), out_specs=[HBM_SPEC] * n,
        input_output_aliases={a: a for a in range(n)},
        compiler_params=pltpu.CompilerParams(has_side_effects=SIDE_EFFECT),
    )(*bufs, *flat_sems)
    return list(outs)


def _gather_blocks(land):
    rows = land.shape[1]
    first = (rows // 2) // 16 * 16

    def block(px, py, pc):
        return land.at[_dev_index(px, py, pc)]

    def halves(px, py, pc):
        return (land.at[_dev_index(px, py, pc), pl.ds(0, first)], land.at[_dev_index(px, py, pc), pl.ds(first, rows - first)])

    return block, halves


def _gather_begin(shards):
    me = _dev_index(*_mesh_pos())
    lands = [lax.dynamic_update_index_in_dim(lax.empty((N_DEV,) + s.shape, s.dtype), s, me, 0) for s in shards]
    return dict(lands=lands, stage={})


STAGE_COPIES = (3, 4, 1)


def _gather_step(state, items, name):
    which = sorted({a for a, _ in items})
    at = {a: i for i, a in enumerate(which)}
    sem_sets = [state["stage"][(a, s - 1)][0] for a, s in items if s > 0]
    offset, n_sems = {}, 0
    for a, s in items:
        offset[(a, s)] = n_sems
        n_sems += STAGE_COPIES[s]

    def waits_of(ins, sems):
        x, y, c = _mesh_pos()
        out, earlier = [], 0
        for a, s in items:
            if s == 0:
                continue
            block, halves = _gather_blocks(ins[at[a]])
            send, recv = sems[earlier]
            off = state["stage"][(a, s - 1)][1]
            earlier += 1
            if s == 1:
                arrived = [(1, block(1 - x, y, c)), (2, block(x, 1 - y, c))]
            else:
                arrived = list(zip((2, 3), halves(1 - x, 1 - y, c)))
            out += [("recv", ref, ref, send.at[off + k], recv.at[off + k]) for k, ref in arrived]
        return out

    def copies_of(ins):
        x, y, c = _mesh_pos()
        sibling = (x, y, 1 - c)
        out = []
        for a, s in items:
            block, halves = _gather_blocks(ins[at[a]])
            off = offset[(a, s)]
            if s == 0:
                mine = block(x, y, c)
                out += [(mine, mine, off + 1, (1 - x, y, c)), (mine, mine, off + 2, (x, 1 - y, c)), (mine, mine, off, sibling)]
            elif s == 1:
                from_x, from_y = block(1 - x, y, c), block(x, 1 - y, c)
                out += [(halves(1 - x, y, c)[0], halves(1 - x, y, c)[0], off + 2, (x, 1 - y, c)),
                        (halves(x, 1 - y, c)[1], halves(x, 1 - y, c)[1], off + 3, (1 - x, y, c)),
                        (from_x, from_x, off, sibling), (from_y, from_y, off + 1, sibling)]
            else:
                diag = block(1 - x, 1 - y, c)
                out.append((diag, diag, off, sibling))
        return out

    send_sems, recv_sems, bufs = _split_start([state["lands"][a] for a in which], copies_of, n_sems, name,
                                              sem_sets=sem_sets, waits_of=waits_of)
    for a in which:
        state["lands"][a] = bufs[at[a]]
    for a, s in items:
        state["stage"][(a, s)] = ((send_sems, recv_sems), offset[(a, s)])


def _gather_end(state, which, name):
    sem_sets = [state["stage"][(a, s)][0] for a in which for s in range(3)]

    def waits(ins, sems):
        x, y, c = _mesh_pos()
        out = []
        for i, a in enumerate(which):
            block, halves = _gather_blocks(ins[i])
            (b_send, b_recv), (s1_send, s1_recv), (s2_send, s2_recv) = sems[3 * i:3 * i + 3]
            o0, o1, o2 = (state["stage"][(a, s)][1] for s in range(3))
            arrivals = [(block(x, y, 1 - c), b_send, b_recv, o0),
                        (block(1 - x, y, 1 - c), s1_send, s1_recv, o1), (block(x, 1 - y, 1 - c), s1_send, s1_recv, o1 + 1),
                        (block(1 - x, 1 - y, 1 - c), s2_send, s2_recv, o2)]
            mine = block(x, y, c)
            sent = [(mine, b_send, b_recv, o0 + k) for k in range(3)]
            sent += [(block(1 - x, y, c), s1_send, s1_recv, o1), (block(x, 1 - y, c), s1_send, s1_recv, o1 + 1),
                     (halves(1 - x, y, c)[0], s1_send, s1_recv, o1 + 2), (halves(x, 1 - y, c)[1], s1_send, s1_recv, o1 + 3),
                     (block(1 - x, 1 - y, c), s2_send, s2_recv, o2)]
            out += [("recv", ref, ref, s.at[k], r.at[k]) for ref, s, r, k in arrivals]
            out += [("send", ref, ref, s.at[k], r.at[k]) for ref, s, r, k in sent]
        return out

    bufs = _split_wait([state["lands"][a] for a in which], sem_sets, waits, name)
    for i, a in enumerate(which):
        state["lands"][a] = bufs[i]
    return bufs


def _sibling_exchange_begin(parts, name):
    lands = [lax.empty((4,) + p.shape[1:], p.dtype) for p in parts]
    n = len(parts)

    def copies_of(ins):
        x, y, c = _mesh_pos()
        return [(ins[a].at[2 * j + (1 - c)], ins[n + a].at[j], 4 * a + j, (x, y, 1 - c)) for a in range(n) for j in range(4)]

    send_sems, recv_sems, bufs = _split_start(list(parts) + lands, copies_of, 4 * n, name)
    return dict(bufs=bufs, sems=(send_sems, recv_sems), n=n)


def _sibling_exchange_end(state, name):
    n = state["n"]

    def waits(ins, sems):
        _, _, c = _mesh_pos()
        return [(kind, ins[a].at[2 * j + (1 - c)], ins[n + a].at[j], sems[0][0].at[4 * a + j], sems[0][1].at[4 * a + j])
                for a in range(n) for j in range(4) for kind in ("send", "recv")]

    bufs = _split_wait(state["bufs"], [state["sems"]], waits, name)
    return [(bufs[a], bufs[n + a]) for a in range(n)]


CHIP_FLIPS = (2, 1, 3)


def _chip_exchange_begin(csums, name):
    lands = [lax.empty((3,) + s.shape[1:], s.dtype) for s in csums]
    n = len(csums)

    def copies_of(ins):
        x, y, c = _mesh_pos()
        chips = [(1 - x, y), (x, 1 - y), (1 - x, 1 - y)]
        return [(ins[a].at[CHIP_FLIPS[r]], ins[n + a].at[r], 3 * a + r, (px, py, c))
                for a in range(n) for r, (px, py) in enumerate(chips)]

    send_sems, recv_sems, bufs = _split_start(list(csums) + lands, copies_of, 3 * n, name)
    return dict(bufs=bufs, sems=(send_sems, recv_sems), n=n)


def _chip_exchange_end(state, a, name):
    n = state["n"]

    def waits(ins, sems):
        return [(kind, ins[0].at[CHIP_FLIPS[r]], ins[1].at[r], sems[0][0].at[3 * a + r], sems[0][1].at[3 * a + r])
                for r in range(3) for kind in ("send", "recv")]

    csum, received = _split_wait([state["bufs"][a], state["bufs"][n + a]], [state["sems"]], waits, name)
    return csum, received


def _chip_sum(part, recv, name):
    _, R, C = part.shape
    tr = _tile(R, 256, 16)
    place = jnp.stack([lax.axis_index("c"), 2 * lax.axis_index("x") + lax.axis_index("y")]).astype(jnp.int32)

    def body(place_ref, p_ref, r_ref, o_ref):
        o_ref[...] = (p_ref[...].astype(F32) + r_ref[...].astype(F32)).astype(o_ref.dtype)

    def chip(p, place_ref):
        return jnp.bitwise_xor(p, place_ref[1])

    grid_spec = pltpu.PrefetchScalarGridSpec(
        num_scalar_prefetch=1, grid=(4, R // tr),
        in_specs=[pl.BlockSpec((None, tr, C), lambda p, i, place_ref: (2 * chip(p, place_ref) + place_ref[0], i, 0)),
                  pl.BlockSpec((None, tr, C), lambda p, i, place_ref: (chip(p, place_ref), i, 0))],
        out_specs=pl.BlockSpec((None, tr, C), lambda p, i, place_ref: (p, i, 0)))
    return pl.pallas_call(body, name=name, grid_spec=grid_spec, out_shape=SDS((4, R, C), part.dtype),
                          compiler_params=_params(2))(place, part, recv)


def _bias_fwd(table_t, onehot_t, onehot_kq_t):
    H = table_t.shape[0]
    n = onehot_t.shape[1]

    def body(t_ref, oh_ref, oh_kq_ref, o_ref, o_kq_ref):
        hi, mid, lo = _split3(t_ref[...])
        for src, dst in ((oh_ref, o_ref), (oh_kq_ref, o_kq_ref)):
            oh = src[...]
            dst[...] = _dot(hi, oh, NN) + _dot(mid, oh, NN) + _dot(lo, oh, NN)

    return _CHAIN.call(body, name="bias_fwd", in_specs=[VMEM_SPEC] * 3, out_specs=[VMEM_SPEC] * 2,
                       out_shape=[SDS((H, n), F32)] * 2, compiler_params=_params(0))(table_t, onehot_t, onehot_kq_t)


def _mix_norm(x, g):
    T, D = x.shape
    tm = _tile(T, 512)

    def body(x_ref, g_ref, n_ref):
        xv = x_ref[...]
        n_ref[...] = (xv * _rms_stats(xv) * g_ref[...]).astype(BF16)

    row = pl.BlockSpec((tm, D), lambda i: (i, 0))
    return _CHAIN.call(body, name="mix_norm", grid=(T // tm,), in_specs=[row, pl.BlockSpec((1, D), lambda i: (0, 0))],
                       out_specs=row, out_shape=SDS((T, D), BF16), compiler_params=_params(1))(x, g)


def _inproj_fwd(n, w_t):
    T, D = n.shape
    P = w_t.shape[0]
    tm = _tile(T, 512)

    def body(n_ref, w_ref, proj_ref):
        proj_ref[...] = _dot(n_ref[...], w_ref[...], NT)

    return _CHAIN.call(
        body, name="inproj_fwd", grid=(T // tm,),
        in_specs=[pl.BlockSpec((tm, D), lambda i: (i, 0)), _resident((P, D))],
        out_specs=pl.BlockSpec((tm, P), lambda i: (i, 0)),
        out_shape=SDS((T, P), F32), compiler_params=_params(1))(n, w_t)


def _layer_norm_group(vg, lg, lb):
    mu = jnp.mean(vg, axis=-1, keepdims=True)
    xc = vg - mu
    rstd = lax.rsqrt(jnp.mean(xc * xc, axis=-1, keepdims=True) + EPS)
    vhat = xc * rstd
    return vhat, rstd, vhat * lg + lb


def _gmlp_fwd(proj, lg, lb, w_s, bs_t, A):
    T = proj.shape[0]
    G = A // GROUP_DIM
    tm = _tile(T, 512)
    nc = tm // CHUNK

    def body(u_ref, v_ref, lg_ref, lb_ref, w_ref, bst_ref, a_ref):
        row = lax.broadcasted_iota(jnp.int32, (CHUNK, CHUNK), 0)
        col = lax.broadcasted_iota(jnp.int32, (CHUNK, CHUNK), 1)
        causal = row >= col
        for g in range(G):
            sl = slice(g * GROUP_DIM, (g + 1) * GROUP_DIM)
            _, _, vn = _layer_norm_group(_gelu(v_ref[:, sl]), lg_ref[:, sl], lb_ref[:, sl])
            vnb = vn.astype(BF16)
            wm = jnp.where(causal, w_ref[g], 0.0).astype(BF16)
            ug = _gelu(u_ref[:, sl])
            bcol = bst_ref[:, g:g + 1]
            for c in range(nc):
                rs = slice(c * CHUNK, (c + 1) * CHUNK)
                a_ref[rs, sl] = ug[rs] * (_dot(wm, vnb[rs], NN) + bcol)

    return _CHAIN.call(
        body, name="gmlp_fwd", grid=(T // tm,),
        in_specs=[pl.BlockSpec((tm, A), lambda i: (i, 0)), pl.BlockSpec((tm, A), lambda i: (i, 1)),
                  pl.BlockSpec((1, A), lambda i: (0, 0)), pl.BlockSpec((1, A), lambda i: (0, 0)),
                  pl.BlockSpec((G, CHUNK, CHUNK), lambda i: (0, 0, 0)), pl.BlockSpec((CHUNK, G), lambda i: (0, 0))],
        out_specs=pl.BlockSpec((tm, A), lambda i: (i, 0)),
        out_shape=SDS((T, A), F32), compiler_params=_params(1))(proj, proj, lg, lb, w_s, bs_t)


def _attn_masks(first_tile):
    ii = lax.broadcasted_iota(jnp.int32, (CHUNK, 2 * CHUNK), 0)
    jj = lax.broadcasted_iota(jnp.int32, (CHUNK, 2 * CHUNK), 1)
    in_window = (jj > ii) & (jj <= ii + CHUNK)
    first_mask = in_window & jnp.logical_or(jnp.logical_not(first_tile), jj >= CHUNK)
    return in_window, first_mask


def _softmax_with_sink(s, sink, axis):
    m = jnp.maximum(jnp.max(s, axis=axis, keepdims=True), sink)
    p = jnp.exp(s - m)
    e_sink = jnp.exp(sink - m)
    inv = 1.0 / (jnp.sum(p, axis=axis, keepdims=True) + e_sink)
    return p * inv, e_sink * inv


def _pad_heads(band, group):
    lane = lax.broadcasted_iota(jnp.int32, band.shape, 1)
    if group == 0:
        low = jnp.where(lane < HEAD_DIM, band, 0.0)
        high = pltpu.roll(low, HEAD_DIM, 1)
    else:
        high = jnp.where(lane >= HEAD_DIM, band, 0.0)
        low = pltpu.roll(high, HEAD_DIM, 1)
    return low.astype(BF16), high.astype(BF16)


def _attn_specs(tq, A, B, reverse_tiles=None):
    nb = tq // CHUNK
    kcol = (2 * A + B) // LANE
    if reverse_tiles is None:
        tile = lambda i: i
    else:
        tile = lambda i: reverse_tiles - 1 - i
    prev = lambda i: jnp.maximum(tile(i) * nb - 1, 0)
    return [pl.BlockSpec((tq, B), lambda i: (tile(i), 2 * A // B)),
            pl.BlockSpec((tq, LANE), lambda i: (tile(i), kcol)),
            pl.BlockSpec((tq, LANE), lambda i: (tile(i), kcol + 1)),
            pl.BlockSpec((CHUNK, LANE), lambda i: (prev(i), kcol)),
            pl.BlockSpec((CHUNK, LANE), lambda i: (prev(i), kcol + 1))]


def _attn_fwd(proj, bias, sinks, A, B):
    T = proj.shape[0]
    H = B // HEAD_DIM
    qpk = H // KV_HEADS
    tq = _tile(T, 512)
    nb = tq // CHUNK

    scale = HEAD_DIM ** -0.5

    def body(sink_ref, q_ref, k_ref, v_ref, kp_ref, vp_ref, bias_ref, o_ref):
        in_window, first_mask = _attn_masks(pl.program_id(0) == 0)
        for b in range(nb):
            rows = slice(b * CHUNK, (b + 1) * CHUNK)
            if b == 0:
                kprev, vprev, mask = kp_ref[...], vp_ref[...], first_mask
            else:
                prows = slice((b - 1) * CHUNK, b * CHUNK)
                kprev, vprev, mask = k_ref[prows, :], v_ref[prows, :], in_window
            kband = jnp.concatenate([kprev, k_ref[rows, :]], axis=0)
            vband = jnp.concatenate([vprev, v_ref[rows, :]], axis=0)
            k_pads = [_pad_heads(kband, g) for g in range(KV_HEADS)]
            v_both = [jnp.concatenate(_pad_heads(vband, g), axis=0) for g in range(KV_HEADS)]
            scores = []
            for pair in range(H // 2):
                h = 2 * pair
                qs = (q_ref[rows, h * HEAD_DIM:(h + 2) * HEAD_DIM] * scale).astype(BF16)
                scores += [_dot(qs, kz, NT) for kz in k_pads[h // qpk]]
            probs = [_softmax_with_sink(jnp.where(mask, s + bias_ref[h], NEG), sink_ref[h], -1)[0].astype(BF16)
                     for h, s in enumerate(scores)]
            outs = [_dot(jnp.concatenate(probs[h:h + 2], axis=1), v_both[h // qpk], NN) for h in range(0, H, 2)]
            o_ref[rows, :] = jnp.concatenate(outs, axis=1)

    return _CHAIN.call(
        body, name="attn_fwd", grid=(T // tq,),
        in_specs=[pl.BlockSpec(memory_space=pltpu.SMEM)] + _attn_specs(tq, A, B)
        + [pl.BlockSpec((H, CHUNK, 2 * CHUNK), lambda i: (0, 0, 0))],
        out_specs=pl.BlockSpec((tq, B), lambda i: (i, 0)),
        out_shape=SDS((T, B), F32), compiler_params=_params(1))(sinks, proj, proj, proj, proj, proj, bias)


def _outproj_fwd(a, b, ga, gb, x, w, g_ffn):
    T, A = a.shape
    B = b.shape[1]
    D = x.shape[1]
    tm = _tile(T, 512)

    def body(a_ref, b_ref, ga_ref, gb_ref, x_ref, w_ref, gf_ref, h_ref, mix_ref, n_ref):
        av, bv = a_ref[...], b_ref[...]
        mix_ref[:, :A] = (av * _rms_stats(av) * ga_ref[...]).astype(BF16)
        mix_ref[:, A:] = (bv * _rms_stats(bv) * gb_ref[...]).astype(BF16)
        hv = x_ref[...] + _dot(mix_ref[...], w_ref[...], NN)
        h_ref[...] = hv
        n_ref[...] = (hv * _rms_stats(hv) * gf_ref[...]).astype(BF16)

    row = pl.BlockSpec((tm, D), lambda i: (i, 0))
    return _CHAIN.call(
        body, name="outproj_fwd", grid=(T // tm,),
        in_specs=[pl.BlockSpec((tm, A), lambda i: (i, 0)), pl.BlockSpec((tm, B), lambda i: (i, 0)),
                  pl.BlockSpec((1, A), lambda i: (0, 0)), pl.BlockSpec((1, B), lambda i: (0, 0)),
                  row, _resident((A + B, D)), pl.BlockSpec((1, D), lambda i: (0, 0))],
        out_specs=[row, pl.BlockSpec((tm, A + B), lambda i: (i, 0)), row],
        out_shape=[SDS((T, D), F32), SDS((T, A + B), BF16), SDS((T, D), BF16)],
        compiler_params=_params(1))(a, b, ga, gb, x, w, g_ffn)


def _ffn_up(n, w_up):
    T, D = n.shape
    Fb = w_up.shape[2]
    F = N_DEV * Fb
    tm, tf = _tile(T, 1024), _tile(Fb, 1024)
    per = Fb // tf

    def body(n_ref, wu_ref, z_ref):
        z_ref[...] = jnp.maximum(_dot(n_ref[...], wu_ref[...], NN), 0.0).astype(BF16)

    return _CHAIN.call(
        body, name="ffn_up", grid=(T // tm, F // tf),
        in_specs=[pl.BlockSpec((tm, D), lambda i, j: (i, 0)),
                  pl.BlockSpec((None, D, tf), lambda i, j: (j // per, 0, j % per))],
        out_specs=pl.BlockSpec((tm, tf), lambda i, j: (i, j)),
        out_shape=SDS((T, F), BF16), compiler_params=_params(2))(n, w_up)


def _ffn_down(h1, z, w_down):
    T, D = h1.shape
    F = w_down.shape[0]
    tm, tn, tk = _tile(T, 1024), _tile(D, 1024), _tile(F, 4096)

    def body(h_ref, z_ref, wd_ref, h2_ref):
        k = pl.program_id(2)

        @pl.when(k == 0)
        def _():
            h2_ref[...] = h_ref[...]

        zf = z_ref[...].astype(F32)
        h2_ref[...] += _dot((zf * zf).astype(BF16), wd_ref[...], NN)

    return _CHAIN.call(
        body, name="ffn_down", grid=(T // tm, D // tn, F // tk),
        in_specs=[pl.BlockSpec((tm, tn), lambda i, j, k: (i, j)), pl.BlockSpec((tm, tk), lambda i, j, k: (i, k)),
                  pl.BlockSpec((tk, tn), lambda i, j, k: (k, j))],
        out_specs=pl.BlockSpec((tm, tn), lambda i, j, k: (i, j)),
        out_shape=SDS((T, D), F32), compiler_params=_params(3))(h1, z, w_down)


def _final_loss(h2, g, target):
    T, D = h2.shape
    tm = _tile(T, 512)

    def body(h_ref, g_ref, t_ref, loss_ref, dg_ref, dh_ref, dhb_ref):
        @pl.when(pl.program_id(0) == 0)
        def _():
            loss_ref[...] = jnp.zeros_like(loss_ref)
            dg_ref[...] = jnp.zeros_like(dg_ref)

        hv, gv = h_ref[...], g_ref[...]
        r = _rms_stats(hv)
        hn = hv * r
        e = hn * gv - t_ref[...]
        loss_ref[...] += (0.5 / D) * jnp.sum(jnp.sum(e * e, axis=0, keepdims=True), axis=-1, keepdims=True)
        dy = e * (1.0 / D)
        dg_ref[...] += jnp.sum(dy * hn, axis=0, keepdims=True)
        dh = _rms_bwd(dy, hv, r, gv)
        dh_ref[...] = dh
        dhb_ref[...] = dh.astype(BF16)

    return _CHAIN.call(
        body, name="final_loss", grid=(T // tm,),
        in_specs=[pl.BlockSpec((tm, D), lambda i: (i, 0)), pl.BlockSpec((1, D), lambda i: (0, 0)),
                  pl.BlockSpec((tm, D), lambda i: (i, 0))],
        out_specs=[pl.BlockSpec((1, 1), lambda i: (0, 0)), pl.BlockSpec((1, D), lambda i: (0, 0)),
                   pl.BlockSpec((tm, D), lambda i: (i, 0)), pl.BlockSpec((tm, D), lambda i: (i, 0))],
        out_shape=[SDS((1, 1), F32), SDS((1, D), F32), SDS((T, D), F32), SDS((T, D), BF16)],
        compiler_params=_params(1))(h2, g, target)


def _ffn_down_bwd(dh2b, z, w_down):
    T, D = dh2b.shape
    F = w_down.shape[0]
    tm, tf = _tile(T, 1024), _tile(F, 1024)

    def body(dh_ref, z_ref, wd_ref, dzp_ref):
        dzz = _dot(dh_ref[...], wd_ref[...], NT)
        dzp_ref[...] = (dzz * (2.0 * z_ref[...].astype(F32))).astype(BF16)

    return _CHAIN.call(
        body, name="ffn_down_bwd", grid=(T // tm, F // tf),
        in_specs=[pl.BlockSpec((tm, D), lambda i, j: (i, 0)), pl.BlockSpec((tm, tf), lambda i, j: (i, j)),
                  pl.BlockSpec((tf, D), lambda i, j: (j, 0))],
        out_specs=pl.BlockSpec((tm, tf), lambda i, j: (i, j)),
        out_shape=SDS((T, F), BF16), compiler_params=_params(2))(dh2b, z, w_down)


def _ffn_up_bwd(dzp, w_up_t):
    T, F = dzp.shape
    D = w_up_t.shape[1]
    tm, tn, tk = _tile(T, 1024), _tile(D, 1024), _tile(F, 4096)

    def body(dzp_ref, w_ref, dn_ref):
        part = _dot(dzp_ref[...], w_ref[...], NN)

        @pl.when(pl.program_id(2) == 0)
        def _():
            dn_ref[...] = part

        @pl.when(pl.program_id(2) > 0)
        def _():
            dn_ref[...] += part

    return _CHAIN.call(
        body, name="ffn_up_bwd", grid=(T // tm, D // tn, F // tk),
        in_specs=[pl.BlockSpec((tm, tk), lambda i, j, k: (i, k)), pl.BlockSpec((tk, tn), lambda i, j, k: (k, j))],
        out_specs=pl.BlockSpec((tm, tn), lambda i, j, k: (i, j)),
        out_shape=SDS((T, D), F32), compiler_params=_params(3))(dzp, w_up_t)


def _ffn_norm_bwd(dn, dh2, h1, g):
    T, D = h1.shape
    tm = _tile(T, 256)

    def body(dn_ref, dh_ref, h_ref, g_ref, dh1_ref, dh1b_ref, dg_ref):
        @pl.when(pl.program_id(0) == 0)
        def _():
            dg_ref[...] = jnp.zeros_like(dg_ref)

        hv, dnv = h_ref[...], dn_ref[...]
        r = _rms_stats(hv)
        dg_ref[...] += jnp.sum(dnv * (hv * r), axis=0, keepdims=True)
        dh1 = dh_ref[...] + _rms_bwd(dnv, hv, r, g_ref[...])
        dh1_ref[...] = dh1
        dh1b_ref[...] = dh1.astype(BF16)

    row = pl.BlockSpec((tm, D), lambda i: (i, 0))
    vec = pl.BlockSpec((1, D), lambda i: (0, 0))
    return _CHAIN.call(
        body, name="ffn_norm_bwd", grid=(T // tm,), in_specs=[row, row, row, vec], out_specs=[row, row, vec],
        out_shape=[SDS((T, D), F32), SDS((T, D), BF16), SDS((1, D), F32)], compiler_params=_params(1))(dn, dh2, h1, g)


def _matmul_tn(a, b, name, square_a=False, col_blocks=None):
    T, K = a.shape
    N = b.shape[1]
    tk = _tile(K, 1792)
    tn = _tile(N if col_blocks is None else N // col_blocks, 1024 if tk <= 1024 else 512)

    def body(a_ref, b_ref, o_ref):
        av = a_ref[...]
        if square_a:
            af = av.astype(F32)
            av = (af * af).astype(BF16)
        o_ref[...] = _dot(av, b_ref[...], TN).astype(o_ref.dtype)

    if col_blocks is None:
        out_shape = SDS((K, N), BF16)
        out_spec = pl.BlockSpec((tk, tn), lambda i, j: (i, j))
    else:
        per = (N // col_blocks) // tn
        out_shape = SDS((col_blocks, K, N // col_blocks), BF16)
        out_spec = pl.BlockSpec((None, tk, tn), lambda i, j: (j // per, i, j % per))
    return _CHAIN.call(
        body, name=name, grid=(K // tk, N // tn),
        in_specs=[pl.BlockSpec((T, tk), lambda i, j: (0, i)), pl.BlockSpec((T, tn), lambda i, j: (0, j))],
        out_specs=out_spec, out_shape=out_shape, compiler_params=_params(2))(a, b)


def _outproj_bwd(dh1b, w, a, b, ga, gb):
    T, D = dh1b.shape
    A, B = a.shape[1], b.shape[1]
    tm = _tile(T, 512)

    def body(dh_ref, w_ref, a_ref, b_ref, ga_ref, gb_ref, da_ref, db_ref, dga_ref, dgb_ref):
        @pl.when(pl.program_id(0) == 0)
        def _():
            dga_ref[...] = jnp.zeros_like(dga_ref)
            dgb_ref[...] = jnp.zeros_like(dgb_ref)

        dmix = _dot(dh_ref[...], w_ref[...], NT)
        for src_ref, g_ref, dx_ref, dg_ref, dn in ((a_ref, ga_ref, da_ref, dga_ref, dmix[:, :A]),
                                                   (b_ref, gb_ref, db_ref, dgb_ref, dmix[:, A:])):
            xv = src_ref[...]
            r = _rms_stats(xv)
            dg_ref[...] += jnp.sum(dn * (xv * r), axis=0, keepdims=True)
            dx_ref[...] = _rms_bwd(dn, xv, r, g_ref[...])

    return _CHAIN.call(
        body, name="outproj_bwd", grid=(T // tm,),
        in_specs=[pl.BlockSpec((tm, D), lambda i: (i, 0)), _resident((A + B, D)),
                  pl.BlockSpec((tm, A), lambda i: (i, 0)), pl.BlockSpec((tm, B), lambda i: (i, 0)),
                  pl.BlockSpec((1, A), lambda i: (0, 0)), pl.BlockSpec((1, B), lambda i: (0, 0))],
        out_specs=[pl.BlockSpec((tm, A), lambda i: (i, 0)), pl.BlockSpec((tm, B), lambda i: (i, 0)),
                   pl.BlockSpec((1, A), lambda i: (0, 0)), pl.BlockSpec((1, B), lambda i: (0, 0))],
        out_shape=[SDS((T, A), F32), SDS((T, B), F32), SDS((1, A), F32), SDS((1, B), F32)],
        compiler_params=_params(1))(dh1b, w, a, b, ga, gb)


def _gmlp_bwd(proj, da, lg, lb, w_s, w_st, bs_t, A):
    T = proj.shape[0]
    G = A // GROUP_DIM
    tm = _tile(T, 512)
    nc = tm // CHUNK

    def body(u_ref, v_ref, da_ref, lg_ref, lb_ref, w_ref, wt_ref, bst_ref, duv_ref, dlg_ref, dlb_ref, dw_ref, dbs_ref):
        @pl.when(pl.program_id(0) == 0)
        def _():
            dlg_ref[...] = jnp.zeros_like(dlg_ref)
            dlb_ref[...] = jnp.zeros_like(dlb_ref)
            dw_ref[...] = jnp.zeros_like(dw_ref)
            dbs_ref[...] = jnp.zeros_like(dbs_ref)

        row = lax.broadcasted_iota(jnp.int32, (CHUNK, CHUNK), 0)
        col = lax.broadcasted_iota(jnp.int32, (CHUNK, CHUNK), 1)
        lower = row >= col
        upper = row <= col
        for g in range(G):
            sl = slice(g * GROUP_DIM, (g + 1) * GROUP_DIM)
            lgv = lg_ref[:, sl]
            vg, vg_grad = _gelu_and_grad(v_ref[:, sl])
            vhat, rstd, vn = _layer_norm_group(vg, lgv, lb_ref[:, sl])
            vnb = vn.astype(BF16)
            ug, ug_grad = _gelu_and_grad(u_ref[:, sl])
            dav = da_ref[:, sl]
            wm = jnp.where(lower, w_ref[g], 0.0).astype(BF16)
            wmt = jnp.where(upper, wt_ref[g], 0.0).astype(BF16)
            bcol = bst_ref[:, g:g + 1]
            dw_acc = jnp.zeros((CHUNK, CHUNK), F32)
            dbs_acc = jnp.zeros((CHUNK, 1), F32)
            dvn_parts = []
            dug_parts = []
            for c in range(nc):
                rs = slice(c * CHUNK, (c + 1) * CHUNK)
                mixed = _dot(wm, vnb[rs], NN) + bcol
                dug_parts.append(dav[rs] * mixed)
                dmix = dav[rs] * ug[rs]
                dbs_acc = dbs_acc + jnp.sum(dmix, axis=-1, keepdims=True)
                dmixb = dmix.astype(BF16)
                dw_acc = dw_acc + _dot(dmixb, vnb[rs], NT)
                dvn_parts.append(_dot(wmt, dmixb, NN))
            dvn = jnp.concatenate(dvn_parts, axis=0)
            dug = jnp.concatenate(dug_parts, axis=0)
            dw_ref[g] += jnp.where(lower, dw_acc, 0.0)
            dbs_ref[:, g:g + 1] += dbs_acc
            dlg_ref[:, sl] += jnp.sum(dvn * vhat, axis=0, keepdims=True)
            dlb_ref[:, sl] += jnp.sum(dvn, axis=0, keepdims=True)
            dvhat = dvn * lgv
            dvg = rstd * (dvhat - jnp.mean(dvhat, axis=-1, keepdims=True)
                          - vhat * jnp.mean(dvhat * vhat, axis=-1, keepdims=True))
            duv_ref[:, sl] = (dug * ug_grad).astype(BF16)
            duv_ref[:, A + g * GROUP_DIM:A + (g + 1) * GROUP_DIM] = (dvg * vg_grad).astype(BF16)

    return _CHAIN.call(
        body, name="gmlp_bwd", grid=(T // tm,),
        in_specs=[pl.BlockSpec((tm, A), lambda i: (i, 0)), pl.BlockSpec((tm, A), lambda i: (i, 1)),
                  pl.BlockSpec((tm, A), lambda i: (i, 0)),
                  pl.BlockSpec((1, A), lambda i: (0, 0)), pl.BlockSpec((1, A), lambda i: (0, 0)),
                  pl.BlockSpec((G, CHUNK, CHUNK), lambda i: (0, 0, 0)),
                  pl.BlockSpec((G, CHUNK, CHUNK), lambda i: (0, 0, 0)), pl.BlockSpec((CHUNK, G), lambda i: (0, 0))],
        out_specs=[pl.BlockSpec((tm, 2 * A), lambda i: (i, 0)),
                   pl.BlockSpec((1, A), lambda i: (0, 0)), pl.BlockSpec((1, A), lambda i: (0, 0)),
                   pl.BlockSpec((G, CHUNK, CHUNK), lambda i: (0, 0, 0)), pl.BlockSpec((CHUNK, G), lambda i: (0, 0))],
        out_shape=[SDS((T, 2 * A), BF16), SDS((1, A), F32), SDS((1, A), F32),
                   SDS((G, CHUNK, CHUNK), F32), SDS((CHUNK, G), F32)],
        compiler_params=_params(1))(proj, proj, da, lg, lb, w_s, w_st, bs_t)


def _attn_bwd(proj, do, duv, bias_t, sinks, A, B):
    T, P = proj.shape
    H = B // HEAD_DIM
    qpk = H // KV_HEADS
    tq = _tile(T, 512)
    nb = tq // CHUNK
    n_tiles = T // tq
    scale = HEAD_DIM ** -0.5
    rev = lambda i: n_tiles - 1 - i

    def body(sink_ref, q_ref, k_ref, v_ref, kp_ref, vp_ref, do_ref, duv_ref, bias_ref,
             dproj_ref, dbias_ref, dsink_ref, carry, dkv, sacc):
        step = pl.program_id(0)

        @pl.when(step == 0)
        def _():
            carry[...] = jnp.zeros_like(carry)
            sacc[...] = jnp.zeros_like(sacc)
            dbias_ref[...] = jnp.zeros_like(dbias_ref)

        jj = lax.broadcasted_iota(jnp.int32, (2 * CHUNK, CHUNK), 0)
        ii = lax.broadcasted_iota(jnp.int32, (2 * CHUNK, CHUNK), 1)
        in_window = (jj > ii) & (jj <= ii + CHUNK)
        first_mask = in_window & jnp.logical_or(step != n_tiles - 1, jj >= CHUNK)
        low_query = lax.broadcasted_iota(jnp.int32, (CHUNK, LANE), 1) < HEAD_DIM
        low_key = lax.broadcasted_iota(jnp.int32, (2 * CHUNK, LANE), 1) < HEAD_DIM

        def split_pair(pair_bf16):
            zero = jnp.zeros_like(pair_bf16)
            return jnp.concatenate([jnp.where(low_query, pair_bf16, zero), jnp.where(low_query, zero, pair_bf16)], axis=0)

        dproj_ref[:, :2 * A] = duv_ref[...]
        dkv[...] = jnp.zeros_like(dkv)
        for b in range(nb):
            rows = slice(b * CHUNK, (b + 1) * CHUNK)
            band = slice(b * CHUNK, (b + 2) * CHUNK)
            if b == 0:
                kprev, vprev, mask = kp_ref[...], vp_ref[...], first_mask
            else:
                prows = slice((b - 1) * CHUNK, b * CHUNK)
                kprev, vprev, mask = k_ref[prows, :], v_ref[prows, :], in_window
            kband = jnp.concatenate([kprev, k_ref[rows, :]], axis=0)
            vband = jnp.concatenate([vprev, v_ref[rows, :]], axis=0)
            k_pads = [_pad_heads(kband, g) for g in range(KV_HEADS)]
            v_pads = [_pad_heads(vband, g) for g in range(KV_HEADS)]
            queries, douts, scores, dprobs = [], [], [], []
            for pair in range(H // 2):
                cols = slice(2 * pair * HEAD_DIM, (2 * pair + 2) * HEAD_DIM)
                qs = (q_ref[rows, cols] * scale).astype(BF16)
                dob = do_ref[rows, cols].astype(BF16)
                queries.append(qs)
                douts.append(dob)
                scores += [_dot(kz, qs, NT) for kz in k_pads[2 * pair // qpk]]
                dprobs += [_dot(vz, dob, NT) for vz in v_pads[2 * pair // qpk]]
            probs, dscores = [], []
            for h in range(H):
                pt, p_sink = _softmax_with_sink(jnp.where(mask, scores[h] + bias_ref[h], NEG), sink_ref[h], 0)
                delta = jnp.sum(pt * dprobs[h], axis=0, keepdims=True)
                dst = pt * (dprobs[h] - delta)
                dbias_ref[h] += dst
                sacc[h:h + 1, :] += -(p_sink * delta)
                probs.append(pt.astype(BF16))
                dscores.append(dst.astype(BF16))
            dq_parts, dk_groups, dv_groups = [], [], []
            for g in range(KV_HEADS):
                k_both = jnp.concatenate(k_pads[g], axis=0)
                dk_acc = jnp.zeros((2 * CHUNK, LANE), F32)
                dv_acc = jnp.zeros((2 * CHUNK, LANE), F32)
                for pair in range(g * qpk // 2, (g + 1) * qpk // 2):
                    pair_heads = slice(2 * pair, 2 * pair + 2)
                    dk_acc = dk_acc + _dot(jnp.concatenate(dscores[pair_heads], axis=1), split_pair(queries[pair]), NN)
                    dv_acc = dv_acc + _dot(jnp.concatenate(probs[pair_heads], axis=1), split_pair(douts[pair]), NN)
                    dq_parts.append(_dot(jnp.concatenate(dscores[pair_heads], axis=0), k_both, TN) * scale)
                dk_groups.append(dk_acc + pltpu.roll(dk_acc, HEAD_DIM, 1))
                dv_groups.append(dv_acc + pltpu.roll(dv_acc, HEAD_DIM, 1))
            dkv[band, :LANE] += jnp.where(low_key, dk_groups[0], dk_groups[1])
            dkv[band, LANE:] += jnp.where(low_key, dv_groups[0], dv_groups[1])
            dproj_ref[rows, 2 * A:2 * A + B] = jnp.concatenate(dq_parts, axis=1).astype(BF16)
        last = slice(tq, tq + CHUNK)
        dkv[last, :] += carry[...]
        dproj_ref[:, 2 * A + B:] = dkv[CHUNK:, :].astype(BF16)
        carry[...] = dkv[:CHUNK, :]

        @pl.when(step == n_tiles - 1)
        def _():
            dsink_ref[...] = jnp.sum(sacc[...], axis=1, keepdims=True)

    specs = _attn_specs(tq, A, B, reverse_tiles=n_tiles)
    return _CHAIN.call(
        body, name="attn_bwd", grid=(n_tiles,),
        in_specs=[pl.BlockSpec(memory_space=pltpu.SMEM)] + specs
        + [pl.BlockSpec((tq, B), lambda i: (rev(i), 0)), pl.BlockSpec((tq, 2 * A), lambda i: (rev(i), 0)),
           pl.BlockSpec((H, 2 * CHUNK, CHUNK), lambda i: (0, 0, 0))],
        out_specs=[pl.BlockSpec((tq, P), lambda i: (rev(i), 0)),
                   pl.BlockSpec((H, 2 * CHUNK, CHUNK), lambda i: (0, 0, 0)), pl.BlockSpec((H, 1), lambda i: (0, 0))],
        out_shape=[SDS((T, P), BF16), SDS((H, 2 * CHUNK, CHUNK), F32), SDS((H, 1), F32)],
        scratch_shapes=[pltpu.VMEM((CHUNK, 2 * LANE), F32), pltpu.VMEM((tq + CHUNK, 2 * LANE), F32),
                        pltpu.VMEM((H, LANE), F32)],
        compiler_params=_params(1))(sinks, proj, proj, proj, proj, proj, do, duv, bias_t)


def _bias_bwd(dbias, onehot):
    H = dbias.shape[0]
    nbk = onehot.shape[1]

    def body(d_ref, oh_ref, o_ref):
        hi, mid, lo = _split3(d_ref[...])
        oh = oh_ref[...]
        o_ref[...] = _dot(hi, oh, NN) + _dot(mid, oh, NN) + _dot(lo, oh, NN)

    return _CHAIN.call(body, name="bias_bwd", in_specs=[VMEM_SPEC] * 2, out_specs=VMEM_SPEC, out_shape=SDS((H, nbk), F32),
                       compiler_params=_params(0))(dbias, onehot)


def _inproj_bwd(dproj, w_t, x, dh1, g):
    T, P = dproj.shape
    D = x.shape[1]
    tm = _tile(T, 512)

    def body(dp_ref, w_ref, x_ref, dh_ref, g_ref, dx_ref, dg_ref):
        @pl.when(pl.program_id(0) == 0)
        def _():
            dg_ref[...] = jnp.zeros_like(dg_ref)

        dn = _dot(dp_ref[...], w_ref[...], NN)
        xv = x_ref[...]
        r = _rms_stats(xv)
        dg_ref[...] += jnp.sum(dn * (xv * r), axis=0, keepdims=True)
        dx_ref[...] = dh_ref[...] + _rms_bwd(dn, xv, r, g_ref[...])

    return _CHAIN.call(
        body, name="inproj_bwd", grid=(T // tm,),
        in_specs=[pl.BlockSpec((tm, P), lambda i: (i, 0)), _resident((P, D)),
                  pl.BlockSpec((tm, D), lambda i: (i, 0)), pl.BlockSpec((tm, D), lambda i: (i, 0)),
                  pl.BlockSpec((1, D), lambda i: (0, 0))],
        out_specs=[pl.BlockSpec((tm, D), lambda i: (i, 0)), pl.BlockSpec((1, D), lambda i: (0, 0))],
        out_shape=[SDS((T, D), F32), SDS((1, D), F32)], compiler_params=_params(1))(dproj, w_t, x, dh1, g)


def _adamw(w, g, m, v):
    m = ADAM_B1 * m + (1.0 - ADAM_B1) * g
    v = ADAM_B2 * v + (1.0 - ADAM_B2) * (g * g)
    m_hat = m / (1.0 - ADAM_B1 ** ADAM_STEP)
    v_hat = v / (1.0 - ADAM_B2 ** ADAM_STEP)
    delta = -ADAM_LR * (m_hat / (jnp.sqrt(v_hat) + ADAM_EPS) + ADAM_WD * w)
    return delta, m, v


def _adam_sharded(csum, recv, w, m, v, name):
    R, C = w.shape
    tr = _tile(R, 128, 16)

    def body(own_ref, recv_ref, w_ref, m_ref, v_ref, g_ref, d_ref, nm_ref, nv_ref):
        g = own_ref[...].astype(F32)
        for r in range(3):
            g = g + recv_ref[r].astype(F32)
        delta, nm, nv = _adamw(w_ref[...], g, m_ref[...], v_ref[...])
        g_ref[...] = g
        d_ref[...] = delta
        nm_ref[...] = nm
        nv_ref[...] = nv

    blk = pl.BlockSpec((tr, C), lambda i: (i, 0))
    return _CHAIN.call(
        body, name=name, grid=(R // tr,),
        in_specs=[pl.BlockSpec((None, tr, C), lambda i: (0, i, 0)), pl.BlockSpec((3, tr, C), lambda i: (0, i, 0)),
                  blk, blk, blk],
        out_specs=[blk] * 4, out_shape=[SDS((R, C), F32)] * 4, compiler_params=_params(1))(csum, recv, w, m, v)


def _rows2d(shape):
    return (int(np.prod(shape[:-1])) if len(shape) > 1 else 1, shape[-1])


def _small_layout(shapes):
    totals, places = {}, []
    for s in shapes:
        r, w = _rows2d(s)
        off = totals.get(w, 0)
        places.append((w, off, r))
        totals[w] = off + -(-r // 8) * 8
    return {w: -(-t // 32) * 32 for w, t in totals.items()}, places


def _pack_small(arrays, totals, places):
    bufs = []
    for w, total in totals.items():
        buf = jnp.zeros((total, w), F32)
        for a, (pw, off, r) in zip(arrays, places):
            if pw == w:
                buf = lax.dynamic_update_slice(buf, a.reshape(r, w).astype(F32), (off, 0))
        bufs.append(buf)
    return bufs


def _adam_small(gathered, totals, places, ws, ms, vs):
    widths = list(totals)
    n, nw = len(places), len(widths)

    def body(*refs):
        gath, params, outs = refs[:nw], refs[nw:nw + 3 * n], refs[nw + 3 * n:]
        for p, (w, off, r) in enumerate(places):
            g_ref = gath[widths.index(w)]
            g = g_ref[0, off:off + r, :]
            for d in range(1, N_DEV):
                g = g + g_ref[d, off:off + r, :]
            delta, nm, nv = _adamw(params[p][...], g, params[n + p][...], params[2 * n + p][...])
            for k, val in enumerate((g, delta, nm, nv)):
                outs[4 * p + k][...] = val

    shapes2d = [SDS((r, w), F32) for w, _, r in places for _ in range(4)]
    outs = _CHAIN.call(body, name="adam_small", in_specs=[VMEM_SPEC] * (nw + 3 * n), out_specs=[VMEM_SPEC] * (4 * n),
                       out_shape=shapes2d, compiler_params=_params(0))(*gathered, *ws, *ms, *vs)
    return [outs[4 * p:4 * p + 4] for p in range(n)]


def kernel(x, rel_bias_table, mix_norm_g, w_in, gate_norm_g, gate_norm_b, w_spatial, b_spatial, attn_sinks, out_norm_a_g, out_norm_b_g, w_out, ffn_norm_g, w_up, w_down, final_norm_g, loss_target, m_rel_bias_table, m_mix_norm_g, m_w_in, m_gate_norm_g, m_gate_norm_b, m_w_spatial, m_b_spatial, m_attn_sinks, m_out_norm_a_g, m_out_norm_b_g, m_w_out, m_ffn_norm_g, m_w_up, m_w_down, m_final_norm_g, v_rel_bias_table, v_mix_norm_g, v_w_in, v_gate_norm_g, v_gate_norm_b, v_w_spatial, v_b_spatial, v_attn_sinks, v_out_norm_a_g, v_out_norm_b_g, v_w_out, v_ffn_norm_g, v_w_up, v_w_down, v_final_norm_g):
    T, D = x.shape[1], x.shape[2]
    A = D // 2
    B = D // 2
    G = A // GROUP_DIM
    H = B // HEAD_DIM
    P = 2 * A + B + 2 * KV_HEADS * HEAD_DIM
    xs = x.reshape(T, D)
    target = loss_target.reshape(T, D)

    win_t, m_win_t, v_win_t = (jnp.swapaxes(a[0], 0, 1) for a in (w_in, m_w_in, v_w_in))
    shards = [win_t.astype(BF16), w_out[0].astype(BF16), w_up[0].astype(BF16), w_down[0].astype(BF16)]
    _CHAIN.token = None
    gather = _gather_begin(shards)
    _gather_step(gather, [(0, 0)], "gather_start")

    g1, g2, g3 = mix_norm_g.reshape(1, D), ffn_norm_g.reshape(1, D), final_norm_g.reshape(1, D)
    lg, lb = gate_norm_g.reshape(1, A), gate_norm_b.reshape(1, A)
    ws = w_spatial[0]
    ws_t = jnp.swapaxes(ws, 1, 2)
    bs_t = jnp.transpose(b_spatial[0])
    ga, gb = out_norm_a_g.reshape(1, A), out_norm_b_g.reshape(1, B)
    sinks = attn_sinks.reshape(H)
    bucket, in_window = _t5_bucket()
    onehot_np = ((bucket[:, :, None] == np.arange(N_BUCKETS)) & in_window[:, :, None]).astype(np.float32)
    onehot = jnp.asarray(onehot_np.reshape(-1, N_BUCKETS)).astype(BF16)
    onehot_kq = jnp.asarray(onehot_np.transpose(1, 0, 2).reshape(-1, N_BUCKETS)).astype(BF16)

    bias, bias_t = _bias_fwd(jnp.transpose(rel_bias_table), jnp.transpose(onehot), jnp.transpose(onehot_kq))
    bias, bias_t = bias.reshape(H, CHUNK, 2 * CHUNK), bias_t.reshape(H, 2 * CHUNK, CHUNK)
    n1 = _mix_norm(xs, g1)
    _gather_step(gather, [(0, 1), (1, 0), (2, 0)], "gather_in_1")
    _gather_step(gather, [(0, 2)], "gather_in_2")
    (win_g,) = _gather_end(gather, [0], "gather_in_end")
    win_t_full = win_g.reshape(P, D)
    proj = _inproj_fwd(n1, win_t_full)
    _gather_step(gather, [(1, 1)], "gather_out_1")
    a_out = _gmlp_fwd(proj, lg, lb, ws, bs_t, A)
    _gather_step(gather, [(1, 2), (2, 1), (3, 0)], "gather_out_2_up_1")
    b_out = _attn_fwd(proj, bias, sinks, A, B)
    (wout_g,) = _gather_end(gather, [1], "gather_out_end")
    _gather_step(gather, [(2, 2)], "gather_up_2")
    wout_full = wout_g.reshape(A + B, D)
    h1, mixed, n2 = _outproj_fwd(a_out, b_out, ga, gb, xs, wout_full, g2)
    (wup_g,) = _gather_end(gather, [2], "gather_up_end")
    _gather_step(gather, [(3, 1)], "gather_down_1")
    wup_t = jnp.transpose(wup_g, (0, 2, 1)).reshape(-1, D)
    z = _ffn_up(n2, wup_g)
    _gather_step(gather, [(3, 2)], "gather_down_2")
    (wdown_g,) = _gather_end(gather, [3], "gather_down_end")
    h2 = _ffn_down(h1, z, wdown_g.reshape(-1, D))
    loss_part, dg3, dh2, dh2b = _final_loss(h2, g3, target)

    def reduce_to_chip(state, name):
        csums = [_chip_sum(part, received, "%s_chip_sum_%d" % (name, a))
                 for a, (part, received) in enumerate(_sibling_exchange_end(state, name + "_sib_end"))]
        return _chip_exchange_begin(csums, name + "_chip")

    dwdown = _matmul_tn(z, dh2b, "grad_w_down", square_a=True).reshape(wdown_g.shape)
    dzp = _ffn_down_bwd(dh2b, z, wdown_g.reshape(-1, D))
    dwup = _matmul_tn(n2, dzp, "grad_w_up", col_blocks=N_DEV)
    sib_ffn = _sibling_exchange_begin([dwdown, dwup], "rs_ffn_sib")
    dh1, dh1b, dg2 = _ffn_norm_bwd(_ffn_up_bwd(dzp, wup_t), dh2, h1, g2)
    chip_ffn = reduce_to_chip(sib_ffn, "rs_ffn")
    da, db, dga, dgb = _outproj_bwd(dh1b, wout_full, a_out, b_out, ga, gb)
    dwout = _matmul_tn(mixed, dh1b, "grad_w_out").reshape(wout_g.shape)
    sib_out = _sibling_exchange_begin([dwout], "rs_out_sib")
    duv, dlg, dlb, dws, dbs_t = _gmlp_bwd(proj, da, lg, lb, ws, ws_t, bs_t, A)
    dproj, dbias_t, dsinks = _attn_bwd(proj, db, duv, bias_t, sinks, A, B)
    chip_out = reduce_to_chip(sib_out, "rs_out")
    dtable_t = _bias_bwd(dbias_t.reshape(H, -1), onehot_kq)
    dwin_t = _matmul_tn(dproj, n1, "grad_w_in").reshape(win_g.shape)
    sib_in = _sibling_exchange_begin([dwin_t], "rs_in_sib")
    grad_x, dg1 = _inproj_bwd(dproj, win_t_full, xs, dh1, g1)

    small_w = [rel_bias_table, mix_norm_g, gate_norm_g, gate_norm_b, w_spatial, b_spatial, attn_sinks,
               out_norm_a_g, out_norm_b_g, ffn_norm_g, final_norm_g]
    small_m = [m_rel_bias_table, m_mix_norm_g, m_gate_norm_g, m_gate_norm_b, m_w_spatial, m_b_spatial, m_attn_sinks,
               m_out_norm_a_g, m_out_norm_b_g, m_ffn_norm_g, m_final_norm_g]
    small_v = [v_rel_bias_table, v_mix_norm_g, v_gate_norm_g, v_gate_norm_b, v_w_spatial, v_b_spatial, v_attn_sinks,
               v_out_norm_a_g, v_out_norm_b_g, v_ffn_norm_g, v_final_norm_g]
    small_g = [jnp.transpose(dtable_t), dg1, dlg, dlb, dws, jnp.transpose(dbs_t), dsinks, dga, dgb, dg2, dg3]
    nothing = jnp.zeros((1, H), F32)
    small_w, small_m, small_v = small_w + [nothing], small_m + [nothing], small_v + [nothing]
    small_g = small_g + [jnp.broadcast_to(loss_part, (1, H))]
    shapes = [w.shape for w in small_w]
    totals, places = _small_layout(shapes)
    as_rows = lambda arrays: [a.reshape(_rows2d(a.shape)) for a in arrays]
    big = [None] * 4

    def adam_of(k, state, a, w, m, v):
        csum, received = _chip_exchange_end(state, a, "rs_%d_end" % k)
        big[k] = _adam_sharded(csum, received, w, m, v, "adam_%d" % k)

    small_gather = _gather_begin(_pack_small(small_g, totals, places))
    every = range(len(totals))
    _gather_step(small_gather, [(a, 0) for a in every], "small_gather_start")
    ((part, received),) = _sibling_exchange_end(sib_in, "rs_in_sib_end")
    csum_in = _chip_sum(part, received, "rs_in_chip_sum")
    _gather_step(small_gather, [(a, 1) for a in every], "small_gather_1")
    chip_in = _chip_exchange_begin([csum_in], "rs_in_chip")
    adam_of(3, chip_ffn, 0, w_down[0], m_w_down[0], v_w_down[0])
    _gather_step(small_gather, [(a, 2) for a in every], "small_gather_2")
    adam_of(2, chip_ffn, 1, w_up[0], m_w_up[0], v_w_up[0])
    gathered = _gather_end(small_gather, list(every), "small_gather_end")
    small_out = _adam_small(gathered, totals, places, as_rows(small_w), as_rows(small_m), as_rows(small_v))
    sg, sd, sm, sv = [[outs[k].reshape(s) for outs, s in zip(small_out, shapes)] for k in range(4)]
    adam_of(1, chip_out, 0, w_out[0], m_w_out[0], v_w_out[0])
    adam_of(0, chip_in, 0, win_t, m_win_t, v_win_t)
    big[0] = [jnp.swapaxes(o, 0, 1) for o in big[0]]
    big = [[o.reshape(w.shape) for o in outs] for outs, w in zip(big, (w_in, w_out, w_up, w_down))]

    loss = sg[-1][0, 0]

    order = ["s0", "s1", "b0", "s2", "s3", "s4", "s5", "s6", "s7", "s8", "b1", "s9", "b2", "b3", "s10"]

    def group(idx):
        small = (sg, sd, sm, sv)[idx]
        return [small[int(t[1:])] if t[0] == "s" else big[int(t[1:])][idx] for t in order]

    return (loss, grad_x.reshape(x.shape), *group(0), *group(1), *group(2), *group(3))
```

```python
import functools
import math

import numpy as np
import jax
import jax.numpy as jnp
from jax import lax
from jax.experimental import pallas as pl
from jax.experimental.pallas import tpu as pltpu

F32 = jnp.float32
BF16 = jnp.bfloat16
SDS = jax.ShapeDtypeStruct
MESH = pl.DeviceIdType.MESH

N_DEV = 8
EPS = 1e-5
NEG = -1e30
CHUNK = 128
GROUP_DIM = 128
HEAD_DIM = 64
KV_HEADS = 2
N_BUCKETS = 32
MAX_DISTANCE = 128
ADAM_LR, ADAM_B1, ADAM_B2, ADAM_EPS, ADAM_WD, ADAM_STEP = 0.001, 0.9, 0.999, 1e-08, 0.01, 10
GELU_C0 = math.sqrt(2.0 / math.pi)
GELU_C1 = 0.044715

V7X_VMEM_BYTES = 64 * 1024 * 1024
VMEM_LIMIT = V7X_VMEM_BYTES - 8 * 1024 * 1024
LANE = 128

NN = ((1,), (0,))
NT = ((1,), (1,))
TN = ((0,), (0,))


def _dot(a, b, dims):
    return lax.dot_general(a, b, (dims, ((), ())), preferred_element_type=F32)


def _tile(n, pref, unit=LANE):
    best = None
    for t in range(unit, min(n, pref) + 1, unit):
        if n % t == 0:
            best = t
    return n if best is None else best


def _params(n_grid):
    return pltpu.CompilerParams(dimension_semantics=("arbitrary",) * n_grid, vmem_limit_bytes=VMEM_LIMIT)


def _resident(shape):
    return pl.BlockSpec(shape, lambda i: (0, 0), pipeline_mode=pl.Buffered(1))


def _gelu(x):
    return 0.5 * x * (1.0 + jnp.tanh(GELU_C0 * (x + GELU_C1 * x * x * x)))


def _gelu_and_grad(x):
    x2 = x * x
    t = jnp.tanh(GELU_C0 * x * (1.0 + GELU_C1 * x2))
    val = 0.5 * x * (1.0 + t)
    grad = 0.5 * (1.0 + t) + 0.5 * x * (1.0 - t * t) * (GELU_C0 * (1.0 + 3.0 * GELU_C1 * x2))
    return val, grad


def _rms_stats(x):
    return lax.rsqrt(jnp.mean(x * x, axis=-1, keepdims=True) + EPS)


def _rms_bwd(dy, x, r, g):
    w = dy * g
    return r * w - x * (r * r * r) * jnp.mean(w * x, axis=-1, keepdims=True)


def _t5_bucket():
    i = np.arange(CHUNK)[:, None]
    j = np.arange(2 * CHUNK)[None, :]
    rel = np.maximum(i + CHUNK - j, 0)
    n_exact = N_BUCKETS // 2
    relf = np.maximum(rel, n_exact).astype(np.float32)
    large = n_exact + (np.log(relf / np.float32(n_exact)) / np.float32(math.log(MAX_DISTANCE / n_exact))
                       * np.float32(N_BUCKETS - n_exact)).astype(np.int32)
    large = np.minimum(large, N_BUCKETS - 1)
    bucket = np.where(rel < n_exact, rel, large)
    in_window = (i + CHUNK - j >= 0) & (i + CHUNK - j < CHUNK)
    return bucket.astype(np.int32), in_window


def _split3(x):
    hi = x.astype(BF16)
    r1 = x - hi.astype(F32)
    mid = r1.astype(BF16)
    lo = (r1 - mid.astype(F32)).astype(BF16)
    return hi, mid, lo


HBM_SPEC = pl.BlockSpec(memory_space=pltpu.HBM)


def _mesh_pos():
    return lax.axis_index("x"), lax.axis_index("y"), lax.axis_index("c")


def _dev_index(px, py, pc):
    return 4 * px + 2 * py + pc


SEM_SPEC = pl.BlockSpec(memory_space=pltpu.SEMAPHORE)
ANY_SPEC = pl.BlockSpec(memory_space=pl.ANY)
VMEM_SPEC = pl.BlockSpec(memory_space=pltpu.VMEM)
TOKEN_SPEC = VMEM_SPEC
TOKEN = SDS((8, LANE), F32)
SIDE_EFFECT = pltpu.SideEffectType.DATAFLOW_SIDE_EFFECTING


def _hbm(x):
    return pltpu.with_memory_space_constraint(x, pltpu.HBM)


class _CallChain:
    def __init__(self):
        self.token = None

    def call(self, body, *, in_specs, out_specs, out_shape, **kwargs):
        dep, n_in = self.token, len(in_specs)
        single = not isinstance(out_shape, (list, tuple))
        out_shapes = [out_shape] if single else list(out_shape)
        out_specs = [out_specs] if single else list(out_specs)
        n_out = len(out_shapes)
        n_dep = 0 if dep is None else 1
        token_spec = pl.BlockSpec((8, LANE), lambda *_: (0, 0)) if kwargs.get("grid") else VMEM_SPEC

        def chained(*refs):
            outs_at = n_in + n_dep
            body(*refs[:n_in], *refs[outs_at:outs_at + n_out], *refs[outs_at + n_out + 1:])
            token = refs[outs_at + n_out]
            token[...] = jnp.zeros_like(token)

        inner = pl.pallas_call(chained, in_specs=list(in_specs) + [ANY_SPEC] * n_dep, out_specs=out_specs + [token_spec],
                               out_shape=out_shapes + [TOKEN], **kwargs)

        def run(*operands):
            outs = inner(*operands) if dep is None else inner(*operands, dep)
            self.token = outs[n_out]
            return outs[0] if single else list(outs[:n_out])

        return run


_CHAIN = _CallChain()


def _wait_all(waits, x, y, c):
    for kind, src, dst, send_sem, recv_sem in waits:
        cp = pltpu.make_async_remote_copy(src_ref=src, dst_ref=dst, send_sem=send_sem, recv_sem=recv_sem,
                                          device_id=(x, y, c), device_id_type=MESH)
        if kind == "send":
            cp.wait_send()
        else:
            cp.wait_recv()


def _split_start(bufs, copies_of, n_sems, name, sem_sets=(), waits_of=None):
    n, ns = len(bufs), len(sem_sets)
    flat_sems = [s for pair in sem_sets for s in pair]

    def body(*refs):
        ins = refs[:n]
        sems = refs[n:n + 2 * ns]
        send_sems, recv_sems = refs[n + 2 * ns], refs[n + 2 * ns + 1]
        if waits_of is not None:
            _wait_all(waits_of(ins, [(sems[2 * i], sems[2 * i + 1]) for i in range(ns)]), *_mesh_pos())
        for src, dst, k, target in copies_of(ins):
            pltpu.make_async_remote_copy(src_ref=src, dst_ref=dst, send_sem=send_sems.at[k], recv_sem=recv_sems.at[k],
                                         device_id=target, device_id_type=MESH).start()

    outs = _CHAIN.call(
        body, name=name,
        out_shape=[pltpu.SemaphoreType.DMA((n_sems,)), pltpu.SemaphoreType.DMA((n_sems,))]
        + [pltpu.HBM(b.shape, b.dtype) for b in bufs],
        in_specs=[HBM_SPEC] * n + [SEM_SPEC] * (2 * ns), out_specs=[SEM_SPEC, SEM_SPEC] + [HBM_SPEC] * n,
        input_output_aliases={a: 2 + a for a in range(n)},
        compiler_params=pltpu.CompilerParams(has_side_effects=SIDE_EFFECT),
    )(*[_hbm(b) for b in bufs], *flat_sems)
    return outs[0], outs[1], list(outs[2:2 + n])


def _split_wait(bufs, sem_sets, waits_of, name):
    n, ns = len(bufs), len(sem_sets)
    flat_sems = [s for pair in sem_sets for s in pair]

    def body(*refs):
        ins = refs[:n]
        sems = refs[n:n + 2 * ns]
        _wait_all(waits_of(ins, [(sems[2 * i], sems[2 * i + 1]) for i in range(ns)]), *_mesh_pos())

    outs = _CHAIN.call(
        body, name=name,
        out_shape=[pltpu.HBM(b.shape, b.dtype) for b in bufs],
        in_specs=[HBM_SPEC] * n + [SEM_SPEC] * (2 * ns), out_specs=[HBM_SPEC] * n,
        input_output_aliases={a: a for a in range(n)},
        compiler_params=pltpu.CompilerParams(has_side_effects=SIDE_EFFECT),
    )(*bufs, *flat_sems)
    return list(outs)


def _gather_blocks(land):
    rows = land.shape[1]
    first = (rows // 2) // 16 * 16

    def block(px, py, pc):
        return land.at[_dev_index(px, py, pc)]

    def halves(px, py, pc):
        return (land.at[_dev_index(px, py, pc), pl.ds(0, first)], land.at[_dev_index(px, py, pc), pl.ds(first, rows - first)])

    return block, halves


def _gather_begin(shards):
    me = _dev_index(*_mesh_pos())
    lands = [lax.dynamic_update_index_in_dim(lax.empty((N_DEV,) + s.shape, s.dtype), s, me, 0) for s in shards]
    return dict(lands=lands, stage={})


STAGE_COPIES = (3, 4, 1)


def _gather_step(state, items, name):
    which = sorted({a for a, _ in items})
    at = {a: i for i, a in enumerate(which)}
    sem_sets = [state["stage"][(a, s - 1)][0] for a, s in items if s > 0]
    offset, n_sems = {}, 0
    for a, s in items:
        offset[(a, s)] = n_sems
        n_sems += STAGE_COPIES[s]

    def waits_of(ins, sems):
        x, y, c = _mesh_pos()
        out, earlier = [], 0
        for a, s in items:
            if s == 0:
                continue
            block, halves = _gather_blocks(ins[at[a]])
            send, recv = sems[earlier]
            off = state["stage"][(a, s - 1)][1]
            earlier += 1
            if s == 1:
                arrived = [(1, block(1 - x, y, c)), (2, block(x, 1 - y, c))]
            else:
                arrived = list(zip((2, 3), halves(1 - x, 1 - y, c)))
            out += [("recv", ref, ref, send.at[off + k], recv.at[off + k]) for k, ref in arrived]
        return out

    def copies_of(ins):
        x, y, c = _mesh_pos()
        sibling = (x, y, 1 - c)
        out = []
        for a, s in items:
            block, halves = _gather_blocks(ins[at[a]])
            off = offset[(a, s)]
            if s == 0:
                mine = block(x, y, c)
                out += [(mine, mine, off + 1, (1 - x, y, c)), (mine, mine, off + 2, (x, 1 - y, c)), (mine, mine, off, sibling)]
            elif s == 1:
                from_x, from_y = block(1 - x, y, c), block(x, 1 - y, c)
                out += [(halves(1 - x, y, c)[0], halves(1 - x, y, c)[0], off + 2, (x, 1 - y, c)),
                        (halves(x, 1 - y, c)[1], halves(x, 1 - y, c)[1], off + 3, (1 - x, y, c)),
                        (from_x, from_x, off, sibling), (from_y, from_y, off + 1, sibling)]
            else:
                diag = block(1 - x, 1 - y, c)
                out.append((diag, diag, off, sibling))
        return out

    send_sems, recv_sems, bufs = _split_start([state["lands"][a] for a in which], copies_of, n_sems, name,
                                              sem_sets=sem_sets, waits_of=waits_of)
    for a in which:
        state["lands"][a] = bufs[at[a]]
    for a, s in items:
        state["stage"][(a, s)] = ((send_sems, recv_sems), offset[(a, s)])


def _gather_end(state, which, name):
    sem_sets = [state["stage"][(a, s)][0] for a in which for s in range(3)]

    def waits(ins, sems):
        x, y, c = _mesh_pos()
        out = []
        for i, a in enumerate(which):
            block, halves = _gather_blocks(ins[i])
            (b_send, b_recv), (s1_send, s1_recv), (s2_send, s2_recv) = sems[3 * i:3 * i + 3]
            o0, o1, o2 = (state["stage"][(a, s)][1] for s in range(3))
            arrivals = [(block(x, y, 1 - c), b_send, b_recv, o0),
                        (block(1 - x, y, 1 - c), s1_send, s1_recv, o1), (block(x, 1 - y, 1 - c), s1_send, s1_recv, o1 + 1),
                        (block(1 - x, 1 - y, 1 - c), s2_send, s2_recv, o2)]
            mine = block(x, y, c)
            sent = [(mine, b_send, b_recv, o0 + k) for k in range(3)]
            sent += [(block(1 - x, y, c), s1_send, s1_recv, o1), (block(x, 1 - y, c), s1_send, s1_recv, o1 + 1),
                     (halves(1 - x, y, c)[0], s1_send, s1_recv, o1 + 2), (halves(x, 1 - y, c)[1], s1_send, s1_recv, o1 + 3),
                     (block(1 - x, 1 - y, c), s2_send, s2_recv, o2)]
            out += [("recv", ref, ref, s.at[k], r.at[k]) for ref, s, r, k in arrivals]
            out += [("send", ref, ref, s.at[k], r.at[k]) for ref, s, r, k in sent]
        return out

    bufs = _split_wait([state["lands"][a] for a in which], sem_sets, waits, name)
    for i, a in enumerate(which):
        state["lands"][a] = bufs[i]
    return bufs


def _sibling_exchange_begin(parts, name):
    lands = [lax.empty((4,) + p.shape[1:], p.dtype) for p in parts]
    n = len(parts)

    def copies_of(ins):
        x, y, c = _mesh_pos()
        return [(ins[a].at[2 * j + (1 - c)], ins[n + a].at[j], 4 * a + j, (x, y, 1 - c)) for a in range(n) for j in range(4)]

    send_sems, recv_sems, bufs = _split_start(list(parts) + lands, copies_of, 4 * n, name)
    return dict(bufs=bufs, sems=(send_sems, recv_sems), n=n)


def _sibling_exchange_end(state, name):
    n = state["n"]

    def waits(ins, sems):
        _, _, c = _mesh_pos()
        return [(kind, ins[a].at[2 * j + (1 - c)], ins[n + a].at[j], sems[0][0].at[4 * a + j], sems[0][1].at[4 * a + j])
                for a in range(n) for j in range(4) for kind in ("send", "recv")]

    bufs = _split_wait(state["bufs"], [state["sems"]], waits, name)
    return [(bufs[a], bufs[n + a]) for a in range(n)]


CHIP_FLIPS = (2, 1, 3)


def _chip_exchange_begin(csums, name):
    lands = [lax.empty((3,) + s.shape[1:], s.dtype) for s in csums]
    n = len(csums)

    def copies_of(ins):
        x, y, c = _mesh_pos()
        chips = [(1 - x, y), (x, 1 - y), (1 - x, 1 - y)]
        return [(ins[a].at[CHIP_FLIPS[r]], ins[n + a].at[r], 3 * a + r, (px, py, c))
                for a in range(n) for r, (px, py) in enumerate(chips)]

    send_sems, recv_sems, bufs = _split_start(list(csums) + lands, copies_of, 3 * n, name)
    return dict(bufs=bufs, sems=(send_sems, recv_sems), n=n)


def _chip_exchange_end(state, a, name):
    n = state["n"]

    def waits(ins, sems):
        return [(kind, ins[0].at[CHIP_FLIPS[r]], ins[1].at[r], sems[0][0].at[3 * a + r], sems[0][1].at[3 * a + r])
                for r in range(3) for kind in ("send", "recv")]

    csum, received = _split_wait([state["bufs"][a], state["bufs"][n + a]], [state["sems"]], waits, name)
    return csum, received


def _chip_sum(part, recv, name):
    _, R, C = part.shape
    tr = _tile(R, 1024, 16)
    place = jnp.stack([lax.axis_index("c"), 2 * lax.axis_index("x") + lax.axis_index("y")]).astype(jnp.int32)

    def body(place_ref, p_ref, r_ref, o_ref):
        o_ref[...] = (p_ref[...].astype(F32) + r_ref[...].astype(F32)).astype(o_ref.dtype)

    def chip(p, place_ref):
        return jnp.bitwise_xor(p, place_ref[1])

    grid_spec = pltpu.PrefetchScalarGridSpec(
        num_scalar_prefetch=1, grid=(4, R // tr),
        in_specs=[pl.BlockSpec((None, tr, C), lambda p, i, place_ref: (2 * chip(p, place_ref) + place_ref[0], i, 0)),
                  pl.BlockSpec((None, tr, C), lambda p, i, place_ref: (chip(p, place_ref), i, 0))],
        out_specs=pl.BlockSpec((None, tr, C), lambda p, i, place_ref: (p, i, 0)))
    return pl.pallas_call(body, name=name, grid_spec=grid_spec, out_shape=SDS((4, R, C), part.dtype),
                          compiler_params=_params(2))(place, part, recv)


def _bias_fwd(table_t, onehot_t, onehot_kq_t):
    H = table_t.shape[0]
    n = onehot_t.shape[1]

    def body(t_ref, oh_ref, oh_kq_ref, o_ref, o_kq_ref):
        hi, mid, lo = _split3(t_ref[...])
        for src, dst in ((oh_ref, o_ref), (oh_kq_ref, o_kq_ref)):
            oh = src[...]
            dst[...] = _dot(hi, oh, NN) + _dot(mid, oh, NN) + _dot(lo, oh, NN)

    return _CHAIN.call(body, name="bias_fwd", in_specs=[VMEM_SPEC] * 3, out_specs=[VMEM_SPEC] * 2,
                       out_shape=[SDS((H, n), F32)] * 2, compiler_params=_params(0))(table_t, onehot_t, onehot_kq_t)


def _mix_norm(x, g):
    T, D = x.shape
    tm = _tile(T, 512)

    def body(x_ref, g_ref, n_ref):
        xv = x_ref[...]
        n_ref[...] = (xv * _rms_stats(xv) * g_ref[...]).astype(BF16)

    row = pl.BlockSpec((tm, D), lambda i: (i, 0))
    return _CHAIN.call(body, name="mix_norm", grid=(T // tm,), in_specs=[row, pl.BlockSpec((1, D), lambda i: (0, 0))],
                       out_specs=row, out_shape=SDS((T, D), BF16), compiler_params=_params(1))(x, g)


def _inproj_fwd(n, w_t):
    T, D = n.shape
    P = w_t.shape[0]
    tm = _tile(T, 512)

    def body(n_ref, w_ref, proj_ref):
        proj_ref[...] = _dot(n_ref[...], w_ref[...], NT)

    return _CHAIN.call(
        body, name="inproj_fwd", grid=(T // tm,),
        in_specs=[pl.BlockSpec((tm, D), lambda i: (i, 0)), _resident((P, D))],
        out_specs=pl.BlockSpec((tm, P), lambda i: (i, 0)),
        out_shape=SDS((T, P), F32), compiler_params=_params(1))(n, w_t)


def _layer_norm_group(vg, lg, lb):
    mu = jnp.mean(vg, axis=-1, keepdims=True)
    xc = vg - mu
    rstd = lax.rsqrt(jnp.mean(xc * xc, axis=-1, keepdims=True) + EPS)
    vhat = xc * rstd
    return vhat, rstd, vhat * lg + lb


def _gmlp_fwd(proj, lg, lb, w_s, bs_t, A):
    T = proj.shape[0]
    G = A // GROUP_DIM
    tm = _tile(T, 512)
    nc = tm // CHUNK

    def body(u_ref, v_ref, lg_ref, lb_ref, w_ref, bst_ref, a_ref):
        row = lax.broadcasted_iota(jnp.int32, (CHUNK, CHUNK), 0)
        col = lax.broadcasted_iota(jnp.int32, (CHUNK, CHUNK), 1)
        causal = row >= col
        for g in range(G):
            sl = slice(g * GROUP_DIM, (g + 1) * GROUP_DIM)
            _, _, vn = _layer_norm_group(_gelu(v_ref[:, sl]), lg_ref[:, sl], lb_ref[:, sl])
            vnb = vn.astype(BF16)
            wm = jnp.where(causal, w_ref[g], 0.0).astype(BF16)
            ug = _gelu(u_ref[:, sl])
            bcol = bst_ref[:, g:g + 1]
            for c in range(nc):
                rs = slice(c * CHUNK, (c + 1) * CHUNK)
                a_ref[rs, sl] = ug[rs] * (_dot(wm, vnb[rs], NN) + bcol)

    return _CHAIN.call(
        body, name="gmlp_fwd", grid=(T // tm,),
        in_specs=[pl.BlockSpec((tm, A), lambda i: (i, 0)), pl.BlockSpec((tm, A), lambda i: (i, 1)),
                  pl.BlockSpec((1, A), lambda i: (0, 0)), pl.BlockSpec((1, A), lambda i: (0, 0)),
                  pl.BlockSpec((G, CHUNK, CHUNK), lambda i: (0, 0, 0)), pl.BlockSpec((CHUNK, G), lambda i: (0, 0))],
        out_specs=pl.BlockSpec((tm, A), lambda i: (i, 0)),
        out_shape=SDS((T, A), F32), compiler_params=_params(1))(proj, proj, lg, lb, w_s, bs_t)


def _attn_masks(first_tile):
    ii = lax.broadcasted_iota(jnp.int32, (CHUNK, 2 * CHUNK), 0)
    jj = lax.broadcasted_iota(jnp.int32, (CHUNK, 2 * CHUNK), 1)
    in_window = (jj > ii) & (jj <= ii + CHUNK)
    first_mask = in_window & jnp.logical_or(jnp.logical_not(first_tile), jj >= CHUNK)
    return in_window, first_mask


def _softmax_with_sink(s, sink, axis):
    m = jnp.maximum(jnp.max(s, axis=axis, keepdims=True), sink)
    p = jnp.exp(s - m)
    e_sink = jnp.exp(sink - m)
    inv = 1.0 / (jnp.sum(p, axis=axis, keepdims=True) + e_sink)
    return p * inv, e_sink * inv


def _pad_heads(band, group):
    lane = lax.broadcasted_iota(jnp.int32, band.shape, 1)
    if group == 0:
        low = jnp.where(lane < HEAD_DIM, band, 0.0)
        high = pltpu.roll(low, HEAD_DIM, 1)
    else:
        high = jnp.where(lane >= HEAD_DIM, band, 0.0)
        low = pltpu.roll(high, HEAD_DIM, 1)
    return low.astype(BF16), high.astype(BF16)


def _attn_specs(tq, A, B, reverse_tiles=None):
    nb = tq // CHUNK
    kcol = (2 * A + B) // LANE
    if reverse_tiles is None:
        tile = lambda i: i
    else:
        tile = lambda i: reverse_tiles - 1 - i
    prev = lambda i: jnp.maximum(tile(i) * nb - 1, 0)
    return [pl.BlockSpec((tq, B), lambda i: (tile(i), 2 * A // B)),
            pl.BlockSpec((tq, LANE), lambda i: (tile(i), kcol)),
            pl.BlockSpec((tq, LANE), lambda i: (tile(i), kcol + 1)),
            pl.BlockSpec((CHUNK, LANE), lambda i: (prev(i), kcol)),
            pl.BlockSpec((CHUNK, LANE), lambda i: (prev(i), kcol + 1))]


def _attn_fwd(proj, bias, sinks, A, B):
    T = proj.shape[0]
    H = B // HEAD_DIM
    qpk = H // KV_HEADS
    tq = _tile(T, 512)
    nb = tq // CHUNK

    scale = HEAD_DIM ** -0.5

    def body(sink_ref, q_ref, k_ref, v_ref, kp_ref, vp_ref, bias_ref, o_ref):
        in_window, first_mask = _attn_masks(pl.program_id(0) == 0)
        for b in range(nb):
            rows = slice(b * CHUNK, (b + 1) * CHUNK)
            if b == 0:
                kprev, vprev, mask = kp_ref[...], vp_ref[...], first_mask
            else:
                prows = slice((b - 1) * CHUNK, b * CHUNK)
                kprev, vprev, mask = k_ref[prows, :], v_ref[prows, :], in_window
            kband = jnp.concatenate([kprev, k_ref[rows, :]], axis=0)
            vband = jnp.concatenate([vprev, v_ref[rows, :]], axis=0)
            k_pads = [_pad_heads(kband, g) for g in range(KV_HEADS)]
            v_both = [jnp.concatenate(_pad_heads(vband, g), axis=0) for g in range(KV_HEADS)]
            scores = []
            for pair in range(H // 2):
                h = 2 * pair
                qs = (q_ref[rows, h * HEAD_DIM:(h + 2) * HEAD_DIM] * scale).astype(BF16)
                scores += [_dot(qs, kz, NT) for kz in k_pads[h // qpk]]
            probs = [_softmax_with_sink(jnp.where(mask, s + bias_ref[h], NEG), sink_ref[h], -1)[0].astype(BF16)
                     for h, s in enumerate(scores)]
            outs = [_dot(jnp.concatenate(probs[h:h + 2], axis=1), v_both[h // qpk], NN) for h in range(0, H, 2)]
            o_ref[rows, :] = jnp.concatenate(outs, axis=1)

    return _CHAIN.call(
        body, name="attn_fwd", grid=(T // tq,),
        in_specs=[pl.BlockSpec(memory_space=pltpu.SMEM)] + _attn_specs(tq, A, B)
        + [pl.BlockSpec((H, CHUNK, 2 * CHUNK), lambda i: (0, 0, 0))],
        out_specs=pl.BlockSpec((tq, B), lambda i: (i, 0)),
        out_shape=SDS((T, B), F32), compiler_params=_params(1))(sinks, proj, proj, proj, proj, proj, bias)


def _outproj_fwd(a, b, ga, gb, x, w, g_ffn):
    T, A = a.shape
    B = b.shape[1]
    D = x.shape[1]
    tm = _tile(T, 512)

    def body(a_ref, b_ref, ga_ref, gb_ref, x_ref, w_ref, gf_ref, h_ref, mix_ref, n_ref):
        av, bv = a_ref[...], b_ref[...]
        mix_ref[:, :A] = (av * _rms_stats(av) * ga_ref[...]).astype(BF16)
        mix_ref[:, A:] = (bv * _rms_stats(bv) * gb_ref[...]).astype(BF16)
        hv = x_ref[...] + _dot(mix_ref[...], w_ref[...], NN)
        h_ref[...] = hv
        n_ref[...] = (hv * _rms_stats(hv) * gf_ref[...]).astype(BF16)

    row = pl.BlockSpec((tm, D), lambda i: (i, 0))
    return _CHAIN.call(
        body, name="outproj_fwd", grid=(T // tm,),
        in_specs=[pl.BlockSpec((tm, A), lambda i: (i, 0)), pl.BlockSpec((tm, B), lambda i: (i, 0)),
                  pl.BlockSpec((1, A), lambda i: (0, 0)), pl.BlockSpec((1, B), lambda i: (0, 0)),
                  row, _resident((A + B, D)), pl.BlockSpec((1, D), lambda i: (0, 0))],
        out_specs=[row, pl.BlockSpec((tm, A + B), lambda i: (i, 0)), row],
        out_shape=[SDS((T, D), F32), SDS((T, A + B), BF16), SDS((T, D), BF16)],
        compiler_params=_params(1))(a, b, ga, gb, x, w, g_ffn)


def _ffn_up(n, w_up):
    T, D = n.shape
    Fb = w_up.shape[2]
    F = N_DEV * Fb
    tm, tf = _tile(T, 1024), _tile(Fb, 1024)
    per = Fb // tf

    def body(n_ref, wu_ref, z_ref):
        z_ref[...] = jnp.maximum(_dot(n_ref[...], wu_ref[...], NN), 0.0).astype(BF16)

    return _CHAIN.call(
        body, name="ffn_up", grid=(T // tm, F // tf),
        in_specs=[pl.BlockSpec((tm, D), lambda i, j: (i, 0)),
                  pl.BlockSpec((None, D, tf), lambda i, j: (j // per, 0, j % per))],
        out_specs=pl.BlockSpec((tm, tf), lambda i, j: (i, j)),
        out_shape=SDS((T, F), BF16), compiler_params=_params(2))(n, w_up)


def _ffn_down(h1, z, w_down):
    T, D = h1.shape
    F = w_down.shape[0]
    tm, tn, tk = _tile(T, 1024), _tile(D, 1024), _tile(F, 4096)

    def body(h_ref, z_ref, wd_ref, h2_ref):
        k = pl.program_id(2)

        @pl.when(k == 0)
        def _():
            h2_ref[...] = h_ref[...]

        zf = z_ref[...].astype(F32)
        h2_ref[...] += _dot((zf * zf).astype(BF16), wd_ref[...], NN)

    return _CHAIN.call(
        body, name="ffn_down", grid=(T // tm, D // tn, F // tk),
        in_specs=[pl.BlockSpec((tm, tn), lambda i, j, k: (i, j)), pl.BlockSpec((tm, tk), lambda i, j, k: (i, k)),
                  pl.BlockSpec((tk, tn), lambda i, j, k: (k, j))],
        out_specs=pl.BlockSpec((tm, tn), lambda i, j, k: (i, j)),
        out_shape=SDS((T, D), F32), compiler_params=_params(3))(h1, z, w_down)


def _final_loss(h2, g, target):
    T, D = h2.shape
    tm = _tile(T, 512)

    def body(h_ref, g_ref, t_ref, loss_ref, dg_ref, dh_ref, dhb_ref):
        @pl.when(pl.program_id(0) == 0)
        def _():
            loss_ref[...] = jnp.zeros_like(loss_ref)
            dg_ref[...] = jnp.zeros_like(dg_ref)

        hv, gv = h_ref[...], g_ref[...]
        r = _rms_stats(hv)
        hn = hv * r
        e = hn * gv - t_ref[...]
        loss_ref[...] += (0.5 / D) * jnp.sum(jnp.sum(e * e, axis=0, keepdims=True), axis=-1, keepdims=True)
        dy = e * (1.0 / D)
        dg_ref[...] += jnp.sum(dy * hn, axis=0, keepdims=True)
        dh = _rms_bwd(dy, hv, r, gv)
        dh_ref[...] = dh
        dhb_ref[...] = dh.astype(BF16)

    return _CHAIN.call(
        body, name="final_loss", grid=(T // tm,),
        in_specs=[pl.BlockSpec((tm, D), lambda i: (i, 0)), pl.BlockSpec((1, D), lambda i: (0, 0)),
                  pl.BlockSpec((tm, D), lambda i: (i, 0))],
        out_specs=[pl.BlockSpec((1, 1), lambda i: (0, 0)), pl.BlockSpec((1, D), lambda i: (0, 0)),
                   pl.BlockSpec((tm, D), lambda i: (i, 0)), pl.BlockSpec((tm, D), lambda i: (i, 0))],
        out_shape=[SDS((1, 1), F32), SDS((1, D), F32), SDS((T, D), F32), SDS((T, D), BF16)],
        compiler_params=_params(1))(h2, g, target)


def _ffn_down_bwd(dh2b, z, w_down):
    T, D = dh2b.shape
    F = w_down.shape[0]
    tm, tf = _tile(T, 1024), _tile(F, 1024)

    def body(dh_ref, z_ref, wd_ref, dzp_ref):
        dzz = _dot(dh_ref[...], wd_ref[...], NT)
        dzp_ref[...] = (dzz * (2.0 * z_ref[...].astype(F32))).astype(BF16)

    return _CHAIN.call(
        body, name="ffn_down_bwd", grid=(T // tm, F // tf),
        in_specs=[pl.BlockSpec((tm, D), lambda i, j: (i, 0)), pl.BlockSpec((tm, tf), lambda i, j: (i, j)),
                  pl.BlockSpec((tf, D), lambda i, j: (j, 0))],
        out_specs=pl.BlockSpec((tm, tf), lambda i, j: (i, j)),
        out_shape=SDS((T, F), BF16), compiler_params=_params(2))(dh2b, z, w_down)


def _ffn_up_bwd(dzp, w_up_t):
    T, F = dzp.shape
    D = w_up_t.shape[1]
    tm, tn, tk = _tile(T, 1024), _tile(D, 1024), _tile(F, 4096)

    def body(dzp_ref, w_ref, dn_ref):
        part = _dot(dzp_ref[...], w_ref[...], NN)

        @pl.when(pl.program_id(2) == 0)
        def _():
            dn_ref[...] = part

        @pl.when(pl.program_id(2) > 0)
        def _():
            dn_ref[...] += part

    return _CHAIN.call(
        body, name="ffn_up_bwd", grid=(T // tm, D // tn, F // tk),
        in_specs=[pl.BlockSpec((tm, tk), lambda i, j, k: (i, k)), pl.BlockSpec((tk, tn), lambda i, j, k: (k, j))],
        out_specs=pl.BlockSpec((tm, tn), lambda i, j, k: (i, j)),
        out_shape=SDS((T, D), F32), compiler_params=_params(3))(dzp, w_up_t)


def _ffn_norm_bwd(dn, dh2, h1, g):
    T, D = h1.shape
    tm = _tile(T, 512)

    def body(dn_ref, dh_ref, h_ref, g_ref, dh1_ref, dh1b_ref, dg_ref):
        @pl.when(pl.program_id(0) == 0)
        def _():
            dg_ref[...] = jnp.zeros_like(dg_ref)

        hv, dnv = h_ref[...], dn_ref[...]
        r = _rms_stats(hv)
        dg_ref[...] += jnp.sum(dnv * (hv * r), axis=0, keepdims=True)
        dh1 = dh_ref[...] + _rms_bwd(dnv, hv, r, g_ref[...])
        dh1_ref[...] = dh1
        dh1b_ref[...] = dh1.astype(BF16)

    row = pl.BlockSpec((tm, D), lambda i: (i, 0))
    vec = pl.BlockSpec((1, D), lambda i: (0, 0))
    return _CHAIN.call(
        body, name="ffn_norm_bwd", grid=(T // tm,), in_specs=[row, row, row, vec], out_specs=[row, row, vec],
        out_shape=[SDS((T, D), F32), SDS((T, D), BF16), SDS((1, D), F32)], compiler_params=_params(1))(dn, dh2, h1, g)


def _matmul_tn(a, b, name, square_a=False, col_blocks=None):
    T, K = a.shape
    N = b.shape[1]
    tk = _tile(K, 1792)
    tn = _tile(N if col_blocks is None else N // col_blocks, 1024 if tk <= 1024 else 512)

    def body(a_ref, b_ref, o_ref):
        av = a_ref[...]
        if square_a:
            af = av.astype(F32)
            av = (af * af).astype(BF16)
        o_ref[...] = _dot(av, b_ref[...], TN).astype(o_ref.dtype)

    if col_blocks is None:
        out_shape = SDS((K, N), BF16)
        out_spec = pl.BlockSpec((tk, tn), lambda i, j: (i, j))
    else:
        per = (N // col_blocks) // tn
        out_shape = SDS((col_blocks, K, N // col_blocks), BF16)
        out_spec = pl.BlockSpec((None, tk, tn), lambda i, j: (j // per, i, j % per))
    return _CHAIN.call(
        body, name=name, grid=(K // tk, N // tn),
        in_specs=[pl.BlockSpec((T, tk), lambda i, j: (0, i)), pl.BlockSpec((T, tn), lambda i, j: (0, j))],
        out_specs=out_spec, out_shape=out_shape, compiler_params=_params(2))(a, b)


def _outproj_bwd(dh1b, w, a, b, ga, gb):
    T, D = dh1b.shape
    A, B = a.shape[1], b.shape[1]
    tm = _tile(T, 512)

    def body(dh_ref, w_ref, a_ref, b_ref, ga_ref, gb_ref, da_ref, db_ref, dga_ref, dgb_ref):
        @pl.when(pl.program_id(0) == 0)
        def _():
            dga_ref[...] = jnp.zeros_like(dga_ref)
            dgb_ref[...] = jnp.zeros_like(dgb_ref)

        dmix = _dot(dh_ref[...], w_ref[...], NT)
        for src_ref, g_ref, dx_ref, dg_ref, dn in ((a_ref, ga_ref, da_ref, dga_ref, dmix[:, :A]),
                                                   (b_ref, gb_ref, db_ref, dgb_ref, dmix[:, A:])):
            xv = src_ref[...]
            r = _rms_stats(xv)
            dg_ref[...] += jnp.sum(dn * (xv * r), axis=0, keepdims=True)
            dx_ref[...] = _rms_bwd(dn, xv, r, g_ref[...])

    return _CHAIN.call(
        body, name="outproj_bwd", grid=(T // tm,),
        in_specs=[pl.BlockSpec((tm, D), lambda i: (i, 0)), _resident((A + B, D)),
                  pl.BlockSpec((tm, A), lambda i: (i, 0)), pl.BlockSpec((tm, B), lambda i: (i, 0)),
                  pl.BlockSpec((1, A), lambda i: (0, 0)), pl.BlockSpec((1, B), lambda i: (0, 0))],
        out_specs=[pl.BlockSpec((tm, A), lambda i: (i, 0)), pl.BlockSpec((tm, B), lambda i: (i, 0)),
                   pl.BlockSpec((1, A), lambda i: (0, 0)), pl.BlockSpec((1, B), lambda i: (0, 0))],
        out_shape=[SDS((T, A), F32), SDS((T, B), F32), SDS((1, A), F32), SDS((1, B), F32)],
        compiler_params=_params(1))(dh1b, w, a, b, ga, gb)


def _gmlp_bwd(proj, da, lg, lb, w_s, w_st, bs_t, A):
    T = proj.shape[0]
    G = A // GROUP_DIM
    tm = _tile(T, 512)
    nc = tm // CHUNK

    def body(u_ref, v_ref, da_ref, lg_ref, lb_ref, w_ref, wt_ref, bst_ref, duv_ref, dlg_ref, dlb_ref, dw_ref, dbs_ref):
        @pl.when(pl.program_id(0) == 0)
        def _():
            dlg_ref[...] = jnp.zeros_like(dlg_ref)
            dlb_ref[...] = jnp.zeros_like(dlb_ref)
            dw_ref[...] = jnp.zeros_like(dw_ref)
            dbs_ref[...] = jnp.zeros_like(dbs_ref)

        row = lax.broadcasted_iota(jnp.int32, (CHUNK, CHUNK), 0)
        col = lax.broadcasted_iota(jnp.int32, (CHUNK, CHUNK), 1)
        lower = row >= col
        upper = row <= col
        for g in range(G):
            sl = slice(g * GROUP_DIM, (g + 1) * GROUP_DIM)
            lgv = lg_ref[:, sl]
            vg, vg_grad = _gelu_and_grad(v_ref[:, sl])
            vhat, rstd, vn = _layer_norm_group(vg, lgv, lb_ref[:, sl])
            vnb = vn.astype(BF16)
            ug, ug_grad = _gelu_and_grad(u_ref[:, sl])
            dav = da_ref[:, sl]
            wm = jnp.where(lower, w_ref[g], 0.0).astype(BF16)
            wmt = jnp.where(upper, wt_ref[g], 0.0).astype(BF16)
            bcol = bst_ref[:, g:g + 1]
            dw_acc = jnp.zeros((CHUNK, CHUNK), F32)
            dbs_acc = jnp.zeros((CHUNK, 1), F32)
            dvn_parts = []
            dug_parts = []
            for c in range(nc):
                rs = slice(c * CHUNK, (c + 1) * CHUNK)
                mixed = _dot(wm, vnb[rs], NN) + bcol
                dug_parts.append(dav[rs] * mixed)
                dmix = dav[rs] * ug[rs]
                dbs_acc = dbs_acc + jnp.sum(dmix, axis=-1, keepdims=True)
                dmixb = dmix.astype(BF16)
                dw_acc = dw_acc + _dot(dmixb, vnb[rs], NT)
                dvn_parts.append(_dot(wmt, dmixb, NN))
            dvn = jnp.concatenate(dvn_parts, axis=0)
            dug = jnp.concatenate(dug_parts, axis=0)
            dw_ref[g] += jnp.where(lower, dw_acc, 0.0)
            dbs_ref[:, g:g + 1] += dbs_acc
            dlg_ref[:, sl] += jnp.sum(dvn * vhat, axis=0, keepdims=True)
            dlb_ref[:, sl] += jnp.sum(dvn, axis=0, keepdims=True)
            dvhat = dvn * lgv
            dvg = rstd * (dvhat - jnp.mean(dvhat, axis=-1, keepdims=True)
                          - vhat * jnp.mean(dvhat * vhat, axis=-1, keepdims=True))
            duv_ref[:, sl] = (dug * ug_grad).astype(BF16)
            duv_ref[:, A + g * GROUP_DIM:A + (g + 1) * GROUP_DIM] = (dvg * vg_grad).astype(BF16)

    return _CHAIN.call(
        body, name="gmlp_bwd", grid=(T // tm,),
        in_specs=[pl.BlockSpec((tm, A), lambda i: (i, 0)), pl.BlockSpec((tm, A), lambda i: (i, 1)),
                  pl.BlockSpec((tm, A), lambda i: (i, 0)),
                  pl.BlockSpec((1, A), lambda i: (0, 0)), pl.BlockSpec((1, A), lambda i: (0, 0)),
                  pl.BlockSpec((G, CHUNK, CHUNK), lambda i: (0, 0, 0)),
                  pl.BlockSpec((G, CHUNK, CHUNK), lambda i: (0, 0, 0)), pl.BlockSpec((CHUNK, G), lambda i: (0, 0))],
        out_specs=[pl.BlockSpec((tm, 2 * A), lambda i: (i, 0)),
                   pl.BlockSpec((1, A), lambda i: (0, 0)), pl.BlockSpec((1, A), lambda i: (0, 0)),
                   pl.BlockSpec((G, CHUNK, CHUNK), lambda i: (0, 0, 0)), pl.BlockSpec((CHUNK, G), lambda i: (0, 0))],
        out_shape=[SDS((T, 2 * A), BF16), SDS((1, A), F32), SDS((1, A), F32),
                   SDS((G, CHUNK, CHUNK), F32), SDS((CHUNK, G), F32)],
        compiler_params=_params(1))(proj, proj, da, lg, lb, w_s, w_st, bs_t)


def _attn_bwd(proj, do, duv, bias_t, sinks, A, B):
    T, P = proj.shape
    H = B // HEAD_DIM
    qpk = H // KV_HEADS
    tq = _tile(T, 512)
    nb = tq // CHUNK
    n_tiles = T // tq
    scale = HEAD_DIM ** -0.5
    rev = lambda i: n_tiles - 1 - i

    def body(sink_ref, q_ref, k_ref, v_ref, kp_ref, vp_ref, do_ref, duv_ref, bias_ref,
             dproj_ref, dbias_ref, dsink_ref, carry, dkv, sacc):
        step = pl.program_id(0)

        @pl.when(step == 0)
        def _():
            carry[...] = jnp.zeros_like(carry)
            sacc[...] = jnp.zeros_like(sacc)
            dbias_ref[...] = jnp.zeros_like(dbias_ref)

        jj = lax.broadcasted_iota(jnp.int32, (2 * CHUNK, CHUNK), 0)
        ii = lax.broadcasted_iota(jnp.int32, (2 * CHUNK, CHUNK), 1)
        in_window = (jj > ii) & (jj <= ii + CHUNK)
        first_mask = in_window & jnp.logical_or(step != n_tiles - 1, jj >= CHUNK)
        low_query = lax.broadcasted_iota(jnp.int32, (CHUNK, LANE), 1) < HEAD_DIM
        low_key = lax.broadcasted_iota(jnp.int32, (2 * CHUNK, LANE), 1) < HEAD_DIM

        def split_pair(pair_bf16):
            zero = jnp.zeros_like(pair_bf16)
            return jnp.concatenate([jnp.where(low_query, pair_bf16, zero), jnp.where(low_query, zero, pair_bf16)], axis=0)

        dproj_ref[:, :2 * A] = duv_ref[...]
        dkv[...] = jnp.zeros_like(dkv)
        for b in range(nb):
            rows = slice(b * CHUNK, (b + 1) * CHUNK)
            band = slice(b * CHUNK, (b + 2) * CHUNK)
            if b == 0:
                kprev, vprev, mask = kp_ref[...], vp_ref[...], first_mask
            else:
                prows = slice((b - 1) * CHUNK, b * CHUNK)
                kprev, vprev, mask = k_ref[prows, :], v_ref[prows, :], in_window
            kband = jnp.concatenate([kprev, k_ref[rows, :]], axis=0)
            vband = jnp.concatenate([vprev, v_ref[rows, :]], axis=0)
            k_pads = [_pad_heads(kband, g) for g in range(KV_HEADS)]
            v_pads = [_pad_heads(vband, g) for g in range(KV_HEADS)]
            queries, douts, scores, dprobs = [], [], [], []
            for pair in range(H // 2):
                cols = slice(2 * pair * HEAD_DIM, (2 * pair + 2) * HEAD_DIM)
                qs = (q_ref[rows, cols] * scale).astype(BF16)
                dob = do_ref[rows, cols].astype(BF16)
                queries.append(qs)
                douts.append(dob)
                scores += [_dot(kz, qs, NT) for kz in k_pads[2 * pair // qpk]]
                dprobs += [_dot(vz, dob, NT) for vz in v_pads[2 * pair // qpk]]
            probs, dscores = [], []
            for h in range(H):
                pt, p_sink = _softmax_with_sink(jnp.where(mask, scores[h] + bias_ref[h], NEG), sink_ref[h], 0)
                delta = jnp.sum(pt * dprobs[h], axis=0, keepdims=True)
                dst = pt * (dprobs[h] - delta)
                dbias_ref[h] += dst
                sacc[h:h + 1, :] += -(p_sink * delta)
                probs.append(pt.astype(BF16))
                dscores.append(dst.astype(BF16))
            dq_parts, dk_groups, dv_groups = [], [], []
            for g in range(KV_HEADS):
                k_both = jnp.concatenate(k_pads[g], axis=0)
                dk_acc = jnp.zeros((2 * CHUNK, LANE), F32)
                dv_acc = jnp.zeros((2 * CHUNK, LANE), F32)
                for pair in range(g * qpk // 2, (g + 1) * qpk // 2):
                    pair_heads = slice(2 * pair, 2 * pair + 2)
                    dk_acc = dk_acc + _dot(jnp.concatenate(dscores[pair_heads], axis=1), split_pair(queries[pair]), NN)
                    dv_acc = dv_acc + _dot(jnp.concatenate(probs[pair_heads], axis=1), split_pair(douts[pair]), NN)
                    dq_parts.append(_dot(jnp.concatenate(dscores[pair_heads], axis=0), k_both, TN) * scale)
                dk_groups.append(dk_acc + pltpu.roll(dk_acc, HEAD_DIM, 1))
                dv_groups.append(dv_acc + pltpu.roll(dv_acc, HEAD_DIM, 1))
            dkv[band, :LANE] += jnp.where(low_key, dk_groups[0], dk_groups[1])
            dkv[band, LANE:] += jnp.where(low_key, dv_groups[0], dv_groups[1])
            dproj_ref[rows, 2 * A:2 * A + B] = jnp.concatenate(dq_parts, axis=1).astype(BF16)
        last = slice(tq, tq + CHUNK)
        dkv[last, :] += carry[...]
        dproj_ref[:, 2 * A + B:] = dkv[CHUNK:, :].astype(BF16)
        carry[...] = dkv[:CHUNK, :]

        @pl.when(step == n_tiles - 1)
        def _():
            dsink_ref[...] = jnp.sum(sacc[...], axis=1, keepdims=True)

    specs = _attn_specs(tq, A, B, reverse_tiles=n_tiles)
    return _CHAIN.call(
        body, name="attn_bwd", grid=(n_tiles,),
        in_specs=[pl.BlockSpec(memory_space=pltpu.SMEM)] + specs
        + [pl.BlockSpec((tq, B), lambda i: (rev(i), 0)), pl.BlockSpec((tq, 2 * A), lambda i: (rev(i), 0)),
           pl.BlockSpec((H, 2 * CHUNK, CHUNK), lambda i: (0, 0, 0))],
        out_specs=[pl.BlockSpec((tq, P), lambda i: (rev(i), 0)),
                   pl.BlockSpec((H, 2 * CHUNK, CHUNK), lambda i: (0, 0, 0)), pl.BlockSpec((H, 1), lambda i: (0, 0))],
        out_shape=[SDS((T, P), BF16), SDS((H, 2 * CHUNK, CHUNK), F32), SDS((H, 1), F32)],
        scratch_shapes=[pltpu.VMEM((CHUNK, 2 * LANE), F32), pltpu.VMEM((tq + CHUNK, 2 * LANE), F32),
                        pltpu.VMEM((H, LANE), F32)],
        compiler_params=_params(1))(sinks, proj, proj, proj, proj, proj, do, duv, bias_t)


def _bias_bwd(dbias, onehot):
    H = dbias.shape[0]
    nbk = onehot.shape[1]

    def body(d_ref, oh_ref, o_ref):
        hi, mid, lo = _split3(d_ref[...])
        oh = oh_ref[...]
        o_ref[...] = _dot(hi, oh, NN) + _dot(mid, oh, NN) + _dot(lo, oh, NN)

    return _CHAIN.call(body, name="bias_bwd", in_specs=[VMEM_SPEC] * 2, out_specs=VMEM_SPEC, out_shape=SDS((H, nbk), F32),
                       compiler_params=_params(0))(dbias, onehot)


def _inproj_bwd(dproj, w_t, x, dh1, g):
    T, P = dproj.shape
    D = x.shape[1]
    tm = _tile(T, 512)

    def body(dp_ref, w_ref, x_ref, dh_ref, g_ref, dx_ref, dg_ref):
        @pl.when(pl.program_id(0) == 0)
        def _():
            dg_ref[...] = jnp.zeros_like(dg_ref)

        dn = _dot(dp_ref[...], w_ref[...], NN)
        xv = x_ref[...]
        r = _rms_stats(xv)
        dg_ref[...] += jnp.sum(dn * (xv * r), axis=0, keepdims=True)
        dx_ref[...] = dh_ref[...] + _rms_bwd(dn, xv, r, g_ref[...])

    return _CHAIN.call(
        body, name="inproj_bwd", grid=(T // tm,),
        in_specs=[pl.BlockSpec((tm, P), lambda i: (i, 0)), _resident((P, D)),
                  pl.BlockSpec((tm, D), lambda i: (i, 0)), pl.BlockSpec((tm, D), lambda i: (i, 0)),
                  pl.BlockSpec((1, D), lambda i: (0, 0))],
        out_specs=[pl.BlockSpec((tm, D), lambda i: (i, 0)), pl.BlockSpec((1, D), lambda i: (0, 0))],
        out_shape=[SDS((T, D), F32), SDS((1, D), F32)], compiler_params=_params(1))(dproj, w_t, x, dh1, g)


def _adamw(w, g, m, v):
    m = ADAM_B1 * m + (1.0 - ADAM_B1) * g
    v = ADAM_B2 * v + (1.0 - ADAM_B2) * (g * g)
    m_hat = m / (1.0 - ADAM_B1 ** ADAM_STEP)
    v_hat = v / (1.0 - ADAM_B2 ** ADAM_STEP)
    delta = -ADAM_LR * (m_hat / (jnp.sqrt(v_hat) + ADAM_EPS) + ADAM_WD * w)
    return delta, m, v


def _adam_sharded(csum, recv, w, m, v, name):
    R, C = w.shape
    tr = _tile(R, 256, 16)

    def body(own_ref, recv_ref, w_ref, m_ref, v_ref, g_ref, d_ref, nm_ref, nv_ref):
        g = own_ref[...].astype(F32)
        for r in range(3):
            g = g + recv_ref[r].astype(F32)
        delta, nm, nv = _adamw(w_ref[...], g, m_ref[...], v_ref[...])
        g_ref[...] = g
        d_ref[...] = delta
        nm_ref[...] = nm
        nv_ref[...] = nv

    blk = pl.BlockSpec((tr, C), lambda i: (i, 0))
    return _CHAIN.call(
        body, name=name, grid=(R // tr,),
        in_specs=[pl.BlockSpec((None, tr, C), lambda i: (0, i, 0)), pl.BlockSpec((3, tr, C), lambda i: (0, i, 0)),
                  blk, blk, blk],
        out_specs=[blk] * 4, out_shape=[SDS((R, C), F32)] * 4, compiler_params=_params(1))(csum, recv, w, m, v)


def _rows2d(shape):
    return (int(np.prod(shape[:-1])) if len(shape) > 1 else 1, shape[-1])


def _small_layout(shapes):
    totals, places = {}, []
    for s in shapes:
        r, w = _rows2d(s)
        off = totals.get(w, 0)
        places.append((w, off, r))
        totals[w] = off + -(-r // 8) * 8
    return {w: -(-t // 32) * 32 for w, t in totals.items()}, places


def _pack_small(arrays, totals, places):
    bufs = []
    for w, total in totals.items():
        buf = jnp.zeros((total, w), F32)
        for a, (pw, off, r) in zip(arrays, places):
            if pw == w:
                buf = lax.dynamic_update_slice(buf, a.reshape(r, w).astype(F32), (off, 0))
        bufs.append(buf)
    return bufs


def _adam_small(gathered, totals, places, ws, ms, vs):
    widths = list(totals)
    n, nw = len(places), len(widths)

    def body(*refs):
        gath, params, outs = refs[:nw], refs[nw:nw + 3 * n], refs[nw + 3 * n:]
        for p, (w, off, r) in enumerate(places):
            g_ref = gath[widths.index(w)]
            g = g_ref[0, off:off + r, :]
            for d in range(1, N_DEV):
                g = g + g_ref[d, off:off + r, :]
            delta, nm, nv = _adamw(params[p][...], g, params[n + p][...], params[2 * n + p][...])
            for k, val in enumerate((g, delta, nm, nv)):
                outs[4 * p + k][...] = val

    shapes2d = [SDS((r, w), F32) for w, _, r in places for _ in range(4)]
    outs = _CHAIN.call(body, name="adam_small", in_specs=[VMEM_SPEC] * (nw + 3 * n), out_specs=[VMEM_SPEC] * (4 * n),
                       out_shape=shapes2d, compiler_params=_params(0))(*gathered, *ws, *ms, *vs)
    return [outs[4 * p:4 * p + 4] for p in range(n)]


def kernel(x, rel_bias_table, mix_norm_g, w_in, gate_norm_g, gate_norm_b, w_spatial, b_spatial, attn_sinks, out_norm_a_g, out_norm_b_g, w_out, ffn_norm_g, w_up, w_down, final_norm_g, loss_target, m_rel_bias_table, m_mix_norm_g, m_w_in, m_gate_norm_g, m_gate_norm_b, m_w_spatial, m_b_spatial, m_attn_sinks, m_out_norm_a_g, m_out_norm_b_g, m_w_out, m_ffn_norm_g, m_w_up, m_w_down, m_final_norm_g, v_rel_bias_table, v_mix_norm_g, v_w_in, v_gate_norm_g, v_gate_norm_b, v_w_spatial, v_b_spatial, v_attn_sinks, v_out_norm_a_g, v_out_norm_b_g, v_w_out, v_ffn_norm_g, v_w_up, v_w_down, v_final_norm_g):
    T, D = x.shape[1], x.shape[2]
    A = D // 2
    B = D // 2
    G = A // GROUP_DIM
    H = B // HEAD_DIM
    P = 2 * A + B + 2 * KV_HEADS * HEAD_DIM
    xs = x.reshape(T, D)
    target = loss_target.reshape(T, D)

    win_t, m_win_t, v_win_t = (jnp.swapaxes(a[0], 0, 1) for a in (w_in, m_w_in, v_w_in))
    shards = [win_t.astype(BF16), w_out[0].astype(BF16), w_up[0].astype(BF16), w_down[0].astype(BF16)]
    _CHAIN.token = None
    gather = _gather_begin(shards)
    _gather_step(gather, [(0, 0)], "gather_start")

    g1, g2, g3 = mix_norm_g.reshape(1, D), ffn_norm_g.reshape(1, D), final_norm_g.reshape(1, D)
    lg, lb = gate_norm_g.reshape(1, A), gate_norm_b.reshape(1, A)
    ws = w_spatial[0]
    ws_t = jnp.swapaxes(ws, 1, 2)
    bs_t = jnp.transpose(b_spatial[0])
    ga, gb = out_norm_a_g.reshape(1, A), out_norm_b_g.reshape(1, B)
    sinks = attn_sinks.reshape(H)
    bucket, in_window = _t5_bucket()
    onehot_np = ((bucket[:, :, None] == np.arange(N_BUCKETS)) & in_window[:, :, None]).astype(np.float32)
    onehot = jnp.asarray(onehot_np.reshape(-1, N_BUCKETS)).astype(BF16)
    onehot_kq = jnp.asarray(onehot_np.transpose(1, 0, 2).reshape(-1, N_BUCKETS)).astype(BF16)

    bias, bias_t = _bias_fwd(jnp.transpose(rel_bias_table), jnp.transpose(onehot), jnp.transpose(onehot_kq))
    bias, bias_t = bias.reshape(H, CHUNK, 2 * CHUNK), bias_t.reshape(H, 2 * CHUNK, CHUNK)
    n1 = _mix_norm(xs, g1)
    _gather_step(gather, [(0, 1), (1, 0), (2, 0)], "gather_in_1")
    _gather_step(gather, [(0, 2)], "gather_in_2")
    (win_g,) = _gather_end(gather, [0], "gather_in_end")
    win_t_full = win_g.reshape(P, D)
    proj = _inproj_fwd(n1, win_t_full)
    _gather_step(gather, [(1, 1)], "gather_out_1")
    a_out = _gmlp_fwd(proj, lg, lb, ws, bs_t, A)
    _gather_step(gather, [(1, 2), (2, 1), (3, 0)], "gather_out_2_up_1")
    b_out = _attn_fwd(proj, bias, sinks, A, B)
    (wout_g,) = _gather_end(gather, [1], "gather_out_end")
    _gather_step(gather, [(2, 2)], "gather_up_2")
    wout_full = wout_g.reshape(A + B, D)
    h1, mixed, n2 = _outproj_fwd(a_out, b_out, ga, gb, xs, wout_full, g2)
    (wup_g,) = _gather_end(gather, [2], "gather_up_end")
    _gather_step(gather, [(3, 1)], "gather_down_1")
    wup_t = jnp.transpose(wup_g, (0, 2, 1)).reshape(-1, D)
    z = _ffn_up(n2, wup_g)
    _gather_step(gather, [(3, 2)], "gather_down_2")
    (wdown_g,) = _gather_end(gather, [3], "gather_down_end")
    h2 = _ffn_down(h1, z, wdown_g.reshape(-1, D))
    loss_part, dg3, dh2, dh2b = _final_loss(h2, g3, target)

    def reduce_to_chip(state, name):
        csums = [_chip_sum(part, received, "%s_chip_sum_%d" % (name, a))
                 for a, (part, received) in enumerate(_sibling_exchange_end(state, name + "_sib_end"))]
        return _chip_exchange_begin(csums, name + "_chip")

    dwdown = _matmul_tn(z, dh2b, "grad_w_down", square_a=True).reshape(wdown_g.shape)
    dzp = _ffn_down_bwd(dh2b, z, wdown_g.reshape(-1, D))
    dwup = _matmul_tn(n2, dzp, "grad_w_up", col_blocks=N_DEV)
    sib_ffn = _sibling_exchange_begin([dwdown, dwup], "rs_ffn_sib")
    dh1, dh1b, dg2 = _ffn_norm_bwd(_ffn_up_bwd(dzp, wup_t), dh2, h1, g2)
    chip_ffn = reduce_to_chip(sib_ffn, "rs_ffn")
    da, db, dga, dgb = _outproj_bwd(dh1b, wout_full, a_out, b_out, ga, gb)
    dwout = _matmul_tn(mixed, dh1b, "grad_w_out").reshape(wout_g.shape)
    sib_out = _sibling_exchange_begin([dwout], "rs_out_sib")
    duv, dlg, dlb, dws, dbs_t = _gmlp_bwd(proj, da, lg, lb, ws, ws_t, bs_t, A)
    dproj, dbias_t, dsinks = _attn_bwd(proj, db, duv, bias_t, sinks, A, B)
    chip_out = reduce_to_chip(sib_out, "rs_out")
    dtable_t = _bias_bwd(dbias_t.reshape(H, -1), onehot_kq)
    dwin_t = _matmul_tn(dproj, n1, "grad_w_in").reshape(win_g.shape)
    sib_in = _sibling_exchange_begin([dwin_t], "rs_in_sib")
    grad_x, dg1 = _inproj_bwd(dproj, win_t_full, xs, dh1, g1)

    small_w = [rel_bias_table, mix_norm_g, gate_norm_g, gate_norm_b, w_spatial, b_spatial, attn_sinks,
               out_norm_a_g, out_norm_b_g, ffn_norm_g, final_norm_g]
    small_m = [m_rel_bias_table, m_mix_norm_g, m_gate_norm_g, m_gate_norm_b, m_w_spatial, m_b_spatial, m_attn_sinks,
               m_out_norm_a_g, m_out_norm_b_g, m_ffn_norm_g, m_final_norm_g]
    small_v = [v_rel_bias_table, v_mix_norm_g, v_gate_norm_g, v_gate_norm_b, v_w_spatial, v_b_spatial, v_attn_sinks,
               v_out_norm_a_g, v_out_norm_b_g, v_ffn_norm_g, v_final_norm_g]
    small_g = [jnp.transpose(dtable_t), dg1, dlg, dlb, dws, jnp.transpose(dbs_t), dsinks, dga, dgb, dg2, dg3]
    nothing = jnp.zeros((1, H), F32)
    small_w, small_m, small_v = small_w + [nothing], small_m + [nothing], small_v + [nothing]
    small_g = small_g + [jnp.broadcast_to(loss_part, (1, H))]
    shapes = [w.shape for w in small_w]
    totals, places = _small_layout(shapes)
    as_rows = lambda arrays: [a.reshape(_rows2d(a.shape)) for a in arrays]
    big = [None] * 4

    def adam_of(k, state, a, w, m, v):
        csum, received = _chip_exchange_end(state, a, "rs_%d_end" % k)
        big[k] = _adam_sharded(csum, received, w, m, v, "adam_%d" % k)

    small_gather = _gather_begin(_pack_small(small_g, totals, places))
    every = range(len(totals))
    _gather_step(small_gather, [(a, 0) for a in every], "small_gather_start")
    ((part, received),) = _sibling_exchange_end(sib_in, "rs_in_sib_end")
    csum_in = _chip_sum(part, received, "rs_in_chip_sum")
    _gather_step(small_gather, [(a, 1) for a in every], "small_gather_1")
    chip_in = _chip_exchange_begin([csum_in], "rs_in_chip")
    adam_of(3, chip_ffn, 0, w_down[0], m_w_down[0], v_w_down[0])
    _gather_step(small_gather, [(a, 2) for a in every], "small_gather_2")
    adam_of(2, chip_ffn, 1, w_up[0], m_w_up[0], v_w_up[0])
    gathered = _gather_end(small_gather, list(every), "small_gather_end")
    small_out = _adam_small(gathered, totals, places, as_rows(small_w), as_rows(small_m), as_rows(small_v))
    sg, sd, sm, sv = [[outs[k].reshape(s) for outs, s in zip(small_out, shapes)] for k in range(4)]
    adam_of(1, chip_out, 0, w_out[0], m_w_out[0], v_w_out[0])
    adam_of(0, chip_in, 0, win_t, m_win_t, v_win_t)
    big[0] = [jnp.swapaxes(o, 0, 1) for o in big[0]]
    big = [[o.reshape(w.shape) for o in outs] for outs, w in zip(big, (w_in, w_out, w_up, w_down))]

    loss = sg[-1][0, 0]

    order = ["s0", "s1", "b0", "s2", "s3", "s4", "s5", "s6", "s7", "s8", "b1", "s9", "b2", "b3", "s10"]

    def group(idx):
        small = (sg, sd, sm, sv)[idx]
        return [small[int(t[1:])] if t[0] == "s" else big[int(t[1:])][idx] for t in order]

    return (loss, grad_x.reshape(x.shape), *group(0), *group(1), *group(2), *group(3))
```

```python
import functools
import math

import numpy as np
import jax
import jax.numpy as jnp
from jax import lax
from jax.experimental import pallas as pl
from jax.experimental.pallas import tpu as pltpu

F32 = jnp.float32
BF16 = jnp.bfloat16
SDS = jax.ShapeDtypeStruct
MESH = pl.DeviceIdType.MESH

N_DEV = 8
EPS = 1e-5
NEG = -1e30
CHUNK = 128
GROUP_DIM = 128
HEAD_DIM = 64
KV_HEADS = 2
N_BUCKETS = 32
MAX_DISTANCE = 128
ADAM_LR, ADAM_B1, ADAM_B2, ADAM_EPS, ADAM_WD, ADAM_STEP = 0.001, 0.9, 0.999, 1e-08, 0.01, 10
GELU_C0 = math.sqrt(2.0 / math.pi)
GELU_C1 = 0.044715

V7X_VMEM_BYTES = 64 * 1024 * 1024
VMEM_LIMIT = V7X_VMEM_BYTES - 8 * 1024 * 1024
LANE = 128

NN = ((1,), (0,))
NT = ((1,), (1,))
TN = ((0,), (0,))


def _dot(a, b, dims):
    return lax.dot_general(a, b, (dims, ((), ())), preferred_element_type=F32)


def _tile(n, pref, unit=LANE):
    best = None
    for t in range(unit, min(n, pref) + 1, unit):
        if n % t == 0:
            best = t
    return n if best is None else best


def _params(n_grid):
    return pltpu.CompilerParams(dimension_semantics=("arbitrary",) * n_grid, vmem_limit_bytes=VMEM_LIMIT)


def _resident(shape):
    return pl.BlockSpec(shape, lambda i: (0, 0), pipeline_mode=pl.Buffered(1))


def _gelu(x):
    return 0.5 * x * (1.0 + jnp.tanh(GELU_C0 * (x + GELU_C1 * x * x * x)))


def _gelu_and_grad(x):
    x2 = x * x
    t = jnp.tanh(GELU_C0 * x * (1.0 + GELU_C1 * x2))
    val = 0.5 * x * (1.0 + t)
    grad = 0.5 * (1.0 + t) + 0.5 * x * (1.0 - t * t) * (GELU_C0 * (1.0 + 3.0 * GELU_C1 * x2))
    return val, grad


def _rms_stats(x):
    return lax.rsqrt(jnp.mean(x * x, axis=-1, keepdims=True) + EPS)


def _rms_bwd(dy, x, r, g):
    w = dy * g
    return r * w - x * (r * r * r) * jnp.mean(w * x, axis=-1, keepdims=True)


def _t5_bucket():
    i = np.arange(CHUNK)[:, None]
    j = np.arange(2 * CHUNK)[None, :]
    rel = np.maximum(i + CHUNK - j, 0)
    n_exact = N_BUCKETS // 2
    relf = np.maximum(rel, n_exact).astype(np.float32)
    large = n_exact + (np.log(relf / np.float32(n_exact)) / np.float32(math.log(MAX_DISTANCE / n_exact))
                       * np.float32(N_BUCKETS - n_exact)).astype(np.int32)
    large = np.minimum(large, N_BUCKETS - 1)
    bucket = np.where(rel < n_exact, rel, large)
    in_window = (i + CHUNK - j >= 0) & (i + CHUNK - j < CHUNK)
    return bucket.astype(np.int32), in_window


def _split3(x):
    hi = x.astype(BF16)
    r1 = x - hi.astype(F32)
    mid = r1.astype(BF16)
    lo = (r1 - mid.astype(F32)).astype(BF16)
    return hi, mid, lo


HBM_SPEC = pl.BlockSpec(memory_space=pltpu.HBM)


def _mesh_pos():
    return lax.axis_index("x"), lax.axis_index("y"), lax.axis_index("c")


def _dev_index(px, py, pc):
    return 4 * px + 2 * py + pc


SEM_SPEC = pl.BlockSpec(memory_space=pltpu.SEMAPHORE)
ANY_SPEC = pl.BlockSpec(memory_space=pl.ANY)
VMEM_SPEC = pl.BlockSpec(memory_space=pltpu.VMEM)
TOKEN_SPEC = VMEM_SPEC
TOKEN = SDS((8, LANE), F32)
SIDE_EFFECT = pltpu.SideEffectType.DATAFLOW_SIDE_EFFECTING


def _hbm(x):
    return pltpu.with_memory_space_constraint(x, pltpu.HBM)


class _CallChain:
    def __init__(self):
        self.token = None

    def call(self, body, *, in_specs, out_specs, out_shape, **kwargs):
        dep, n_in = self.token, len(in_specs)
        single = not isinstance(out_shape, (list, tuple))
        out_shapes = [out_shape] if single else list(out_shape)
        out_specs = [out_specs] if single else list(out_specs)
        n_out = len(out_shapes)
        n_dep = 0 if dep is None else 1
        token_spec = pl.BlockSpec((8, LANE), lambda *_: (0, 0)) if kwargs.get("grid") else VMEM_SPEC

        def chained(*refs):
            outs_at = n_in + n_dep
            body(*refs[:n_in], *refs[outs_at:outs_at + n_out], *refs[outs_at + n_out + 1:])
            token = refs[outs_at + n_out]
            token[...] = jnp.zeros_like(token)

        inner = pl.pallas_call(chained, in_specs=list(in_specs) + [ANY_SPEC] * n_dep, out_specs=out_specs + [token_spec],
                               out_shape=out_shapes + [TOKEN], **kwargs)

        def run(*operands):
            outs = inner(*operands) if dep is None else inner(*operands, dep)
            self.token = outs[n_out]
            return outs[0] if single else list(outs[:n_out])

        return run


_CHAIN = _CallChain()


def _wait_all(waits, x, y, c):
    for kind, src, dst, send_sem, recv_sem in waits:
        cp = pltpu.make_async_remote_copy(src_ref=src, dst_ref=dst, send_sem=send_sem, recv_sem=recv_sem,
                                          device_id=(x, y, c), device_id_type=MESH)
        if kind == "send":
            cp.wait_send()
        else:
            cp.wait_recv()


def _split_start(bufs, copies_of, n_sems, name, sem_sets=(), waits_of=None):
    n, ns = len(bufs), len(sem_sets)
    flat_sems = [s for pair in sem_sets for s in pair]

    def body(*refs):
        ins = refs[:n]
        sems = refs[n:n + 2 * ns]
        send_sems, recv_sems = refs[n + 2 * ns], refs[n + 2 * ns + 1]
        if waits_of is not None:
            _wait_all(waits_of(ins, [(sems[2 * i], sems[2 * i + 1]) for i in range(ns)]), *_mesh_pos())
        for src, dst, k, target in copies_of(ins):
            pltpu.make_async_remote_copy(src_ref=src, dst_ref=dst, send_sem=send_sems.at[k], recv_sem=recv_sems.at[k],
                                         device_id=target, device_id_type=MESH).start()

    outs = _CHAIN.call(
        body, name=name,
        out_shape=[pltpu.SemaphoreType.DMA((n_sems,)), pltpu.SemaphoreType.DMA((n_sems,))]
        + [pltpu.HBM(b.shape, b.dtype) for b in bufs],
        in_specs=[HBM_SPEC] * n + [SEM_SPEC] * (2 * ns), out_specs=[SEM_SPEC, SEM_SPEC] + [HBM_SPEC] * n,
        input_output_aliases={a: 2 + a for a in range(n)},
        compiler_params=pltpu.CompilerParams(has_side_effects=SIDE_EFFECT),
    )(*[_hbm(b) for b in bufs], *flat_sems)
    return outs[0], outs[1], list(outs[2:2 + n])


def _split_wait(bufs, sem_sets, waits_of, name):
    n, ns = len(bufs), len(sem_sets)
    flat_sems = [s for pair in sem_sets for s in pair]

    def body(*refs):
        ins = refs[:n]
        sems = refs[n:n + 2 * ns]
        _wait_all(waits_of(ins, [(sems[2 * i], sems[2 * i + 1]) for i in range(ns)]), *_mesh_pos())

    outs = _CHAIN.call(
        body, name=name,
        out_shape=[pltpu.HBM(b.shape, b.dtype) for b in bufs],
        in_specs=[HBM_SPEC] * n + [SEM_SPEC] * (2 * ns), out_specs=[HBM_SPEC] * n,
        input_output_aliases={a: a for a in range(n)},
        compiler_params=pltpu.CompilerParams(has_side_effects=SIDE_EFFECT),
    )(*bufs, *flat_sems)
    return list(outs)


def _gather_blocks(land):
    rows = land.shape[1]
    first = (rows // 2) // 16 * 16

    def block(px, py, pc):
        return land.at[_dev_index(px, py, pc)]

    def halves(px, py, pc):
        return (land.at[_dev_index(px, py, pc), pl.ds(0, first)], land.at[_dev_index(px, py, pc), pl.ds(first, rows - first)])

    return block, halves


def _gather_begin(shards):
    me = _dev_index(*_mesh_pos())
    lands = [lax.dynamic_update_index_in_dim(lax.empty((N_DEV,) + s.shape, s.dtype), s, me, 0) for s in shards]
    return dict(lands=lands, stage={})


STAGE_COPIES = (3, 4, 1)


def _gather_step(state, items, name):
    which = sorted({a for a, _ in items})
    at = {a: i for i, a in enumerate(which)}
    sem_sets = [state["stage"][(a, s - 1)][0] for a, s in items if s > 0]
    offset, n_sems = {}, 0
    for a, s in items:
        offset[(a, s)] = n_sems
        n_sems += STAGE_COPIES[s]

    def waits_of(ins, sems):
        x, y, c = _mesh_pos()
        out, earlier = [], 0
        for a, s in items:
            if s == 0:
                continue
            block, halves = _gather_blocks(ins[at[a]])
            send, recv = sems[earlier]
            off = state["stage"][(a, s - 1)][1]
            earlier += 1
            if s == 1:
                arrived = [(1, block(1 - x, y, c)), (2, block(x, 1 - y, c))]
            else:
                arrived = list(zip((2, 3), halves(1 - x, 1 - y, c)))
            out += [("recv", ref, ref, send.at[off + k], recv.at[off + k]) for k, ref in arrived]
        return out

    def copies_of(ins):
        x, y, c = _mesh_pos()
        sibling = (x, y, 1 - c)
        out = []
        for a, s in items:
            block, halves = _gather_blocks(ins[at[a]])
            off = offset[(a, s)]
            if s == 0:
                mine = block(x, y, c)
                out += [(mine, mine, off + 1, (1 - x, y, c)), (mine, mine, off + 2, (x, 1 - y, c)), (mine, mine, off, sibling)]
            elif s == 1:
                from_x, from_y = block(1 - x, y, c), block(x, 1 - y, c)
                out += [(halves(1 - x, y, c)[0], halves(1 - x, y, c)[0], off + 2, (x, 1 - y, c)),
                        (halves(x, 1 - y, c)[1], halves(x, 1 - y, c)[1], off + 3, (1 - x, y, c)),
                        (from_x, from_x, off, sibling), (from_y, from_y, off + 1, sibling)]
            else:
                diag = block(1 - x, 1 - y, c)
                out.append((diag, diag, off, sibling))
        return out

    send_sems, recv_sems, bufs = _split_start([state["lands"][a] for a in which], copies_of, n_sems, name,
                                              sem_sets=sem_sets, waits_of=waits_of)
    for a in which:
        state["lands"][a] = bufs[at[a]]
    for a, s in items:
        state["stage"][(a, s)] = ((send_sems, recv_sems), offset[(a, s)])


def _gather_end(state, which, name):
    sem_sets = [state["stage"][(a, s)][0] for a in which for s in range(3)]

    def waits(ins, sems):
        x, y, c = _mesh_pos()
        out = []
        for i, a in enumerate(which):
            block, halves = _gather_blocks(ins[i])
            (b_send, b_recv), (s1_send, s1_recv), (s2_send, s2_recv) = sems[3 * i:3 * i + 3]
            o0, o1, o2 = (state["stage"][(a, s)][1] for s in range(3))
            arrivals = [(block(x, y, 1 - c), b_send, b_recv, o0),
                        (block(1 - x, y, 1 - c), s1_send, s1_recv, o1), (block(x, 1 - y, 1 - c), s1_send, s1_recv, o1 + 1),
                        (block(1 - x, 1 - y, 1 - c), s2_send, s2_recv, o2)]
            mine = block(x, y, c)
            sent = [(mine, b_send, b_recv, o0 + k) for k in range(3)]
            sent += [(block(1 - x, y, c), s1_send, s1_recv, o1), (block(x, 1 - y, c), s1_send, s1_recv, o1 + 1),
                     (halves(1 - x, y, c)[0], s1_send, s1_recv, o1 + 2), (halves(x, 1 - y, c)[1], s1_send, s1_recv, o1 + 3),
                     (block(1 - x, 1 - y, c), s2_send, s2_recv, o2)]
            out += [("recv", ref, ref, s.at[k], r.at[k]) for ref, s, r, k in arrivals]
            out += [("send", ref, ref, s.at[k], r.at[k]) for ref, s, r, k in sent]
        return out

    bufs = _split_wait([state["lands"][a] for a in which], sem_sets, waits, name)
    for i, a in enumerate(which):
        state["lands"][a] = bufs[i]
    return bufs


def _sibling_exchange_begin(parts, name):
    lands = [lax.empty((4,) + p.shape[1:], p.dtype) for p in parts]
    n = len(parts)

    def copies_of(ins):
        x, y, c = _mesh_pos()
        return [(ins[a].at[2 * j + (1 - c)], ins[n + a].at[j], 4 * a + j, (x, y, 1 - c)) for a in range(n) for j in range(4)]

    send_sems, recv_sems, bufs = _split_start(list(parts) + lands, copies_of, 4 * n, name)
    return dict(bufs=bufs, sems=(send_sems, recv_sems), n=n)


def _sibling_exchange_end(state, name):
    n = state["n"]

    def waits(ins, sems):
        _, _, c = _mesh_pos()
        return [(kind, ins[a].at[2 * j + (1 - c)], ins[n + a].at[j], sems[0][0].at[4 * a + j], sems[0][1].at[4 * a + j])
                for a in range(n) for j in range(4) for kind in ("send", "recv")]

    bufs = _split_wait(state["bufs"], [state["sems"]], waits, name)
    return [(bufs[a], bufs[n + a]) for a in range(n)]


CHIP_FLIPS = (2, 1, 3)


def _chip_exchange_begin(csums, name):
    lands = [lax.empty((3,) + s.shape[1:], s.dtype) for s in csums]
    n = len(csums)

    def copies_of(ins):
        x, y, c = _mesh_pos()
        chips = [(1 - x, y), (x, 1 - y), (1 - x, 1 - y)]
        return [(ins[a].at[CHIP_FLIPS[r]], ins[n + a].at[r], 3 * a + r, (px, py, c))
                for a in range(n) for r, (px, py) in enumerate(chips)]

    send_sems, recv_sems, bufs = _split_start(list(csums) + lands, copies_of, 3 * n, name)
    return dict(bufs=bufs, sems=(send_sems, recv_sems), n=n)


def _chip_exchange_end(state, a, name):
    n = state["n"]

    def waits(ins, sems):
        return [(kind, ins[0].at[CHIP_FLIPS[r]], ins[1].at[r], sems[0][0].at[3 * a + r], sems[0][1].at[3 * a + r])
                for r in range(3) for kind in ("send", "recv")]

    csum, received = _split_wait([state["bufs"][a], state["bufs"][n + a]], [state["sems"]], waits, name)
    return csum, received


def _chip_sum(part, recv, name):
    _, R, C = part.shape
    tr = _tile(R, 1024, 16)
    place = jnp.stack([lax.axis_index("c"), 2 * lax.axis_index("x") + lax.axis_index("y")]).astype(jnp.int32)

    def body(place_ref, p_ref, r_ref, o_ref):
        o_ref[...] = (p_ref[...].astype(F32) + r_ref[...].astype(F32)).astype(o_ref.dtype)

    def chip(p, place_ref):
        return jnp.bitwise_xor(p, place_ref[1])

    grid_spec = pltpu.PrefetchScalarGridSpec(
        num_scalar_prefetch=1, grid=(4, R // tr),
        in_specs=[pl.BlockSpec((None, tr, C), lambda p, i, place_ref: (2 * chip(p, place_ref) + place_ref[0], i, 0)),
                  pl.BlockSpec((None, tr, C), lambda p, i, place_ref: (chip(p, place_ref), i, 0))],
        out_specs=pl.BlockSpec((None, tr, C), lambda p, i, place_ref: (p, i, 0)))
    return pl.pallas_call(body, name=name, grid_spec=grid_spec, out_shape=SDS((4, R, C), part.dtype),
                          compiler_params=_params(2))(place, part, recv)


def _bias_fwd(table_t, onehot_t, onehot_kq_t):
    H = table_t.shape[0]
    n = onehot_t.shape[1]

    def body(t_ref, oh_ref, oh_kq_ref, o_ref, o_kq_ref):
        hi, mid, lo = _split3(t_ref[...])
        for src, dst in ((oh_ref, o_ref), (oh_kq_ref, o_kq_ref)):
            oh = src[...]
            dst[...] = _dot(hi, oh, NN) + _dot(mid, oh, NN) + _dot(lo, oh, NN)

    return _CHAIN.call(body, name="bias_fwd", in_specs=[VMEM_SPEC] * 3, out_specs=[VMEM_SPEC] * 2,
                       out_shape=[SDS((H, n), F32)] * 2, compiler_params=_params(0))(table_t, onehot_t, onehot_kq_t)


def _mix_norm(x, g):
    T, D = x.shape
    tm = _tile(T, 512)

    def body(x_ref, g_ref, n_ref):
        xv = x_ref[...]
        n_ref[...] = (xv * _rms_stats(xv) * g_ref[...]).astype(BF16)

    row = pl.BlockSpec((tm, D), lambda i: (i, 0))
    return _CHAIN.call(body, name="mix_norm", grid=(T // tm,), in_specs=[row, pl.BlockSpec((1, D), lambda i: (0, 0))],
                       out_specs=row, out_shape=SDS((T, D), BF16), compiler_params=_params(1))(x, g)


def _inproj_fwd(n, w_t):
    T, D = n.shape
    P = w_t.shape[0]
    tm = _tile(T, 512)

    def body(n_ref, w_ref, proj_ref):
        proj_ref[...] = _dot(n_ref[...], w_ref[...], NT)

    return _CHAIN.call(
        body, name="inproj_fwd", grid=(T // tm,),
        in_specs=[pl.BlockSpec((tm, D), lambda i: (i, 0)), _resident((P, D))],
        out_specs=pl.BlockSpec((tm, P), lambda i: (i, 0)),
        out_shape=SDS((T, P), F32), compiler_params=_params(1))(n, w_t)


def _layer_norm_group(vg, lg, lb):
    mu = jnp.mean(vg, axis=-1, keepdims=True)
    xc = vg - mu
    rstd = lax.rsqrt(jnp.mean(xc * xc, axis=-1, keepdims=True) + EPS)
    vhat = xc * rstd
    return vhat, rstd, vhat * lg + lb


def _gmlp_fwd(proj, lg, lb, w_s, bs_t, A):
    T = proj.shape[0]
    G = A // GROUP_DIM
    tm = _tile(T, 512)
    nc = tm // CHUNK

    def body(u_ref, v_ref, lg_ref, lb_ref, w_ref, bst_ref, a_ref):
        row = lax.broadcasted_iota(jnp.int32, (CHUNK, CHUNK), 0)
        col = lax.broadcasted_iota(jnp.int32, (CHUNK, CHUNK), 1)
        causal = row >= col
        for g in range(G):
            sl = slice(g * GROUP_DIM, (g + 1) * GROUP_DIM)
            _, _, vn = _layer_norm_group(_gelu(v_ref[:, sl]), lg_ref[:, sl], lb_ref[:, sl])
            vnb = vn.astype(BF16)
            wm = jnp.where(causal, w_ref[g], 0.0).astype(BF16)
            ug = _gelu(u_ref[:, sl])
            bcol = bst_ref[:, g:g + 1]
            for c in range(nc):
                rs = slice(c * CHUNK, (c + 1) * CHUNK)
                a_ref[rs, sl] = ug[rs] * (_dot(wm, vnb[rs], NN) + bcol)

    return _CHAIN.call(
        body, name="gmlp_fwd", grid=(T // tm,),
        in_specs=[pl.BlockSpec((tm, A), lambda i: (i, 0)), pl.BlockSpec((tm, A), lambda i: (i, 1)),
                  pl.BlockSpec((1, A), lambda i: (0, 0)), pl.BlockSpec((1, A), lambda i: (0, 0)),
                  pl.BlockSpec((G, CHUNK, CHUNK), lambda i: (0, 0, 0)), pl.BlockSpec((CHUNK, G), lambda i: (0, 0))],
        out_specs=pl.BlockSpec((tm, A), lambda i: (i, 0)),
        out_shape=SDS((T, A), F32), compiler_params=_params(1))(proj, proj, lg, lb, w_s, bs_t)


def _attn_masks(first_tile):
    ii = lax.broadcasted_iota(jnp.int32, (CHUNK, 2 * CHUNK), 0)
    jj = lax.broadcasted_iota(jnp.int32, (CHUNK, 2 * CHUNK), 1)
    in_window = (jj > ii) & (jj <= ii + CHUNK)
    first_mask = in_window & jnp.logical_or(jnp.logical_not(first_tile), jj >= CHUNK)
    return in_window, first_mask


def _softmax_with_sink(s, sink, axis):
    m = jnp.maximum(jnp.max(s, axis=axis, keepdims=True), sink)
    p = jnp.exp(s - m)
    e_sink = jnp.exp(sink - m)
    inv = 1.0 / (jnp.sum(p, axis=axis, keepdims=True) + e_sink)
    return p * inv, e_sink * inv


def _pad_heads(band, group):
    lane = lax.broadcasted_iota(jnp.int32, band.shape, 1)
    if group == 0:
        low = jnp.where(lane < HEAD_DIM, band, 0.0)
        high = pltpu.roll(low, HEAD_DIM, 1)
    else:
        high = jnp.where(lane >= HEAD_DIM, band, 0.0)
        low = pltpu.roll(high, HEAD_DIM, 1)
    return low.astype(BF16), high.astype(BF16)


def _attn_specs(tq, A, B, reverse_tiles=None):
    nb = tq // CHUNK
    kcol = (2 * A + B) // LANE
    if reverse_tiles is None:
        tile = lambda i: i
    else:
        tile = lambda i: reverse_tiles - 1 - i
    prev = lambda i: jnp.maximum(tile(i) * nb - 1, 0)
    return [pl.BlockSpec((tq, B), lambda i: (tile(i), 2 * A // B)),
            pl.BlockSpec((tq, LANE), lambda i: (tile(i), kcol)),
            pl.BlockSpec((tq, LANE), lambda i: (tile(i), kcol + 1)),
            pl.BlockSpec((CHUNK, LANE), lambda i: (prev(i), kcol)),
            pl.BlockSpec((CHUNK, LANE), lambda i: (prev(i), kcol + 1))]


def _attn_fwd(proj, bias, sinks, A, B):
    T = proj.shape[0]
    H = B // HEAD_DIM
    qpk = H // KV_HEADS
    tq = _tile(T, 512)
    nb = tq // CHUNK

    scale = HEAD_DIM ** -0.5

    def body(sink_ref, q_ref, k_ref, v_ref, kp_ref, vp_ref, bias_ref, o_ref):
        in_window, first_mask = _attn_masks(pl.program_id(0) == 0)
        for b in range(nb):
            rows = slice(b * CHUNK, (b + 1) * CHUNK)
            if b == 0:
                kprev, vprev, mask = kp_ref[...], vp_ref[...], first_mask
            else:
                prows = slice((b - 1) * CHUNK, b * CHUNK)
                kprev, vprev, mask = k_ref[prows, :], v_ref[prows, :], in_window
            kband = jnp.concatenate([kprev, k_ref[rows, :]], axis=0)
            vband = jnp.concatenate([vprev, v_ref[rows, :]], axis=0)
            k_pads = [_pad_heads(kband, g) for g in range(KV_HEADS)]
            v_both = [jnp.concatenate(_pad_heads(vband, g), axis=0) for g in range(KV_HEADS)]
            scores = []
            for pair in range(H // 2):
                h = 2 * pair
                qs = (q_ref[rows, h * HEAD_DIM:(h + 2) * HEAD_DIM] * scale).astype(BF16)
                scores += [_dot(qs, kz, NT) for kz in k_pads[h // qpk]]
            probs = [_softmax_with_sink(jnp.where(mask, s + bias_ref[h], NEG), sink_ref[h], -1)[0].astype(BF16)
                     for h, s in enumerate(scores)]
            outs = [_dot(jnp.concatenate(probs[h:h + 2], axis=1), v_both[h // qpk], NN) for h in range(0, H, 2)]
            o_ref[rows, :] = jnp.concatenate(outs, axis=1)

    return _CHAIN.call(
        body, name="attn_fwd", grid=(T // tq,),
        in_specs=[pl.BlockSpec(memory_space=pltpu.SMEM)] + _attn_specs(tq, A, B)
        + [pl.BlockSpec((H, CHUNK, 2 * CHUNK), lambda i: (0, 0, 0))],
        out_specs=pl.BlockSpec((tq, B), lambda i: (i, 0)),
        out_shape=SDS((T, B), F32), compiler_params=_params(1))(sinks, proj, proj, proj, proj, proj, bias)


def _outproj_fwd(a, b, ga, gb, x, w, g_ffn):
    T, A = a.shape
    B = b.shape[1]
    D = x.shape[1]
    tm = _tile(T, 512)

    def body(a_ref, b_ref, ga_ref, gb_ref, x_ref, w_ref, gf_ref, h_ref, mix_ref, n_ref):
        av, bv = a_ref[...], b_ref[...]
        mix_ref[:, :A] = (av * _rms_stats(av) * ga_ref[...]).astype(BF16)
        mix_ref[:, A:] = (bv * _rms_stats(bv) * gb_ref[...]).astype(BF16)
        hv = x_ref[...] + _dot(mix_ref[...], w_ref[...], NN)
        h_ref[...] = hv
        n_ref[...] = (hv * _rms_stats(hv) * gf_ref[...]).astype(BF16)

    row = pl.BlockSpec((tm, D), lambda i: (i, 0))
    return _CHAIN.call(
        body, name="outproj_fwd", grid=(T // tm,),
        in_specs=[pl.BlockSpec((tm, A), lambda i: (i, 0)), pl.BlockSpec((tm, B), lambda i: (i, 0)),
                  pl.BlockSpec((1, A), lambda i: (0, 0)), pl.BlockSpec((1, B), lambda i: (0, 0)),
                  row, _resident((A + B, D)), pl.BlockSpec((1, D), lambda i: (0, 0))],
        out_specs=[row, pl.BlockSpec((tm, A + B), lambda i: (i, 0)), row],
        out_shape=[SDS((T, D), F32), SDS((T, A + B), BF16), SDS((T, D), BF16)],
        compiler_params=_params(1))(a, b, ga, gb, x, w, g_ffn)


def _ffn_up(n, w_up):
    T, D = n.shape
    Fb = w_up.shape[2]
    F = N_DEV * Fb
    tm, tf = _tile(T, 1024), _tile(Fb, 1024)
    per = Fb // tf

    def body(n_ref, wu_ref, z_ref):
        z_ref[...] = jnp.maximum(_dot(n_ref[...], wu_ref[...], NN), 0.0).astype(BF16)

    return _CHAIN.call(
        body, name="ffn_up", grid=(T // tm, F // tf),
        in_specs=[pl.BlockSpec((tm, D), lambda i, j: (i, 0)),
                  pl.BlockSpec((None, D, tf), lambda i, j: (j // per, 0, j % per))],
        out_specs=pl.BlockSpec((tm, tf), lambda i, j: (i, j)),
        out_shape=SDS((T, F), BF16), compiler_params=_params(2))(n, w_up)


def _ffn_down(h1, z, w_down):
    T, D = h1.shape
    F = w_down.shape[0]
    tm, tn, tk = _tile(T, 1024), _tile(D, 1024), _tile(F, 4096)

    def body(h_ref, z_ref, wd_ref, h2_ref):
        k = pl.program_id(2)

        @pl.when(k == 0)
        def _():
            h2_ref[...] = h_ref[...]

        zf = z_ref[...].astype(F32)
        h2_ref[...] += _dot((zf * zf).astype(BF16), wd_ref[...], NN)

    return _CHAIN.call(
        body, name="ffn_down", grid=(T // tm, D // tn, F // tk),
        in_specs=[pl.BlockSpec((tm, tn), lambda i, j, k: (i, j)), pl.BlockSpec((tm, tk), lambda i, j, k: (i, k)),
                  pl.BlockSpec((tk, tn), lambda i, j, k: (k, j))],
        out_specs=pl.BlockSpec((tm, tn), lambda i, j, k: (i, j)),
        out_shape=SDS((T, D), F32), compiler_params=_params(3))(h1, z, w_down)


def _final_loss(h2, g, target):
    T, D = h2.shape
    tm = _tile(T, 512)

    def body(h_ref, g_ref, t_ref, loss_ref, dg_ref, dh_ref, dhb_ref):
        @pl.when(pl.program_id(0) == 0)
        def _():
            loss_ref[...] = jnp.zeros_like(loss_ref)
            dg_ref[...] = jnp.zeros_like(dg_ref)

        hv, gv = h_ref[...], g_ref[...]
        r = _rms_stats(hv)
        hn = hv * r
        e = hn * gv - t_ref[...]
        loss_ref[...] += (0.5 / D) * jnp.sum(jnp.sum(e * e, axis=0, keepdims=True), axis=-1, keepdims=True)
        dy = e * (1.0 / D)
        dg_ref[...] += jnp.sum(dy * hn, axis=0, keepdims=True)
        dh = _rms_bwd(dy, hv, r, gv)
        dh_ref[...] = dh
        dhb_ref[...] = dh.astype(BF16)

    return _CHAIN.call(
        body, name="final_loss", grid=(T // tm,),
        in_specs=[pl.BlockSpec((tm, D), lambda i: (i, 0)), pl.BlockSpec((1, D), lambda i: (0, 0)),
                  pl.BlockSpec((tm, D), lambda i: (i, 0))],
        out_specs=[pl.BlockSpec((1, 1), lambda i: (0, 0)), pl.BlockSpec((1, D), lambda i: (0, 0)),
                   pl.BlockSpec((tm, D), lambda i: (i, 0)), pl.BlockSpec((tm, D), lambda i: (i, 0))],
        out_shape=[SDS((1, 1), F32), SDS((1, D), F32), SDS((T, D), F32), SDS((T, D), BF16)],
        compiler_params=_params(1))(h2, g, target)


def _ffn_down_bwd(dh2b, z, w_down):
    T, D = dh2b.shape
    F = w_down.shape[0]
    tm, tf = _tile(T, 1024), _tile(F, 1024)

    def body(dh_ref, z_ref, wd_ref, dzp_ref):
        dzz = _dot(dh_ref[...], wd_ref[...], NT)
        dzp_ref[...] = (dzz * (2.0 * z_ref[...].astype(F32))).astype(BF16)

    return _CHAIN.call(
        body, name="ffn_down_bwd", grid=(T // tm, F // tf),
        in_specs=[pl.BlockSpec((tm, D), lambda i, j: (i, 0)), pl.BlockSpec((tm, tf), lambda i, j: (i, j)),
                  pl.BlockSpec((tf, D), lambda i, j: (j, 0))],
        out_specs=pl.BlockSpec((tm, tf), lambda i, j: (i, j)),
        out_shape=SDS((T, F), BF16), compiler_params=_params(2))(dh2b, z, w_down)


def _ffn_up_bwd(dzp, w_up_t):
    T, F = dzp.shape
    D = w_up_t.shape[1]
    tm, tn, tk = _tile(T, 1024), _tile(D, 1024), _tile(F, 4096)

    def body(dzp_ref, w_ref, dn_ref):
        part = _dot(dzp_ref[...], w_ref[...], NN)

        @pl.when(pl.program_id(2) == 0)
        def _():
            dn_ref[...] = part

        @pl.when(pl.program_id(2) > 0)
        def _():
            dn_ref[...] += part

    return _CHAIN.call(
        body, name="ffn_up_bwd", grid=(T // tm, D // tn, F // tk),
        in_specs=[pl.BlockSpec((tm, tk), lambda i, j, k: (i, k)), pl.BlockSpec((tk, tn), lambda i, j, k: (k, j))],
        out_specs=pl.BlockSpec((tm, tn), lambda i, j, k: (i, j)),
        out_shape=SDS((T, D), F32), compiler_params=_params(3))(dzp, w_up_t)


def _ffn_norm_bwd(dn, dh2, h1, g):
    T, D = h1.shape
    tm = _tile(T, 512)

    def body(dn_ref, dh_ref, h_ref, g_ref, dh1_ref, dh1b_ref, dg_ref):
        @pl.when(pl.program_id(0) == 0)
        def _():
            dg_ref[...] = jnp.zeros_like(dg_ref)

        hv, dnv = h_ref[...], dn_ref[...]
        r = _rms_stats(hv)
        dg_ref[...] += jnp.sum(dnv * (hv * r), axis=0, keepdims=True)
        dh1 = dh_ref[...] + _rms_bwd(dnv, hv, r, g_ref[...])
        dh1_ref[...] = dh1
        dh1b_ref[...] = dh1.astype(BF16)

    row = pl.BlockSpec((tm, D), lambda i: (i, 0))
    vec = pl.BlockSpec((1, D), lambda i: (0, 0))
    return _CHAIN.call(
        body, name="ffn_norm_bwd", grid=(T // tm,), in_specs=[row, row, row, vec], out_specs=[row, row, vec],
        out_shape=[SDS((T, D), F32), SDS((T, D), BF16), SDS((1, D), F32)], compiler_params=_params(1))(dn, dh2, h1, g)


def _matmul_tn(a, b, name, square_a=False, col_blocks=None):
    T, K = a.shape
    N = b.shape[1]
    tk = _tile(K, 1792)
    tn = _tile(N if col_blocks is None else N // col_blocks, 1024 if tk <= 1024 else 512)

    def body(a_ref, b_ref, o_ref):
        av = a_ref[...]
        if square_a:
            af = av.astype(F32)
            av = (af * af).astype(BF16)
        o_ref[...] = _dot(av, b_ref[...], TN).astype(o_ref.dtype)

    if col_blocks is None:
        out_shape = SDS((K, N), BF16)
        out_spec = pl.BlockSpec((tk, tn), lambda i, j: (i, j))
    else:
        per = (N // col_blocks) // tn
        out_shape = SDS((col_blocks, K, N // col_blocks), BF16)
        out_spec = pl.BlockSpec((None, tk, tn), lambda i, j: (j // per, i, j % per))
    return _CHAIN.call(
        body, name=name, grid=(K // tk, N // tn),
        in_specs=[pl.BlockSpec((T, tk), lambda i, j: (0, i)), pl.BlockSpec((T, tn), lambda i, j: (0, j))],
        out_specs=out_spec, out_shape=out_shape, compiler_params=_params(2))(a, b)


def _outproj_bwd(dh1b, w, a, b, ga, gb):
    T, D = dh1b.shape
    A, B = a.shape[1], b.shape[1]
    tm = _tile(T, 512)

    def body(dh_ref, w_ref, a_ref, b_ref, ga_ref, gb_ref, da_ref, db_ref, dga_ref, dgb_ref):
        @pl.when(pl.program_id(0) == 0)
        def _():
            dga_ref[...] = jnp.zeros_like(dga_ref)
            dgb_ref[...] = jnp.zeros_like(dgb_ref)

        dmix = _dot(dh_ref[...], w_ref[...], NT)
        for src_ref, g_ref, dx_ref, dg_ref, dn in ((a_ref, ga_ref, da_ref, dga_ref, dmix[:, :A]),
                                                   (b_ref, gb_ref, db_ref, dgb_ref, dmix[:, A:])):
            xv = src_ref[...]
            r = _rms_stats(xv)
            dg_ref[...] += jnp.sum(dn * (xv * r), axis=0, keepdims=True)
            dx_ref[...] = _rms_bwd(dn, xv, r, g_ref[...])

    return _CHAIN.call(
        body, name="outproj_bwd", grid=(T // tm,),
        in_specs=[pl.BlockSpec((tm, D), lambda i: (i, 0)), _resident((A + B, D)),
                  pl.BlockSpec((tm, A), lambda i: (i, 0)), pl.BlockSpec((tm, B), lambda i: (i, 0)),
                  pl.BlockSpec((1, A), lambda i: (0, 0)), pl.BlockSpec((1, B), lambda i: (0, 0))],
        out_specs=[pl.BlockSpec((tm, A), lambda i: (i, 0)), pl.BlockSpec((tm, B), lambda i: (i, 0)),
                   pl.BlockSpec((1, A), lambda i: (0, 0)), pl.BlockSpec((1, B), lambda i: (0, 0))],
        out_shape=[SDS((T, A), F32), SDS((T, B), F32), SDS((1, A), F32), SDS((1, B), F32)],
        compiler_params=_params(1))(dh1b, w, a, b, ga, gb)


def _gmlp_bwd(proj, da, lg, lb, w_s, w_st, bs_t, A):
    T = proj.shape[0]
    G = A // GROUP_DIM
    tm = _tile(T, 512)
    nc = tm // CHUNK

    def body(u_ref, v_ref, da_ref, lg_ref, lb_ref, w_ref, wt_ref, bst_ref, duv_ref, dlg_ref, dlb_ref, dw_ref, dbs_ref):
        @pl.when(pl.program_id(0) == 0)
        def _():
            dlg_ref[...] = jnp.zeros_like(dlg_ref)
            dlb_ref[...] = jnp.zeros_like(dlb_ref)
            dw_ref[...] = jnp.zeros_like(dw_ref)
            dbs_ref[...] = jnp.zeros_like(dbs_ref)

        row = lax.broadcasted_iota(jnp.int32, (CHUNK, CHUNK), 0)
        col = lax.broadcasted_iota(jnp.int32, (CHUNK, CHUNK), 1)
        lower = row >= col
        upper = row <= col
        for g in range(G):
            sl = slice(g * GROUP_DIM, (g + 1) * GROUP_DIM)
            lgv = lg_ref[:, sl]
            vg, vg_grad = _gelu_and_grad(v_ref[:, sl])
            vhat, rstd, vn = _layer_norm_group(vg, lgv, lb_ref[:, sl])
            vnb = vn.astype(BF16)
            ug, ug_grad = _gelu_and_grad(u_ref[:, sl])
            dav = da_ref[:, sl]
            wm = jnp.where(lower, w_ref[g], 0.0).astype(BF16)
            wmt = jnp.where(upper, wt_ref[g], 0.0).astype(BF16)
            bcol = bst_ref[:, g:g + 1]
            dw_acc = jnp.zeros((CHUNK, CHUNK), F32)
            dbs_acc = jnp.zeros((CHUNK, 1), F32)
            dvn_parts = []
            dug_parts = []
            for c in range(nc):
                rs = slice(c * CHUNK, (c + 1) * CHUNK)
                mixed = _dot(wm, vnb[rs], NN) + bcol
                dug_parts.append(dav[rs] * mixed)
                dmix = dav[rs] * ug[rs]
                dbs_acc = dbs_acc + jnp.sum(dmix, axis=-1, keepdims=True)
                dmixb = dmix.astype(BF16)
                dw_acc = dw_acc + _dot(dmixb, vnb[rs], NT)
                dvn_parts.append(_dot(wmt, dmixb, NN))
            dvn = jnp.concatenate(dvn_parts, axis=0)
            dug = jnp.concatenate(dug_parts, axis=0)
            dw_ref[g] += jnp.where(lower, dw_acc, 0.0)
            dbs_ref[:, g:g + 1] += dbs_acc
            dlg_ref[:, sl] += jnp.sum(dvn * vhat, axis=0, keepdims=True)
            dlb_ref[:, sl] += jnp.sum(dvn, axis=0, keepdims=True)
            dvhat = dvn * lgv
            dvg = rstd * (dvhat - jnp.mean(dvhat, axis=-1, keepdims=True)
                          - vhat * jnp.mean(dvhat * vhat, axis=-1, keepdims=True))
            duv_ref[:, sl] = (dug * ug_grad).astype(BF16)
            duv_ref[:, A + g * GROUP_DIM:A + (g + 1) * GROUP_DIM] = (dvg * vg_grad).astype(BF16)

    return _CHAIN.call(
        body, name="gmlp_bwd", grid=(T // tm,),
        in_specs=[pl.BlockSpec((tm, A), lambda i: (i, 0)), pl.BlockSpec((tm, A), lambda i: (i, 1)),
                  pl.BlockSpec((tm, A), lambda i: (i, 0)),
                  pl.BlockSpec((1, A), lambda i: (0, 0)), pl.BlockSpec((1, A), lambda i: (0, 0)),
                  pl.BlockSpec((G, CHUNK, CHUNK), lambda i: (0, 0, 0)),
                  pl.BlockSpec((G, CHUNK, CHUNK), lambda i: (0, 0, 0)), pl.BlockSpec((CHUNK, G), lambda i: (0, 0))],
        out_specs=[pl.BlockSpec((tm, 2 * A), lambda i: (i, 0)),
                   pl.BlockSpec((1, A), lambda i: (0, 0)), pl.BlockSpec((1, A), lambda i: (0, 0)),
                   pl.BlockSpec((G, CHUNK, CHUNK), lambda i: (0, 0, 0)), pl.BlockSpec((CHUNK, G), lambda i: (0, 0))],
        out_shape=[SDS((T, 2 * A), BF16), SDS((1, A), F32), SDS((1, A), F32),
                   SDS((G, CHUNK, CHUNK), F32), SDS((CHUNK, G), F32)],
        compiler_params=_params(1))(proj, proj, da, lg, lb, w_s, w_st, bs_t)


def _attn_bwd(proj, do, duv, bias_t, sinks, A, B):
    T, P = proj.shape
    H = B // HEAD_DIM
    qpk = H // KV_HEADS
    tq = _tile(T, 512)
    nb = tq // CHUNK
    n_tiles = T // tq
    scale = HEAD_DIM ** -0.5
    rev = lambda i: n_tiles - 1 - i

    def body(sink_ref, q_ref, k_ref, v_ref, kp_ref, vp_ref, do_ref, duv_ref, bias_ref,
             dproj_ref, dbias_ref, dsink_ref, carry, dkv, sacc):
        step = pl.program_id(0)

        @pl.when(step == 0)
        def _():
            carry[...] = jnp.zeros_like(carry)
            sacc[...] = jnp.zeros_like(sacc)
            dbias_ref[...] = jnp.zeros_like(dbias_ref)

        jj = lax.broadcasted_iota(jnp.int32, (2 * CHUNK, CHUNK), 0)
        ii = lax.broadcasted_iota(jnp.int32, (2 * CHUNK, CHUNK), 1)
        in_window = (jj > ii) & (jj <= ii + CHUNK)
        first_mask = in_window & jnp.logical_or(step != n_tiles - 1, jj >= CHUNK)
        low_query = lax.broadcasted_iota(jnp.int32, (CHUNK, LANE), 1) < HEAD_DIM
        low_key = lax.broadcasted_iota(jnp.int32, (2 * CHUNK, LANE), 1) < HEAD_DIM

        def split_pair(pair_bf16):
            zero = jnp.zeros_like(pair_bf16)
            return jnp.concatenate([jnp.where(low_query, pair_bf16, zero), jnp.where(low_query, zero, pair_bf16)], axis=0)

        dproj_ref[:, :2 * A] = duv_ref[...]
        dkv[...] = jnp.zeros_like(dkv)
        for b in range(nb):
            rows = slice(b * CHUNK, (b + 1) * CHUNK)
            band = slice(b * CHUNK, (b + 2) * CHUNK)
            if b == 0:
                kprev, vprev, mask = kp_ref[...], vp_ref[...], first_mask
            else:
                prows = slice((b - 1) * CHUNK, b * CHUNK)
                kprev, vprev, mask = k_ref[prows, :], v_ref[prows, :], in_window
            kband = jnp.concatenate([kprev, k_ref[rows, :]], axis=0)
            vband = jnp.concatenate([vprev, v_ref[rows, :]], axis=0)
            k_pads = [_pad_heads(kband, g) for g in range(KV_HEADS)]
            v_pads = [_pad_heads(vband, g) for g in range(KV_HEADS)]
            queries, douts, scores, dprobs = [], [], [], []
            for pair in range(H // 2):
                cols = slice(2 * pair * HEAD_DIM, (2 * pair + 2) * HEAD_DIM)
                qs = (q_ref[rows, cols] * scale).astype(BF16)
                dob = do_ref[rows, cols].astype(BF16)
                queries.append(qs)
                douts.append(dob)
                scores += [_dot(kz, qs, NT) for kz in k_pads[2 * pair // qpk]]
                dprobs += [_dot(vz, dob, NT) for vz in v_pads[2 * pair // qpk]]
            probs, dscores = [], []
            for h in range(H):
                pt, p_sink = _softmax_with_sink(jnp.where(mask, scores[h] + bias_ref[h], NEG), sink_ref[h], 0)
                delta = jnp.sum(pt * dprobs[h], axis=0, keepdims=True)
                dst = pt * (dprobs[h] - delta)
                dbias_ref[h] += dst
                sacc[h:h + 1, :] += -(p_sink * delta)
                probs.append(pt.astype(BF16))
                dscores.append(dst.astype(BF16))
            dq_parts, dk_groups, dv_groups = [], [], []
            for g in range(KV_HEADS):
                k_both = jnp.concatenate(k_pads[g], axis=0)
                dk_acc = jnp.zeros((2 * CHUNK, LANE), F32)
                dv_acc = jnp.zeros((2 * CHUNK, LANE), F32)
                for pair in range(g * qpk // 2, (g + 1) * qpk // 2):
                    pair_heads = slice(2 * pair, 2 * pair + 2)
                    dk_acc = dk_acc + _dot(jnp.concatenate(dscores[pair_heads], axis=1), split_pair(queries[pair]), NN)
                    dv_acc = dv_acc + _dot(jnp.concatenate(probs[pair_heads], axis=1), split_pair(douts[pair]), NN)
                    dq_parts.append(_dot(jnp.concatenate(dscores[pair_heads], axis=0), k_both, TN) * scale)
                dk_groups.append(dk_acc + pltpu.roll(dk_acc, HEAD_DIM, 1))
                dv_groups.append(dv_acc + pltpu.roll(dv_acc, HEAD_DIM, 1))
            dkv[band, :LANE] += jnp.where(low_key, dk_groups[0], dk_groups[1])
            dkv[band, LANE:] += jnp.where(low_key, dv_groups[0], dv_groups[1])
            dproj_ref[rows, 2 * A:2 * A + B] = jnp.concatenate(dq_parts, axis=1).astype(BF16)
        last = slice(tq, tq + CHUNK)
        dkv[last, :] += carry[...]
        dproj_ref[:, 2 * A + B:] = dkv[CHUNK:, :].astype(BF16)
        carry[...] = dkv[:CHUNK, :]

        @pl.when(step == n_tiles - 1)
        def _():
            dsink_ref[...] = jnp.sum(sacc[...], axis=1, keepdims=True)

    specs = _attn_specs(tq, A, B, reverse_tiles=n_tiles)
    return _CHAIN.call(
        body, name="attn_bwd", grid=(n_tiles,),
        in_specs=[pl.BlockSpec(memory_space=pltpu.SMEM)] + specs
        + [pl.BlockSpec((tq, B), lambda i: (rev(i), 0)), pl.BlockSpec((tq, 2 * A), lambda i: (rev(i), 0)),
           pl.BlockSpec((H, 2 * CHUNK, CHUNK), lambda i: (0, 0, 0))],
        out_specs=[pl.BlockSpec((tq, P), lambda i: (rev(i), 0)),
                   pl.BlockSpec((H, 2 * CHUNK, CHUNK), lambda i: (0, 0, 0)), pl.BlockSpec((H, 1), lambda i: (0, 0))],
        out_shape=[SDS((T, P), BF16), SDS((H, 2 * CHUNK, CHUNK), F32), SDS((H, 1), F32)],
        scratch_shapes=[pltpu.VMEM((CHUNK, 2 * LANE), F32), pltpu.VMEM((tq + CHUNK, 2 * LANE), F32),
                        pltpu.VMEM((H, LANE), F32)],
        compiler_params=_params(1))(sinks, proj, proj, proj, proj, proj, do, duv, bias_t)


def _bias_bwd(dbias, onehot):
    H = dbias.shape[0]
    nbk = onehot.shape[1]

    def body(d_ref, oh_ref, o_ref):
        hi, mid, lo = _split3(d_ref[...])
        oh = oh_ref[...]
        o_ref[...] = _dot(hi, oh, NN) + _dot(mid, oh, NN) + _dot(lo, oh, NN)

    return _CHAIN.call(body, name="bias_bwd", in_specs=[VMEM_SPEC] * 2, out_specs=VMEM_SPEC, out_shape=SDS((H, nbk), F32),
                       compiler_params=_params(0))(dbias, onehot)


def _inproj_bwd(dproj, w_t, x, dh1, g):
    T, P = dproj.shape
    D = x.shape[1]
    tm = _tile(T, 512)

    def body(dp_ref, w_ref, x_ref, dh_ref, g_ref, dx_ref, dg_ref):
        @pl.when(pl.program_id(0) == 0)
        def _():
            dg_ref[...] = jnp.zeros_like(dg_ref)

        dn = _dot(dp_ref[...], w_ref[...], NN)
        xv = x_ref[...]
        r = _rms_stats(xv)
        dg_ref[...] += jnp.sum(dn * (xv * r), axis=0, keepdims=True)
        dx_ref[...] = dh_ref[...] + _rms_bwd(dn, xv, r, g_ref[...])

    return _CHAIN.call(
        body, name="inproj_bwd", grid=(T // tm,),
        in_specs=[pl.BlockSpec((tm, P), lambda i: (i, 0)), _resident((P, D)),
                  pl.BlockSpec((tm, D), lambda i: (i, 0)), pl.BlockSpec((tm, D), lambda i: (i, 0)),
                  pl.BlockSpec((1, D), lambda i: (0, 0))],
        out_specs=[pl.BlockSpec((tm, D), lambda i: (i, 0)), pl.BlockSpec((1, D), lambda i: (0, 0))],
        out_shape=[SDS((T, D), F32), SDS((1, D), F32)], compiler_params=_params(1))(dproj, w_t, x, dh1, g)


def _adamw(w, g, m, v):
    m = ADAM_B1 * m + (1.0 - ADAM_B1) * g
    v = ADAM_B2 * v + (1.0 - ADAM_B2) * (g * g)
    m_hat = m / (1.0 - ADAM_B1 ** ADAM_STEP)
    v_hat = v / (1.0 - ADAM_B2 ** ADAM_STEP)
    delta = -ADAM_LR * (m_hat / (jnp.sqrt(v_hat) + ADAM_EPS) + ADAM_WD * w)
    return delta, m, v


def _adam_sharded(csum, recv, w, m, v, name):
    R, C = w.shape
    tr = _tile(R, 256, 16)

    def body(own_ref, recv_ref, w_ref, m_ref, v_ref, g_ref, d_ref, nm_ref, nv_ref):
        g = own_ref[...].astype(F32)
        for r in range(3):
            g = g + recv_ref[r].astype(F32)
        delta, nm, nv = _adamw(w_ref[...], g, m_ref[...], v_ref[...])
        g_ref[...] = g
        d_ref[...] = delta
        nm_ref[...] = nm
        nv_ref[...] = nv

    blk = pl.BlockSpec((tr, C), lambda i: (i, 0))
    return _CHAIN.call(
        body, name=name, grid=(R // tr,),
        in_specs=[pl.BlockSpec((None, tr, C), lambda i: (0, i, 0)), pl.BlockSpec((3, tr, C), lambda i: (0, i, 0)),
                  blk, blk, blk],
        out_specs=[blk] * 4, out_shape=[SDS((R, C), F32)] * 4, compiler_params=_params(1))(csum, recv, w, m, v)


def _rows2d(shape):
    return (int(np.prod(shape[:-1])) if len(shape) > 1 else 1, shape[-1])


def _small_layout(shapes):
    totals, places = {}, []
    for s in shapes:
        r, w = _rows2d(s)
        off = totals.get(w, 0)
        places.append((w, off, r))
        totals[w] = off + -(-r // 8) * 8
    return {w: -(-t // 32) * 32 for w, t in totals.items()}, places


def _pack_small(arrays, totals, places):
    bufs = []
    for w, total in totals.items():
        buf = jnp.zeros((total, w), F32)
        for a, (pw, off, r) in zip(arrays, places):
            if pw == w:
                buf = lax.dynamic_update_slice(buf, a.reshape(r, w).astype(F32), (off, 0))
        bufs.append(buf)
    return bufs


def _adam_small(gathered, totals, places, ws, ms, vs):
    widths = list(totals)
    n, nw = len(places), len(widths)

    def body(*refs):
        gath, params, outs = refs[:nw], refs[nw:nw + 3 * n], refs[nw + 3 * n:]
        for p, (w, off, r) in enumerate(places):
            g_ref = gath[widths.index(w)]
            g = g_ref[0, off:off + r, :]
            for d in range(1, N_DEV):
                g = g + g_ref[d, off:off + r, :]
            delta, nm, nv = _adamw(params[p][...], g, params[n + p][...], params[2 * n + p][...])
            for k, val in enumerate((g, delta, nm, nv)):
                outs[4 * p + k][...] = val

    shapes2d = [SDS((r, w), F32) for w, _, r in places for _ in range(4)]
    outs = _CHAIN.call(body, name="adam_small", in_specs=[VMEM_SPEC] * (nw + 3 * n), out_specs=[VMEM_SPEC] * (4 * n),
                       out_shape=shapes2d, compiler_params=_params(0))(*gathered, *ws, *ms, *vs)
    return [outs[4 * p:4 * p + 4] for p in range(n)]


def kernel(x, rel_bias_table, mix_norm_g, w_in, gate_norm_g, gate_norm_b, w_spatial, b_spatial, attn_sinks, out_norm_a_g, out_norm_b_g, w_out, ffn_norm_g, w_up, w_down, final_norm_g, loss_target, m_rel_bias_table, m_mix_norm_g, m_w_in, m_gate_norm_g, m_gate_norm_b, m_w_spatial, m_b_spatial, m_attn_sinks, m_out_norm_a_g, m_out_norm_b_g, m_w_out, m_ffn_norm_g, m_w_up, m_w_down, m_final_norm_g, v_rel_bias_table, v_mix_norm_g, v_w_in, v_gate_norm_g, v_gate_norm_b, v_w_spatial, v_b_spatial, v_attn_sinks, v_out_norm_a_g, v_out_norm_b_g, v_w_out, v_ffn_norm_g, v_w_up, v_w_down, v_final_norm_g):
    T, D = x.shape[1], x.shape[2]
    A = D // 2
    B = D // 2
    G = A // GROUP_DIM
    H = B // HEAD_DIM
    P = 2 * A + B + 2 * KV_HEADS * HEAD_DIM
    xs = x.reshape(T, D)
    target = loss_target.reshape(T, D)

    win_t, m_win_t, v_win_t = (jnp.swapaxes(a[0], 0, 1) for a in (w_in, m_w_in, v_w_in))
    shards = [win_t.astype(BF16), w_out[0].astype(BF16), w_up[0].astype(BF16), w_down[0].astype(BF16)]
    _CHAIN.token = None
    gather = _gather_begin(shards)
    _gather_step(gather, [(0, 0)], "gather_start")

    g1, g2, g3 = mix_norm_g.reshape(1, D), ffn_norm_g.reshape(1, D), final_norm_g.reshape(1, D)
    lg, lb = gate_norm_g.reshape(1, A), gate_norm_b.reshape(1, A)
    ws = w_spatial[0]
    ws_t = jnp.swapaxes(ws, 1, 2)
    bs_t = jnp.transpose(b_spatial[0])
    ga, gb = out_norm_a_g.reshape(1, A), out_norm_b_g.reshape(1, B)
    sinks = attn_sinks.reshape(H)
    bucket, in_window = _t5_bucket()
    onehot_np = ((bucket[:, :, None] == np.arange(N_BUCKETS)) & in_window[:, :, None]).astype(np.float32)
    onehot = jnp.asarray(onehot_np.reshape(-1, N_BUCKETS)).astype(BF16)
    onehot_kq = jnp.asarray(onehot_np.transpose(1, 0, 2).reshape(-1, N_BUCKETS)).astype(BF16)

    bias, bias_t = _bias_fwd(jnp.transpose(rel_bias_table), jnp.transpose(onehot), jnp.transpose(onehot_kq))
    bias, bias_t = bias.reshape(H, CHUNK, 2 * CHUNK), bias_t.reshape(H, 2 * CHUNK, CHUNK)
    n1 = _mix_norm(xs, g1)
    _gather_step(gather, [(0, 1), (1, 0), (2, 0)], "gather_in_1")
    _gather_step(gather, [(0, 2)], "gather_in_2")
    (win_g,) = _gather_end(gather, [0], "gather_in_end")
    win_t_full = win_g.reshape(P, D)
    proj = _inproj_fwd(n1, win_t_full)
    _gather_step(gather, [(1, 1)], "gather_out_1")
    a_out = _gmlp_fwd(proj, lg, lb, ws, bs_t, A)
    _gather_step(gather, [(1, 2), (2, 1), (3, 0)], "gather_out_2_up_1")
    b_out = _attn_fwd(proj, bias, sinks, A, B)
    (wout_g,) = _gather_end(gather, [1], "gather_out_end")
    _gather_step(gather, [(2, 2)], "gather_up_2")
    wout_full = wout_g.reshape(A + B, D)
    h1, mixed, n2 = _outproj_fwd(a_out, b_out, ga, gb, xs, wout_full, g2)
    (wup_g,) = _gather_end(gather, [2], "gather_up_end")
    _gather_step(gather, [(3, 1)], "gather_down_1")
    wup_t = jnp.transpose(wup_g, (0, 2, 1)).reshape(-1, D)
    z = _ffn_up(n2, wup_g)
    _gather_step(gather, [(3, 2)], "gather_down_2")
    (wdown_g,) = _gather_end(gather, [3], "gather_down_end")
    h2 = _ffn_down(h1, z, wdown_g.reshape(-1, D))
    loss_part, dg3, dh2, dh2b = _final_loss(h2, g3, target)

    def reduce_to_chip(state, name):
        csums = [_chip_sum(part, received, "%s_chip_sum_%d" % (name, a))
                 for a, (part, received) in enumerate(_sibling_exchange_end(state, name + "_sib_end"))]
        return _chip_exchange_begin(csums, name + "_chip")

    dwdown = _matmul_tn(z, dh2b, "grad_w_down", square_a=True).reshape(wdown_g.shape)
    dzp = _ffn_down_bwd(dh2b, z, wdown_g.reshape(-1, D))
    dwup = _matmul_tn(n2, dzp, "grad_w_up", col_blocks=N_DEV)
    sib_ffn = _sibling_exchange_begin([dwdown, dwup], "rs_ffn_sib")
    dh1, dh1b, dg2 = _ffn_norm_bwd(_ffn_up_bwd(dzp, wup_t), dh2, h1, g2)
    chip_ffn = reduce_to_chip(sib_ffn, "rs_ffn")
    da, db, dga, dgb = _outproj_bwd(dh1b, wout_full, a_out, b_out, ga, gb)
    dwout = _matmul_tn(mixed, dh1b, "grad_w_out").reshape(wout_g.shape)
    sib_out = _sibling_exchange_begin([dwout], "rs_out_sib")
    duv, dlg, dlb, dws, dbs_t = _gmlp_bwd(proj, da, lg, lb, ws, ws_t, bs_t, A)
    dproj, dbias_t, dsinks = _attn_bwd(proj, db, duv, bias_t, sinks, A, B)
    chip_out = reduce_to_chip(sib_out, "rs_out")
    dwin_t = _matmul_tn(dproj, n1, "grad_w_in").reshape(win_g.shape)
    sib_in = _sibling_exchange_begin([dwin_t], "rs_in_sib")
    dtable_t = _bias_bwd(dbias_t.reshape(H, -1), onehot_kq)
    chip_in = reduce_to_chip(sib_in, "rs_in")
    grad_x, dg1 = _inproj_bwd(dproj, win_t_full, xs, dh1, g1)

    small_w = [rel_bias_table, mix_norm_g, gate_norm_g, gate_norm_b, w_spatial, b_spatial, attn_sinks,
               out_norm_a_g, out_norm_b_g, ffn_norm_g, final_norm_g]
    small_m = [m_rel_bias_table, m_mix_norm_g, m_gate_norm_g, m_gate_norm_b, m_w_spatial, m_b_spatial, m_attn_sinks,
               m_out_norm_a_g, m_out_norm_b_g, m_ffn_norm_g, m_final_norm_g]
    small_v = [v_rel_bias_table, v_mix_norm_g, v_gate_norm_g, v_gate_norm_b, v_w_spatial, v_b_spatial, v_attn_sinks,
               v_out_norm_a_g, v_out_norm_b_g, v_ffn_norm_g, v_final_norm_g]
    small_g = [jnp.transpose(dtable_t), dg1, dlg, dlb, dws, jnp.transpose(dbs_t), dsinks, dga, dgb, dg2, dg3]
    nothing = jnp.zeros((1, H), F32)
    small_w, small_m, small_v = small_w + [nothing], small_m + [nothing], small_v + [nothing]
    small_g = small_g + [jnp.broadcast_to(loss_part, (1, H))]
    shapes = [w.shape for w in small_w]
    totals, places = _small_layout(shapes)
    as_rows = lambda arrays: [a.reshape(_rows2d(a.shape)) for a in arrays]
    big = [None] * 4

    def adam_of(k, state, a, w, m, v):
        csum, received = _chip_exchange_end(state, a, "rs_%d_end" % k)
        big[k] = _adam_sharded(csum, received, w, m, v, "adam_%d" % k)

    small_gather = _gather_begin(_pack_small(small_g, totals, places))
    every = range(len(totals))
    _gather_step(small_gather, [(a, 0) for a in every], "small_gather_start")
    adam_of(3, chip_ffn, 0, w_down[0], m_w_down[0], v_w_down[0])
    _gather_step(small_gather, [(a, 1) for a in every], "small_gather_1")
    adam_of(2, chip_ffn, 1, w_up[0], m_w_up[0], v_w_up[0])
    _gather_step(small_gather, [(a, 2) for a in every], "small_gather_2")
    adam_of(1, chip_out, 0, w_out[0], m_w_out[0], v_w_out[0])
    adam_of(0, chip_in, 0, win_t, m_win_t, v_win_t)
    gathered = _gather_end(small_gather, list(every), "small_gather_end")
    small_out = _adam_small(gathered, totals, places, as_rows(small_w), as_rows(small_m), as_rows(small_v))
    sg, sd, sm, sv = [[outs[k].reshape(s) for outs, s in zip(small_out, shapes)] for k in range(4)]
    big[0] = [jnp.swapaxes(o, 0, 1) for o in big[0]]
    big = [[o.reshape(w.shape) for o in outs] for outs, w in zip(big, (w_in, w_out, w_up, w_down))]

    loss = sg[-1][0, 0]

    order = ["s0", "s1", "b0", "s2", "s3", "s4", "s5", "s6", "s7", "s8", "b1", "s9", "b2", "b3", "s10"]

    def group(idx):
        small = (sg, sd, sm, sv)[idx]
        return [small[int(t[1:])] if t[0] == "s" else big[int(t[1:])][idx] for t in order]

    return (loss, grad_x.reshape(x.shape), *group(0), *group(1), *group(2), *group(3))
```

```python
import math

import numpy as np
import jax
import jax.numpy as jnp
from jax import lax
from jax.experimental import pallas as pl
from jax.experimental.pallas import tpu as pltpu

F32 = jnp.float32
BF16 = jnp.bfloat16
SDS = jax.ShapeDtypeStruct
MESH = pl.DeviceIdType.MESH

N_DEV = 8
EPS = 1e-5
NEG = -1e30
CHUNK = 128
GROUP_DIM = 128
HEAD_DIM = 64
KV_HEADS = 2
N_BUCKETS = 32
MAX_DISTANCE = 128
ADAM_LR, ADAM_B1, ADAM_B2, ADAM_EPS, ADAM_WD, ADAM_STEP = 0.001, 0.9, 0.999, 1e-08, 0.01, 10
GELU_C0 = math.sqrt(2.0 / math.pi)
GELU_C1 = 0.044715

V7X_VMEM_BYTES = 64 * 1024 * 1024
VMEM_LIMIT = V7X_VMEM_BYTES - 8 * 1024 * 1024
LANE = 128

NN = ((1,), (0,))
NT = ((1,), (1,))
TN = ((0,), (0,))


def _dot(a, b, dims):
    return lax.dot_general(a, b, (dims, ((), ())), preferred_element_type=F32)


def _tile(n, pref, unit=LANE):
    best = None
    for t in range(unit, min(n, pref) + 1, unit):
        if n % t == 0:
            best = t
    return n if best is None else best


def _params(n_grid):
    return pltpu.CompilerParams(dimension_semantics=("arbitrary",) * n_grid, vmem_limit_bytes=VMEM_LIMIT)


def _resident(shape):
    return pl.BlockSpec(shape, lambda i: (0, 0), pipeline_mode=pl.Buffered(1))


def _gelu(x):
    return 0.5 * x * (1.0 + jnp.tanh(GELU_C0 * (x + GELU_C1 * x * x * x)))


def _gelu_and_grad(x):
    x2 = x * x
    t = jnp.tanh(GELU_C0 * x * (1.0 + GELU_C1 * x2))
    val = 0.5 * x * (1.0 + t)
    grad = 0.5 * (1.0 + t) + 0.5 * x * (1.0 - t * t) * (GELU_C0 * (1.0 + 3.0 * GELU_C1 * x2))
    return val, grad


def _rms_stats(x):
    return lax.rsqrt(jnp.mean(x * x, axis=-1, keepdims=True) + EPS)


def _rms_bwd(dy, x, r, g):
    w = dy * g
    return r * w - x * (r * r * r) * jnp.mean(w * x, axis=-1, keepdims=True)


def _t5_bucket():
    i = np.arange(CHUNK)[:, None]
    j = np.arange(2 * CHUNK)[None, :]
    rel = np.maximum(i + CHUNK - j, 0)
    n_exact = N_BUCKETS // 2
    relf = np.maximum(rel, n_exact).astype(np.float32)
    large = n_exact + (np.log(relf / np.float32(n_exact)) / np.float32(math.log(MAX_DISTANCE / n_exact))
                       * np.float32(N_BUCKETS - n_exact)).astype(np.int32)
    large = np.minimum(large, N_BUCKETS - 1)
    bucket = np.where(rel < n_exact, rel, large)
    in_window = (i + CHUNK - j >= 0) & (i + CHUNK - j < CHUNK)
    return bucket.astype(np.int32), in_window


def _split3(x):
    hi = x.astype(BF16)
    r1 = x - hi.astype(F32)
    mid = r1.astype(BF16)
    lo = (r1 - mid.astype(F32)).astype(BF16)
    return hi, mid, lo


HBM_SPEC = pl.BlockSpec(memory_space=pltpu.HBM)


def _mesh_pos():
    return lax.axis_index("x"), lax.axis_index("y"), lax.axis_index("c")


def _dev_index(px, py, pc):
    return 4 * px + 2 * py + pc


SEM_SPEC = pl.BlockSpec(memory_space=pltpu.SEMAPHORE)
ANY_SPEC = pl.BlockSpec(memory_space=pl.ANY)
VMEM_SPEC = pl.BlockSpec(memory_space=pltpu.VMEM)
TOKEN = SDS((8, LANE), F32)
SIDE_EFFECT = pltpu.SideEffectType.DATAFLOW_SIDE_EFFECTING


def _hbm(x):
    return pltpu.with_memory_space_constraint(x, pltpu.HBM)


class _CallChain:
    def __init__(self):
        self.token = None

    def call(self, body, *, in_specs, out_specs, out_shape, **kwargs):
        dep, n_in = self.token, len(in_specs)
        single = not isinstance(out_shape, (list, tuple))
        out_shapes = [out_shape] if single else list(out_shape)
        out_specs = [out_specs] if single else list(out_specs)
        n_out = len(out_shapes)
        n_dep = 0 if dep is None else 1
        token_spec = pl.BlockSpec((8, LANE), lambda *_: (0, 0)) if kwargs.get("grid") else VMEM_SPEC

        def chained(*refs):
            outs_at = n_in + n_dep
            body(*refs[:n_in], *refs[outs_at:outs_at + n_out], *refs[outs_at + n_out + 1:])
            token = refs[outs_at + n_out]
            token[...] = jnp.zeros_like(token)

        inner = pl.pallas_call(chained, in_specs=list(in_specs) + [ANY_SPEC] * n_dep, out_specs=out_specs + [token_spec],
                               out_shape=out_shapes + [TOKEN], **kwargs)

        def run(*operands):
            outs = inner(*operands) if dep is None else inner(*operands, dep)
            self.token = outs[n_out]
            return outs[0] if single else list(outs[:n_out])

        return run


_CHAIN = _CallChain()


def _wait_all(waits, x, y, c):
    for kind, src, dst, send_sem, recv_sem in waits:
        cp = pltpu.make_async_remote_copy(src_ref=src, dst_ref=dst, send_sem=send_sem, recv_sem=recv_sem,
                                          device_id=(x, y, c), device_id_type=MESH)
        if kind == "send":
            cp.wait_send()
        else:
            cp.wait_recv()


def _split_start(bufs, copies_of, n_sems, name, sem_sets=(), waits_of=None):
    n, ns = len(bufs), len(sem_sets)
    flat_sems = [s for pair in sem_sets for s in pair]

    def body(*refs):
        ins = refs[:n]
        sems = refs[n:n + 2 * ns]
        send_sems, recv_sems = refs[n + 2 * ns], refs[n + 2 * ns + 1]
        if waits_of is not None:
            _wait_all(waits_of(ins, [(sems[2 * i], sems[2 * i + 1]) for i in range(ns)]), *_mesh_pos())
        for src, dst, k, target in copies_of(ins):
            pltpu.make_async_remote_copy(src_ref=src, dst_ref=dst, send_sem=send_sems.at[k], recv_sem=recv_sems.at[k],
                                         device_id=target, device_id_type=MESH).start()

    outs = _CHAIN.call(
        body, name=name,
        out_shape=[pltpu.SemaphoreType.DMA((n_sems,)), pltpu.SemaphoreType.DMA((n_sems,))]
        + [pltpu.HBM(b.shape, b.dtype) for b in bufs],
        in_specs=[HBM_SPEC] * n + [SEM_SPEC] * (2 * ns), out_specs=[SEM_SPEC, SEM_SPEC] + [HBM_SPEC] * n,
        input_output_aliases={a: 2 + a for a in range(n)},
        compiler_params=pltpu.CompilerParams(has_side_effects=SIDE_EFFECT),
    )(*[_hbm(b) for b in bufs], *flat_sems)
    return outs[0], outs[1], list(outs[2:2 + n])


def _split_wait(bufs, sem_sets, waits_of, name):
    n, ns = len(bufs), len(sem_sets)
    flat_sems = [s for pair in sem_sets for s in pair]

    def body(*refs):
        ins = refs[:n]
        sems = refs[n:n + 2 * ns]
        _wait_all(waits_of(ins, [(sems[2 * i], sems[2 * i + 1]) for i in range(ns)]), *_mesh_pos())

    outs = _CHAIN.call(
        body, name=name,
        out_shape=[pltpu.HBM(b.shape, b.dtype) for b in bufs],
        in_specs=[HBM_SPEC] * n + [SEM_SPEC] * (2 * ns), out_specs=[HBM_SPEC] * n,
        input_output_aliases={a: a for a in range(n)},
        compiler_params=pltpu.CompilerParams(has_side_effects=SIDE_EFFECT),
    )(*bufs, *flat_sems)
    return list(outs)


def _gather_blocks(land):
    rows = land.shape[1]
    first = (rows // 2) // 16 * 16

    def block(px, py, pc):
        return land.at[_dev_index(px, py, pc)]

    def halves(px, py, pc):
        return (land.at[_dev_index(px, py, pc), pl.ds(0, first)], land.at[_dev_index(px, py, pc), pl.ds(first, rows - first)])

    return block, halves


def _gather_begin(shards):
    me = _dev_index(*_mesh_pos())
    lands = [lax.dynamic_update_index_in_dim(lax.empty((N_DEV,) + s.shape, s.dtype), s, me, 0) for s in shards]
    return dict(lands=lands, stage={})


STAGE_COPIES = (3, 4, 1)


def _gather_step(state, items, name):
    which = sorted({a for a, _ in items})
    at = {a: i for i, a in enumerate(which)}
    sem_sets = [state["stage"][(a, s - 1)][0] for a, s in items if s > 0]
    offset, n_sems = {}, 0
    for a, s in items:
        offset[(a, s)] = n_sems
        n_sems += STAGE_COPIES[s]

    def waits_of(ins, sems):
        x, y, c = _mesh_pos()
        out, earlier = [], 0
        for a, s in items:
            if s == 0:
                continue
            block, halves = _gather_blocks(ins[at[a]])
            send, recv = sems[earlier]
            off = state["stage"][(a, s - 1)][1]
            earlier += 1
            if s == 1:
                arrived = [(1, block(1 - x, y, c)), (2, block(x, 1 - y, c))]
            else:
                arrived = list(zip((2, 3), halves(1 - x, 1 - y, c)))
            out += [("recv", ref, ref, send.at[off + k], recv.at[off + k]) for k, ref in arrived]
        return out

    def copies_of(ins):
        x, y, c = _mesh_pos()
        sibling = (x, y, 1 - c)
        out = []
        for a, s in items:
            block, halves = _gather_blocks(ins[at[a]])
            off = offset[(a, s)]
            if s == 0:
                mine = block(x, y, c)
                out += [(mine, mine, off + 1, (1 - x, y, c)), (mine, mine, off + 2, (x, 1 - y, c)), (mine, mine, off, sibling)]
            elif s == 1:
                from_x, from_y = block(1 - x, y, c), block(x, 1 - y, c)
                out += [(halves(1 - x, y, c)[0], halves(1 - x, y, c)[0], off + 2, (x, 1 - y, c)),
                        (halves(x, 1 - y, c)[1], halves(x, 1 - y, c)[1], off + 3, (1 - x, y, c)),
                        (from_x, from_x, off, sibling), (from_y, from_y, off + 1, sibling)]
            else:
                diag = block(1 - x, 1 - y, c)
                out.append((diag, diag, off, sibling))
        return out

    send_sems, recv_sems, bufs = _split_start([state["lands"][a] for a in which], copies_of, n_sems, name,
                                              sem_sets=sem_sets, waits_of=waits_of)
    for a in which:
        state["lands"][a] = bufs[at[a]]
    for a, s in items:
        state["stage"][(a, s)] = ((send_sems, recv_sems), offset[(a, s)])


def _gather_end(state, which, name):
    sem_sets = [state["stage"][(a, s)][0] for a in which for s in range(3)]

    def waits(ins, sems):
        x, y, c = _mesh_pos()
        out = []
        for i, a in enumerate(which):
            block, halves = _gather_blocks(ins[i])
            (b_send, b_recv), (s1_send, s1_recv), (s2_send, s2_recv) = sems[3 * i:3 * i + 3]
            o0, o1, o2 = (state["stage"][(a, s)][1] for s in range(3))
            arrivals = [(block(x, y, 1 - c), b_send, b_recv, o0),
                        (block(1 - x, y, 1 - c), s1_send, s1_recv, o1), (block(x, 1 - y, 1 - c), s1_send, s1_recv, o1 + 1),
                        (block(1 - x, 1 - y, 1 - c), s2_send, s2_recv, o2)]
            mine = block(x, y, c)
            sent = [(mine, b_send, b_recv, o0 + k) for k in range(3)]
            sent += [(block(1 - x, y, c), s1_send, s1_recv, o1), (block(x, 1 - y, c), s1_send, s1_recv, o1 + 1),
                     (halves(1 - x, y, c)[0], s1_send, s1_recv, o1 + 2), (halves(x, 1 - y, c)[1], s1_send, s1_recv, o1 + 3),
                     (block(1 - x, 1 - y, c), s2_send, s2_recv, o2)]
            out += [("recv", ref, ref, s.at[k], r.at[k]) for ref, s, r, k in arrivals]
            out += [("send", ref, ref, s.at[k], r.at[k]) for ref, s, r, k in sent]
        return out

    bufs = _split_wait([state["lands"][a] for a in which], sem_sets, waits, name)
    for i, a in enumerate(which):
        state["lands"][a] = bufs[i]
    return bufs


def _sibling_exchange_begin(parts, name):
    lands = [lax.empty((4,) + p.shape[1:], p.dtype) for p in parts]
    n = len(parts)

    def copies_of(ins):
        x, y, c = _mesh_pos()
        return [(ins[a].at[2 * j + (1 - c)], ins[n + a].at[j], 4 * a + j, (x, y, 1 - c)) for a in range(n) for j in range(4)]

    send_sems, recv_sems, bufs = _split_start(list(parts) + lands, copies_of, 4 * n, name)
    return dict(bufs=bufs, sems=(send_sems, recv_sems), n=n)


def _sibling_exchange_end(state, name):
    n = state["n"]

    def waits(ins, sems):
        _, _, c = _mesh_pos()
        return [(kind, ins[a].at[2 * j + (1 - c)], ins[n + a].at[j], sems[0][0].at[4 * a + j], sems[0][1].at[4 * a + j])
                for a in range(n) for j in range(4) for kind in ("send", "recv")]

    bufs = _split_wait(state["bufs"], [state["sems"]], waits, name)
    return [(bufs[a], bufs[n + a]) for a in range(n)]


CHIP_FLIPS = (2, 1, 3)


def _chip_exchange_begin(csums, name):
    lands = [lax.empty((3,) + s.shape[1:], s.dtype) for s in csums]
    n = len(csums)

    def copies_of(ins):
        x, y, c = _mesh_pos()
        chips = [(1 - x, y), (x, 1 - y), (1 - x, 1 - y)]
        return [(ins[a].at[CHIP_FLIPS[r]], ins[n + a].at[r], 3 * a + r, (px, py, c))
                for a in range(n) for r, (px, py) in enumerate(chips)]

    send_sems, recv_sems, bufs = _split_start(list(csums) + lands, copies_of, 3 * n, name)
    return dict(bufs=bufs, sems=(send_sems, recv_sems), n=n)


def _chip_exchange_end(state, a, name):
    n = state["n"]

    def waits(ins, sems):
        return [(kind, ins[0].at[CHIP_FLIPS[r]], ins[1].at[r], sems[0][0].at[3 * a + r], sems[0][1].at[3 * a + r])
                for r in range(3) for kind in ("send", "recv")]

    csum, received = _split_wait([state["bufs"][a], state["bufs"][n + a]], [state["sems"]], waits, name)
    return csum, received


def _chip_sum(part, recv, name):
    _, R, C = part.shape
    tr = _tile(R, 2048, 16)
    place = jnp.stack([lax.axis_index("c"), 2 * lax.axis_index("x") + lax.axis_index("y")]).astype(jnp.int32)

    def body(place_ref, p_ref, r_ref, o_ref):
        o_ref[...] = (p_ref[...].astype(F32) + r_ref[...].astype(F32)).astype(o_ref.dtype)

    def chip(p, place_ref):
        return jnp.bitwise_xor(p, place_ref[1])

    grid_spec = pltpu.PrefetchScalarGridSpec(
        num_scalar_prefetch=1, grid=(4, R // tr),
        in_specs=[pl.BlockSpec((None, tr, C), lambda p, i, place_ref: (2 * chip(p, place_ref) + place_ref[0], i, 0)),
                  pl.BlockSpec((None, tr, C), lambda p, i, place_ref: (chip(p, place_ref), i, 0))],
        out_specs=pl.BlockSpec((None, tr, C), lambda p, i, place_ref: (p, i, 0)))
    return pl.pallas_call(body, name=name, grid_spec=grid_spec, out_shape=SDS((4, R, C), part.dtype),
                          compiler_params=_params(2))(place, part, recv)


def _bias_fwd(table_t, onehot_t, onehot_kq_t):
    H = table_t.shape[0]
    n = onehot_t.shape[1]

    def body(t_ref, oh_ref, oh_kq_ref, o_ref, o_kq_ref):
        hi, mid, lo = _split3(t_ref[...])
        for src, dst in ((oh_ref, o_ref), (oh_kq_ref, o_kq_ref)):
            oh = src[...]
            dst[...] = _dot(hi, oh, NN) + _dot(mid, oh, NN) + _dot(lo, oh, NN)

    return _CHAIN.call(body, name="bias_fwd", in_specs=[VMEM_SPEC] * 3, out_specs=[VMEM_SPEC] * 2,
                       out_shape=[SDS((H, n), F32)] * 2, compiler_params=_params(0))(table_t, onehot_t, onehot_kq_t)


def _mix_norm(x, g):
    T, D = x.shape
    tm = _tile(T, 512)

    def body(x_ref, g_ref, n_ref):
        xv = x_ref[...]
        n_ref[...] = (xv * _rms_stats(xv) * g_ref[...]).astype(BF16)

    row = pl.BlockSpec((tm, D), lambda i: (i, 0))
    return _CHAIN.call(body, name="mix_norm", grid=(T // tm,), in_specs=[row, pl.BlockSpec((1, D), lambda i: (0, 0))],
                       out_specs=row, out_shape=SDS((T, D), BF16), compiler_params=_params(1))(x, g)


def _inproj_fwd(n, w_t):
    T, D = n.shape
    P = w_t.shape[0]
    tm = _tile(T, 512)

    def body(n_ref, w_ref, proj_ref):
        proj_ref[...] = _dot(n_ref[...], w_ref[...], NT)

    return _CHAIN.call(
        body, name="inproj_fwd", grid=(T // tm,),
        in_specs=[pl.BlockSpec((tm, D), lambda i: (i, 0)), _resident((P, D))],
        out_specs=pl.BlockSpec((tm, P), lambda i: (i, 0)),
        out_shape=SDS((T, P), F32), compiler_params=_params(1))(n, w_t)


def _layer_norm_group(vg, lg, lb):
    mu = jnp.mean(vg, axis=-1, keepdims=True)
    xc = vg - mu
    rstd = lax.rsqrt(jnp.mean(xc * xc, axis=-1, keepdims=True) + EPS)
    vhat = xc * rstd
    return vhat, rstd, vhat * lg + lb


def _gmlp_fwd(proj, lg, lb, w_s, bs_t, A):
    T = proj.shape[0]
    G = A // GROUP_DIM
    tm = _tile(T, 512)
    nc = tm // CHUNK

    def body(u_ref, v_ref, lg_ref, lb_ref, w_ref, bst_ref, a_ref):
        row = lax.broadcasted_iota(jnp.int32, (CHUNK, CHUNK), 0)
        col = lax.broadcasted_iota(jnp.int32, (CHUNK, CHUNK), 1)
        causal = row >= col
        for g in range(G):
            sl = slice(g * GROUP_DIM, (g + 1) * GROUP_DIM)
            _, _, vn = _layer_norm_group(_gelu(v_ref[:, sl]), lg_ref[:, sl], lb_ref[:, sl])
            vnb = vn.astype(BF16)
            wm = jnp.where(causal, w_ref[g], 0.0).astype(BF16)
            ug = _gelu(u_ref[:, sl])
            bcol = bst_ref[:, g:g + 1]
            for c in range(nc):
                rs = slice(c * CHUNK, (c + 1) * CHUNK)
                a_ref[rs, sl] = ug[rs] * (_dot(wm, vnb[rs], NN) + bcol)

    return _CHAIN.call(
        body, name="gmlp_fwd", grid=(T // tm,),
        in_specs=[pl.BlockSpec((tm, A), lambda i: (i, 0)), pl.BlockSpec((tm, A), lambda i: (i, 1)),
                  pl.BlockSpec((1, A), lambda i: (0, 0)), pl.BlockSpec((1, A), lambda i: (0, 0)),
                  pl.BlockSpec((G, CHUNK, CHUNK), lambda i: (0, 0, 0)), pl.BlockSpec((CHUNK, G), lambda i: (0, 0))],
        out_specs=pl.BlockSpec((tm, A), lambda i: (i, 0)),
        out_shape=SDS((T, A), F32), compiler_params=_params(1))(proj, proj, lg, lb, w_s, bs_t)


def _attn_masks(first_tile):
    ii = lax.broadcasted_iota(jnp.int32, (CHUNK, 2 * CHUNK), 0)
    jj = lax.broadcasted_iota(jnp.int32, (CHUNK, 2 * CHUNK), 1)
    in_window = (jj > ii) & (jj <= ii + CHUNK)
    first_mask = in_window & jnp.logical_or(jnp.logical_not(first_tile), jj >= CHUNK)
    return in_window, first_mask


def _softmax_with_sink(s, sink, axis):
    m = jnp.maximum(jnp.max(s, axis=axis, keepdims=True), sink)
    p = jnp.exp(s - m)
    e_sink = jnp.exp(sink - m)
    inv = 1.0 / (jnp.sum(p, axis=axis, keepdims=True) + e_sink)
    return p * inv, e_sink * inv


def _pad_heads(band, group):
    lane = lax.broadcasted_iota(jnp.int32, band.shape, 1)
    if group == 0:
        low = jnp.where(lane < HEAD_DIM, band, 0.0)
        high = pltpu.roll(low, HEAD_DIM, 1)
    else:
        high = jnp.where(lane >= HEAD_DIM, band, 0.0)
        low = pltpu.roll(high, HEAD_DIM, 1)
    return low.astype(BF16), high.astype(BF16)


def _attn_specs(tq, A, B, reverse_tiles=None):
    nb = tq // CHUNK
    kcol = (2 * A + B) // LANE
    if reverse_tiles is None:
        tile = lambda i: i
    else:
        tile = lambda i: reverse_tiles - 1 - i
    prev = lambda i: jnp.maximum(tile(i) * nb - 1, 0)
    return [pl.BlockSpec((tq, B), lambda i: (tile(i), 2 * A // B)),
            pl.BlockSpec((tq, LANE), lambda i: (tile(i), kcol)),
            pl.BlockSpec((tq, LANE), lambda i: (tile(i), kcol + 1)),
            pl.BlockSpec((CHUNK, LANE), lambda i: (prev(i), kcol)),
            pl.BlockSpec((CHUNK, LANE), lambda i: (prev(i), kcol + 1))]


def _attn_fwd(proj, bias, sinks, A, B):
    T = proj.shape[0]
    H = B // HEAD_DIM
    qpk = H // KV_HEADS
    tq = _tile(T, 512)
    nb = tq // CHUNK

    scale = HEAD_DIM ** -0.5

    def body(sink_ref, q_ref, k_ref, v_ref, kp_ref, vp_ref, bias_ref, o_ref):
        in_window, first_mask = _attn_masks(pl.program_id(0) == 0)
        for b in range(nb):
            rows = slice(b * CHUNK, (b + 1) * CHUNK)
            if b == 0:
                kprev, vprev, mask = kp_ref[...], vp_ref[...], first_mask
            else:
                prows = slice((b - 1) * CHUNK, b * CHUNK)
                kprev, vprev, mask = k_ref[prows, :], v_ref[prows, :], in_window
            kband = jnp.concatenate([kprev, k_ref[rows, :]], axis=0)
            vband = jnp.concatenate([vprev, v_ref[rows, :]], axis=0)
            k_pads = [_pad_heads(kband, g) for g in range(KV_HEADS)]
            v_both = [jnp.concatenate(_pad_heads(vband, g), axis=0) for g in range(KV_HEADS)]
            scores = []
            for pair in range(H // 2):
                h = 2 * pair
                qs = (q_ref[rows, h * HEAD_DIM:(h + 2) * HEAD_DIM] * scale).astype(BF16)
                scores += [_dot(qs, kz, NT) for kz in k_pads[h // qpk]]
            probs = [_softmax_with_sink(jnp.where(mask, s + bias_ref[h], NEG), sink_ref[h], -1)[0].astype(BF16)
                     for h, s in enumerate(scores)]
            outs = [_dot(jnp.concatenate(probs[h:h + 2], axis=1), v_both[h // qpk], NN) for h in range(0, H, 2)]
            o_ref[rows, :] = jnp.concatenate(outs, axis=1)

    return _CHAIN.call(
        body, name="attn_fwd", grid=(T // tq,),
        in_specs=[pl.BlockSpec(memory_space=pltpu.SMEM)] + _attn_specs(tq, A, B)
        + [pl.BlockSpec((H, CHUNK, 2 * CHUNK), lambda i: (0, 0, 0))],
        out_specs=pl.BlockSpec((tq, B), lambda i: (i, 0)),
        out_shape=SDS((T, B), F32), compiler_params=_params(1))(sinks, proj, proj, proj, proj, proj, bias)


def _outproj_fwd(a, b, ga, gb, x, w, g_ffn):
    T, A = a.shape
    B = b.shape[1]
    D = x.shape[1]
    tm = _tile(T, 512)

    def body(a_ref, b_ref, ga_ref, gb_ref, x_ref, w_ref, gf_ref, h_ref, mix_ref, n_ref):
        av, bv = a_ref[...], b_ref[...]
        mix_ref[:, :A] = (av * _rms_stats(av) * ga_ref[...]).astype(BF16)
        mix_ref[:, A:] = (bv * _rms_stats(bv) * gb_ref[...]).astype(BF16)
        hv = x_ref[...] + _dot(mix_ref[...], w_ref[...], NN)
        h_ref[...] = hv
        n_ref[...] = (hv * _rms_stats(hv) * gf_ref[...]).astype(BF16)

    row = pl.BlockSpec((tm, D), lambda i: (i, 0))
    return _CHAIN.call(
        body, name="outproj_fwd", grid=(T // tm,),
        in_specs=[pl.BlockSpec((tm, A), lambda i: (i, 0)), pl.BlockSpec((tm, B), lambda i: (i, 0)),
                  pl.BlockSpec((1, A), lambda i: (0, 0)), pl.BlockSpec((1, B), lambda i: (0, 0)),
                  row, _resident((A + B, D)), pl.BlockSpec((1, D), lambda i: (0, 0))],
        out_specs=[row, pl.BlockSpec((tm, A + B), lambda i: (i, 0)), row],
        out_shape=[SDS((T, D), F32), SDS((T, A + B), BF16), SDS((T, D), BF16)],
        compiler_params=_params(1))(a, b, ga, gb, x, w, g_ffn)


def _ffn_up(n, w_up):
    T, D = n.shape
    Fb = w_up.shape[2]
    F = N_DEV * Fb
    tm, tf = _tile(T, 1024), _tile(Fb, 1024)
    per = Fb // tf

    def body(n_ref, wu_ref, z_ref):
        z_ref[...] = jnp.maximum(_dot(n_ref[...], wu_ref[...], NN), 0.0).astype(BF16)

    return _CHAIN.call(
        body, name="ffn_up", grid=(T // tm, F // tf),
        in_specs=[pl.BlockSpec((tm, D), lambda i, j: (i, 0)),
                  pl.BlockSpec((None, D, tf), lambda i, j: (j // per, 0, j % per))],
        out_specs=pl.BlockSpec((tm, tf), lambda i, j: (i, j)),
        out_shape=SDS((T, F), BF16), compiler_params=_params(2))(n, w_up)


def _ffn_down(h1, z, w_down):
    T, D = h1.shape
    F = w_down.shape[0]
    tm, tn, tk = _tile(T, 1024), _tile(D, 1024), _tile(F, 4096)

    def body(h_ref, z_ref, wd_ref, h2_ref):
        k = pl.program_id(2)

        @pl.when(k == 0)
        def _():
            h2_ref[...] = h_ref[...]

        zf = z_ref[...].astype(F32)
        h2_ref[...] += _dot((zf * zf).astype(BF16), wd_ref[...], NN)

    return _CHAIN.call(
        body, name="ffn_down", grid=(T // tm, D // tn, F // tk),
        in_specs=[pl.BlockSpec((tm, tn), lambda i, j, k: (i, j)), pl.BlockSpec((tm, tk), lambda i, j, k: (i, k)),
                  pl.BlockSpec((tk, tn), lambda i, j, k: (k, j))],
        out_specs=pl.BlockSpec((tm, tn), lambda i, j, k: (i, j)),
        out_shape=SDS((T, D), F32), compiler_params=_params(3))(h1, z, w_down)


def _final_loss(h2, g, target):
    T, D = h2.shape
    tm = _tile(T, 512)

    def body(h_ref, g_ref, t_ref, loss_ref, dg_ref, dh_ref, dhb_ref):
        @pl.when(pl.program_id(0) == 0)
        def _():
            loss_ref[...] = jnp.zeros_like(loss_ref)
            dg_ref[...] = jnp.zeros_like(dg_ref)

        hv, gv = h_ref[...], g_ref[...]
        r = _rms_stats(hv)
        hn = hv * r
        e = hn * gv - t_ref[...]
        loss_ref[...] += (0.5 / D) * jnp.sum(jnp.sum(e * e, axis=0, keepdims=True), axis=-1, keepdims=True)
        dy = e * (1.0 / D)
        dg_ref[...] += jnp.sum(dy * hn, axis=0, keepdims=True)
        dh = _rms_bwd(dy, hv, r, gv)
        dh_ref[...] = dh
        dhb_ref[...] = dh.astype(BF16)

    return _CHAIN.call(
        body, name="final_loss", grid=(T // tm,),
        in_specs=[pl.BlockSpec((tm, D), lambda i: (i, 0)), pl.BlockSpec((1, D), lambda i: (0, 0)),
                  pl.BlockSpec((tm, D), lambda i: (i, 0))],
        out_specs=[pl.BlockSpec((1, 1), lambda i: (0, 0)), pl.BlockSpec((1, D), lambda i: (0, 0)),
                   pl.BlockSpec((tm, D), lambda i: (i, 0)), pl.BlockSpec((tm, D), lambda i: (i, 0))],
        out_shape=[SDS((1, 1), F32), SDS((1, D), F32), SDS((T, D), F32), SDS((T, D), BF16)],
        compiler_params=_params(1))(h2, g, target)


def _ffn_down_bwd(dh2b, z, w_down):
    T, D = dh2b.shape
    F = w_down.shape[0]
    tm, tf = _tile(T, 1024), _tile(F, 1024)

    def body(dh_ref, z_ref, wd_ref, dzp_ref):
        dzz = _dot(dh_ref[...], wd_ref[...], NT)
        dzp_ref[...] = (dzz * (2.0 * z_ref[...].astype(F32))).astype(BF16)

    return _CHAIN.call(
        body, name="ffn_down_bwd", grid=(T // tm, F // tf),
        in_specs=[pl.BlockSpec((tm, D), lambda i, j: (i, 0)), pl.BlockSpec((tm, tf), lambda i, j: (i, j)),
                  pl.BlockSpec((tf, D), lambda i, j: (j, 0))],
        out_specs=pl.BlockSpec((tm, tf), lambda i, j: (i, j)),
        out_shape=SDS((T, F), BF16), compiler_params=_params(2))(dh2b, z, w_down)


def _ffn_up_bwd(dzp, w_up_t):
    T, F = dzp.shape
    D = w_up_t.shape[1]
    tm, tn, tk = _tile(T, 1024), _tile(D, 1024), _tile(F, 4096)

    def body(dzp_ref, w_ref, dn_ref):
        part = _dot(dzp_ref[...], w_ref[...], NN)

        @pl.when(pl.program_id(2) == 0)
        def _():
            dn_ref[...] = part

        @pl.when(pl.program_id(2) > 0)
        def _():
            dn_ref[...] += part

    return _CHAIN.call(
        body, name="ffn_up_bwd", grid=(T // tm, D // tn, F // tk),
        in_specs=[pl.BlockSpec((tm, tk), lambda i, j, k: (i, k)), pl.BlockSpec((tk, tn), lambda i, j, k: (k, j))],
        out_specs=pl.BlockSpec((tm, tn), lambda i, j, k: (i, j)),
        out_shape=SDS((T, D), F32), compiler_params=_params(3))(dzp, w_up_t)


def _ffn_norm_bwd(dn, dh2, h1, g):
    T, D = h1.shape
    tm = _tile(T, 512)

    def body(dn_ref, dh_ref, h_ref, g_ref, dh1_ref, dh1b_ref, dg_ref):
        @pl.when(pl.program_id(0) == 0)
        def _():
            dg_ref[...] = jnp.zeros_like(dg_ref)

        hv, dnv = h_ref[...], dn_ref[...]
        r = _rms_stats(hv)
        dg_ref[...] += jnp.sum(dnv * (hv * r), axis=0, keepdims=True)
        dh1 = dh_ref[...] + _rms_bwd(dnv, hv, r, g_ref[...])
        dh1_ref[...] = dh1
        dh1b_ref[...] = dh1.astype(BF16)

    row = pl.BlockSpec((tm, D), lambda i: (i, 0))
    vec = pl.BlockSpec((1, D), lambda i: (0, 0))
    return _CHAIN.call(
        body, name="ffn_norm_bwd", grid=(T // tm,), in_specs=[row, row, row, vec], out_specs=[row, row, vec],
        out_shape=[SDS((T, D), F32), SDS((T, D), BF16), SDS((1, D), F32)], compiler_params=_params(1))(dn, dh2, h1, g)


def _matmul_tn(a, b, name, square_a=False, col_blocks=None):
    T, K = a.shape
    N = b.shape[1]
    tk = _tile(K, 1792)
    tn = _tile(N if col_blocks is None else N // col_blocks, 1024 if tk <= 1024 else 512)

    def body(a_ref, b_ref, o_ref):
        av = a_ref[...]
        if square_a:
            af = av.astype(F32)
            av = (af * af).astype(BF16)
        o_ref[...] = _dot(av, b_ref[...], TN).astype(o_ref.dtype)

    if col_blocks is None:
        out_shape = SDS((K, N), BF16)
        out_spec = pl.BlockSpec((tk, tn), lambda i, j: (i, j))
    else:
        per = (N // col_blocks) // tn
        out_shape = SDS((col_blocks, K, N // col_blocks), BF16)
        out_spec = pl.BlockSpec((None, tk, tn), lambda i, j: (j // per, i, j % per))
    return _CHAIN.call(
        body, name=name, grid=(K // tk, N // tn),
        in_specs=[pl.BlockSpec((T, tk), lambda i, j: (0, i)), pl.BlockSpec((T, tn), lambda i, j: (0, j))],
        out_specs=out_spec, out_shape=out_shape, compiler_params=_params(2))(a, b)


def _outproj_bwd(dh1b, w, a, b, ga, gb):
    T, D = dh1b.shape
    A, B = a.shape[1], b.shape[1]
    tm = _tile(T, 512)

    def body(dh_ref, w_ref, a_ref, b_ref, ga_ref, gb_ref, da_ref, db_ref, dga_ref, dgb_ref):
        @pl.when(pl.program_id(0) == 0)
        def _():
            dga_ref[...] = jnp.zeros_like(dga_ref)
            dgb_ref[...] = jnp.zeros_like(dgb_ref)

        dmix = _dot(dh_ref[...], w_ref[...], NT)
        for src_ref, g_ref, dx_ref, dg_ref, dn in ((a_ref, ga_ref, da_ref, dga_ref, dmix[:, :A]),
                                                   (b_ref, gb_ref, db_ref, dgb_ref, dmix[:, A:])):
            xv = src_ref[...]
            r = _rms_stats(xv)
            dg_ref[...] += jnp.sum(dn * (xv * r), axis=0, keepdims=True)
            dx_ref[...] = _rms_bwd(dn, xv, r, g_ref[...])

    return _CHAIN.call(
        body, name="outproj_bwd", grid=(T // tm,),
        in_specs=[pl.BlockSpec((tm, D), lambda i: (i, 0)), _resident((A + B, D)),
                  pl.BlockSpec((tm, A), lambda i: (i, 0)), pl.BlockSpec((tm, B), lambda i: (i, 0)),
                  pl.BlockSpec((1, A), lambda i: (0, 0)), pl.BlockSpec((1, B), lambda i: (0, 0))],
        out_specs=[pl.BlockSpec((tm, A), lambda i: (i, 0)), pl.BlockSpec((tm, B), lambda i: (i, 0)),
                   pl.BlockSpec((1, A), lambda i: (0, 0)), pl.BlockSpec((1, B), lambda i: (0, 0))],
        out_shape=[SDS((T, A), F32), SDS((T, B), F32), SDS((1, A), F32), SDS((1, B), F32)],
        compiler_params=_params(1))(dh1b, w, a, b, ga, gb)


def _gmlp_bwd(proj, da, lg, lb, w_s, w_st, bs_t, A):
    T = proj.shape[0]
    G = A // GROUP_DIM
    tm = _tile(T, 512)
    nc = tm // CHUNK

    def body(u_ref, v_ref, da_ref, lg_ref, lb_ref, w_ref, wt_ref, bst_ref, duv_ref, dlg_ref, dlb_ref, dw_ref, dbs_ref):
        @pl.when(pl.program_id(0) == 0)
        def _():
            dlg_ref[...] = jnp.zeros_like(dlg_ref)
            dlb_ref[...] = jnp.zeros_like(dlb_ref)
            dw_ref[...] = jnp.zeros_like(dw_ref)
            dbs_ref[...] = jnp.zeros_like(dbs_ref)

        row = lax.broadcasted_iota(jnp.int32, (CHUNK, CHUNK), 0)
        col = lax.broadcasted_iota(jnp.int32, (CHUNK, CHUNK), 1)
        lower = row >= col
        upper = row <= col
        for g in range(G):
            sl = slice(g * GROUP_DIM, (g + 1) * GROUP_DIM)
            lgv = lg_ref[:, sl]
            vg, vg_grad = _gelu_and_grad(v_ref[:, sl])
            vhat, rstd, vn = _layer_norm_group(vg, lgv, lb_ref[:, sl])
            vnb = vn.astype(BF16)
            ug, ug_grad = _gelu_and_grad(u_ref[:, sl])
            dav = da_ref[:, sl]
            wm = jnp.where(lower, w_ref[g], 0.0).astype(BF16)
            wmt = jnp.where(upper, wt_ref[g], 0.0).astype(BF16)
            bcol = bst_ref[:, g:g + 1]
            dw_acc = jnp.zeros((CHUNK, CHUNK), F32)
            dbs_acc = jnp.zeros((CHUNK, 1), F32)
            dvn_parts = []
            dug_parts = []
            for c in range(nc):
                rs = slice(c * CHUNK, (c + 1) * CHUNK)
                mixed = _dot(wm, vnb[rs], NN) + bcol
                dug_parts.append(dav[rs] * mixed)
                dmix = dav[rs] * ug[rs]
                dbs_acc = dbs_acc + jnp.sum(dmix, axis=-1, keepdims=True)
                dmixb = dmix.astype(BF16)
                dw_acc = dw_acc + _dot(dmixb, vnb[rs], NT)
                dvn_parts.append(_dot(wmt, dmixb, NN))
            dvn = jnp.concatenate(dvn_parts, axis=0)
            dug = jnp.concatenate(dug_parts, axis=0)
            dw_ref[g] += jnp.where(lower, dw_acc, 0.0)
            dbs_ref[:, g:g + 1] += dbs_acc
            dlg_ref[:, sl] += jnp.sum(dvn * vhat, axis=0, keepdims=True)
            dlb_ref[:, sl] += jnp.sum(dvn, axis=0, keepdims=True)
            dvhat = dvn * lgv
            dvg = rstd * (dvhat - jnp.mean(dvhat, axis=-1, keepdims=True)
                          - vhat * jnp.mean(dvhat * vhat, axis=-1, keepdims=True))
            duv_ref[:, sl] = (dug * ug_grad).astype(BF16)
            duv_ref[:, A + g * GROUP_DIM:A + (g + 1) * GROUP_DIM] = (dvg * vg_grad).astype(BF16)

    return _CHAIN.call(
        body, name="gmlp_bwd", grid=(T // tm,),
        in_specs=[pl.BlockSpec((tm, A), lambda i: (i, 0)), pl.BlockSpec((tm, A), lambda i: (i, 1)),
                  pl.BlockSpec((tm, A), lambda i: (i, 0)),
                  pl.BlockSpec((1, A), lambda i: (0, 0)), pl.BlockSpec((1, A), lambda i: (0, 0)),
                  pl.BlockSpec((G, CHUNK, CHUNK), lambda i: (0, 0, 0)),
                  pl.BlockSpec((G, CHUNK, CHUNK), lambda i: (0, 0, 0)), pl.BlockSpec((CHUNK, G), lambda i: (0, 0))],
        out_specs=[pl.BlockSpec((tm, 2 * A), lambda i: (i, 0)),
                   pl.BlockSpec((1, A), lambda i: (0, 0)), pl.BlockSpec((1, A), lambda i: (0, 0)),
                   pl.BlockSpec((G, CHUNK, CHUNK), lambda i: (0, 0, 0)), pl.BlockSpec((CHUNK, G), lambda i: (0, 0))],
        out_shape=[SDS((T, 2 * A), BF16), SDS((1, A), F32), SDS((1, A), F32),
                   SDS((G, CHUNK, CHUNK), F32), SDS((CHUNK, G), F32)],
        compiler_params=_params(1))(proj, proj, da, lg, lb, w_s, w_st, bs_t)


def _attn_bwd(proj, do, duv, bias_t, sinks, A, B):
    T, P = proj.shape
    H = B // HEAD_DIM
    qpk = H // KV_HEADS
    tq = _tile(T, 512)
    nb = tq // CHUNK
    n_tiles = T // tq
    scale = HEAD_DIM ** -0.5
    rev = lambda i: n_tiles - 1 - i

    def body(sink_ref, q_ref, k_ref, v_ref, kp_ref, vp_ref, do_ref, duv_ref, bias_ref,
             dproj_ref, dbias_ref, dsink_ref, carry, dkv, sacc):
        step = pl.program_id(0)

        @pl.when(step == 0)
        def _():
            carry[...] = jnp.zeros_like(carry)
            sacc[...] = jnp.zeros_like(sacc)
            dbias_ref[...] = jnp.zeros_like(dbias_ref)

        jj = lax.broadcasted_iota(jnp.int32, (2 * CHUNK, CHUNK), 0)
        ii = lax.broadcasted_iota(jnp.int32, (2 * CHUNK, CHUNK), 1)
        in_window = (jj > ii) & (jj <= ii + CHUNK)
        first_mask = in_window & jnp.logical_or(step != n_tiles - 1, jj >= CHUNK)
        low_query = lax.broadcasted_iota(jnp.int32, (CHUNK, LANE), 1) < HEAD_DIM
        low_key = lax.broadcasted_iota(jnp.int32, (2 * CHUNK, LANE), 1) < HEAD_DIM

        def split_pair(pair_bf16):
            zero = jnp.zeros_like(pair_bf16)
            return jnp.concatenate([jnp.where(low_query, pair_bf16, zero), jnp.where(low_query, zero, pair_bf16)], axis=0)

        dproj_ref[:, :2 * A] = duv_ref[...]
        dkv[...] = jnp.zeros_like(dkv)
        for b in range(nb):
            rows = slice(b * CHUNK, (b + 1) * CHUNK)
            band = slice(b * CHUNK, (b + 2) * CHUNK)
            if b == 0:
                kprev, vprev, mask = kp_ref[...], vp_ref[...], first_mask
            else:
                prows = slice((b - 1) * CHUNK, b * CHUNK)
                kprev, vprev, mask = k_ref[prows, :], v_ref[prows, :], in_window
            kband = jnp.concatenate([kprev, k_ref[rows, :]], axis=0)
            vband = jnp.concatenate([vprev, v_ref[rows, :]], axis=0)
            k_pads = [_pad_heads(kband, g) for g in range(KV_HEADS)]
            v_pads = [_pad_heads(vband, g) for g in range(KV_HEADS)]
            queries, douts, scores, dprobs = [], [], [], []
            for pair in range(H // 2):
                cols = slice(2 * pair * HEAD_DIM, (2 * pair + 2) * HEAD_DIM)
                qs = (q_ref[rows, cols] * scale).astype(BF16)
                dob = do_ref[rows, cols].astype(BF16)
                queries.append(qs)
                douts.append(dob)
                scores += [_dot(kz, qs, NT) for kz in k_pads[2 * pair // qpk]]
                dprobs += [_dot(vz, dob, NT) for vz in v_pads[2 * pair // qpk]]
            probs, dscores = [], []
            for h in range(H):
                pt, p_sink = _softmax_with_sink(jnp.where(mask, scores[h] + bias_ref[h], NEG), sink_ref[h], 0)
                delta = jnp.sum(pt * dprobs[h], axis=0, keepdims=True)
                dst = pt * (dprobs[h] - delta)
                dbias_ref[h] += dst
                sacc[h:h + 1, :] += -(p_sink * delta)
                probs.append(pt.astype(BF16))
                dscores.append(dst.astype(BF16))
            dq_parts, dk_groups, dv_groups = [], [], []
            for g in range(KV_HEADS):
                k_both = jnp.concatenate(k_pads[g], axis=0)
                dk_acc = jnp.zeros((2 * CHUNK, LANE), F32)
                dv_acc = jnp.zeros((2 * CHUNK, LANE), F32)
                for pair in range(g * qpk // 2, (g + 1) * qpk // 2):
                    pair_heads = slice(2 * pair, 2 * pair + 2)
                    dk_acc = dk_acc + _dot(jnp.concatenate(dscores[pair_heads], axis=1), split_pair(queries[pair]), NN)
                    dv_acc = dv_acc + _dot(jnp.concatenate(probs[pair_heads], axis=1), split_pair(douts[pair]), NN)
                    dq_parts.append(_dot(jnp.concatenate(dscores[pair_heads], axis=0), k_both, TN) * scale)
                dk_groups.append(dk_acc + pltpu.roll(dk_acc, HEAD_DIM, 1))
                dv_groups.append(dv_acc + pltpu.roll(dv_acc, HEAD_DIM, 1))
            dkv[band, :LANE] += jnp.where(low_key, dk_groups[0], dk_groups[1])
            dkv[band, LANE:] += jnp.where(low_key, dv_groups[0], dv_groups[1])
            dproj_ref[rows, 2 * A:2 * A + B] = jnp.concatenate(dq_parts, axis=1).astype(BF16)
        last = slice(tq, tq + CHUNK)
        dkv[last, :] += carry[...]
        dproj_ref[:, 2 * A + B:] = dkv[CHUNK:, :].astype(BF16)
        carry[...] = dkv[:CHUNK, :]

        @pl.when(step == n_tiles - 1)
        def _():
            dsink_ref[...] = jnp.sum(sacc[...], axis=1, keepdims=True)

    specs = _attn_specs(tq, A, B, reverse_tiles=n_tiles)
    return _CHAIN.call(
        body, name="attn_bwd", grid=(n_tiles,),
        in_specs=[pl.BlockSpec(memory_space=pltpu.SMEM)] + specs
        + [pl.BlockSpec((tq, B), lambda i: (rev(i), 0)), pl.BlockSpec((tq, 2 * A), lambda i: (rev(i), 0)),
           pl.BlockSpec((H, 2 * CHUNK, CHUNK), lambda i: (0, 0, 0))],
        out_specs=[pl.BlockSpec((tq, P), lambda i: (rev(i), 0)),
                   pl.BlockSpec((H, 2 * CHUNK, CHUNK), lambda i: (0, 0, 0)), pl.BlockSpec((H, 1), lambda i: (0, 0))],
        out_shape=[SDS((T, P), BF16), SDS((H, 2 * CHUNK, CHUNK), F32), SDS((H, 1), F32)],
        scratch_shapes=[pltpu.VMEM((CHUNK, 2 * LANE), F32), pltpu.VMEM((tq + CHUNK, 2 * LANE), F32),
                        pltpu.VMEM((H, LANE), F32)],
        compiler_params=_params(1))(sinks, proj, proj, proj, proj, proj, do, duv, bias_t)


def _bias_bwd(dbias, onehot):
    H = dbias.shape[0]
    nbk = onehot.shape[1]

    def body(d_ref, oh_ref, o_ref):
        hi, mid, lo = _split3(d_ref[...])
        oh = oh_ref[...]
        o_ref[...] = _dot(hi, oh, NN) + _dot(mid, oh, NN) + _dot(lo, oh, NN)

    return _CHAIN.call(body, name="bias_bwd", in_specs=[VMEM_SPEC] * 2, out_specs=VMEM_SPEC, out_shape=SDS((H, nbk), F32),
                       compiler_params=_params(0))(dbias, onehot)


def _inproj_bwd(dproj, w_t, x, dh1, g):
    T, P = dproj.shape
    D = x.shape[1]
    tm = _tile(T, 512)

    def body(dp_ref, w_ref, x_ref, dh_ref, g_ref, dx_ref, dg_ref):
        @pl.when(pl.program_id(0) == 0)
        def _():
            dg_ref[...] = jnp.zeros_like(dg_ref)

        dn = _dot(dp_ref[...], w_ref[...], NN)
        xv = x_ref[...]
        r = _rms_stats(xv)
        dg_ref[...] += jnp.sum(dn * (xv * r), axis=0, keepdims=True)
        dx_ref[...] = dh_ref[...] + _rms_bwd(dn, xv, r, g_ref[...])

    return _CHAIN.call(
        body, name="inproj_bwd", grid=(T // tm,),
        in_specs=[pl.BlockSpec((tm, P), lambda i: (i, 0)), _resident((P, D)),
                  pl.BlockSpec((tm, D), lambda i: (i, 0)), pl.BlockSpec((tm, D), lambda i: (i, 0)),
                  pl.BlockSpec((1, D), lambda i: (0, 0))],
        out_specs=[pl.BlockSpec((tm, D), lambda i: (i, 0)), pl.BlockSpec((1, D), lambda i: (0, 0))],
        out_shape=[SDS((T, D), F32), SDS((1, D), F32)], compiler_params=_params(1))(dproj, w_t, x, dh1, g)


def _adamw(w, g, m, v):
    m = ADAM_B1 * m + (1.0 - ADAM_B1) * g
    v = ADAM_B2 * v + (1.0 - ADAM_B2) * (g * g)
    m_hat = m / (1.0 - ADAM_B1 ** ADAM_STEP)
    v_hat = v / (1.0 - ADAM_B2 ** ADAM_STEP)
    delta = -ADAM_LR * (m_hat / (jnp.sqrt(v_hat) + ADAM_EPS) + ADAM_WD * w)
    return delta, m, v


def _adam_sharded(csum, recv, w, m, v, name):
    R, C = w.shape
    tr = _tile(R, 256, 16)

    def body(own_ref, recv_ref, w_ref, m_ref, v_ref, g_ref, d_ref, nm_ref, nv_ref):
        g = own_ref[...].astype(F32)
        for r in range(3):
            g = g + recv_ref[r].astype(F32)
        delta, nm, nv = _adamw(w_ref[...], g, m_ref[...], v_ref[...])
        g_ref[...] = g
        d_ref[...] = delta
        nm_ref[...] = nm
        nv_ref[...] = nv

    blk = pl.BlockSpec((tr, C), lambda i: (i, 0))
    return _CHAIN.call(
        body, name=name, grid=(R // tr,),
        in_specs=[pl.BlockSpec((None, tr, C), lambda i: (0, i, 0)), pl.BlockSpec((3, tr, C), lambda i: (0, i, 0)),
                  blk, blk, blk],
        out_specs=[blk] * 4, out_shape=[SDS((R, C), F32)] * 4, compiler_params=_params(1))(csum, recv, w, m, v)


def _rows2d(shape):
    return (int(np.prod(shape[:-1])) if len(shape) > 1 else 1, shape[-1])


def _small_layout(shapes):
    totals, places = {}, []
    for s in shapes:
        r, w = _rows2d(s)
        off = totals.get(w, 0)
        places.append((w, off, r))
        totals[w] = off + -(-r // 8) * 8
    return {w: -(-t // 32) * 32 for w, t in totals.items()}, places


def _pack_small(arrays, totals, places):
    bufs = []
    for w, total in totals.items():
        buf = jnp.zeros((total, w), F32)
        for a, (pw, off, r) in zip(arrays, places):
            if pw == w:
                buf = lax.dynamic_update_slice(buf, a.reshape(r, w).astype(F32), (off, 0))
        bufs.append(buf)
    return bufs


def _adam_small(gathered, totals, places, ws, ms, vs):
    widths = list(totals)
    n, nw = len(places), len(widths)

    def body(*refs):
        gath, params, outs = refs[:nw], refs[nw:nw + 3 * n], refs[nw + 3 * n:]
        for p, (w, off, r) in enumerate(places):
            g_ref = gath[widths.index(w)]
            g = g_ref[0, off:off + r, :]
            for d in range(1, N_DEV):
                g = g + g_ref[d, off:off + r, :]
            delta, nm, nv = _adamw(params[p][...], g, params[n + p][...], params[2 * n + p][...])
            for k, val in enumerate((g, delta, nm, nv)):
                outs[4 * p + k][...] = val

    shapes2d = [SDS((r, w), F32) for w, _, r in places for _ in range(4)]
    outs = _CHAIN.call(body, name="adam_small", in_specs=[VMEM_SPEC] * (nw + 3 * n), out_specs=[VMEM_SPEC] * (4 * n),
                       out_shape=shapes2d, compiler_params=_params(0))(*gathered, *ws, *ms, *vs)
    return [outs[4 * p:4 * p + 4] for p in range(n)]


def kernel(x, rel_bias_table, mix_norm_g, w_in, gate_norm_g, gate_norm_b, w_spatial, b_spatial, attn_sinks, out_norm_a_g, out_norm_b_g, w_out, ffn_norm_g, w_up, w_down, final_norm_g, loss_target, m_rel_bias_table, m_mix_norm_g, m_w_in, m_gate_norm_g, m_gate_norm_b, m_w_spatial, m_b_spatial, m_attn_sinks, m_out_norm_a_g, m_out_norm_b_g, m_w_out, m_ffn_norm_g, m_w_up, m_w_down, m_final_norm_g, v_rel_bias_table, v_mix_norm_g, v_w_in, v_gate_norm_g, v_gate_norm_b, v_w_spatial, v_b_spatial, v_attn_sinks, v_out_norm_a_g, v_out_norm_b_g, v_w_out, v_ffn_norm_g, v_w_up, v_w_down, v_final_norm_g):
    T, D = x.shape[1], x.shape[2]
    A = D // 2
    B = D // 2
    H = B // HEAD_DIM
    P = 2 * A + B + 2 * KV_HEADS * HEAD_DIM
    xs = x.reshape(T, D)
    target = loss_target.reshape(T, D)

    win_t, m_win_t, v_win_t = (jnp.swapaxes(a[0], 0, 1) for a in (w_in, m_w_in, v_w_in))
    shards = [win_t.astype(BF16), w_out[0].astype(BF16), w_up[0].astype(BF16), w_down[0].astype(BF16)]
    _CHAIN.token = None
    gather = _gather_begin(shards)
    _gather_step(gather, [(0, 0)], "gather_start")

    g1, g2, g3 = mix_norm_g.reshape(1, D), ffn_norm_g.reshape(1, D), final_norm_g.reshape(1, D)
    lg, lb = gate_norm_g.reshape(1, A), gate_norm_b.reshape(1, A)
    ws = w_spatial[0]
    ws_t = jnp.swapaxes(ws, 1, 2)
    bs_t = jnp.transpose(b_spatial[0])
    ga, gb = out_norm_a_g.reshape(1, A), out_norm_b_g.reshape(1, B)
    sinks = attn_sinks.reshape(H)
    bucket, in_window = _t5_bucket()
    onehot_np = ((bucket[:, :, None] == np.arange(N_BUCKETS)) & in_window[:, :, None]).astype(np.float32)
    onehot = jnp.asarray(onehot_np.reshape(-1, N_BUCKETS)).astype(BF16)
    onehot_kq = jnp.asarray(onehot_np.transpose(1, 0, 2).reshape(-1, N_BUCKETS)).astype(BF16)

    bias, bias_t = _bias_fwd(jnp.transpose(rel_bias_table), jnp.transpose(onehot), jnp.transpose(onehot_kq))
    bias, bias_t = bias.reshape(H, CHUNK, 2 * CHUNK), bias_t.reshape(H, 2 * CHUNK, CHUNK)
    n1 = _mix_norm(xs, g1)
    _gather_step(gather, [(0, 1), (1, 0), (2, 0)], "gather_in_1")
    _gather_step(gather, [(0, 2)], "gather_in_2")
    (win_g,) = _gather_end(gather, [0], "gather_in_end")
    win_t_full = win_g.reshape(P, D)
    proj = _inproj_fwd(n1, win_t_full)
    _gather_step(gather, [(1, 1)], "gather_out_1")
    a_out = _gmlp_fwd(proj, lg, lb, ws, bs_t, A)
    _gather_step(gather, [(1, 2), (2, 1), (3, 0)], "gather_out_2_up_1")
    b_out = _attn_fwd(proj, bias, sinks, A, B)
    (wout_g,) = _gather_end(gather, [1], "gather_out_end")
    _gather_step(gather, [(2, 2)], "gather_up_2")
    wout_full = wout_g.reshape(A + B, D)
    h1, mixed, n2 = _outproj_fwd(a_out, b_out, ga, gb, xs, wout_full, g2)
    (wup_g,) = _gather_end(gather, [2], "gather_up_end")
    _gather_step(gather, [(3, 1)], "gather_down_1")
    wup_t = jnp.transpose(wup_g, (0, 2, 1)).reshape(-1, D)
    z = _ffn_up(n2, wup_g)
    _gather_step(gather, [(3, 2)], "gather_down_2")
    (wdown_g,) = _gather_end(gather, [3], "gather_down_end")
    h2 = _ffn_down(h1, z, wdown_g.reshape(-1, D))
    loss_part, dg3, dh2, dh2b = _final_loss(h2, g3, target)

    def reduce_to_chip(state, name):
        csums = [_chip_sum(part, received, "%s_chip_sum_%d" % (name, a))
                 for a, (part, received) in enumerate(_sibling_exchange_end(state, name + "_sib_end"))]
        return _chip_exchange_begin(csums, name + "_chip")

    dwdown = _matmul_tn(z, dh2b, "grad_w_down", square_a=True).reshape(wdown_g.shape)
    dzp = _ffn_down_bwd(dh2b, z, wdown_g.reshape(-1, D))
    dwup = _matmul_tn(n2, dzp, "grad_w_up", col_blocks=N_DEV)
    sib_ffn = _sibling_exchange_begin([dwdown, dwup], "rs_ffn_sib")
    dh1, dh1b, dg2 = _ffn_norm_bwd(_ffn_up_bwd(dzp, wup_t), dh2, h1, g2)
    chip_ffn = reduce_to_chip(sib_ffn, "rs_ffn")
    da, db, dga, dgb = _outproj_bwd(dh1b, wout_full, a_out, b_out, ga, gb)
    dwout = _matmul_tn(mixed, dh1b, "grad_w_out").reshape(wout_g.shape)
    sib_out = _sibling_exchange_begin([dwout], "rs_out_sib")
    duv, dlg, dlb, dws, dbs_t = _gmlp_bwd(proj, da, lg, lb, ws, ws_t, bs_t, A)
    dproj, dbias_t, dsinks = _attn_bwd(proj, db, duv, bias_t, sinks, A, B)
    chip_out = reduce_to_chip(sib_out, "rs_out")
    dwin_t = _matmul_tn(dproj, n1, "grad_w_in").reshape(win_g.shape)
    sib_in = _sibling_exchange_begin([dwin_t], "rs_in_sib")
    dtable_t = _bias_bwd(dbias_t.reshape(H, -1), onehot_kq)
    chip_in = reduce_to_chip(sib_in, "rs_in")
    grad_x, dg1 = _inproj_bwd(dproj, win_t_full, xs, dh1, g1)

    small_w = [rel_bias_table, mix_norm_g, gate_norm_g, gate_norm_b, w_spatial, b_spatial, attn_sinks,
               out_norm_a_g, out_norm_b_g, ffn_norm_g, final_norm_g]
    small_m = [m_rel_bias_table, m_mix_norm_g, m_gate_norm_g, m_gate_norm_b, m_w_spatial, m_b_spatial, m_attn_sinks,
               m_out_norm_a_g, m_out_norm_b_g, m_ffn_norm_g, m_final_norm_g]
    small_v = [v_rel_bias_table, v_mix_norm_g, v_gate_norm_g, v_gate_norm_b, v_w_spatial, v_b_spatial, v_attn_sinks,
               v_out_norm_a_g, v_out_norm_b_g, v_ffn_norm_g, v_final_norm_g]
    small_g = [jnp.transpose(dtable_t), dg1, dlg, dlb, dws, jnp.transpose(dbs_t), dsinks, dga, dgb, dg2, dg3]
    nothing = jnp.zeros((1, H), F32)
    small_w, small_m, small_v = small_w + [nothing], small_m + [nothing], small_v + [nothing]
    small_g = small_g + [jnp.broadcast_to(loss_part, (1, H))]
    shapes = [w.shape for w in small_w]
    totals, places = _small_layout(shapes)
    as_rows = lambda arrays: [a.reshape(_rows2d(a.shape)) for a in arrays]
    big = [None] * 4

    def adam_of(k, state, a, w, m, v):
        csum, received = _chip_exchange_end(state, a, "rs_%d_end" % k)
        big[k] = _adam_sharded(csum, received, w, m, v, "adam_%d" % k)

    small_gather = _gather_begin(_pack_small(small_g, totals, places))
    every = range(len(totals))
    _gather_step(small_gather, [(a, 0) for a in every], "small_gather_start")
    adam_of(3, chip_ffn, 0, w_down[0], m_w_down[0], v_w_down[0])
    _gather_step(small_gather, [(a, 1) for a in every], "small_gather_1")
    adam_of(2, chip_ffn, 1, w_up[0], m_w_up[0], v_w_up[0])
    _gather_step(small_gather, [(a, 2) for a in every], "small_gather_2")
    adam_of(1, chip_out, 0, w_out[0], m_w_out[0], v_w_out[0])
    adam_of(0, chip_in, 0, win_t, m_win_t, v_win_t)
    gathered = _gather_end(small_gather, list(every), "small_gather_end")
    small_out = _adam_small(gathered, totals, places, as_rows(small_w), as_rows(small_m), as_rows(small_v))
    sg, sd, sm, sv = [[outs[k].reshape(s) for outs, s in zip(small_out, shapes)] for k in range(4)]
    big[0] = [jnp.swapaxes(o, 0, 1) for o in big[0]]
    big = [[o.reshape(w.shape) for o in outs] for outs, w in zip(big, (w_in, w_out, w_up, w_down))]

    loss = sg[-1][0, 0]

    order = ["s0", "s1", "b0", "s2", "s3", "s4", "s5", "s6", "s7", "s8", "b1", "s9", "b2", "b3", "s10"]

    def group(idx):
        small = (sg, sd, sm, sv)[idx]
        return [small[int(t[1:])] if t[0] == "s" else big[int(t[1:])][idx] for t in order]

    return (loss, grad_x.reshape(x.shape), *group(0), *group(1), *group(2), *group(3))
```

```python
import math

import numpy as np
import jax
import jax.numpy as jnp
from jax import lax
from jax.experimental import pallas as pl
from jax.experimental.pallas import tpu as pltpu

F32 = jnp.float32
BF16 = jnp.bfloat16
SDS = jax.ShapeDtypeStruct
MESH = pl.DeviceIdType.MESH

N_DEV = 8
EPS = 1e-5
NEG = -1e30
CHUNK = 128
GROUP_DIM = 128
HEAD_DIM = 64
KV_HEADS = 2
N_BUCKETS = 32
MAX_DISTANCE = 128
ADAM_LR, ADAM_B1, ADAM_B2, ADAM_EPS, ADAM_WD, ADAM_STEP = 0.001, 0.9, 0.999, 1e-08, 0.01, 10
GELU_C0 = math.sqrt(2.0 / math.pi)
GELU_C1 = 0.044715

V7X_VMEM_BYTES = 64 * 1024 * 1024
VMEM_LIMIT = V7X_VMEM_BYTES - 8 * 1024 * 1024
LANE = 128

NN = ((1,), (0,))
NT = ((1,), (1,))
TN = ((0,), (0,))


def _dot(a, b, dims):
    return lax.dot_general(a, b, (dims, ((), ())), preferred_element_type=F32)


def _tile(n, pref, unit=LANE):
    best = None
    for t in range(unit, min(n, pref) + 1, unit):
        if n % t == 0:
            best = t
    return n if best is None else best


def _params(n_grid):
    return pltpu.CompilerParams(dimension_semantics=("arbitrary",) * n_grid, vmem_limit_bytes=VMEM_LIMIT)


def _resident(shape):
    return pl.BlockSpec(shape, lambda i: (0, 0), pipeline_mode=pl.Buffered(1))


def _gelu(x):
    return 0.5 * x * (1.0 + jnp.tanh(GELU_C0 * (x + GELU_C1 * x * x * x)))


def _gelu_and_grad(x):
    x2 = x * x
    t = jnp.tanh(GELU_C0 * x * (1.0 + GELU_C1 * x2))
    val = 0.5 * x * (1.0 + t)
    grad = 0.5 * (1.0 + t) + 0.5 * x * (1.0 - t * t) * (GELU_C0 * (1.0 + 3.0 * GELU_C1 * x2))
    return val, grad


def _rms_stats(x):
    return lax.rsqrt(jnp.mean(x * x, axis=-1, keepdims=True) + EPS)


def _rms_bwd(dy, x, r, g):
    w = dy * g
    return r * w - x * (r * r * r) * jnp.mean(w * x, axis=-1, keepdims=True)


def _t5_bucket():
    i = np.arange(CHUNK)[:, None]
    j = np.arange(2 * CHUNK)[None, :]
    rel = np.maximum(i + CHUNK - j, 0)
    n_exact = N_BUCKETS // 2
    relf = np.maximum(rel, n_exact).astype(np.float32)
    large = n_exact + (np.log(relf / np.float32(n_exact)) / np.float32(math.log(MAX_DISTANCE / n_exact))
                       * np.float32(N_BUCKETS - n_exact)).astype(np.int32)
    large = np.minimum(large, N_BUCKETS - 1)
    bucket = np.where(rel < n_exact, rel, large)
    in_window = (i + CHUNK - j >= 0) & (i + CHUNK - j < CHUNK)
    return bucket.astype(np.int32), in_window


def _split3(x):
    hi = x.astype(BF16)
    r1 = x - hi.astype(F32)
    mid = r1.astype(BF16)
    lo = (r1 - mid.astype(F32)).astype(BF16)
    return hi, mid, lo


HBM_SPEC = pl.BlockSpec(memory_space=pltpu.HBM)


def _mesh_pos():
    return lax.axis_index("x"), lax.axis_index("y"), lax.axis_index("c")


def _dev_index(px, py, pc):
    return 4 * px + 2 * py + pc


SEM_SPEC = pl.BlockSpec(memory_space=pltpu.SEMAPHORE)
ANY_SPEC = pl.BlockSpec(memory_space=pl.ANY)
VMEM_SPEC = pl.BlockSpec(memory_space=pltpu.VMEM)
TOKEN = SDS((8, LANE), F32)
SIDE_EFFECT = pltpu.SideEffectType.DATAFLOW_SIDE_EFFECTING


def _hbm(x):
    return pltpu.with_memory_space_constraint(x, pltpu.HBM)


class _CallChain:
    def __init__(self):
        self.token = None

    def call(self, body, *, in_specs, out_specs, out_shape, **kwargs):
        dep, n_in = self.token, len(in_specs)
        single = not isinstance(out_shape, (list, tuple))
        out_shapes = [out_shape] if single else list(out_shape)
        out_specs = [out_specs] if single else list(out_specs)
        n_out = len(out_shapes)
        n_dep = 0 if dep is None else 1
        token_spec = pl.BlockSpec((8, LANE), lambda *_: (0, 0)) if kwargs.get("grid") else VMEM_SPEC

        def chained(*refs):
            outs_at = n_in + n_dep
            body(*refs[:n_in], *refs[outs_at:outs_at + n_out], *refs[outs_at + n_out + 1:])
            token = refs[outs_at + n_out]
            token[...] = jnp.zeros_like(token)

        inner = pl.pallas_call(chained, in_specs=list(in_specs) + [ANY_SPEC] * n_dep, out_specs=out_specs + [token_spec],
                               out_shape=out_shapes + [TOKEN], **kwargs)

        def run(*operands):
            outs = inner(*operands) if dep is None else inner(*operands, dep)
            self.token = outs[n_out]
            return outs[0] if single else list(outs[:n_out])

        return run


_CHAIN = _CallChain()


def _wait_all(waits, x, y, c):
    for kind, src, dst, send_sem, recv_sem in waits:
        cp = pltpu.make_async_remote_copy(src_ref=src, dst_ref=dst, send_sem=send_sem, recv_sem=recv_sem,
                                          device_id=(x, y, c), device_id_type=MESH)
        if kind == "send":
            cp.wait_send()
        else:
            cp.wait_recv()


def _split_start(bufs, copies_of, n_sems, name, sem_sets=(), waits_of=None):
    n, ns = len(bufs), len(sem_sets)
    flat_sems = [s for pair in sem_sets for s in pair]

    def body(*refs):
        ins = refs[:n]
        sems = refs[n:n + 2 * ns]
        send_sems, recv_sems = refs[n + 2 * ns], refs[n + 2 * ns + 1]
        if waits_of is not None:
            _wait_all(waits_of(ins, [(sems[2 * i], sems[2 * i + 1]) for i in range(ns)]), *_mesh_pos())
        for src, dst, k, target in copies_of(ins):
            pltpu.make_async_remote_copy(src_ref=src, dst_ref=dst, send_sem=send_sems.at[k], recv_sem=recv_sems.at[k],
                                         device_id=target, device_id_type=MESH).start()

    outs = _CHAIN.call(
        body, name=name,
        out_shape=[pltpu.SemaphoreType.DMA((n_sems,)), pltpu.SemaphoreType.DMA((n_sems,))]
        + [pltpu.HBM(b.shape, b.dtype) for b in bufs],
        in_specs=[HBM_SPEC] * n + [SEM_SPEC] * (2 * ns), out_specs=[SEM_SPEC, SEM_SPEC] + [HBM_SPEC] * n,
        input_output_aliases={a: 2 + a for a in range(n)},
        compiler_params=pltpu.CompilerParams(has_side_effects=SIDE_EFFECT),
    )(*[_hbm(b) for b in bufs], *flat_sems)
    return outs[0], outs[1], list(outs[2:2 + n])


def _split_wait(bufs, sem_sets, waits_of, name):
    n, ns = len(bufs), len(sem_sets)
    flat_sems = [s for pair in sem_sets for s in pair]

    def body(*refs):
        ins = refs[:n]
        sems = refs[n:n + 2 * ns]
        _wait_all(waits_of(ins, [(sems[2 * i], sems[2 * i + 1]) for i in range(ns)]), *_mesh_pos())

    outs = _CHAIN.call(
        body, name=name,
        out_shape=[pltpu.HBM(b.shape, b.dtype) for b in bufs],
        in_specs=[HBM_SPEC] * n + [SEM_SPEC] * (2 * ns), out_specs=[HBM_SPEC] * n,
        input_output_aliases={a: a for a in range(n)},
        compiler_params=pltpu.CompilerParams(has_side_effects=SIDE_EFFECT),
    )(*bufs, *flat_sems)
    return list(outs)


def _gather_blocks(land):
    rows = land.shape[1]
    first = (rows // 2) // 16 * 16

    def block(px, py, pc):
        return land.at[_dev_index(px, py, pc)]

    def halves(px, py, pc):
        return (land.at[_dev_index(px, py, pc), pl.ds(0, first)], land.at[_dev_index(px, py, pc), pl.ds(first, rows - first)])

    return block, halves


def _gather_begin(shards):
    me = _dev_index(*_mesh_pos())
    lands = [lax.dynamic_update_index_in_dim(lax.empty((N_DEV,) + s.shape, s.dtype), s, me, 0) for s in shards]
    return dict(lands=lands, stage={})


STAGE_COPIES = (3, 4, 1)


def _gather_step(state, items, name):
    which = sorted({a for a, _ in items})
    at = {a: i for i, a in enumerate(which)}
    sem_sets = [state["stage"][(a, s - 1)][0] for a, s in items if s > 0]
    offset, n_sems = {}, 0
    for a, s in items:
        offset[(a, s)] = n_sems
        n_sems += STAGE_COPIES[s]

    def waits_of(ins, sems):
        x, y, c = _mesh_pos()
        out, earlier = [], 0
        for a, s in items:
            if s == 0:
                continue
            block, halves = _gather_blocks(ins[at[a]])
            send, recv = sems[earlier]
            off = state["stage"][(a, s - 1)][1]
            earlier += 1
            if s == 1:
                arrived = [(1, block(1 - x, y, c)), (2, block(x, 1 - y, c))]
            else:
                arrived = list(zip((2, 3), halves(1 - x, 1 - y, c)))
            out += [("recv", ref, ref, send.at[off + k], recv.at[off + k]) for k, ref in arrived]
        return out

    def copies_of(ins):
        x, y, c = _mesh_pos()
        sibling = (x, y, 1 - c)
        out = []
        for a, s in items:
            block, halves = _gather_blocks(ins[at[a]])
            off = offset[(a, s)]
            if s == 0:
                mine = block(x, y, c)
                out += [(mine, mine, off + 1, (1 - x, y, c)), (mine, mine, off + 2, (x, 1 - y, c)), (mine, mine, off, sibling)]
            elif s == 1:
                from_x, from_y = block(1 - x, y, c), block(x, 1 - y, c)
                out += [(halves(1 - x, y, c)[0], halves(1 - x, y, c)[0], off + 2, (x, 1 - y, c)),
                        (halves(x, 1 - y, c)[1], halves(x, 1 - y, c)[1], off + 3, (1 - x, y, c)),
                        (from_x, from_x, off, sibling), (from_y, from_y, off + 1, sibling)]
            else:
                diag = block(1 - x, 1 - y, c)
                out.append((diag, diag, off, sibling))
        return out

    send_sems, recv_sems, bufs = _split_start([state["lands"][a] for a in which], copies_of, n_sems, name,
                                              sem_sets=sem_sets, waits_of=waits_of)
    for a in which:
        state["lands"][a] = bufs[at[a]]
    for a, s in items:
        state["stage"][(a, s)] = ((send_sems, recv_sems), offset[(a, s)])


def _gather_end(state, which, name):
    sem_sets = [state["stage"][(a, s)][0] for a in which for s in range(3)]

    def waits(ins, sems):
        x, y, c = _mesh_pos()
        out = []
        for i, a in enumerate(which):
            block, halves = _gather_blocks(ins[i])
            (b_send, b_recv), (s1_send, s1_recv), (s2_send, s2_recv) = sems[3 * i:3 * i + 3]
            o0, o1, o2 = (state["stage"][(a, s)][1] for s in range(3))
            arrivals = [(block(x, y, 1 - c), b_send, b_recv, o0),
                        (block(1 - x, y, 1 - c), s1_send, s1_recv, o1), (block(x, 1 - y, 1 - c), s1_send, s1_recv, o1 + 1),
                        (block(1 - x, 1 - y, 1 - c), s2_send, s2_recv, o2)]
            mine = block(x, y, c)
            sent = [(mine, b_send, b_recv, o0 + k) for k in range(3)]
            sent += [(block(1 - x, y, c), s1_send, s1_recv, o1), (block(x, 1 - y, c), s1_send, s1_recv, o1 + 1),
                     (halves(1 - x, y, c)[0], s1_send, s1_recv, o1 + 2), (halves(x, 1 - y, c)[1], s1_send, s1_recv, o1 + 3),
                     (block(1 - x, 1 - y, c), s2_send, s2_recv, o2)]
            out += [("recv", ref, ref, s.at[k], r.at[k]) for ref, s, r, k in arrivals]
            out += [("send", ref, ref, s.at[k], r.at[k]) for ref, s, r, k in sent]
        return out

    bufs = _split_wait([state["lands"][a] for a in which], sem_sets, waits, name)
    for i, a in enumerate(which):
        state["lands"][a] = bufs[i]
    return bufs


def _sibling_exchange_begin(parts, name):
    lands = [lax.empty((4,) + p.shape[1:], p.dtype) for p in parts]
    n = len(parts)

    def copies_of(ins):
        x, y, c = _mesh_pos()
        return [(ins[a].at[2 * j + (1 - c)], ins[n + a].at[j], 4 * a + j, (x, y, 1 - c)) for a in range(n) for j in range(4)]

    send_sems, recv_sems, bufs = _split_start(list(parts) + lands, copies_of, 4 * n, name)
    return dict(bufs=bufs, sems=(send_sems, recv_sems), n=n)


def _sibling_exchange_end(state, name):
    n = state["n"]

    def waits(ins, sems):
        _, _, c = _mesh_pos()
        return [(kind, ins[a].at[2 * j + (1 - c)], ins[n + a].at[j], sems[0][0].at[4 * a + j], sems[0][1].at[4 * a + j])
                for a in range(n) for j in range(4) for kind in ("send", "recv")]

    bufs = _split_wait(state["bufs"], [state["sems"]], waits, name)
    return [(bufs[a], bufs[n + a]) for a in range(n)]


CHIP_FLIPS = (2, 1, 3)


def _chip_exchange_begin(csums, name):
    lands = [lax.empty((3,) + s.shape[1:], s.dtype) for s in csums]
    n = len(csums)

    def copies_of(ins):
        x, y, c = _mesh_pos()
        chips = [(1 - x, y), (x, 1 - y), (1 - x, 1 - y)]
        return [(ins[a].at[CHIP_FLIPS[r]], ins[n + a].at[r], 3 * a + r, (px, py, c))
                for a in range(n) for r, (px, py) in enumerate(chips)]

    send_sems, recv_sems, bufs = _split_start(list(csums) + lands, copies_of, 3 * n, name)
    return dict(bufs=bufs, sems=(send_sems, recv_sems), n=n)


def _chip_exchange_end(state, a, name):
    n = state["n"]

    def waits(ins, sems):
        return [(kind, ins[0].at[CHIP_FLIPS[r]], ins[1].at[r], sems[0][0].at[3 * a + r], sems[0][1].at[3 * a + r])
                for r in range(3) for kind in ("send", "recv")]

    csum, received = _split_wait([state["bufs"][a], state["bufs"][n + a]], [state["sems"]], waits, name)
    return csum, received


def _chip_sum(part, recv, name):
    _, R, C = part.shape
    tr = _tile(R, 2048, 16)
    place = jnp.stack([lax.axis_index("c"), 2 * lax.axis_index("x") + lax.axis_index("y")]).astype(jnp.int32)

    def body(place_ref, p_ref, r_ref, o_ref):
        o_ref[...] = (p_ref[...].astype(F32) + r_ref[...].astype(F32)).astype(o_ref.dtype)

    def chip(p, place_ref):
        return jnp.bitwise_xor(p, place_ref[1])

    grid_spec = pltpu.PrefetchScalarGridSpec(
        num_scalar_prefetch=1, grid=(4, R // tr),
        in_specs=[pl.BlockSpec((None, tr, C), lambda p, i, place_ref: (2 * chip(p, place_ref) + place_ref[0], i, 0)),
                  pl.BlockSpec((None, tr, C), lambda p, i, place_ref: (chip(p, place_ref), i, 0))],
        out_specs=pl.BlockSpec((None, tr, C), lambda p, i, place_ref: (p, i, 0)))
    return pl.pallas_call(body, name=name, grid_spec=grid_spec, out_shape=SDS((4, R, C), part.dtype),
                          compiler_params=_params(2))(place, part, recv)


def _bias_fwd(table_t, onehot_t, onehot_kq_t):
    H = table_t.shape[0]
    n = onehot_t.shape[1]

    def body(t_ref, oh_ref, oh_kq_ref, o_ref, o_kq_ref):
        hi, mid, lo = _split3(t_ref[...])
        for src, dst in ((oh_ref, o_ref), (oh_kq_ref, o_kq_ref)):
            oh = src[...]
            dst[...] = _dot(hi, oh, NN) + _dot(mid, oh, NN) + _dot(lo, oh, NN)

    return _CHAIN.call(body, name="bias_fwd", in_specs=[VMEM_SPEC] * 3, out_specs=[VMEM_SPEC] * 2,
                       out_shape=[SDS((H, n), F32)] * 2, compiler_params=_params(0))(table_t, onehot_t, onehot_kq_t)


def _mix_norm(x, g):
    T, D = x.shape
    tm = _tile(T, 512)

    def body(x_ref, g_ref, n_ref):
        xv = x_ref[...]
        n_ref[...] = (xv * _rms_stats(xv) * g_ref[...]).astype(BF16)

    row = pl.BlockSpec((tm, D), lambda i: (i, 0))
    return _CHAIN.call(body, name="mix_norm", grid=(T // tm,), in_specs=[row, pl.BlockSpec((1, D), lambda i: (0, 0))],
                       out_specs=row, out_shape=SDS((T, D), BF16), compiler_params=_params(1))(x, g)


def _inproj_fwd(n, w_t):
    T, D = n.shape
    P = w_t.shape[0]
    tm = _tile(T, 512)

    def body(n_ref, w_ref, proj_ref):
        proj_ref[...] = _dot(n_ref[...], w_ref[...], NT)

    return _CHAIN.call(
        body, name="inproj_fwd", grid=(T // tm,),
        in_specs=[pl.BlockSpec((tm, D), lambda i: (i, 0)), _resident((P, D))],
        out_specs=pl.BlockSpec((tm, P), lambda i: (i, 0)),
        out_shape=SDS((T, P), F32), compiler_params=_params(1))(n, w_t)


def _layer_norm_group(vg, lg, lb):
    mu = jnp.mean(vg, axis=-1, keepdims=True)
    xc = vg - mu
    rstd = lax.rsqrt(jnp.mean(xc * xc, axis=-1, keepdims=True) + EPS)
    vhat = xc * rstd
    return vhat, rstd, vhat * lg + lb


def _gmlp_fwd(proj, lg, lb, w_s, bs_t, A):
    T = proj.shape[0]
    G = A // GROUP_DIM
    tm = _tile(T, 512)
    nc = tm // CHUNK

    def body(u_ref, v_ref, lg_ref, lb_ref, w_ref, bst_ref, a_ref):
        row = lax.broadcasted_iota(jnp.int32, (CHUNK, CHUNK), 0)
        col = lax.broadcasted_iota(jnp.int32, (CHUNK, CHUNK), 1)
        causal = row >= col
        for g in range(G):
            sl = slice(g * GROUP_DIM, (g + 1) * GROUP_DIM)
            _, _, vn = _layer_norm_group(_gelu(v_ref[:, sl]), lg_ref[:, sl], lb_ref[:, sl])
            vnb = vn.astype(BF16)
            wm = jnp.where(causal, w_ref[g], 0.0).astype(BF16)
            ug = _gelu(u_ref[:, sl])
            bcol = bst_ref[:, g:g + 1]
            for c in range(nc):
                rs = slice(c * CHUNK, (c + 1) * CHUNK)
                a_ref[rs, sl] = ug[rs] * (_dot(wm, vnb[rs], NN) + bcol)

    return _CHAIN.call(
        body, name="gmlp_fwd", grid=(T // tm,),
        in_specs=[pl.BlockSpec((tm, A), lambda i: (i, 0)), pl.BlockSpec((tm, A), lambda i: (i, 1)),
                  pl.BlockSpec((1, A), lambda i: (0, 0)), pl.BlockSpec((1, A), lambda i: (0, 0)),
                  pl.BlockSpec((G, CHUNK, CHUNK), lambda i: (0, 0, 0)), pl.BlockSpec((CHUNK, G), lambda i: (0, 0))],
        out_specs=pl.BlockSpec((tm, A), lambda i: (i, 0)),
        out_shape=SDS((T, A), F32), compiler_params=_params(1))(proj, proj, lg, lb, w_s, bs_t)


def _attn_masks(first_tile):
    ii = lax.broadcasted_iota(jnp.int32, (CHUNK, 2 * CHUNK), 0)
    jj = lax.broadcasted_iota(jnp.int32, (CHUNK, 2 * CHUNK), 1)
    in_window = (jj > ii) & (jj <= ii + CHUNK)
    first_mask = in_window & jnp.logical_or(jnp.logical_not(first_tile), jj >= CHUNK)
    return in_window, first_mask


def _softmax_with_sink(s, sink, axis):
    m = jnp.maximum(jnp.max(s, axis=axis, keepdims=True), sink)
    p = jnp.exp(s - m)
    e_sink = jnp.exp(sink - m)
    inv = 1.0 / (jnp.sum(p, axis=axis, keepdims=True) + e_sink)
    return p * inv, e_sink * inv


def _pad_heads(band, group):
    lane = lax.broadcasted_iota(jnp.int32, band.shape, 1)
    if group == 0:
        low = jnp.where(lane < HEAD_DIM, band, 0.0)
        high = pltpu.roll(low, HEAD_DIM, 1)
    else:
        high = jnp.where(lane >= HEAD_DIM, band, 0.0)
        low = pltpu.roll(high, HEAD_DIM, 1)
    return low.astype(BF16), high.astype(BF16)


def _attn_specs(tq, A, B, reverse_tiles=None):
    nb = tq // CHUNK
    kcol = (2 * A + B) // LANE
    if reverse_tiles is None:
        tile = lambda i: i
    else:
        tile = lambda i: reverse_tiles - 1 - i
    prev = lambda i: jnp.maximum(tile(i) * nb - 1, 0)
    return [pl.BlockSpec((tq, B), lambda i: (tile(i), 2 * A // B)),
            pl.BlockSpec((tq, LANE), lambda i: (tile(i), kcol)),
            pl.BlockSpec((tq, LANE), lambda i: (tile(i), kcol + 1)),
            pl.BlockSpec((CHUNK, LANE), lambda i: (prev(i), kcol)),
            pl.BlockSpec((CHUNK, LANE), lambda i: (prev(i), kcol + 1))]


def _attn_fwd(proj, bias, sinks, A, B):
    T = proj.shape[0]
    H = B // HEAD_DIM
    qpk = H // KV_HEADS
    tq = _tile(T, 512)
    nb = tq // CHUNK

    scale = HEAD_DIM ** -0.5

    def body(sink_ref, q_ref, k_ref, v_ref, kp_ref, vp_ref, bias_ref, o_ref):
        in_window, first_mask = _attn_masks(pl.program_id(0) == 0)
        for b in range(nb):
            rows = slice(b * CHUNK, (b + 1) * CHUNK)
            if b == 0:
                kprev, vprev, mask = kp_ref[...], vp_ref[...], first_mask
            else:
                prows = slice((b - 1) * CHUNK, b * CHUNK)
                kprev, vprev, mask = k_ref[prows, :], v_ref[prows, :], in_window
            kband = jnp.concatenate([kprev, k_ref[rows, :]], axis=0)
            vband = jnp.concatenate([vprev, v_ref[rows, :]], axis=0)
            k_pads = [_pad_heads(kband, g) for g in range(KV_HEADS)]
            v_both = [jnp.concatenate(_pad_heads(vband, g), axis=0) for g in range(KV_HEADS)]
            scores = []
            for pair in range(H // 2):
                h = 2 * pair
                qs = (q_ref[rows, h * HEAD_DIM:(h + 2) * HEAD_DIM] * scale).astype(BF16)
                scores += [_dot(qs, kz, NT) for kz in k_pads[h // qpk]]
            probs = [_softmax_with_sink(jnp.where(mask, s + bias_ref[h], NEG), sink_ref[h], -1)[0].astype(BF16)
                     for h, s in enumerate(scores)]
            outs = [_dot(jnp.concatenate(probs[h:h + 2], axis=1), v_both[h // qpk], NN) for h in range(0, H, 2)]
            o_ref[rows, :] = jnp.concatenate(outs, axis=1)

    return _CHAIN.call(
        body, name="attn_fwd", grid=(T // tq,),
        in_specs=[pl.BlockSpec(memory_space=pltpu.SMEM)] + _attn_specs(tq, A, B)
        + [pl.BlockSpec((H, CHUNK, 2 * CHUNK), lambda i: (0, 0, 0))],
        out_specs=pl.BlockSpec((tq, B), lambda i: (i, 0)),
        out_shape=SDS((T, B), F32), compiler_params=_params(1))(sinks, proj, proj, proj, proj, proj, bias)


def _outproj_fwd(a, b, ga, gb, x, w, g_ffn):
    T, A = a.shape
    B = b.shape[1]
    D = x.shape[1]
    tm = _tile(T, 512)

    def body(a_ref, b_ref, ga_ref, gb_ref, x_ref, w_ref, gf_ref, h_ref, mix_ref, n_ref):
        av, bv = a_ref[...], b_ref[...]
        mix_ref[:, :A] = (av * _rms_stats(av) * ga_ref[...]).astype(BF16)
        mix_ref[:, A:] = (bv * _rms_stats(bv) * gb_ref[...]).astype(BF16)
        hv = x_ref[...] + _dot(mix_ref[...], w_ref[...], NN)
        h_ref[...] = hv
        n_ref[...] = (hv * _rms_stats(hv) * gf_ref[...]).astype(BF16)

    row = pl.BlockSpec((tm, D), lambda i: (i, 0))
    return _CHAIN.call(
        body, name="outproj_fwd", grid=(T // tm,),
        in_specs=[pl.BlockSpec((tm, A), lambda i: (i, 0)), pl.BlockSpec((tm, B), lambda i: (i, 0)),
                  pl.BlockSpec((1, A), lambda i: (0, 0)), pl.BlockSpec((1, B), lambda i: (0, 0)),
                  row, _resident((A + B, D)), pl.BlockSpec((1, D), lambda i: (0, 0))],
        out_specs=[row, pl.BlockSpec((tm, A + B), lambda i: (i, 0)), row],
        out_shape=[SDS((T, D), F32), SDS((T, A + B), BF16), SDS((T, D), BF16)],
        compiler_params=_params(1))(a, b, ga, gb, x, w, g_ffn)


def _ffn_up(n, w_up, z_so_far=None):
    T, D = n.shape
    Fb = w_up.shape[2]
    F = N_DEV * Fb
    tm, tf = _tile(T // 2, 1024), _tile(Fb, 1024)
    per = Fb // tf
    tiles = T // 2 // tm
    first = 0 if z_so_far is None else tiles

    def body(n_ref, wu_ref, *rest):
        z_ref = rest[-1]
        z_ref[...] = jnp.maximum(_dot(n_ref[...], wu_ref[...], NN), 0.0).astype(BF16)

    in_specs = [pl.BlockSpec((tm, D), lambda i, j: (i + first, 0)),
                pl.BlockSpec((None, D, tf), lambda i, j: (j // per, 0, j % per))]
    operands, extra = [n, w_up], {}
    if z_so_far is not None:
        in_specs.append(ANY_SPEC)
        operands.append(z_so_far)
        extra = dict(input_output_aliases={2: 0})
    return _CHAIN.call(
        body, name="ffn_up_%d" % (z_so_far is not None), grid=(tiles, F // tf), in_specs=in_specs,
        out_specs=pl.BlockSpec((tm, tf), lambda i, j: (i + first, j)),
        out_shape=SDS((T, F), BF16), compiler_params=_params(2), **extra)(*operands)


def _ffn_down(h1, z, w_down):
    T, D = h1.shape
    F = w_down.shape[0]
    tm, tn, tk = _tile(T, 1024), _tile(D, 1024), _tile(F, 4096)

    def body(h_ref, z_ref, wd_ref, h2_ref):
        k = pl.program_id(2)

        @pl.when(k == 0)
        def _():
            h2_ref[...] = h_ref[...]

        zf = z_ref[...].astype(F32)
        h2_ref[...] += _dot((zf * zf).astype(BF16), wd_ref[...], NN)

    return _CHAIN.call(
        body, name="ffn_down", grid=(T // tm, D // tn, F // tk),
        in_specs=[pl.BlockSpec((tm, tn), lambda i, j, k: (i, j)), pl.BlockSpec((tm, tk), lambda i, j, k: (i, k)),
                  pl.BlockSpec((tk, tn), lambda i, j, k: (k, j))],
        out_specs=pl.BlockSpec((tm, tn), lambda i, j, k: (i, j)),
        out_shape=SDS((T, D), F32), compiler_params=_params(3))(h1, z, w_down)


def _final_loss(h2, g, target):
    T, D = h2.shape
    tm = _tile(T, 512)

    def body(h_ref, g_ref, t_ref, loss_ref, dg_ref, dh_ref, dhb_ref):
        @pl.when(pl.program_id(0) == 0)
        def _():
            loss_ref[...] = jnp.zeros_like(loss_ref)
            dg_ref[...] = jnp.zeros_like(dg_ref)

        hv, gv = h_ref[...], g_ref[...]
        r = _rms_stats(hv)
        hn = hv * r
        e = hn * gv - t_ref[...]
        loss_ref[...] += (0.5 / D) * jnp.sum(jnp.sum(e * e, axis=0, keepdims=True), axis=-1, keepdims=True)
        dy = e * (1.0 / D)
        dg_ref[...] += jnp.sum(dy * hn, axis=0, keepdims=True)
        dh = _rms_bwd(dy, hv, r, gv)
        dh_ref[...] = dh
        dhb_ref[...] = dh.astype(BF16)

    return _CHAIN.call(
        body, name="final_loss", grid=(T // tm,),
        in_specs=[pl.BlockSpec((tm, D), lambda i: (i, 0)), pl.BlockSpec((1, D), lambda i: (0, 0)),
                  pl.BlockSpec((tm, D), lambda i: (i, 0))],
        out_specs=[pl.BlockSpec((1, 1), lambda i: (0, 0)), pl.BlockSpec((1, D), lambda i: (0, 0)),
                   pl.BlockSpec((tm, D), lambda i: (i, 0)), pl.BlockSpec((tm, D), lambda i: (i, 0))],
        out_shape=[SDS((1, 1), F32), SDS((1, D), F32), SDS((T, D), F32), SDS((T, D), BF16)],
        compiler_params=_params(1))(h2, g, target)


def _ffn_down_bwd(dh2b, z, w_down):
    T, D = dh2b.shape
    F = w_down.shape[0]
    tm, tf = _tile(T, 1024), _tile(F, 1024)

    def body(dh_ref, z_ref, wd_ref, dzp_ref):
        dzz = _dot(dh_ref[...], wd_ref[...], NT)
        dzp_ref[...] = (dzz * (2.0 * z_ref[...].astype(F32))).astype(BF16)

    return _CHAIN.call(
        body, name="ffn_down_bwd", grid=(T // tm, F // tf),
        in_specs=[pl.BlockSpec((tm, D), lambda i, j: (i, 0)), pl.BlockSpec((tm, tf), lambda i, j: (i, j)),
                  pl.BlockSpec((tf, D), lambda i, j: (j, 0))],
        out_specs=pl.BlockSpec((tm, tf), lambda i, j: (i, j)),
        out_shape=SDS((T, F), BF16), compiler_params=_params(2))(dh2b, z, w_down)


def _ffn_up_bwd(dzp, w_up_t):
    T, F = dzp.shape
    D = w_up_t.shape[1]
    tm, tn, tk = _tile(T, 1024), _tile(D, 1024), _tile(F, 4096)

    def body(dzp_ref, w_ref, dn_ref):
        part = _dot(dzp_ref[...], w_ref[...], NN)

        @pl.when(pl.program_id(2) == 0)
        def _():
            dn_ref[...] = part

        @pl.when(pl.program_id(2) > 0)
        def _():
            dn_ref[...] += part

    return _CHAIN.call(
        body, name="ffn_up_bwd", grid=(T // tm, D // tn, F // tk),
        in_specs=[pl.BlockSpec((tm, tk), lambda i, j, k: (i, k)), pl.BlockSpec((tk, tn), lambda i, j, k: (k, j))],
        out_specs=pl.BlockSpec((tm, tn), lambda i, j, k: (i, j)),
        out_shape=SDS((T, D), F32), compiler_params=_params(3))(dzp, w_up_t)


def _ffn_norm_bwd(dn, dh2, h1, g):
    T, D = h1.shape
    tm = _tile(T, 512)

    def body(dn_ref, dh_ref, h_ref, g_ref, dh1_ref, dh1b_ref, dg_ref):
        @pl.when(pl.program_id(0) == 0)
        def _():
            dg_ref[...] = jnp.zeros_like(dg_ref)

        hv, dnv = h_ref[...], dn_ref[...]
        r = _rms_stats(hv)
        dg_ref[...] += jnp.sum(dnv * (hv * r), axis=0, keepdims=True)
        dh1 = dh_ref[...] + _rms_bwd(dnv, hv, r, g_ref[...])
        dh1_ref[...] = dh1
        dh1b_ref[...] = dh1.astype(BF16)

    row = pl.BlockSpec((tm, D), lambda i: (i, 0))
    vec = pl.BlockSpec((1, D), lambda i: (0, 0))
    return _CHAIN.call(
        body, name="ffn_norm_bwd", grid=(T // tm,), in_specs=[row, row, row, vec], out_specs=[row, row, vec],
        out_shape=[SDS((T, D), F32), SDS((T, D), BF16), SDS((1, D), F32)], compiler_params=_params(1))(dn, dh2, h1, g)


def _matmul_tn(a, b, name, square_a=False, col_blocks=None):
    T, K = a.shape
    N = b.shape[1]
    tk = _tile(K, 1792)
    tn = _tile(N if col_blocks is None else N // col_blocks, 1024 if tk <= 1024 else 512)

    def body(a_ref, b_ref, o_ref):
        av = a_ref[...]
        if square_a:
            af = av.astype(F32)
            av = (af * af).astype(BF16)
        o_ref[...] = _dot(av, b_ref[...], TN).astype(o_ref.dtype)

    if col_blocks is None:
        out_shape = SDS((K, N), BF16)
        out_spec = pl.BlockSpec((tk, tn), lambda i, j: (i, j))
    else:
        per = (N // col_blocks) // tn
        out_shape = SDS((col_blocks, K, N // col_blocks), BF16)
        out_spec = pl.BlockSpec((None, tk, tn), lambda i, j: (j // per, i, j % per))
    return _CHAIN.call(
        body, name=name, grid=(K // tk, N // tn),
        in_specs=[pl.BlockSpec((T, tk), lambda i, j: (0, i)), pl.BlockSpec((T, tn), lambda i, j: (0, j))],
        out_specs=out_spec, out_shape=out_shape, compiler_params=_params(2))(a, b)


def _outproj_bwd(dh1b, w, a, b, ga, gb):
    T, D = dh1b.shape
    A, B = a.shape[1], b.shape[1]
    tm = _tile(T, 512)

    def body(dh_ref, w_ref, a_ref, b_ref, ga_ref, gb_ref, da_ref, db_ref, dga_ref, dgb_ref):
        @pl.when(pl.program_id(0) == 0)
        def _():
            dga_ref[...] = jnp.zeros_like(dga_ref)
            dgb_ref[...] = jnp.zeros_like(dgb_ref)

        dmix = _dot(dh_ref[...], w_ref[...], NT)
        for src_ref, g_ref, dx_ref, dg_ref, dn in ((a_ref, ga_ref, da_ref, dga_ref, dmix[:, :A]),
                                                   (b_ref, gb_ref, db_ref, dgb_ref, dmix[:, A:])):
            xv = src_ref[...]
            r = _rms_stats(xv)
            dg_ref[...] += jnp.sum(dn * (xv * r), axis=0, keepdims=True)
            dx_ref[...] = _rms_bwd(dn, xv, r, g_ref[...])

    return _CHAIN.call(
        body, name="outproj_bwd", grid=(T // tm,),
        in_specs=[pl.BlockSpec((tm, D), lambda i: (i, 0)), _resident((A + B, D)),
                  pl.BlockSpec((tm, A), lambda i: (i, 0)), pl.BlockSpec((tm, B), lambda i: (i, 0)),
                  pl.BlockSpec((1, A), lambda i: (0, 0)), pl.BlockSpec((1, B), lambda i: (0, 0))],
        out_specs=[pl.BlockSpec((tm, A), lambda i: (i, 0)), pl.BlockSpec((tm, B), lambda i: (i, 0)),
                   pl.BlockSpec((1, A), lambda i: (0, 0)), pl.BlockSpec((1, B), lambda i: (0, 0))],
        out_shape=[SDS((T, A), F32), SDS((T, B), F32), SDS((1, A), F32), SDS((1, B), F32)],
        compiler_params=_params(1))(dh1b, w, a, b, ga, gb)


def _gmlp_bwd(proj, da, lg, lb, w_s, w_st, bs_t, A):
    T = proj.shape[0]
    G = A // GROUP_DIM
    tm = _tile(T, 512)
    nc = tm // CHUNK

    def body(u_ref, v_ref, da_ref, lg_ref, lb_ref, w_ref, wt_ref, bst_ref, duv_ref, dlg_ref, dlb_ref, dw_ref, dbs_ref):
        @pl.when(pl.program_id(0) == 0)
        def _():
            dlg_ref[...] = jnp.zeros_like(dlg_ref)
            dlb_ref[...] = jnp.zeros_like(dlb_ref)
            dw_ref[...] = jnp.zeros_like(dw_ref)
            dbs_ref[...] = jnp.zeros_like(dbs_ref)

        row = lax.broadcasted_iota(jnp.int32, (CHUNK, CHUNK), 0)
        col = lax.broadcasted_iota(jnp.int32, (CHUNK, CHUNK), 1)
        lower = row >= col
        upper = row <= col
        for g in range(G):
            sl = slice(g * GROUP_DIM, (g + 1) * GROUP_DIM)
            lgv = lg_ref[:, sl]
            vg, vg_grad = _gelu_and_grad(v_ref[:, sl])
            vhat, rstd, vn = _layer_norm_group(vg, lgv, lb_ref[:, sl])
            vnb = vn.astype(BF16)
            ug, ug_grad = _gelu_and_grad(u_ref[:, sl])
            dav = da_ref[:, sl]
            wm = jnp.where(lower, w_ref[g], 0.0).astype(BF16)
            wmt = jnp.where(upper, wt_ref[g], 0.0).astype(BF16)
            bcol = bst_ref[:, g:g + 1]
            dw_acc = jnp.zeros((CHUNK, CHUNK), F32)
            dbs_acc = jnp.zeros((CHUNK, 1), F32)
            dvn_parts = []
            dug_parts = []
            for c in range(nc):
                rs = slice(c * CHUNK, (c + 1) * CHUNK)
                mixed = _dot(wm, vnb[rs], NN) + bcol
                dug_parts.append(dav[rs] * mixed)
                dmix = dav[rs] * ug[rs]
                dbs_acc = dbs_acc + jnp.sum(dmix, axis=-1, keepdims=True)
                dmixb = dmix.astype(BF16)
                dw_acc = dw_acc + _dot(dmixb, vnb[rs], NT)
                dvn_parts.append(_dot(wmt, dmixb, NN))
            dvn = jnp.concatenate(dvn_parts, axis=0)
            dug = jnp.concatenate(dug_parts, axis=0)
            dw_ref[g] += jnp.where(lower, dw_acc, 0.0)
            dbs_ref[:, g:g + 1] += dbs_acc
            dlg_ref[:, sl] += jnp.sum(dvn * vhat, axis=0, keepdims=True)
            dlb_ref[:, sl] += jnp.sum(dvn, axis=0, keepdims=True)
            dvhat = dvn * lgv
            dvg = rstd * (dvhat - jnp.mean(dvhat, axis=-1, keepdims=True)
                          - vhat * jnp.mean(dvhat * vhat, axis=-1, keepdims=True))
            duv_ref[:, sl] = (dug * ug_grad).astype(BF16)
            duv_ref[:, A + g * GROUP_DIM:A + (g + 1) * GROUP_DIM] = (dvg * vg_grad).astype(BF16)

    return _CHAIN.call(
        body, name="gmlp_bwd", grid=(T // tm,),
        in_specs=[pl.BlockSpec((tm, A), lambda i: (i, 0)), pl.BlockSpec((tm, A), lambda i: (i, 1)),
                  pl.BlockSpec((tm, A), lambda i: (i, 0)),
                  pl.BlockSpec((1, A), lambda i: (0, 0)), pl.BlockSpec((1, A), lambda i: (0, 0)),
                  pl.BlockSpec((G, CHUNK, CHUNK), lambda i: (0, 0, 0)),
                  pl.BlockSpec((G, CHUNK, CHUNK), lambda i: (0, 0, 0)), pl.BlockSpec((CHUNK, G), lambda i: (0, 0))],
        out_specs=[pl.BlockSpec((tm, 2 * A), lambda i: (i, 0)),
                   pl.BlockSpec((1, A), lambda i: (0, 0)), pl.BlockSpec((1, A), lambda i: (0, 0)),
                   pl.BlockSpec((G, CHUNK, CHUNK), lambda i: (0, 0, 0)), pl.BlockSpec((CHUNK, G), lambda i: (0, 0))],
        out_shape=[SDS((T, 2 * A), BF16), SDS((1, A), F32), SDS((1, A), F32),
                   SDS((G, CHUNK, CHUNK), F32), SDS((CHUNK, G), F32)],
        compiler_params=_params(1))(proj, proj, da, lg, lb, w_s, w_st, bs_t)


def _attn_bwd(proj, do, duv, bias_t, sinks, A, B):
    T, P = proj.shape
    H = B // HEAD_DIM
    qpk = H // KV_HEADS
    tq = _tile(T, 512)
    nb = tq // CHUNK
    n_tiles = T // tq
    scale = HEAD_DIM ** -0.5
    rev = lambda i: n_tiles - 1 - i

    def body(sink_ref, q_ref, k_ref, v_ref, kp_ref, vp_ref, do_ref, duv_ref, bias_ref,
             dproj_ref, dbias_ref, dsink_ref, carry, dkv, sacc):
        step = pl.program_id(0)

        @pl.when(step == 0)
        def _():
            carry[...] = jnp.zeros_like(carry)
            sacc[...] = jnp.zeros_like(sacc)
            dbias_ref[...] = jnp.zeros_like(dbias_ref)

        jj = lax.broadcasted_iota(jnp.int32, (2 * CHUNK, CHUNK), 0)
        ii = lax.broadcasted_iota(jnp.int32, (2 * CHUNK, CHUNK), 1)
        in_window = (jj > ii) & (jj <= ii + CHUNK)
        first_mask = in_window & jnp.logical_or(step != n_tiles - 1, jj >= CHUNK)
        low_query = lax.broadcasted_iota(jnp.int32, (CHUNK, LANE), 1) < HEAD_DIM
        low_key = lax.broadcasted_iota(jnp.int32, (2 * CHUNK, LANE), 1) < HEAD_DIM

        def split_pair(pair_bf16):
            zero = jnp.zeros_like(pair_bf16)
            return jnp.concatenate([jnp.where(low_query, pair_bf16, zero), jnp.where(low_query, zero, pair_bf16)], axis=0)

        dproj_ref[:, :2 * A] = duv_ref[...]
        dkv[...] = jnp.zeros_like(dkv)
        for b in range(nb):
            rows = slice(b * CHUNK, (b + 1) * CHUNK)
            band = slice(b * CHUNK, (b + 2) * CHUNK)
            if b == 0:
                kprev, vprev, mask = kp_ref[...], vp_ref[...], first_mask
            else:
                prows = slice((b - 1) * CHUNK, b * CHUNK)
                kprev, vprev, mask = k_ref[prows, :], v_ref[prows, :], in_window
            kband = jnp.concatenate([kprev, k_ref[rows, :]], axis=0)
            vband = jnp.concatenate([vprev, v_ref[rows, :]], axis=0)
            k_pads = [_pad_heads(kband, g) for g in range(KV_HEADS)]
            v_pads = [_pad_heads(vband, g) for g in range(KV_HEADS)]
            queries, douts, scores, dprobs = [], [], [], []
            for pair in range(H // 2):
                cols = slice(2 * pair * HEAD_DIM, (2 * pair + 2) * HEAD_DIM)
                qs = (q_ref[rows, cols] * scale).astype(BF16)
                dob = do_ref[rows, cols].astype(BF16)
                queries.append(qs)
                douts.append(dob)
                scores += [_dot(kz, qs, NT) for kz in k_pads[2 * pair // qpk]]
                dprobs += [_dot(vz, dob, NT) for vz in v_pads[2 * pair // qpk]]
            probs, dscores = [], []
            for h in range(H):
                pt, p_sink = _softmax_with_sink(jnp.where(mask, scores[h] + bias_ref[h], NEG), sink_ref[h], 0)
                delta = jnp.sum(pt * dprobs[h], axis=0, keepdims=True)
                dst = pt * (dprobs[h] - delta)
                dbias_ref[h] += dst
                sacc[h:h + 1, :] += -(p_sink * delta)
                probs.append(pt.astype(BF16))
                dscores.append(dst.astype(BF16))
            dq_parts, dk_groups, dv_groups = [], [], []
            for g in range(KV_HEADS):
                k_both = jnp.concatenate(k_pads[g], axis=0)
                dk_acc = jnp.zeros((2 * CHUNK, LANE), F32)
                dv_acc = jnp.zeros((2 * CHUNK, LANE), F32)
                for pair in range(g * qpk // 2, (g + 1) * qpk // 2):
                    pair_heads = slice(2 * pair, 2 * pair + 2)
                    dk_acc = dk_acc + _dot(jnp.concatenate(dscores[pair_heads], axis=1), split_pair(queries[pair]), NN)
                    dv_acc = dv_acc + _dot(jnp.concatenate(probs[pair_heads], axis=1), split_pair(douts[pair]), NN)
                    dq_parts.append(_dot(jnp.concatenate(dscores[pair_heads], axis=0), k_both, TN) * scale)
                dk_groups.append(dk_acc + pltpu.roll(dk_acc, HEAD_DIM, 1))
                dv_groups.append(dv_acc + pltpu.roll(dv_acc, HEAD_DIM, 1))
            dkv[band, :LANE] += jnp.where(low_key, dk_groups[0], dk_groups[1])
            dkv[band, LANE:] += jnp.where(low_key, dv_groups[0], dv_groups[1])
            dproj_ref[rows, 2 * A:2 * A + B] = jnp.concatenate(dq_parts, axis=1).astype(BF16)
        last = slice(tq, tq + CHUNK)
        dkv[last, :] += carry[...]
        dproj_ref[:, 2 * A + B:] = dkv[CHUNK:, :].astype(BF16)
        carry[...] = dkv[:CHUNK, :]

        @pl.when(step == n_tiles - 1)
        def _():
            dsink_ref[...] = jnp.sum(sacc[...], axis=1, keepdims=True)

    specs = _attn_specs(tq, A, B, reverse_tiles=n_tiles)
    return _CHAIN.call(
        body, name="attn_bwd", grid=(n_tiles,),
        in_specs=[pl.BlockSpec(memory_space=pltpu.SMEM)] + specs
        + [pl.BlockSpec((tq, B), lambda i: (rev(i), 0)), pl.BlockSpec((tq, 2 * A), lambda i: (rev(i), 0)),
           pl.BlockSpec((H, 2 * CHUNK, CHUNK), lambda i: (0, 0, 0))],
        out_specs=[pl.BlockSpec((tq, P), lambda i: (rev(i), 0)),
                   pl.BlockSpec((H, 2 * CHUNK, CHUNK), lambda i: (0, 0, 0)), pl.BlockSpec((H, 1), lambda i: (0, 0))],
        out_shape=[SDS((T, P), BF16), SDS((H, 2 * CHUNK, CHUNK), F32), SDS((H, 1), F32)],
        scratch_shapes=[pltpu.VMEM((CHUNK, 2 * LANE), F32), pltpu.VMEM((tq + CHUNK, 2 * LANE), F32),
                        pltpu.VMEM((H, LANE), F32)],
        compiler_params=_params(1))(sinks, proj, proj, proj, proj, proj, do, duv, bias_t)


def _bias_bwd(dbias, onehot):
    H = dbias.shape[0]
    nbk = onehot.shape[1]

    def body(d_ref, oh_ref, o_ref):
        hi, mid, lo = _split3(d_ref[...])
        oh = oh_ref[...]
        o_ref[...] = _dot(hi, oh, NN) + _dot(mid, oh, NN) + _dot(lo, oh, NN)

    return _CHAIN.call(body, name="bias_bwd", in_specs=[VMEM_SPEC] * 2, out_specs=VMEM_SPEC, out_shape=SDS((H, nbk), F32),
                       compiler_params=_params(0))(dbias, onehot)


def _inproj_bwd(dproj, w_t, x, dh1, g):
    T, P = dproj.shape
    D = x.shape[1]
    tm = _tile(T, 512)

    def body(dp_ref, w_ref, x_ref, dh_ref, g_ref, dx_ref, dg_ref):
        @pl.when(pl.program_id(0) == 0)
        def _():
            dg_ref[...] = jnp.zeros_like(dg_ref)

        dn = _dot(dp_ref[...], w_ref[...], NN)
        xv = x_ref[...]
        r = _rms_stats(xv)
        dg_ref[...] += jnp.sum(dn * (xv * r), axis=0, keepdims=True)
        dx_ref[...] = dh_ref[...] + _rms_bwd(dn, xv, r, g_ref[...])

    return _CHAIN.call(
        body, name="inproj_bwd", grid=(T // tm,),
        in_specs=[pl.BlockSpec((tm, P), lambda i: (i, 0)), _resident((P, D)),
                  pl.BlockSpec((tm, D), lambda i: (i, 0)), pl.BlockSpec((tm, D), lambda i: (i, 0)),
                  pl.BlockSpec((1, D), lambda i: (0, 0))],
        out_specs=[pl.BlockSpec((tm, D), lambda i: (i, 0)), pl.BlockSpec((1, D), lambda i: (0, 0))],
        out_shape=[SDS((T, D), F32), SDS((1, D), F32)], compiler_params=_params(1))(dproj, w_t, x, dh1, g)


def _adamw(w, g, m, v):
    m = ADAM_B1 * m + (1.0 - ADAM_B1) * g
    v = ADAM_B2 * v + (1.0 - ADAM_B2) * (g * g)
    m_hat = m / (1.0 - ADAM_B1 ** ADAM_STEP)
    v_hat = v / (1.0 - ADAM_B2 ** ADAM_STEP)
    delta = -ADAM_LR * (m_hat / (jnp.sqrt(v_hat) + ADAM_EPS) + ADAM_WD * w)
    return delta, m, v


def _adam_sharded(csum, recv, w, m, v, name):
    R, C = w.shape
    tr = _tile(R, 256, 16)

    def body(own_ref, recv_ref, w_ref, m_ref, v_ref, g_ref, d_ref, nm_ref, nv_ref):
        g = own_ref[...].astype(F32)
        for r in range(3):
            g = g + recv_ref[r].astype(F32)
        delta, nm, nv = _adamw(w_ref[...], g, m_ref[...], v_ref[...])
        g_ref[...] = g
        d_ref[...] = delta
        nm_ref[...] = nm
        nv_ref[...] = nv

    blk = pl.BlockSpec((tr, C), lambda i: (i, 0))
    return _CHAIN.call(
        body, name=name, grid=(R // tr,),
        in_specs=[pl.BlockSpec((None, tr, C), lambda i: (0, i, 0)), pl.BlockSpec((3, tr, C), lambda i: (0, i, 0)),
                  blk, blk, blk],
        out_specs=[blk] * 4, out_shape=[SDS((R, C), F32)] * 4, compiler_params=_params(1))(csum, recv, w, m, v)


def _rows2d(shape):
    return (int(np.prod(shape[:-1])) if len(shape) > 1 else 1, shape[-1])


def _small_layout(shapes):
    totals, places = {}, []
    for s in shapes:
        r, w = _rows2d(s)
        off = totals.get(w, 0)
        places.append((w, off, r))
        totals[w] = off + -(-r // 8) * 8
    return {w: -(-t // 32) * 32 for w, t in totals.items()}, places


def _pack_small(arrays, totals, places):
    bufs = []
    for w, total in totals.items():
        buf = jnp.zeros((total, w), F32)
        for a, (pw, off, r) in zip(arrays, places):
            if pw == w:
                buf = lax.dynamic_update_slice(buf, a.reshape(r, w).astype(F32), (off, 0))
        bufs.append(buf)
    return bufs


def _adam_small(gathered, totals, places, ws, ms, vs):
    widths = list(totals)
    n, nw = len(places), len(widths)

    def body(*refs):
        gath, params, outs = refs[:nw], refs[nw:nw + 3 * n], refs[nw + 3 * n:]
        for p, (w, off, r) in enumerate(places):
            g_ref = gath[widths.index(w)]
            g = g_ref[0, off:off + r, :]
            for d in range(1, N_DEV):
                g = g + g_ref[d, off:off + r, :]
            delta, nm, nv = _adamw(params[p][...], g, params[n + p][...], params[2 * n + p][...])
            for k, val in enumerate((g, delta, nm, nv)):
                outs[4 * p + k][...] = val

    shapes2d = [SDS((r, w), F32) for w, _, r in places for _ in range(4)]
    outs = _CHAIN.call(body, name="adam_small", in_specs=[VMEM_SPEC] * (nw + 3 * n), out_specs=[VMEM_SPEC] * (4 * n),
                       out_shape=shapes2d, compiler_params=_params(0))(*gathered, *ws, *ms, *vs)
    return [outs[4 * p:4 * p + 4] for p in range(n)]


def kernel(x, rel_bias_table, mix_norm_g, w_in, gate_norm_g, gate_norm_b, w_spatial, b_spatial, attn_sinks, out_norm_a_g, out_norm_b_g, w_out, ffn_norm_g, w_up, w_down, final_norm_g, loss_target, m_rel_bias_table, m_mix_norm_g, m_w_in, m_gate_norm_g, m_gate_norm_b, m_w_spatial, m_b_spatial, m_attn_sinks, m_out_norm_a_g, m_out_norm_b_g, m_w_out, m_ffn_norm_g, m_w_up, m_w_down, m_final_norm_g, v_rel_bias_table, v_mix_norm_g, v_w_in, v_gate_norm_g, v_gate_norm_b, v_w_spatial, v_b_spatial, v_attn_sinks, v_out_norm_a_g, v_out_norm_b_g, v_w_out, v_ffn_norm_g, v_w_up, v_w_down, v_final_norm_g):
    T, D = x.shape[1], x.shape[2]
    A = D // 2
    B = D // 2
    H = B // HEAD_DIM
    P = 2 * A + B + 2 * KV_HEADS * HEAD_DIM
    xs = x.reshape(T, D)
    target = loss_target.reshape(T, D)

    win_t, m_win_t, v_win_t = (jnp.swapaxes(a[0], 0, 1) for a in (w_in, m_w_in, v_w_in))
    shards = [win_t.astype(BF16), w_out[0].astype(BF16), w_up[0].astype(BF16), w_down[0].astype(BF16)]
    _CHAIN.token = None
    gather = _gather_begin(shards)
    _gather_step(gather, [(0, 0)], "gather_start")

    g1, g2, g3 = mix_norm_g.reshape(1, D), ffn_norm_g.reshape(1, D), final_norm_g.reshape(1, D)
    lg, lb = gate_norm_g.reshape(1, A), gate_norm_b.reshape(1, A)
    ws = w_spatial[0]
    ws_t = jnp.swapaxes(ws, 1, 2)
    bs_t = jnp.transpose(b_spatial[0])
    ga, gb = out_norm_a_g.reshape(1, A), out_norm_b_g.reshape(1, B)
    sinks = attn_sinks.reshape(H)
    bucket, in_window = _t5_bucket()
    onehot_np = ((bucket[:, :, None] == np.arange(N_BUCKETS)) & in_window[:, :, None]).astype(np.float32)
    onehot = jnp.asarray(onehot_np.reshape(-1, N_BUCKETS)).astype(BF16)
    onehot_kq = jnp.asarray(onehot_np.transpose(1, 0, 2).reshape(-1, N_BUCKETS)).astype(BF16)

    bias, bias_t = _bias_fwd(jnp.transpose(rel_bias_table), jnp.transpose(onehot), jnp.transpose(onehot_kq))
    bias, bias_t = bias.reshape(H, CHUNK, 2 * CHUNK), bias_t.reshape(H, 2 * CHUNK, CHUNK)
    n1 = _mix_norm(xs, g1)
    _gather_step(gather, [(0, 1), (1, 0), (2, 0)], "gather_in_1")
    _gather_step(gather, [(0, 2)], "gather_in_2")
    (win_g,) = _gather_end(gather, [0], "gather_in_end")
    win_t_full = win_g.reshape(P, D)
    proj = _inproj_fwd(n1, win_t_full)
    _gather_step(gather, [(1, 1)], "gather_out_1")
    a_out = _gmlp_fwd(proj, lg, lb, ws, bs_t, A)
    _gather_step(gather, [(1, 2), (2, 1), (3, 0)], "gather_out_2_up_1")
    b_out = _attn_fwd(proj, bias, sinks, A, B)
    (wout_g,) = _gather_end(gather, [1], "gather_out_end")
    _gather_step(gather, [(2, 2)], "gather_up_2")
    wout_full = wout_g.reshape(A + B, D)
    h1, mixed, n2 = _outproj_fwd(a_out, b_out, ga, gb, xs, wout_full, g2)
    (wup_g,) = _gather_end(gather, [2], "gather_up_end")
    wup_t = jnp.transpose(wup_g, (0, 2, 1)).reshape(-1, D)
    z = _ffn_up(n2, wup_g)
    _gather_step(gather, [(3, 1)], "gather_down_1")
    z = _ffn_up(n2, wup_g, z)
    _gather_step(gather, [(3, 2)], "gather_down_2")
    (wdown_g,) = _gather_end(gather, [3], "gather_down_end")
    h2 = _ffn_down(h1, z, wdown_g.reshape(-1, D))
    loss_part, dg3, dh2, dh2b = _final_loss(h2, g3, target)

    def reduce_to_chip(state, name):
        csums = [_chip_sum(part, received, "%s_chip_sum_%d" % (name, a))
                 for a, (part, received) in enumerate(_sibling_exchange_end(state, name + "_sib_end"))]
        return _chip_exchange_begin(csums, name + "_chip")

    dwdown = _matmul_tn(z, dh2b, "grad_w_down", square_a=True).reshape(wdown_g.shape)
    dzp = _ffn_down_bwd(dh2b, z, wdown_g.reshape(-1, D))
    dwup = _matmul_tn(n2, dzp, "grad_w_up", col_blocks=N_DEV)
    sib_ffn = _sibling_exchange_begin([dwdown, dwup], "rs_ffn_sib")
    dh1, dh1b, dg2 = _ffn_norm_bwd(_ffn_up_bwd(dzp, wup_t), dh2, h1, g2)
    chip_ffn = reduce_to_chip(sib_ffn, "rs_ffn")
    da, db, dga, dgb = _outproj_bwd(dh1b, wout_full, a_out, b_out, ga, gb)
    dwout = _matmul_tn(mixed, dh1b, "grad_w_out").reshape(wout_g.shape)
    sib_out = _sibling_exchange_begin([dwout], "rs_out_sib")
    duv, dlg, dlb, dws, dbs_t = _gmlp_bwd(proj, da, lg, lb, ws, ws_t, bs_t, A)
    dproj, dbias_t, dsinks = _attn_bwd(proj, db, duv, bias_t, sinks, A, B)
    chip_out = reduce_to_chip(sib_out, "rs_out")
    dwin_t = _matmul_tn(dproj, n1, "grad_w_in").reshape(win_g.shape)
    sib_in = _sibling_exchange_begin([dwin_t], "rs_in_sib")
    dtable_t = _bias_bwd(dbias_t.reshape(H, -1), onehot_kq)
    chip_in = reduce_to_chip(sib_in, "rs_in")
    grad_x, dg1 = _inproj_bwd(dproj, win_t_full, xs, dh1, g1)

    small_w = [rel_bias_table, mix_norm_g, gate_norm_g, gate_norm_b, w_spatial, b_spatial, attn_sinks,
               out_norm_a_g, out_norm_b_g, ffn_norm_g, final_norm_g]
    small_m = [m_rel_bias_table, m_mix_norm_g, m_gate_norm_g, m_gate_norm_b, m_w_spatial, m_b_spatial, m_attn_sinks,
               m_out_norm_a_g, m_out_norm_b_g, m_ffn_norm_g, m_final_norm_g]
    small_v = [v_rel_bias_table, v_mix_norm_g, v_gate_norm_g, v_gate_norm_b, v_w_spatial, v_b_spatial, v_attn_sinks,
               v_out_norm_a_g, v_out_norm_b_g, v_ffn_norm_g, v_final_norm_g]
    small_g = [jnp.transpose(dtable_t), dg1, dlg, dlb, dws, jnp.transpose(dbs_t), dsinks, dga, dgb, dg2, dg3]
    nothing = jnp.zeros((1, H), F32)
    small_w, small_m, small_v = small_w + [nothing], small_m + [nothing], small_v + [nothing]
    small_g = small_g + [jnp.broadcast_to(loss_part, (1, H))]
    shapes = [w.shape for w in small_w]
    totals, places = _small_layout(shapes)
    as_rows = lambda arrays: [a.reshape(_rows2d(a.shape)) for a in arrays]
    big = [None] * 4

    def adam_of(k, state, a, w, m, v):
        csum, received = _chip_exchange_end(state, a, "rs_%d_end" % k)
        big[k] = _adam_sharded(csum, received, w, m, v, "adam_%d" % k)

    small_gather = _gather_begin(_pack_small(small_g, totals, places))
    every = range(len(totals))
    _gather_step(small_gather, [(a, 0) for a in every], "small_gather_start")
    adam_of(3, chip_ffn, 0, w_down[0], m_w_down[0], v_w_down[0])
    _gather_step(small_gather, [(a, 1) for a in every], "small_gather_1")
    adam_of(2, chip_ffn, 1, w_up[0], m_w_up[0], v_w_up[0])
    _gather_step(small_gather, [(a, 2) for a in every], "small_gather_2")
    adam_of(1, chip_out, 0, w_out[0], m_w_out[0], v_w_out[0])
    adam_of(0, chip_in, 0, win_t, m_win_t, v_win_t)
    gathered = _gather_end(small_gather, list(every), "small_gather_end")
    small_out = _adam_small(gathered, totals, places, as_rows(small_w), as_rows(small_m), as_rows(small_v))
    sg, sd, sm, sv = [[outs[k].reshape(s) for outs, s in zip(small_out, shapes)] for k in range(4)]
    big[0] = [jnp.swapaxes(o, 0, 1) for o in big[0]]
    big = [[o.reshape(w.shape) for o in outs] for outs, w in zip(big, (w_in, w_out, w_up, w_down))]

    loss = sg[-1][0, 0]

    order = ["s0", "s1", "b0", "s2", "s3", "s4", "s5", "s6", "s7", "s8", "b1", "s9", "b2", "b3", "s10"]

    def group(idx):
        small = (sg, sd, sm, sv)[idx]
        return [small[int(t[1:])] if t[0] == "s" else big[int(t[1:])][idx] for t in order]

    return (loss, grad_x.reshape(x.shape), *group(0), *group(1), *group(2), *group(3))
```

```python
import math

import numpy as np
import jax
import jax.numpy as jnp
from jax import lax
from jax.experimental import pallas as pl
from jax.experimental.pallas import tpu as pltpu

F32 = jnp.float32
BF16 = jnp.bfloat16
SDS = jax.ShapeDtypeStruct
MESH = pl.DeviceIdType.MESH

N_DEV = 8
EPS = 1e-5
NEG = -1e30
CHUNK = 128
GROUP_DIM = 128
HEAD_DIM = 64
KV_HEADS = 2
N_BUCKETS = 32
MAX_DISTANCE = 128
ADAM_LR, ADAM_B1, ADAM_B2, ADAM_EPS, ADAM_WD, ADAM_STEP = 0.001, 0.9, 0.999, 1e-08, 0.01, 10
GELU_C0 = math.sqrt(2.0 / math.pi)
GELU_C1 = 0.044715

V7X_VMEM_BYTES = 64 * 1024 * 1024
VMEM_LIMIT = V7X_VMEM_BYTES - 8 * 1024 * 1024
LANE = 128

NN = ((1,), (0,))
NT = ((1,), (1,))
TN = ((0,), (0,))


def _dot(a, b, dims):
    return lax.dot_general(a, b, (dims, ((), ())), preferred_element_type=F32)


def _tile(n, pref, unit=LANE):
    best = None
    for t in range(unit, min(n, pref) + 1, unit):
        if n % t == 0:
            best = t
    return n if best is None else best


def _params(n_grid):
    return pltpu.CompilerParams(dimension_semantics=("arbitrary",) * n_grid, vmem_limit_bytes=VMEM_LIMIT)


def _resident(shape):
    return pl.BlockSpec(shape, lambda i: (0, 0), pipeline_mode=pl.Buffered(1))


def _gelu(x):
    return 0.5 * x * (1.0 + jnp.tanh(GELU_C0 * (x + GELU_C1 * x * x * x)))


def _gelu_and_grad(x):
    x2 = x * x
    t = jnp.tanh(GELU_C0 * x * (1.0 + GELU_C1 * x2))
    val = 0.5 * x * (1.0 + t)
    grad = 0.5 * (1.0 + t) + 0.5 * x * (1.0 - t * t) * (GELU_C0 * (1.0 + 3.0 * GELU_C1 * x2))
    return val, grad


def _rms_stats(x):
    return lax.rsqrt(jnp.mean(x * x, axis=-1, keepdims=True) + EPS)


def _rms_bwd(dy, x, r, g):
    w = dy * g
    return r * w - x * (r * r * r) * jnp.mean(w * x, axis=-1, keepdims=True)


def _t5_bucket():
    i = np.arange(CHUNK)[:, None]
    j = np.arange(2 * CHUNK)[None, :]
    rel = np.maximum(i + CHUNK - j, 0)
    n_exact = N_BUCKETS // 2
    relf = np.maximum(rel, n_exact).astype(np.float32)
    large = n_exact + (np.log(relf / np.float32(n_exact)) / np.float32(math.log(MAX_DISTANCE / n_exact))
                       * np.float32(N_BUCKETS - n_exact)).astype(np.int32)
    large = np.minimum(large, N_BUCKETS - 1)
    bucket = np.where(rel < n_exact, rel, large)
    in_window = (i + CHUNK - j >= 0) & (i + CHUNK - j < CHUNK)
    return bucket.astype(np.int32), in_window


def _split3(x):
    hi = x.astype(BF16)
    r1 = x - hi.astype(F32)
    mid = r1.astype(BF16)
    lo = (r1 - mid.astype(F32)).astype(BF16)
    return hi, mid, lo


HBM_SPEC = pl.BlockSpec(memory_space=pltpu.HBM)


def _mesh_pos():
    return lax.axis_index("x"), lax.axis_index("y"), lax.axis_index("c")


def _dev_index(px, py, pc):
    return 4 * px + 2 * py + pc


SEM_SPEC = pl.BlockSpec(memory_space=pltpu.SEMAPHORE)
ANY_SPEC = pl.BlockSpec(memory_space=pl.ANY)
VMEM_SPEC = pl.BlockSpec(memory_space=pltpu.VMEM)
TOKEN = SDS((8, LANE), F32)
SIDE_EFFECT = pltpu.SideEffectType.DATAFLOW_SIDE_EFFECTING


def _hbm(x):
    return pltpu.with_memory_space_constraint(x, pltpu.HBM)


class _CallChain:
    def __init__(self):
        self.token = None

    def call(self, body, *, in_specs, out_specs, out_shape, **kwargs):
        dep, n_in = self.token, len(in_specs)
        single = not isinstance(out_shape, (list, tuple))
        out_shapes = [out_shape] if single else list(out_shape)
        out_specs = [out_specs] if single else list(out_specs)
        n_out = len(out_shapes)
        n_dep = 0 if dep is None else 1
        token_spec = pl.BlockSpec((8, LANE), lambda *_: (0, 0)) if kwargs.get("grid") else VMEM_SPEC

        def chained(*refs):
            outs_at = n_in + n_dep
            body(*refs[:n_in], *refs[outs_at:outs_at + n_out], *refs[outs_at + n_out + 1:])
            token = refs[outs_at + n_out]
            token[...] = jnp.zeros_like(token)

        inner = pl.pallas_call(chained, in_specs=list(in_specs) + [ANY_SPEC] * n_dep, out_specs=out_specs + [token_spec],
                               out_shape=out_shapes + [TOKEN], **kwargs)

        def run(*operands):
            outs = inner(*operands) if dep is None else inner(*operands, dep)
            self.token = outs[n_out]
            return outs[0] if single else list(outs[:n_out])

        return run


_CHAIN = _CallChain()


def _wait_all(waits, x, y, c):
    for kind, src, dst, send_sem, recv_sem in waits:
        cp = pltpu.make_async_remote_copy(src_ref=src, dst_ref=dst, send_sem=send_sem, recv_sem=recv_sem,
                                          device_id=(x, y, c), device_id_type=MESH)
        if kind == "send":
            cp.wait_send()
        else:
            cp.wait_recv()


def _split_start(bufs, copies_of, n_sems, name, sem_sets=(), waits_of=None):
    n, ns = len(bufs), len(sem_sets)
    flat_sems = [s for pair in sem_sets for s in pair]

    def body(*refs):
        ins = refs[:n]
        sems = refs[n:n + 2 * ns]
        send_sems, recv_sems = refs[n + 2 * ns], refs[n + 2 * ns + 1]
        if waits_of is not None:
            _wait_all(waits_of(ins, [(sems[2 * i], sems[2 * i + 1]) for i in range(ns)]), *_mesh_pos())
        for src, dst, k, target in copies_of(ins):
            pltpu.make_async_remote_copy(src_ref=src, dst_ref=dst, send_sem=send_sems.at[k], recv_sem=recv_sems.at[k],
                                         device_id=target, device_id_type=MESH).start()

    outs = _CHAIN.call(
        body, name=name,
        out_shape=[pltpu.SemaphoreType.DMA((n_sems,)), pltpu.SemaphoreType.DMA((n_sems,))]
        + [pltpu.HBM(b.shape, b.dtype) for b in bufs],
        in_specs=[HBM_SPEC] * n + [SEM_SPEC] * (2 * ns), out_specs=[SEM_SPEC, SEM_SPEC] + [HBM_SPEC] * n,
        input_output_aliases={a: 2 + a for a in range(n)},
        compiler_params=pltpu.CompilerParams(has_side_effects=SIDE_EFFECT),
    )(*[_hbm(b) for b in bufs], *flat_sems)
    return outs[0], outs[1], list(outs[2:2 + n])


def _split_wait(bufs, sem_sets, waits_of, name):
    n, ns = len(bufs), len(sem_sets)
    flat_sems = [s for pair in sem_sets for s in pair]

    def body(*refs):
        ins = refs[:n]
        sems = refs[n:n + 2 * ns]
        _wait_all(waits_of(ins, [(sems[2 * i], sems[2 * i + 1]) for i in range(ns)]), *_mesh_pos())

    outs = _CHAIN.call(
        body, name=name,
        out_shape=[pltpu.HBM(b.shape, b.dtype) for b in bufs],
        in_specs=[HBM_SPEC] * n + [SEM_SPEC] * (2 * ns), out_specs=[HBM_SPEC] * n,
        input_output_aliases={a: a for a in range(n)},
        compiler_params=pltpu.CompilerParams(has_side_effects=SIDE_EFFECT),
    )(*bufs, *flat_sems)
    return list(outs)


def _gather_blocks(land):
    rows = land.shape[1]
    first = (rows // 2) // 16 * 16

    def block(px, py, pc):
        return land.at[_dev_index(px, py, pc)]

    def halves(px, py, pc):
        return (land.at[_dev_index(px, py, pc), pl.ds(0, first)], land.at[_dev_index(px, py, pc), pl.ds(first, rows - first)])

    return block, halves


def _gather_begin(shards):
    me = _dev_index(*_mesh_pos())
    lands = [lax.dynamic_update_index_in_dim(lax.empty((N_DEV,) + s.shape, s.dtype), s, me, 0) for s in shards]
    return dict(lands=lands, stage={})


STAGE_COPIES = (3, 4, 1)


def _gather_step(state, items, name):
    which = sorted({a for a, _ in items})
    at = {a: i for i, a in enumerate(which)}
    sem_sets = [state["stage"][(a, s - 1)][0] for a, s in items if s > 0]
    offset, n_sems = {}, 0
    for a, s in items:
        offset[(a, s)] = n_sems
        n_sems += STAGE_COPIES[s]

    def waits_of(ins, sems):
        x, y, c = _mesh_pos()
        out, earlier = [], 0
        for a, s in items:
            if s == 0:
                continue
            block, halves = _gather_blocks(ins[at[a]])
            send, recv = sems[earlier]
            off = state["stage"][(a, s - 1)][1]
            earlier += 1
            if s == 1:
                arrived = [(1, block(1 - x, y, c)), (2, block(x, 1 - y, c))]
            else:
                arrived = list(zip((2, 3), halves(1 - x, 1 - y, c)))
            out += [("recv", ref, ref, send.at[off + k], recv.at[off + k]) for k, ref in arrived]
        return out

    def copies_of(ins):
        x, y, c = _mesh_pos()
        sibling = (x, y, 1 - c)
        out = []
        for a, s in items:
            block, halves = _gather_blocks(ins[at[a]])
            off = offset[(a, s)]
            if s == 0:
                mine = block(x, y, c)
                out += [(mine, mine, off + 1, (1 - x, y, c)), (mine, mine, off + 2, (x, 1 - y, c)), (mine, mine, off, sibling)]
            elif s == 1:
                from_x, from_y = block(1 - x, y, c), block(x, 1 - y, c)
                out += [(halves(1 - x, y, c)[0], halves(1 - x, y, c)[0], off + 2, (x, 1 - y, c)),
                        (halves(x, 1 - y, c)[1], halves(x, 1 - y, c)[1], off + 3, (1 - x, y, c)),
                        (from_x, from_x, off, sibling), (from_y, from_y, off + 1, sibling)]
            else:
                diag = block(1 - x, 1 - y, c)
                out.append((diag, diag, off, sibling))
        return out

    send_sems, recv_sems, bufs = _split_start([state["lands"][a] for a in which], copies_of, n_sems, name,
                                              sem_sets=sem_sets, waits_of=waits_of)
    for a in which:
        state["lands"][a] = bufs[at[a]]
    for a, s in items:
        state["stage"][(a, s)] = ((send_sems, recv_sems), offset[(a, s)])


def _gather_end(state, which, name):
    sem_sets = [state["stage"][(a, s)][0] for a in which for s in range(3)]

    def waits(ins, sems):
        x, y, c = _mesh_pos()
        out = []
        for i, a in enumerate(which):
            block, halves = _gather_blocks(ins[i])
            (b_send, b_recv), (s1_send, s1_recv), (s2_send, s2_recv) = sems[3 * i:3 * i + 3]
            o0, o1, o2 = (state["stage"][(a, s)][1] for s in range(3))
            arrivals = [(block(x, y, 1 - c), b_send, b_recv, o0),
                        (block(1 - x, y, 1 - c), s1_send, s1_recv, o1), (block(x, 1 - y, 1 - c), s1_send, s1_recv, o1 + 1),
                        (block(1 - x, 1 - y, 1 - c), s2_send, s2_recv, o2)]
            mine = block(x, y, c)
            sent = [(mine, b_send, b_recv, o0 + k) for k in range(3)]
            sent += [(block(1 - x, y, c), s1_send, s1_recv, o1), (block(x, 1 - y, c), s1_send, s1_recv, o1 + 1),
                     (halves(1 - x, y, c)[0], s1_send, s1_recv, o1 + 2), (halves(x, 1 - y, c)[1], s1_send, s1_recv, o1 + 3),
                     (block(1 - x, 1 - y, c), s2_send, s2_recv, o2)]
            out += [("recv", ref, ref, s.at[k], r.at[k]) for ref, s, r, k in arrivals]
            out += [("send", ref, ref, s.at[k], r.at[k]) for ref, s, r, k in sent]
        return out

    bufs = _split_wait([state["lands"][a] for a in which], sem_sets, waits, name)
    for i, a in enumerate(which):
        state["lands"][a] = bufs[i]
    return bufs


def _sibling_exchange_begin(parts, name):
    lands = [lax.empty((4,) + p.shape[1:], p.dtype) for p in parts]
    n = len(parts)

    def copies_of(ins):
        x, y, c = _mesh_pos()
        return [(ins[a].at[2 * j + (1 - c)], ins[n + a].at[j], 4 * a + j, (x, y, 1 - c)) for a in range(n) for j in range(4)]

    send_sems, recv_sems, bufs = _split_start(list(parts) + lands, copies_of, 4 * n, name)
    return dict(bufs=bufs, sems=(send_sems, recv_sems), n=n)


def _sibling_exchange_end(state, name):
    n = state["n"]

    def waits(ins, sems):
        _, _, c = _mesh_pos()
        return [(kind, ins[a].at[2 * j + (1 - c)], ins[n + a].at[j], sems[0][0].at[4 * a + j], sems[0][1].at[4 * a + j])
                for a in range(n) for j in range(4) for kind in ("send", "recv")]

    bufs = _split_wait(state["bufs"], [state["sems"]], waits, name)
    return [(bufs[a], bufs[n + a]) for a in range(n)]


CHIP_FLIPS = (2, 1, 3)


def _chip_exchange_begin(csums, name):
    lands = [lax.empty((3,) + s.shape[1:], s.dtype) for s in csums]
    n = len(csums)

    def copies_of(ins):
        x, y, c = _mesh_pos()
        chips = [(1 - x, y), (x, 1 - y), (1 - x, 1 - y)]
        return [(ins[a].at[CHIP_FLIPS[r]], ins[n + a].at[r], 3 * a + r, (px, py, c))
                for a in range(n) for r, (px, py) in enumerate(chips)]

    send_sems, recv_sems, bufs = _split_start(list(csums) + lands, copies_of, 3 * n, name)
    return dict(bufs=bufs, sems=(send_sems, recv_sems), n=n)


def _chip_exchange_end(state, a, name):
    n = state["n"]

    def waits(ins, sems):
        return [(kind, ins[0].at[CHIP_FLIPS[r]], ins[1].at[r], sems[0][0].at[3 * a + r], sems[0][1].at[3 * a + r])
                for r in range(3) for kind in ("send", "recv")]

    csum, received = _split_wait([state["bufs"][a], state["bufs"][n + a]], [state["sems"]], waits, name)
    return csum, received


def _chip_sum(part, recv, name):
    _, R, C = part.shape
    tr = _tile(R, 1024, 16)
    place = jnp.stack([lax.axis_index("c"), 2 * lax.axis_index("x") + lax.axis_index("y")]).astype(jnp.int32)

    def body(place_ref, p_ref, r_ref, o_ref):
        o_ref[...] = (p_ref[...].astype(F32) + r_ref[...].astype(F32)).astype(o_ref.dtype)

    def chip(p, place_ref):
        return jnp.bitwise_xor(p, place_ref[1])

    grid_spec = pltpu.PrefetchScalarGridSpec(
        num_scalar_prefetch=1, grid=(4, R // tr),
        in_specs=[pl.BlockSpec((None, tr, C), lambda p, i, place_ref: (2 * chip(p, place_ref) + place_ref[0], i, 0)),
                  pl.BlockSpec((None, tr, C), lambda p, i, place_ref: (chip(p, place_ref), i, 0))],
        out_specs=pl.BlockSpec((None, tr, C), lambda p, i, place_ref: (p, i, 0)))
    return pl.pallas_call(body, name=name, grid_spec=grid_spec, out_shape=SDS((4, R, C), part.dtype),
                          compiler_params=_params(2))(place, part, recv)


def _bias_fwd(table_t, onehot_t, onehot_kq_t):
    H = table_t.shape[0]
    n = onehot_t.shape[1]

    def body(t_ref, oh_ref, oh_kq_ref, o_ref, o_kq_ref):
        hi, mid, lo = _split3(t_ref[...])
        for src, dst in ((oh_ref, o_ref), (oh_kq_ref, o_kq_ref)):
            oh = src[...]
            dst[...] = _dot(hi, oh, NN) + _dot(mid, oh, NN) + _dot(lo, oh, NN)

    return _CHAIN.call(body, name="bias_fwd", in_specs=[VMEM_SPEC] * 3, out_specs=[VMEM_SPEC] * 2,
                       out_shape=[SDS((H, n), F32)] * 2, compiler_params=_params(0))(table_t, onehot_t, onehot_kq_t)


def _mix_norm(x, g):
    T, D = x.shape
    tm = _tile(T, 512)

    def body(x_ref, g_ref, n_ref):
        xv = x_ref[...]
        n_ref[...] = (xv * _rms_stats(xv) * g_ref[...]).astype(BF16)

    row = pl.BlockSpec((tm, D), lambda i: (i, 0))
    return _CHAIN.call(body, name="mix_norm", grid=(T // tm,), in_specs=[row, pl.BlockSpec((1, D), lambda i: (0, 0))],
                       out_specs=row, out_shape=SDS((T, D), BF16), compiler_params=_params(1))(x, g)


def _inproj_fwd(n, w_t):
    T, D = n.shape
    P = w_t.shape[0]
    tm = _tile(T, 512)

    def body(n_ref, w_ref, proj_ref):
        proj_ref[...] = _dot(n_ref[...], w_ref[...], NT)

    return _CHAIN.call(
        body, name="inproj_fwd", grid=(T // tm,),
        in_specs=[pl.BlockSpec((tm, D), lambda i: (i, 0)), _resident((P, D))],
        out_specs=pl.BlockSpec((tm, P), lambda i: (i, 0)),
        out_shape=SDS((T, P), F32), compiler_params=_params(1))(n, w_t)


def _layer_norm_group(vg, lg, lb):
    mu = jnp.mean(vg, axis=-1, keepdims=True)
    xc = vg - mu
    rstd = lax.rsqrt(jnp.mean(xc * xc, axis=-1, keepdims=True) + EPS)
    vhat = xc * rstd
    return vhat, rstd, vhat * lg + lb


def _gmlp_fwd(proj, lg, lb, w_s, bs_t, A):
    T = proj.shape[0]
    G = A // GROUP_DIM
    tm = _tile(T, 512)
    nc = tm // CHUNK

    def body(u_ref, v_ref, lg_ref, lb_ref, w_ref, bst_ref, a_ref):
        row = lax.broadcasted_iota(jnp.int32, (CHUNK, CHUNK), 0)
        col = lax.broadcasted_iota(jnp.int32, (CHUNK, CHUNK), 1)
        causal = row >= col
        for g in range(G):
            sl = slice(g * GROUP_DIM, (g + 1) * GROUP_DIM)
            _, _, vn = _layer_norm_group(_gelu(v_ref[:, sl]), lg_ref[:, sl], lb_ref[:, sl])
            vnb = vn.astype(BF16)
            wm = jnp.where(causal, w_ref[g], 0.0).astype(BF16)
            ug = _gelu(u_ref[:, sl])
            bcol = bst_ref[:, g:g + 1]
            for c in range(nc):
                rs = slice(c * CHUNK, (c + 1) * CHUNK)
                a_ref[rs, sl] = ug[rs] * (_dot(wm, vnb[rs], NN) + bcol)

    return _CHAIN.call(
        body, name="gmlp_fwd", grid=(T // tm,),
        in_specs=[pl.BlockSpec((tm, A), lambda i: (i, 0)), pl.BlockSpec((tm, A), lambda i: (i, 1)),
                  pl.BlockSpec((1, A), lambda i: (0, 0)), pl.BlockSpec((1, A), lambda i: (0, 0)),
                  pl.BlockSpec((G, CHUNK, CHUNK), lambda i: (0, 0, 0)), pl.BlockSpec((CHUNK, G), lambda i: (0, 0))],
        out_specs=pl.BlockSpec((tm, A), lambda i: (i, 0)),
        out_shape=SDS((T, A), F32), compiler_params=_params(1))(proj, proj, lg, lb, w_s, bs_t)


def _attn_masks(first_tile):
    ii = lax.broadcasted_iota(jnp.int32, (CHUNK, 2 * CHUNK), 0)
    jj = lax.broadcasted_iota(jnp.int32, (CHUNK, 2 * CHUNK), 1)
    in_window = (jj > ii) & (jj <= ii + CHUNK)
    first_mask = in_window & jnp.logical_or(jnp.logical_not(first_tile), jj >= CHUNK)
    return in_window, first_mask


def _softmax_with_sink(s, sink, axis):
    m = jnp.maximum(jnp.max(s, axis=axis, keepdims=True), sink)
    p = jnp.exp(s - m)
    e_sink = jnp.exp(sink - m)
    inv = 1.0 / (jnp.sum(p, axis=axis, keepdims=True) + e_sink)
    return p * inv, e_sink * inv


def _pad_heads(band, group):
    lane = lax.broadcasted_iota(jnp.int32, band.shape, 1)
    if group == 0:
        low = jnp.where(lane < HEAD_DIM, band, 0.0)
        high = pltpu.roll(low, HEAD_DIM, 1)
    else:
        high = jnp.where(lane >= HEAD_DIM, band, 0.0)
        low = pltpu.roll(high, HEAD_DIM, 1)
    return low.astype(BF16), high.astype(BF16)


def _attn_specs(tq, A, B, reverse_tiles=None):
    nb = tq // CHUNK
    kcol = (2 * A + B) // LANE
    if reverse_tiles is None:
        tile = lambda i: i
    else:
        tile = lambda i: reverse_tiles - 1 - i
    prev = lambda i: jnp.maximum(tile(i) * nb - 1, 0)
    return [pl.BlockSpec((tq, B), lambda i: (tile(i), 2 * A // B)),
            pl.BlockSpec((tq, LANE), lambda i: (tile(i), kcol)),
            pl.BlockSpec((tq, LANE), lambda i: (tile(i), kcol + 1)),
            pl.BlockSpec((CHUNK, LANE), lambda i: (prev(i), kcol)),
            pl.BlockSpec((CHUNK, LANE), lambda i: (prev(i), kcol + 1))]


def _attn_fwd(proj, bias, sinks, A, B):
    T = proj.shape[0]
    H = B // HEAD_DIM
    qpk = H // KV_HEADS
    tq = _tile(T, 512)
    nb = tq // CHUNK

    scale = HEAD_DIM ** -0.5

    def body(sink_ref, q_ref, k_ref, v_ref, kp_ref, vp_ref, bias_ref, o_ref):
        in_window, first_mask = _attn_masks(pl.program_id(0) == 0)
        for b in range(nb):
            rows = slice(b * CHUNK, (b + 1) * CHUNK)
            if b == 0:
                kprev, vprev, mask = kp_ref[...], vp_ref[...], first_mask
            else:
                prows = slice((b - 1) * CHUNK, b * CHUNK)
                kprev, vprev, mask = k_ref[prows, :], v_ref[prows, :], in_window
            kband = jnp.concatenate([kprev, k_ref[rows, :]], axis=0)
            vband = jnp.concatenate([vprev, v_ref[rows, :]], axis=0)
            k_pads = [_pad_heads(kband, g) for g in range(KV_HEADS)]
            v_both = [jnp.concatenate(_pad_heads(vband, g), axis=0) for g in range(KV_HEADS)]
            scores = []
            for pair in range(H // 2):
                h = 2 * pair
                qs = (q_ref[rows, h * HEAD_DIM:(h + 2) * HEAD_DIM] * scale).astype(BF16)
                scores += [_dot(qs, kz, NT) for kz in k_pads[h // qpk]]
            probs = [_softmax_with_sink(jnp.where(mask, s + bias_ref[h], NEG), sink_ref[h], -1)[0].astype(BF16)
                     for h, s in enumerate(scores)]
            outs = [_dot(jnp.concatenate(probs[h:h + 2], axis=1), v_both[h // qpk], NN) for h in range(0, H, 2)]
            o_ref[rows, :] = jnp.concatenate(outs, axis=1)

    return _CHAIN.call(
        body, name="attn_fwd", grid=(T // tq,),
        in_specs=[pl.BlockSpec(memory_space=pltpu.SMEM)] + _attn_specs(tq, A, B)
        + [pl.BlockSpec((H, CHUNK, 2 * CHUNK), lambda i: (0, 0, 0))],
        out_specs=pl.BlockSpec((tq, B), lambda i: (i, 0)),
        out_shape=SDS((T, B), F32), compiler_params=_params(1))(sinks, proj, proj, proj, proj, proj, bias)


def _outproj_fwd(a, b, ga, gb, x, w, g_ffn):
    T, A = a.shape
    B = b.shape[1]
    D = x.shape[1]
    tm = _tile(T, 512)

    def body(a_ref, b_ref, ga_ref, gb_ref, x_ref, w_ref, gf_ref, h_ref, mix_ref, n_ref):
        av, bv = a_ref[...], b_ref[...]
        mix_ref[:, :A] = (av * _rms_stats(av) * ga_ref[...]).astype(BF16)
        mix_ref[:, A:] = (bv * _rms_stats(bv) * gb_ref[...]).astype(BF16)
        hv = x_ref[...] + _dot(mix_ref[...], w_ref[...], NN)
        h_ref[...] = hv
        n_ref[...] = (hv * _rms_stats(hv) * gf_ref[...]).astype(BF16)

    row = pl.BlockSpec((tm, D), lambda i: (i, 0))
    return _CHAIN.call(
        body, name="outproj_fwd", grid=(T // tm,),
        in_specs=[pl.BlockSpec((tm, A), lambda i: (i, 0)), pl.BlockSpec((tm, B), lambda i: (i, 0)),
                  pl.BlockSpec((1, A), lambda i: (0, 0)), pl.BlockSpec((1, B), lambda i: (0, 0)),
                  row, _resident((A + B, D)), pl.BlockSpec((1, D), lambda i: (0, 0))],
        out_specs=[row, pl.BlockSpec((tm, A + B), lambda i: (i, 0)), row],
        out_shape=[SDS((T, D), F32), SDS((T, A + B), BF16), SDS((T, D), BF16)],
        compiler_params=_params(1))(a, b, ga, gb, x, w, g_ffn)


def _ffn_up(n, w_up):
    T, D = n.shape
    Fb = w_up.shape[2]
    F = N_DEV * Fb
    tm, tf = _tile(T, 1024), _tile(Fb, 1024)
    per = Fb // tf

    def body(n_ref, wu_ref, z_ref):
        z_ref[...] = jnp.maximum(_dot(n_ref[...], wu_ref[...], NN), 0.0).astype(BF16)

    return _CHAIN.call(
        body, name="ffn_up", grid=(T // tm, F // tf),
        in_specs=[pl.BlockSpec((tm, D), lambda i, j: (i, 0)),
                  pl.BlockSpec((None, D, tf), lambda i, j: (j // per, 0, j % per))],
        out_specs=pl.BlockSpec((tm, tf), lambda i, j: (i, j)),
        out_shape=SDS((T, F), BF16), compiler_params=_params(2))(n, w_up)


def _ffn_down(h1, z, w_down):
    T, D = h1.shape
    F = w_down.shape[0]
    tm, tn, tk = _tile(T, 1024), _tile(D, 1024), _tile(F, 4096)

    def body(h_ref, z_ref, wd_ref, h2_ref):
        k = pl.program_id(2)

        @pl.when(k == 0)
        def _():
            h2_ref[...] = h_ref[...]

        zf = z_ref[...].astype(F32)
        h2_ref[...] += _dot((zf * zf).astype(BF16), wd_ref[...], NN)

    return _CHAIN.call(
        body, name="ffn_down", grid=(T // tm, D // tn, F // tk),
        in_specs=[pl.BlockSpec((tm, tn), lambda i, j, k: (i, j)), pl.BlockSpec((tm, tk), lambda i, j, k: (i, k)),
                  pl.BlockSpec((tk, tn), lambda i, j, k: (k, j))],
        out_specs=pl.BlockSpec((tm, tn), lambda i, j, k: (i, j)),
        out_shape=SDS((T, D), F32), compiler_params=_params(3))(h1, z, w_down)


def _final_loss(h2, g, target):
    T, D = h2.shape
    tm = _tile(T, 512)

    def body(h_ref, g_ref, t_ref, loss_ref, dg_ref, dh_ref, dhb_ref):
        @pl.when(pl.program_id(0) == 0)
        def _():
            loss_ref[...] = jnp.zeros_like(loss_ref)
            dg_ref[...] = jnp.zeros_like(dg_ref)

        hv, gv = h_ref[...], g_ref[...]
        r = _rms_stats(hv)
        hn = hv * r
        e = hn * gv - t_ref[...]
        loss_ref[...] += (0.5 / D) * jnp.sum(jnp.sum(e * e, axis=0, keepdims=True), axis=-1, keepdims=True)
        dy = e * (1.0 / D)
        dg_ref[...] += jnp.sum(dy * hn, axis=0, keepdims=True)
        dh = _rms_bwd(dy, hv, r, gv)
        dh_ref[...] = dh
        dhb_ref[...] = dh.astype(BF16)

    return _CHAIN.call(
        body, name="final_loss", grid=(T // tm,),
        in_specs=[pl.BlockSpec((tm, D), lambda i: (i, 0)), pl.BlockSpec((1, D), lambda i: (0, 0)),
                  pl.BlockSpec((tm, D), lambda i: (i, 0))],
        out_specs=[pl.BlockSpec((1, 1), lambda i: (0, 0)), pl.BlockSpec((1, D), lambda i: (0, 0)),
                   pl.BlockSpec((tm, D), lambda i: (i, 0)), pl.BlockSpec((tm, D), lambda i: (i, 0))],
        out_shape=[SDS((1, 1), F32), SDS((1, D), F32), SDS((T, D), F32), SDS((T, D), BF16)],
        compiler_params=_params(1))(h2, g, target)


def _ffn_down_bwd(dh2b, z, w_down):
    T, D = dh2b.shape
    F = w_down.shape[0]
    tm, tf = _tile(T, 1024), _tile(F, 1024)

    def body(dh_ref, z_ref, wd_ref, dzp_ref):
        dzz = _dot(dh_ref[...], wd_ref[...], NT)
        dzp_ref[...] = (dzz * (2.0 * z_ref[...].astype(F32))).astype(BF16)

    return _CHAIN.call(
        body, name="ffn_down_bwd", grid=(T // tm, F // tf),
        in_specs=[pl.BlockSpec((tm, D), lambda i, j: (i, 0)), pl.BlockSpec((tm, tf), lambda i, j: (i, j)),
                  pl.BlockSpec((tf, D), lambda i, j: (j, 0))],
        out_specs=pl.BlockSpec((tm, tf), lambda i, j: (i, j)),
        out_shape=SDS((T, F), BF16), compiler_params=_params(2))(dh2b, z, w_down)


def _ffn_up_bwd(dzp, w_up_t):
    T, F = dzp.shape
    D = w_up_t.shape[1]
    tm, tn, tk = _tile(T, 1024), _tile(D, 1024), _tile(F, 4096)

    def body(dzp_ref, w_ref, dn_ref):
        part = _dot(dzp_ref[...], w_ref[...], NN)

        @pl.when(pl.program_id(2) == 0)
        def _():
            dn_ref[...] = part

        @pl.when(pl.program_id(2) > 0)
        def _():
            dn_ref[...] += part

    return _CHAIN.call(
        body, name="ffn_up_bwd", grid=(T // tm, D // tn, F // tk),
        in_specs=[pl.BlockSpec((tm, tk), lambda i, j, k: (i, k)), pl.BlockSpec((tk, tn), lambda i, j, k: (k, j))],
        out_specs=pl.BlockSpec((tm, tn), lambda i, j, k: (i, j)),
        out_shape=SDS((T, D), F32), compiler_params=_params(3))(dzp, w_up_t)


def _ffn_norm_bwd(dn, dh2, h1, g):
    T, D = h1.shape
    tm = _tile(T, 512)

    def body(dn_ref, dh_ref, h_ref, g_ref, dh1_ref, dh1b_ref, dg_ref):
        @pl.when(pl.program_id(0) == 0)
        def _():
            dg_ref[...] = jnp.zeros_like(dg_ref)

        hv, dnv = h_ref[...], dn_ref[...]
        r = _rms_stats(hv)
        dg_ref[...] += jnp.sum(dnv * (hv * r), axis=0, keepdims=True)
        dh1 = dh_ref[...] + _rms_bwd(dnv, hv, r, g_ref[...])
        dh1_ref[...] = dh1
        dh1b_ref[...] = dh1.astype(BF16)

    row = pl.BlockSpec((tm, D), lambda i: (i, 0))
    vec = pl.BlockSpec((1, D), lambda i: (0, 0))
    return _CHAIN.call(
        body, name="ffn_norm_bwd", grid=(T // tm,), in_specs=[row, row, row, vec], out_specs=[row, row, vec],
        out_shape=[SDS((T, D), F32), SDS((T, D), BF16), SDS((1, D), F32)], compiler_params=_params(1))(dn, dh2, h1, g)


def _matmul_tn(a, b, name, square_a=False, col_blocks=None):
    T, K = a.shape
    N = b.shape[1]
    tk = _tile(K, 1792)
    tn = _tile(N if col_blocks is None else N // col_blocks, 1024 if tk <= 1024 else 512)

    def body(a_ref, b_ref, o_ref):
        av = a_ref[...]
        if square_a:
            af = av.astype(F32)
            av = (af * af).astype(BF16)
        o_ref[...] = _dot(av, b_ref[...], TN).astype(o_ref.dtype)

    if col_blocks is None:
        out_shape = SDS((K, N), BF16)
        out_spec = pl.BlockSpec((tk, tn), lambda i, j: (i, j))
    else:
        per = (N // col_blocks) // tn
        out_shape = SDS((col_blocks, K, N // col_blocks), BF16)
        out_spec = pl.BlockSpec((None, tk, tn), lambda i, j: (j // per, i, j % per))
    return _CHAIN.call(
        body, name=name, grid=(K // tk, N // tn),
        in_specs=[pl.BlockSpec((T, tk), lambda i, j: (0, i)), pl.BlockSpec((T, tn), lambda i, j: (0, j))],
        out_specs=out_spec, out_shape=out_shape, compiler_params=_params(2))(a, b)


def _outproj_bwd(dh1b, w, a, b, ga, gb):
    T, D = dh1b.shape
    A, B = a.shape[1], b.shape[1]
    tm = _tile(T, 512)

    def body(dh_ref, w_ref, a_ref, b_ref, ga_ref, gb_ref, da_ref, db_ref, dga_ref, dgb_ref):
        @pl.when(pl.program_id(0) == 0)
        def _():
            dga_ref[...] = jnp.zeros_like(dga_ref)
            dgb_ref[...] = jnp.zeros_like(dgb_ref)

        dmix = _dot(dh_ref[...], w_ref[...], NT)
        for src_ref, g_ref, dx_ref, dg_ref, dn in ((a_ref, ga_ref, da_ref, dga_ref, dmix[:, :A]),
                                                   (b_ref, gb_ref, db_ref, dgb_ref, dmix[:, A:])):
            xv = src_ref[...]
            r = _rms_stats(xv)
            dg_ref[...] += jnp.sum(dn * (xv * r), axis=0, keepdims=True)
            dx_ref[...] = _rms_bwd(dn, xv, r, g_ref[...])

    return _CHAIN.call(
        body, name="outproj_bwd", grid=(T // tm,),
        in_specs=[pl.BlockSpec((tm, D), lambda i: (i, 0)), _resident((A + B, D)),
                  pl.BlockSpec((tm, A), lambda i: (i, 0)), pl.BlockSpec((tm, B), lambda i: (i, 0)),
                  pl.BlockSpec((1, A), lambda i: (0, 0)), pl.BlockSpec((1, B), lambda i: (0, 0))],
        out_specs=[pl.BlockSpec((tm, A), lambda i: (i, 0)), pl.BlockSpec((tm, B), lambda i: (i, 0)),
                   pl.BlockSpec((1, A), lambda i: (0, 0)), pl.BlockSpec((1, B), lambda i: (0, 0))],
        out_shape=[SDS((T, A), F32), SDS((T, B), F32), SDS((1, A), F32), SDS((1, B), F32)],
        compiler_params=_params(1))(dh1b, w, a, b, ga, gb)


def _gmlp_bwd(proj, da, lg, lb, w_s, w_st, bs_t, A):
    T = proj.shape[0]
    G = A // GROUP_DIM
    tm = _tile(T, 512)
    nc = tm // CHUNK

    def body(u_ref, v_ref, da_ref, lg_ref, lb_ref, w_ref, wt_ref, bst_ref, duv_ref, dlg_ref, dlb_ref, dw_ref, dbs_ref):
        @pl.when(pl.program_id(0) == 0)
        def _():
            dlg_ref[...] = jnp.zeros_like(dlg_ref)
            dlb_ref[...] = jnp.zeros_like(dlb_ref)
            dw_ref[...] = jnp.zeros_like(dw_ref)
            dbs_ref[...] = jnp.zeros_like(dbs_ref)

        row = lax.broadcasted_iota(jnp.int32, (CHUNK, CHUNK), 0)
        col = lax.broadcasted_iota(jnp.int32, (CHUNK, CHUNK), 1)
        lower = row >= col
        upper = row <= col
        for g in range(G):
            sl = slice(g * GROUP_DIM, (g + 1) * GROUP_DIM)
            lgv = lg_ref[:, sl]
            vg, vg_grad = _gelu_and_grad(v_ref[:, sl])
            vhat, rstd, vn = _layer_norm_group(vg, lgv, lb_ref[:, sl])
            vnb = vn.astype(BF16)
            ug, ug_grad = _gelu_and_grad(u_ref[:, sl])
            dav = da_ref[:, sl]
            wm = jnp.where(lower, w_ref[g], 0.0).astype(BF16)
            wmt = jnp.where(upper, wt_ref[g], 0.0).astype(BF16)
            bcol = bst_ref[:, g:g + 1]
            dw_acc = jnp.zeros((CHUNK, CHUNK), F32)
            dbs_acc = jnp.zeros((CHUNK, 1), F32)
            dvn_parts = []
            dug_parts = []
            for c in range(nc):
                rs = slice(c * CHUNK, (c + 1) * CHUNK)
                mixed = _dot(wm, vnb[rs], NN) + bcol
                dug_parts.append(dav[rs] * mixed)
                dmix = dav[rs] * ug[rs]
                dbs_acc = dbs_acc + jnp.sum(dmix, axis=-1, keepdims=True)
                dmixb = dmix.astype(BF16)
                dw_acc = dw_acc + _dot(dmixb, vnb[rs], NT)
                dvn_parts.append(_dot(wmt, dmixb, NN))
            dvn = jnp.concatenate(dvn_parts, axis=0)
            dug = jnp.concatenate(dug_parts, axis=0)
            dw_ref[g] += jnp.where(lower, dw_acc, 0.0)
            dbs_ref[:, g:g + 1] += dbs_acc
            dlg_ref[:, sl] += jnp.sum(dvn * vhat, axis=0, keepdims=True)
            dlb_ref[:, sl] += jnp.sum(dvn, axis=0, keepdims=True)
            dvhat = dvn * lgv
            dvg = rstd * (dvhat - jnp.mean(dvhat, axis=-1, keepdims=True)
                          - vhat * jnp.mean(dvhat * vhat, axis=-1, keepdims=True))
            duv_ref[:, sl] = (dug * ug_grad).astype(BF16)
            duv_ref[:, A + g * GROUP_DIM:A + (g + 1) * GROUP_DIM] = (dvg * vg_grad).astype(BF16)

    return _CHAIN.call(
        body, name="gmlp_bwd", grid=(T // tm,),
        in_specs=[pl.BlockSpec((tm, A), lambda i: (i, 0)), pl.BlockSpec((tm, A), lambda i: (i, 1)),
                  pl.BlockSpec((tm, A), lambda i: (i, 0)),
                  pl.BlockSpec((1, A), lambda i: (0, 0)), pl.BlockSpec((1, A), lambda i: (0, 0)),
                  pl.BlockSpec((G, CHUNK, CHUNK), lambda i: (0, 0, 0)),
                  pl.BlockSpec((G, CHUNK, CHUNK), lambda i: (0, 0, 0)), pl.BlockSpec((CHUNK, G), lambda i: (0, 0))],
        out_specs=[pl.BlockSpec((tm, 2 * A), lambda i: (i, 0)),
                   pl.BlockSpec((1, A), lambda i: (0, 0)), pl.BlockSpec((1, A), lambda i: (0, 0)),
                   pl.BlockSpec((G, CHUNK, CHUNK), lambda i: (0, 0, 0)), pl.BlockSpec((CHUNK, G), lambda i: (0, 0))],
        out_shape=[SDS((T, 2 * A), BF16), SDS((1, A), F32), SDS((1, A), F32),
                   SDS((G, CHUNK, CHUNK), F32), SDS((CHUNK, G), F32)],
        compiler_params=_params(1))(proj, proj, da, lg, lb, w_s, w_st, bs_t)


def _attn_bwd(proj, do, duv, bias_t, sinks, A, B):
    T, P = proj.shape
    H = B // HEAD_DIM
    qpk = H // KV_HEADS
    tq = _tile(T, 512)
    nb = tq // CHUNK
    n_tiles = T // tq
    scale = HEAD_DIM ** -0.5
    rev = lambda i: n_tiles - 1 - i

    def body(sink_ref, q_ref, k_ref, v_ref, kp_ref, vp_ref, do_ref, duv_ref, bias_ref,
             dproj_ref, dbias_ref, dsink_ref, carry, dkv, sacc):
        step = pl.program_id(0)

        @pl.when(step == 0)
        def _():
            carry[...] = jnp.zeros_like(carry)
            sacc[...] = jnp.zeros_like(sacc)
            dbias_ref[...] = jnp.zeros_like(dbias_ref)

        jj = lax.broadcasted_iota(jnp.int32, (2 * CHUNK, CHUNK), 0)
        ii = lax.broadcasted_iota(jnp.int32, (2 * CHUNK, CHUNK), 1)
        in_window = (jj > ii) & (jj <= ii + CHUNK)
        first_mask = in_window & jnp.logical_or(step != n_tiles - 1, jj >= CHUNK)
        low_query = lax.broadcasted_iota(jnp.int32, (CHUNK, LANE), 1) < HEAD_DIM
        low_key = lax.broadcasted_iota(jnp.int32, (2 * CHUNK, LANE), 1) < HEAD_DIM

        def split_pair(pair_bf16):
            zero = jnp.zeros_like(pair_bf16)
            return jnp.concatenate([jnp.where(low_query, pair_bf16, zero), jnp.where(low_query, zero, pair_bf16)], axis=0)

        dproj_ref[:, :2 * A] = duv_ref[...]
        dkv[...] = jnp.zeros_like(dkv)
        for b in range(nb):
            rows = slice(b * CHUNK, (b + 1) * CHUNK)
            band = slice(b * CHUNK, (b + 2) * CHUNK)
            if b == 0:
                kprev, vprev, mask = kp_ref[...], vp_ref[...], first_mask
            else:
                prows = slice((b - 1) * CHUNK, b * CHUNK)
                kprev, vprev, mask = k_ref[prows, :], v_ref[prows, :], in_window
            kband = jnp.concatenate([kprev, k_ref[rows, :]], axis=0)
            vband = jnp.concatenate([vprev, v_ref[rows, :]], axis=0)
            k_pads = [_pad_heads(kband, g) for g in range(KV_HEADS)]
            v_pads = [_pad_heads(vband, g) for g in range(KV_HEADS)]
            queries, douts, scores, dprobs = [], [], [], []
            for pair in range(H // 2):
                cols = slice(2 * pair * HEAD_DIM, (2 * pair + 2) * HEAD_DIM)
                qs = (q_ref[rows, cols] * scale).astype(BF16)
                dob = do_ref[rows, cols].astype(BF16)
                queries.append(qs)
                douts.append(dob)
                scores += [_dot(kz, qs, NT) for kz in k_pads[2 * pair // qpk]]
                dprobs += [_dot(vz, dob, NT) for vz in v_pads[2 * pair // qpk]]
            probs, dscores = [], []
            for h in range(H):
                pt, p_sink = _softmax_with_sink(jnp.where(mask, scores[h] + bias_ref[h], NEG), sink_ref[h], 0)
                delta = jnp.sum(pt * dprobs[h], axis=0, keepdims=True)
                dst = pt * (dprobs[h] - delta)
                dbias_ref[h] += dst
                sacc[h:h + 1, :] += -(p_sink * delta)
                probs.append(pt.astype(BF16))
                dscores.append(dst.astype(BF16))
            dq_parts, dk_groups, dv_groups = [], [], []
            for g in range(KV_HEADS):
                k_both = jnp.concatenate(k_pads[g], axis=0)
                dk_acc = jnp.zeros((2 * CHUNK, LANE), F32)
                dv_acc = jnp.zeros((2 * CHUNK, LANE), F32)
                for pair in range(g * qpk // 2, (g + 1) * qpk // 2):
                    pair_heads = slice(2 * pair, 2 * pair + 2)
                    dk_acc = dk_acc + _dot(jnp.concatenate(dscores[pair_heads], axis=1), split_pair(queries[pair]), NN)
                    dv_acc = dv_acc + _dot(jnp.concatenate(probs[pair_heads], axis=1), split_pair(douts[pair]), NN)
                    dq_parts.append(_dot(jnp.concatenate(dscores[pair_heads], axis=0), k_both, TN) * scale)
                dk_groups.append(dk_acc + pltpu.roll(dk_acc, HEAD_DIM, 1))
                dv_groups.append(dv_acc + pltpu.roll(dv_acc, HEAD_DIM, 1))
            dkv[band, :LANE] += jnp.where(low_key, dk_groups[0], dk_groups[1])
            dkv[band, LANE:] += jnp.where(low_key, dv_groups[0], dv_groups[1])
            dproj_ref[rows, 2 * A:2 * A + B] = jnp.concatenate(dq_parts, axis=1).astype(BF16)
        last = slice(tq, tq + CHUNK)
        dkv[last, :] += carry[...]
        dproj_ref[:, 2 * A + B:] = dkv[CHUNK:, :].astype(BF16)
        carry[...] = dkv[:CHUNK, :]

        @pl.when(step == n_tiles - 1)
        def _():
            dsink_ref[...] = jnp.sum(sacc[...], axis=1, keepdims=True)

    specs = _attn_specs(tq, A, B, reverse_tiles=n_tiles)
    return _CHAIN.call(
        body, name="attn_bwd", grid=(n_tiles,),
        in_specs=[pl.BlockSpec(memory_space=pltpu.SMEM)] + specs
        + [pl.BlockSpec((tq, B), lambda i: (rev(i), 0)), pl.BlockSpec((tq, 2 * A), lambda i: (rev(i), 0)),
           pl.BlockSpec((H, 2 * CHUNK, CHUNK), lambda i: (0, 0, 0))],
        out_specs=[pl.BlockSpec((tq, P), lambda i: (rev(i), 0)),
                   pl.BlockSpec((H, 2 * CHUNK, CHUNK), lambda i: (0, 0, 0)), pl.BlockSpec((H, 1), lambda i: (0, 0))],
        out_shape=[SDS((T, P), BF16), SDS((H, 2 * CHUNK, CHUNK), F32), SDS((H, 1), F32)],
        scratch_shapes=[pltpu.VMEM((CHUNK, 2 * LANE), F32), pltpu.VMEM((tq + CHUNK, 2 * LANE), F32),
                        pltpu.VMEM((H, LANE), F32)],
        compiler_params=_params(1))(sinks, proj, proj, proj, proj, proj, do, duv, bias_t)


def _bias_bwd(dbias, onehot):
    H = dbias.shape[0]
    nbk = onehot.shape[1]

    def body(d_ref, oh_ref, o_ref):
        hi, mid, lo = _split3(d_ref[...])
        oh = oh_ref[...]
        o_ref[...] = _dot(hi, oh, NN) + _dot(mid, oh, NN) + _dot(lo, oh, NN)

    return _CHAIN.call(body, name="bias_bwd", in_specs=[VMEM_SPEC] * 2, out_specs=VMEM_SPEC, out_shape=SDS((H, nbk), F32),
                       compiler_params=_params(0))(dbias, onehot)


def _inproj_bwd(dproj, w_t, x, dh1, g):
    T, P = dproj.shape
    D = x.shape[1]
    tm = _tile(T, 512)

    def body(dp_ref, w_ref, x_ref, dh_ref, g_ref, dx_ref, dg_ref):
        @pl.when(pl.program_id(0) == 0)
        def _():
            dg_ref[...] = jnp.zeros_like(dg_ref)

        dn = _dot(dp_ref[...], w_ref[...], NN)
        xv = x_ref[...]
        r = _rms_stats(xv)
        dg_ref[...] += jnp.sum(dn * (xv * r), axis=0, keepdims=True)
        dx_ref[...] = dh_ref[...] + _rms_bwd(dn, xv, r, g_ref[...])

    return _CHAIN.call(
        body, name="inproj_bwd", grid=(T // tm,),
        in_specs=[pl.BlockSpec((tm, P), lambda i: (i, 0)), _resident((P, D)),
                  pl.BlockSpec((tm, D), lambda i: (i, 0)), pl.BlockSpec((tm, D), lambda i: (i, 0)),
                  pl.BlockSpec((1, D), lambda i: (0, 0))],
        out_specs=[pl.BlockSpec((tm, D), lambda i: (i, 0)), pl.BlockSpec((1, D), lambda i: (0, 0))],
        out_shape=[SDS((T, D), F32), SDS((1, D), F32)], compiler_params=_params(1))(dproj, w_t, x, dh1, g)


def _adamw(w, g, m, v):
    m = ADAM_B1 * m + (1.0 - ADAM_B1) * g
    v = ADAM_B2 * v + (1.0 - ADAM_B2) * (g * g)
    m_hat = m / (1.0 - ADAM_B1 ** ADAM_STEP)
    v_hat = v / (1.0 - ADAM_B2 ** ADAM_STEP)
    delta = -ADAM_LR * (m_hat / (jnp.sqrt(v_hat) + ADAM_EPS) + ADAM_WD * w)
    return delta, m, v


def _adam_sharded(csum, recv, w, m, v, name):
    R, C = w.shape
    tr = _tile(R, 256, 16)

    def body(own_ref, recv_ref, w_ref, m_ref, v_ref, g_ref, d_ref, nm_ref, nv_ref):
        g = own_ref[...].astype(F32)
        for r in range(3):
            g = g + recv_ref[r].astype(F32)
        delta, nm, nv = _adamw(w_ref[...], g, m_ref[...], v_ref[...])
        g_ref[...] = g
        d_ref[...] = delta
        nm_ref[...] = nm
        nv_ref[...] = nv

    blk = pl.BlockSpec((tr, C), lambda i: (i, 0))
    return _CHAIN.call(
        body, name=name, grid=(R // tr,),
        in_specs=[pl.BlockSpec((None, tr, C), lambda i: (0, i, 0)), pl.BlockSpec((3, tr, C), lambda i: (0, i, 0)),
                  blk, blk, blk],
        out_specs=[blk] * 4, out_shape=[SDS((R, C), F32)] * 4, compiler_params=_params(1))(csum, recv, w, m, v)


def _rows2d(shape):
    return (int(np.prod(shape[:-1])) if len(shape) > 1 else 1, shape[-1])


def _small_layout(shapes):
    totals, places = {}, []
    for s in shapes:
        r, w = _rows2d(s)
        off = totals.get(w, 0)
        places.append((w, off, r))
        totals[w] = off + -(-r // 8) * 8
    return {w: -(-t // 32) * 32 for w, t in totals.items()}, places


def _pack_small(arrays, totals, places):
    bufs = []
    for w, total in totals.items():
        buf = jnp.zeros((total, w), F32)
        for a, (pw, off, r) in zip(arrays, places):
            if pw == w:
                buf = lax.dynamic_update_slice(buf, a.reshape(r, w).astype(F32), (off, 0))
        bufs.append(buf)
    return bufs


def _adam_small(gathered, totals, places, ws, ms, vs):
    widths = list(totals)
    n, nw = len(places), len(widths)

    def body(*refs):
        gath, params, outs = refs[:nw], refs[nw:nw + 3 * n], refs[nw + 3 * n:]
        for p, (w, off, r) in enumerate(places):
            g_ref = gath[widths.index(w)]
            g = g_ref[0, off:off + r, :]
            for d in range(1, N_DEV):
                g = g + g_ref[d, off:off + r, :]
            delta, nm, nv = _adamw(params[p][...], g, params[n + p][...], params[2 * n + p][...])
            for k, val in enumerate((g, delta, nm, nv)):
                outs[4 * p + k][...] = val

    shapes2d = [SDS((r, w), F32) for w, _, r in places for _ in range(4)]
    outs = _CHAIN.call(body, name="adam_small", in_specs=[VMEM_SPEC] * (nw + 3 * n), out_specs=[VMEM_SPEC] * (4 * n),
                       out_shape=shapes2d, compiler_params=_params(0))(*gathered, *ws, *ms, *vs)
    return [outs[4 * p:4 * p + 4] for p in range(n)]


def kernel(x, rel_bias_table, mix_norm_g, w_in, gate_norm_g, gate_norm_b, w_spatial, b_spatial, attn_sinks, out_norm_a_g, out_norm_b_g, w_out, ffn_norm_g, w_up, w_down, final_norm_g, loss_target, m_rel_bias_table, m_mix_norm_g, m_w_in, m_gate_norm_g, m_gate_norm_b, m_w_spatial, m_b_spatial, m_attn_sinks, m_out_norm_a_g, m_out_norm_b_g, m_w_out, m_ffn_norm_g, m_w_up, m_w_down, m_final_norm_g, v_rel_bias_table, v_mix_norm_g, v_w_in, v_gate_norm_g, v_gate_norm_b, v_w_spatial, v_b_spatial, v_attn_sinks, v_out_norm_a_g, v_out_norm_b_g, v_w_out, v_ffn_norm_g, v_w_up, v_w_down, v_final_norm_g):
    T, D = x.shape[1], x.shape[2]
    A = D // 2
    B = D // 2
    H = B // HEAD_DIM
    P = 2 * A + B + 2 * KV_HEADS * HEAD_DIM
    xs = x.reshape(T, D)
    target = loss_target.reshape(T, D)

    win_t, m_win_t, v_win_t = (jnp.swapaxes(a[0], 0, 1) for a in (w_in, m_w_in, v_w_in))
    shards = [win_t.astype(BF16), w_out[0].astype(BF16), w_up[0].astype(BF16), w_down[0].astype(BF16)]
    _CHAIN.token = None
    gather = _gather_begin(shards)
    _gather_step(gather, [(0, 0)], "gather_start")

    g1, g2, g3 = mix_norm_g.reshape(1, D), ffn_norm_g.reshape(1, D), final_norm_g.reshape(1, D)
    lg, lb = gate_norm_g.reshape(1, A), gate_norm_b.reshape(1, A)
    ws = w_spatial[0]
    ws_t = jnp.swapaxes(ws, 1, 2)
    bs_t = jnp.transpose(b_spatial[0])
    ga, gb = out_norm_a_g.reshape(1, A), out_norm_b_g.reshape(1, B)
    sinks = attn_sinks.reshape(H)
    bucket, in_window = _t5_bucket()
    onehot_np = ((bucket[:, :, None] == np.arange(N_BUCKETS)) & in_window[:, :, None]).astype(np.float32)
    onehot = jnp.asarray(onehot_np.reshape(-1, N_BUCKETS)).astype(BF16)
    onehot_kq = jnp.asarray(onehot_np.transpose(1, 0, 2).reshape(-1, N_BUCKETS)).astype(BF16)

    bias, bias_t = _bias_fwd(jnp.transpose(rel_bias_table), jnp.transpose(onehot), jnp.transpose(onehot_kq))
    bias, bias_t = bias.reshape(H, CHUNK, 2 * CHUNK), bias_t.reshape(H, 2 * CHUNK, CHUNK)
    n1 = _mix_norm(xs, g1)
    _gather_step(gather, [(0, 1), (2, 0), (1, 0)], "gather_in_1")
    _gather_step(gather, [(0, 2)], "gather_in_2")
    (win_g,) = _gather_end(gather, [0], "gather_in_end")
    win_t_full = win_g.reshape(P, D)
    proj = _inproj_fwd(n1, win_t_full)
    _gather_step(gather, [(2, 1)], "gather_up_1")
    a_out = _gmlp_fwd(proj, lg, lb, ws, bs_t, A)
    _gather_step(gather, [(1, 1), (3, 0)], "gather_out_1")
    b_out = _attn_fwd(proj, bias, sinks, A, B)
    _gather_step(gather, [(1, 2), (2, 2)], "gather_out_2_up_2")
    (wout_g,) = _gather_end(gather, [1], "gather_out_end")
    wout_full = wout_g.reshape(A + B, D)
    h1, mixed, n2 = _outproj_fwd(a_out, b_out, ga, gb, xs, wout_full, g2)
    (wup_g,) = _gather_end(gather, [2], "gather_up_end")
    _gather_step(gather, [(3, 1)], "gather_down_1")
    wup_t = jnp.transpose(wup_g, (0, 2, 1)).reshape(-1, D)
    z = _ffn_up(n2, wup_g)
    _gather_step(gather, [(3, 2)], "gather_down_2")
    (wdown_g,) = _gather_end(gather, [3], "gather_down_end")
    h2 = _ffn_down(h1, z, wdown_g.reshape(-1, D))
    loss_part, dg3, dh2, dh2b = _final_loss(h2, g3, target)

    def reduce_to_chip(state, name):
        csums = [_chip_sum(part, received, "%s_chip_sum_%d" % (name, a))
                 for a, (part, received) in enumerate(_sibling_exchange_end(state, name + "_sib_end"))]
        return _chip_exchange_begin(csums, name + "_chip")

    dwdown = _matmul_tn(z, dh2b, "grad_w_down", square_a=True).reshape(wdown_g.shape)
    dzp = _ffn_down_bwd(dh2b, z, wdown_g.reshape(-1, D))
    dwup = _matmul_tn(n2, dzp, "grad_w_up", col_blocks=N_DEV)
    sib_ffn = _sibling_exchange_begin([dwdown, dwup], "rs_ffn_sib")
    dh1, dh1b, dg2 = _ffn_norm_bwd(_ffn_up_bwd(dzp, wup_t), dh2, h1, g2)
    chip_ffn = reduce_to_chip(sib_ffn, "rs_ffn")
    da, db, dga, dgb = _outproj_bwd(dh1b, wout_full, a_out, b_out, ga, gb)
    dwout = _matmul_tn(mixed, dh1b, "grad_w_out").reshape(wout_g.shape)
    sib_out = _sibling_exchange_begin([dwout], "rs_out_sib")
    duv, dlg, dlb, dws, dbs_t = _gmlp_bwd(proj, da, lg, lb, ws, ws_t, bs_t, A)
    dproj, dbias_t, dsinks = _attn_bwd(proj, db, duv, bias_t, sinks, A, B)
    chip_out = reduce_to_chip(sib_out, "rs_out")
    dwin_t = _matmul_tn(dproj, n1, "grad_w_in").reshape(win_g.shape)
    sib_in = _sibling_exchange_begin([dwin_t], "rs_in_sib")
    dtable_t = _bias_bwd(dbias_t.reshape(H, -1), onehot_kq)
    chip_in = reduce_to_chip(sib_in, "rs_in")
    grad_x, dg1 = _inproj_bwd(dproj, win_t_full, xs, dh1, g1)

    small_w = [rel_bias_table, mix_norm_g, gate_norm_g, gate_norm_b, w_spatial, b_spatial, attn_sinks,
               out_norm_a_g, out_norm_b_g, ffn_norm_g, final_norm_g]
    small_m = [m_rel_bias_table, m_mix_norm_g, m_gate_norm_g, m_gate_norm_b, m_w_spatial, m_b_spatial, m_attn_sinks,
               m_out_norm_a_g, m_out_norm_b_g, m_ffn_norm_g, m_final_norm_g]
    small_v = [v_rel_bias_table, v_mix_norm_g, v_gate_norm_g, v_gate_norm_b, v_w_spatial, v_b_spatial, v_attn_sinks,
               v_out_norm_a_g, v_out_norm_b_g, v_ffn_norm_g, v_final_norm_g]
    small_g = [jnp.transpose(dtable_t), dg1, dlg, dlb, dws, jnp.transpose(dbs_t), dsinks, dga, dgb, dg2, dg3]
    nothing = jnp.zeros((1, H), F32)
    small_w, small_m, small_v = small_w + [nothing], small_m + [nothing], small_v + [nothing]
    small_g = small_g + [jnp.broadcast_to(loss_part, (1, H))]
    shapes = [w.shape for w in small_w]
    totals, places = _small_layout(shapes)
    as_rows = lambda arrays: [a.reshape(_rows2d(a.shape)) for a in arrays]
    big = [None] * 4

    def adam_of(k, state, a, w, m, v):
        csum, received = _chip_exchange_end(state, a, "rs_%d_end" % k)
        big[k] = _adam_sharded(csum, received, w, m, v, "adam_%d" % k)

    small_gather = _gather_begin(_pack_small(small_g, totals, places))
    every = range(len(totals))
    _gather_step(small_gather, [(a, 0) for a in every], "small_gather_start")
    adam_of(3, chip_ffn, 0, w_down[0], m_w_down[0], v_w_down[0])
    _gather_step(small_gather, [(a, 1) for a in every], "small_gather_1")
    adam_of(2, chip_ffn, 1, w_up[0], m_w_up[0], v_w_up[0])
    _gather_step(small_gather, [(a, 2) for a in every], "small_gather_2")
    adam_of(1, chip_out, 0, w_out[0], m_w_out[0], v_w_out[0])
    adam_of(0, chip_in, 0, win_t, m_win_t, v_win_t)
    gathered = _gather_end(small_gather, list(every), "small_gather_end")
    small_out = _adam_small(gathered, totals, places, as_rows(small_w), as_rows(small_m), as_rows(small_v))
    sg, sd, sm, sv = [[outs[k].reshape(s) for outs, s in zip(small_out, shapes)] for k in range(4)]
    big[0] = [jnp.swapaxes(o, 0, 1) for o in big[0]]
    big = [[o.reshape(w.shape) for o in outs] for outs, w in zip(big, (w_in, w_out, w_up, w_down))]

    loss = sg[-1][0, 0]

    order = ["s0", "s1", "b0", "s2", "s3", "s4", "s5", "s6", "s7", "s8", "b1", "s9", "b2", "b3", "s10"]

    def group(idx):
        small = (sg, sd, sm, sv)[idx]
        return [small[int(t[1:])] if t[0] == "s" else big[int(t[1:])][idx] for t in order]

    return (loss, grad_x.reshape(x.shape), *group(0), *group(1), *group(2), *group(3))
```

```python
import math

import numpy as np
import jax
import jax.numpy as jnp
from jax import lax
from jax.experimental import pallas as pl
from jax.experimental.pallas import tpu as pltpu

F32 = jnp.float32
BF16 = jnp.bfloat16
SDS = jax.ShapeDtypeStruct
MESH = pl.DeviceIdType.MESH

N_DEV = 8
EPS = 1e-5
NEG = -1e30
CHUNK = 128
GROUP_DIM = 128
HEAD_DIM = 64
KV_HEADS = 2
N_BUCKETS = 32
MAX_DISTANCE = 128
ADAM_LR, ADAM_B1, ADAM_B2, ADAM_EPS, ADAM_WD, ADAM_STEP = 0.001, 0.9, 0.999, 1e-08, 0.01, 10
GELU_C0 = math.sqrt(2.0 / math.pi)
GELU_C1 = 0.044715

V7X_VMEM_BYTES = 64 * 1024 * 1024
VMEM_LIMIT = V7X_VMEM_BYTES - 8 * 1024 * 1024
LANE = 128

NN = ((1,), (0,))
NT = ((1,), (1,))
TN = ((0,), (0,))


def _dot(a, b, dims):
    return lax.dot_general(a, b, (dims, ((), ())), preferred_element_type=F32)


def _tile(n, pref, unit=LANE):
    best = None
    for t in range(unit, min(n, pref) + 1, unit):
        if n % t == 0:
            best = t
    return n if best is None else best


def _params(n_grid):
    return pltpu.CompilerParams(dimension_semantics=("arbitrary",) * n_grid, vmem_limit_bytes=VMEM_LIMIT)


def _resident(shape):
    return pl.BlockSpec(shape, lambda i: (0, 0), pipeline_mode=pl.Buffered(1))


def _gelu(x):
    return 0.5 * x * (1.0 + jnp.tanh(GELU_C0 * (x + GELU_C1 * x * x * x)))


def _gelu_and_grad(x):
    x2 = x * x
    t = jnp.tanh(GELU_C0 * x * (1.0 + GELU_C1 * x2))
    val = 0.5 * x * (1.0 + t)
    grad = 0.5 * (1.0 + t) + 0.5 * x * (1.0 - t * t) * (GELU_C0 * (1.0 + 3.0 * GELU_C1 * x2))
    return val, grad


def _rms_stats(x):
    return lax.rsqrt(jnp.mean(x * x, axis=-1, keepdims=True) + EPS)


def _rms_bwd(dy, x, r, g):
    w = dy * g
    return r * w - x * (r * r * r) * jnp.mean(w * x, axis=-1, keepdims=True)


def _t5_bucket():
    i = np.arange(CHUNK)[:, None]
    j = np.arange(2 * CHUNK)[None, :]
    rel = np.maximum(i + CHUNK - j, 0)
    n_exact = N_BUCKETS // 2
    relf = np.maximum(rel, n_exact).astype(np.float32)
    large = n_exact + (np.log(relf / np.float32(n_exact)) / np.float32(math.log(MAX_DISTANCE / n_exact))
                       * np.float32(N_BUCKETS - n_exact)).astype(np.int32)
    large = np.minimum(large, N_BUCKETS - 1)
    bucket = np.where(rel < n_exact, rel, large)
    in_window = (i + CHUNK - j >= 0) & (i + CHUNK - j < CHUNK)
    return bucket.astype(np.int32), in_window


def _split3(x):
    hi = x.astype(BF16)
    r1 = x - hi.astype(F32)
    mid = r1.astype(BF16)
    lo = (r1 - mid.astype(F32)).astype(BF16)
    return hi, mid, lo


HBM_SPEC = pl.BlockSpec(memory_space=pltpu.HBM)


def _mesh_pos():
    return lax.axis_index("x"), lax.axis_index("y"), lax.axis_index("c")


def _dev_index(px, py, pc):
    return 4 * px + 2 * py + pc


SEM_SPEC = pl.BlockSpec(memory_space=pltpu.SEMAPHORE)
ANY_SPEC = pl.BlockSpec(memory_space=pl.ANY)
VMEM_SPEC = pl.BlockSpec(memory_space=pltpu.VMEM)
TOKEN = SDS((8, LANE), F32)
SIDE_EFFECT = pltpu.SideEffectType.DATAFLOW_SIDE_EFFECTING


def _hbm(x):
    return pltpu.with_memory_space_constraint(x, pltpu.HBM)


class _CallChain:
    def __init__(self):
        self.token = None

    def call(self, body, *, in_specs, out_specs, out_shape, **kwargs):
        dep, n_in = self.token, len(in_specs)
        single = not isinstance(out_shape, (list, tuple))
        out_shapes = [out_shape] if single else list(out_shape)
        out_specs = [out_specs] if single else list(out_specs)
        n_out = len(out_shapes)
        n_dep = 0 if dep is None else 1
        token_spec = pl.BlockSpec((8, LANE), lambda *_: (0, 0)) if kwargs.get("grid") else VMEM_SPEC

        def chained(*refs):
            outs_at = n_in + n_dep
            body(*refs[:n_in], *refs[outs_at:outs_at + n_out], *refs[outs_at + n_out + 1:])
            token = refs[outs_at + n_out]
            token[...] = jnp.zeros_like(token)

        inner = pl.pallas_call(chained, in_specs=list(in_specs) + [ANY_SPEC] * n_dep, out_specs=out_specs + [token_spec],
                               out_shape=out_shapes + [TOKEN], **kwargs)

        def run(*operands):
            outs = inner(*operands) if dep is None else inner(*operands, dep)
            self.token = outs[n_out]
            return outs[0] if single else list(outs[:n_out])

        return run


_CHAIN = _CallChain()


def _wait_all(waits, x, y, c):
    for kind, src, dst, send_sem, recv_sem in waits:
        cp = pltpu.make_async_remote_copy(src_ref=src, dst_ref=dst, send_sem=send_sem, recv_sem=recv_sem,
                                          device_id=(x, y, c), device_id_type=MESH)
        if kind == "send":
            cp.wait_send()
        else:
            cp.wait_recv()


PEER_SETS = {"sibling": 0, "near": 1, "chips": 2}
_handshakes = {}


def _handshake(peer_set):
    x, y, c = _mesh_pos()
    peers = {"sibling": [(x, y, 1 - c)],
             "near": [(x, y, 1 - c), (1 - x, y, c), (x, 1 - y, c)],
             "chips": [(1 - x, y, c), (x, 1 - y, c), (1 - x, 1 - y, c)]}[peer_set]
    barrier = pltpu.get_barrier_semaphore()
    for peer in peers:
        pl.semaphore_signal(barrier, inc=1, device_id=peer, device_id_type=MESH)
    pl.semaphore_wait(barrier, len(peers))


def _split_start(bufs, copies_of, n_sems, name, peer_set, sem_sets=(), waits_of=None):
    n, ns = len(bufs), len(sem_sets)
    flat_sems = [s for pair in sem_sets for s in pair]
    uses = _handshakes.get(peer_set, 0)
    _handshakes[peer_set] = uses + 1
    collective_id = 3 * PEER_SETS[peer_set] + uses % 3

    def body(*refs):
        ins = refs[:n]
        sems = refs[n:n + 2 * ns]
        send_sems, recv_sems = refs[n + 2 * ns], refs[n + 2 * ns + 1]
        _handshake(peer_set)
        if waits_of is not None:
            _wait_all(waits_of(ins, [(sems[2 * i], sems[2 * i + 1]) for i in range(ns)]), *_mesh_pos())
        for src, dst, k, target in copies_of(ins):
            pltpu.make_async_remote_copy(src_ref=src, dst_ref=dst, send_sem=send_sems.at[k], recv_sem=recv_sems.at[k],
                                         device_id=target, device_id_type=MESH).start()

    outs = _CHAIN.call(
        body, name=name,
        out_shape=[pltpu.SemaphoreType.DMA((n_sems,)), pltpu.SemaphoreType.DMA((n_sems,))]
        + [pltpu.HBM(b.shape, b.dtype) for b in bufs],
        in_specs=[HBM_SPEC] * n + [SEM_SPEC] * (2 * ns), out_specs=[SEM_SPEC, SEM_SPEC] + [HBM_SPEC] * n,
        input_output_aliases={a: 2 + a for a in range(n)},
        compiler_params=pltpu.CompilerParams(has_side_effects=SIDE_EFFECT, collective_id=collective_id),
    )(*[_hbm(b) for b in bufs], *flat_sems)
    return outs[0], outs[1], list(outs[2:2 + n])


def _split_wait(bufs, sem_sets, waits_of, name):
    n, ns = len(bufs), len(sem_sets)
    flat_sems = [s for pair in sem_sets for s in pair]

    def body(*refs):
        ins = refs[:n]
        sems = refs[n:n + 2 * ns]
        _wait_all(waits_of(ins, [(sems[2 * i], sems[2 * i + 1]) for i in range(ns)]), *_mesh_pos())

    outs = _CHAIN.call(
        body, name=name,
        out_shape=[pltpu.HBM(b.shape, b.dtype) for b in bufs],
        in_specs=[HBM_SPEC] * n + [SEM_SPEC] * (2 * ns), out_specs=[HBM_SPEC] * n,
        input_output_aliases={a: a for a in range(n)},
        compiler_params=pltpu.CompilerParams(has_side_effects=SIDE_EFFECT),
    )(*bufs, *flat_sems)
    return list(outs)


def _gather_blocks(land):
    rows = land.shape[1]
    first = (rows // 2) // 16 * 16

    def block(px, py, pc):
        return land.at[_dev_index(px, py, pc)]

    def halves(px, py, pc):
        return (land.at[_dev_index(px, py, pc), pl.ds(0, first)], land.at[_dev_index(px, py, pc), pl.ds(first, rows - first)])

    return block, halves


def _gather_begin(shards):
    me = _dev_index(*_mesh_pos())
    lands = [lax.dynamic_update_index_in_dim(lax.empty((N_DEV,) + s.shape, s.dtype), s, me, 0) for s in shards]
    return dict(lands=lands, stage={})


STAGE_COPIES = (3, 4, 1)


def _gather_step(state, items, name):
    which = sorted({a for a, _ in items})
    at = {a: i for i, a in enumerate(which)}
    sem_sets = [state["stage"][(a, s - 1)][0] for a, s in items if s > 0]
    offset, n_sems = {}, 0
    for a, s in items:
        offset[(a, s)] = n_sems
        n_sems += STAGE_COPIES[s]

    def waits_of(ins, sems):
        x, y, c = _mesh_pos()
        out, earlier = [], 0
        for a, s in items:
            if s == 0:
                continue
            block, halves = _gather_blocks(ins[at[a]])
            send, recv = sems[earlier]
            off = state["stage"][(a, s - 1)][1]
            earlier += 1
            if s == 1:
                arrived = [(1, block(1 - x, y, c)), (2, block(x, 1 - y, c))]
            else:
                arrived = list(zip((2, 3), halves(1 - x, 1 - y, c)))
            out += [("recv", ref, ref, send.at[off + k], recv.at[off + k]) for k, ref in arrived]
        return out

    def copies_of(ins):
        x, y, c = _mesh_pos()
        sibling = (x, y, 1 - c)
        out = []
        for a, s in items:
            block, halves = _gather_blocks(ins[at[a]])
            off = offset[(a, s)]
            if s == 0:
                mine = block(x, y, c)
                out += [(mine, mine, off + 1, (1 - x, y, c)), (mine, mine, off + 2, (x, 1 - y, c)), (mine, mine, off, sibling)]
            elif s == 1:
                from_x, from_y = block(1 - x, y, c), block(x, 1 - y, c)
                out += [(halves(1 - x, y, c)[0], halves(1 - x, y, c)[0], off + 2, (x, 1 - y, c)),
                        (halves(x, 1 - y, c)[1], halves(x, 1 - y, c)[1], off + 3, (1 - x, y, c)),
                        (from_x, from_x, off, sibling), (from_y, from_y, off + 1, sibling)]
            else:
                diag = block(1 - x, 1 - y, c)
                out.append((diag, diag, off, sibling))
        return out

    send_sems, recv_sems, bufs = _split_start([state["lands"][a] for a in which], copies_of, n_sems, name, "near",
                                              sem_sets=sem_sets, waits_of=waits_of)
    for a in which:
        state["lands"][a] = bufs[at[a]]
    for a, s in items:
        state["stage"][(a, s)] = ((send_sems, recv_sems), offset[(a, s)])


def _gather_end(state, which, name):
    sem_sets = [state["stage"][(a, s)][0] for a in which for s in range(3)]

    def waits(ins, sems):
        x, y, c = _mesh_pos()
        out = []
        for i, a in enumerate(which):
            block, halves = _gather_blocks(ins[i])
            (b_send, b_recv), (s1_send, s1_recv), (s2_send, s2_recv) = sems[3 * i:3 * i + 3]
            o0, o1, o2 = (state["stage"][(a, s)][1] for s in range(3))
            arrivals = [(block(x, y, 1 - c), b_send, b_recv, o0),
                        (block(1 - x, y, 1 - c), s1_send, s1_recv, o1), (block(x, 1 - y, 1 - c), s1_send, s1_recv, o1 + 1),
                        (block(1 - x, 1 - y, 1 - c), s2_send, s2_recv, o2)]
            mine = block(x, y, c)
            sent = [(mine, b_send, b_recv, o0 + k) for k in range(3)]
            sent += [(block(1 - x, y, c), s1_send, s1_recv, o1), (block(x, 1 - y, c), s1_send, s1_recv, o1 + 1),
                     (halves(1 - x, y, c)[0], s1_send, s1_recv, o1 + 2), (halves(x, 1 - y, c)[1], s1_send, s1_recv, o1 + 3),
                     (block(1 - x, 1 - y, c), s2_send, s2_recv, o2)]
            out += [("recv", ref, ref, s.at[k], r.at[k]) for ref, s, r, k in arrivals]
            out += [("send", ref, ref, s.at[k], r.at[k]) for ref, s, r, k in sent]
        return out

    bufs = _split_wait([state["lands"][a] for a in which], sem_sets, waits, name)
    for i, a in enumerate(which):
        state["lands"][a] = bufs[i]
    return bufs


def _sibling_exchange_begin(parts, name):
    lands = [lax.empty((4,) + p.shape[1:], p.dtype) for p in parts]
    n = len(parts)

    def copies_of(ins):
        x, y, c = _mesh_pos()
        return [(ins[a].at[2 * j + (1 - c)], ins[n + a].at[j], 4 * a + j, (x, y, 1 - c)) for a in range(n) for j in range(4)]

    send_sems, recv_sems, bufs = _split_start(list(parts) + lands, copies_of, 4 * n, name, "sibling")
    return dict(bufs=bufs, sems=(send_sems, recv_sems), n=n)


def _sibling_exchange_end(state, name):
    n = state["n"]

    def waits(ins, sems):
        _, _, c = _mesh_pos()
        return [(kind, ins[a].at[2 * j + (1 - c)], ins[n + a].at[j], sems[0][0].at[4 * a + j], sems[0][1].at[4 * a + j])
                for a in range(n) for j in range(4) for kind in ("send", "recv")]

    bufs = _split_wait(state["bufs"], [state["sems"]], waits, name)
    return [(bufs[a], bufs[n + a]) for a in range(n)]


CHIP_FLIPS = (2, 1, 3)


def _chip_exchange_begin(csums, name):
    lands = [lax.empty((3,) + s.shape[1:], s.dtype) for s in csums]
    n = len(csums)

    def copies_of(ins):
        x, y, c = _mesh_pos()
        chips = [(1 - x, y), (x, 1 - y), (1 - x, 1 - y)]
        return [(ins[a].at[CHIP_FLIPS[r]], ins[n + a].at[r], 3 * a + r, (px, py, c))
                for a in range(n) for r, (px, py) in enumerate(chips)]

    send_sems, recv_sems, bufs = _split_start(list(csums) + lands, copies_of, 3 * n, name, "chips")
    return dict(bufs=bufs, sems=(send_sems, recv_sems), n=n)


def _chip_exchange_end(state, a, name):
    n = state["n"]

    def waits(ins, sems):
        return [(kind, ins[0].at[CHIP_FLIPS[r]], ins[1].at[r], sems[0][0].at[3 * a + r], sems[0][1].at[3 * a + r])
                for r in range(3) for kind in ("send", "recv")]

    csum, received = _split_wait([state["bufs"][a], state["bufs"][n + a]], [state["sems"]], waits, name)
    return csum, received


def _chip_sum(part, recv, name):
    _, R, C = part.shape
    tr = _tile(R, 1024, 16)
    place = jnp.stack([lax.axis_index("c"), 2 * lax.axis_index("x") + lax.axis_index("y")]).astype(jnp.int32)

    def body(place_ref, p_ref, r_ref, o_ref):
        o_ref[...] = (p_ref[...].astype(F32) + r_ref[...].astype(F32)).astype(o_ref.dtype)

    def chip(p, place_ref):
        return jnp.bitwise_xor(p, place_ref[1])

    grid_spec = pltpu.PrefetchScalarGridSpec(
        num_scalar_prefetch=1, grid=(4, R // tr),
        in_specs=[pl.BlockSpec((None, tr, C), lambda p, i, place_ref: (2 * chip(p, place_ref) + place_ref[0], i, 0)),
                  pl.BlockSpec((None, tr, C), lambda p, i, place_ref: (chip(p, place_ref), i, 0))],
        out_specs=pl.BlockSpec((None, tr, C), lambda p, i, place_ref: (p, i, 0)))
    return pl.pallas_call(body, name=name, grid_spec=grid_spec, out_shape=SDS((4, R, C), part.dtype),
                          compiler_params=_params(2))(place, part, recv)


def _bias_fwd(table_t, onehot_t, onehot_kq_t):
    H = table_t.shape[0]
    n = onehot_t.shape[1]

    def body(t_ref, oh_ref, oh_kq_ref, o_ref, o_kq_ref):
        hi, mid, lo = _split3(t_ref[...])
        for src, dst in ((oh_ref, o_ref), (oh_kq_ref, o_kq_ref)):
            oh = src[...]
            dst[...] = _dot(hi, oh, NN) + _dot(mid, oh, NN) + _dot(lo, oh, NN)

    return _CHAIN.call(body, name="bias_fwd", in_specs=[VMEM_SPEC] * 3, out_specs=[VMEM_SPEC] * 2,
                       out_shape=[SDS((H, n), F32)] * 2, compiler_params=_params(0))(table_t, onehot_t, onehot_kq_t)


def _mix_norm(x, g):
    T, D = x.shape
    tm = _tile(T, 512)

    def body(x_ref, g_ref, n_ref):
        xv = x_ref[...]
        n_ref[...] = (xv * _rms_stats(xv) * g_ref[...]).astype(BF16)

    row = pl.BlockSpec((tm, D), lambda i: (i, 0))
    return _CHAIN.call(body, name="mix_norm", grid=(T // tm,), in_specs=[row, pl.BlockSpec((1, D), lambda i: (0, 0))],
                       out_specs=row, out_shape=SDS((T, D), BF16), compiler_params=_params(1))(x, g)


def _inproj_fwd(n, w_t):
    T, D = n.shape
    P = w_t.shape[0]
    tm = _tile(T, 512)

    def body(n_ref, w_ref, proj_ref):
        proj_ref[...] = _dot(n_ref[...], w_ref[...], NT)

    return _CHAIN.call(
        body, name="inproj_fwd", grid=(T // tm,),
        in_specs=[pl.BlockSpec((tm, D), lambda i: (i, 0)), _resident((P, D))],
        out_specs=pl.BlockSpec((tm, P), lambda i: (i, 0)),
        out_shape=SDS((T, P), F32), compiler_params=_params(1))(n, w_t)


def _layer_norm_group(vg, lg, lb):
    mu = jnp.mean(vg, axis=-1, keepdims=True)
    xc = vg - mu
    rstd = lax.rsqrt(jnp.mean(xc * xc, axis=-1, keepdims=True) + EPS)
    vhat = xc * rstd
    return vhat, rstd, vhat * lg + lb


def _gmlp_fwd(proj, lg, lb, w_s, bs_t, A):
    T = proj.shape[0]
    G = A // GROUP_DIM
    tm = _tile(T, 512)
    nc = tm // CHUNK

    def body(u_ref, v_ref, lg_ref, lb_ref, w_ref, bst_ref, a_ref):
        row = lax.broadcasted_iota(jnp.int32, (CHUNK, CHUNK), 0)
        col = lax.broadcasted_iota(jnp.int32, (CHUNK, CHUNK), 1)
        causal = row >= col
        for g in range(G):
            sl = slice(g * GROUP_DIM, (g + 1) * GROUP_DIM)
            _, _, vn = _layer_norm_group(_gelu(v_ref[:, sl]), lg_ref[:, sl], lb_ref[:, sl])
            vnb = vn.astype(BF16)
            wm = jnp.where(causal, w_ref[g], 0.0).astype(BF16)
            ug = _gelu(u_ref[:, sl])
            bcol = bst_ref[:, g:g + 1]
            for c in range(nc):
                rs = slice(c * CHUNK, (c + 1) * CHUNK)
                a_ref[rs, sl] = ug[rs] * (_dot(wm, vnb[rs], NN) + bcol)

    return _CHAIN.call(
        body, name="gmlp_fwd", grid=(T // tm,),
        in_specs=[pl.BlockSpec((tm, A), lambda i: (i, 0)), pl.BlockSpec((tm, A), lambda i: (i, 1)),
                  pl.BlockSpec((1, A), lambda i: (0, 0)), pl.BlockSpec((1, A), lambda i: (0, 0)),
                  pl.BlockSpec((G, CHUNK, CHUNK), lambda i: (0, 0, 0)), pl.BlockSpec((CHUNK, G), lambda i: (0, 0))],
        out_specs=pl.BlockSpec((tm, A), lambda i: (i, 0)),
        out_shape=SDS((T, A), F32), compiler_params=_params(1))(proj, proj, lg, lb, w_s, bs_t)


def _attn_masks(first_tile):
    ii = lax.broadcasted_iota(jnp.int32, (CHUNK, 2 * CHUNK), 0)
    jj = lax.broadcasted_iota(jnp.int32, (CHUNK, 2 * CHUNK), 1)
    in_window = (jj > ii) & (jj <= ii + CHUNK)
    first_mask = in_window & jnp.logical_or(jnp.logical_not(first_tile), jj >= CHUNK)
    return in_window, first_mask


def _softmax_with_sink(s, sink, axis):
    m = jnp.maximum(jnp.max(s, axis=axis, keepdims=True), sink)
    p = jnp.exp(s - m)
    e_sink = jnp.exp(sink - m)
    inv = 1.0 / (jnp.sum(p, axis=axis, keepdims=True) + e_sink)
    return p * inv, e_sink * inv


def _pad_heads(band, group):
    lane = lax.broadcasted_iota(jnp.int32, band.shape, 1)
    if group == 0:
        low = jnp.where(lane < HEAD_DIM, band, 0.0)
        high = pltpu.roll(low, HEAD_DIM, 1)
    else:
        high = jnp.where(lane >= HEAD_DIM, band, 0.0)
        low = pltpu.roll(high, HEAD_DIM, 1)
    return low.astype(BF16), high.astype(BF16)


def _attn_specs(tq, A, B, reverse_tiles=None):
    nb = tq // CHUNK
    kcol = (2 * A + B) // LANE
    if reverse_tiles is None:
        tile = lambda i: i
    else:
        tile = lambda i: reverse_tiles - 1 - i
    prev = lambda i: jnp.maximum(tile(i) * nb - 1, 0)
    return [pl.BlockSpec((tq, B), lambda i: (tile(i), 2 * A // B)),
            pl.BlockSpec((tq, LANE), lambda i: (tile(i), kcol)),
            pl.BlockSpec((tq, LANE), lambda i: (tile(i), kcol + 1)),
            pl.BlockSpec((CHUNK, LANE), lambda i: (prev(i), kcol)),
            pl.BlockSpec((CHUNK, LANE), lambda i: (prev(i), kcol + 1))]


def _attn_fwd(proj, bias, sinks, A, B):
    T = proj.shape[0]
    H = B // HEAD_DIM
    qpk = H // KV_HEADS
    tq = _tile(T, 512)
    nb = tq // CHUNK

    scale = HEAD_DIM ** -0.5

    def body(sink_ref, q_ref, k_ref, v_ref, kp_ref, vp_ref, bias_ref, o_ref):
        in_window, first_mask = _attn_masks(pl.program_id(0) == 0)
        for b in range(nb):
            rows = slice(b * CHUNK, (b + 1) * CHUNK)
            if b == 0:
                kprev, vprev, mask = kp_ref[...], vp_ref[...], first_mask
            else:
                prows = slice((b - 1) * CHUNK, b * CHUNK)
                kprev, vprev, mask = k_ref[prows, :], v_ref[prows, :], in_window
            kband = jnp.concatenate([kprev, k_ref[rows, :]], axis=0)
            vband = jnp.concatenate([vprev, v_ref[rows, :]], axis=0)
            k_pads = [_pad_heads(kband, g) for g in range(KV_HEADS)]
            v_both = [jnp.concatenate(_pad_heads(vband, g), axis=0) for g in range(KV_HEADS)]
            scores = []
            for pair in range(H // 2):
                h = 2 * pair
                qs = (q_ref[rows, h * HEAD_DIM:(h + 2) * HEAD_DIM] * scale).astype(BF16)
                scores += [_dot(qs, kz, NT) for kz in k_pads[h // qpk]]
            probs = [_softmax_with_sink(jnp.where(mask, s + bias_ref[h], NEG), sink_ref[h], -1)[0].astype(BF16)
                     for h, s in enumerate(scores)]
            outs = [_dot(jnp.concatenate(probs[h:h + 2], axis=1), v_both[h // qpk], NN) for h in range(0, H, 2)]
            o_ref[rows, :] = jnp.concatenate(outs, axis=1)

    return _CHAIN.call(
        body, name="attn_fwd", grid=(T // tq,),
        in_specs=[pl.BlockSpec(memory_space=pltpu.SMEM)] + _attn_specs(tq, A, B)
        + [pl.BlockSpec((H, CHUNK, 2 * CHUNK), lambda i: (0, 0, 0))],
        out_specs=pl.BlockSpec((tq, B), lambda i: (i, 0)),
        out_shape=SDS((T, B), F32), compiler_params=_params(1))(sinks, proj, proj, proj, proj, proj, bias)


def _outproj_fwd(a, b, ga, gb, x, w, g_ffn):
    T, A = a.shape
    B = b.shape[1]
    D = x.shape[1]
    tm = _tile(T, 512)

    def body(a_ref, b_ref, ga_ref, gb_ref, x_ref, w_ref, gf_ref, h_ref, mix_ref, n_ref):
        av, bv = a_ref[...], b_ref[...]
        mix_ref[:, :A] = (av * _rms_stats(av) * ga_ref[...]).astype(BF16)
        mix_ref[:, A:] = (bv * _rms_stats(bv) * gb_ref[...]).astype(BF16)
        hv = x_ref[...] + _dot(mix_ref[...], w_ref[...], NN)
        h_ref[...] = hv
        n_ref[...] = (hv * _rms_stats(hv) * gf_ref[...]).astype(BF16)

    row = pl.BlockSpec((tm, D), lambda i: (i, 0))
    return _CHAIN.call(
        body, name="outproj_fwd", grid=(T // tm,),
        in_specs=[pl.BlockSpec((tm, A), lambda i: (i, 0)), pl.BlockSpec((tm, B), lambda i: (i, 0)),
                  pl.BlockSpec((1, A), lambda i: (0, 0)), pl.BlockSpec((1, B), lambda i: (0, 0)),
                  row, _resident((A + B, D)), pl.BlockSpec((1, D), lambda i: (0, 0))],
        out_specs=[row, pl.BlockSpec((tm, A + B), lambda i: (i, 0)), row],
        out_shape=[SDS((T, D), F32), SDS((T, A + B), BF16), SDS((T, D), BF16)],
        compiler_params=_params(1))(a, b, ga, gb, x, w, g_ffn)


def _ffn_up(n, w_up):
    T, D = n.shape
    Fb = w_up.shape[2]
    F = N_DEV * Fb
    tm, tf = _tile(T, 1024), _tile(Fb, 1024)
    per = Fb // tf

    def body(n_ref, wu_ref, z_ref):
        z_ref[...] = jnp.maximum(_dot(n_ref[...], wu_ref[...], NN), 0.0).astype(BF16)

    return _CHAIN.call(
        body, name="ffn_up", grid=(T // tm, F // tf),
        in_specs=[pl.BlockSpec((tm, D), lambda i, j: (i, 0)),
                  pl.BlockSpec((None, D, tf), lambda i, j: (j // per, 0, j % per))],
        out_specs=pl.BlockSpec((tm, tf), lambda i, j: (i, j)),
        out_shape=SDS((T, F), BF16), compiler_params=_params(2))(n, w_up)


def _ffn_down(h1, z, w_down):
    T, D = h1.shape
    F = w_down.shape[0]
    tm, tn, tk = _tile(T, 1024), _tile(D, 1024), _tile(F, 4096)

    def body(h_ref, z_ref, wd_ref, h2_ref):
        k = pl.program_id(2)

        @pl.when(k == 0)
        def _():
            h2_ref[...] = h_ref[...]

        zf = z_ref[...].astype(F32)
        h2_ref[...] += _dot((zf * zf).astype(BF16), wd_ref[...], NN)

    return _CHAIN.call(
        body, name="ffn_down", grid=(T // tm, D // tn, F // tk),
        in_specs=[pl.BlockSpec((tm, tn), lambda i, j, k: (i, j)), pl.BlockSpec((tm, tk), lambda i, j, k: (i, k)),
                  pl.BlockSpec((tk, tn), lambda i, j, k: (k, j))],
        out_specs=pl.BlockSpec((tm, tn), lambda i, j, k: (i, j)),
        out_shape=SDS((T, D), F32), compiler_params=_params(3))(h1, z, w_down)


def _final_loss(h2, g, target):
    T, D = h2.shape
    tm = _tile(T, 512)

    def body(h_ref, g_ref, t_ref, loss_ref, dg_ref, dh_ref, dhb_ref):
        @pl.when(pl.program_id(0) == 0)
        def _():
            loss_ref[...] = jnp.zeros_like(loss_ref)
            dg_ref[...] = jnp.zeros_like(dg_ref)

        hv, gv = h_ref[...], g_ref[...]
        r = _rms_stats(hv)
        hn = hv * r
        e = hn * gv - t_ref[...]
        loss_ref[...] += (0.5 / D) * jnp.sum(jnp.sum(e * e, axis=0, keepdims=True), axis=-1, keepdims=True)
        dy = e * (1.0 / D)
        dg_ref[...] += jnp.sum(dy * hn, axis=0, keepdims=True)
        dh = _rms_bwd(dy, hv, r, gv)
        dh_ref[...] = dh
        dhb_ref[...] = dh.astype(BF16)

    return _CHAIN.call(
        body, name="final_loss", grid=(T // tm,),
        in_specs=[pl.BlockSpec((tm, D), lambda i: (i, 0)), pl.BlockSpec((1, D), lambda i: (0, 0)),
                  pl.BlockSpec((tm, D), lambda i: (i, 0))],
        out_specs=[pl.BlockSpec((1, 1), lambda i: (0, 0)), pl.BlockSpec((1, D), lambda i: (0, 0)),
                   pl.BlockSpec((tm, D), lambda i: (i, 0)), pl.BlockSpec((tm, D), lambda i: (i, 0))],
        out_shape=[SDS((1, 1), F32), SDS((1, D), F32), SDS((T, D), F32), SDS((T, D), BF16)],
        compiler_params=_params(1))(h2, g, target)


def _ffn_down_bwd(dh2b, z, w_down):
    T, D = dh2b.shape
    F = w_down.shape[0]
    tm, tf = _tile(T, 1024), _tile(F, 1024)

    def body(dh_ref, z_ref, wd_ref, dzp_ref):
        dzz = _dot(dh_ref[...], wd_ref[...], NT)
        dzp_ref[...] = (dzz * (2.0 * z_ref[...].astype(F32))).astype(BF16)

    return _CHAIN.call(
        body, name="ffn_down_bwd", grid=(T // tm, F // tf),
        in_specs=[pl.BlockSpec((tm, D), lambda i, j: (i, 0)), pl.BlockSpec((tm, tf), lambda i, j: (i, j)),
                  pl.BlockSpec((tf, D), lambda i, j: (j, 0))],
        out_specs=pl.BlockSpec((tm, tf), lambda i, j: (i, j)),
        out_shape=SDS((T, F), BF16), compiler_params=_params(2))(dh2b, z, w_down)


def _ffn_up_bwd(dzp, w_up_t):
    T, F = dzp.shape
    D = w_up_t.shape[1]
    tm, tn, tk = _tile(T, 1024), _tile(D, 1024), _tile(F, 4096)

    def body(dzp_ref, w_ref, dn_ref):
        part = _dot(dzp_ref[...], w_ref[...], NN)

        @pl.when(pl.program_id(2) == 0)
        def _():
            dn_ref[...] = part

        @pl.when(pl.program_id(2) > 0)
        def _():
            dn_ref[...] += part

    return _CHAIN.call(
        body, name="ffn_up_bwd", grid=(T // tm, D // tn, F // tk),
        in_specs=[pl.BlockSpec((tm, tk), lambda i, j, k: (i, k)), pl.BlockSpec((tk, tn), lambda i, j, k: (k, j))],
        out_specs=pl.BlockSpec((tm, tn), lambda i, j, k: (i, j)),
        out_shape=SDS((T, D), F32), compiler_params=_params(3))(dzp, w_up_t)


def _ffn_norm_bwd(dn, dh2, h1, g):
    T, D = h1.shape
    tm = _tile(T, 512)

    def body(dn_ref, dh_ref, h_ref, g_ref, dh1_ref, dh1b_ref, dg_ref):
        @pl.when(pl.program_id(0) == 0)
        def _():
            dg_ref[...] = jnp.zeros_like(dg_ref)

        hv, dnv = h_ref[...], dn_ref[...]
        r = _rms_stats(hv)
        dg_ref[...] += jnp.sum(dnv * (hv * r), axis=0, keepdims=True)
        dh1 = dh_ref[...] + _rms_bwd(dnv, hv, r, g_ref[...])
        dh1_ref[...] = dh1
        dh1b_ref[...] = dh1.astype(BF16)

    row = pl.BlockSpec((tm, D), lambda i: (i, 0))
    vec = pl.BlockSpec((1, D), lambda i: (0, 0))
    return _CHAIN.call(
        body, name="ffn_norm_bwd", grid=(T // tm,), in_specs=[row, row, row, vec], out_specs=[row, row, vec],
        out_shape=[SDS((T, D), F32), SDS((T, D), BF16), SDS((1, D), F32)], compiler_params=_params(1))(dn, dh2, h1, g)


def _matmul_tn(a, b, name, square_a=False, col_blocks=None):
    T, K = a.shape
    N = b.shape[1]
    tk = _tile(K, 1792)
    tn = _tile(N if col_blocks is None else N // col_blocks, 1024 if tk <= 1024 else 512)

    def body(a_ref, b_ref, o_ref):
        av = a_ref[...]
        if square_a:
            af = av.astype(F32)
            av = (af * af).astype(BF16)
        o_ref[...] = _dot(av, b_ref[...], TN).astype(o_ref.dtype)

    if col_blocks is None:
        out_shape = SDS((K, N), BF16)
        out_spec = pl.BlockSpec((tk, tn), lambda i, j: (i, j))
    else:
        per = (N // col_blocks) // tn
        out_shape = SDS((col_blocks, K, N // col_blocks), BF16)
        out_spec = pl.BlockSpec((None, tk, tn), lambda i, j: (j // per, i, j % per))
    return _CHAIN.call(
        body, name=name, grid=(K // tk, N // tn),
        in_specs=[pl.BlockSpec((T, tk), lambda i, j: (0, i)), pl.BlockSpec((T, tn), lambda i, j: (0, j))],
        out_specs=out_spec, out_shape=out_shape, compiler_params=_params(2))(a, b)


def _outproj_bwd(dh1b, w, a, b, ga, gb):
    T, D = dh1b.shape
    A, B = a.shape[1], b.shape[1]
    tm = _tile(T, 512)

    def body(dh_ref, w_ref, a_ref, b_ref, ga_ref, gb_ref, da_ref, db_ref, dga_ref, dgb_ref):
        @pl.when(pl.program_id(0) == 0)
        def _():
            dga_ref[...] = jnp.zeros_like(dga_ref)
            dgb_ref[...] = jnp.zeros_like(dgb_ref)

        dmix = _dot(dh_ref[...], w_ref[...], NT)
        for src_ref, g_ref, dx_ref, dg_ref, dn in ((a_ref, ga_ref, da_ref, dga_ref, dmix[:, :A]),
                                                   (b_ref, gb_ref, db_ref, dgb_ref, dmix[:, A:])):
            xv = src_ref[...]
            r = _rms_stats(xv)
            dg_ref[...] += jnp.sum(dn * (xv * r), axis=0, keepdims=True)
            dx_ref[...] = _rms_bwd(dn, xv, r, g_ref[...])

    return _CHAIN.call(
        body, name="outproj_bwd", grid=(T // tm,),
        in_specs=[pl.BlockSpec((tm, D), lambda i: (i, 0)), _resident((A + B, D)),
                  pl.BlockSpec((tm, A), lambda i: (i, 0)), pl.BlockSpec((tm, B), lambda i: (i, 0)),
                  pl.BlockSpec((1, A), lambda i: (0, 0)), pl.BlockSpec((1, B), lambda i: (0, 0))],
        out_specs=[pl.BlockSpec((tm, A), lambda i: (i, 0)), pl.BlockSpec((tm, B), lambda i: (i, 0)),
                   pl.BlockSpec((1, A), lambda i: (0, 0)), pl.BlockSpec((1, B), lambda i: (0, 0))],
        out_shape=[SDS((T, A), F32), SDS((T, B), F32), SDS((1, A), F32), SDS((1, B), F32)],
        compiler_params=_params(1))(dh1b, w, a, b, ga, gb)


def _gmlp_bwd(proj, da, lg, lb, w_s, w_st, bs_t, A):
    T = proj.shape[0]
    G = A // GROUP_DIM
    tm = _tile(T, 512)
    nc = tm // CHUNK

    def body(u_ref, v_ref, da_ref, lg_ref, lb_ref, w_ref, wt_ref, bst_ref, duv_ref, dlg_ref, dlb_ref, dw_ref, dbs_ref):
        @pl.when(pl.program_id(0) == 0)
        def _():
            dlg_ref[...] = jnp.zeros_like(dlg_ref)
            dlb_ref[...] = jnp.zeros_like(dlb_ref)
            dw_ref[...] = jnp.zeros_like(dw_ref)
            dbs_ref[...] = jnp.zeros_like(dbs_ref)

        row = lax.broadcasted_iota(jnp.int32, (CHUNK, CHUNK), 0)
        col = lax.broadcasted_iota(jnp.int32, (CHUNK, CHUNK), 1)
        lower = row >= col
        upper = row <= col
        for g in range(G):
            sl = slice(g * GROUP_DIM, (g + 1) * GROUP_DIM)
            lgv = lg_ref[:, sl]
            vg, vg_grad = _gelu_and_grad(v_ref[:, sl])
            vhat, rstd, vn = _layer_norm_group(vg, lgv, lb_ref[:, sl])
            vnb = vn.astype(BF16)
            ug, ug_grad = _gelu_and_grad(u_ref[:, sl])
            dav = da_ref[:, sl]
            wm = jnp.where(lower, w_ref[g], 0.0).astype(BF16)
            wmt = jnp.where(upper, wt_ref[g], 0.0).astype(BF16)
            bcol = bst_ref[:, g:g + 1]
            dw_acc = jnp.zeros((CHUNK, CHUNK), F32)
            dbs_acc = jnp.zeros((CHUNK, 1), F32)
            dvn_parts = []
            dug_parts = []
            for c in range(nc):
                rs = slice(c * CHUNK, (c + 1) * CHUNK)
                mixed = _dot(wm, vnb[rs], NN) + bcol
                dug_parts.append(dav[rs] * mixed)
                dmix = dav[rs] * ug[rs]
                dbs_acc = dbs_acc + jnp.sum(dmix, axis=-1, keepdims=True)
                dmixb = dmix.astype(BF16)
                dw_acc = dw_acc + _dot(dmixb, vnb[rs], NT)
                dvn_parts.append(_dot(wmt, dmixb, NN))
            dvn = jnp.concatenate(dvn_parts, axis=0)
            dug = jnp.concatenate(dug_parts, axis=0)
            dw_ref[g] += jnp.where(lower, dw_acc, 0.0)
            dbs_ref[:, g:g + 1] += dbs_acc
            dlg_ref[:, sl] += jnp.sum(dvn * vhat, axis=0, keepdims=True)
            dlb_ref[:, sl] += jnp.sum(dvn, axis=0, keepdims=True)
            dvhat = dvn * lgv
            dvg = rstd * (dvhat - jnp.mean(dvhat, axis=-1, keepdims=True)
                          - vhat * jnp.mean(dvhat * vhat, axis=-1, keepdims=True))
            duv_ref[:, sl] = (dug * ug_grad).astype(BF16)
            duv_ref[:, A + g * GROUP_DIM:A + (g + 1) * GROUP_DIM] = (dvg * vg_grad).astype(BF16)

    return _CHAIN.call(
        body, name="gmlp_bwd", grid=(T // tm,),
        in_specs=[pl.BlockSpec((tm, A), lambda i: (i, 0)), pl.BlockSpec((tm, A), lambda i: (i, 1)),
                  pl.BlockSpec((tm, A), lambda i: (i, 0)),
                  pl.BlockSpec((1, A), lambda i: (0, 0)), pl.BlockSpec((1, A), lambda i: (0, 0)),
                  pl.BlockSpec((G, CHUNK, CHUNK), lambda i: (0, 0, 0)),
                  pl.BlockSpec((G, CHUNK, CHUNK), lambda i: (0, 0, 0)), pl.BlockSpec((CHUNK, G), lambda i: (0, 0))],
        out_specs=[pl.BlockSpec((tm, 2 * A), lambda i: (i, 0)),
                   pl.BlockSpec((1, A), lambda i: (0, 0)), pl.BlockSpec((1, A), lambda i: (0, 0)),
                   pl.BlockSpec((G, CHUNK, CHUNK), lambda i: (0, 0, 0)), pl.BlockSpec((CHUNK, G), lambda i: (0, 0))],
        out_shape=[SDS((T, 2 * A), BF16), SDS((1, A), F32), SDS((1, A), F32),
                   SDS((G, CHUNK, CHUNK), F32), SDS((CHUNK, G), F32)],
        compiler_params=_params(1))(proj, proj, da, lg, lb, w_s, w_st, bs_t)


def _attn_bwd(proj, do, duv, bias_t, sinks, A, B):
    T, P = proj.shape
    H = B // HEAD_DIM
    qpk = H // KV_HEADS
    tq = _tile(T, 512)
    nb = tq // CHUNK
    n_tiles = T // tq
    scale = HEAD_DIM ** -0.5
    rev = lambda i: n_tiles - 1 - i

    def body(sink_ref, q_ref, k_ref, v_ref, kp_ref, vp_ref, do_ref, duv_ref, bias_ref,
             dproj_ref, dbias_ref, dsink_ref, carry, dkv, sacc):
        step = pl.program_id(0)

        @pl.when(step == 0)
        def _():
            carry[...] = jnp.zeros_like(carry)
            sacc[...] = jnp.zeros_like(sacc)
            dbias_ref[...] = jnp.zeros_like(dbias_ref)

        jj = lax.broadcasted_iota(jnp.int32, (2 * CHUNK, CHUNK), 0)
        ii = lax.broadcasted_iota(jnp.int32, (2 * CHUNK, CHUNK), 1)
        in_window = (jj > ii) & (jj <= ii + CHUNK)
        first_mask = in_window & jnp.logical_or(step != n_tiles - 1, jj >= CHUNK)
        low_query = lax.broadcasted_iota(jnp.int32, (CHUNK, LANE), 1) < HEAD_DIM
        low_key = lax.broadcasted_iota(jnp.int32, (2 * CHUNK, LANE), 1) < HEAD_DIM

        def split_pair(pair_bf16):
            zero = jnp.zeros_like(pair_bf16)
            return jnp.concatenate([jnp.where(low_query, pair_bf16, zero), jnp.where(low_query, zero, pair_bf16)], axis=0)

        dproj_ref[:, :2 * A] = duv_ref[...]
        dkv[...] = jnp.zeros_like(dkv)
        for b in range(nb):
            rows = slice(b * CHUNK, (b + 1) * CHUNK)
            band = slice(b * CHUNK, (b + 2) * CHUNK)
            if b == 0:
                kprev, vprev, mask = kp_ref[...], vp_ref[...], first_mask
            else:
                prows = slice((b - 1) * CHUNK, b * CHUNK)
                kprev, vprev, mask = k_ref[prows, :], v_ref[prows, :], in_window
            kband = jnp.concatenate([kprev, k_ref[rows, :]], axis=0)
            vband = jnp.concatenate([vprev, v_ref[rows, :]], axis=0)
            k_pads = [_pad_heads(kband, g) for g in range(KV_HEADS)]
            v_pads = [_pad_heads(vband, g) for g in range(KV_HEADS)]
            queries, douts, scores, dprobs = [], [], [], []
            for pair in range(H // 2):
                cols = slice(2 * pair * HEAD_DIM, (2 * pair + 2) * HEAD_DIM)
                qs = (q_ref[rows, cols] * scale).astype(BF16)
                dob = do_ref[rows, cols].astype(BF16)
                queries.append(qs)
                douts.append(dob)
                scores += [_dot(kz, qs, NT) for kz in k_pads[2 * pair // qpk]]
                dprobs += [_dot(vz, dob, NT) for vz in v_pads[2 * pair // qpk]]
            probs, dscores = [], []
            for h in range(H):
                pt, p_sink = _softmax_with_sink(jnp.where(mask, scores[h] + bias_ref[h], NEG), sink_ref[h], 0)
                delta = jnp.sum(pt * dprobs[h], axis=0, keepdims=True)
                dst = pt * (dprobs[h] - delta)
                dbias_ref[h] += dst
                sacc[h:h + 1, :] += -(p_sink * delta)
                probs.append(pt.astype(BF16))
                dscores.append(dst.astype(BF16))
            dq_parts, dk_groups, dv_groups = [], [], []
            for g in range(KV_HEADS):
                k_both = jnp.concatenate(k_pads[g], axis=0)
                dk_acc = jnp.zeros((2 * CHUNK, LANE), F32)
                dv_acc = jnp.zeros((2 * CHUNK, LANE), F32)
                for pair in range(g * qpk // 2, (g + 1) * qpk // 2):
                    pair_heads = slice(2 * pair, 2 * pair + 2)
                    dk_acc = dk_acc + _dot(jnp.concatenate(dscores[pair_heads], axis=1), split_pair(queries[pair]), NN)
                    dv_acc = dv_acc + _dot(jnp.concatenate(probs[pair_heads], axis=1), split_pair(douts[pair]), NN)
                    dq_parts.append(_dot(jnp.concatenate(dscores[pair_heads], axis=0), k_both, TN) * scale)
                dk_groups.append(dk_acc + pltpu.roll(dk_acc, HEAD_DIM, 1))
                dv_groups.append(dv_acc + pltpu.roll(dv_acc, HEAD_DIM, 1))
            dkv[band, :LANE] += jnp.where(low_key, dk_groups[0], dk_groups[1])
            dkv[band, LANE:] += jnp.where(low_key, dv_groups[0], dv_groups[1])
            dproj_ref[rows, 2 * A:2 * A + B] = jnp.concatenate(dq_parts, axis=1).astype(BF16)
        last = slice(tq, tq + CHUNK)
        dkv[last, :] += carry[...]
        dproj_ref[:, 2 * A + B:] = dkv[CHUNK:, :].astype(BF16)
        carry[...] = dkv[:CHUNK, :]

        @pl.when(step == n_tiles - 1)
        def _():
            dsink_ref[...] = jnp.sum(sacc[...], axis=1, keepdims=True)

    specs = _attn_specs(tq, A, B, reverse_tiles=n_tiles)
    return _CHAIN.call(
        body, name="attn_bwd", grid=(n_tiles,),
        in_specs=[pl.BlockSpec(memory_space=pltpu.SMEM)] + specs
        + [pl.BlockSpec((tq, B), lambda i: (rev(i), 0)), pl.BlockSpec((tq, 2 * A), lambda i: (rev(i), 0)),
           pl.BlockSpec((H, 2 * CHUNK, CHUNK), lambda i: (0, 0, 0))],
        out_specs=[pl.BlockSpec((tq, P), lambda i: (rev(i), 0)),
                   pl.BlockSpec((H, 2 * CHUNK, CHUNK), lambda i: (0, 0, 0)), pl.BlockSpec((H, 1), lambda i: (0, 0))],
        out_shape=[SDS((T, P), BF16), SDS((H, 2 * CHUNK, CHUNK), F32), SDS((H, 1), F32)],
        scratch_shapes=[pltpu.VMEM((CHUNK, 2 * LANE), F32), pltpu.VMEM((tq + CHUNK, 2 * LANE), F32),
                        pltpu.VMEM((H, LANE), F32)],
        compiler_params=_params(1))(sinks, proj, proj, proj, proj, proj, do, duv, bias_t)


def _bias_bwd(dbias, onehot):
    H = dbias.shape[0]
    nbk = onehot.shape[1]

    def body(d_ref, oh_ref, o_ref):
        hi, mid, lo = _split3(d_ref[...])
        oh = oh_ref[...]
        o_ref[...] = _dot(hi, oh, NN) + _dot(mid, oh, NN) + _dot(lo, oh, NN)

    return _CHAIN.call(body, name="bias_bwd", in_specs=[VMEM_SPEC] * 2, out_specs=VMEM_SPEC, out_shape=SDS((H, nbk), F32),
                       compiler_params=_params(0))(dbias, onehot)


def _inproj_bwd(dproj, w_t, x, dh1, g):
    T, P = dproj.shape
    D = x.shape[1]
    tm = _tile(T, 512)

    def body(dp_ref, w_ref, x_ref, dh_ref, g_ref, dx_ref, dg_ref):
        @pl.when(pl.program_id(0) == 0)
        def _():
            dg_ref[...] = jnp.zeros_like(dg_ref)

        dn = _dot(dp_ref[...], w_ref[...], NN)
        xv = x_ref[...]
        r = _rms_stats(xv)
        dg_ref[...] += jnp.sum(dn * (xv * r), axis=0, keepdims=True)
        dx_ref[...] = dh_ref[...] + _rms_bwd(dn, xv, r, g_ref[...])

    return _CHAIN.call(
        body, name="inproj_bwd", grid=(T // tm,),
        in_specs=[pl.BlockSpec((tm, P), lambda i: (i, 0)), _resident((P, D)),
                  pl.BlockSpec((tm, D), lambda i: (i, 0)), pl.BlockSpec((tm, D), lambda i: (i, 0)),
                  pl.BlockSpec((1, D), lambda i: (0, 0))],
        out_specs=[pl.BlockSpec((tm, D), lambda i: (i, 0)), pl.BlockSpec((1, D), lambda i: (0, 0))],
        out_shape=[SDS((T, D), F32), SDS((1, D), F32)], compiler_params=_params(1))(dproj, w_t, x, dh1, g)


def _adamw(w, g, m, v):
    m = ADAM_B1 * m + (1.0 - ADAM_B1) * g
    v = ADAM_B2 * v + (1.0 - ADAM_B2) * (g * g)
    m_hat = m / (1.0 - ADAM_B1 ** ADAM_STEP)
    v_hat = v / (1.0 - ADAM_B2 ** ADAM_STEP)
    delta = -ADAM_LR * (m_hat / (jnp.sqrt(v_hat) + ADAM_EPS) + ADAM_WD * w)
    return delta, m, v


def _adam_sharded(csum, recv, w, m, v, name):
    R, C = w.shape
    tr = _tile(R, 256, 16)

    def body(own_ref, recv_ref, w_ref, m_ref, v_ref, g_ref, d_ref, nm_ref, nv_ref):
        g = own_ref[...].astype(F32)
        for r in range(3):
            g = g + recv_ref[r].astype(F32)
        delta, nm, nv = _adamw(w_ref[...], g, m_ref[...], v_ref[...])
        g_ref[...] = g
        d_ref[...] = delta
        nm_ref[...] = nm
        nv_ref[...] = nv

    blk = pl.BlockSpec((tr, C), lambda i: (i, 0))
    return _CHAIN.call(
        body, name=name, grid=(R // tr,),
        in_specs=[pl.BlockSpec((None, tr, C), lambda i: (0, i, 0)), pl.BlockSpec((3, tr, C), lambda i: (0, i, 0)),
                  blk, blk, blk],
        out_specs=[blk] * 4, out_shape=[SDS((R, C), F32)] * 4, compiler_params=_params(1))(csum, recv, w, m, v)


def _rows2d(shape):
    return (int(np.prod(shape[:-1])) if len(shape) > 1 else 1, shape[-1])


def _small_layout(shapes):
    totals, places = {}, []
    for s in shapes:
        r, w = _rows2d(s)
        off = totals.get(w, 0)
        places.append((w, off, r))
        totals[w] = off + -(-r // 8) * 8
    return {w: -(-t // 32) * 32 for w, t in totals.items()}, places


def _pack_small(arrays, totals, places):
    bufs = []
    for w, total in totals.items():
        buf = jnp.zeros((total, w), F32)
        for a, (pw, off, r) in zip(arrays, places):
            if pw == w:
                buf = lax.dynamic_update_slice(buf, a.reshape(r, w).astype(F32), (off, 0))
        bufs.append(buf)
    return bufs


def _adam_small(gathered, totals, places, ws, ms, vs):
    widths = list(totals)
    n, nw = len(places), len(widths)

    def body(*refs):
        gath, params, outs = refs[:nw], refs[nw:nw + 3 * n], refs[nw + 3 * n:]
        for p, (w, off, r) in enumerate(places):
            g_ref = gath[widths.index(w)]
            g = g_ref[0, off:off + r, :]
            for d in range(1, N_DEV):
                g = g + g_ref[d, off:off + r, :]
            delta, nm, nv = _adamw(params[p][...], g, params[n + p][...], params[2 * n + p][...])
            for k, val in enumerate((g, delta, nm, nv)):
                outs[4 * p + k][...] = val

    shapes2d = [SDS((r, w), F32) for w, _, r in places for _ in range(4)]
    outs = _CHAIN.call(body, name="adam_small", in_specs=[VMEM_SPEC] * (nw + 3 * n), out_specs=[VMEM_SPEC] * (4 * n),
                       out_shape=shapes2d, compiler_params=_params(0))(*gathered, *ws, *ms, *vs)
    return [outs[4 * p:4 * p + 4] for p in range(n)]


def kernel(x, rel_bias_table, mix_norm_g, w_in, gate_norm_g, gate_norm_b, w_spatial, b_spatial, attn_sinks, out_norm_a_g, out_norm_b_g, w_out, ffn_norm_g, w_up, w_down, final_norm_g, loss_target, m_rel_bias_table, m_mix_norm_g, m_w_in, m_gate_norm_g, m_gate_norm_b, m_w_spatial, m_b_spatial, m_attn_sinks, m_out_norm_a_g, m_out_norm_b_g, m_w_out, m_ffn_norm_g, m_w_up, m_w_down, m_final_norm_g, v_rel_bias_table, v_mix_norm_g, v_w_in, v_gate_norm_g, v_gate_norm_b, v_w_spatial, v_b_spatial, v_attn_sinks, v_out_norm_a_g, v_out_norm_b_g, v_w_out, v_ffn_norm_g, v_w_up, v_w_down, v_final_norm_g):
    T, D = x.shape[1], x.shape[2]
    A = D // 2
    B = D // 2
    H = B // HEAD_DIM
    P = 2 * A + B + 2 * KV_HEADS * HEAD_DIM
    xs = x.reshape(T, D)
    target = loss_target.reshape(T, D)

    win_t, m_win_t, v_win_t = (jnp.swapaxes(a[0], 0, 1) for a in (w_in, m_w_in, v_w_in))
    shards = [win_t.astype(BF16), w_out[0].astype(BF16), w_up[0].astype(BF16), w_down[0].astype(BF16)]
    _CHAIN.token = None
    _handshakes.clear()
    gather = _gather_begin(shards)
    _gather_step(gather, [(0, 0)], "gather_start")

    g1, g2, g3 = mix_norm_g.reshape(1, D), ffn_norm_g.reshape(1, D), final_norm_g.reshape(1, D)
    lg, lb = gate_norm_g.reshape(1, A), gate_norm_b.reshape(1, A)
    ws = w_spatial[0]
    ws_t = jnp.swapaxes(ws, 1, 2)
    bs_t = jnp.transpose(b_spatial[0])
    ga, gb = out_norm_a_g.reshape(1, A), out_norm_b_g.reshape(1, B)
    sinks = attn_sinks.reshape(H)
    bucket, in_window = _t5_bucket()
    onehot_np = ((bucket[:, :, None] == np.arange(N_BUCKETS)) & in_window[:, :, None]).astype(np.float32)
    onehot = jnp.asarray(onehot_np.reshape(-1, N_BUCKETS)).astype(BF16)
    onehot_kq = jnp.asarray(onehot_np.transpose(1, 0, 2).reshape(-1, N_BUCKETS)).astype(BF16)

    bias, bias_t = _bias_fwd(jnp.transpose(rel_bias_table), jnp.transpose(onehot), jnp.transpose(onehot_kq))
    bias, bias_t = bias.reshape(H, CHUNK, 2 * CHUNK), bias_t.reshape(H, 2 * CHUNK, CHUNK)
    n1 = _mix_norm(xs, g1)
    _gather_step(gather, [(0, 1), (1, 0), (2, 0)], "gather_in_1")
    _gather_step(gather, [(0, 2)], "gather_in_2")
    (win_g,) = _gather_end(gather, [0], "gather_in_end")
    win_t_full = win_g.reshape(P, D)
    proj = _inproj_fwd(n1, win_t_full)
    _gather_step(gather, [(1, 1)], "gather_out_1")
    a_out = _gmlp_fwd(proj, lg, lb, ws, bs_t, A)
    _gather_step(gather, [(1, 2), (2, 1), (3, 0)], "gather_out_2_up_1")
    b_out = _attn_fwd(proj, bias, sinks, A, B)
    (wout_g,) = _gather_end(gather, [1], "gather_out_end")
    _gather_step(gather, [(2, 2)], "gather_up_2")
    wout_full = wout_g.reshape(A + B, D)
    h1, mixed, n2 = _outproj_fwd(a_out, b_out, ga, gb, xs, wout_full, g2)
    (wup_g,) = _gather_end(gather, [2], "gather_up_end")
    _gather_step(gather, [(3, 1)], "gather_down_1")
    wup_t = jnp.transpose(wup_g, (0, 2, 1)).reshape(-1, D)
    z = _ffn_up(n2, wup_g)
    _gather_step(gather, [(3, 2)], "gather_down_2")
    (wdown_g,) = _gather_end(gather, [3], "gather_down_end")
    h2 = _ffn_down(h1, z, wdown_g.reshape(-1, D))
    loss_part, dg3, dh2, dh2b = _final_loss(h2, g3, target)

    def reduce_to_chip(state, name):
        csums = [_chip_sum(part, received, "%s_chip_sum_%d" % (name, a))
                 for a, (part, received) in enumerate(_sibling_exchange_end(state, name + "_sib_end"))]
        return _chip_exchange_begin(csums, name + "_chip")

    dwdown = _matmul_tn(z, dh2b, "grad_w_down", square_a=True).reshape(wdown_g.shape)
    dzp = _ffn_down_bwd(dh2b, z, wdown_g.reshape(-1, D))
    dwup = _matmul_tn(n2, dzp, "grad_w_up", col_blocks=N_DEV)
    sib_ffn = _sibling_exchange_begin([dwdown, dwup], "rs_ffn_sib")
    dh1, dh1b, dg2 = _ffn_norm_bwd(_ffn_up_bwd(dzp, wup_t), dh2, h1, g2)
    chip_ffn = reduce_to_chip(sib_ffn, "rs_ffn")
    da, db, dga, dgb = _outproj_bwd(dh1b, wout_full, a_out, b_out, ga, gb)
    dwout = _matmul_tn(mixed, dh1b, "grad_w_out").reshape(wout_g.shape)
    sib_out = _sibling_exchange_begin([dwout], "rs_out_sib")
    duv, dlg, dlb, dws, dbs_t = _gmlp_bwd(proj, da, lg, lb, ws, ws_t, bs_t, A)
    dproj, dbias_t, dsinks = _attn_bwd(proj, db, duv, bias_t, sinks, A, B)
    chip_out = reduce_to_chip(sib_out, "rs_out")
    dwin_t = _matmul_tn(dproj, n1, "grad_w_in").reshape(win_g.shape)
    sib_in = _sibling_exchange_begin([dwin_t], "rs_in_sib")
    dtable_t = _bias_bwd(dbias_t.reshape(H, -1), onehot_kq)
    chip_in = reduce_to_chip(sib_in, "rs_in")
    grad_x, dg1 = _inproj_bwd(dproj, win_t_full, xs, dh1, g1)

    small_w = [rel_bias_table, mix_norm_g, gate_norm_g, gate_norm_b, w_spatial, b_spatial, attn_sinks,
               out_norm_a_g, out_norm_b_g, ffn_norm_g, final_norm_g]
    small_m = [m_rel_bias_table, m_mix_norm_g, m_gate_norm_g, m_gate_norm_b, m_w_spatial, m_b_spatial, m_attn_sinks,
               m_out_norm_a_g, m_out_norm_b_g, m_ffn_norm_g, m_final_norm_g]
    small_v = [v_rel_bias_table, v_mix_norm_g, v_gate_norm_g, v_gate_norm_b, v_w_spatial, v_b_spatial, v_attn_sinks,
               v_out_norm_a_g, v_out_norm_b_g, v_ffn_norm_g, v_final_norm_g]
    small_g = [jnp.transpose(dtable_t), dg1, dlg, dlb, dws, jnp.transpose(dbs_t), dsinks, dga, dgb, dg2, dg3]
    nothing = jnp.zeros((1, H), F32)
    small_w, small_m, small_v = small_w + [nothing], small_m + [nothing], small_v + [nothing]
    small_g = small_g + [jnp.broadcast_to(loss_part, (1, H))]
    shapes = [w.shape for w in small_w]
    totals, places = _small_layout(shapes)
    as_rows = lambda arrays: [a.reshape(_rows2d(a.shape)) for a in arrays]
    big = [None] * 4

    def adam_of(k, state, a, w, m, v):
        csum, received = _chip_exchange_end(state, a, "rs_%d_end" % k)
        big[k] = _adam_sharded(csum, received, w, m, v, "adam_%d" % k)

    small_gather = _gather_begin(_pack_small(small_g, totals, places))
    every = range(len(totals))
    _gather_step(small_gather, [(a, 0) for a in every], "small_gather_start")
    adam_of(3, chip_ffn, 0, w_down[0], m_w_down[0], v_w_down[0])
    _gather_step(small_gather, [(a, 1) for a in every], "small_gather_1")
    adam_of(2, chip_ffn, 1, w_up[0], m_w_up[0], v_w_up[0])
    _gather_step(small_gather, [(a, 2) for a in every], "small_gather_2")
    adam_of(1, chip_out, 0, w_out[0], m_w_out[0], v_w_out[0])
    adam_of(0, chip_in, 0, win_t, m_win_t, v_win_t)
    gathered = _gather_end(small_gather, list(every), "small_gather_end")
    small_out = _adam_small(gathered, totals, places, as_rows(small_w), as_rows(small_m), as_rows(small_v))
    sg, sd, sm, sv = [[outs[k].reshape(s) for outs, s in zip(small_out, shapes)] for k in range(4)]
    big[0] = [jnp.swapaxes(o, 0, 1) for o in big[0]]
    big = [[o.reshape(w.shape) for o in outs] for outs, w in zip(big, (w_in, w_out, w_up, w_down))]

    loss = sg[-1][0, 0]

    order = ["s0", "s1", "b0", "s2", "s3", "s4", "s5", "s6", "s7", "s8", "b1", "s9", "b2", "b3", "s10"]

    def group(idx):
        small = (sg, sd, sm, sv)[idx]
        return [small[int(t[1:])] if t[0] == "s" else big[int(t[1:])][idx] for t in order]

    return (loss, grad_x.reshape(x.shape), *group(0), *group(1), *group(2), *group(3))
```

```python
import math

import numpy as np
import jax
import jax.numpy as jnp
from jax import lax
from jax.experimental import pallas as pl
from jax.experimental.pallas import tpu as pltpu

F32 = jnp.float32
BF16 = jnp.bfloat16
SDS = jax.ShapeDtypeStruct
MESH = pl.DeviceIdType.MESH

N_DEV = 8
EPS = 1e-5
NEG = -1e30
CHUNK = 128
GROUP_DIM = 128
HEAD_DIM = 64
KV_HEADS = 2
N_BUCKETS = 32
MAX_DISTANCE = 128
ADAM_LR, ADAM_B1, ADAM_B2, ADAM_EPS, ADAM_WD, ADAM_STEP = 0.001, 0.9, 0.999, 1e-08, 0.01, 10
GELU_C0 = math.sqrt(2.0 / math.pi)
GELU_C1 = 0.044715

V7X_VMEM_BYTES = 64 * 1024 * 1024
VMEM_LIMIT = V7X_VMEM_BYTES - 8 * 1024 * 1024
LANE = 128

NN = ((1,), (0,))
NT = ((1,), (1,))
TN = ((0,), (0,))


def _dot(a, b, dims):
    return lax.dot_general(a, b, (dims, ((), ())), preferred_element_type=F32)


def _tile(n, pref, unit=LANE):
    best = None
    for t in range(unit, min(n, pref) + 1, unit):
        if n % t == 0:
            best = t
    return n if best is None else best


def _params(n_grid):
    return pltpu.CompilerParams(dimension_semantics=("arbitrary",) * n_grid, vmem_limit_bytes=VMEM_LIMIT)


def _resident(shape):
    return pl.BlockSpec(shape, lambda i: (0, 0), pipeline_mode=pl.Buffered(1))


def _gelu(x):
    return 0.5 * x * (1.0 + jnp.tanh(GELU_C0 * (x + GELU_C1 * x * x * x)))


def _gelu_and_grad(x):
    x2 = x * x
    t = jnp.tanh(GELU_C0 * x * (1.0 + GELU_C1 * x2))
    val = 0.5 * x * (1.0 + t)
    grad = 0.5 * (1.0 + t) + 0.5 * x * (1.0 - t * t) * (GELU_C0 * (1.0 + 3.0 * GELU_C1 * x2))
    return val, grad


def _rms_stats(x):
    return lax.rsqrt(jnp.mean(x * x, axis=-1, keepdims=True) + EPS)


def _rms_bwd(dy, x, r, g):
    w = dy * g
    return r * w - x * (r * r * r) * jnp.mean(w * x, axis=-1, keepdims=True)


def _t5_bucket():
    i = np.arange(CHUNK)[:, None]
    j = np.arange(2 * CHUNK)[None, :]
    rel = np.maximum(i + CHUNK - j, 0)
    n_exact = N_BUCKETS // 2
    relf = np.maximum(rel, n_exact).astype(np.float32)
    large = n_exact + (np.log(relf / np.float32(n_exact)) / np.float32(math.log(MAX_DISTANCE / n_exact))
                       * np.float32(N_BUCKETS - n_exact)).astype(np.int32)
    large = np.minimum(large, N_BUCKETS - 1)
    bucket = np.where(rel < n_exact, rel, large)
    in_window = (i + CHUNK - j >= 0) & (i + CHUNK - j < CHUNK)
    return bucket.astype(np.int32), in_window


def _split3(x):
    hi = x.astype(BF16)
    r1 = x - hi.astype(F32)
    mid = r1.astype(BF16)
    lo = (r1 - mid.astype(F32)).astype(BF16)
    return hi, mid, lo


HBM_SPEC = pl.BlockSpec(memory_space=pltpu.HBM)


def _mesh_pos():
    return lax.axis_index("x"), lax.axis_index("y"), lax.axis_index("c")


def _dev_index(px, py, pc):
    return 4 * px + 2 * py + pc


SEM_SPEC = pl.BlockSpec(memory_space=pltpu.SEMAPHORE)
ANY_SPEC = pl.BlockSpec(memory_space=pl.ANY)
VMEM_SPEC = pl.BlockSpec(memory_space=pltpu.VMEM)
TOKEN = SDS((8, LANE), F32)
SIDE_EFFECT = pltpu.SideEffectType.DATAFLOW_SIDE_EFFECTING


def _hbm(x):
    return pltpu.with_memory_space_constraint(x, pltpu.HBM)


class _CallChain:
    def __init__(self):
        self.token = None

    def call(self, body, *, in_specs, out_specs, out_shape, **kwargs):
        dep, n_in = self.token, len(in_specs)
        single = not isinstance(out_shape, (list, tuple))
        out_shapes = [out_shape] if single else list(out_shape)
        out_specs = [out_specs] if single else list(out_specs)
        n_out = len(out_shapes)
        n_dep = 0 if dep is None else 1
        token_spec = pl.BlockSpec((8, LANE), lambda *_: (0, 0)) if kwargs.get("grid") else VMEM_SPEC

        def chained(*refs):
            outs_at = n_in + n_dep
            body(*refs[:n_in], *refs[outs_at:outs_at + n_out], *refs[outs_at + n_out + 1:])
            token = refs[outs_at + n_out]
            token[...] = jnp.zeros_like(token)

        inner = pl.pallas_call(chained, in_specs=list(in_specs) + [ANY_SPEC] * n_dep, out_specs=out_specs + [token_spec],
                               out_shape=out_shapes + [TOKEN], **kwargs)

        def run(*operands):
            outs = inner(*operands) if dep is None else inner(*operands, dep)
            self.token = outs[n_out]
            return outs[0] if single else list(outs[:n_out])

        return run


_CHAIN = _CallChain()


def _wait_all(waits, x, y, c):
    for kind, src, dst, send_sem, recv_sem in waits:
        cp = pltpu.make_async_remote_copy(src_ref=src, dst_ref=dst, send_sem=send_sem, recv_sem=recv_sem,
                                          device_id=(x, y, c), device_id_type=MESH)
        if kind == "send":
            cp.wait_send()
        else:
            cp.wait_recv()


PEER_SETS = {"sibling": 0, "near": 1, "chips": 2}
_handshakes = {}


def _handshake(peer_set):
    x, y, c = _mesh_pos()
    peers = {"sibling": [(x, y, 1 - c)],
             "near": [(x, y, 1 - c), (1 - x, y, c), (x, 1 - y, c)],
             "chips": [(1 - x, y, c), (x, 1 - y, c), (1 - x, 1 - y, c)]}[peer_set]
    barrier = pltpu.get_barrier_semaphore()
    for peer in peers:
        pl.semaphore_signal(barrier, inc=1, device_id=peer, device_id_type=MESH)
    pl.semaphore_wait(barrier, len(peers))


def _split_start(bufs, copies_of, n_sems, name, peer_set, sem_sets=(), waits_of=None):
    n, ns = len(bufs), len(sem_sets)
    flat_sems = [s for pair in sem_sets for s in pair]
    uses = _handshakes.get(peer_set, 0)
    _handshakes[peer_set] = uses + 1
    collective_id = 3 * PEER_SETS[peer_set] + uses % 3

    def body(*refs):
        ins = refs[:n]
        sems = refs[n:n + 2 * ns]
        send_sems, recv_sems = refs[n + 2 * ns], refs[n + 2 * ns + 1]
        _handshake(peer_set)
        if waits_of is not None:
            _wait_all(waits_of(ins, [(sems[2 * i], sems[2 * i + 1]) for i in range(ns)]), *_mesh_pos())
        for src, dst, k, target in copies_of(ins):
            pltpu.make_async_remote_copy(src_ref=src, dst_ref=dst, send_sem=send_sems.at[k], recv_sem=recv_sems.at[k],
                                         device_id=target, device_id_type=MESH).start()

    outs = _CHAIN.call(
        body, name=name,
        out_shape=[pltpu.SemaphoreType.DMA((n_sems,)), pltpu.SemaphoreType.DMA((n_sems,))]
        + [pltpu.HBM(b.shape, b.dtype) for b in bufs],
        in_specs=[HBM_SPEC] * n + [SEM_SPEC] * (2 * ns), out_specs=[SEM_SPEC, SEM_SPEC] + [HBM_SPEC] * n,
        input_output_aliases={a: 2 + a for a in range(n)},
        compiler_params=pltpu.CompilerParams(has_side_effects=SIDE_EFFECT, collective_id=collective_id),
    )(*[_hbm(b) for b in bufs], *flat_sems)
    return outs[0], outs[1], list(outs[2:2 + n])


def _split_wait(bufs, sem_sets, waits_of, name):
    n, ns = len(bufs), len(sem_sets)
    flat_sems = [s for pair in sem_sets for s in pair]

    def body(*refs):
        ins = refs[:n]
        sems = refs[n:n + 2 * ns]
        _wait_all(waits_of(ins, [(sems[2 * i], sems[2 * i + 1]) for i in range(ns)]), *_mesh_pos())

    outs = _CHAIN.call(
        body, name=name,
        out_shape=[pltpu.HBM(b.shape, b.dtype) for b in bufs],
        in_specs=[HBM_SPEC] * n + [SEM_SPEC] * (2 * ns), out_specs=[HBM_SPEC] * n,
        input_output_aliases={a: a for a in range(n)},
        compiler_params=pltpu.CompilerParams(has_side_effects=SIDE_EFFECT),
    )(*bufs, *flat_sems)
    return list(outs)


def _gather_blocks(land):
    rows = land.shape[1]
    first = (rows // 2) // 16 * 16

    def block(px, py, pc):
        return land.at[_dev_index(px, py, pc)]

    def halves(px, py, pc):
        return (land.at[_dev_index(px, py, pc), pl.ds(0, first)], land.at[_dev_index(px, py, pc), pl.ds(first, rows - first)])

    return block, halves


def _gather_begin(shards):
    me = _dev_index(*_mesh_pos())
    lands = [lax.dynamic_update_index_in_dim(lax.empty((N_DEV,) + s.shape, s.dtype), s, me, 0) for s in shards]
    return dict(lands=lands, stage={})


STAGE_COPIES = (3, 4, 1)


def _gather_step(state, items, name):
    which = sorted({a for a, _ in items})
    at = {a: i for i, a in enumerate(which)}
    sem_sets = [state["stage"][(a, s - 1)][0] for a, s in items if s > 0]
    offset, n_sems = {}, 0
    for a, s in items:
        offset[(a, s)] = n_sems
        n_sems += STAGE_COPIES[s]

    def waits_of(ins, sems):
        x, y, c = _mesh_pos()
        out, earlier = [], 0
        for a, s in items:
            if s == 0:
                continue
            block, halves = _gather_blocks(ins[at[a]])
            send, recv = sems[earlier]
            off = state["stage"][(a, s - 1)][1]
            earlier += 1
            if s == 1:
                arrived = [(1, block(1 - x, y, c)), (2, block(x, 1 - y, c))]
            else:
                arrived = list(zip((2, 3), halves(1 - x, 1 - y, c)))
            out += [("recv", ref, ref, send.at[off + k], recv.at[off + k]) for k, ref in arrived]
        return out

    def copies_of(ins):
        x, y, c = _mesh_pos()
        sibling = (x, y, 1 - c)
        out = []
        for a, s in items:
            block, halves = _gather_blocks(ins[at[a]])
            off = offset[(a, s)]
            if s == 0:
                mine = block(x, y, c)
                out += [(mine, mine, off + 1, (1 - x, y, c)), (mine, mine, off + 2, (x, 1 - y, c)), (mine, mine, off, sibling)]
            elif s == 1:
                from_x, from_y = block(1 - x, y, c), block(x, 1 - y, c)
                out += [(halves(1 - x, y, c)[0], halves(1 - x, y, c)[0], off + 2, (x, 1 - y, c)),
                        (halves(x, 1 - y, c)[1], halves(x, 1 - y, c)[1], off + 3, (1 - x, y, c)),
                        (from_x, from_x, off, sibling), (from_y, from_y, off + 1, sibling)]
            else:
                diag = block(1 - x, 1 - y, c)
                out.append((diag, diag, off, sibling))
        return out

    send_sems, recv_sems, bufs = _split_start([state["lands"][a] for a in which], copies_of, n_sems, name, "near",
                                              sem_sets=sem_sets, waits_of=waits_of)
    for a in which:
        state["lands"][a] = bufs[at[a]]
    for a, s in items:
        state["stage"][(a, s)] = ((send_sems, recv_sems), offset[(a, s)])


def _gather_end(state, which, name):
    sem_sets = [state["stage"][(a, s)][0] for a in which for s in range(3)]

    def waits(ins, sems):
        x, y, c = _mesh_pos()
        out = []
        for i, a in enumerate(which):
            block, halves = _gather_blocks(ins[i])
            (b_send, b_recv), (s1_send, s1_recv), (s2_send, s2_recv) = sems[3 * i:3 * i + 3]
            o0, o1, o2 = (state["stage"][(a, s)][1] for s in range(3))
            arrivals = [(block(x, y, 1 - c), b_send, b_recv, o0),
                        (block(1 - x, y, 1 - c), s1_send, s1_recv, o1), (block(x, 1 - y, 1 - c), s1_send, s1_recv, o1 + 1),
                        (block(1 - x, 1 - y, 1 - c), s2_send, s2_recv, o2)]
            mine = block(x, y, c)
            sent = [(mine, b_send, b_recv, o0 + k) for k in range(3)]
            sent += [(block(1 - x, y, c), s1_send, s1_recv, o1), (block(x, 1 - y, c), s1_send, s1_recv, o1 + 1),
                     (halves(1 - x, y, c)[0], s1_send, s1_recv, o1 + 2), (halves(x, 1 - y, c)[1], s1_send, s1_recv, o1 + 3),
                     (block(1 - x, 1 - y, c), s2_send, s2_recv, o2)]
            out += [("recv", ref, ref, s.at[k], r.at[k]) for ref, s, r, k in arrivals]
            out += [("send", ref, ref, s.at[k], r.at[k]) for ref, s, r, k in sent]
        return out

    bufs = _split_wait([state["lands"][a] for a in which], sem_sets, waits, name)
    for i, a in enumerate(which):
        state["lands"][a] = bufs[i]
    return bufs


def _sibling_exchange_begin(parts, name):
    lands = [lax.empty((4,) + p.shape[1:], p.dtype) for p in parts]
    n = len(parts)

    def copies_of(ins):
        x, y, c = _mesh_pos()
        return [(ins[a].at[2 * j + (1 - c)], ins[n + a].at[j], 4 * a + j, (x, y, 1 - c)) for a in range(n) for j in range(4)]

    send_sems, recv_sems, bufs = _split_start(list(parts) + lands, copies_of, 4 * n, name, "sibling")
    return dict(bufs=bufs, sems=(send_sems, recv_sems), n=n)


def _sibling_exchange_end(state, name):
    n = state["n"]

    def waits(ins, sems):
        _, _, c = _mesh_pos()
        return [(kind, ins[a].at[2 * j + (1 - c)], ins[n + a].at[j], sems[0][0].at[4 * a + j], sems[0][1].at[4 * a + j])
                for a in range(n) for j in range(4) for kind in ("send", "recv")]

    bufs = _split_wait(state["bufs"], [state["sems"]], waits, name)
    return [(bufs[a], bufs[n + a]) for a in range(n)]


CHIP_FLIPS = (2, 1, 3)


def _chip_exchange_begin(csums, name):
    lands = [lax.empty((3,) + s.shape[1:], s.dtype) for s in csums]
    n = len(csums)

    def copies_of(ins):
        x, y, c = _mesh_pos()
        chips = [(1 - x, y), (x, 1 - y), (1 - x, 1 - y)]
        return [(ins[a].at[CHIP_FLIPS[r]], ins[n + a].at[r], 3 * a + r, (px, py, c))
                for a in range(n) for r, (px, py) in enumerate(chips)]

    send_sems, recv_sems, bufs = _split_start(list(csums) + lands, copies_of, 3 * n, name, "chips")
    return dict(bufs=bufs, sems=(send_sems, recv_sems), n=n)


def _chip_exchange_end(state, a, name):
    n = state["n"]

    def waits(ins, sems):
        return [(kind, ins[0].at[CHIP_FLIPS[r]], ins[1].at[r], sems[0][0].at[3 * a + r], sems[0][1].at[3 * a + r])
                for r in range(3) for kind in ("send", "recv")]

    csum, received = _split_wait([state["bufs"][a], state["bufs"][n + a]], [state["sems"]], waits, name)
    return csum, received


def _chip_sum(part, recv, name):
    _, R, C = part.shape
    tr = _tile(R, 1024, 16)
    place = jnp.stack([lax.axis_index("c"), 2 * lax.axis_index("x") + lax.axis_index("y")]).astype(jnp.int32)

    def body(place_ref, p_ref, r_ref, o_ref):
        o_ref[...] = (p_ref[...].astype(F32) + r_ref[...].astype(F32)).astype(o_ref.dtype)

    def chip(p, place_ref):
        return jnp.bitwise_xor(p, place_ref[1])

    grid_spec = pltpu.PrefetchScalarGridSpec(
        num_scalar_prefetch=1, grid=(4, R // tr),
        in_specs=[pl.BlockSpec((None, tr, C), lambda p, i, place_ref: (2 * chip(p, place_ref) + place_ref[0], i, 0)),
                  pl.BlockSpec((None, tr, C), lambda p, i, place_ref: (chip(p, place_ref), i, 0))],
        out_specs=pl.BlockSpec((None, tr, C), lambda p, i, place_ref: (p, i, 0)))
    return pl.pallas_call(body, name=name, grid_spec=grid_spec, out_shape=SDS((4, R, C), part.dtype),
                          compiler_params=_params(2))(place, part, recv)


def _bias_fwd(table_t, onehot_t, onehot_kq_t):
    H = table_t.shape[0]
    n = onehot_t.shape[1]

    def body(t_ref, oh_ref, oh_kq_ref, o_ref, o_kq_ref):
        hi, mid, lo = _split3(t_ref[...])
        for src, dst in ((oh_ref, o_ref), (oh_kq_ref, o_kq_ref)):
            oh = src[...]
            dst[...] = _dot(hi, oh, NN) + _dot(mid, oh, NN) + _dot(lo, oh, NN)

    return _CHAIN.call(body, name="bias_fwd", in_specs=[VMEM_SPEC] * 3, out_specs=[VMEM_SPEC] * 2,
                       out_shape=[SDS((H, n), F32)] * 2, compiler_params=_params(0))(table_t, onehot_t, onehot_kq_t)


def _mix_norm(x, g):
    T, D = x.shape
    tm = _tile(T, 512)

    def body(x_ref, g_ref, n_ref):
        xv = x_ref[...]
        n_ref[...] = (xv * _rms_stats(xv) * g_ref[...]).astype(BF16)

    row = pl.BlockSpec((tm, D), lambda i: (i, 0))
    return _CHAIN.call(body, name="mix_norm", grid=(T // tm,), in_specs=[row, pl.BlockSpec((1, D), lambda i: (0, 0))],
                       out_specs=row, out_shape=SDS((T, D), BF16), compiler_params=_params(1))(x, g)


def _inproj_fwd(n, w_t):
    T, D = n.shape
    P = w_t.shape[0]
    tm = _tile(T, 512)

    def body(n_ref, w_ref, proj_ref):
        proj_ref[...] = _dot(n_ref[...], w_ref[...], NT)

    return _CHAIN.call(
        body, name="inproj_fwd", grid=(T // tm,),
        in_specs=[pl.BlockSpec((tm, D), lambda i: (i, 0)), _resident((P, D))],
        out_specs=pl.BlockSpec((tm, P), lambda i: (i, 0)),
        out_shape=SDS((T, P), F32), compiler_params=_params(1))(n, w_t)


def _layer_norm_group(vg, lg, lb):
    mu = jnp.mean(vg, axis=-1, keepdims=True)
    xc = vg - mu
    rstd = lax.rsqrt(jnp.mean(xc * xc, axis=-1, keepdims=True) + EPS)
    vhat = xc * rstd
    return vhat, rstd, vhat * lg + lb


def _gmlp_fwd(proj, lg, lb, w_s, bs_t, A):
    T = proj.shape[0]
    G = A // GROUP_DIM
    tm = _tile(T, 512)
    nc = tm // CHUNK

    def body(u_ref, v_ref, lg_ref, lb_ref, w_ref, bst_ref, a_ref):
        row = lax.broadcasted_iota(jnp.int32, (CHUNK, CHUNK), 0)
        col = lax.broadcasted_iota(jnp.int32, (CHUNK, CHUNK), 1)
        causal = row >= col
        for g in range(G):
            sl = slice(g * GROUP_DIM, (g + 1) * GROUP_DIM)
            _, _, vn = _layer_norm_group(_gelu(v_ref[:, sl]), lg_ref[:, sl], lb_ref[:, sl])
            vnb = vn.astype(BF16)
            wm = jnp.where(causal, w_ref[g], 0.0).astype(BF16)
            ug = _gelu(u_ref[:, sl])
            bcol = bst_ref[:, g:g + 1]
            for c in range(nc):
                rs = slice(c * CHUNK, (c + 1) * CHUNK)
                a_ref[rs, sl] = ug[rs] * (_dot(wm, vnb[rs], NN) + bcol)

    return _CHAIN.call(
        body, name="gmlp_fwd", grid=(T // tm,),
        in_specs=[pl.BlockSpec((tm, A), lambda i: (i, 0)), pl.BlockSpec((tm, A), lambda i: (i, 1)),
                  pl.BlockSpec((1, A), lambda i: (0, 0)), pl.BlockSpec((1, A), lambda i: (0, 0)),
                  pl.BlockSpec((G, CHUNK, CHUNK), lambda i: (0, 0, 0)), pl.BlockSpec((CHUNK, G), lambda i: (0, 0))],
        out_specs=pl.BlockSpec((tm, A), lambda i: (i, 0)),
        out_shape=SDS((T, A), F32), compiler_params=_params(1))(proj, proj, lg, lb, w_s, bs_t)


def _attn_masks(first_tile):
    ii = lax.broadcasted_iota(jnp.int32, (CHUNK, 2 * CHUNK), 0)
    jj = lax.broadcasted_iota(jnp.int32, (CHUNK, 2 * CHUNK), 1)
    in_window = (jj > ii) & (jj <= ii + CHUNK)
    first_mask = in_window & jnp.logical_or(jnp.logical_not(first_tile), jj >= CHUNK)
    return in_window, first_mask


def _softmax_with_sink(s, sink, axis):
    m = jnp.maximum(jnp.max(s, axis=axis, keepdims=True), sink)
    p = jnp.exp(s - m)
    e_sink = jnp.exp(sink - m)
    inv = 1.0 / (jnp.sum(p, axis=axis, keepdims=True) + e_sink)
    return p * inv, e_sink * inv


def _pad_heads(band, group):
    lane = lax.broadcasted_iota(jnp.int32, band.shape, 1)
    if group == 0:
        low = jnp.where(lane < HEAD_DIM, band, 0.0)
        high = pltpu.roll(low, HEAD_DIM, 1)
    else:
        high = jnp.where(lane >= HEAD_DIM, band, 0.0)
        low = pltpu.roll(high, HEAD_DIM, 1)
    return low.astype(BF16), high.astype(BF16)


def _attn_specs(tq, A, B, reverse_tiles=None):
    nb = tq // CHUNK
    kcol = (2 * A + B) // LANE
    if reverse_tiles is None:
        tile = lambda i: i
    else:
        tile = lambda i: reverse_tiles - 1 - i
    prev = lambda i: jnp.maximum(tile(i) * nb - 1, 0)
    return [pl.BlockSpec((tq, B), lambda i: (tile(i), 2 * A // B)),
            pl.BlockSpec((tq, LANE), lambda i: (tile(i), kcol)),
            pl.BlockSpec((tq, LANE), lambda i: (tile(i), kcol + 1)),
            pl.BlockSpec((CHUNK, LANE), lambda i: (prev(i), kcol)),
            pl.BlockSpec((CHUNK, LANE), lambda i: (prev(i), kcol + 1))]


def _attn_fwd(proj, bias, sinks, A, B):
    T = proj.shape[0]
    H = B // HEAD_DIM
    qpk = H // KV_HEADS
    tq = _tile(T, 512)
    nb = tq // CHUNK

    scale = HEAD_DIM ** -0.5

    def body(sink_ref, q_ref, k_ref, v_ref, kp_ref, vp_ref, bias_ref, o_ref):
        in_window, first_mask = _attn_masks(pl.program_id(0) == 0)
        for b in range(nb):
            rows = slice(b * CHUNK, (b + 1) * CHUNK)
            if b == 0:
                kprev, vprev, mask = kp_ref[...], vp_ref[...], first_mask
            else:
                prows = slice((b - 1) * CHUNK, b * CHUNK)
                kprev, vprev, mask = k_ref[prows, :], v_ref[prows, :], in_window
            kband = jnp.concatenate([kprev, k_ref[rows, :]], axis=0)
            vband = jnp.concatenate([vprev, v_ref[rows, :]], axis=0)
            k_pads = [_pad_heads(kband, g) for g in range(KV_HEADS)]
            v_both = [jnp.concatenate(_pad_heads(vband, g), axis=0) for g in range(KV_HEADS)]
            scores = []
            for pair in range(H // 2):
                h = 2 * pair
                qs = (q_ref[rows, h * HEAD_DIM:(h + 2) * HEAD_DIM] * scale).astype(BF16)
                scores += [_dot(qs, kz, NT) for kz in k_pads[h // qpk]]
            probs = [_softmax_with_sink(jnp.where(mask, s + bias_ref[h], NEG), sink_ref[h], -1)[0].astype(BF16)
                     for h, s in enumerate(scores)]
            outs = [_dot(jnp.concatenate(probs[h:h + 2], axis=1), v_both[h // qpk], NN) for h in range(0, H, 2)]
            o_ref[rows, :] = jnp.concatenate(outs, axis=1)

    return _CHAIN.call(
        body, name="attn_fwd", grid=(T // tq,),
        in_specs=[pl.BlockSpec(memory_space=pltpu.SMEM)] + _attn_specs(tq, A, B)
        + [pl.BlockSpec((H, CHUNK, 2 * CHUNK), lambda i: (0, 0, 0))],
        out_specs=pl.BlockSpec((tq, B), lambda i: (i, 0)),
        out_shape=SDS((T, B), F32), compiler_params=_params(1))(sinks, proj, proj, proj, proj, proj, bias)


def _outproj_fwd(a, b, ga, gb, x, w, g_ffn):
    T, A = a.shape
    B = b.shape[1]
    D = x.shape[1]
    tm = _tile(T, 512)

    def body(a_ref, b_ref, ga_ref, gb_ref, x_ref, w_ref, gf_ref, h_ref, mix_ref, n_ref):
        av, bv = a_ref[...], b_ref[...]
        mix_ref[:, :A] = (av * _rms_stats(av) * ga_ref[...]).astype(BF16)
        mix_ref[:, A:] = (bv * _rms_stats(bv) * gb_ref[...]).astype(BF16)
        hv = x_ref[...] + _dot(mix_ref[...], w_ref[...], NN)
        h_ref[...] = hv
        n_ref[...] = (hv * _rms_stats(hv) * gf_ref[...]).astype(BF16)

    row = pl.BlockSpec((tm, D), lambda i: (i, 0))
    return _CHAIN.call(
        body, name="outproj_fwd", grid=(T // tm,),
        in_specs=[pl.BlockSpec((tm, A), lambda i: (i, 0)), pl.BlockSpec((tm, B), lambda i: (i, 0)),
                  pl.BlockSpec((1, A), lambda i: (0, 0)), pl.BlockSpec((1, B), lambda i: (0, 0)),
                  row, _resident((A + B, D)), pl.BlockSpec((1, D), lambda i: (0, 0))],
        out_specs=[row, pl.BlockSpec((tm, A + B), lambda i: (i, 0)), row],
        out_shape=[SDS((T, D), F32), SDS((T, A + B), BF16), SDS((T, D), BF16)],
        compiler_params=_params(1))(a, b, ga, gb, x, w, g_ffn)


def _ffn_up(n, w_up):
    T, D = n.shape
    Fb = w_up.shape[2]
    F = N_DEV * Fb
    tm, tf = _tile(T, 1024), _tile(Fb, 1024)
    per = Fb // tf

    def body(n_ref, wu_ref, z_ref):
        z_ref[...] = jnp.maximum(_dot(n_ref[...], wu_ref[...], NN), 0.0).astype(BF16)

    return _CHAIN.call(
        body, name="ffn_up", grid=(T // tm, F // tf),
        in_specs=[pl.BlockSpec((tm, D), lambda i, j: (i, 0)),
                  pl.BlockSpec((None, D, tf), lambda i, j: (j // per, 0, j % per))],
        out_specs=pl.BlockSpec((tm, tf), lambda i, j: (i, j)),
        out_shape=SDS((T, F), BF16), compiler_params=_params(2))(n, w_up)


def _ffn_down(h1, z, w_down):
    T, D = h1.shape
    F = w_down.shape[0]
    tm, tn, tk = _tile(T, 1024), _tile(D, 1024), _tile(F, 4096)

    def body(h_ref, z_ref, wd_ref, h2_ref):
        k = pl.program_id(2)

        @pl.when(k == 0)
        def _():
            h2_ref[...] = h_ref[...]

        zf = z_ref[...].astype(F32)
        h2_ref[...] += _dot((zf * zf).astype(BF16), wd_ref[...], NN)

    return _CHAIN.call(
        body, name="ffn_down", grid=(T // tm, D // tn, F // tk),
        in_specs=[pl.BlockSpec((tm, tn), lambda i, j, k: (i, j)), pl.BlockSpec((tm, tk), lambda i, j, k: (i, k)),
                  pl.BlockSpec((tk, tn), lambda i, j, k: (k, j))],
        out_specs=pl.BlockSpec((tm, tn), lambda i, j, k: (i, j)),
        out_shape=SDS((T, D), F32), compiler_params=_params(3))(h1, z, w_down)


def _final_loss(h2, g, target):
    T, D = h2.shape
    tm = _tile(T, 512)

    def body(h_ref, g_ref, t_ref, loss_ref, dg_ref, dh_ref, dhb_ref):
        @pl.when(pl.program_id(0) == 0)
        def _():
            loss_ref[...] = jnp.zeros_like(loss_ref)
            dg_ref[...] = jnp.zeros_like(dg_ref)

        hv, gv = h_ref[...], g_ref[...]
        r = _rms_stats(hv)
        hn = hv * r
        e = hn * gv - t_ref[...]
        loss_ref[...] += (0.5 / D) * jnp.sum(jnp.sum(e * e, axis=0, keepdims=True), axis=-1, keepdims=True)
        dy = e * (1.0 / D)
        dg_ref[...] += jnp.sum(dy * hn, axis=0, keepdims=True)
        dh = _rms_bwd(dy, hv, r, gv)
        dh_ref[...] = dh
        dhb_ref[...] = dh.astype(BF16)

    return _CHAIN.call(
        body, name="final_loss", grid=(T // tm,),
        in_specs=[pl.BlockSpec((tm, D), lambda i: (i, 0)), pl.BlockSpec((1, D), lambda i: (0, 0)),
                  pl.BlockSpec((tm, D), lambda i: (i, 0))],
        out_specs=[pl.BlockSpec((1, 1), lambda i: (0, 0)), pl.BlockSpec((1, D), lambda i: (0, 0)),
                   pl.BlockSpec((tm, D), lambda i: (i, 0)), pl.BlockSpec((tm, D), lambda i: (i, 0))],
        out_shape=[SDS((1, 1), F32), SDS((1, D), F32), SDS((T, D), F32), SDS((T, D), BF16)],
        compiler_params=_params(1))(h2, g, target)


def _ffn_down_bwd(dh2b, z, w_down):
    T, D = dh2b.shape
    F = w_down.shape[0]
    tm, tf = _tile(T, 1024), _tile(F, 1024)

    def body(dh_ref, z_ref, wd_ref, dzp_ref):
        dzz = _dot(dh_ref[...], wd_ref[...], NT)
        dzp_ref[...] = (dzz * (2.0 * z_ref[...].astype(F32))).astype(BF16)

    return _CHAIN.call(
        body, name="ffn_down_bwd", grid=(T // tm, F // tf),
        in_specs=[pl.BlockSpec((tm, D), lambda i, j: (i, 0)), pl.BlockSpec((tm, tf), lambda i, j: (i, j)),
                  pl.BlockSpec((tf, D), lambda i, j: (j, 0))],
        out_specs=pl.BlockSpec((tm, tf), lambda i, j: (i, j)),
        out_shape=SDS((T, F), BF16), compiler_params=_params(2))(dh2b, z, w_down)


def _ffn_up_bwd(dzp, w_up_t):
    T, F = dzp.shape
    D = w_up_t.shape[1]
    tm, tn, tk = _tile(T, 1024), _tile(D, 1024), _tile(F, 4096)

    def body(dzp_ref, w_ref, dn_ref):
        part = _dot(dzp_ref[...], w_ref[...], NN)

        @pl.when(pl.program_id(2) == 0)
        def _():
            dn_ref[...] = part

        @pl.when(pl.program_id(2) > 0)
        def _():
            dn_ref[...] += part

    return _CHAIN.call(
        body, name="ffn_up_bwd", grid=(T // tm, D // tn, F // tk),
        in_specs=[pl.BlockSpec((tm, tk), lambda i, j, k: (i, k)), pl.BlockSpec((tk, tn), lambda i, j, k: (k, j))],
        out_specs=pl.BlockSpec((tm, tn), lambda i, j, k: (i, j)),
        out_shape=SDS((T, D), F32), compiler_params=_params(3))(dzp, w_up_t)


def _ffn_norm_bwd(dn, dh2, h1, g):
    T, D = h1.shape
    tm = _tile(T, 512)

    def body(dn_ref, dh_ref, h_ref, g_ref, dh1_ref, dh1b_ref, dg_ref):
        @pl.when(pl.program_id(0) == 0)
        def _():
            dg_ref[...] = jnp.zeros_like(dg_ref)

        hv, dnv = h_ref[...], dn_ref[...]
        r = _rms_stats(hv)
        dg_ref[...] += jnp.sum(dnv * (hv * r), axis=0, keepdims=True)
        dh1 = dh_ref[...] + _rms_bwd(dnv, hv, r, g_ref[...])
        dh1_ref[...] = dh1
        dh1b_ref[...] = dh1.astype(BF16)

    row = pl.BlockSpec((tm, D), lambda i: (i, 0))
    vec = pl.BlockSpec((1, D), lambda i: (0, 0))
    return _CHAIN.call(
        body, name="ffn_norm_bwd", grid=(T // tm,), in_specs=[row, row, row, vec], out_specs=[row, row, vec],
        out_shape=[SDS((T, D), F32), SDS((T, D), BF16), SDS((1, D), F32)], compiler_params=_params(1))(dn, dh2, h1, g)


def _matmul_tn(a, b, name, square_a=False, col_blocks=None):
    T, K = a.shape
    N = b.shape[1]
    tk = _tile(K, 1792)
    tn = _tile(N if col_blocks is None else N // col_blocks, 1024 if tk <= 1024 else 512)

    def body(a_ref, b_ref, o_ref):
        av = a_ref[...]
        if square_a:
            af = av.astype(F32)
            av = (af * af).astype(BF16)
        o_ref[...] = _dot(av, b_ref[...], TN).astype(o_ref.dtype)

    if col_blocks is None:
        out_shape = SDS((K, N), BF16)
        out_spec = pl.BlockSpec((tk, tn), lambda i, j: (i, j))
    else:
        per = (N // col_blocks) // tn
        out_shape = SDS((col_blocks, K, N // col_blocks), BF16)
        out_spec = pl.BlockSpec((None, tk, tn), lambda i, j: (j // per, i, j % per))
    return _CHAIN.call(
        body, name=name, grid=(K // tk, N // tn),
        in_specs=[pl.BlockSpec((T, tk), lambda i, j: (0, i)), pl.BlockSpec((T, tn), lambda i, j: (0, j))],
        out_specs=out_spec, out_shape=out_shape, compiler_params=_params(2))(a, b)


def _outproj_bwd(dh1b, w, a, b, ga, gb):
    T, D = dh1b.shape
    A, B = a.shape[1], b.shape[1]
    tm = _tile(T, 512)

    def body(dh_ref, w_ref, a_ref, b_ref, ga_ref, gb_ref, da_ref, db_ref, dga_ref, dgb_ref):
        @pl.when(pl.program_id(0) == 0)
        def _():
            dga_ref[...] = jnp.zeros_like(dga_ref)
            dgb_ref[...] = jnp.zeros_like(dgb_ref)

        dmix = _dot(dh_ref[...], w_ref[...], NT)
        for src_ref, g_ref, dx_ref, dg_ref, dn in ((a_ref, ga_ref, da_ref, dga_ref, dmix[:, :A]),
                                                   (b_ref, gb_ref, db_ref, dgb_ref, dmix[:, A:])):
            xv = src_ref[...]
            r = _rms_stats(xv)
            dg_ref[...] += jnp.sum(dn * (xv * r), axis=0, keepdims=True)
            dx_ref[...] = _rms_bwd(dn, xv, r, g_ref[...])

    return _CHAIN.call(
        body, name="outproj_bwd", grid=(T // tm,),
        in_specs=[pl.BlockSpec((tm, D), lambda i: (i, 0)), _resident((A + B, D)),
                  pl.BlockSpec((tm, A), lambda i: (i, 0)), pl.BlockSpec((tm, B), lambda i: (i, 0)),
                  pl.BlockSpec((1, A), lambda i: (0, 0)), pl.BlockSpec((1, B), lambda i: (0, 0))],
        out_specs=[pl.BlockSpec((tm, A), lambda i: (i, 0)), pl.BlockSpec((tm, B), lambda i: (i, 0)),
                   pl.BlockSpec((1, A), lambda i: (0, 0)), pl.BlockSpec((1, B), lambda i: (0, 0))],
        out_shape=[SDS((T, A), F32), SDS((T, B), F32), SDS((1, A), F32), SDS((1, B), F32)],
        compiler_params=_params(1))(dh1b, w, a, b, ga, gb)


def _gmlp_bwd(proj, da, lg, lb, w_s, w_st, bs_t, A):
    T = proj.shape[0]
    G = A // GROUP_DIM
    tm = _tile(T, 512)
    nc = tm // CHUNK

    def body(u_ref, v_ref, da_ref, lg_ref, lb_ref, w_ref, wt_ref, bst_ref, duv_ref, dlg_ref, dlb_ref, dw_ref, dbs_ref):
        @pl.when(pl.program_id(0) == 0)
        def _():
            dlg_ref[...] = jnp.zeros_like(dlg_ref)
            dlb_ref[...] = jnp.zeros_like(dlb_ref)
            dw_ref[...] = jnp.zeros_like(dw_ref)
            dbs_ref[...] = jnp.zeros_like(dbs_ref)

        row = lax.broadcasted_iota(jnp.int32, (CHUNK, CHUNK), 0)
        col = lax.broadcasted_iota(jnp.int32, (CHUNK, CHUNK), 1)
        lower = row >= col
        upper = row <= col
        for g in range(G):
            sl = slice(g * GROUP_DIM, (g + 1) * GROUP_DIM)
            lgv = lg_ref[:, sl]
            vg, vg_grad = _gelu_and_grad(v_ref[:, sl])
            vhat, rstd, vn = _layer_norm_group(vg, lgv, lb_ref[:, sl])
            vnb = vn.astype(BF16)
            ug, ug_grad = _gelu_and_grad(u_ref[:, sl])
            dav = da_ref[:, sl]
            wm = jnp.where(lower, w_ref[g], 0.0).astype(BF16)
            wmt = jnp.where(upper, wt_ref[g], 0.0).astype(BF16)
            bcol = bst_ref[:, g:g + 1]
            dw_acc = jnp.zeros((CHUNK, CHUNK), F32)
            dbs_acc = jnp.zeros((CHUNK, 1), F32)
            dvn_parts = []
            dug_parts = []
            for c in range(nc):
                rs = slice(c * CHUNK, (c + 1) * CHUNK)
                mixed = _dot(wm, vnb[rs], NN) + bcol
                dug_parts.append(dav[rs] * mixed)
                dmix = dav[rs] * ug[rs]
                dbs_acc = dbs_acc + jnp.sum(dmix, axis=-1, keepdims=True)
                dmixb = dmix.astype(BF16)
                dw_acc = dw_acc + _dot(dmixb, vnb[rs], NT)
                dvn_parts.append(_dot(wmt, dmixb, NN))
            dvn = jnp.concatenate(dvn_parts, axis=0)
            dug = jnp.concatenate(dug_parts, axis=0)
            dw_ref[g] += jnp.where(lower, dw_acc, 0.0)
            dbs_ref[:, g:g + 1] += dbs_acc
            dlg_ref[:, sl] += jnp.sum(dvn * vhat, axis=0, keepdims=True)
            dlb_ref[:, sl] += jnp.sum(dvn, axis=0, keepdims=True)
            dvhat = dvn * lgv
            dvg = rstd * (dvhat - jnp.mean(dvhat, axis=-1, keepdims=True)
                          - vhat * jnp.mean(dvhat * vhat, axis=-1, keepdims=True))
            duv_ref[:, sl] = (dug * ug_grad).astype(BF16)
            duv_ref[:, A + g * GROUP_DIM:A + (g + 1) * GROUP_DIM] = (dvg * vg_grad).astype(BF16)

    return _CHAIN.call(
        body, name="gmlp_bwd", grid=(T // tm,),
        in_specs=[pl.BlockSpec((tm, A), lambda i: (i, 0)), pl.BlockSpec((tm, A), lambda i: (i, 1)),
                  pl.BlockSpec((tm, A), lambda i: (i, 0)),
                  pl.BlockSpec((1, A), lambda i: (0, 0)), pl.BlockSpec((1, A), lambda i: (0, 0)),
                  pl.BlockSpec((G, CHUNK, CHUNK), lambda i: (0, 0, 0)),
                  pl.BlockSpec((G, CHUNK, CHUNK), lambda i: (0, 0, 0)), pl.BlockSpec((CHUNK, G), lambda i: (0, 0))],
        out_specs=[pl.BlockSpec((tm, 2 * A), lambda i: (i, 0)),
                   pl.BlockSpec((1, A), lambda i: (0, 0)), pl.BlockSpec((1, A), lambda i: (0, 0)),
                   pl.BlockSpec((G, CHUNK, CHUNK), lambda i: (0, 0, 0)), pl.BlockSpec((CHUNK, G), lambda i: (0, 0))],
        out_shape=[SDS((T, 2 * A), BF16), SDS((1, A), F32), SDS((1, A), F32),
                   SDS((G, CHUNK, CHUNK), F32), SDS((CHUNK, G), F32)],
        compiler_params=_params(1))(proj, proj, da, lg, lb, w_s, w_st, bs_t)


def _attn_bwd(proj, do, duv, bias_t, sinks, A, B):
    T, P = proj.shape
    H = B // HEAD_DIM
    qpk = H // KV_HEADS
    tq = _tile(T, 512)
    nb = tq // CHUNK
    n_tiles = T // tq
    scale = HEAD_DIM ** -0.5
    rev = lambda i: n_tiles - 1 - i

    def body(sink_ref, q_ref, k_ref, v_ref, kp_ref, vp_ref, do_ref, duv_ref, bias_ref,
             dproj_ref, dbias_ref, dsink_ref, carry, dkv, sacc):
        step = pl.program_id(0)

        @pl.when(step == 0)
        def _():
            carry[...] = jnp.zeros_like(carry)
            sacc[...] = jnp.zeros_like(sacc)
            dbias_ref[...] = jnp.zeros_like(dbias_ref)

        jj = lax.broadcasted_iota(jnp.int32, (2 * CHUNK, CHUNK), 0)
        ii = lax.broadcasted_iota(jnp.int32, (2 * CHUNK, CHUNK), 1)
        in_window = (jj > ii) & (jj <= ii + CHUNK)
        first_mask = in_window & jnp.logical_or(step != n_tiles - 1, jj >= CHUNK)
        low_query = lax.broadcasted_iota(jnp.int32, (CHUNK, LANE), 1) < HEAD_DIM
        low_key = lax.broadcasted_iota(jnp.int32, (2 * CHUNK, LANE), 1) < HEAD_DIM

        def split_pair(pair_bf16):
            zero = jnp.zeros_like(pair_bf16)
            return jnp.concatenate([jnp.where(low_query, pair_bf16, zero), jnp.where(low_query, zero, pair_bf16)], axis=0)

        dproj_ref[:, :2 * A] = duv_ref[...]
        dkv[...] = jnp.zeros_like(dkv)
        for b in range(nb):
            rows = slice(b * CHUNK, (b + 1) * CHUNK)
            band = slice(b * CHUNK, (b + 2) * CHUNK)
            if b == 0:
                kprev, vprev, mask = kp_ref[...], vp_ref[...], first_mask
            else:
                prows = slice((b - 1) * CHUNK, b * CHUNK)
                kprev, vprev, mask = k_ref[prows, :], v_ref[prows, :], in_window
            kband = jnp.concatenate([kprev, k_ref[rows, :]], axis=0)
            vband = jnp.concatenate([vprev, v_ref[rows, :]], axis=0)
            k_pads = [_pad_heads(kband, g) for g in range(KV_HEADS)]
            v_pads = [_pad_heads(vband, g) for g in range(KV_HEADS)]
            queries, douts, scores, dprobs = [], [], [], []
            for pair in range(H // 2):
                cols = slice(2 * pair * HEAD_DIM, (2 * pair + 2) * HEAD_DIM)
                qs = (q_ref[rows, cols] * scale).astype(BF16)
                dob = do_ref[rows, cols].astype(BF16)
                queries.append(qs)
                douts.append(dob)
                scores += [_dot(kz, qs, NT) for kz in k_pads[2 * pair // qpk]]
                dprobs += [_dot(vz, dob, NT) for vz in v_pads[2 * pair // qpk]]
            probs, dscores = [], []
            for h in range(H):
                pt, p_sink = _softmax_with_sink(jnp.where(mask, scores[h] + bias_ref[h], NEG), sink_ref[h], 0)
                delta = jnp.sum(pt * dprobs[h], axis=0, keepdims=True)
                dst = pt * (dprobs[h] - delta)
                dbias_ref[h] += dst
                sacc[h:h + 1, :] += -(p_sink * delta)
                probs.append(pt.astype(BF16))
                dscores.append(dst.astype(BF16))
            dq_parts, dk_groups, dv_groups = [], [], []
            for g in range(KV_HEADS):
                k_both = jnp.concatenate(k_pads[g], axis=0)
                dk_acc = jnp.zeros((2 * CHUNK, LANE), F32)
                dv_acc = jnp.zeros((2 * CHUNK, LANE), F32)
                for pair in range(g * qpk // 2, (g + 1) * qpk // 2):
                    pair_heads = slice(2 * pair, 2 * pair + 2)
                    dk_acc = dk_acc + _dot(jnp.concatenate(dscores[pair_heads], axis=1), split_pair(queries[pair]), NN)
                    dv_acc = dv_acc + _dot(jnp.concatenate(probs[pair_heads], axis=1), split_pair(douts[pair]), NN)
                    dq_parts.append(_dot(jnp.concatenate(dscores[pair_heads], axis=0), k_both, TN) * scale)
                dk_groups.append(dk_acc + pltpu.roll(dk_acc, HEAD_DIM, 1))
                dv_groups.append(dv_acc + pltpu.roll(dv_acc, HEAD_DIM, 1))
            dkv[band, :LANE] += jnp.where(low_key, dk_groups[0], dk_groups[1])
            dkv[band, LANE:] += jnp.where(low_key, dv_groups[0], dv_groups[1])
            dproj_ref[rows, 2 * A:2 * A + B] = jnp.concatenate(dq_parts, axis=1).astype(BF16)
        last = slice(tq, tq + CHUNK)
        dkv[last, :] += carry[...]
        dproj_ref[:, 2 * A + B:] = dkv[CHUNK:, :].astype(BF16)
        carry[...] = dkv[:CHUNK, :]

        @pl.when(step == n_tiles - 1)
        def _():
            dsink_ref[...] = jnp.sum(sacc[...], axis=1, keepdims=True)

    specs = _attn_specs(tq, A, B, reverse_tiles=n_tiles)
    return _CHAIN.call(
        body, name="attn_bwd", grid=(n_tiles,),
        in_specs=[pl.BlockSpec(memory_space=pltpu.SMEM)] + specs
        + [pl.BlockSpec((tq, B), lambda i: (rev(i), 0)), pl.BlockSpec((tq, 2 * A), lambda i: (rev(i), 0)),
           pl.BlockSpec((H, 2 * CHUNK, CHUNK), lambda i: (0, 0, 0))],
        out_specs=[pl.BlockSpec((tq, P), lambda i: (rev(i), 0)),
                   pl.BlockSpec((H, 2 * CHUNK, CHUNK), lambda i: (0, 0, 0)), pl.BlockSpec((H, 1), lambda i: (0, 0))],
        out_shape=[SDS((T, P), BF16), SDS((H, 2 * CHUNK, CHUNK), F32), SDS((H, 1), F32)],
        scratch_shapes=[pltpu.VMEM((CHUNK, 2 * LANE), F32), pltpu.VMEM((tq + CHUNK, 2 * LANE), F32),
                        pltpu.VMEM((H, LANE), F32)],
        compiler_params=_params(1))(sinks, proj, proj, proj, proj, proj, do, duv, bias_t)


def _bias_bwd(dbias, onehot):
    H = dbias.shape[0]
    nbk = onehot.shape[1]

    def body(d_ref, oh_ref, o_ref):
        hi, mid, lo = _split3(d_ref[...])
        oh = oh_ref[...]
        o_ref[...] = _dot(hi, oh, NN) + _dot(mid, oh, NN) + _dot(lo, oh, NN)

    return _CHAIN.call(body, name="bias_bwd", in_specs=[VMEM_SPEC] * 2, out_specs=VMEM_SPEC, out_shape=SDS((H, nbk), F32),
                       compiler_params=_params(0))(dbias, onehot)


def _inproj_bwd(dproj, w_t, x, dh1, g):
    T, P = dproj.shape
    D = x.shape[1]
    tm = _tile(T, 512)

    def body(dp_ref, w_ref, x_ref, dh_ref, g_ref, dx_ref, dg_ref):
        @pl.when(pl.program_id(0) == 0)
        def _():
            dg_ref[...] = jnp.zeros_like(dg_ref)

        dn = _dot(dp_ref[...], w_ref[...], NN)
        xv = x_ref[...]
        r = _rms_stats(xv)
        dg_ref[...] += jnp.sum(dn * (xv * r), axis=0, keepdims=True)
        dx_ref[...] = dh_ref[...] + _rms_bwd(dn, xv, r, g_ref[...])

    return _CHAIN.call(
        body, name="inproj_bwd", grid=(T // tm,),
        in_specs=[pl.BlockSpec((tm, P), lambda i: (i, 0)), _resident((P, D)),
                  pl.BlockSpec((tm, D), lambda i: (i, 0)), pl.BlockSpec((tm, D), lambda i: (i, 0)),
                  pl.BlockSpec((1, D), lambda i: (0, 0))],
        out_specs=[pl.BlockSpec((tm, D), lambda i: (i, 0)), pl.BlockSpec((1, D), lambda i: (0, 0))],
        out_shape=[SDS((T, D), F32), SDS((1, D), F32)], compiler_params=_params(1))(dproj, w_t, x, dh1, g)


def _adamw(w, g, m, v):
    m = ADAM_B1 * m + (1.0 - ADAM_B1) * g
    v = ADAM_B2 * v + (1.0 - ADAM_B2) * (g * g)
    m_hat = m / (1.0 - ADAM_B1 ** ADAM_STEP)
    v_hat = v / (1.0 - ADAM_B2 ** ADAM_STEP)
    delta = -ADAM_LR * (m_hat / (jnp.sqrt(v_hat) + ADAM_EPS) + ADAM_WD * w)
    return delta, m, v


def _adam_sharded(csum, recv, w, m, v, name):
    R, C = w.shape
    tr = _tile(R, 256, 16)

    def body(own_ref, recv_ref, w_ref, m_ref, v_ref, g_ref, d_ref, nm_ref, nv_ref):
        g = own_ref[...].astype(F32)
        for r in range(3):
            g = g + recv_ref[r].astype(F32)
        delta, nm, nv = _adamw(w_ref[...], g, m_ref[...], v_ref[...])
        g_ref[...] = g
        d_ref[...] = delta
        nm_ref[...] = nm
        nv_ref[...] = nv

    blk = pl.BlockSpec((tr, C), lambda i: (i, 0))
    return _CHAIN.call(
        body, name=name, grid=(R // tr,),
        in_specs=[pl.BlockSpec((None, tr, C), lambda i: (0, i, 0)), pl.BlockSpec((3, tr, C), lambda i: (0, i, 0)),
                  blk, blk, blk],
        out_specs=[blk] * 4, out_shape=[SDS((R, C), F32)] * 4, compiler_params=_params(1))(csum, recv, w, m, v)


def _rows2d(shape):
    return (int(np.prod(shape[:-1])) if len(shape) > 1 else 1, shape[-1])


def _small_layout(shapes):
    totals, places = {}, []
    for s in shapes:
        r, w = _rows2d(s)
        off = totals.get(w, 0)
        places.append((w, off, r))
        totals[w] = off + -(-r // 8) * 8
    return {w: -(-t // 32) * 32 for w, t in totals.items()}, places


def _pack_small(arrays, totals, places):
    bufs = []
    for w, total in totals.items():
        buf = jnp.zeros((total, w), F32)
        for a, (pw, off, r) in zip(arrays, places):
            if pw == w:
                buf = lax.dynamic_update_slice(buf, a.reshape(r, w).astype(F32), (off, 0))
        bufs.append(buf)
    return bufs


def _adam_small(gathered, totals, places, ws, ms, vs):
    widths = list(totals)
    n, nw = len(places), len(widths)

    def body(*refs):
        gath, params, outs = refs[:nw], refs[nw:nw + 3 * n], refs[nw + 3 * n:]
        for p, (w, off, r) in enumerate(places):
            g_ref = gath[widths.index(w)]
            g = g_ref[0, off:off + r, :]
            for d in range(1, N_DEV):
                g = g + g_ref[d, off:off + r, :]
            delta, nm, nv = _adamw(params[p][...], g, params[n + p][...], params[2 * n + p][...])
            for k, val in enumerate((g, delta, nm, nv)):
                outs[4 * p + k][...] = val

    shapes2d = [SDS((r, w), F32) for w, _, r in places for _ in range(4)]
    outs = _CHAIN.call(body, name="adam_small", in_specs=[VMEM_SPEC] * (nw + 3 * n), out_specs=[VMEM_SPEC] * (4 * n),
                       out_shape=shapes2d, compiler_params=_params(0))(*gathered, *ws, *ms, *vs)
    return [outs[4 * p:4 * p + 4] for p in range(n)]


def kernel(x, rel_bias_table, mix_norm_g, w_in, gate_norm_g, gate_norm_b, w_spatial, b_spatial, attn_sinks, out_norm_a_g, out_norm_b_g, w_out, ffn_norm_g, w_up, w_down, final_norm_g, loss_target, m_rel_bias_table, m_mix_norm_g, m_w_in, m_gate_norm_g, m_gate_norm_b, m_w_spatial, m_b_spatial, m_attn_sinks, m_out_norm_a_g, m_out_norm_b_g, m_w_out, m_ffn_norm_g, m_w_up, m_w_down, m_final_norm_g, v_rel_bias_table, v_mix_norm_g, v_w_in, v_gate_norm_g, v_gate_norm_b, v_w_spatial, v_b_spatial, v_attn_sinks, v_out_norm_a_g, v_out_norm_b_g, v_w_out, v_ffn_norm_g, v_w_up, v_w_down, v_final_norm_g):
    T, D = x.shape[1], x.shape[2]
    A = D // 2
    B = D // 2
    H = B // HEAD_DIM
    P = 2 * A + B + 2 * KV_HEADS * HEAD_DIM
    xs = x.reshape(T, D)
    target = loss_target.reshape(T, D)

    win_t, m_win_t, v_win_t = (jnp.swapaxes(a[0], 0, 1) for a in (w_in, m_w_in, v_w_in))
    shards = [win_t.astype(BF16), w_out[0].astype(BF16), w_up[0].astype(BF16), w_down[0].astype(BF16)]
    _CHAIN.token = None
    _handshakes.clear()
    gather = _gather_begin(shards)
    _gather_step(gather, [(0, 0)], "gather_start")

    g1, g2, g3 = mix_norm_g.reshape(1, D), ffn_norm_g.reshape(1, D), final_norm_g.reshape(1, D)
    lg, lb = gate_norm_g.reshape(1, A), gate_norm_b.reshape(1, A)
    ws = w_spatial[0]
    ws_t = jnp.swapaxes(ws, 1, 2)
    bs_t = jnp.transpose(b_spatial[0])
    ga, gb = out_norm_a_g.reshape(1, A), out_norm_b_g.reshape(1, B)
    sinks = attn_sinks.reshape(H)
    bucket, in_window = _t5_bucket()
    onehot_np = ((bucket[:, :, None] == np.arange(N_BUCKETS)) & in_window[:, :, None]).astype(np.float32)
    onehot = jnp.asarray(onehot_np.reshape(-1, N_BUCKETS)).astype(BF16)
    onehot_kq = jnp.asarray(onehot_np.transpose(1, 0, 2).reshape(-1, N_BUCKETS)).astype(BF16)

    bias, bias_t = _bias_fwd(jnp.transpose(rel_bias_table), jnp.transpose(onehot), jnp.transpose(onehot_kq))
    bias, bias_t = bias.reshape(H, CHUNK, 2 * CHUNK), bias_t.reshape(H, 2 * CHUNK, CHUNK)
    n1 = _mix_norm(xs, g1)
    _gather_step(gather, [(0, 1), (1, 0), (2, 0)], "gather_in_1")
    _gather_step(gather, [(0, 2)], "gather_in_2")
    (win_g,) = _gather_end(gather, [0], "gather_in_end")
    win_t_full = win_g.reshape(P, D)
    proj = _inproj_fwd(n1, win_t_full)
    _gather_step(gather, [(1, 1)], "gather_out_1")
    a_out = _gmlp_fwd(proj, lg, lb, ws, bs_t, A)
    _gather_step(gather, [(1, 2), (2, 1), (3, 0)], "gather_out_2_up_1")
    b_out = _attn_fwd(proj, bias, sinks, A, B)
    (wout_g,) = _gather_end(gather, [1], "gather_out_end")
    _gather_step(gather, [(2, 2)], "gather_up_2")
    wout_full = wout_g.reshape(A + B, D)
    h1, mixed, n2 = _outproj_fwd(a_out, b_out, ga, gb, xs, wout_full, g2)
    (wup_g,) = _gather_end(gather, [2], "gather_up_end")
    _gather_step(gather, [(3, 1)], "gather_down_1")
    wup_t = jnp.transpose(wup_g, (0, 2, 1)).reshape(-1, D)
    z = _ffn_up(n2, wup_g)
    _gather_step(gather, [(3, 2)], "gather_down_2")
    (wdown_g,) = _gather_end(gather, [3], "gather_down_end")
    h2 = _ffn_down(h1, z, wdown_g.reshape(-1, D))
    loss_part, dg3, dh2, dh2b = _final_loss(h2, g3, target)

    def reduce_to_chip(state, name):
        csums = [_chip_sum(part, received, "%s_chip_sum_%d" % (name, a))
                 for a, (part, received) in enumerate(_sibling_exchange_end(state, name + "_sib_end"))]
        return _chip_exchange_begin(csums, name + "_chip")

    dwdown = _matmul_tn(z, dh2b, "grad_w_down", square_a=True).reshape(wdown_g.shape)
    dzp = _ffn_down_bwd(dh2b, z, wdown_g.reshape(-1, D))
    dwup = _matmul_tn(n2, dzp, "grad_w_up", col_blocks=N_DEV)
    sib_ffn = _sibling_exchange_begin([dwdown, dwup], "rs_ffn_sib")
    dh1, dh1b, dg2 = _ffn_norm_bwd(_ffn_up_bwd(dzp, wup_t), dh2, h1, g2)
    chip_ffn = reduce_to_chip(sib_ffn, "rs_ffn")
    da, db, dga, dgb = _outproj_bwd(dh1b, wout_full, a_out, b_out, ga, gb)
    dwout = _matmul_tn(mixed, dh1b, "grad_w_out").reshape(wout_g.shape)
    sib_out = _sibling_exchange_begin([dwout], "rs_out_sib")
    duv, dlg, dlb, dws, dbs_t = _gmlp_bwd(proj, da, lg, lb, ws, ws_t, bs_t, A)
    dproj, dbias_t, dsinks = _attn_bwd(proj, db, duv, bias_t, sinks, A, B)
    chip_out = reduce_to_chip(sib_out, "rs_out")
    dwin_t = _matmul_tn(dproj, n1, "grad_w_in").reshape(win_g.shape)
    sib_in = _sibling_exchange_begin([dwin_t], "rs_in_sib")
    dtable_t = _bias_bwd(dbias_t.reshape(H, -1), onehot_kq)
    chip_in = reduce_to_chip(sib_in, "rs_in")
    grad_x, dg1 = _inproj_bwd(dproj, win_t_full, xs, dh1, g1)

    small_w = [rel_bias_table, mix_norm_g, gate_norm_g, gate_norm_b, w_spatial, b_spatial, attn_sinks,
               out_norm_a_g, out_norm_b_g, ffn_norm_g, final_norm_g]
    small_m = [m_rel_bias_table, m_mix_norm_g, m_gate_norm_g, m_gate_norm_b, m_w_spatial, m_b_spatial, m_attn_sinks,
               m_out_norm_a_g, m_out_norm_b_g, m_ffn_norm_g, m_final_norm_g]
    small_v = [v_rel_bias_table, v_mix_norm_g, v_gate_norm_g, v_gate_norm_b, v_w_spatial, v_b_spatial, v_attn_sinks,
               v_out_norm_a_g, v_out_norm_b_g, v_ffn_norm_g, v_final_norm_g]
    small_g = [jnp.transpose(dtable_t), dg1, dlg, dlb, dws, jnp.transpose(dbs_t), dsinks, dga, dgb, dg2, dg3]
    nothing = jnp.zeros((1, H), F32)
    small_w, small_m, small_v = small_w + [nothing], small_m + [nothing], small_v + [nothing]
    small_g = small_g + [jnp.broadcast_to(loss_part, (1, H))]
    shapes = [w.shape for w in small_w]
    totals, places = _small_layout(shapes)
    as_rows = lambda arrays: [a.reshape(_rows2d(a.shape)) for a in arrays]
    big = [None] * 4

    def adam_of(k, state, a, w, m, v):
        csum, received = _chip_exchange_end(state, a, "rs_%d_end" % k)
        big[k] = _adam_sharded(csum, received, w, m, v, "adam_%d" % k)

    small_gather = _gather_begin(_pack_small(small_g, totals, places))
    every = range(len(totals))
    _gather_step(small_gather, [(a, 0) for a in every], "small_gather_start")
    adam_of(3, chip_ffn, 0, w_down[0], m_w_down[0], v_w_down[0])
    adam_of(2, chip_ffn, 1, w_up[0], m_w_up[0], v_w_up[0])
    _gather_step(small_gather, [(a, 1) for a in every], "small_gather_1")
    adam_of(1, chip_out, 0, w_out[0], m_w_out[0], v_w_out[0])
    _gather_step(small_gather, [(a, 2) for a in every], "small_gather_2")
    adam_of(0, chip_in, 0, win_t, m_win_t, v_win_t)
    gathered = _gather_end(small_gather, list(every), "small_gather_end")
    small_out = _adam_small(gathered, totals, places, as_rows(small_w), as_rows(small_m), as_rows(small_v))
    sg, sd, sm, sv = [[outs[k].reshape(s) for outs, s in zip(small_out, shapes)] for k in range(4)]
    big[0] = [jnp.swapaxes(o, 0, 1) for o in big[0]]
    big = [[o.reshape(w.shape) for o in outs] for outs, w in zip(big, (w_in, w_out, w_up, w_down))]

    loss = sg[-1][0, 0]

    order = ["s0", "s1", "b0", "s2", "s3", "s4", "s5", "s6", "s7", "s8", "b1", "s9", "b2", "b3", "s10"]

    def group(idx):
        small = (sg, sd, sm, sv)[idx]
        return [small[int(t[1:])] if t[0] == "s" else big[int(t[1:])][idx] for t in order]

    return (loss, grad_x.reshape(x.shape), *group(0), *group(1), *group(2), *group(3))
```

```python
import math

import numpy as np
import jax
import jax.numpy as jnp
from jax import lax
from jax.experimental import pallas as pl
from jax.experimental.pallas import tpu as pltpu

F32 = jnp.float32
BF16 = jnp.bfloat16
SDS = jax.ShapeDtypeStruct
MESH = pl.DeviceIdType.MESH

N_DEV = 8
EPS = 1e-5
NEG = -1e30
CHUNK = 128
GROUP_DIM = 128
HEAD_DIM = 64
KV_HEADS = 2
N_BUCKETS = 32
MAX_DISTANCE = 128
ADAM_LR, ADAM_B1, ADAM_B2, ADAM_EPS, ADAM_WD, ADAM_STEP = 0.001, 0.9, 0.999, 1e-08, 0.01, 10
GELU_C0 = math.sqrt(2.0 / math.pi)
GELU_C1 = 0.044715

V7X_VMEM_BYTES = 64 * 1024 * 1024
VMEM_LIMIT = V7X_VMEM_BYTES - 8 * 1024 * 1024
LANE = 128

NN = ((1,), (0,))
NT = ((1,), (1,))
TN = ((0,), (0,))


def _dot(a, b, dims):
    return lax.dot_general(a, b, (dims, ((), ())), preferred_element_type=F32)


def _tile(n, pref, unit=LANE):
    best = None
    for t in range(unit, min(n, pref) + 1, unit):
        if n % t == 0:
            best = t
    return n if best is None else best


def _params(n_grid):
    return pltpu.CompilerParams(dimension_semantics=("arbitrary",) * n_grid, vmem_limit_bytes=VMEM_LIMIT)


def _resident(shape):
    return pl.BlockSpec(shape, lambda i: (0, 0), pipeline_mode=pl.Buffered(1))


def _gelu(x):
    return 0.5 * x * (1.0 + jnp.tanh(GELU_C0 * (x + GELU_C1 * x * x * x)))


def _gelu_and_grad(x):
    x2 = x * x
    t = jnp.tanh(GELU_C0 * x * (1.0 + GELU_C1 * x2))
    val = 0.5 * x * (1.0 + t)
    grad = 0.5 * (1.0 + t) + 0.5 * x * (1.0 - t * t) * (GELU_C0 * (1.0 + 3.0 * GELU_C1 * x2))
    return val, grad


def _rms_stats(x):
    return lax.rsqrt(jnp.mean(x * x, axis=-1, keepdims=True) + EPS)


def _rms_bwd(dy, x, r, g):
    w = dy * g
    return r * w - x * (r * r * r) * jnp.mean(w * x, axis=-1, keepdims=True)


def _t5_bucket():
    i = np.arange(CHUNK)[:, None]
    j = np.arange(2 * CHUNK)[None, :]
    rel = np.maximum(i + CHUNK - j, 0)
    n_exact = N_BUCKETS // 2
    relf = np.maximum(rel, n_exact).astype(np.float32)
    large = n_exact + (np.log(relf / np.float32(n_exact)) / np.float32(math.log(MAX_DISTANCE / n_exact))
                       * np.float32(N_BUCKETS - n_exact)).astype(np.int32)
    large = np.minimum(large, N_BUCKETS - 1)
    bucket = np.where(rel < n_exact, rel, large)
    in_window = (i + CHUNK - j >= 0) & (i + CHUNK - j < CHUNK)
    return bucket.astype(np.int32), in_window


def _split3(x):
    hi = x.astype(BF16)
    r1 = x - hi.astype(F32)
    mid = r1.astype(BF16)
    lo = (r1 - mid.astype(F32)).astype(BF16)
    return hi, mid, lo


HBM_SPEC = pl.BlockSpec(memory_space=pltpu.HBM)


def _mesh_pos():
    return lax.axis_index("x"), lax.axis_index("y"), lax.axis_index("c")


def _dev_index(px, py, pc):
    return 4 * px + 2 * py + pc


SEM_SPEC = pl.BlockSpec(memory_space=pltpu.SEMAPHORE)
ANY_SPEC = pl.BlockSpec(memory_space=pl.ANY)
VMEM_SPEC = pl.BlockSpec(memory_space=pltpu.VMEM)
TOKEN = SDS((8, LANE), F32)
SIDE_EFFECT = pltpu.SideEffectType.DATAFLOW_SIDE_EFFECTING


def _hbm(x):
    return pltpu.with_memory_space_constraint(x, pltpu.HBM)


class _CallChain:
    def __init__(self):
        self.token = None

    def call(self, body, *, in_specs, out_specs, out_shape, **kwargs):
        dep, n_in = self.token, len(in_specs)
        single = not isinstance(out_shape, (list, tuple))
        out_shapes = [out_shape] if single else list(out_shape)
        out_specs = [out_specs] if single else list(out_specs)
        n_out = len(out_shapes)
        n_dep = 0 if dep is None else 1
        token_spec = pl.BlockSpec((8, LANE), lambda *_: (0, 0)) if kwargs.get("grid") else VMEM_SPEC

        def chained(*refs):
            outs_at = n_in + n_dep
            body(*refs[:n_in], *refs[outs_at:outs_at + n_out], *refs[outs_at + n_out + 1:])
            token = refs[outs_at + n_out]
            token[...] = jnp.zeros_like(token)

        inner = pl.pallas_call(chained, in_specs=list(in_specs) + [ANY_SPEC] * n_dep, out_specs=out_specs + [token_spec],
                               out_shape=out_shapes + [TOKEN], **kwargs)

        def run(*operands):
            outs = inner(*operands) if dep is None else inner(*operands, dep)
            self.token = outs[n_out]
            return outs[0] if single else list(outs[:n_out])

        return run


_CHAIN = _CallChain()


def _wait_all(waits, x, y, c):
    for kind, src, dst, send_sem, recv_sem in waits:
        cp = pltpu.make_async_remote_copy(src_ref=src, dst_ref=dst, send_sem=send_sem, recv_sem=recv_sem,
                                          device_id=(x, y, c), device_id_type=MESH)
        if kind == "send":
            cp.wait_send()
        else:
            cp.wait_recv()


PEER_SETS = {"sibling": 0, "near": 1, "chips": 2}
_handshakes = {}


def _handshake(peer_set):
    x, y, c = _mesh_pos()
    peers = {"sibling": [(x, y, 1 - c)],
             "near": [(x, y, 1 - c), (1 - x, y, c), (x, 1 - y, c)],
             "chips": [(1 - x, y, c), (x, 1 - y, c), (1 - x, 1 - y, c)]}[peer_set]
    barrier = pltpu.get_barrier_semaphore()
    for peer in peers:
        pl.semaphore_signal(barrier, inc=1, device_id=peer, device_id_type=MESH)
    pl.semaphore_wait(barrier, len(peers))


def _split_start(bufs, copies_of, n_sems, name, peer_set, sem_sets=(), waits_of=None):
    n, ns = len(bufs), len(sem_sets)
    flat_sems = [s for pair in sem_sets for s in pair]
    uses = _handshakes.get(peer_set, 0)
    _handshakes[peer_set] = uses + 1
    collective_id = 3 * PEER_SETS[peer_set] + uses % 3

    def body(*refs):
        ins = refs[:n]
        sems = refs[n:n + 2 * ns]
        send_sems, recv_sems = refs[n + 2 * ns], refs[n + 2 * ns + 1]
        _handshake(peer_set)
        if waits_of is not None:
            _wait_all(waits_of(ins, [(sems[2 * i], sems[2 * i + 1]) for i in range(ns)]), *_mesh_pos())
        for src, dst, k, target in copies_of(ins):
            pltpu.make_async_remote_copy(src_ref=src, dst_ref=dst, send_sem=send_sems.at[k], recv_sem=recv_sems.at[k],
                                         device_id=target, device_id_type=MESH).start()

    outs = _CHAIN.call(
        body, name=name,
        out_shape=[pltpu.SemaphoreType.DMA((n_sems,)), pltpu.SemaphoreType.DMA((n_sems,))]
        + [pltpu.HBM(b.shape, b.dtype) for b in bufs],
        in_specs=[HBM_SPEC] * n + [SEM_SPEC] * (2 * ns), out_specs=[SEM_SPEC, SEM_SPEC] + [HBM_SPEC] * n,
        input_output_aliases={a: 2 + a for a in range(n)},
        compiler_params=pltpu.CompilerParams(has_side_effects=SIDE_EFFECT, collective_id=collective_id),
    )(*[_hbm(b) for b in bufs], *flat_sems)
    return outs[0], outs[1], list(outs[2:2 + n])


def _split_wait(bufs, sem_sets, waits_of, name):
    n, ns = len(bufs), len(sem_sets)
    flat_sems = [s for pair in sem_sets for s in pair]

    def body(*refs):
        ins = refs[:n]
        sems = refs[n:n + 2 * ns]
        _wait_all(waits_of(ins, [(sems[2 * i], sems[2 * i + 1]) for i in range(ns)]), *_mesh_pos())

    outs = _CHAIN.call(
        body, name=name,
        out_shape=[pltpu.HBM(b.shape, b.dtype) for b in bufs],
        in_specs=[HBM_SPEC] * n + [SEM_SPEC] * (2 * ns), out_specs=[HBM_SPEC] * n,
        input_output_aliases={a: a for a in range(n)},
        compiler_params=pltpu.CompilerParams(has_side_effects=SIDE_EFFECT),
    )(*bufs, *flat_sems)
    return list(outs)


def _gather_blocks(land):
    rows = land.shape[1]
    first = (rows // 2) // 16 * 16

    def block(px, py, pc):
        return land.at[_dev_index(px, py, pc)]

    def halves(px, py, pc):
        return (land.at[_dev_index(px, py, pc), pl.ds(0, first)], land.at[_dev_index(px, py, pc), pl.ds(first, rows - first)])

    return block, halves


def _gather_begin(shards):
    me = _dev_index(*_mesh_pos())
    lands = [lax.dynamic_update_index_in_dim(lax.empty((N_DEV,) + s.shape, s.dtype), s, me, 0) for s in shards]
    return dict(lands=lands, stage={})


STAGE_COPIES = (3, 4, 1)


def _gather_step(state, items, name):
    which = sorted({a for a, _ in items})
    at = {a: i for i, a in enumerate(which)}
    sem_sets = [state["stage"][(a, s - 1)][0] for a, s in items if s > 0]
    offset, n_sems = {}, 0
    for a, s in items:
        offset[(a, s)] = n_sems
        n_sems += STAGE_COPIES[s]

    def waits_of(ins, sems):
        x, y, c = _mesh_pos()
        out, earlier = [], 0
        for a, s in items:
            if s == 0:
                continue
            block, halves = _gather_blocks(ins[at[a]])
            send, recv = sems[earlier]
            off = state["stage"][(a, s - 1)][1]
            earlier += 1
            if s == 1:
                arrived = [(1, block(1 - x, y, c)), (2, block(x, 1 - y, c))]
            else:
                arrived = list(zip((2, 3), halves(1 - x, 1 - y, c)))
            out += [("recv", ref, ref, send.at[off + k], recv.at[off + k]) for k, ref in arrived]
        return out

    def copies_of(ins):
        x, y, c = _mesh_pos()
        sibling = (x, y, 1 - c)
        out = []
        for a, s in items:
            block, halves = _gather_blocks(ins[at[a]])
            off = offset[(a, s)]
            if s == 0:
                mine = block(x, y, c)
                out += [(mine, mine, off + 1, (1 - x, y, c)), (mine, mine, off + 2, (x, 1 - y, c)), (mine, mine, off, sibling)]
            elif s == 1:
                from_x, from_y = block(1 - x, y, c), block(x, 1 - y, c)
                out += [(halves(1 - x, y, c)[0], halves(1 - x, y, c)[0], off + 2, (x, 1 - y, c)),
                        (halves(x, 1 - y, c)[1], halves(x, 1 - y, c)[1], off + 3, (1 - x, y, c)),
                        (from_x, from_x, off, sibling), (from_y, from_y, off + 1, sibling)]
            else:
                diag = block(1 - x, 1 - y, c)
                out.append((diag, diag, off, sibling))
        return out

    send_sems, recv_sems, bufs = _split_start([state["lands"][a] for a in which], copies_of, n_sems, name, "near",
                                              sem_sets=sem_sets, waits_of=waits_of)
    for a in which:
        state["lands"][a] = bufs[at[a]]
    for a, s in items:
        state["stage"][(a, s)] = ((send_sems, recv_sems), offset[(a, s)])


def _gather_end(state, which, name):
    sem_sets = [state["stage"][(a, s)][0] for a in which for s in range(3)]

    def waits(ins, sems):
        x, y, c = _mesh_pos()
        out = []
        for i, a in enumerate(which):
            block, halves = _gather_blocks(ins[i])
            (b_send, b_recv), (s1_send, s1_recv), (s2_send, s2_recv) = sems[3 * i:3 * i + 3]
            o0, o1, o2 = (state["stage"][(a, s)][1] for s in range(3))
            arrivals = [(block(x, y, 1 - c), b_send, b_recv, o0),
                        (block(1 - x, y, 1 - c), s1_send, s1_recv, o1), (block(x, 1 - y, 1 - c), s1_send, s1_recv, o1 + 1),
                        (block(1 - x, 1 - y, 1 - c), s2_send, s2_recv, o2)]
            mine = block(x, y, c)
            sent = [(mine, b_send, b_recv, o0 + k) for k in range(3)]
            sent += [(block(1 - x, y, c), s1_send, s1_recv, o1), (block(x, 1 - y, c), s1_send, s1_recv, o1 + 1),
                     (halves(1 - x, y, c)[0], s1_send, s1_recv, o1 + 2), (halves(x, 1 - y, c)[1], s1_send, s1_recv, o1 + 3),
                     (block(1 - x, 1 - y, c), s2_send, s2_recv, o2)]
            out += [("recv", ref, ref, s.at[k], r.at[k]) for ref, s, r, k in arrivals]
            out += [("send", ref, ref, s.at[k], r.at[k]) for ref, s, r, k in sent]
        return out

    bufs = _split_wait([state["lands"][a] for a in which], sem_sets, waits, name)
    for i, a in enumerate(which):
        state["lands"][a] = bufs[i]
    return bufs


def _sibling_exchange_begin(parts, name):
    lands = [lax.empty((4,) + p.shape[1:], p.dtype) for p in parts]
    n = len(parts)

    def copies_of(ins):
        x, y, c = _mesh_pos()
        return [(ins[a].at[2 * j + (1 - c)], ins[n + a].at[j], 4 * a + j, (x, y, 1 - c)) for a in range(n) for j in range(4)]

    send_sems, recv_sems, bufs = _split_start(list(parts) + lands, copies_of, 4 * n, name, "sibling")
    return dict(bufs=bufs, sems=(send_sems, recv_sems), n=n)


def _sibling_exchange_end(state, name):
    n = state["n"]

    def waits(ins, sems):
        _, _, c = _mesh_pos()
        return [(kind, ins[a].at[2 * j + (1 - c)], ins[n + a].at[j], sems[0][0].at[4 * a + j], sems[0][1].at[4 * a + j])
                for a in range(n) for j in range(4) for kind in ("send", "recv")]

    bufs = _split_wait(state["bufs"], [state["sems"]], waits, name)
    return [(bufs[a], bufs[n + a]) for a in range(n)]


CHIP_FLIPS = (2, 1, 3)


def _chip_exchange_begin(csums, name):
    lands = [lax.empty((3,) + s.shape[1:], s.dtype) for s in csums]
    n = len(csums)

    def copies_of(ins):
        x, y, c = _mesh_pos()
        chips = [(1 - x, y), (x, 1 - y), (1 - x, 1 - y)]
        return [(ins[a].at[CHIP_FLIPS[r]], ins[n + a].at[r], 3 * a + r, (px, py, c))
                for a in range(n) for r, (px, py) in enumerate(chips)]

    send_sems, recv_sems, bufs = _split_start(list(csums) + lands, copies_of, 3 * n, name, "chips")
    return dict(bufs=bufs, sems=(send_sems, recv_sems), n=n)


def _chip_exchanges_end(states, name):
    starts = np.cumsum([0] + [2 * s["n"] for s in states])

    def waits(ins, sems):
        return [(kind, ins[off + a].at[CHIP_FLIPS[r]], ins[off + s["n"] + a].at[r],
                 sems[i][0].at[3 * a + r], sems[i][1].at[3 * a + r])
                for i, (s, off) in enumerate(zip(states, starts)) for a in range(s["n"])
                for r in range(3) for kind in ("send", "recv")]

    bufs = _split_wait([b for s in states for b in s["bufs"]], [s["sems"] for s in states], waits, name)
    return [(bufs[off + a], bufs[off + s["n"] + a]) for s, off in zip(states, starts) for a in range(s["n"])]


def _chip_sum(part, recv, name):
    _, R, C = part.shape
    tr = _tile(R, 1024, 16)
    place = jnp.stack([lax.axis_index("c"), 2 * lax.axis_index("x") + lax.axis_index("y")]).astype(jnp.int32)

    def body(place_ref, p_ref, r_ref, o_ref):
        o_ref[...] = (p_ref[...].astype(F32) + r_ref[...].astype(F32)).astype(o_ref.dtype)

    def chip(p, place_ref):
        return jnp.bitwise_xor(p, place_ref[1])

    grid_spec = pltpu.PrefetchScalarGridSpec(
        num_scalar_prefetch=1, grid=(4, R // tr),
        in_specs=[pl.BlockSpec((None, tr, C), lambda p, i, place_ref: (2 * chip(p, place_ref) + place_ref[0], i, 0)),
                  pl.BlockSpec((None, tr, C), lambda p, i, place_ref: (chip(p, place_ref), i, 0))],
        out_specs=pl.BlockSpec((None, tr, C), lambda p, i, place_ref: (p, i, 0)))
    return pl.pallas_call(body, name=name, grid_spec=grid_spec, out_shape=SDS((4, R, C), part.dtype),
                          compiler_params=_params(2))(place, part, recv)


def _bias_fwd(table_t, onehot_t, onehot_kq_t):
    H = table_t.shape[0]
    n = onehot_t.shape[1]

    def body(t_ref, oh_ref, oh_kq_ref, o_ref, o_kq_ref):
        hi, mid, lo = _split3(t_ref[...])
        for src, dst in ((oh_ref, o_ref), (oh_kq_ref, o_kq_ref)):
            oh = src[...]
            dst[...] = _dot(hi, oh, NN) + _dot(mid, oh, NN) + _dot(lo, oh, NN)

    return _CHAIN.call(body, name="bias_fwd", in_specs=[VMEM_SPEC] * 3, out_specs=[VMEM_SPEC] * 2,
                       out_shape=[SDS((H, n), F32)] * 2, compiler_params=_params(0))(table_t, onehot_t, onehot_kq_t)


def _mix_norm(x, g):
    T, D = x.shape
    tm = _tile(T, 512)

    def body(x_ref, g_ref, n_ref):
        xv = x_ref[...]
        n_ref[...] = (xv * _rms_stats(xv) * g_ref[...]).astype(BF16)

    row = pl.BlockSpec((tm, D), lambda i: (i, 0))
    return _CHAIN.call(body, name="mix_norm", grid=(T // tm,), in_specs=[row, pl.BlockSpec((1, D), lambda i: (0, 0))],
                       out_specs=row, out_shape=SDS((T, D), BF16), compiler_params=_params(1))(x, g)


def _inproj_fwd(n, w_t):
    T, D = n.shape
    P = w_t.shape[0]
    tm = _tile(T, 512)

    def body(n_ref, w_ref, proj_ref):
        proj_ref[...] = _dot(n_ref[...], w_ref[...], NT)

    return _CHAIN.call(
        body, name="inproj_fwd", grid=(T // tm,),
        in_specs=[pl.BlockSpec((tm, D), lambda i: (i, 0)), _resident((P, D))],
        out_specs=pl.BlockSpec((tm, P), lambda i: (i, 0)),
        out_shape=SDS((T, P), F32), compiler_params=_params(1))(n, w_t)


def _layer_norm_group(vg, lg, lb):
    mu = jnp.mean(vg, axis=-1, keepdims=True)
    xc = vg - mu
    rstd = lax.rsqrt(jnp.mean(xc * xc, axis=-1, keepdims=True) + EPS)
    vhat = xc * rstd
    return vhat, rstd, vhat * lg + lb


def _gmlp_fwd(proj, lg, lb, w_s, bs_t, A):
    T = proj.shape[0]
    G = A // GROUP_DIM
    tm = _tile(T, 512)
    nc = tm // CHUNK

    def body(u_ref, v_ref, lg_ref, lb_ref, w_ref, bst_ref, a_ref):
        row = lax.broadcasted_iota(jnp.int32, (CHUNK, CHUNK), 0)
        col = lax.broadcasted_iota(jnp.int32, (CHUNK, CHUNK), 1)
        causal = row >= col
        for g in range(G):
            sl = slice(g * GROUP_DIM, (g + 1) * GROUP_DIM)
            _, _, vn = _layer_norm_group(_gelu(v_ref[:, sl]), lg_ref[:, sl], lb_ref[:, sl])
            vnb = vn.astype(BF16)
            wm = jnp.where(causal, w_ref[g], 0.0).astype(BF16)
            ug = _gelu(u_ref[:, sl])
            bcol = bst_ref[:, g:g + 1]
            for c in range(nc):
                rs = slice(c * CHUNK, (c + 1) * CHUNK)
                a_ref[rs, sl] = ug[rs] * (_dot(wm, vnb[rs], NN) + bcol)

    return _CHAIN.call(
        body, name="gmlp_fwd", grid=(T // tm,),
        in_specs=[pl.BlockSpec((tm, A), lambda i: (i, 0)), pl.BlockSpec((tm, A), lambda i: (i, 1)),
                  pl.BlockSpec((1, A), lambda i: (0, 0)), pl.BlockSpec((1, A), lambda i: (0, 0)),
                  pl.BlockSpec((G, CHUNK, CHUNK), lambda i: (0, 0, 0)), pl.BlockSpec((CHUNK, G), lambda i: (0, 0))],
        out_specs=pl.BlockSpec((tm, A), lambda i: (i, 0)),
        out_shape=SDS((T, A), F32), compiler_params=_params(1))(proj, proj, lg, lb, w_s, bs_t)


def _attn_masks(first_tile):
    ii = lax.broadcasted_iota(jnp.int32, (CHUNK, 2 * CHUNK), 0)
    jj = lax.broadcasted_iota(jnp.int32, (CHUNK, 2 * CHUNK), 1)
    in_window = (jj > ii) & (jj <= ii + CHUNK)
    first_mask = in_window & jnp.logical_or(jnp.logical_not(first_tile), jj >= CHUNK)
    return in_window, first_mask


def _softmax_with_sink(s, sink, axis):
    m = jnp.maximum(jnp.max(s, axis=axis, keepdims=True), sink)
    p = jnp.exp(s - m)
    e_sink = jnp.exp(sink - m)
    inv = 1.0 / (jnp.sum(p, axis=axis, keepdims=True) + e_sink)
    return p * inv, e_sink * inv


def _pad_heads(band, group):
    lane = lax.broadcasted_iota(jnp.int32, band.shape, 1)
    if group == 0:
        low = jnp.where(lane < HEAD_DIM, band, 0.0)
        high = pltpu.roll(low, HEAD_DIM, 1)
    else:
        high = jnp.where(lane >= HEAD_DIM, band, 0.0)
        low = pltpu.roll(high, HEAD_DIM, 1)
    return low.astype(BF16), high.astype(BF16)


def _attn_specs(tq, A, B, reverse_tiles=None):
    nb = tq // CHUNK
    kcol = (2 * A + B) // LANE
    if reverse_tiles is None:
        tile = lambda i: i
    else:
        tile = lambda i: reverse_tiles - 1 - i
    prev = lambda i: jnp.maximum(tile(i) * nb - 1, 0)
    return [pl.BlockSpec((tq, B), lambda i: (tile(i), 2 * A // B)),
            pl.BlockSpec((tq, LANE), lambda i: (tile(i), kcol)),
            pl.BlockSpec((tq, LANE), lambda i: (tile(i), kcol + 1)),
            pl.BlockSpec((CHUNK, LANE), lambda i: (prev(i), kcol)),
            pl.BlockSpec((CHUNK, LANE), lambda i: (prev(i), kcol + 1))]


def _attn_fwd(proj, bias, sinks, A, B):
    T = proj.shape[0]
    H = B // HEAD_DIM
    qpk = H // KV_HEADS
    tq = _tile(T, 512)
    nb = tq // CHUNK

    scale = HEAD_DIM ** -0.5

    def body(sink_ref, q_ref, k_ref, v_ref, kp_ref, vp_ref, bias_ref, o_ref):
        in_window, first_mask = _attn_masks(pl.program_id(0) == 0)
        for b in range(nb):
            rows = slice(b * CHUNK, (b + 1) * CHUNK)
            if b == 0:
                kprev, vprev, mask = kp_ref[...], vp_ref[...], first_mask
            else:
                prows = slice((b - 1) * CHUNK, b * CHUNK)
                kprev, vprev, mask = k_ref[prows, :], v_ref[prows, :], in_window
            kband = jnp.concatenate([kprev, k_ref[rows, :]], axis=0)
            vband = jnp.concatenate([vprev, v_ref[rows, :]], axis=0)
            k_pads = [_pad_heads(kband, g) for g in range(KV_HEADS)]
            v_both = [jnp.concatenate(_pad_heads(vband, g), axis=0) for g in range(KV_HEADS)]
            scores = []
            for pair in range(H // 2):
                h = 2 * pair
                qs = (q_ref[rows, h * HEAD_DIM:(h + 2) * HEAD_DIM] * scale).astype(BF16)
                scores += [_dot(qs, kz, NT) for kz in k_pads[h // qpk]]
            probs = [_softmax_with_sink(jnp.where(mask, s + bias_ref[h], NEG), sink_ref[h], -1)[0].astype(BF16)
                     for h, s in enumerate(scores)]
            outs = [_dot(jnp.concatenate(probs[h:h + 2], axis=1), v_both[h // qpk], NN) for h in range(0, H, 2)]
            o_ref[rows, :] = jnp.concatenate(outs, axis=1)

    return _CHAIN.call(
        body, name="attn_fwd", grid=(T // tq,),
        in_specs=[pl.BlockSpec(memory_space=pltpu.SMEM)] + _attn_specs(tq, A, B)
        + [pl.BlockSpec((H, CHUNK, 2 * CHUNK), lambda i: (0, 0, 0))],
        out_specs=pl.BlockSpec((tq, B), lambda i: (i, 0)),
        out_shape=SDS((T, B), F32), compiler_params=_params(1))(sinks, proj, proj, proj, proj, proj, bias)


def _outproj_fwd(a, b, ga, gb, x, w, g_ffn):
    T, A = a.shape
    B = b.shape[1]
    D = x.shape[1]
    tm = _tile(T, 512)

    def body(a_ref, b_ref, ga_ref, gb_ref, x_ref, w_ref, gf_ref, h_ref, mix_ref, n_ref):
        av, bv = a_ref[...], b_ref[...]
        mix_ref[:, :A] = (av * _rms_stats(av) * ga_ref[...]).astype(BF16)
        mix_ref[:, A:] = (bv * _rms_stats(bv) * gb_ref[...]).astype(BF16)
        hv = x_ref[...] + _dot(mix_ref[...], w_ref[...], NN)
        h_ref[...] = hv
        n_ref[...] = (hv * _rms_stats(hv) * gf_ref[...]).astype(BF16)

    row = pl.BlockSpec((tm, D), lambda i: (i, 0))
    return _CHAIN.call(
        body, name="outproj_fwd", grid=(T // tm,),
        in_specs=[pl.BlockSpec((tm, A), lambda i: (i, 0)), pl.BlockSpec((tm, B), lambda i: (i, 0)),
                  pl.BlockSpec((1, A), lambda i: (0, 0)), pl.BlockSpec((1, B), lambda i: (0, 0)),
                  row, _resident((A + B, D)), pl.BlockSpec((1, D), lambda i: (0, 0))],
        out_specs=[row, pl.BlockSpec((tm, A + B), lambda i: (i, 0)), row],
        out_shape=[SDS((T, D), F32), SDS((T, A + B), BF16), SDS((T, D), BF16)],
        compiler_params=_params(1))(a, b, ga, gb, x, w, g_ffn)


def _ffn_up(n, w_up):
    T, D = n.shape
    Fb = w_up.shape[2]
    F = N_DEV * Fb
    tm, tf = _tile(T, 1024), _tile(Fb, 1024)
    per = Fb // tf

    def body(n_ref, wu_ref, z_ref):
        z_ref[...] = jnp.maximum(_dot(n_ref[...], wu_ref[...], NN), 0.0).astype(BF16)

    return _CHAIN.call(
        body, name="ffn_up", grid=(T // tm, F // tf),
        in_specs=[pl.BlockSpec((tm, D), lambda i, j: (i, 0)),
                  pl.BlockSpec((None, D, tf), lambda i, j: (j // per, 0, j % per))],
        out_specs=pl.BlockSpec((tm, tf), lambda i, j: (i, j)),
        out_shape=SDS((T, F), BF16), compiler_params=_params(2))(n, w_up)


def _ffn_down(h1, z, w_down):
    T, D = h1.shape
    F = w_down.shape[0]
    tm, tn, tk = _tile(T, 1024), _tile(D, 1024), _tile(F, 4096)

    def body(h_ref, z_ref, wd_ref, h2_ref):
        k = pl.program_id(2)

        @pl.when(k == 0)
        def _():
            h2_ref[...] = h_ref[...]

        zf = z_ref[...].astype(F32)
        h2_ref[...] += _dot((zf * zf).astype(BF16), wd_ref[...], NN)

    return _CHAIN.call(
        body, name="ffn_down", grid=(T // tm, D // tn, F // tk),
        in_specs=[pl.BlockSpec((tm, tn), lambda i, j, k: (i, j)), pl.BlockSpec((tm, tk), lambda i, j, k: (i, k)),
                  pl.BlockSpec((tk, tn), lambda i, j, k: (k, j))],
        out_specs=pl.BlockSpec((tm, tn), lambda i, j, k: (i, j)),
        out_shape=SDS((T, D), F32), compiler_params=_params(3))(h1, z, w_down)


def _final_loss(h2, g, target):
    T, D = h2.shape
    tm = _tile(T, 512)

    def body(h_ref, g_ref, t_ref, loss_ref, dg_ref, dh_ref, dhb_ref):
        @pl.when(pl.program_id(0) == 0)
        def _():
            loss_ref[...] = jnp.zeros_like(loss_ref)
            dg_ref[...] = jnp.zeros_like(dg_ref)

        hv, gv = h_ref[...], g_ref[...]
        r = _rms_stats(hv)
        hn = hv * r
        e = hn * gv - t_ref[...]
        loss_ref[...] += (0.5 / D) * jnp.sum(jnp.sum(e * e, axis=0, keepdims=True), axis=-1, keepdims=True)
        dy = e * (1.0 / D)
        dg_ref[...] += jnp.sum(dy * hn, axis=0, keepdims=True)
        dh = _rms_bwd(dy, hv, r, gv)
        dh_ref[...] = dh
        dhb_ref[...] = dh.astype(BF16)

    return _CHAIN.call(
        body, name="final_loss", grid=(T // tm,),
        in_specs=[pl.BlockSpec((tm, D), lambda i: (i, 0)), pl.BlockSpec((1, D), lambda i: (0, 0)),
                  pl.BlockSpec((tm, D), lambda i: (i, 0))],
        out_specs=[pl.BlockSpec((1, 1), lambda i: (0, 0)), pl.BlockSpec((1, D), lambda i: (0, 0)),
                   pl.BlockSpec((tm, D), lambda i: (i, 0)), pl.BlockSpec((tm, D), lambda i: (i, 0))],
        out_shape=[SDS((1, 1), F32), SDS((1, D), F32), SDS((T, D), F32), SDS((T, D), BF16)],
        compiler_params=_params(1))(h2, g, target)


def _ffn_down_bwd(dh2b, z, w_down):
    T, D = dh2b.shape
    F = w_down.shape[0]
    tm, tf = _tile(T, 1024), _tile(F, 1024)

    def body(dh_ref, z_ref, wd_ref, dzp_ref):
        dzz = _dot(dh_ref[...], wd_ref[...], NT)
        dzp_ref[...] = (dzz * (2.0 * z_ref[...].astype(F32))).astype(BF16)

    return _CHAIN.call(
        body, name="ffn_down_bwd", grid=(T // tm, F // tf),
        in_specs=[pl.BlockSpec((tm, D), lambda i, j: (i, 0)), pl.BlockSpec((tm, tf), lambda i, j: (i, j)),
                  pl.BlockSpec((tf, D), lambda i, j: (j, 0))],
        out_specs=pl.BlockSpec((tm, tf), lambda i, j: (i, j)),
        out_shape=SDS((T, F), BF16), compiler_params=_params(2))(dh2b, z, w_down)


def _ffn_up_bwd(dzp, w_up_t):
    T, F = dzp.shape
    D = w_up_t.shape[1]
    tm, tn, tk = _tile(T, 1024), _tile(D, 1024), _tile(F, 4096)

    def body(dzp_ref, w_ref, dn_ref):
        part = _dot(dzp_ref[...], w_ref[...], NN)

        @pl.when(pl.program_id(2) == 0)
        def _():
            dn_ref[...] = part

        @pl.when(pl.program_id(2) > 0)
        def _():
            dn_ref[...] += part

    return _CHAIN.call(
        body, name="ffn_up_bwd", grid=(T // tm, D // tn, F // tk),
        in_specs=[pl.BlockSpec((tm, tk), lambda i, j, k: (i, k)), pl.BlockSpec((tk, tn), lambda i, j, k: (k, j))],
        out_specs=pl.BlockSpec((tm, tn), lambda i, j, k: (i, j)),
        out_shape=SDS((T, D), F32), compiler_params=_params(3))(dzp, w_up_t)


def _ffn_norm_bwd(dn, dh2, h1, g):
    T, D = h1.shape
    tm = _tile(T, 512)

    def body(dn_ref, dh_ref, h_ref, g_ref, dh1_ref, dh1b_ref, dg_ref):
        @pl.when(pl.program_id(0) == 0)
        def _():
            dg_ref[...] = jnp.zeros_like(dg_ref)

        hv, dnv = h_ref[...], dn_ref[...]
        r = _rms_stats(hv)
        dg_ref[...] += jnp.sum(dnv * (hv * r), axis=0, keepdims=True)
        dh1 = dh_ref[...] + _rms_bwd(dnv, hv, r, g_ref[...])
        dh1_ref[...] = dh1
        dh1b_ref[...] = dh1.astype(BF16)

    row = pl.BlockSpec((tm, D), lambda i: (i, 0))
    vec = pl.BlockSpec((1, D), lambda i: (0, 0))
    return _CHAIN.call(
        body, name="ffn_norm_bwd", grid=(T // tm,), in_specs=[row, row, row, vec], out_specs=[row, row, vec],
        out_shape=[SDS((T, D), F32), SDS((T, D), BF16), SDS((1, D), F32)], compiler_params=_params(1))(dn, dh2, h1, g)


def _matmul_tn(a, b, name, square_a=False, col_blocks=None):
    T, K = a.shape
    N = b.shape[1]
    tk = _tile(K, 1792)
    tn = _tile(N if col_blocks is None else N // col_blocks, 1024 if tk <= 1024 else 512)

    def body(a_ref, b_ref, o_ref):
        av = a_ref[...]
        if square_a:
            af = av.astype(F32)
            av = (af * af).astype(BF16)
        o_ref[...] = _dot(av, b_ref[...], TN).astype(o_ref.dtype)

    if col_blocks is None:
        out_shape = SDS((K, N), BF16)
        out_spec = pl.BlockSpec((tk, tn), lambda i, j: (i, j))
    else:
        per = (N // col_blocks) // tn
        out_shape = SDS((col_blocks, K, N // col_blocks), BF16)
        out_spec = pl.BlockSpec((None, tk, tn), lambda i, j: (j // per, i, j % per))
    return _CHAIN.call(
        body, name=name, grid=(K // tk, N // tn),
        in_specs=[pl.BlockSpec((T, tk), lambda i, j: (0, i)), pl.BlockSpec((T, tn), lambda i, j: (0, j))],
        out_specs=out_spec, out_shape=out_shape, compiler_params=_params(2))(a, b)


def _outproj_bwd(dh1b, w, a, b, ga, gb):
    T, D = dh1b.shape
    A, B = a.shape[1], b.shape[1]
    tm = _tile(T, 512)

    def body(dh_ref, w_ref, a_ref, b_ref, ga_ref, gb_ref, da_ref, db_ref, dga_ref, dgb_ref):
        @pl.when(pl.program_id(0) == 0)
        def _():
            dga_ref[...] = jnp.zeros_like(dga_ref)
            dgb_ref[...] = jnp.zeros_like(dgb_ref)

        dmix = _dot(dh_ref[...], w_ref[...], NT)
        for src_ref, g_ref, dx_ref, dg_ref, dn in ((a_ref, ga_ref, da_ref, dga_ref, dmix[:, :A]),
                                                   (b_ref, gb_ref, db_ref, dgb_ref, dmix[:, A:])):
            xv = src_ref[...]
            r = _rms_stats(xv)
            dg_ref[...] += jnp.sum(dn * (xv * r), axis=0, keepdims=True)
            dx_ref[...] = _rms_bwd(dn, xv, r, g_ref[...])

    return _CHAIN.call(
        body, name="outproj_bwd", grid=(T // tm,),
        in_specs=[pl.BlockSpec((tm, D), lambda i: (i, 0)), _resident((A + B, D)),
                  pl.BlockSpec((tm, A), lambda i: (i, 0)), pl.BlockSpec((tm, B), lambda i: (i, 0)),
                  pl.BlockSpec((1, A), lambda i: (0, 0)), pl.BlockSpec((1, B), lambda i: (0, 0))],
        out_specs=[pl.BlockSpec((tm, A), lambda i: (i, 0)), pl.BlockSpec((tm, B), lambda i: (i, 0)),
                   pl.BlockSpec((1, A), lambda i: (0, 0)), pl.BlockSpec((1, B), lambda i: (0, 0))],
        out_shape=[SDS((T, A), F32), SDS((T, B), F32), SDS((1, A), F32), SDS((1, B), F32)],
        compiler_params=_params(1))(dh1b, w, a, b, ga, gb)


def _gmlp_bwd(proj, da, lg, lb, w_s, w_st, bs_t, A):
    T = proj.shape[0]
    G = A // GROUP_DIM
    tm = _tile(T, 512)
    nc = tm // CHUNK

    def body(u_ref, v_ref, da_ref, lg_ref, lb_ref, w_ref, wt_ref, bst_ref, duv_ref, dlg_ref, dlb_ref, dw_ref, dbs_ref):
        @pl.when(pl.program_id(0) == 0)
        def _():
            dlg_ref[...] = jnp.zeros_like(dlg_ref)
            dlb_ref[...] = jnp.zeros_like(dlb_ref)
            dw_ref[...] = jnp.zeros_like(dw_ref)
            dbs_ref[...] = jnp.zeros_like(dbs_ref)

        row = lax.broadcasted_iota(jnp.int32, (CHUNK, CHUNK), 0)
        col = lax.broadcasted_iota(jnp.int32, (CHUNK, CHUNK), 1)
        lower = row >= col
        upper = row <= col
        for g in range(G):
            sl = slice(g * GROUP_DIM, (g + 1) * GROUP_DIM)
            lgv = lg_ref[:, sl]
            vg, vg_grad = _gelu_and_grad(v_ref[:, sl])
            vhat, rstd, vn = _layer_norm_group(vg, lgv, lb_ref[:, sl])
            vnb = vn.astype(BF16)
            ug, ug_grad = _gelu_and_grad(u_ref[:, sl])
            dav = da_ref[:, sl]
            wm = jnp.where(lower, w_ref[g], 0.0).astype(BF16)
            wmt = jnp.where(upper, wt_ref[g], 0.0).astype(BF16)
            bcol = bst_ref[:, g:g + 1]
            dw_acc = jnp.zeros((CHUNK, CHUNK), F32)
            dbs_acc = jnp.zeros((CHUNK, 1), F32)
            dvn_parts = []
            dug_parts = []
            for c in range(nc):
                rs = slice(c * CHUNK, (c + 1) * CHUNK)
                mixed = _dot(wm, vnb[rs], NN) + bcol
                dug_parts.append(dav[rs] * mixed)
                dmix = dav[rs] * ug[rs]
                dbs_acc = dbs_acc + jnp.sum(dmix, axis=-1, keepdims=True)
                dmixb = dmix.astype(BF16)
                dw_acc = dw_acc + _dot(dmixb, vnb[rs], NT)
                dvn_parts.append(_dot(wmt, dmixb, NN))
            dvn = jnp.concatenate(dvn_parts, axis=0)
            dug = jnp.concatenate(dug_parts, axis=0)
            dw_ref[g] += jnp.where(lower, dw_acc, 0.0)
            dbs_ref[:, g:g + 1] += dbs_acc
            dlg_ref[:, sl] += jnp.sum(dvn * vhat, axis=0, keepdims=True)
            dlb_ref[:, sl] += jnp.sum(dvn, axis=0, keepdims=True)
            dvhat = dvn * lgv
            dvg = rstd * (dvhat - jnp.mean(dvhat, axis=-1, keepdims=True)
                          - vhat * jnp.mean(dvhat * vhat, axis=-1, keepdims=True))
            duv_ref[:, sl] = (dug * ug_grad).astype(BF16)
            duv_ref[:, A + g * GROUP_DIM:A + (g + 1) * GROUP_DIM] = (dvg * vg_grad).astype(BF16)

    return _CHAIN.call(
        body, name="gmlp_bwd", grid=(T // tm,),
        in_specs=[pl.BlockSpec((tm, A), lambda i: (i, 0)), pl.BlockSpec((tm, A), lambda i: (i, 1)),
                  pl.BlockSpec((tm, A), lambda i: (i, 0)),
                  pl.BlockSpec((1, A), lambda i: (0, 0)), pl.BlockSpec((1, A), lambda i: (0, 0)),
                  pl.BlockSpec((G, CHUNK, CHUNK), lambda i: (0, 0, 0)),
                  pl.BlockSpec((G, CHUNK, CHUNK), lambda i: (0, 0, 0)), pl.BlockSpec((CHUNK, G), lambda i: (0, 0))],
        out_specs=[pl.BlockSpec((tm, 2 * A), lambda i: (i, 0)),
                   pl.BlockSpec((1, A), lambda i: (0, 0)), pl.BlockSpec((1, A), lambda i: (0, 0)),
                   pl.BlockSpec((G, CHUNK, CHUNK), lambda i: (0, 0, 0)), pl.BlockSpec((CHUNK, G), lambda i: (0, 0))],
        out_shape=[SDS((T, 2 * A), BF16), SDS((1, A), F32), SDS((1, A), F32),
                   SDS((G, CHUNK, CHUNK), F32), SDS((CHUNK, G), F32)],
        compiler_params=_params(1))(proj, proj, da, lg, lb, w_s, w_st, bs_t)


def _attn_bwd(proj, do, duv, bias_t, sinks, A, B):
    T, P = proj.shape
    H = B // HEAD_DIM
    qpk = H // KV_HEADS
    tq = _tile(T, 512)
    nb = tq // CHUNK
    n_tiles = T // tq
    scale = HEAD_DIM ** -0.5
    rev = lambda i: n_tiles - 1 - i

    def body(sink_ref, q_ref, k_ref, v_ref, kp_ref, vp_ref, do_ref, duv_ref, bias_ref,
             dproj_ref, dbias_ref, dsink_ref, carry, dkv, sacc):
        step = pl.program_id(0)

        @pl.when(step == 0)
        def _():
            carry[...] = jnp.zeros_like(carry)
            sacc[...] = jnp.zeros_like(sacc)
            dbias_ref[...] = jnp.zeros_like(dbias_ref)

        jj = lax.broadcasted_iota(jnp.int32, (2 * CHUNK, CHUNK), 0)
        ii = lax.broadcasted_iota(jnp.int32, (2 * CHUNK, CHUNK), 1)
        in_window = (jj > ii) & (jj <= ii + CHUNK)
        first_mask = in_window & jnp.logical_or(step != n_tiles - 1, jj >= CHUNK)
        low_query = lax.broadcasted_iota(jnp.int32, (CHUNK, LANE), 1) < HEAD_DIM
        low_key = lax.broadcasted_iota(jnp.int32, (2 * CHUNK, LANE), 1) < HEAD_DIM

        def split_pair(pair_bf16):
            zero = jnp.zeros_like(pair_bf16)
            return jnp.concatenate([jnp.where(low_query, pair_bf16, zero), jnp.where(low_query, zero, pair_bf16)], axis=0)

        dproj_ref[:, :2 * A] = duv_ref[...]
        dkv[...] = jnp.zeros_like(dkv)
        for b in range(nb):
            rows = slice(b * CHUNK, (b + 1) * CHUNK)
            band = slice(b * CHUNK, (b + 2) * CHUNK)
            if b == 0:
                kprev, vprev, mask = kp_ref[...], vp_ref[...], first_mask
            else:
                prows = slice((b - 1) * CHUNK, b * CHUNK)
                kprev, vprev, mask = k_ref[prows, :], v_ref[prows, :], in_window
            kband = jnp.concatenate([kprev, k_ref[rows, :]], axis=0)
            vband = jnp.concatenate([vprev, v_ref[rows, :]], axis=0)
            k_pads = [_pad_heads(kband, g) for g in range(KV_HEADS)]
            v_pads = [_pad_heads(vband, g) for g in range(KV_HEADS)]
            queries, douts, scores, dprobs = [], [], [], []
            for pair in range(H // 2):
                cols = slice(2 * pair * HEAD_DIM, (2 * pair + 2) * HEAD_DIM)
                qs = (q_ref[rows, cols] * scale).astype(BF16)
                dob = do_ref[rows, cols].astype(BF16)
                queries.append(qs)
                douts.append(dob)
                scores += [_dot(kz, qs, NT) for kz in k_pads[2 * pair // qpk]]
                dprobs += [_dot(vz, dob, NT) for vz in v_pads[2 * pair // qpk]]
            probs, dscores = [], []
            for h in range(H):
                pt, p_sink = _softmax_with_sink(jnp.where(mask, scores[h] + bias_ref[h], NEG), sink_ref[h], 0)
                delta = jnp.sum(pt * dprobs[h], axis=0, keepdims=True)
                dst = pt * (dprobs[h] - delta)
                dbias_ref[h] += dst
                sacc[h:h + 1, :] += -(p_sink * delta)
                probs.append(pt.astype(BF16))
                dscores.append(dst.astype(BF16))
            dq_parts, dk_groups, dv_groups = [], [], []
            for g in range(KV_HEADS):
                k_both = jnp.concatenate(k_pads[g], axis=0)
                dk_acc = jnp.zeros((2 * CHUNK, LANE), F32)
                dv_acc = jnp.zeros((2 * CHUNK, LANE), F32)
                for pair in range(g * qpk // 2, (g + 1) * qpk // 2):
                    pair_heads = slice(2 * pair, 2 * pair + 2)
                    dk_acc = dk_acc + _dot(jnp.concatenate(dscores[pair_heads], axis=1), split_pair(queries[pair]), NN)
                    dv_acc = dv_acc + _dot(jnp.concatenate(probs[pair_heads], axis=1), split_pair(douts[pair]), NN)
                    dq_parts.append(_dot(jnp.concatenate(dscores[pair_heads], axis=0), k_both, TN) * scale)
                dk_groups.append(dk_acc + pltpu.roll(dk_acc, HEAD_DIM, 1))
                dv_groups.append(dv_acc + pltpu.roll(dv_acc, HEAD_DIM, 1))
            dkv[band, :LANE] += jnp.where(low_key, dk_groups[0], dk_groups[1])
            dkv[band, LANE:] += jnp.where(low_key, dv_groups[0], dv_groups[1])
            dproj_ref[rows, 2 * A:2 * A + B] = jnp.concatenate(dq_parts, axis=1).astype(BF16)
        last = slice(tq, tq + CHUNK)
        dkv[last, :] += carry[...]
        dproj_ref[:, 2 * A + B:] = dkv[CHUNK:, :].astype(BF16)
        carry[...] = dkv[:CHUNK, :]

        @pl.when(step == n_tiles - 1)
        def _():
            dsink_ref[...] = jnp.sum(sacc[...], axis=1, keepdims=True)

    specs = _attn_specs(tq, A, B, reverse_tiles=n_tiles)
    return _CHAIN.call(
        body, name="attn_bwd", grid=(n_tiles,),
        in_specs=[pl.BlockSpec(memory_space=pltpu.SMEM)] + specs
        + [pl.BlockSpec((tq, B), lambda i: (rev(i), 0)), pl.BlockSpec((tq, 2 * A), lambda i: (rev(i), 0)),
           pl.BlockSpec((H, 2 * CHUNK, CHUNK), lambda i: (0, 0, 0))],
        out_specs=[pl.BlockSpec((tq, P), lambda i: (rev(i), 0)),
                   pl.BlockSpec((H, 2 * CHUNK, CHUNK), lambda i: (0, 0, 0)), pl.BlockSpec((H, 1), lambda i: (0, 0))],
        out_shape=[SDS((T, P), BF16), SDS((H, 2 * CHUNK, CHUNK), F32), SDS((H, 1), F32)],
        scratch_shapes=[pltpu.VMEM((CHUNK, 2 * LANE), F32), pltpu.VMEM((tq + CHUNK, 2 * LANE), F32),
                        pltpu.VMEM((H, LANE), F32)],
        compiler_params=_params(1))(sinks, proj, proj, proj, proj, proj, do, duv, bias_t)


def _bias_bwd(dbias, onehot):
    H = dbias.shape[0]
    nbk = onehot.shape[1]

    def body(d_ref, oh_ref, o_ref):
        hi, mid, lo = _split3(d_ref[...])
        oh = oh_ref[...]
        o_ref[...] = _dot(hi, oh, NN) + _dot(mid, oh, NN) + _dot(lo, oh, NN)

    return _CHAIN.call(body, name="bias_bwd", in_specs=[VMEM_SPEC] * 2, out_specs=VMEM_SPEC, out_shape=SDS((H, nbk), F32),
                       compiler_params=_params(0))(dbias, onehot)


def _inproj_bwd(dproj, w_t, x, dh1, g):
    T, P = dproj.shape
    D = x.shape[1]
    tm = _tile(T, 512)

    def body(dp_ref, w_ref, x_ref, dh_ref, g_ref, dx_ref, dg_ref):
        @pl.when(pl.program_id(0) == 0)
        def _():
            dg_ref[...] = jnp.zeros_like(dg_ref)

        dn = _dot(dp_ref[...], w_ref[...], NN)
        xv = x_ref[...]
        r = _rms_stats(xv)
        dg_ref[...] += jnp.sum(dn * (xv * r), axis=0, keepdims=True)
        dx_ref[...] = dh_ref[...] + _rms_bwd(dn, xv, r, g_ref[...])

    return _CHAIN.call(
        body, name="inproj_bwd", grid=(T // tm,),
        in_specs=[pl.BlockSpec((tm, P), lambda i: (i, 0)), _resident((P, D)),
                  pl.BlockSpec((tm, D), lambda i: (i, 0)), pl.BlockSpec((tm, D), lambda i: (i, 0)),
                  pl.BlockSpec((1, D), lambda i: (0, 0))],
        out_specs=[pl.BlockSpec((tm, D), lambda i: (i, 0)), pl.BlockSpec((1, D), lambda i: (0, 0))],
        out_shape=[SDS((T, D), F32), SDS((1, D), F32)], compiler_params=_params(1))(dproj, w_t, x, dh1, g)


def _adamw(w, g, m, v):
    m = ADAM_B1 * m + (1.0 - ADAM_B1) * g
    v = ADAM_B2 * v + (1.0 - ADAM_B2) * (g * g)
    m_hat = m / (1.0 - ADAM_B1 ** ADAM_STEP)
    v_hat = v / (1.0 - ADAM_B2 ** ADAM_STEP)
    delta = -ADAM_LR * (m_hat / (jnp.sqrt(v_hat) + ADAM_EPS) + ADAM_WD * w)
    return delta, m, v


def _adam_sharded(csum, recv, w, m, v, name):
    R, C = w.shape
    tr = _tile(R, 256, 16)

    def body(own_ref, recv_ref, w_ref, m_ref, v_ref, g_ref, d_ref, nm_ref, nv_ref):
        g = own_ref[...].astype(F32)
        for r in range(3):
            g = g + recv_ref[r].astype(F32)
        delta, nm, nv = _adamw(w_ref[...], g, m_ref[...], v_ref[...])
        g_ref[...] = g
        d_ref[...] = delta
        nm_ref[...] = nm
        nv_ref[...] = nv

    blk = pl.BlockSpec((tr, C), lambda i: (i, 0))
    return _CHAIN.call(
        body, name=name, grid=(R // tr,),
        in_specs=[pl.BlockSpec((None, tr, C), lambda i: (0, i, 0)), pl.BlockSpec((3, tr, C), lambda i: (0, i, 0)),
                  blk, blk, blk],
        out_specs=[blk] * 4, out_shape=[SDS((R, C), F32)] * 4, compiler_params=_params(1))(csum, recv, w, m, v)


def _rows2d(shape):
    return (int(np.prod(shape[:-1])) if len(shape) > 1 else 1, shape[-1])


def _small_layout(shapes):
    totals, places = {}, []
    for s in shapes:
        r, w = _rows2d(s)
        off = totals.get(w, 0)
        places.append((w, off, r))
        totals[w] = off + -(-r // 8) * 8
    return {w: -(-t // 32) * 32 for w, t in totals.items()}, places


def _pack_small(arrays, totals, places):
    bufs = []
    for w, total in totals.items():
        buf = jnp.zeros((total, w), F32)
        for a, (pw, off, r) in zip(arrays, places):
            if pw == w:
                buf = lax.dynamic_update_slice(buf, a.reshape(r, w).astype(F32), (off, 0))
        bufs.append(buf)
    return bufs


def _adam_small(gathered, totals, places, ws, ms, vs):
    widths = list(totals)
    n, nw = len(places), len(widths)

    def body(*refs):
        gath, params, outs = refs[:nw], refs[nw:nw + 3 * n], refs[nw + 3 * n:]
        for p, (w, off, r) in enumerate(places):
            g_ref = gath[widths.index(w)]
            g = g_ref[0, off:off + r, :]
            for d in range(1, N_DEV):
                g = g + g_ref[d, off:off + r, :]
            delta, nm, nv = _adamw(params[p][...], g, params[n + p][...], params[2 * n + p][...])
            for k, val in enumerate((g, delta, nm, nv)):
                outs[4 * p + k][...] = val

    shapes2d = [SDS((r, w), F32) for w, _, r in places for _ in range(4)]
    outs = _CHAIN.call(body, name="adam_small", in_specs=[VMEM_SPEC] * (nw + 3 * n), out_specs=[VMEM_SPEC] * (4 * n),
                       out_shape=shapes2d, compiler_params=_params(0))(*gathered, *ws, *ms, *vs)
    return [outs[4 * p:4 * p + 4] for p in range(n)]


def kernel(x, rel_bias_table, mix_norm_g, w_in, gate_norm_g, gate_norm_b, w_spatial, b_spatial, attn_sinks, out_norm_a_g, out_norm_b_g, w_out, ffn_norm_g, w_up, w_down, final_norm_g, loss_target, m_rel_bias_table, m_mix_norm_g, m_w_in, m_gate_norm_g, m_gate_norm_b, m_w_spatial, m_b_spatial, m_attn_sinks, m_out_norm_a_g, m_out_norm_b_g, m_w_out, m_ffn_norm_g, m_w_up, m_w_down, m_final_norm_g, v_rel_bias_table, v_mix_norm_g, v_w_in, v_gate_norm_g, v_gate_norm_b, v_w_spatial, v_b_spatial, v_attn_sinks, v_out_norm_a_g, v_out_norm_b_g, v_w_out, v_ffn_norm_g, v_w_up, v_w_down, v_final_norm_g):
    T, D = x.shape[1], x.shape[2]
    A = D // 2
    B = D // 2
    H = B // HEAD_DIM
    P = 2 * A + B + 2 * KV_HEADS * HEAD_DIM
    xs = x.reshape(T, D)
    target = loss_target.reshape(T, D)

    win_t, m_win_t, v_win_t = (jnp.swapaxes(a[0], 0, 1) for a in (w_in, m_w_in, v_w_in))
    shards = [win_t.astype(BF16), w_out[0].astype(BF16), w_up[0].astype(BF16), w_down[0].astype(BF16)]
    _CHAIN.token = None
    _handshakes.clear()
    gather = _gather_begin(shards)
    _gather_step(gather, [(0, 0)], "gather_start")

    g1, g2, g3 = mix_norm_g.reshape(1, D), ffn_norm_g.reshape(1, D), final_norm_g.reshape(1, D)
    lg, lb = gate_norm_g.reshape(1, A), gate_norm_b.reshape(1, A)
    ws = w_spatial[0]
    ws_t = jnp.swapaxes(ws, 1, 2)
    bs_t = jnp.transpose(b_spatial[0])
    ga, gb = out_norm_a_g.reshape(1, A), out_norm_b_g.reshape(1, B)
    sinks = attn_sinks.reshape(H)
    bucket, in_window = _t5_bucket()
    onehot_np = ((bucket[:, :, None] == np.arange(N_BUCKETS)) & in_window[:, :, None]).astype(np.float32)
    onehot = jnp.asarray(onehot_np.reshape(-1, N_BUCKETS)).astype(BF16)
    onehot_kq = jnp.asarray(onehot_np.transpose(1, 0, 2).reshape(-1, N_BUCKETS)).astype(BF16)

    bias, bias_t = _bias_fwd(jnp.transpose(rel_bias_table), jnp.transpose(onehot), jnp.transpose(onehot_kq))
    bias, bias_t = bias.reshape(H, CHUNK, 2 * CHUNK), bias_t.reshape(H, 2 * CHUNK, CHUNK)
    n1 = _mix_norm(xs, g1)
    _gather_step(gather, [(0, 1), (1, 0), (2, 0)], "gather_in_1")
    _gather_step(gather, [(0, 2)], "gather_in_2")
    (win_g,) = _gather_end(gather, [0], "gather_in_end")
    win_t_full = win_g.reshape(P, D)
    proj = _inproj_fwd(n1, win_t_full)
    _gather_step(gather, [(1, 1)], "gather_out_1")
    a_out = _gmlp_fwd(proj, lg, lb, ws, bs_t, A)
    _gather_step(gather, [(1, 2), (2, 1), (3, 0)], "gather_out_2_up_1")
    b_out = _attn_fwd(proj, bias, sinks, A, B)
    (wout_g,) = _gather_end(gather, [1], "gather_out_end")
    _gather_step(gather, [(2, 2)], "gather_up_2")
    wout_full = wout_g.reshape(A + B, D)
    h1, mixed, n2 = _outproj_fwd(a_out, b_out, ga, gb, xs, wout_full, g2)
    (wup_g,) = _gather_end(gather, [2], "gather_up_end")
    _gather_step(gather, [(3, 1)], "gather_down_1")
    wup_t = jnp.transpose(wup_g, (0, 2, 1)).reshape(-1, D)
    z = _ffn_up(n2, wup_g)
    _gather_step(gather, [(3, 2)], "gather_down_2")
    (wdown_g,) = _gather_end(gather, [3], "gather_down_end")
    h2 = _ffn_down(h1, z, wdown_g.reshape(-1, D))
    loss_part, dg3, dh2, dh2b = _final_loss(h2, g3, target)

    def reduce_to_chip(state, name):
        csums = [_chip_sum(part, received, "%s_chip_sum_%d" % (name, a))
                 for a, (part, received) in enumerate(_sibling_exchange_end(state, name + "_sib_end"))]
        return _chip_exchange_begin(csums, name + "_chip")

    dwdown = _matmul_tn(z, dh2b, "grad_w_down", square_a=True).reshape(wdown_g.shape)
    dzp = _ffn_down_bwd(dh2b, z, wdown_g.reshape(-1, D))
    dwup = _matmul_tn(n2, dzp, "grad_w_up", col_blocks=N_DEV)
    sib_ffn = _sibling_exchange_begin([dwdown, dwup], "rs_ffn_sib")
    dh1, dh1b, dg2 = _ffn_norm_bwd(_ffn_up_bwd(dzp, wup_t), dh2, h1, g2)
    chip_ffn = reduce_to_chip(sib_ffn, "rs_ffn")
    da, db, dga, dgb = _outproj_bwd(dh1b, wout_full, a_out, b_out, ga, gb)
    dwout = _matmul_tn(mixed, dh1b, "grad_w_out").reshape(wout_g.shape)
    sib_out = _sibling_exchange_begin([dwout], "rs_out_sib")
    duv, dlg, dlb, dws, dbs_t = _gmlp_bwd(proj, da, lg, lb, ws, ws_t, bs_t, A)
    dproj, dbias_t, dsinks = _attn_bwd(proj, db, duv, bias_t, sinks, A, B)
    chip_out = reduce_to_chip(sib_out, "rs_out")
    dwin_t = _matmul_tn(dproj, n1, "grad_w_in").reshape(win_g.shape)
    sib_in = _sibling_exchange_begin([dwin_t], "rs_in_sib")
    dtable_t = _bias_bwd(dbias_t.reshape(H, -1), onehot_kq)
    chip_in = reduce_to_chip(sib_in, "rs_in")
    grad_x, dg1 = _inproj_bwd(dproj, win_t_full, xs, dh1, g1)

    small_w = [rel_bias_table, mix_norm_g, gate_norm_g, gate_norm_b, w_spatial, b_spatial, attn_sinks,
               out_norm_a_g, out_norm_b_g, ffn_norm_g, final_norm_g]
    small_m = [m_rel_bias_table, m_mix_norm_g, m_gate_norm_g, m_gate_norm_b, m_w_spatial, m_b_spatial, m_attn_sinks,
               m_out_norm_a_g, m_out_norm_b_g, m_ffn_norm_g, m_final_norm_g]
    small_v = [v_rel_bias_table, v_mix_norm_g, v_gate_norm_g, v_gate_norm_b, v_w_spatial, v_b_spatial, v_attn_sinks,
               v_out_norm_a_g, v_out_norm_b_g, v_ffn_norm_g, v_final_norm_g]
    small_g = [jnp.transpose(dtable_t), dg1, dlg, dlb, dws, jnp.transpose(dbs_t), dsinks, dga, dgb, dg2, dg3]
    nothing = jnp.zeros((1, H), F32)
    small_w, small_m, small_v = small_w + [nothing], small_m + [nothing], small_v + [nothing]
    small_g = small_g + [jnp.broadcast_to(loss_part, (1, H))]
    shapes = [w.shape for w in small_w]
    totals, places = _small_layout(shapes)
    as_rows = lambda arrays: [a.reshape(_rows2d(a.shape)) for a in arrays]
    big = [None] * 4

    def adam_of(k, sums, w, m, v):
        big[k] = _adam_sharded(*sums, w, m, v, "adam_%d" % k)

    small_gather = _gather_begin(_pack_small(small_g, totals, places))
    every = range(len(totals))
    _gather_step(small_gather, [(a, 0) for a in every], "small_gather_start")
    sums_down, sums_up, sums_out, sums_in = _chip_exchanges_end([chip_ffn, chip_out, chip_in], "rs_end")
    adam_of(3, sums_down, w_down[0], m_w_down[0], v_w_down[0])
    adam_of(2, sums_up, w_up[0], m_w_up[0], v_w_up[0])
    _gather_step(small_gather, [(a, 1) for a in every], "small_gather_1")
    adam_of(1, sums_out, w_out[0], m_w_out[0], v_w_out[0])
    _gather_step(small_gather, [(a, 2) for a in every], "small_gather_2")
    adam_of(0, sums_in, win_t, m_win_t, v_win_t)
    gathered = _gather_end(small_gather, list(every), "small_gather_end")
    small_out = _adam_small(gathered, totals, places, as_rows(small_w), as_rows(small_m), as_rows(small_v))
    sg, sd, sm, sv = [[outs[k].reshape(s) for outs, s in zip(small_out, shapes)] for k in range(4)]
    big[0] = [jnp.swapaxes(o, 0, 1) for o in big[0]]
    big = [[o.reshape(w.shape) for o in outs] for outs, w in zip(big, (w_in, w_out, w_up, w_down))]

    loss = sg[-1][0, 0]

    order = ["s0", "s1", "b0", "s2", "s3", "s4", "s5", "s6", "s7", "s8", "b1", "s9", "b2", "b3", "s10"]

    def group(idx):
        small = (sg, sd, sm, sv)[idx]
        return [small[int(t[1:])] if t[0] == "s" else big[int(t[1:])][idx] for t in order]

    return (loss, grad_x.reshape(x.shape), *group(0), *group(1), *group(2), *group(3))
```
